```python
import jax, jax.numpy as jnp
from jax import lax
import numpy as np

D_MODEL = 1024
BATCH = 8
SEQ = 8192
DEPTH = 2

N_AB_LAYERS = (DEPTH + 1) // 2
N_C_LAYERS = DEPTH // 2

DN_HEADS = 4
DN_HEAD_DIM = 128
DN_KEY = DN_HEADS * DN_HEAD_DIM
DN_VAL = DN_HEADS * DN_HEAD_DIM
DN_CONV = 4
DN_CHUNK = 64
SG_GROUPS = 4
SG_GROUP_DIM = 128
SG_WIDTH = SG_GROUPS * SG_GROUP_DIM
SG_CHUNK = 128
POOL_WINDOWS = (2, 4, 8, 16)
POOL_GROUP_DIM = D_MODEL // len(POOL_WINDOWS)
D_FF = 2816
NORM_EPS = 1e-6

MIX_WIDTH = DN_VAL + SG_WIDTH
PROJ_SIZES = (DN_KEY, DN_KEY, DN_VAL, DN_VAL, DN_HEADS, DN_HEADS, SG_WIDTH, SG_WIDTH)
IN_PROJ = int(sum(PROJ_SIZES))
SPLIT_POINTS = tuple(int(s) for s in np.cumsum(PROJ_SIZES)[:-1])
QKV_WIDTH = 2 * DN_KEY + DN_VAL

kernel_name = "hybrid_deltanet_sgmlp_pool_macaron"


def _rmsnorm(x, w):
    xf = x.astype(jnp.float32)
    y = xf * lax.rsqrt(jnp.mean(xf * xf, axis=-1, keepdims=True) + NORM_EPS)
    return (y * w.astype(jnp.float32)).astype(x.dtype)


def _l2norm(x):
    return x * lax.rsqrt(jnp.sum(x * x, axis=-1, keepdims=True) + NORM_EPS)


def _swiglu(h, w_in, w_out):
    gate, up = jnp.split(h @ w_in, 2, axis=-1)
    return (jax.nn.silu(gate) * up) @ w_out


def _causal_dwconv(x, w):
    c = x.shape[-1]
    return lax.conv_general_dilated(
        x, w[:, None, :].astype(x.dtype), window_strides=(1,), padding=[(w.shape[0] - 1, 0)],
        dimension_numbers=("NWC", "WIO", "NWC"), feature_group_count=c)


def _gated_delta_rule(q, k, v, g, beta):
    bsz, t, h, dk = q.shape
    dv = v.shape[-1]
    n, c = t // DN_CHUNK, DN_CHUNK

    def chunk(a):
        return jnp.moveaxis(a.reshape((bsz, n, c, h) + a.shape[3:]), 3, 1)

    q = chunk(_l2norm(q) * dk ** -0.5)
    k = chunk(_l2norm(k))
    v = chunk(v)
    beta = chunk(beta)
    g = jnp.cumsum(chunk(g), axis=-1)
    causal = jnp.tril(jnp.ones((c, c), dtype=bool))
    strict = jnp.tril(jnp.ones((c, c), dtype=bool), k=-1)
    diff = g[..., :, None] - g[..., None, :]
    decay = jnp.where(causal, jnp.exp(jnp.where(causal, diff, 0.0)), 0.0)
    k_beta = k * beta[..., None]
    lower = jnp.where(strict, jnp.einsum('bhncd,bhnsd->bhncs', k_beta, k) * decay, 0.0)
    eye = jnp.eye(c, dtype=q.dtype)
    rhs = jnp.concatenate([v * beta[..., None], k_beta * jnp.exp(g)[..., None]], axis=-1)
    sol = lax.linalg.triangular_solve(eye + lower, rhs, left_side=True, lower=True, unit_diagonal=True)
    u, w = sol[..., :dv], sol[..., dv:]
    attn = jnp.einsum('bhncd,bhnsd->bhncs', q, k) * decay
    q_dec = q * jnp.exp(g)[..., None]
    g_last = g[..., -1]
    k_tail = k * jnp.exp(g_last[..., None] - g)[..., None]

    def step(state, inp):
        q_i, w_i, u_i, a_i, k_i, gl_i = inp
        v_new = u_i - jnp.einsum('bhck,bhkv->bhcv', w_i, state)
        o_i = jnp.einsum('bhck,bhkv->bhcv', q_i, state) + jnp.einsum('bhcs,bhsv->bhcv', a_i, v_new)
        state = state * jnp.exp(gl_i)[..., None, None] + jnp.einsum('bhck,bhcv->bhkv', k_i, v_new)
        return state, o_i

    xs = tuple(jnp.moveaxis(a, 2, 0) for a in (q_dec, w, u, attn, k_tail, g_last))
    state0 = jnp.zeros((bsz, h, dk, dv), q.dtype)
    _, o = lax.scan(step, state0, xs)
    return jnp.transpose(o, (1, 0, 3, 2, 4)).reshape(bsz, t, h, dv)


def _hybrid_ab_mixer(h, w_in, conv_w, a_log, dt_bias, dn_norm, sg_norm, sg_w, sg_b, w_out):
    bsz, t, _ = h.shape
    q, k, v, z, b, a, su, sv = jnp.split(h @ w_in, SPLIT_POINTS, axis=-1)
    qkv = jax.nn.silu(_causal_dwconv(jnp.concatenate([q, k, v], axis=-1), conv_w)).astype(jnp.float32)
    q, k, v = jnp.split(qkv, (DN_KEY, 2 * DN_KEY), axis=-1)
    q = q.reshape(bsz, t, DN_HEADS, DN_HEAD_DIM)
    k = k.reshape(bsz, t, DN_HEADS, DN_HEAD_DIM)
    v = v.reshape(bsz, t, DN_HEADS, DN_HEAD_DIM)
    beta = jax.nn.sigmoid(b.astype(jnp.float32))
    g = -jnp.exp(a_log.astype(jnp.float32)) * jax.nn.softplus(a.astype(jnp.float32) + dt_bias.astype(jnp.float32))
    o = _gated_delta_rule(q, k, v, g, beta)
    zf = z.astype(jnp.float32).reshape(bsz, t, DN_HEADS, DN_HEAD_DIM)
    o = _rmsnorm(o, dn_norm) * jax.nn.silu(zf)
    o_a = o.reshape(bsz, t, DN_VAL).astype(h.dtype)
    su = jax.nn.gelu(su, approximate=False).reshape(bsz, t, SG_GROUPS, SG_GROUP_DIM)
    sv = _rmsnorm(jax.nn.gelu(sv, approximate=False).reshape(bsz, t, SG_GROUPS, SG_GROUP_DIM), sg_norm)
    sv = sv.reshape(bsz, t // SG_CHUNK, SG_CHUNK, SG_GROUPS, SG_GROUP_DIM)
    tri = jnp.tril(jnp.ones((SG_CHUNK, SG_CHUNK), dtype=bool))
    w_s = jnp.where(tri, sg_w, 0.0).astype(sv.dtype)
    mixed = jnp.einsum('gts,bnsgc->bntgc', w_s, sv) + jnp.transpose(sg_b)[:, :, None].astype(sv.dtype)
    o_b = (su * mixed.reshape(bsz, t, SG_GROUPS, SG_GROUP_DIM)).reshape(bsz, t, SG_WIDTH)
    return jnp.concatenate([o_a, o_b], axis=-1) @ w_out


def _pool_mixer(h, pool_w, pool_scale):
    t = h.shape[1]
    hf = h.astype(jnp.float32)
    csum = jnp.cumsum(hf, axis=1)
    pos = jnp.arange(1, t + 1)
    outs = []
    for gi, win in enumerate(POOL_WINDOWS):
        sl = slice(gi * POOL_GROUP_DIM, (gi + 1) * POOL_GROUP_DIM)
        cs = csum[..., sl]
        lag = jnp.pad(cs[:, :-win], ((0, 0), (win, 0), (0, 0)))
        count = jnp.minimum(pos, win).astype(jnp.float32)[None, :, None]
        pooled = ((cs - lag) / count - hf[..., sl]).astype(h.dtype)
        outs.append(pooled @ pool_w[gi])
    return jnp.concatenate(outs, axis=-1) * pool_scale


def _fwd_setup_inputs(seed: int = 0) -> dict:
    key = jax.random.key(seed)
    ks = jax.random.split(key, 24)
    f32 = jnp.float32
    nrm = lambda k, s, sc: jax.random.normal(k, s, f32) * sc
    gain = lambda k, s: 1.0 + 0.02 * jax.random.normal(k, s, f32)
    x = jax.random.normal(ks[0], (BATCH, SEQ, D_MODEL), f32)
    ffn_norm1 = gain(ks[1], (DEPTH, D_MODEL))
    ffn1_w_in = nrm(ks[2], (DEPTH, D_MODEL, 2 * D_FF), D_MODEL ** -0.5)
    ffn1_w_out = nrm(ks[3], (DEPTH, D_FF, D_MODEL), D_FF ** -0.5)
    mix_norm = gain(ks[4], (DEPTH, D_MODEL))
    ffn_norm2 = gain(ks[5], (DEPTH, D_MODEL))
    ffn2_w_in = nrm(ks[6], (DEPTH, D_MODEL, 2 * D_FF), D_MODEL ** -0.5)
    ffn2_w_out = nrm(ks[7], (DEPTH, D_FF, D_MODEL), D_FF ** -0.5)
    ab_w_in = nrm(ks[8], (N_AB_LAYERS, D_MODEL, IN_PROJ), D_MODEL ** -0.5)
    dn_conv_w = nrm(ks[9], (N_AB_LAYERS, DN_CONV, QKV_WIDTH), DN_CONV ** -0.5)
    dn_a_log = jnp.log(jax.random.uniform(ks[10], (N_AB_LAYERS, DN_HEADS), f32, 1.0, 16.0))
    dt = jnp.exp(jax.random.uniform(ks[11], (N_AB_LAYERS, DN_HEADS), f32, np.log(1e-3), np.log(1e-1)))
    dn_dt_bias = dt + jnp.log(-jnp.expm1(-dt))
    dn_out_norm = gain(ks[12], (N_AB_LAYERS, DN_HEAD_DIM))
    sg_norm = gain(ks[13], (N_AB_LAYERS, SG_GROUPS, SG_GROUP_DIM))
    sg_w = nrm(ks[14], (N_AB_LAYERS, SG_GROUPS, SG_CHUNK, SG_CHUNK), SG_CHUNK ** -0.5)
    sg_b = 1.0 + 0.1 * jax.random.normal(ks[15], (N_AB_LAYERS, SG_GROUPS, SG_CHUNK), f32)
    ab_w_out = nrm(ks[16], (N_AB_LAYERS, MIX_WIDTH, D_MODEL), MIX_WIDTH ** -0.5)
    pool_w = nrm(ks[17], (N_C_LAYERS, len(POOL_WINDOWS), POOL_GROUP_DIM, POOL_GROUP_DIM), POOL_GROUP_DIM ** -0.5)
    pool_scale = gain(ks[18], (N_C_LAYERS, D_MODEL))
    final_norm = gain(ks[19], (D_MODEL,))
    return {"x": x, "ffn_norm1": ffn_norm1, "ffn1_w_in": ffn1_w_in, "ffn1_w_out": ffn1_w_out,
            "mix_norm": mix_norm, "ffn_norm2": ffn_norm2, "ffn2_w_in": ffn2_w_in, "ffn2_w_out": ffn2_w_out,
            "ab_w_in": ab_w_in, "dn_conv_w": dn_conv_w, "dn_a_log": dn_a_log, "dn_dt_bias": dn_dt_bias,
            "dn_out_norm": dn_out_norm, "sg_norm": sg_norm, "sg_w": sg_w, "sg_b": sg_b, "ab_w_out": ab_w_out,
            "pool_w": pool_w, "pool_scale": pool_scale, "final_norm": final_norm}


def _fwd_reference(x, ffn_norm1, ffn1_w_in, ffn1_w_out, mix_norm, ffn_norm2, ffn2_w_in, ffn2_w_out,
              ab_w_in, dn_conv_w, dn_a_log, dn_dt_bias, dn_out_norm, sg_norm, sg_w, sg_b, ab_w_out,
              pool_w, pool_scale, final_norm):
    for l in range(DEPTH):
        x = x + 0.5 * _swiglu(_rmsnorm(x, ffn_norm1[l]), ffn1_w_in[l], ffn1_w_out[l])
        h = _rmsnorm(x, mix_norm[l])
        i = l // 2
        if l % 2 == 0:
            x = x + _hybrid_ab_mixer(h, ab_w_in[i], dn_conv_w[i], dn_a_log[i], dn_dt_bias[i], dn_out_norm[i],
                                     sg_norm[i], sg_w[i], sg_b[i], ab_w_out[i])
        else:
            x = x + _pool_mixer(h, pool_w[i], pool_scale[i])
        x = x + 0.5 * _swiglu(_rmsnorm(x, ffn_norm2[l]), ffn2_w_in[l], ffn2_w_out[l])
    return _rmsnorm(x, final_norm)


import jax as _jax
import jax.numpy as _jnp

TWIN_FORMAT = 'train_step'
FWD_PARAMS = ['x', 'ffn_norm1', 'ffn1_w_in', 'ffn1_w_out', 'mix_norm', 'ffn_norm2', 'ffn2_w_in', 'ffn2_w_out', 'ab_w_in', 'dn_conv_w', 'dn_a_log', 'dn_dt_bias', 'dn_out_norm', 'sg_norm', 'sg_w', 'sg_b', 'ab_w_out', 'pool_w', 'pool_scale', 'final_norm']
TWIN_WEIGHTS = ['ffn_norm1', 'ffn1_w_in', 'ffn1_w_out', 'mix_norm', 'ffn_norm2', 'ffn2_w_in', 'ffn2_w_out', 'ab_w_in', 'dn_conv_w', 'dn_a_log', 'dn_dt_bias', 'dn_out_norm', 'sg_norm', 'sg_w', 'sg_b', 'ab_w_out', 'pool_w', 'pool_scale', 'final_norm']
TWIN_DIFF_INPUT = 'x'
TWIN_INPUTS = ['x', 'ffn_norm1', 'ffn1_w_in', 'ffn1_w_out', 'mix_norm', 'ffn_norm2', 'ffn2_w_in', 'ffn2_w_out', 'ab_w_in', 'dn_conv_w', 'dn_a_log', 'dn_dt_bias', 'dn_out_norm', 'sg_norm', 'sg_w', 'sg_b', 'ab_w_out', 'pool_w', 'pool_scale', 'final_norm', 'loss_target', 'm_ffn_norm1', 'm_ffn1_w_in', 'm_ffn1_w_out', 'm_mix_norm', 'm_ffn_norm2', 'm_ffn2_w_in', 'm_ffn2_w_out', 'm_ab_w_in', 'm_dn_conv_w', 'm_dn_a_log', 'm_dn_dt_bias', 'm_dn_out_norm', 'm_sg_norm', 'm_sg_w', 'm_sg_b', 'm_ab_w_out', 'm_pool_w', 'm_pool_scale', 'm_final_norm', 'v_ffn_norm1', 'v_ffn1_w_in', 'v_ffn1_w_out', 'v_mix_norm', 'v_ffn_norm2', 'v_ffn2_w_in', 'v_ffn2_w_out', 'v_ab_w_in', 'v_dn_conv_w', 'v_dn_a_log', 'v_dn_dt_bias', 'v_dn_out_norm', 'v_sg_norm', 'v_sg_w', 'v_sg_b', 'v_ab_w_out', 'v_pool_w', 'v_pool_scale', 'v_final_norm']
TWIN_OUTPUTS = ['loss', 'grad_x', 'grad_ffn_norm1', 'grad_ffn1_w_in', 'grad_ffn1_w_out', 'grad_mix_norm', 'grad_ffn_norm2', 'grad_ffn2_w_in', 'grad_ffn2_w_out', 'grad_ab_w_in', 'grad_dn_conv_w', 'grad_dn_a_log', 'grad_dn_dt_bias', 'grad_dn_out_norm', 'grad_sg_norm', 'grad_sg_w', 'grad_sg_b', 'grad_ab_w_out', 'grad_pool_w', 'grad_pool_scale', 'grad_final_norm', 'delta_ffn_norm1', 'delta_ffn1_w_in', 'delta_ffn1_w_out', 'delta_mix_norm', 'delta_ffn_norm2', 'delta_ffn2_w_in', 'delta_ffn2_w_out', 'delta_ab_w_in', 'delta_dn_conv_w', 'delta_dn_a_log', 'delta_dn_dt_bias', 'delta_dn_out_norm', 'delta_sg_norm', 'delta_sg_w', 'delta_sg_b', 'delta_ab_w_out', 'delta_pool_w', 'delta_pool_scale', 'delta_final_norm', 'new_m_ffn_norm1', 'new_m_ffn1_w_in', 'new_m_ffn1_w_out', 'new_m_mix_norm', 'new_m_ffn_norm2', 'new_m_ffn2_w_in', 'new_m_ffn2_w_out', 'new_m_ab_w_in', 'new_m_dn_conv_w', 'new_m_dn_a_log', 'new_m_dn_dt_bias', 'new_m_dn_out_norm', 'new_m_sg_norm', 'new_m_sg_w', 'new_m_sg_b', 'new_m_ab_w_out', 'new_m_pool_w', 'new_m_pool_scale', 'new_m_final_norm', 'new_v_ffn_norm1', 'new_v_ffn1_w_in', 'new_v_ffn1_w_out', 'new_v_mix_norm', 'new_v_ffn_norm2', 'new_v_ffn2_w_in', 'new_v_ffn2_w_out', 'new_v_ab_w_in', 'new_v_dn_conv_w', 'new_v_dn_a_log', 'new_v_dn_dt_bias', 'new_v_dn_out_norm', 'new_v_sg_norm', 'new_v_sg_w', 'new_v_sg_b', 'new_v_ab_w_out', 'new_v_pool_w', 'new_v_pool_scale', 'new_v_final_norm']
TWIN_LEAF_KINDS = {'loss': 'loss', 'grad_x': 'grad_x', 'grad_ffn_norm1': 'grad_w', 'grad_ffn1_w_in': 'grad_w', 'grad_ffn1_w_out': 'grad_w', 'grad_mix_norm': 'grad_w', 'grad_ffn_norm2': 'grad_w', 'grad_ffn2_w_in': 'grad_w', 'grad_ffn2_w_out': 'grad_w', 'grad_ab_w_in': 'grad_w', 'grad_dn_conv_w': 'grad_w', 'grad_dn_a_log': 'grad_w', 'grad_dn_dt_bias': 'grad_w', 'grad_dn_out_norm': 'grad_w', 'grad_sg_norm': 'grad_w', 'grad_sg_w': 'grad_w', 'grad_sg_b': 'grad_w', 'grad_ab_w_out': 'grad_w', 'grad_pool_w': 'grad_w', 'grad_pool_scale': 'grad_w', 'grad_final_norm': 'grad_w', 'delta_ffn_norm1': 'delta_w', 'delta_ffn1_w_in': 'delta_w', 'delta_ffn1_w_out': 'delta_w', 'delta_mix_norm': 'delta_w', 'delta_ffn_norm2': 'delta_w', 'delta_ffn2_w_in': 'delta_w', 'delta_ffn2_w_out': 'delta_w', 'delta_ab_w_in': 'delta_w', 'delta_dn_conv_w': 'delta_w', 'delta_dn_a_log': 'delta_w', 'delta_dn_dt_bias': 'delta_w', 'delta_dn_out_norm': 'delta_w', 'delta_sg_norm': 'delta_w', 'delta_sg_w': 'delta_w', 'delta_sg_b': 'delta_w', 'delta_ab_w_out': 'delta_w', 'delta_pool_w': 'delta_w', 'delta_pool_scale': 'delta_w', 'delta_final_norm': 'delta_w', 'new_m_ffn_norm1': 'new_m', 'new_m_ffn1_w_in': 'new_m', 'new_m_ffn1_w_out': 'new_m', 'new_m_mix_norm': 'new_m', 'new_m_ffn_norm2': 'new_m', 'new_m_ffn2_w_in': 'new_m', 'new_m_ffn2_w_out': 'new_m', 'new_m_ab_w_in': 'new_m', 'new_m_dn_conv_w': 'new_m', 'new_m_dn_a_log': 'new_m', 'new_m_dn_dt_bias': 'new_m', 'new_m_dn_out_norm': 'new_m', 'new_m_sg_norm': 'new_m', 'new_m_sg_w': 'new_m', 'new_m_sg_b': 'new_m', 'new_m_ab_w_out': 'new_m', 'new_m_pool_w': 'new_m', 'new_m_pool_scale': 'new_m', 'new_m_final_norm': 'new_m', 'new_v_ffn_norm1': 'new_v', 'new_v_ffn1_w_in': 'new_v', 'new_v_ffn1_w_out': 'new_v', 'new_v_mix_norm': 'new_v', 'new_v_ffn_norm2': 'new_v', 'new_v_ffn2_w_in': 'new_v', 'new_v_ffn2_w_out': 'new_v', 'new_v_ab_w_in': 'new_v', 'new_v_dn_conv_w': 'new_v', 'new_v_dn_a_log': 'new_v', 'new_v_dn_dt_bias': 'new_v', 'new_v_dn_out_norm': 'new_v', 'new_v_sg_norm': 'new_v', 'new_v_sg_w': 'new_v', 'new_v_sg_b': 'new_v', 'new_v_ab_w_out': 'new_v', 'new_v_pool_w': 'new_v', 'new_v_pool_scale': 'new_v', 'new_v_final_norm': 'new_v'}


def _forward(args):
    return _fwd_reference(*[args[k] for k in FWD_PARAMS])


def _output_shape():
    def fwd():
        inp = _fwd_setup_inputs(0)
        return _fwd_reference(*[inp[k] for k in FWD_PARAMS])
    out = _jax.eval_shape(fwd)
    return out.shape, out.dtype

N_MICROBATCH = 1
ADAM_LR = 0.001
ADAM_B1 = 0.9
ADAM_B2 = 0.999
ADAM_EPS = 1e-08
ADAM_WD = 0.01
ADAM_STEP = 10
PER_EXAMPLE_BATCH_AXIS = {'x': 0, 'loss_target': 0}
SHARED_INPUTS = []
_WEIGHT_DTYPES = {'ffn_norm1': _jnp.float32, 'ffn1_w_in': _jnp.float32, 'ffn1_w_out': _jnp.float32, 'mix_norm': _jnp.float32, 'ffn_norm2': _jnp.float32, 'ffn2_w_in': _jnp.float32, 'ffn2_w_out': _jnp.float32, 'ab_w_in': _jnp.float32, 'dn_conv_w': _jnp.float32, 'dn_a_log': _jnp.float32, 'dn_dt_bias': _jnp.float32, 'dn_out_norm': _jnp.float32, 'sg_norm': _jnp.float32, 'sg_w': _jnp.float32, 'sg_b': _jnp.float32, 'ab_w_out': _jnp.float32, 'pool_w': _jnp.float32, 'pool_scale': _jnp.float32, 'final_norm': _jnp.float32}
MOMENT_SCALE = {'ffn_norm1': 1.114701e-01, 'ffn1_w_in': 4.551960e-02, 'ffn1_w_out': 7.421718e-02, 'mix_norm': 1.892383e-01, 'ffn_norm2': 8.323945e-02, 'ffn2_w_in': 3.323593e-02, 'ffn2_w_out': 5.428348e-02, 'ab_w_in': 1.195166e-01, 'dn_conv_w': 9.188100e-02, 'dn_a_log': 1.504616e+00, 'dn_dt_bias': 1.234539e+00, 'dn_out_norm': 2.934820e-01, 'sg_norm': 1.017851e-01, 'sg_w': 9.903062e-02, 'sg_b': 1.479534e-01, 'ab_w_out': 1.514896e-01, 'pool_w': 1.456739e-01, 'pool_scale': 6.129059e-01, 'final_norm': 6.424756e+01}


def _to_microbatches(a, axis):
    t = _jnp.moveaxis(a, axis, 0)
    t = t.reshape((N_MICROBATCH, t.shape[0] // N_MICROBATCH) + t.shape[1:])
    return _jnp.moveaxis(t, 1, axis + 1)


def setup_inputs(seed: int = 0) -> dict:
    inp = _fwd_setup_inputs(seed)
    key = _jax.random.fold_in(_jax.random.key(seed), 7919)
    shape, _ = _output_shape()
    out = dict(inp)
    out["loss_target"] = _jax.random.normal(_jax.random.fold_in(key, 0), shape, _jnp.float32)
    for i, name in enumerate(TWIN_WEIGHTS):
        w = inp[name].astype(_jnp.float32)
        if MOMENT_SCALE is None:
            s = _jnp.sqrt(_jnp.mean(_jnp.square(w)) + 1e-30)
        else:
            s = MOMENT_SCALE[name]
        km, kv = _jax.random.split(_jax.random.fold_in(key, i + 1))
        out[name] = w
        out["m_" + name] = s * _jax.random.normal(km, w.shape, _jnp.float32)
        out["v_" + name] = (s * s) * _jax.random.uniform(kv, w.shape, _jnp.float32, 0.5, 1.5)
    if N_MICROBATCH > 1:
        for name, axis in PER_EXAMPLE_BATCH_AXIS.items():
            out[name] = _to_microbatches(out[name], axis)
    return {'x': out['x'], 'ffn_norm1': out['ffn_norm1'], 'ffn1_w_in': out['ffn1_w_in'], 'ffn1_w_out': out['ffn1_w_out'], 'mix_norm': out['mix_norm'], 'ffn_norm2': out['ffn_norm2'], 'ffn2_w_in': out['ffn2_w_in'], 'ffn2_w_out': out['ffn2_w_out'], 'ab_w_in': out['ab_w_in'], 'dn_conv_w': out['dn_conv_w'], 'dn_a_log': out['dn_a_log'], 'dn_dt_bias': out['dn_dt_bias'], 'dn_out_norm': out['dn_out_norm'], 'sg_norm': out['sg_norm'], 'sg_w': out['sg_w'], 'sg_b': out['sg_b'], 'ab_w_out': out['ab_w_out'], 'pool_w': out['pool_w'], 'pool_scale': out['pool_scale'], 'final_norm': out['final_norm'], 'loss_target': out['loss_target'], 'm_ffn_norm1': out['m_ffn_norm1'], 'm_ffn1_w_in': out['m_ffn1_w_in'], 'm_ffn1_w_out': out['m_ffn1_w_out'], 'm_mix_norm': out['m_mix_norm'], 'm_ffn_norm2': out['m_ffn_norm2'], 'm_ffn2_w_in': out['m_ffn2_w_in'], 'm_ffn2_w_out': out['m_ffn2_w_out'], 'm_ab_w_in': out['m_ab_w_in'], 'm_dn_conv_w': out['m_dn_conv_w'], 'm_dn_a_log': out['m_dn_a_log'], 'm_dn_dt_bias': out['m_dn_dt_bias'], 'm_dn_out_norm': out['m_dn_out_norm'], 'm_sg_norm': out['m_sg_norm'], 'm_sg_w': out['m_sg_w'], 'm_sg_b': out['m_sg_b'], 'm_ab_w_out': out['m_ab_w_out'], 'm_pool_w': out['m_pool_w'], 'm_pool_scale': out['m_pool_scale'], 'm_final_norm': out['m_final_norm'], 'v_ffn_norm1': out['v_ffn_norm1'], 'v_ffn1_w_in': out['v_ffn1_w_in'], 'v_ffn1_w_out': out['v_ffn1_w_out'], 'v_mix_norm': out['v_mix_norm'], 'v_ffn_norm2': out['v_ffn_norm2'], 'v_ffn2_w_in': out['v_ffn2_w_in'], 'v_ffn2_w_out': out['v_ffn2_w_out'], 'v_ab_w_in': out['v_ab_w_in'], 'v_dn_conv_w': out['v_dn_conv_w'], 'v_dn_a_log': out['v_dn_a_log'], 'v_dn_dt_bias': out['v_dn_dt_bias'], 'v_dn_out_norm': out['v_dn_out_norm'], 'v_sg_norm': out['v_sg_norm'], 'v_sg_w': out['v_sg_w'], 'v_sg_b': out['v_sg_b'], 'v_ab_w_out': out['v_ab_w_out'], 'v_pool_w': out['v_pool_w'], 'v_pool_scale': out['v_pool_scale'], 'v_final_norm': out['v_final_norm']}


def _loss(weights, diff, rest, loss_target):
    with _jax.named_scope("forward"):
        args = {**rest, TWIN_DIFF_INPUT: diff, **{k: w.astype(_WEIGHT_DTYPES[k]) for k, w in weights.items()}}
        y = _forward(args)
    with _jax.named_scope("loss_head"):
        err = _jnp.square(y.astype(_jnp.float32) - loss_target)
        return 0.5 * _jnp.sum(_jnp.mean(err, axis=-1)) if err.ndim else 0.5 * err


def _adamw(w, g, m, v):
    m = ADAM_B1 * m + (1.0 - ADAM_B1) * g
    v = ADAM_B2 * v + (1.0 - ADAM_B2) * _jnp.square(g)
    m_hat = m / (1.0 - ADAM_B1 ** ADAM_STEP)
    v_hat = v / (1.0 - ADAM_B2 ** ADAM_STEP)
    delta = -ADAM_LR * (m_hat / (_jnp.sqrt(v_hat) + ADAM_EPS) + ADAM_WD * w)
    return delta, m, v


def reference(x, ffn_norm1, ffn1_w_in, ffn1_w_out, mix_norm, ffn_norm2, ffn2_w_in, ffn2_w_out, ab_w_in, dn_conv_w, dn_a_log, dn_dt_bias, dn_out_norm, sg_norm, sg_w, sg_b, ab_w_out, pool_w, pool_scale, final_norm, loss_target, m_ffn_norm1, m_ffn1_w_in, m_ffn1_w_out, m_mix_norm, m_ffn_norm2, m_ffn2_w_in, m_ffn2_w_out, m_ab_w_in, m_dn_conv_w, m_dn_a_log, m_dn_dt_bias, m_dn_out_norm, m_sg_norm, m_sg_w, m_sg_b, m_ab_w_out, m_pool_w, m_pool_scale, m_final_norm, v_ffn_norm1, v_ffn1_w_in, v_ffn1_w_out, v_mix_norm, v_ffn_norm2, v_ffn2_w_in, v_ffn2_w_out, v_ab_w_in, v_dn_conv_w, v_dn_a_log, v_dn_dt_bias, v_dn_out_norm, v_sg_norm, v_sg_w, v_sg_b, v_ab_w_out, v_pool_w, v_pool_scale, v_final_norm):
    given = dict(x=x, ffn_norm1=ffn_norm1, ffn1_w_in=ffn1_w_in, ffn1_w_out=ffn1_w_out, mix_norm=mix_norm, ffn_norm2=ffn_norm2, ffn2_w_in=ffn2_w_in, ffn2_w_out=ffn2_w_out, ab_w_in=ab_w_in, dn_conv_w=dn_conv_w, dn_a_log=dn_a_log, dn_dt_bias=dn_dt_bias, dn_out_norm=dn_out_norm, sg_norm=sg_norm, sg_w=sg_w, sg_b=sg_b, ab_w_out=ab_w_out, pool_w=pool_w, pool_scale=pool_scale, final_norm=final_norm, loss_target=loss_target, m_ffn_norm1=m_ffn_norm1, m_ffn1_w_in=m_ffn1_w_in, m_ffn1_w_out=m_ffn1_w_out, m_mix_norm=m_mix_norm, m_ffn_norm2=m_ffn_norm2, m_ffn2_w_in=m_ffn2_w_in, m_ffn2_w_out=m_ffn2_w_out, m_ab_w_in=m_ab_w_in, m_dn_conv_w=m_dn_conv_w, m_dn_a_log=m_dn_a_log, m_dn_dt_bias=m_dn_dt_bias, m_dn_out_norm=m_dn_out_norm, m_sg_norm=m_sg_norm, m_sg_w=m_sg_w, m_sg_b=m_sg_b, m_ab_w_out=m_ab_w_out, m_pool_w=m_pool_w, m_pool_scale=m_pool_scale, m_final_norm=m_final_norm, v_ffn_norm1=v_ffn_norm1, v_ffn1_w_in=v_ffn1_w_in, v_ffn1_w_out=v_ffn1_w_out, v_mix_norm=v_mix_norm, v_ffn_norm2=v_ffn_norm2, v_ffn2_w_in=v_ffn2_w_in, v_ffn2_w_out=v_ffn2_w_out, v_ab_w_in=v_ab_w_in, v_dn_conv_w=v_dn_conv_w, v_dn_a_log=v_dn_a_log, v_dn_dt_bias=v_dn_dt_bias, v_dn_out_norm=v_dn_out_norm, v_sg_norm=v_sg_norm, v_sg_w=v_sg_w, v_sg_b=v_sg_b, v_ab_w_out=v_ab_w_out, v_pool_w=v_pool_w, v_pool_scale=v_pool_scale, v_final_norm=v_final_norm)
    weights = {n: given[n] for n in TWIN_WEIGHTS}
    shared = {n: given[n] for n in SHARED_INPUTS}
    per_example = {n: given[n] for n in ['x']}
    grad_fn = _jax.value_and_grad(_loss, argnums=(0, 1))

    def one_microbatch(ex, loss_target):
        ex = dict(ex)
        diff = ex.pop(TWIN_DIFF_INPUT)
        return grad_fn(weights, diff, {**shared, **ex}, loss_target)

    if N_MICROBATCH == 1:
        loss, (grad_w, grad_x) = one_microbatch(per_example, given["loss_target"])
    else:
        def body(carry, xs):
            loss_sum, grad_sum = carry
            l_k, (gw_k, gx_k) = one_microbatch(xs[0], xs[1])
            with _jax.named_scope("update"):
                return (loss_sum + l_k, _jax.tree.map(_jnp.add, grad_sum, gw_k)), gx_k

        init = (_jnp.zeros((), _jnp.float32), _jax.tree.map(_jnp.zeros_like, weights))
        (loss, grad_w), grad_x = _jax.lax.scan(body, init, (per_example, given["loss_target"]))
    with _jax.named_scope("update"):
        delta_w, new_m, new_v = {}, {}, {}
        for n in TWIN_WEIGHTS:
            delta_w[n], new_m[n], new_v[n] = _adamw(weights[n], grad_w[n], given["m_" + n], given["v_" + n])
    return (loss, grad_x, *[grad_w[n] for n in TWIN_WEIGHTS], *[delta_w[n] for n in TWIN_WEIGHTS],
            *[new_m[n] for n in TWIN_WEIGHTS], *[new_v[n] for n in TWIN_WEIGHTS])
```

```python
import functools

import jax
import jax.numpy as jnp
from jax import lax
from jax.experimental import pallas as pl
from jax.experimental.pallas import tpu as pltpu

F32, BF16 = jnp.float32, jnp.bfloat16
NORM_EPS = 1e-6
D_MODEL = 1024
D_FF = 2816
N_CHIPS = 4
FF_CHUNK = 2 * D_FF // N_CHIPS
DN_HEADS, DN_DIM, DN_CHUNK, DN_CONV = 4, 128, 64, 4
SG_GROUPS, SG_DIM, SG_CHUNK = 4, 128, 128
POOL_WINDOWS = (2, 4, 8, 16)
POOL_DIM = 256
POOL_HALO = 16
CONV_HALO = 8
PROJ_W = 3200
BA_BLOCK = 3072 // 128
ADAM_LR, ADAM_B1, ADAM_B2, ADAM_EPS, ADAM_WD, ADAM_STEP = 0.001, 0.9, 0.999, 1e-08, 0.01, 10
VMEM_BIG = 52 * 1024 * 1024
PACK_LANES = 1024
PACK_ROW_BLOCK = 256
MESH = pl.DeviceIdType.MESH
HI = lax.Precision.HIGHEST
ANY = pl.BlockSpec(memory_space=pl.ANY)


def _params(sem=None, vmem=None):
    return pltpu.CompilerParams(dimension_semantics=sem, vmem_limit_bytes=vmem)


def _dot(a, b):
    return jnp.dot(a, b, preferred_element_type=F32)


def _dot_nt(a, b):
    return lax.dot_general(a, b, (((1,), (1,)), ((), ())), preferred_element_type=F32)


def _dot_tn(a, b):
    return lax.dot_general(a, b, (((0,), (0,)), ((), ())), preferred_element_type=F32)


def _dot_hi(a, b):
    return jnp.dot(a, b, preferred_element_type=F32, precision=HI)


def _bf(a):
    return a.astype(BF16)


def _rms_stats(x):
    r = lax.rsqrt(jnp.mean(x * x, axis=-1, keepdims=True) + NORM_EPS)
    return x * r, r


def _rms_bwd(dh, xhat, r, w):
    dhn = dh * w
    return r * (dhn - xhat * jnp.mean(dhn * xhat, axis=-1, keepdims=True))


def _sigmoid(x):
    return jax.nn.sigmoid(x)


def _silu_grad(x, s):
    return s * (1.0 + x * (1.0 - s))


def _gelu(x):
    return 0.5 * x * (1.0 + lax.erf(x * 0.7071067811865476))


def _gelu_grad(x):
    return 0.5 * (1.0 + lax.erf(x * 0.7071067811865476)) + x * jnp.exp(-0.5 * x * x) * 0.3989422804014327


def _softplus(x):
    return jnp.maximum(x, 0.0) + jnp.log(1.0 + jnp.exp(-jnp.abs(x)))


def _tile(n, pref):
    t = min(n, pref)
    assert n % t == 0, (n, t)
    return t


def _ffn_fwd(x, nw, win, wout, layer, name):
    T, D = x.shape
    tm = _tile(T, 512)
    nj = N_CHIPS // 2

    def body(x_ref, n_ref, wg_ref, wu_ref, wo_ref, xo_ref, g_ref, u_ref, hb_ref, h_s, acc_s):
        j = pl.program_id(1)

        @pl.when(j == 0)
        def _():
            xhat, _ = _rms_stats(x_ref[...])
            h = _bf(xhat * n_ref[...])
            h_s[...] = h
            hb_ref[...] = h
            acc_s[...] = jnp.zeros_like(acc_s)

        h = h_s[...]
        g = _dot(h, wg_ref[...])
        u = _dot(h, wu_ref[...])
        a = _bf(g * _sigmoid(g) * u)
        acc_s[...] += _dot(a, wo_ref[...].reshape(FF_CHUNK, D))
        g_ref[...] = _bf(g)
        u_ref[...] = _bf(u)

        @pl.when(j == nj - 1)
        def _():
            xo_ref[...] = x_ref[...] + 0.5 * acc_s[...]

    return pl.pallas_call(
        body, name=name, grid=(T // tm, nj),
        in_specs=[pl.BlockSpec((tm, D), lambda i, j: (i, 0)),
                  pl.BlockSpec((None, 1, D), lambda i, j: (layer, 0, 0)),
                  pl.BlockSpec((None, None, D, FF_CHUNK), lambda i, j: (j, layer, 0, 0)),
                  pl.BlockSpec((None, None, D, FF_CHUNK), lambda i, j: (nj + j, layer, 0, 0)),
                  pl.BlockSpec((2, None, FF_CHUNK // 2, D), lambda i, j: (j, layer, 0, 0))],
        out_specs=[pl.BlockSpec((tm, D), lambda i, j: (i, 0)),
                   pl.BlockSpec((tm, FF_CHUNK), lambda i, j: (i, j)),
                   pl.BlockSpec((tm, FF_CHUNK), lambda i, j: (i, j)),
                   pl.BlockSpec((tm, D), lambda i, j: (i, 0))],
        out_shape=[jax.ShapeDtypeStruct((T, D), F32), jax.ShapeDtypeStruct((T, D_FF), BF16),
                   jax.ShapeDtypeStruct((T, D_FF), BF16), jax.ShapeDtypeStruct((T, D), BF16)],
        scratch_shapes=[pltpu.VMEM((tm, D), BF16), pltpu.VMEM((tm, D), F32)],
        compiler_params=_params(("arbitrary", "arbitrary"), VMEM_BIG),
    )(x, nw, win, win, wout)


def _ffn_bwd(dxo, x, nw, g, u, win, wout, layer, name):
    T, D = x.shape
    tm = _tile(T, 256)
    nj = N_CHIPS // 2

    def body(dxo_ref, x_ref, n_ref, g_ref, u_ref, wg_ref, wu_ref, wo_ref,
             dx_ref, dg_ref, du_ref, a_ref, dyb_ref, dn_ref, dyb_s, dh_s):
        i, j = pl.program_id(0), pl.program_id(1)

        @pl.when(j == 0)
        def _():
            dyb = _bf(0.5 * dxo_ref[...])
            dyb_s[...] = dyb
            dyb_ref[...] = dyb
            dh_s[...] = jnp.zeros_like(dh_s)

        @pl.when((i == 0) & (j == 0))
        def _():
            dn_ref[...] = jnp.zeros_like(dn_ref)

        da = _dot_nt(dyb_s[...], wo_ref[...].reshape(FF_CHUNK, D))
        gv = g_ref[...].astype(F32)
        uv = u_ref[...].astype(F32)
        sg = _sigmoid(gv)
        sl = gv * sg
        dgb = _bf(da * uv * _silu_grad(gv, sg))
        dub = _bf(da * sl)
        a_ref[...] = _bf(sl * uv)
        dg_ref[...] = dgb
        du_ref[...] = dub
        dh_s[...] += _dot_nt(dgb, wg_ref[...]) + _dot_nt(dub, wu_ref[...])

        @pl.when(j == nj - 1)
        def _():
            xhat, r = _rms_stats(x_ref[...])
            dh = dh_s[...]
            dx_ref[...] = dxo_ref[...] + _rms_bwd(dh, xhat, r, n_ref[...])
            dn_ref[...] += jnp.sum(dh * xhat, axis=0, keepdims=True)

    return pl.pallas_call(
        body, name=name, grid=(T // tm, nj),
        in_specs=[pl.BlockSpec((tm, D), lambda i, j: (i, 0)),
                  pl.BlockSpec((tm, D), lambda i, j: (i, 0)),
                  pl.BlockSpec((None, 1, D), lambda i, j: (layer, 0, 0)),
                  pl.BlockSpec((tm, FF_CHUNK), lambda i, j: (i, j)),
                  pl.BlockSpec((tm, FF_CHUNK), lambda i, j: (i, j)),
                  pl.BlockSpec((None, None, D, FF_CHUNK), lambda i, j: (j, layer, 0, 0)),
                  pl.BlockSpec((None, None, D, FF_CHUNK), lambda i, j: (nj + j, layer, 0, 0)),
                  pl.BlockSpec((2, None, FF_CHUNK // 2, D), lambda i, j: (j, layer, 0, 0))],
        out_specs=[pl.BlockSpec((tm, D), lambda i, j: (i, 0)),
                   pl.BlockSpec((tm, FF_CHUNK), lambda i, j: (i, j)),
                   pl.BlockSpec((tm, FF_CHUNK), lambda i, j: (i, j)),
                   pl.BlockSpec((tm, FF_CHUNK), lambda i, j: (i, j)),
                   pl.BlockSpec((tm, D), lambda i, j: (i, 0)),
                   pl.BlockSpec((1, D), lambda i, j: (0, 0))],
        out_shape=[jax.ShapeDtypeStruct((T, D), F32), jax.ShapeDtypeStruct((T, D_FF), BF16),
                   jax.ShapeDtypeStruct((T, D_FF), BF16), jax.ShapeDtypeStruct((T, D_FF), BF16),
                   jax.ShapeDtypeStruct((T, D), BF16), jax.ShapeDtypeStruct((1, D), F32)],
        scratch_shapes=[pltpu.VMEM((tm, D), BF16), pltpu.VMEM((tm, D), F32)],
        compiler_params=_params(("arbitrary", "arbitrary"), VMEM_BIG),
    )(dxo, x, nw, g, u, win, win, wout)


def _matmul_tn(a, b, bm, bn, name, stack_n=False):
    T, M = a.shape
    N = b.shape[1]
    tk = _tile(T, 1024)
    bm, bn = _tile(M, bm), _tile(N, bn)

    def body(a_ref, b_ref, o_ref):
        @pl.when(pl.program_id(2) == 0)
        def _():
            o_ref[...] = jnp.zeros_like(o_ref)

        o_ref[...] += _dot_tn(_bf(a_ref[...]), _bf(b_ref[...]))

    if stack_n:
        out_spec = pl.BlockSpec((None, bm, bn), lambda m, n, k: (n, m, 0))
        out_shape = jax.ShapeDtypeStruct((N // bn, M, bn), F32)
    else:
        out_spec = pl.BlockSpec((bm, bn), lambda m, n, k: (m, n))
        out_shape = jax.ShapeDtypeStruct((M, N), F32)
    return pl.pallas_call(
        body, name=name, grid=(M // bm, N // bn, T // tk),
        in_specs=[pl.BlockSpec((tk, bm), lambda m, n, k: (k, m)),
                  pl.BlockSpec((tk, bn), lambda m, n, k: (k, n))],
        out_specs=out_spec, out_shape=out_shape,
        compiler_params=_params(("parallel", "parallel", "arbitrary"), VMEM_BIG),
    )(a, b)


def _matmul(a, b, name, trans_b=False, res=None, out_dtype=F32):
    T, K = a.shape
    N = b.shape[0] if trans_b else b.shape[1]
    tm = _tile(T, 512)

    def body(*refs):
        a_ref, b_ref = refs[0], refs[1]
        o_ref = refs[-1]
        av, bv = _bf(a_ref[...]), _bf(b_ref[...])
        acc = _dot_nt(av, bv) if trans_b else _dot(av, bv)
        if res is not None:
            acc = acc + refs[2][...]
        o_ref[...] = acc.astype(out_dtype)

    in_specs = [pl.BlockSpec((tm, K), lambda i: (i, 0)), pl.BlockSpec(b.shape, lambda i: (0, 0))]
    args = [a, b]
    if res is not None:
        in_specs.append(pl.BlockSpec((tm, N), lambda i: (i, 0)))
        args.append(res)
    return pl.pallas_call(
        body, name=name, grid=(T // tm,), in_specs=in_specs,
        out_specs=pl.BlockSpec((tm, N), lambda i: (i, 0)),
        out_shape=jax.ShapeDtypeStruct((T, N), out_dtype),
        compiler_params=_params(("parallel",), VMEM_BIG),
    )(*args)


def _rms_fwd_call(x, nw, layer, name):
    T, D = x.shape
    tm = _tile(T, 512)

    def body(x_ref, n_ref, o_ref):
        xhat, _ = _rms_stats(x_ref[...])
        o_ref[...] = _bf(xhat * n_ref[...])

    return pl.pallas_call(
        body, name=name, grid=(T // tm,),
        in_specs=[pl.BlockSpec((tm, D), lambda i: (i, 0)), pl.BlockSpec((None, 1, D), lambda i: (layer, 0, 0))],
        out_specs=pl.BlockSpec((tm, D), lambda i: (i, 0)),
        out_shape=jax.ShapeDtypeStruct((T, D), BF16),
        compiler_params=_params(("parallel",)),
    )(x, nw)


def _rms_bwd_call(dh, x, nw, dres, layer, name):
    T, D = x.shape
    tm = _tile(T, 512)

    def body(dh_ref, x_ref, n_ref, dr_ref, dx_ref, dn_ref):
        @pl.when(pl.program_id(0) == 0)
        def _():
            dn_ref[...] = jnp.zeros_like(dn_ref)

        xhat, r = _rms_stats(x_ref[...])
        dh_v = dh_ref[...]
        dx_ref[...] = dr_ref[...] + _rms_bwd(dh_v, xhat, r, n_ref[...])
        dn_ref[...] += jnp.sum(dh_v * xhat, axis=0, keepdims=True)

    row = pl.BlockSpec((tm, D), lambda i: (i, 0))
    return pl.pallas_call(
        body, name=name, grid=(T // tm,),
        in_specs=[row, row, pl.BlockSpec((None, 1, D), lambda i: (layer, 0, 0)), row],
        out_specs=[row, pl.BlockSpec((1, D), lambda i: (0, 0))],
        out_shape=[jax.ShapeDtypeStruct((T, D), F32), jax.ShapeDtypeStruct((1, D), F32)],
        compiler_params=_params(("arbitrary",)),
    )(dh, x, nw, dres)


def _shift_rows(x, s):
    n = x.shape[0]
    s = s % n
    return x if s == 0 else pltpu.roll(x, s, 0)


def _conv_fwd(proj, conv_w, name):
    T = proj.shape[0]
    C = 3 * DN_HEADS * DN_DIM
    cb = 512
    tm = _tile(T, 512)
    hb = tm // CONV_HALO

    def body(x_ref, xp_ref, w_ref, o_ref):
        i = pl.program_id(1)
        prev = jnp.where(i == 0, 0.0, xp_ref[...])
        ext = jnp.concatenate([prev, x_ref[...]], axis=0)
        w = w_ref[...]
        y = ext * w[DN_CONV - 1:DN_CONV, :]
        for k in range(DN_CONV - 1):
            y = y + _shift_rows(ext, DN_CONV - 1 - k) * w[k:k + 1, :]
        y = y[CONV_HALO:, :]
        o_ref[...] = y * _sigmoid(y)

    return pl.pallas_call(
        body, name=name, grid=(C // cb, T // tm),
        in_specs=[pl.BlockSpec((tm, cb), lambda c, i: (i, c)),
                  pl.BlockSpec((CONV_HALO, cb), lambda c, i: (jnp.maximum(i * hb - 1, 0), c)),
                  pl.BlockSpec((DN_CONV, cb), lambda c, i: (0, c))],
        out_specs=pl.BlockSpec((tm, cb), lambda c, i: (i, c)),
        out_shape=jax.ShapeDtypeStruct((T, C), F32),
        compiler_params=_params(("parallel", "parallel")),
    )(proj, proj, conv_w)


def _conv_bwd(dy, proj, conv_w, name):
    T = proj.shape[0]
    C = 3 * DN_HEADS * DN_DIM
    cb = 512
    tm = _tile(T, 512)
    hb = tm // CONV_HALO
    nt = T // tm

    def body(x_ref, xp_ref, xn_ref, dy_ref, dyn_ref, w_ref, dx_ref, dw_ref):
        i = pl.program_id(1)

        @pl.when(i == 0)
        def _():
            dw_ref[...] = jnp.zeros_like(dw_ref)

        prev = jnp.where(i == 0, 0.0, xp_ref[...])
        ext = jnp.concatenate([prev, x_ref[...], xn_ref[...]], axis=0)
        dy_ext = jnp.concatenate([jnp.zeros((CONV_HALO, cb), F32), dy_ref[...],
                                  jnp.where(i == nt - 1, 0.0, dyn_ref[...])], axis=0)
        w = w_ref[...]
        shifted = [_shift_rows(ext, DN_CONV - 1 - k) for k in range(DN_CONV)]
        y = shifted[0] * w[0:1, :]
        for k in range(1, DN_CONV):
            y = y + shifted[k] * w[k:k + 1, :]
        s = _sigmoid(y)
        dpre = dy_ext * _silu_grad(y, s)
        dx = dpre * w[DN_CONV - 1:DN_CONV, :]
        for k in range(DN_CONV - 1):
            dx = dx + _shift_rows(dpre, -(DN_CONV - 1 - k)) * w[k:k + 1, :]
        dx_ref[...] = _bf(dx[CONV_HALO:CONV_HALO + tm, :])
        rows = [jnp.sum((dpre * shifted[k])[CONV_HALO:CONV_HALO + tm, :], axis=0, keepdims=True) for k in range(DN_CONV)]
        dw_ref[...] += jnp.concatenate(rows, axis=0)

    last_halo = T // CONV_HALO - 1
    return pl.pallas_call(
        body, name=name, grid=(C // cb, nt),
        in_specs=[pl.BlockSpec((tm, cb), lambda c, i: (i, c)),
                  pl.BlockSpec((CONV_HALO, cb), lambda c, i: (jnp.maximum(i * hb - 1, 0), c)),
                  pl.BlockSpec((CONV_HALO, cb), lambda c, i: (jnp.minimum((i + 1) * hb, last_halo), c)),
                  pl.BlockSpec((tm, cb), lambda c, i: (i, c)),
                  pl.BlockSpec((CONV_HALO, cb), lambda c, i: (jnp.minimum((i + 1) * hb, last_halo), c)),
                  pl.BlockSpec((DN_CONV, cb), lambda c, i: (0, c))],
        out_specs=[pl.BlockSpec((tm, cb), lambda c, i: (i, c)),
                   pl.BlockSpec((DN_CONV, cb), lambda c, i: (0, c))],
        out_shape=[jax.ShapeDtypeStruct((T, C), BF16), jax.ShapeDtypeStruct((DN_CONV, C), F32)],
        compiler_params=_params(("parallel", "arbitrary")),
    )(proj, proj, proj, dy, dy, conv_w)


def _unit_lower_inverse(low, eye):
    x = -low
    inv = eye + x
    p = x
    for _ in range(5):
        p = _dot_hi(p, p)
        inv = inv + _dot_hi(inv, p)
    return inv


def _l2_unit(x):
    r = lax.rsqrt(jnp.sum(x * x, axis=-1, keepdims=True) + NORM_EPS)
    return x * r, r


class _ChunkMasks:
    def __init__(self):
        c = DN_CHUNK
        row = lax.broadcasted_iota(jnp.int32, (c, c), 0)
        col = lax.broadcasted_iota(jnp.int32, (c, c), 1)
        self.lower, self.strict_lower = row >= col, row > col
        self.upper, self.strict_upper = row <= col, row < col
        self.eye = (row == col).astype(F32)
        self.ones = jnp.ones((c, c), F32)


def _dn_gates(ba, hp):
    coef = -jnp.exp(hp[0:1, :])
    pre = ba + hp[1:2, :]
    return _sigmoid(ba), coef * _softplus(pre), coef, pre


def _dn_head_forward(mk, qraw, kraw, vh, bc, gc, glast, state):
    f = {}
    f["qn"], f["rq"] = _l2_unit(qraw)
    qh = f["qn"] * (DN_DIM ** -0.5)
    kh, f["rk"] = _l2_unit(kraw)
    gr = _dot_hi(mk.ones, mk.eye * gc)
    dec = jnp.where(mk.lower, jnp.exp(jnp.where(mk.lower, gc - gr, 0.0)), 0.0)
    kb = kh * bc
    mkk = _dot_nt(_bf(kb), _bf(kh))
    low = jnp.where(mk.strict_lower, mkk * dec, 0.0)
    inv = _unit_lower_inverse(low, mk.eye)
    eg = jnp.exp(gc)
    sol = _dot_hi(inv, jnp.concatenate([vh * bc, kb * eg], axis=1))
    u, w = sol[:, :DN_DIM], sol[:, DN_DIM:]
    mqk = _dot_nt(_bf(qh), _bf(kh))
    attn = mqk * dec
    etl = jnp.exp(glast - gc)
    qd, kt = qh * eg, kh * etl
    sb = _bf(state)
    vn = u - _dot(_bf(w), sb)
    out = _dot(_bf(qd), sb) + _dot(_bf(attn), _bf(vn))
    egl = jnp.exp(glast)
    new_state = state * egl + _dot_tn(_bf(kt), _bf(vn))
    f.update(qh=qh, kh=kh, gr=gr, dec=dec, kb=kb, mkk=mkk, eg=eg, u=u, w=w, mqk=mqk, attn=attn, etl=etl,
             qd=qd, kt=kt, vn=vn, egl=egl)
    return out, new_state, f


def _dn_fwd(qkv, proj, hp, name):
    T = qkv.shape[0]
    n_chunks = T // DN_CHUNK
    hw = DN_HEADS * DN_DIM

    def body(q_ref, k_ref, v_ref, ba_ref, hp_ref, o_ref, sall_ref, s_s):
        @pl.when(pl.program_id(0) == 0)
        def _():
            s_s[...] = jnp.zeros_like(s_s)

        mk = _ChunkMasks()
        beta_t, graw_t, _, _ = _dn_gates(ba_ref[...], hp_ref[...])
        gcum_t = _dot_hi(mk.lower.astype(F32), graw_t)
        for h in range(DN_HEADS):
            sl = slice(h * DN_DIM, (h + 1) * DN_DIM)
            gc = gcum_t[:, DN_HEADS + h:DN_HEADS + h + 1]
            glast = gc[DN_CHUNK - 1:DN_CHUNK, :]
            state = s_s[h]
            sall_ref[0, h] = state
            out, new_state, _ = _dn_head_forward(mk, q_ref[:, sl], k_ref[:, sl], v_ref[:, sl], beta_t[:, h:h + 1],
                                                 gc, glast, state)
            o_ref[:, sl] = out
            s_s[h] = new_state

    return pl.pallas_call(
        body, name=name, grid=(n_chunks,),
        in_specs=[pl.BlockSpec((DN_CHUNK, hw), lambda n: (n, 0)),
                  pl.BlockSpec((DN_CHUNK, hw), lambda n: (n, 1)),
                  pl.BlockSpec((DN_CHUNK, hw), lambda n: (n, 2)),
                  pl.BlockSpec((DN_CHUNK, 128), lambda n: (n, BA_BLOCK)),
                  pl.BlockSpec((8, 128), lambda n: (0, 0))],
        out_specs=[pl.BlockSpec((DN_CHUNK, hw), lambda n: (n, 0)),
                   pl.BlockSpec((1, DN_HEADS, DN_DIM, DN_DIM), lambda n: (n, 0, 0, 0))],
        out_shape=[jax.ShapeDtypeStruct((T, hw), F32),
                   jax.ShapeDtypeStruct((n_chunks, DN_HEADS, DN_DIM, DN_DIM), F32)],
        scratch_shapes=[pltpu.VMEM((DN_HEADS, DN_DIM, DN_DIM), F32)],
        compiler_params=_params(("arbitrary",)),
    )(qkv, qkv, qkv, proj, hp)


def _dn_bwd(qkv, proj, hp, sall, do, name):
    T = qkv.shape[0]
    n_chunks = T // DN_CHUNK
    hw = DN_HEADS * DN_DIM
    last = n_chunks - 1

    def rowsum(x):
        return jnp.sum(x, axis=1, keepdims=True)

    def body(q_ref, k_ref, v_ref, ba_ref, hp_ref, sall_ref, do_ref,
             dq_ref, dk_ref, dv_ref, dba_ref, dhp_ref, ds_s):
        @pl.when(pl.program_id(0) == 0)
        def _():
            ds_s[...] = jnp.zeros_like(ds_s)
            dhp_ref[...] = jnp.zeros_like(dhp_ref)

        mk = _ChunkMasks()
        ba = ba_ref[...]
        beta_t, graw_t, coef, pre = _dn_gates(ba, hp_ref[...])
        gcum_t = _dot_hi(mk.lower.astype(F32), graw_t)
        lane = lax.broadcasted_iota(jnp.int32, (DN_CHUNK, 128), 1)
        rowi = lax.broadcasted_iota(jnp.int32, (DN_CHUNK, 1), 0)
        dgcum_t = jnp.zeros((DN_CHUNK, 128), F32)
        dbeta_t = jnp.zeros((DN_CHUNK, 128), F32)
        for h in range(DN_HEADS):
            sl = slice(h * DN_DIM, (h + 1) * DN_DIM)
            gc = gcum_t[:, DN_HEADS + h:DN_HEADS + h + 1]
            glast = gc[DN_CHUNK - 1:DN_CHUNK, :]
            bc = beta_t[:, h:h + 1]
            state = sall_ref[0, h]
            vh = v_ref[:, sl]
            _, _, f = _dn_head_forward(mk, q_ref[:, sl], k_ref[:, sl], vh, bc, gc, glast, state)
            qh, kh, kb, dec, eg, etl = f["qh"], f["kh"], f["kb"], f["dec"], f["eg"], f["etl"]
            u, w, vn, qd, kt = f["u"], f["w"], f["vn"], f["qd"], f["kt"]
            qb, kbf, kbb = _bf(qh), _bf(kh), _bf(kb)
            dec_t = jnp.where(mk.upper, jnp.exp(jnp.where(mk.upper, f["gr"] - gc, 0.0)), 0.0)
            mkk_t = _dot_nt(kbf, kbb)
            inv_t = _unit_lower_inverse(jnp.where(mk.strict_upper, mkk_t * dec_t, 0.0), mk.eye)
            mqk_t = _dot_nt(kbf, qb)
            attn_t = mqk_t * dec_t

            d_out = _bf(do_ref[:, sl])
            d_new = ds_s[h]
            d_newb, sb, vnb = _bf(d_new), _bf(state), _bf(vn)
            d_qd = _dot_nt(d_out, sb)
            d_attn = _dot_nt(d_out, vnb)
            d_attn_t = _dot_nt(vnb, d_out)
            d_vn = _dot(_bf(attn_t), d_out) + _dot(_bf(kt), d_newb)
            d_kt = _dot_nt(vnb, d_newb)
            d_vnb = _bf(d_vn)
            d_w = -_dot_nt(d_vnb, sb)
            ds_s[h] = d_new * f["egl"] + _dot_tn(_bf(qd), d_out) - _dot_tn(_bf(w), d_vnb)
            d_glast = jnp.sum(rowsum(d_new * state), axis=0, keepdims=True) * f["egl"]
            d_rhs = _dot_hi(inv_t, jnp.concatenate([d_vn, d_w], axis=1))
            d_bu, d_bw = d_rhs[:, :DN_DIM], d_rhs[:, DN_DIM:]
            ub, wb, d_bub, d_bwb = _bf(u), _bf(w), _bf(d_bu), _bf(d_bw)
            d_low = -(_dot_nt(d_bub, ub) + _dot_nt(d_bwb, wb))
            d_low_t = -(_dot_nt(ub, d_bub) + _dot_nt(wb, d_bwb))
            d_mkk = jnp.where(mk.strict_lower, d_low * dec, 0.0)
            d_mkk_t = jnp.where(mk.strict_upper, d_low_t * dec_t, 0.0)
            d_mqk = jnp.where(mk.lower, d_attn * dec, 0.0)
            d_mqk_t = jnp.where(mk.upper, d_attn_t * dec_t, 0.0)
            bw = kb * eg
            d_kb = _dot(_bf(d_mkk), kbf) + d_bw * eg
            d_k = _dot(_bf(d_mkk_t), kbb) + _dot(_bf(d_mqk_t), qb) + d_kt * etl + d_kb * bc
            d_q = _dot(_bf(d_mqk), kbf) + d_qd * eg
            d_beta = rowsum(d_kb * kh) + rowsum(d_bu * vh)
            dv_ref[:, sl] = d_bu * bc
            e_mat = d_mkk * f["mkk"] + d_mqk * f["mqk"]
            e_mat_t = d_mkk_t * mkk_t + d_mqk_t * mqk_t
            kt_term = rowsum(d_kt * kt)
            d_g = rowsum(e_mat) - rowsum(e_mat_t) + rowsum(d_qd * qd) + rowsum(d_bw * bw) - kt_term
            d_glast = d_glast + jnp.sum(kt_term, axis=0, keepdims=True)
            d_g = d_g + jnp.where(rowi == DN_CHUNK - 1, d_glast, 0.0)
            qn = f["qn"]
            d_qs = d_q * (DN_DIM ** -0.5)
            dq_ref[:, sl] = f["rq"] * (d_qs - qn * rowsum(d_qs * qn))
            dk_ref[:, sl] = f["rk"] * (d_k - kh * rowsum(d_k * kh))
            dgcum_t = jnp.where(lane == DN_HEADS + h, d_g, dgcum_t)
            dbeta_t = jnp.where(lane == h, d_beta, dbeta_t)
        dgraw_t = _dot_hi(mk.upper.astype(F32), dgcum_t)
        sp = _sigmoid(pre)
        d_pre = dgraw_t * coef * sp
        dba_ref[...] = jnp.where(lane < DN_HEADS, dbeta_t * beta_t * (1.0 - beta_t),
                                 jnp.where(lane < 2 * DN_HEADS, d_pre, 0.0))
        in_g = (lane >= DN_HEADS) & (lane < 2 * DN_HEADS)
        d_alog = jnp.sum(jnp.where(in_g, dgraw_t * graw_t, 0.0), axis=0, keepdims=True)
        d_dtb = jnp.sum(jnp.where(in_g, d_pre, 0.0), axis=0, keepdims=True)
        dhp_ref[...] += jnp.concatenate([d_alog, d_dtb, jnp.zeros((6, 128), F32)], axis=0)

    rev = lambda n: last - n
    return pl.pallas_call(
        body, name=name, grid=(n_chunks,),
        in_specs=[pl.BlockSpec((DN_CHUNK, hw), lambda n: (rev(n), 0)),
                  pl.BlockSpec((DN_CHUNK, hw), lambda n: (rev(n), 1)),
                  pl.BlockSpec((DN_CHUNK, hw), lambda n: (rev(n), 2)),
                  pl.BlockSpec((DN_CHUNK, 128), lambda n: (rev(n), BA_BLOCK)),
                  pl.BlockSpec((8, 128), lambda n: (0, 0)),
                  pl.BlockSpec((1, DN_HEADS, DN_DIM, DN_DIM), lambda n: (rev(n), 0, 0, 0)),
                  pl.BlockSpec((DN_CHUNK, hw), lambda n: (rev(n), 0))],
        out_specs=[pl.BlockSpec((DN_CHUNK, hw), lambda n: (rev(n), 0)),
                   pl.BlockSpec((DN_CHUNK, hw), lambda n: (rev(n), 0)),
                   pl.BlockSpec((DN_CHUNK, hw), lambda n: (rev(n), 0)),
                   pl.BlockSpec((DN_CHUNK, 128), lambda n: (rev(n), 0)),
                   pl.BlockSpec((8, 128), lambda n: (0, 0))],
        out_shape=[jax.ShapeDtypeStruct((T, hw), F32)] * 3
        + [jax.ShapeDtypeStruct((T, 128), F32), jax.ShapeDtypeStruct((8, 128), F32)],
        scratch_shapes=[pltpu.VMEM((DN_HEADS, DN_DIM, DN_DIM), F32)],
        compiler_params=_params(("arbitrary",)),
    )(qkv, qkv, qkv, proj, hp, sall, do)


def _mix_fwd(o, proj, dn_norm, sg_norm, sg_w, sg_bt, name):
    T = o.shape[0]
    tm = _tile(T, 512)
    hw = DN_HEADS * DN_DIM
    nc = tm // SG_CHUNK

    def body(o_ref, z_ref, su_ref, sv_ref, dnn_ref, sgn_ref, sgw_ref, sgb_ref, mix_ref):
        dnn = dnn_ref[...]
        for h in range(DN_HEADS):
            sl = slice(h * DN_DIM, (h + 1) * DN_DIM)
            xhat, _ = _rms_stats(o_ref[:, sl])
            z = z_ref[:, sl]
            mix_ref[:, sl] = _bf(xhat * dnn * (z * _sigmoid(z)))
        tri = lax.broadcasted_iota(jnp.int32, (SG_CHUNK, SG_CHUNK), 0) >= lax.broadcasted_iota(jnp.int32, (SG_CHUNK, SG_CHUNK), 1)
        for g in range(SG_GROUPS):
            sl = slice(g * SG_DIM, (g + 1) * SG_DIM)
            xhat, _ = _rms_stats(_gelu(sv_ref[:, sl]))
            svn = _bf(xhat * sgn_ref[g:g + 1, :])
            sua = _gelu(su_ref[:, sl])
            wt = _bf(jnp.where(tri, sgw_ref[g], 0.0))
            bias = sgb_ref[:, g:g + 1]
            for c in range(nc):
                rows = slice(c * SG_CHUNK, (c + 1) * SG_CHUNK)
                mixed = _dot(wt, svn[rows, :]) + bias
                mix_ref[rows, hw + g * SG_DIM:hw + (g + 1) * SG_DIM] = _bf(sua[rows, :] * mixed)

    full = lambda shape: pl.BlockSpec(shape, lambda i: (0,) * len(shape))
    return pl.pallas_call(
        body, name=name, grid=(T // tm,),
        in_specs=[pl.BlockSpec((tm, hw), lambda i: (i, 0)),
                  pl.BlockSpec((tm, hw), lambda i: (i, 3)),
                  pl.BlockSpec((tm, hw), lambda i: (i, 4)),
                  pl.BlockSpec((tm, hw), lambda i: (i, 5)),
                  full((1, DN_DIM)), full((SG_GROUPS, SG_DIM)), full((SG_GROUPS, SG_CHUNK, SG_CHUNK)),
                  full((SG_CHUNK, 128))],
        out_specs=pl.BlockSpec((tm, 2 * hw), lambda i: (i, 0)),
        out_shape=jax.ShapeDtypeStruct((T, 2 * hw), BF16),
        compiler_params=_params(("parallel",)),
    )(o, proj, proj, proj, dn_norm, sg_norm, sg_w, sg_bt)


def _mix_bwd(dmix, o, proj, dn_norm, sg_norm, sg_w, sg_bt, name):
    T = o.shape[0]
    tm = _tile(T, 512)
    hw = DN_HEADS * DN_DIM
    nc = tm // SG_CHUNK

    def body(dm_ref, o_ref, z_ref, su_ref, sv_ref, dnn_ref, sgn_ref, sgw_ref, sgb_ref,
             do_ref, dz_ref, ddnn_ref, dsgn_ref, dsgw_ref, dsgb_ref):
        @pl.when(pl.program_id(0) == 0)
        def _():
            ddnn_ref[...] = jnp.zeros_like(ddnn_ref)
            dsgn_ref[...] = jnp.zeros_like(dsgn_ref)
            dsgw_ref[...] = jnp.zeros_like(dsgw_ref)
            dsgb_ref[...] = jnp.zeros_like(dsgb_ref)

        dnn = dnn_ref[...]
        ddnn = jnp.zeros((1, DN_DIM), F32)
        for h in range(DN_HEADS):
            sl = slice(h * DN_DIM, (h + 1) * DN_DIM)
            xhat, r = _rms_stats(o_ref[:, sl])
            z = z_ref[:, sl]
            sz = _sigmoid(z)
            doa = dm_ref[:, sl]
            dyn = doa * (z * sz)
            dz_ref[:, sl] = _bf(doa * xhat * dnn * _silu_grad(z, sz))
            do_ref[:, sl] = _rms_bwd(dyn, xhat, r, dnn)
            ddnn = ddnn + jnp.sum(dyn * xhat, axis=0, keepdims=True)
        ddnn_ref[...] += ddnn
        tri = lax.broadcasted_iota(jnp.int32, (SG_CHUNK, SG_CHUNK), 0) >= lax.broadcasted_iota(jnp.int32, (SG_CHUNK, SG_CHUNK), 1)
        lane = lax.broadcasted_iota(jnp.int32, (SG_CHUNK, 128), 1)
        dsgb = jnp.zeros((SG_CHUNK, 128), F32)
        dsgn_rows = []
        for g in range(SG_GROUPS):
            sl = slice(g * SG_DIM, (g + 1) * SG_DIM)
            sv = sv_ref[:, sl]
            su = su_ref[:, sl]
            xhat, r = _rms_stats(_gelu(sv))
            sgn = sgn_ref[g:g + 1, :]
            svn = _bf(xhat * sgn)
            sua = _gelu(su)
            wt = _bf(jnp.where(tri, sgw_ref[g], 0.0))
            bias = sgb_ref[:, g:g + 1]
            dw = jnp.zeros((SG_CHUNK, SG_CHUNK), F32)
            db = jnp.zeros((SG_CHUNK, 1), F32)
            dsua, dsvn = [], []
            for c in range(nc):
                rows = slice(c * SG_CHUNK, (c + 1) * SG_CHUNK)
                mixed = _dot(wt, svn[rows, :]) + bias
                dob = dm_ref[rows, hw + g * SG_DIM:hw + (g + 1) * SG_DIM]
                dsua.append(dob * mixed)
                dmixed = dob * sua[rows, :]
                dmb = _bf(dmixed)
                dsvn.append(_dot_tn(wt, dmb))
                dw = dw + _dot_nt(dmb, svn[rows, :])
                db = db + jnp.sum(dmixed, axis=1, keepdims=True)
            dsua = jnp.concatenate(dsua, axis=0) if nc > 1 else dsua[0]
            dsvn = jnp.concatenate(dsvn, axis=0) if nc > 1 else dsvn[0]
            dz_ref[:, hw + g * SG_DIM:hw + (g + 1) * SG_DIM] = _bf(dsua * _gelu_grad(su))
            dz_ref[:, 2 * hw + g * SG_DIM:2 * hw + (g + 1) * SG_DIM] = _bf(_rms_bwd(dsvn, xhat, r, sgn) * _gelu_grad(sv))
            dsgn_rows.append(jnp.sum(dsvn * xhat, axis=0, keepdims=True))
            dsgw_ref[g] += jnp.where(tri, dw, 0.0)
            dsgb = jnp.where(lane == g, db, dsgb)
        dsgn_ref[...] += jnp.concatenate(dsgn_rows, axis=0)
        dsgb_ref[...] += dsgb

    full = lambda shape: pl.BlockSpec(shape, lambda i: (0,) * len(shape))
    return pl.pallas_call(
        body, name=name, grid=(T // tm,),
        in_specs=[pl.BlockSpec((tm, 2 * hw), lambda i: (i, 0)),
                  pl.BlockSpec((tm, hw), lambda i: (i, 0)),
                  pl.BlockSpec((tm, hw), lambda i: (i, 3)),
                  pl.BlockSpec((tm, hw), lambda i: (i, 4)),
                  pl.BlockSpec((tm, hw), lambda i: (i, 5)),
                  full((1, DN_DIM)), full((SG_GROUPS, SG_DIM)), full((SG_GROUPS, SG_CHUNK, SG_CHUNK)),
                  full((SG_CHUNK, 128))],
        out_specs=[pl.BlockSpec((tm, hw), lambda i: (i, 0)),
                   pl.BlockSpec((tm, 3 * hw), lambda i: (i, 0)),
                   full((1, DN_DIM)), full((SG_GROUPS, SG_DIM)), full((SG_GROUPS, SG_CHUNK, SG_CHUNK)),
                   full((SG_CHUNK, 128))],
        out_shape=[jax.ShapeDtypeStruct((T, hw), F32), jax.ShapeDtypeStruct((T, 3 * hw), BF16),
                   jax.ShapeDtypeStruct((1, DN_DIM), F32), jax.ShapeDtypeStruct((SG_GROUPS, SG_DIM), F32),
                   jax.ShapeDtypeStruct((SG_GROUPS, SG_CHUNK, SG_CHUNK), F32),
                   jax.ShapeDtypeStruct((SG_CHUNK, 128), F32)],
        compiler_params=_params(("arbitrary",)),
    )(dmix, o, proj, proj, proj, dn_norm, sg_norm, sg_w, sg_bt)


def _window_sums(h, sign):
    sums, s, w = {}, h, 1
    while w < POOL_WINDOWS[-1]:
        s = s + _shift_rows(s, sign * w)
        w *= 2
        sums[w] = s
    return sums


def _pool_counts(t_global):
    return [jnp.minimum(t_global + 1, win).astype(F32) for win in POOL_WINDOWS]


def _pooled_groups(ext_h, row0, tm):
    sums = _window_sums(ext_h, 1)
    t_global = row0 + lax.broadcasted_iota(jnp.int32, (tm, 1), 0)
    counts = _pool_counts(t_global)
    out = []
    for gi, win in enumerate(POOL_WINDOWS):
        cols = slice(gi * POOL_DIM, (gi + 1) * POOL_DIM)
        out.append(sums[win][POOL_HALO:, cols] / counts[gi] - ext_h[POOL_HALO:, cols])
    return out


def _pool_fwd(x, nw, pool_w, pool_scale, layer, name):
    T, D = x.shape
    tm = _tile(T, 256)
    hb = tm // POOL_HALO

    def body(x_ref, xp_ref, n_ref, w_ref, s_ref, xo_ref):
        i = pl.program_id(0)
        prev = jnp.where(i == 0, 0.0, xp_ref[...])
        ext = jnp.concatenate([prev, x_ref[...]], axis=0)
        xhat, _ = _rms_stats(ext)
        pooled = _pooled_groups(xhat * n_ref[...], i * tm, tm)
        for gi in range(len(POOL_WINDOWS)):
            cols = slice(gi * POOL_DIM, (gi + 1) * POOL_DIM)
            xo_ref[:, cols] = x_ref[:, cols] + _dot(_bf(pooled[gi]), w_ref[gi]) * s_ref[:, cols]

    return pl.pallas_call(
        body, name=name, grid=(T // tm,),
        in_specs=[pl.BlockSpec((tm, D), lambda i: (i, 0)),
                  pl.BlockSpec((POOL_HALO, D), lambda i: (jnp.maximum(i * hb - 1, 0), 0)),
                  pl.BlockSpec((None, 1, D), lambda i: (layer, 0, 0)),
                  pl.BlockSpec(pool_w.shape, lambda i: (0, 0, 0)),
                  pl.BlockSpec((1, D), lambda i: (0, 0))],
        out_specs=pl.BlockSpec((tm, D), lambda i: (i, 0)),
        out_shape=jax.ShapeDtypeStruct((T, D), F32),
        compiler_params=_params(("parallel",)),
    )(x, x, nw, pool_w, pool_scale)


def _pool_bwd(dxo, x, nw, pool_w, pool_scale, layer, name):
    T, D = x.shape
    tm = _tile(T, 256)
    hb = tm // POOL_HALO
    nt = T // tm
    ng = len(POOL_WINDOWS)

    def body(dxo_ref, dxn_ref, x_ref, xp_ref, n_ref, w_ref, s_ref, dx_ref, dw_ref, ds_ref, dn_ref):
        i = pl.program_id(0)

        @pl.when(i == 0)
        def _():
            dw_ref[...] = jnp.zeros_like(dw_ref)
            ds_ref[...] = jnp.zeros_like(ds_ref)
            dn_ref[...] = jnp.zeros_like(dn_ref)

        prev = jnp.where(i == 0, 0.0, xp_ref[...])
        ext = jnp.concatenate([prev, x_ref[...]], axis=0)
        xhat_ext, r_ext = _rms_stats(ext)
        nv = n_ref[...]
        pooled = _pooled_groups(xhat_ext * nv, i * tm, tm)
        dxo = dxo_ref[...]
        scale = s_ref[...]
        dout_ext = jnp.concatenate([dxo, jnp.where(i == nt - 1, 0.0, dxn_ref[...])], axis=0) * scale
        t_ext = i * tm + lax.broadcasted_iota(jnp.int32, (tm + POOL_HALO, 1), 0)
        counts = _pool_counts(t_ext)
        dh_cols, ds_cols = [], []
        for gi, win in enumerate(POOL_WINDOWS):
            cols = slice(gi * POOL_DIM, (gi + 1) * POOL_DIM)
            wg = w_ref[gi]
            pb = _bf(pooled[gi])
            doutb = _bf(dout_ext[:, cols])
            dpooled = _dot_nt(doutb, wg)
            ahead = _window_sums(dpooled / counts[gi], -1)[win]
            dh_cols.append(ahead[:tm, :] - dpooled[:tm, :])
            dw_ref[gi] += _dot_tn(pb, doutb[:tm, :])
            ds_cols.append(jnp.sum(dxo[:, cols] * _dot(pb, wg), axis=0, keepdims=True))
        dh = jnp.concatenate(dh_cols, axis=1)
        xhat, r = xhat_ext[POOL_HALO:, :], r_ext[POOL_HALO:, :]
        dx_ref[...] = dxo + _rms_bwd(dh, xhat, r, nv)
        dn_ref[...] += jnp.sum(dh * xhat, axis=0, keepdims=True)
        ds_ref[...] += jnp.concatenate(ds_cols, axis=1)

    last_halo = T // POOL_HALO - 1
    return pl.pallas_call(
        body, name=name, grid=(nt,),
        in_specs=[pl.BlockSpec((tm, D), lambda i: (i, 0)),
                  pl.BlockSpec((POOL_HALO, D), lambda i: (jnp.minimum((i + 1) * hb, last_halo), 0)),
                  pl.BlockSpec((tm, D), lambda i: (i, 0)),
                  pl.BlockSpec((POOL_HALO, D), lambda i: (jnp.maximum(i * hb - 1, 0), 0)),
                  pl.BlockSpec((None, 1, D), lambda i: (layer, 0, 0)),
                  pl.BlockSpec(pool_w.shape, lambda i: (0, 0, 0)),
                  pl.BlockSpec((1, D), lambda i: (0, 0))],
        out_specs=[pl.BlockSpec((tm, D), lambda i: (i, 0)),
                   pl.BlockSpec((ng, POOL_DIM, POOL_DIM), lambda i: (0, 0, 0)),
                   pl.BlockSpec((1, D), lambda i: (0, 0)),
                   pl.BlockSpec((1, D), lambda i: (0, 0))],
        out_shape=[jax.ShapeDtypeStruct((T, D), F32), jax.ShapeDtypeStruct((ng, POOL_DIM, POOL_DIM), F32),
                   jax.ShapeDtypeStruct((1, D), F32), jax.ShapeDtypeStruct((1, D), F32)],
        compiler_params=_params(("arbitrary",)),
    )(dxo, dxo, x, x, nw, pool_w, pool_scale)


def _loss_head(x, target, fn, name):
    T, D = x.shape
    tm = _tile(T, 512)

    def body(x_ref, t_ref, n_ref, loss_ref, dx_ref, dn_ref):
        @pl.when(pl.program_id(0) == 0)
        def _():
            loss_ref[...] = jnp.zeros_like(loss_ref)
            dn_ref[...] = jnp.zeros_like(dn_ref)

        xhat, r = _rms_stats(x_ref[...])
        nv = n_ref[...]
        err = xhat * nv - t_ref[...]
        part = jnp.sum(jnp.sum(err * err, axis=1, keepdims=True), axis=0, keepdims=True)
        loss_ref[...] += 0.5 * part / D
        dy = err / D
        dx_ref[...] = _rms_bwd(dy, xhat, r, nv)
        dn_ref[...] += jnp.sum(dy * xhat, axis=0, keepdims=True)

    row = pl.BlockSpec((tm, D), lambda i: (i, 0))
    return pl.pallas_call(
        body, name=name, grid=(T // tm,),
        in_specs=[row, row, pl.BlockSpec((1, D), lambda i: (0, 0))],
        out_specs=[pl.BlockSpec((1, 1), lambda i: (0, 0)), row, pl.BlockSpec((1, D), lambda i: (0, 0))],
        out_shape=[jax.ShapeDtypeStruct((1, 1), F32), jax.ShapeDtypeStruct((T, D), F32),
                   jax.ShapeDtypeStruct((1, D), F32)],
        compiler_params=_params(("arbitrary",)),
    )(x, target, fn)


def _adamw(w, g, m, v, name):
    R, C = w.shape
    br = R
    for cand in (512, 256, 128, 64, 32, 16, 8):
        if R % cand == 0 and cand * C * 4 <= 2 * 1024 * 1024:
            br = cand
            break

    def body(w_ref, g_ref, m_ref, v_ref, d_ref, mo_ref, vo_ref):
        gv = g_ref[...]
        m_new = ADAM_B1 * m_ref[...] + (1.0 - ADAM_B1) * gv
        v_new = ADAM_B2 * v_ref[...] + (1.0 - ADAM_B2) * (gv * gv)
        m_hat = m_new / (1.0 - ADAM_B1 ** ADAM_STEP)
        v_hat = v_new / (1.0 - ADAM_B2 ** ADAM_STEP)
        d_ref[...] = -ADAM_LR * (m_hat / (jnp.sqrt(v_hat) + ADAM_EPS) + ADAM_WD * w_ref[...])
        mo_ref[...] = m_new
        vo_ref[...] = v_new

    blk = pl.BlockSpec((br, C), lambda i: (i, 0))
    return pl.pallas_call(
        body, name=name, grid=(R // br,), in_specs=[blk] * 4, out_specs=[blk] * 3,
        out_shape=[jax.ShapeDtypeStruct((R, C), F32)] * 3,
        compiler_params=_params(("parallel",)),
    )(w, g, m, v)


def _mesh_pos():
    return lax.axis_index("x"), lax.axis_index("y"), lax.axis_index("c")


def _other_chips(x, y):
    return [(1 - x, y), (x, 1 - y), (1 - x, 1 - y)]


def _all_gather_chips(shards, name):
    n = len(shards)

    def body(*refs):
        ins, outs = refs[:n], refs[n:2 * n]
        send_sems, recv_sems, local_sems = refs[2 * n:]
        x, y, c = _mesh_pos()
        me = 2 * x + y
        started = []
        for a in range(n):
            loc = pltpu.make_async_copy(ins[a], outs[a].at[me], local_sems.at[a])
            loc.start()
            started.append(loc)
            for k, (px, py) in enumerate(_other_chips(x, y)):
                cp = pltpu.make_async_remote_copy(ins[a], outs[a].at[me], send_sems.at[a, k], recv_sems.at[a, k],
                                                  device_id=(px, py, c), device_id_type=MESH)
                cp.start()
                started.append(cp)
        for a in range(n):
            for k, (px, py) in enumerate(_other_chips(x, y)):
                pltpu.make_async_remote_copy(ins[a], outs[a].at[2 * px + py], send_sems.at[a, k], recv_sems.at[a, k],
                                             device_id=(px, py, c), device_id_type=MESH).wait_recv()
        for a in range(n):
            started[a * 4].wait()
            for k in range(3):
                started[a * 4 + 1 + k].wait_send()

    return pl.pallas_call(
        body, name=name, in_specs=[ANY] * n, out_specs=[ANY] * n,
        out_shape=[jax.ShapeDtypeStruct((N_CHIPS,) + s.shape, s.dtype) for s in shards],
        scratch_shapes=[pltpu.SemaphoreType.DMA((n, 3)), pltpu.SemaphoreType.DMA((n, 3)), pltpu.SemaphoreType.DMA((n,))],
        compiler_params=pltpu.CompilerParams(has_side_effects=True),
    )(*shards)


def _ffn_weight_grads(hb, dg, du, a, dyb, tag):
    dwg = _matmul_tn(hb, dg, D_MODEL, FF_CHUNK, f"{tag}_dw_gate", stack_n=True)
    dwu = _matmul_tn(hb, du, D_MODEL, FF_CHUNK, f"{tag}_dw_up", stack_n=True)
    dwo = _matmul_tn(a, dyb, FF_CHUNK, D_MODEL, f"{tag}_dw_out")
    return jnp.concatenate([dwg, dwu], axis=0), dwo.reshape(N_CHIPS, D_FF // N_CHIPS, D_MODEL)


def _local_step(x, target, w):
    g = {}
    acts = []
    ffn_w = {1: (w["n1"], w["win1"], w["wout1"]), 2: (w["n2"], w["win2"], w["wout2"])}

    def ffn(xin, which, layer):
        nw, win, wout = ffn_w[which]
        xo, gv, uv, hb = _ffn_fwd(xin, nw, win, wout, layer, f"ffn{which}_l{layer}_fwd")
        acts.append((xin, gv, uv, hb))
        return xo

    x1 = ffn(x, 1, 0)
    hb_mix = _rms_fwd_call(x1, w["nmix"], 0, "ab_norm_fwd")
    proj = _matmul(hb_mix, w["wp"], "ab_in_proj")
    qkv = _conv_fwd(proj, w["conv_w"], "dn_conv_fwd")
    o, sall = _dn_fwd(qkv, proj, w["hp"], "dn_fwd")
    mix = _mix_fwd(o, proj, w["dn_norm"], w["sg_norm"], w["sg_w"], w["sg_bt"], "ab_gate_fwd")
    x2 = _matmul(mix, w["wo"], "ab_out_proj", res=x1)
    x3 = ffn(x2, 2, 0)
    x4 = ffn(x3, 1, 1)
    x5 = _pool_fwd(x4, w["nmix"], w["pool_w"], w["pool_scale"], 1, "pool_fwd")
    x6 = ffn(x5, 2, 1)
    loss, dx, g["fn"] = _loss_head(x6, target, w["fn"], "loss_head")

    dn = {1: [None, None], 2: [None, None]}
    dwin = {1: [None, None], 2: [None, None]}
    dwout = {1: [None, None], 2: [None, None]}

    def ffn_back(dxo, which, layer, saved):
        nw, win, wout = ffn_w[which]
        xin, gv, uv, hb = saved
        tag = f"ffn{which}_l{layer}"
        dxi, dg, du, a, dyb, dnw = _ffn_bwd(dxo, xin, nw, gv, uv, win, wout, layer, f"{tag}_bwd")
        dn[which][layer] = dnw
        dwin[which][layer], dwout[which][layer] = _ffn_weight_grads(hb, dg, du, a, dyb, tag)
        return dxi

    dx = ffn_back(dx, 2, 1, acts[3])
    dx, g["pool_w"], g["pool_scale"], dnmix1 = _pool_bwd(dx, x4, w["nmix"], w["pool_w"], w["pool_scale"], 1, "pool_bwd")
    dx = ffn_back(dx, 1, 1, acts[2])
    dx2 = ffn_back(dx, 2, 0, acts[1])
    dmix = _matmul(dx2, w["wo"], "ab_out_proj_bwd", trans_b=True)
    g["wo"] = _matmul_tn(mix, dx2, D_MODEL, D_MODEL, "ab_out_proj_dw")
    do, dzuv, g["dn_norm"], g["sg_norm"], g["sg_w"], g["sg_bt"] = _mix_bwd(
        dmix, o, proj, w["dn_norm"], w["sg_norm"], w["sg_w"], w["sg_bt"], "ab_gate_bwd")
    dq, dk, dv, dba, g["hp"] = _dn_bwd(qkv, proj, w["hp"], sall, do, "dn_bwd")
    dqkv, g["conv_w"] = _conv_bwd(jnp.concatenate([dq, dk, dv], axis=1), proj, w["conv_w"], "dn_conv_bwd")
    dproj = jnp.concatenate([dqkv, dzuv, dba.astype(BF16)], axis=1)
    dh = _matmul(dproj, w["wp"], "ab_in_proj_bwd", trans_b=True)
    g["wp"] = _matmul_tn(hb_mix, dproj, D_MODEL, 640, "ab_in_proj_dw")
    dx1, dnmix0 = _rms_bwd_call(dh, x1, w["nmix"], dx2, 0, "ab_norm_bwd")
    dx0 = ffn_back(dx1, 1, 0, acts[0])

    g["n1"] = jnp.concatenate(dn[1], axis=0)
    g["n2"] = jnp.concatenate(dn[2], axis=0)
    g["nmix"] = jnp.concatenate([dnmix0, dnmix1], axis=0)
    for which in (1, 2):
        g[f"win{which}"] = jnp.stack(dwin[which], axis=1)
        g[f"wout{which}"] = jnp.stack(dwout[which], axis=1)
    return loss, dx0, g


SHARDED = ("ffn1_w_in", "ffn1_w_out", "ffn2_w_in", "ffn2_w_out", "ab_w_in", "ab_w_out", "pool_w", "dn_conv_w", "pool_scale")
REPLICATED = ("ffn_norm1", "mix_norm", "ffn_norm2", "dn_a_log", "dn_dt_bias", "dn_out_norm", "sg_norm", "sg_w", "sg_b", "final_norm")
QKVZ = 4 * DN_HEADS * DN_DIM
N_GATES = 2 * DN_HEADS
IN_PROJ = QKVZ + N_GATES + 2 * SG_GROUPS * SG_DIM


def _kernel_layouts(gathered, rep):
    per_layer = lambda a: a.reshape(a.shape[0], 1, D_MODEL)
    w = {"n1": per_layer(rep["ffn_norm1"]), "nmix": per_layer(rep["mix_norm"]), "n2": per_layer(rep["ffn_norm2"]),
         "win1": gathered["ffn1_w_in"], "wout1": gathered["ffn1_w_out"],
         "win2": gathered["ffn2_w_in"], "wout2": gathered["ffn2_w_out"]}
    ab_in = jnp.transpose(gathered["ab_w_in"][:, 0], (1, 0, 2)).reshape(D_MODEL, IN_PROJ)
    w["wp"] = jnp.concatenate([ab_in[:, :QKVZ], ab_in[:, QKVZ + N_GATES:], ab_in[:, QKVZ:QKVZ + N_GATES],
                               jnp.zeros((D_MODEL, PROJ_W - IN_PROJ), ab_in.dtype)], axis=1)
    w["conv_w"] = jnp.transpose(gathered["dn_conv_w"][:, 0], (1, 0, 2)).reshape(DN_CONV, 3 * DN_HEADS * DN_DIM)
    hp = jnp.zeros((8, 128), F32)
    hp = hp.at[0, DN_HEADS:N_GATES].set(rep["dn_a_log"][0]).at[1, DN_HEADS:N_GATES].set(rep["dn_dt_bias"][0])
    w["hp"] = hp
    w["dn_norm"] = rep["dn_out_norm"]
    w["sg_norm"] = rep["sg_norm"][0]
    w["sg_w"] = rep["sg_w"][0]
    w["sg_bt"] = jnp.zeros((SG_CHUNK, 128), F32).at[:, :SG_GROUPS].set(rep["sg_b"][0].T)
    w["wo"] = gathered["ab_w_out"][:, 0].reshape(D_MODEL, D_MODEL)
    w["pool_w"] = jnp.transpose(gathered["pool_w"][:, 0], (1, 0, 2, 3)).reshape(len(POOL_WINDOWS), POOL_DIM, POOL_DIM)
    w["pool_scale"] = gathered["pool_scale"].reshape(1, D_MODEL)
    w["fn"] = rep["final_norm"].reshape(1, D_MODEL)
    return w


def _grads_by_chip(g):
    wp = g["wp"]
    ab_in = jnp.concatenate([wp[:, :QKVZ], wp[:, IN_PROJ - N_GATES:IN_PROJ], wp[:, QKVZ:IN_PROJ - N_GATES]], axis=1)
    nw = len(POOL_WINDOWS)
    sharded = {
        "ffn1_w_in": g["win1"], "ffn1_w_out": g["wout1"], "ffn2_w_in": g["win2"], "ffn2_w_out": g["wout2"],
        "ab_w_in": jnp.transpose(ab_in.reshape(D_MODEL, N_CHIPS, IN_PROJ // N_CHIPS), (1, 0, 2))[:, None],
        "ab_w_out": g["wo"].reshape(N_CHIPS, 1, D_MODEL // N_CHIPS, D_MODEL),
        "pool_w": jnp.transpose(g["pool_w"].reshape(nw, N_CHIPS, POOL_DIM // N_CHIPS, POOL_DIM), (1, 0, 2, 3))[:, None],
        "dn_conv_w": jnp.transpose(g["conv_w"].reshape(DN_CONV, N_CHIPS, -1), (1, 0, 2))[:, None],
        "pool_scale": g["pool_scale"].reshape(N_CHIPS, 1, D_MODEL // N_CHIPS),
    }
    rep = {
        "ffn_norm1": g["n1"], "mix_norm": g["nmix"], "ffn_norm2": g["n2"],
        "dn_a_log": g["hp"][0:1, DN_HEADS:N_GATES], "dn_dt_bias": g["hp"][1:2, DN_HEADS:N_GATES],
        "dn_out_norm": g["dn_norm"], "sg_norm": g["sg_norm"][None], "sg_w": g["sg_w"][None],
        "sg_b": g["sg_bt"][:, :SG_GROUPS].T[None], "final_norm": g["fn"].reshape(D_MODEL),
    }
    return sharded, rep


def _pack_rows(n_elems):
    rows = -(-n_elems // PACK_LANES)
    return -(-rows // PACK_ROW_BLOCK) * PACK_ROW_BLOCK


def _pack_by_half(sharded):
    parts = [sharded[n].reshape(N_CHIPS, 2, -1) for n in SHARDED]
    flat = jnp.concatenate(parts, axis=2)
    rows = _pack_rows(flat.shape[2])
    flat = jnp.pad(flat, ((0, 0), (0, 0), (0, rows * PACK_LANES - flat.shape[2])))
    return jnp.transpose(flat, (1, 0, 2)).reshape(2, N_CHIPS, rows, PACK_LANES)


def _unpack_halves(full, shapes):
    flat = full.reshape(2, -1)
    out, off = {}, 0
    for n in SHARDED:
        half = 1
        for d in shapes[n]:
            half *= d
        half //= 2
        out[n] = flat[:, off:off + half].reshape(shapes[n])
        off += half
    return out


def _swap_with_sibling(pack, name):
    _, nchip, rows, lanes = pack.shape

    def body(pack_ref, recv_ref, send_sem, recv_sem):
        x, y, c = _mesh_pos()
        cp = pltpu.make_async_remote_copy(pack_ref.at[1 - c], recv_ref, send_sem, recv_sem,
                                          device_id=(x, y, 1 - c), device_id_type=MESH)
        cp.start()
        cp.wait()

    return pl.pallas_call(
        body, name=name, in_specs=[ANY], out_specs=ANY,
        out_shape=jax.ShapeDtypeStruct((nchip, rows, lanes), pack.dtype),
        scratch_shapes=[pltpu.SemaphoreType.DMA, pltpu.SemaphoreType.DMA],
        compiler_params=pltpu.CompilerParams(has_side_effects=True),
    )(pack)


def _add_pair(pack, recv, core, name):
    _, nchip, rows, lanes = pack.shape

    def body(c_ref, a_ref, b_ref, o32_ref, o16_ref):
        s = a_ref[...] + b_ref[...]
        o32_ref[...] = s
        o16_ref[...] = _bf(s)

    blk = pl.BlockSpec((None, PACK_ROW_BLOCK, lanes), lambda p, i, c: (p, i, 0))
    return pl.pallas_call(
        body, name=name,
        grid_spec=pltpu.PrefetchScalarGridSpec(
            num_scalar_prefetch=1, grid=(nchip, rows // PACK_ROW_BLOCK),
            in_specs=[pl.BlockSpec((None, None, PACK_ROW_BLOCK, lanes), lambda p, i, c: (c[0], p, i, 0)), blk],
            out_specs=[blk, blk]),
        out_shape=[jax.ShapeDtypeStruct((nchip, rows, lanes), F32), jax.ShapeDtypeStruct((nchip, rows, lanes), BF16)],
        compiler_params=_params(("parallel", "parallel")),
    )(core, pack, recv)


def _scatter_to_chips(part16, name):
    nchip, rows, lanes = part16.shape

    def body(src_ref, recv_ref, send_sems, recv_sems):
        x, y, c = _mesh_pos()
        copies = []
        for k, (px, py) in enumerate(_other_chips(x, y)):
            cp = pltpu.make_async_remote_copy(src_ref.at[2 * px + py], recv_ref.at[k], send_sems.at[k], recv_sems.at[k],
                                              device_id=(px, py, c), device_id_type=MESH)
            cp.start()
            copies.append(cp)
        for cp in copies:
            cp.wait()

    return pl.pallas_call(
        body, name=name, in_specs=[ANY], out_specs=ANY,
        out_shape=jax.ShapeDtypeStruct((nchip - 1, rows, lanes), part16.dtype),
        scratch_shapes=[pltpu.SemaphoreType.DMA((nchip - 1,)), pltpu.SemaphoreType.DMA((nchip - 1,))],
        compiler_params=pltpu.CompilerParams(has_side_effects=True),
    )(part16)


def _sum_chips(part32, recv16, chip, name):
    nchip, rows, lanes = part32.shape

    def body(p_ref, own_ref, r_ref, o_ref):
        s = own_ref[...]
        for k in range(nchip - 1):
            s = s + r_ref[k].astype(F32)
        o_ref[...] = s

    return pl.pallas_call(
        body, name=name,
        grid_spec=pltpu.PrefetchScalarGridSpec(
            num_scalar_prefetch=1, grid=(rows // PACK_ROW_BLOCK,),
            in_specs=[pl.BlockSpec((None, PACK_ROW_BLOCK, lanes), lambda i, p: (p[0], i, 0)),
                      pl.BlockSpec((nchip - 1, PACK_ROW_BLOCK, lanes), lambda i, p: (0, i, 0))],
            out_specs=pl.BlockSpec((PACK_ROW_BLOCK, lanes), lambda i, p: (i, 0))),
        out_shape=jax.ShapeDtypeStruct((rows, lanes), F32),
        compiler_params=_params(("parallel",)),
    )(chip, part32, recv16)


def _share_with_sibling(half, name):
    rows, lanes = half.shape

    def body(h_ref, full_ref, send_sem, recv_sem, local_sem):
        x, y, c = _mesh_pos()
        loc = pltpu.make_async_copy(h_ref, full_ref.at[c], local_sem)
        loc.start()
        cp = pltpu.make_async_remote_copy(h_ref, full_ref.at[c], send_sem, recv_sem,
                                          device_id=(x, y, 1 - c), device_id_type=MESH)
        cp.start()
        pltpu.make_async_remote_copy(h_ref, full_ref.at[1 - c], send_sem, recv_sem,
                                     device_id=(x, y, 1 - c), device_id_type=MESH).wait_recv()
        cp.wait_send()
        loc.wait()

    return pl.pallas_call(
        body, name=name, in_specs=[ANY], out_specs=ANY,
        out_shape=jax.ShapeDtypeStruct((2, rows, lanes), half.dtype),
        scratch_shapes=[pltpu.SemaphoreType.DMA, pltpu.SemaphoreType.DMA, pltpu.SemaphoreType.DMA],
        compiler_params=pltpu.CompilerParams(has_side_effects=True),
    )(half)


def _reduce_sharded(sharded, shapes):
    x, y, c = _mesh_pos()
    core = jnp.reshape(c, (1,)).astype(jnp.int32)
    chip = jnp.reshape(2 * x + y, (1,)).astype(jnp.int32)
    pack = _pack_by_half(sharded)
    recv = _swap_with_sibling(pack, "grad_pair_swap")
    part32, part16 = _add_pair(pack, recv, core, "grad_pair_add")
    recv16 = _scatter_to_chips(part16, "grad_chip_scatter")
    half = _sum_chips(part32, recv16, chip, "grad_chip_sum")
    full = _share_with_sibling(half, "grad_pair_share")
    return _unpack_halves(full, shapes)


def _pack_small(vals):
    parts = []
    for n in REPLICATED:
        flat = vals[n].reshape(-1)
        rows = -(-flat.shape[0] // 128)
        rows = -(-rows // 8) * 8
        parts.append(jnp.pad(flat, (0, rows * 128 - flat.shape[0])).reshape(rows, 128))
    return jnp.concatenate(parts, axis=0)


def _unpack_small(pack, like):
    out, off = {}, 0
    for n in REPLICATED:
        size = like[n].size
        rows = -(-size // 128)
        rows = -(-rows // 8) * 8
        out[n] = pack[off:off + rows].reshape(-1)[:size].reshape(like[n].shape)
        off += rows
    return out


def _all_to_all_small(pack, name):
    rows, lanes = pack.shape
    flips = [(dx, dy, dc) for dx in (0, 1) for dy in (0, 1) for dc in (0, 1)][1:]

    def body(src_ref, out_ref, send_sems, recv_sems, local_sem):
        x, y, c = _mesh_pos()
        me = 4 * x + 2 * y + c
        loc = pltpu.make_async_copy(src_ref, out_ref.at[me], local_sem)
        loc.start()
        copies = []
        for k, (dx, dy, dc) in enumerate(flips):
            peer = (x ^ dx, y ^ dy, c ^ dc)
            cp = pltpu.make_async_remote_copy(src_ref, out_ref.at[me], send_sems.at[k], recv_sems.at[k],
                                              device_id=peer, device_id_type=MESH)
            cp.start()
            copies.append(cp)
        for k, (dx, dy, dc) in enumerate(flips):
            peer = (x ^ dx, y ^ dy, c ^ dc)
            pltpu.make_async_remote_copy(src_ref, out_ref.at[4 * peer[0] + 2 * peer[1] + peer[2]], send_sems.at[k],
                                         recv_sems.at[k], device_id=peer, device_id_type=MESH).wait_recv()
        for cp in copies:
            cp.wait_send()
        loc.wait()

    return pl.pallas_call(
        body, name=name, in_specs=[ANY], out_specs=ANY,
        out_shape=jax.ShapeDtypeStruct((8, rows, lanes), pack.dtype),
        scratch_shapes=[pltpu.SemaphoreType.DMA((7,)), pltpu.SemaphoreType.DMA((7,)), pltpu.SemaphoreType.DMA],
        compiler_params=pltpu.CompilerParams(has_side_effects=True),
    )(pack)


def _sum_devices(stack, name):
    ndev, rows, lanes = stack.shape

    def body(s_ref, o_ref):
        s = s_ref[0]
        for d in range(1, ndev):
            s = s + s_ref[d]
        o_ref[...] = s

    return pl.pallas_call(
        body, name=name, grid=(1,),
        in_specs=[pl.BlockSpec((ndev, rows, lanes), lambda i: (0, 0, 0))],
        out_specs=pl.BlockSpec((rows, lanes), lambda i: (0, 0)),
        out_shape=jax.ShapeDtypeStruct((rows, lanes), F32),
    )(stack)


WEIGHT_ORDER = ("ffn_norm1", "ffn1_w_in", "ffn1_w_out", "mix_norm", "ffn_norm2", "ffn2_w_in", "ffn2_w_out", "ab_w_in",
                "dn_conv_w", "dn_a_log", "dn_dt_bias", "dn_out_norm", "sg_norm", "sg_w", "sg_b", "ab_w_out", "pool_w",
                "pool_scale", "final_norm")
MATRICES = ("ffn1_w_in", "ffn1_w_out", "ffn2_w_in", "ffn2_w_out", "ab_w_in", "ab_w_out", "pool_w")


def _as_2d(a):
    return a.reshape(-1, a.shape[-1])


def kernel(x, ffn_norm1, ffn1_w_in, ffn1_w_out, mix_norm, ffn_norm2, ffn2_w_in, ffn2_w_out, ab_w_in, dn_conv_w, dn_a_log, dn_dt_bias, dn_out_norm, sg_norm, sg_w, sg_b, ab_w_out, pool_w, pool_scale, final_norm, loss_target, m_ffn_norm1, m_ffn1_w_in, m_ffn1_w_out, m_mix_norm, m_ffn_norm2, m_ffn2_w_in, m_ffn2_w_out, m_ab_w_in, m_dn_conv_w, m_dn_a_log, m_dn_dt_bias, m_dn_out_norm, m_sg_norm, m_sg_w, m_sg_b, m_ab_w_out, m_pool_w, m_pool_scale, m_final_norm, v_ffn_norm1, v_ffn1_w_in, v_ffn1_w_out, v_mix_norm, v_ffn_norm2, v_ffn2_w_in, v_ffn2_w_out, v_ab_w_in, v_dn_conv_w, v_dn_a_log, v_dn_dt_bias, v_dn_out_norm, v_sg_norm, v_sg_w, v_sg_b, v_ab_w_out, v_pool_w, v_pool_scale, v_final_norm):
    given = dict(locals())
    wts = {n: given[n] for n in WEIGHT_ORDER}
    mom_m = {n: given["m_" + n] for n in WEIGHT_ORDER}
    mom_v = {n: given["v_" + n] for n in WEIGHT_ORDER}

    shards = [wts[n].astype(BF16) if n in MATRICES else wts[n] for n in SHARDED]
    gathered = dict(zip(SHARDED, _all_gather_chips(shards, "weight_all_gather")))
    rep = {n: wts[n] for n in REPLICATED}
    w = _kernel_layouts(gathered, rep)

    loss, dx, g = _local_step(x[0], loss_target[0], w)
    g_sharded, g_rep = _grads_by_chip(g)

    grads = _reduce_sharded(g_sharded, {n: wts[n].shape for n in SHARDED})
    small = _sum_devices(_all_to_all_small(_pack_small(g_rep), "grad_small_exchange"), "grad_small_sum")
    grads.update(_unpack_small(small, rep))

    delta, new_m, new_v = {}, {}, {}
    for n in SHARDED:
        d, m1, v1 = _adamw(_as_2d(wts[n]), _as_2d(grads[n]), _as_2d(mom_m[n]), _as_2d(mom_v[n]), f"adamw_{n}")
        delta[n], new_m[n], new_v[n] = (t.reshape(wts[n].shape) for t in (d, m1, v1))
    d, m1, v1 = _adamw(_pack_small(rep), small, _pack_small({n: mom_m[n] for n in REPLICATED}),
                       _pack_small({n: mom_v[n] for n in REPLICATED}), "adamw_replicated")
    for tgt, packed in ((delta, d), (new_m, m1), (new_v, v1)):
        tgt.update(_unpack_small(packed, rep))

    total = lax.psum(loss[0, 0], ("x", "y", "c"))
    outs = [total, dx[None]]
    for group in (grads, delta, new_m, new_v):
        outs.extend(group[n] for n in WEIGHT_ORDER)
    return tuple(outs)
```

```python
import functools

import jax
import jax.numpy as jnp
from jax import lax
from jax.experimental import pallas as pl
from jax.experimental.pallas import tpu as pltpu

F32, BF16 = jnp.float32, jnp.bfloat16
NORM_EPS = 1e-6
D_MODEL = 1024
D_FF = 2816
N_CHIPS = 4
FF_CHUNK = 2 * D_FF // N_CHIPS
DN_HEADS, DN_DIM, DN_CHUNK, DN_CONV = 4, 128, 64, 4
SG_GROUPS, SG_DIM, SG_CHUNK = 4, 128, 128
POOL_WINDOWS = (2, 4, 8, 16)
POOL_DIM = 256
POOL_HALO = 16
CONV_HALO = 8
PROJ_W = 3200
BA_BLOCK = 3072 // 128
ADAM_LR, ADAM_B1, ADAM_B2, ADAM_EPS, ADAM_WD, ADAM_STEP = 0.001, 0.9, 0.999, 1e-08, 0.01, 10
VMEM_BIG = 52 * 1024 * 1024
PACK_LANES = 1024
PACK_ROW_BLOCK = 256
MESH = pl.DeviceIdType.MESH
HI = lax.Precision.HIGHEST
ANY = pl.BlockSpec(memory_space=pl.ANY)


def _params(sem=None, vmem=None):
    return pltpu.CompilerParams(dimension_semantics=sem, vmem_limit_bytes=vmem)


def _dot(a, b):
    return jnp.dot(a, b, preferred_element_type=F32)


def _dot_nt(a, b):
    return lax.dot_general(a, b, (((1,), (1,)), ((), ())), preferred_element_type=F32)


def _dot_tn(a, b):
    return lax.dot_general(a, b, (((0,), (0,)), ((), ())), preferred_element_type=F32)


def _dot_hi(a, b):
    return jnp.dot(a, b, preferred_element_type=F32, precision=HI)


def _bf(a):
    return a.astype(BF16)


def _rms_stats(x):
    r = lax.rsqrt(jnp.mean(x * x, axis=-1, keepdims=True) + NORM_EPS)
    return x * r, r


def _rms_bwd(dh, xhat, r, w):
    dhn = dh * w
    return r * (dhn - xhat * jnp.mean(dhn * xhat, axis=-1, keepdims=True))


def _sigmoid(x):
    return jax.nn.sigmoid(x)


def _silu_grad(x, s):
    return s * (1.0 + x * (1.0 - s))


def _gelu(x):
    return 0.5 * x * (1.0 + lax.erf(x * 0.7071067811865476))


def _gelu_grad(x):
    return 0.5 * (1.0 + lax.erf(x * 0.7071067811865476)) + x * jnp.exp(-0.5 * x * x) * 0.3989422804014327


def _softplus(x):
    return jnp.maximum(x, 0.0) + jnp.log(1.0 + jnp.exp(-jnp.abs(x)))


def _tile(n, pref):
    t = min(n, pref)
    assert n % t == 0, (n, t)
    return t


def _ffn_fwd(x, nw, win, wout, layer, name):
    T, D = x.shape
    tm = _tile(T, 512)
    nj = N_CHIPS // 2

    def body(x_ref, n_ref, wg_ref, wu_ref, wo_ref, xo_ref, g_ref, u_ref, hb_ref, h_s, acc_s):
        j = pl.program_id(1)

        @pl.when(j == 0)
        def _():
            xhat, _ = _rms_stats(x_ref[...])
            h = _bf(xhat * n_ref[...])
            h_s[...] = h
            hb_ref[...] = h
            acc_s[...] = jnp.zeros_like(acc_s)

        h = h_s[...]
        g = _dot(h, wg_ref[...])
        u = _dot(h, wu_ref[...])
        a = _bf(g * _sigmoid(g) * u)
        acc_s[...] += _dot(a, wo_ref[...].reshape(FF_CHUNK, D))
        g_ref[...] = _bf(g)
        u_ref[...] = _bf(u)

        @pl.when(j == nj - 1)
        def _():
            xo_ref[...] = x_ref[...] + 0.5 * acc_s[...]

    return pl.pallas_call(
        body, name=name, grid=(T // tm, nj),
        in_specs=[pl.BlockSpec((tm, D), lambda i, j: (i, 0)),
                  pl.BlockSpec((None, 1, D), lambda i, j: (layer, 0, 0)),
                  pl.BlockSpec((None, None, D, FF_CHUNK), lambda i, j: (j, layer, 0, 0)),
                  pl.BlockSpec((None, None, D, FF_CHUNK), lambda i, j: (nj + j, layer, 0, 0)),
                  pl.BlockSpec((2, None, FF_CHUNK // 2, D), lambda i, j: (j, layer, 0, 0))],
        out_specs=[pl.BlockSpec((tm, D), lambda i, j: (i, 0)),
                   pl.BlockSpec((tm, FF_CHUNK), lambda i, j: (i, j)),
                   pl.BlockSpec((tm, FF_CHUNK), lambda i, j: (i, j)),
                   pl.BlockSpec((tm, D), lambda i, j: (i, 0))],
        out_shape=[jax.ShapeDtypeStruct((T, D), F32), jax.ShapeDtypeStruct((T, D_FF), BF16),
                   jax.ShapeDtypeStruct((T, D_FF), BF16), jax.ShapeDtypeStruct((T, D), BF16)],
        scratch_shapes=[pltpu.VMEM((tm, D), BF16), pltpu.VMEM((tm, D), F32)],
        compiler_params=_params(("arbitrary", "arbitrary"), VMEM_BIG),
    )(x, nw, win, win, wout)


def _ffn_bwd(dxo, x, nw, g, u, win, wout, layer, name):
    T, D = x.shape
    tm = _tile(T, 256)
    nj = N_CHIPS // 2

    def body(dxo_ref, x_ref, n_ref, g_ref, u_ref, wg_ref, wu_ref, wo_ref,
             dx_ref, dg_ref, du_ref, a_ref, dyb_ref, dn_ref, dyb_s, dh_s):
        i, j = pl.program_id(0), pl.program_id(1)

        @pl.when(j == 0)
        def _():
            dyb = _bf(0.5 * dxo_ref[...])
            dyb_s[...] = dyb
            dyb_ref[...] = dyb
            dh_s[...] = jnp.zeros_like(dh_s)

        @pl.when((i == 0) & (j == 0))
        def _():
            dn_ref[...] = jnp.zeros_like(dn_ref)

        da = _dot_nt(dyb_s[...], wo_ref[...].reshape(FF_CHUNK, D))
        gv = g_ref[...].astype(F32)
        uv = u_ref[...].astype(F32)
        sg = _sigmoid(gv)
        sl = gv * sg
        dgb = _bf(da * uv * _silu_grad(gv, sg))
        dub = _bf(da * sl)
        a_ref[...] = _bf(sl * uv)
        dg_ref[...] = dgb
        du_ref[...] = dub
        dh_s[...] += _dot_nt(dgb, wg_ref[...]) + _dot_nt(dub, wu_ref[...])

        @pl.when(j == nj - 1)
        def _():
            xhat, r = _rms_stats(x_ref[...])
            dh = dh_s[...]
            dx_ref[...] = dxo_ref[...] + _rms_bwd(dh, xhat, r, n_ref[...])
            dn_ref[...] += jnp.sum(dh * xhat, axis=0, keepdims=True)

    return pl.pallas_call(
        body, name=name, grid=(T // tm, nj),
        in_specs=[pl.BlockSpec((tm, D), lambda i, j: (i, 0)),
                  pl.BlockSpec((tm, D), lambda i, j: (i, 0)),
                  pl.BlockSpec((None, 1, D), lambda i, j: (layer, 0, 0)),
                  pl.BlockSpec((tm, FF_CHUNK), lambda i, j: (i, j)),
                  pl.BlockSpec((tm, FF_CHUNK), lambda i, j: (i, j)),
                  pl.BlockSpec((None, None, D, FF_CHUNK), lambda i, j: (j, layer, 0, 0)),
                  pl.BlockSpec((None, None, D, FF_CHUNK), lambda i, j: (nj + j, layer, 0, 0)),
                  pl.BlockSpec((2, None, FF_CHUNK // 2, D), lambda i, j: (j, layer, 0, 0))],
        out_specs=[pl.BlockSpec((tm, D), lambda i, j: (i, 0)),
                   pl.BlockSpec((tm, FF_CHUNK), lambda i, j: (i, j)),
                   pl.BlockSpec((tm, FF_CHUNK), lambda i, j: (i, j)),
                   pl.BlockSpec((tm, FF_CHUNK), lambda i, j: (i, j)),
                   pl.BlockSpec((tm, D), lambda i, j: (i, 0)),
                   pl.BlockSpec((1, D), lambda i, j: (0, 0))],
        out_shape=[jax.ShapeDtypeStruct((T, D), F32), jax.ShapeDtypeStruct((T, D_FF), BF16),
                   jax.ShapeDtypeStruct((T, D_FF), BF16), jax.ShapeDtypeStruct((T, D_FF), BF16),
                   jax.ShapeDtypeStruct((T, D), BF16), jax.ShapeDtypeStruct((1, D), F32)],
        scratch_shapes=[pltpu.VMEM((tm, D), BF16), pltpu.VMEM((tm, D), F32)],
        compiler_params=_params(("arbitrary", "arbitrary"), VMEM_BIG),
    )(dxo, x, nw, g, u, win, win, wout)


def _matmul_tn(a, b, bm, bn, name, stack_n=False):
    T, M = a.shape
    N = b.shape[1]
    tk = _tile(T, 1024)
    bm, bn = _tile(M, bm), _tile(N, bn)

    def body(a_ref, b_ref, o_ref):
        @pl.when(pl.program_id(2) == 0)
        def _():
            o_ref[...] = jnp.zeros_like(o_ref)

        o_ref[...] += _dot_tn(_bf(a_ref[...]), _bf(b_ref[...]))

    if stack_n:
        out_spec = pl.BlockSpec((None, bm, bn), lambda m, n, k: (n, m, 0))
        out_shape = jax.ShapeDtypeStruct((N // bn, M, bn), F32)
    else:
        out_spec = pl.BlockSpec((bm, bn), lambda m, n, k: (m, n))
        out_shape = jax.ShapeDtypeStruct((M, N), F32)
    return pl.pallas_call(
        body, name=name, grid=(M // bm, N // bn, T // tk),
        in_specs=[pl.BlockSpec((tk, bm), lambda m, n, k: (k, m)),
                  pl.BlockSpec((tk, bn), lambda m, n, k: (k, n))],
        out_specs=out_spec, out_shape=out_shape,
        compiler_params=_params(("parallel", "parallel", "arbitrary"), VMEM_BIG),
    )(a, b)


def _matmul(a, b, name, trans_b=False, res=None, out_dtype=F32):
    T, K = a.shape
    N = b.shape[0] if trans_b else b.shape[1]
    tm = _tile(T, 512)

    def body(*refs):
        a_ref, b_ref = refs[0], refs[1]
        o_ref = refs[-1]
        av, bv = _bf(a_ref[...]), _bf(b_ref[...])
        acc = _dot_nt(av, bv) if trans_b else _dot(av, bv)
        if res is not None:
            acc = acc + refs[2][...]
        o_ref[...] = acc.astype(out_dtype)

    in_specs = [pl.BlockSpec((tm, K), lambda i: (i, 0)), pl.BlockSpec(b.shape, lambda i: (0, 0))]
    args = [a, b]
    if res is not None:
        in_specs.append(pl.BlockSpec((tm, N), lambda i: (i, 0)))
        args.append(res)
    return pl.pallas_call(
        body, name=name, grid=(T // tm,), in_specs=in_specs,
        out_specs=pl.BlockSpec((tm, N), lambda i: (i, 0)),
        out_shape=jax.ShapeDtypeStruct((T, N), out_dtype),
        compiler_params=_params(("parallel",), VMEM_BIG),
    )(*args)


def _rms_fwd_call(x, nw, layer, name):
    T, D = x.shape
    tm = _tile(T, 512)

    def body(x_ref, n_ref, o_ref):
        xhat, _ = _rms_stats(x_ref[...])
        o_ref[...] = _bf(xhat * n_ref[...])

    return pl.pallas_call(
        body, name=name, grid=(T // tm,),
        in_specs=[pl.BlockSpec((tm, D), lambda i: (i, 0)), pl.BlockSpec((None, 1, D), lambda i: (layer, 0, 0))],
        out_specs=pl.BlockSpec((tm, D), lambda i: (i, 0)),
        out_shape=jax.ShapeDtypeStruct((T, D), BF16),
        compiler_params=_params(("parallel",)),
    )(x, nw)


def _rms_bwd_call(dh, x, nw, dres, layer, name):
    T, D = x.shape
    tm = _tile(T, 512)

    def body(dh_ref, x_ref, n_ref, dr_ref, dx_ref, dn_ref):
        @pl.when(pl.program_id(0) == 0)
        def _():
            dn_ref[...] = jnp.zeros_like(dn_ref)

        xhat, r = _rms_stats(x_ref[...])
        dh_v = dh_ref[...]
        dx_ref[...] = dr_ref[...] + _rms_bwd(dh_v, xhat, r, n_ref[...])
        dn_ref[...] += jnp.sum(dh_v * xhat, axis=0, keepdims=True)

    row = pl.BlockSpec((tm, D), lambda i: (i, 0))
    return pl.pallas_call(
        body, name=name, grid=(T // tm,),
        in_specs=[row, row, pl.BlockSpec((None, 1, D), lambda i: (layer, 0, 0)), row],
        out_specs=[row, pl.BlockSpec((1, D), lambda i: (0, 0))],
        out_shape=[jax.ShapeDtypeStruct((T, D), F32), jax.ShapeDtypeStruct((1, D), F32)],
        compiler_params=_params(("arbitrary",)),
    )(dh, x, nw, dres)


def _shift_rows(x, s):
    n = x.shape[0]
    s = s % n
    return x if s == 0 else pltpu.roll(x, s, 0)


def _conv_fwd(proj, conv_w, name):
    T = proj.shape[0]
    C = 3 * DN_HEADS * DN_DIM
    cb = 512
    tm = _tile(T, 512)
    hb = tm // CONV_HALO

    def body(x_ref, xp_ref, w_ref, o_ref):
        i = pl.program_id(1)
        prev = jnp.where(i == 0, 0.0, xp_ref[...])
        ext = jnp.concatenate([prev, x_ref[...]], axis=0)
        w = w_ref[...]
        y = ext * w[DN_CONV - 1:DN_CONV, :]
        for k in range(DN_CONV - 1):
            y = y + _shift_rows(ext, DN_CONV - 1 - k) * w[k:k + 1, :]
        y = y[CONV_HALO:, :]
        o_ref[...] = y * _sigmoid(y)

    return pl.pallas_call(
        body, name=name, grid=(C // cb, T // tm),
        in_specs=[pl.BlockSpec((tm, cb), lambda c, i: (i, c)),
                  pl.BlockSpec((CONV_HALO, cb), lambda c, i: (jnp.maximum(i * hb - 1, 0), c)),
                  pl.BlockSpec((DN_CONV, cb), lambda c, i: (0, c))],
        out_specs=pl.BlockSpec((tm, cb), lambda c, i: (i, c)),
        out_shape=jax.ShapeDtypeStruct((T, C), F32),
        compiler_params=_params(("parallel", "parallel")),
    )(proj, proj, conv_w)


def _conv_bwd(dy, proj, conv_w, name):
    T = proj.shape[0]
    C = 3 * DN_HEADS * DN_DIM
    cb = 512
    tm = _tile(T, 512)
    hb = tm // CONV_HALO
    nt = T // tm

    def body(x_ref, xp_ref, xn_ref, dy_ref, dyn_ref, w_ref, dx_ref, dw_ref):
        i = pl.program_id(1)

        @pl.when(i == 0)
        def _():
            dw_ref[...] = jnp.zeros_like(dw_ref)

        prev = jnp.where(i == 0, 0.0, xp_ref[...])
        ext = jnp.concatenate([prev, x_ref[...], xn_ref[...]], axis=0)
        dy_ext = jnp.concatenate([jnp.zeros((CONV_HALO, cb), F32), dy_ref[...],
                                  jnp.where(i == nt - 1, 0.0, dyn_ref[...])], axis=0)
        w = w_ref[...]
        shifted = [_shift_rows(ext, DN_CONV - 1 - k) for k in range(DN_CONV)]
        y = shifted[0] * w[0:1, :]
        for k in range(1, DN_CONV):
            y = y + shifted[k] * w[k:k + 1, :]
        s = _sigmoid(y)
        dpre = dy_ext * _silu_grad(y, s)
        dx = dpre * w[DN_CONV - 1:DN_CONV, :]
        for k in range(DN_CONV - 1):
            dx = dx + _shift_rows(dpre, -(DN_CONV - 1 - k)) * w[k:k + 1, :]
        dx_ref[...] = _bf(dx[CONV_HALO:CONV_HALO + tm, :])
        rows = [jnp.sum((dpre * shifted[k])[CONV_HALO:CONV_HALO + tm, :], axis=0, keepdims=True) for k in range(DN_CONV)]
        dw_ref[...] += jnp.concatenate(rows, axis=0)

    last_halo = T // CONV_HALO - 1
    return pl.pallas_call(
        body, name=name, grid=(C // cb, nt),
        in_specs=[pl.BlockSpec((tm, cb), lambda c, i: (i, c)),
                  pl.BlockSpec((CONV_HALO, cb), lambda c, i: (jnp.maximum(i * hb - 1, 0), c)),
                  pl.BlockSpec((CONV_HALO, cb), lambda c, i: (jnp.minimum((i + 1) * hb, last_halo), c)),
                  pl.BlockSpec((tm, cb), lambda c, i: (i, c)),
                  pl.BlockSpec((CONV_HALO, cb), lambda c, i: (jnp.minimum((i + 1) * hb, last_halo), c)),
                  pl.BlockSpec((DN_CONV, cb), lambda c, i: (0, c))],
        out_specs=[pl.BlockSpec((tm, cb), lambda c, i: (i, c)),
                   pl.BlockSpec((DN_CONV, cb), lambda c, i: (0, c))],
        out_shape=[jax.ShapeDtypeStruct((T, C), BF16), jax.ShapeDtypeStruct((DN_CONV, C), F32)],
        compiler_params=_params(("parallel", "arbitrary")),
    )(proj, proj, proj, dy, dy, conv_w)


def _unit_lower_inverse(low, eye):
    x = -low
    inv = eye + x
    p = x
    for _ in range(5):
        p = _dot_hi(p, p)
        inv = inv + _dot_hi(inv, p)
    return inv


def _l2_unit(x):
    r = lax.rsqrt(jnp.sum(x * x, axis=-1, keepdims=True) + NORM_EPS)
    return x * r, r


class _ChunkMasks:
    def __init__(self):
        c = DN_CHUNK
        row = lax.broadcasted_iota(jnp.int32, (c, c), 0)
        col = lax.broadcasted_iota(jnp.int32, (c, c), 1)
        self.lower, self.strict_lower = row >= col, row > col
        self.upper, self.strict_upper = row <= col, row < col
        self.eye = (row == col).astype(F32)
        self.ones = jnp.ones((c, c), F32)


def _dn_gates(ba, hp):
    coef = -jnp.exp(hp[0:1, :])
    pre = ba + hp[1:2, :]
    return _sigmoid(ba), coef * _softplus(pre), coef, pre


def _dn_head_forward(mk, qraw, kraw, vh, bc, gc, glast, state):
    f = {}
    f["qn"], f["rq"] = _l2_unit(qraw)
    qh = f["qn"] * (DN_DIM ** -0.5)
    kh, f["rk"] = _l2_unit(kraw)
    gr = _dot_hi(mk.ones, mk.eye * gc)
    dec = jnp.where(mk.lower, jnp.exp(jnp.where(mk.lower, gc - gr, 0.0)), 0.0)
    kb = kh * bc
    mkk = _dot_nt(_bf(kb), _bf(kh))
    low = jnp.where(mk.strict_lower, mkk * dec, 0.0)
    inv = _unit_lower_inverse(low, mk.eye)
    eg = jnp.exp(gc)
    sol = _dot_hi(inv, jnp.concatenate([vh * bc, kb * eg], axis=1))
    u, w = sol[:, :DN_DIM], sol[:, DN_DIM:]
    mqk = _dot_nt(_bf(qh), _bf(kh))
    attn = mqk * dec
    etl = jnp.exp(glast - gc)
    qd, kt = qh * eg, kh * etl
    sb = _bf(state)
    vn = u - _dot(_bf(w), sb)
    out = _dot(_bf(qd), sb) + _dot(_bf(attn), _bf(vn))
    egl = jnp.exp(glast)
    new_state = state * egl + _dot_tn(_bf(kt), _bf(vn))
    f.update(qh=qh, kh=kh, gr=gr, dec=dec, kb=kb, mkk=mkk, eg=eg, u=u, w=w, mqk=mqk, attn=attn, etl=etl,
             qd=qd, kt=kt, vn=vn, egl=egl)
    return out, new_state, f


def _dn_fwd(qkv, proj, hp, name):
    T = qkv.shape[0]
    n_chunks = T // DN_CHUNK
    hw = DN_HEADS * DN_DIM

    def body(q_ref, k_ref, v_ref, ba_ref, hp_ref, o_ref, sall_ref, s_s):
        @pl.when(pl.program_id(0) == 0)
        def _():
            s_s[...] = jnp.zeros_like(s_s)

        mk = _ChunkMasks()
        beta_t, graw_t, _, _ = _dn_gates(ba_ref[...], hp_ref[...])
        gcum_t = _dot_hi(mk.lower.astype(F32), graw_t)
        for h in range(DN_HEADS):
            sl = slice(h * DN_DIM, (h + 1) * DN_DIM)
            gc = gcum_t[:, DN_HEADS + h:DN_HEADS + h + 1]
            glast = gc[DN_CHUNK - 1:DN_CHUNK, :]
            state = s_s[h]
            sall_ref[0, h] = state
            out, new_state, _ = _dn_head_forward(mk, q_ref[:, sl], k_ref[:, sl], v_ref[:, sl], beta_t[:, h:h + 1],
                                                 gc, glast, state)
            o_ref[:, sl] = out
            s_s[h] = new_state

    return pl.pallas_call(
        body, name=name, grid=(n_chunks,),
        in_specs=[pl.BlockSpec((DN_CHUNK, hw), lambda n: (n, 0)),
                  pl.BlockSpec((DN_CHUNK, hw), lambda n: (n, 1)),
                  pl.BlockSpec((DN_CHUNK, hw), lambda n: (n, 2)),
                  pl.BlockSpec((DN_CHUNK, 128), lambda n: (n, BA_BLOCK)),
                  pl.BlockSpec((8, 128), lambda n: (0, 0))],
        out_specs=[pl.BlockSpec((DN_CHUNK, hw), lambda n: (n, 0)),
                   pl.BlockSpec((1, DN_HEADS, DN_DIM, DN_DIM), lambda n: (n, 0, 0, 0))],
        out_shape=[jax.ShapeDtypeStruct((T, hw), F32),
                   jax.ShapeDtypeStruct((n_chunks, DN_HEADS, DN_DIM, DN_DIM), F32)],
        scratch_shapes=[pltpu.VMEM((DN_HEADS, DN_DIM, DN_DIM), F32)],
        compiler_params=_params(("arbitrary",)),
    )(qkv, qkv, qkv, proj, hp)


def _dn_bwd(qkv, proj, hp, sall, do, name):
    T = qkv.shape[0]
    n_chunks = T // DN_CHUNK
    hw = DN_HEADS * DN_DIM
    last = n_chunks - 1

    def rowsum(x):
        return jnp.sum(x, axis=1, keepdims=True)

    def body(q_ref, k_ref, v_ref, ba_ref, hp_ref, sall_ref, do_ref,
             dq_ref, dk_ref, dv_ref, dba_ref, dhp_ref, ds_s):
        @pl.when(pl.program_id(0) == 0)
        def _():
            ds_s[...] = jnp.zeros_like(ds_s)
            dhp_ref[...] = jnp.zeros_like(dhp_ref)

        mk = _ChunkMasks()
        ba = ba_ref[...]
        beta_t, graw_t, coef, pre = _dn_gates(ba, hp_ref[...])
        gcum_t = _dot_hi(mk.lower.astype(F32), graw_t)
        lane = lax.broadcasted_iota(jnp.int32, (DN_CHUNK, 128), 1)
        rowi = lax.broadcasted_iota(jnp.int32, (DN_CHUNK, 1), 0)
        dgcum_t = jnp.zeros((DN_CHUNK, 128), F32)
        dbeta_t = jnp.zeros((DN_CHUNK, 128), F32)
        for h in range(DN_HEADS):
            sl = slice(h * DN_DIM, (h + 1) * DN_DIM)
            gc = gcum_t[:, DN_HEADS + h:DN_HEADS + h + 1]
            glast = gc[DN_CHUNK - 1:DN_CHUNK, :]
            bc = beta_t[:, h:h + 1]
            state = sall_ref[0, h]
            vh = v_ref[:, sl]
            _, _, f = _dn_head_forward(mk, q_ref[:, sl], k_ref[:, sl], vh, bc, gc, glast, state)
            qh, kh, kb, dec, eg, etl = f["qh"], f["kh"], f["kb"], f["dec"], f["eg"], f["etl"]
            u, w, vn, qd, kt = f["u"], f["w"], f["vn"], f["qd"], f["kt"]
            qb, kbf, kbb = _bf(qh), _bf(kh), _bf(kb)
            dec_t = jnp.where(mk.upper, jnp.exp(jnp.where(mk.upper, f["gr"] - gc, 0.0)), 0.0)
            mkk_t = _dot_nt(kbf, kbb)
            inv_t = _unit_lower_inverse(jnp.where(mk.strict_upper, mkk_t * dec_t, 0.0), mk.eye)
            mqk_t = _dot_nt(kbf, qb)
            attn_t = mqk_t * dec_t

            d_out = _bf(do_ref[:, sl])
            d_new = ds_s[h]
            d_newb, sb, vnb = _bf(d_new), _bf(state), _bf(vn)
            d_qd = _dot_nt(d_out, sb)
            d_attn = _dot_nt(d_out, vnb)
            d_attn_t = _dot_nt(vnb, d_out)
            d_vn = _dot(_bf(attn_t), d_out) + _dot(_bf(kt), d_newb)
            d_kt = _dot_nt(vnb, d_newb)
            d_vnb = _bf(d_vn)
            d_w = -_dot_nt(d_vnb, sb)
            ds_s[h] = d_new * f["egl"] + _dot_tn(_bf(qd), d_out) - _dot_tn(_bf(w), d_vnb)
            d_glast = jnp.sum(rowsum(d_new * state), axis=0, keepdims=True) * f["egl"]
            d_rhs = _dot_hi(inv_t, jnp.concatenate([d_vn, d_w], axis=1))
            d_bu, d_bw = d_rhs[:, :DN_DIM], d_rhs[:, DN_DIM:]
            ub, wb, d_bub, d_bwb = _bf(u), _bf(w), _bf(d_bu), _bf(d_bw)
            d_low = -(_dot_nt(d_bub, ub) + _dot_nt(d_bwb, wb))
            d_low_t = -(_dot_nt(ub, d_bub) + _dot_nt(wb, d_bwb))
            d_mkk = jnp.where(mk.strict_lower, d_low * dec, 0.0)
            d_mkk_t = jnp.where(mk.strict_upper, d_low_t * dec_t, 0.0)
            d_mqk = jnp.where(mk.lower, d_attn * dec, 0.0)
            d_mqk_t = jnp.where(mk.upper, d_attn_t * dec_t, 0.0)
            bw = kb * eg
            d_kb = _dot(_bf(d_mkk), kbf) + d_bw * eg
            d_k = _dot(_bf(d_mkk_t), kbb) + _dot(_bf(d_mqk_t), qb) + d_kt * etl + d_kb * bc
            d_q = _dot(_bf(d_mqk), kbf) + d_qd * eg
            d_beta = rowsum(d_kb * kh) + rowsum(d_bu * vh)
            dv_ref[:, sl] = d_bu * bc
            e_mat = d_mkk * f["mkk"] + d_mqk * f["mqk"]
            e_mat_t = d_mkk_t * mkk_t + d_mqk_t * mqk_t
            kt_term = rowsum(d_kt * kt)
            d_g = rowsum(e_mat) - rowsum(e_mat_t) + rowsum(d_qd * qd) + rowsum(d_bw * bw) - kt_term
            d_glast = d_glast + jnp.sum(kt_term, axis=0, keepdims=True)
            d_g = d_g + jnp.where(rowi == DN_CHUNK - 1, d_glast, 0.0)
            qn = f["qn"]
            d_qs = d_q * (DN_DIM ** -0.5)
            dq_ref[:, sl] = f["rq"] * (d_qs - qn * rowsum(d_qs * qn))
            dk_ref[:, sl] = f["rk"] * (d_k - kh * rowsum(d_k * kh))
            dgcum_t = jnp.where(lane == DN_HEADS + h, d_g, dgcum_t)
            dbeta_t = jnp.where(lane == h, d_beta, dbeta_t)
        dgraw_t = _dot_hi(mk.upper.astype(F32), dgcum_t)
        sp = _sigmoid(pre)
        d_pre = dgraw_t * coef * sp
        dba_ref[...] = jnp.where(lane < DN_HEADS, dbeta_t * beta_t * (1.0 - beta_t),
                                 jnp.where(lane < 2 * DN_HEADS, d_pre, 0.0))
        in_g = (lane >= DN_HEADS) & (lane < 2 * DN_HEADS)
        d_alog = jnp.sum(jnp.where(in_g, dgraw_t * graw_t, 0.0), axis=0, keepdims=True)
        d_dtb = jnp.sum(jnp.where(in_g, d_pre, 0.0), axis=0, keepdims=True)
        dhp_ref[...] += jnp.concatenate([d_alog, d_dtb, jnp.zeros((6, 128), F32)], axis=0)

    rev = lambda n: last - n
    return pl.pallas_call(
        body, name=name, grid=(n_chunks,),
        in_specs=[pl.BlockSpec((DN_CHUNK, hw), lambda n: (rev(n), 0)),
                  pl.BlockSpec((DN_CHUNK, hw), lambda n: (rev(n), 1)),
                  pl.BlockSpec((DN_CHUNK, hw), lambda n: (rev(n), 2)),
                  pl.BlockSpec((DN_CHUNK, 128), lambda n: (rev(n), BA_BLOCK)),
                  pl.BlockSpec((8, 128), lambda n: (0, 0)),
                  pl.BlockSpec((1, DN_HEADS, DN_DIM, DN_DIM), lambda n: (rev(n), 0, 0, 0)),
                  pl.BlockSpec((DN_CHUNK, hw), lambda n: (rev(n), 0))],
        out_specs=[pl.BlockSpec((DN_CHUNK, hw), lambda n: (rev(n), 0)),
                   pl.BlockSpec((DN_CHUNK, hw), lambda n: (rev(n), 0)),
                   pl.BlockSpec((DN_CHUNK, hw), lambda n: (rev(n), 0)),
                   pl.BlockSpec((DN_CHUNK, 128), lambda n: (rev(n), 0)),
                   pl.BlockSpec((8, 128), lambda n: (0, 0))],
        out_shape=[jax.ShapeDtypeStruct((T, hw), F32)] * 3
        + [jax.ShapeDtypeStruct((T, 128), F32), jax.ShapeDtypeStruct((8, 128), F32)],
        scratch_shapes=[pltpu.VMEM((DN_HEADS, DN_DIM, DN_DIM), F32)],
        compiler_params=_params(("arbitrary",)),
    )(qkv, qkv, qkv, proj, hp, sall, do)


def _mix_fwd(o, proj, dn_norm, sg_norm, sg_w, sg_bt, name):
    T = o.shape[0]
    tm = _tile(T, 512)
    hw = DN_HEADS * DN_DIM
    nc = tm // SG_CHUNK

    def body(o_ref, z_ref, su_ref, sv_ref, dnn_ref, sgn_ref, sgw_ref, sgb_ref, mix_ref):
        dnn = dnn_ref[...]
        for h in range(DN_HEADS):
            sl = slice(h * DN_DIM, (h + 1) * DN_DIM)
            xhat, _ = _rms_stats(o_ref[:, sl])
            z = z_ref[:, sl]
            mix_ref[:, sl] = _bf(xhat * dnn * (z * _sigmoid(z)))
        tri = lax.broadcasted_iota(jnp.int32, (SG_CHUNK, SG_CHUNK), 0) >= lax.broadcasted_iota(jnp.int32, (SG_CHUNK, SG_CHUNK), 1)
        for g in range(SG_GROUPS):
            sl = slice(g * SG_DIM, (g + 1) * SG_DIM)
            xhat, _ = _rms_stats(_gelu(sv_ref[:, sl]))
            svn = _bf(xhat * sgn_ref[g:g + 1, :])
            sua = _gelu(su_ref[:, sl])
            wt = _bf(jnp.where(tri, sgw_ref[g], 0.0))
            bias = sgb_ref[:, g:g + 1]
            for c in range(nc):
                rows = slice(c * SG_CHUNK, (c + 1) * SG_CHUNK)
                mixed = _dot(wt, svn[rows, :]) + bias
                mix_ref[rows, hw + g * SG_DIM:hw + (g + 1) * SG_DIM] = _bf(sua[rows, :] * mixed)

    full = lambda shape: pl.BlockSpec(shape, lambda i: (0,) * len(shape))
    return pl.pallas_call(
        body, name=name, grid=(T // tm,),
        in_specs=[pl.BlockSpec((tm, hw), lambda i: (i, 0)),
                  pl.BlockSpec((tm, hw), lambda i: (i, 3)),
                  pl.BlockSpec((tm, hw), lambda i: (i, 4)),
                  pl.BlockSpec((tm, hw), lambda i: (i, 5)),
                  full((1, DN_DIM)), full((SG_GROUPS, SG_DIM)), full((SG_GROUPS, SG_CHUNK, SG_CHUNK)),
                  full((SG_CHUNK, 128))],
        out_specs=pl.BlockSpec((tm, 2 * hw), lambda i: (i, 0)),
        out_shape=jax.ShapeDtypeStruct((T, 2 * hw), BF16),
        compiler_params=_params(("parallel",)),
    )(o, proj, proj, proj, dn_norm, sg_norm, sg_w, sg_bt)


def _mix_bwd(dmix, o, proj, dn_norm, sg_norm, sg_w, sg_bt, name):
    T = o.shape[0]
    tm = _tile(T, 512)
    hw = DN_HEADS * DN_DIM
    nc = tm // SG_CHUNK

    def body(dm_ref, o_ref, z_ref, su_ref, sv_ref, dnn_ref, sgn_ref, sgw_ref, sgb_ref,
             do_ref, dz_ref, ddnn_ref, dsgn_ref, dsgw_ref, dsgb_ref):
        @pl.when(pl.program_id(0) == 0)
        def _():
            ddnn_ref[...] = jnp.zeros_like(ddnn_ref)
            dsgn_ref[...] = jnp.zeros_like(dsgn_ref)
            dsgw_ref[...] = jnp.zeros_like(dsgw_ref)
            dsgb_ref[...] = jnp.zeros_like(dsgb_ref)

        dnn = dnn_ref[...]
        ddnn = jnp.zeros((1, DN_DIM), F32)
        for h in range(DN_HEADS):
            sl = slice(h * DN_DIM, (h + 1) * DN_DIM)
            xhat, r = _rms_stats(o_ref[:, sl])
            z = z_ref[:, sl]
            sz = _sigmoid(z)
            doa = dm_ref[:, sl]
            dyn = doa * (z * sz)
            dz_ref[:, sl] = _bf(doa * xhat * dnn * _silu_grad(z, sz))
            do_ref[:, sl] = _rms_bwd(dyn, xhat, r, dnn)
            ddnn = ddnn + jnp.sum(dyn * xhat, axis=0, keepdims=True)
        ddnn_ref[...] += ddnn
        tri = lax.broadcasted_iota(jnp.int32, (SG_CHUNK, SG_CHUNK), 0) >= lax.broadcasted_iota(jnp.int32, (SG_CHUNK, SG_CHUNK), 1)
        lane = lax.broadcasted_iota(jnp.int32, (SG_CHUNK, 128), 1)
        dsgb = jnp.zeros((SG_CHUNK, 128), F32)
        dsgn_rows = []
        for g in range(SG_GROUPS):
            sl = slice(g * SG_DIM, (g + 1) * SG_DIM)
            sv = sv_ref[:, sl]
            su = su_ref[:, sl]
            xhat, r = _rms_stats(_gelu(sv))
            sgn = sgn_ref[g:g + 1, :]
            svn = _bf(xhat * sgn)
            sua = _gelu(su)
            wt = _bf(jnp.where(tri, sgw_ref[g], 0.0))
            bias = sgb_ref[:, g:g + 1]
            dw = jnp.zeros((SG_CHUNK, SG_CHUNK), F32)
            db = jnp.zeros((SG_CHUNK, 1), F32)
            dsua, dsvn = [], []
            for c in range(nc):
                rows = slice(c * SG_CHUNK, (c + 1) * SG_CHUNK)
                mixed = _dot(wt, svn[rows, :]) + bias
                dob = dm_ref[rows, hw + g * SG_DIM:hw + (g + 1) * SG_DIM]
                dsua.append(dob * mixed)
                dmixed = dob * sua[rows, :]
                dmb = _bf(dmixed)
                dsvn.append(_dot_tn(wt, dmb))
                dw = dw + _dot_nt(dmb, svn[rows, :])
                db = db + jnp.sum(dmixed, axis=1, keepdims=True)
            dsua = jnp.concatenate(dsua, axis=0) if nc > 1 else dsua[0]
            dsvn = jnp.concatenate(dsvn, axis=0) if nc > 1 else dsvn[0]
            dz_ref[:, hw + g * SG_DIM:hw + (g + 1) * SG_DIM] = _bf(dsua * _gelu_grad(su))
            dz_ref[:, 2 * hw + g * SG_DIM:2 * hw + (g + 1) * SG_DIM] = _bf(_rms_bwd(dsvn, xhat, r, sgn) * _gelu_grad(sv))
            dsgn_rows.append(jnp.sum(dsvn * xhat, axis=0, keepdims=True))
            dsgw_ref[g] += jnp.where(tri, dw, 0.0)
            dsgb = jnp.where(lane == g, db, dsgb)
        dsgn_ref[...] += jnp.concatenate(dsgn_rows, axis=0)
        dsgb_ref[...] += dsgb

    full = lambda shape: pl.BlockSpec(shape, lambda i: (0,) * len(shape))
    return pl.pallas_call(
        body, name=name, grid=(T // tm,),
        in_specs=[pl.BlockSpec((tm, 2 * hw), lambda i: (i, 0)),
                  pl.BlockSpec((tm, hw), lambda i: (i, 0)),
                  pl.BlockSpec((tm, hw), lambda i: (i, 3)),
                  pl.BlockSpec((tm, hw), lambda i: (i, 4)),
                  pl.BlockSpec((tm, hw), lambda i: (i, 5)),
                  full((1, DN_DIM)), full((SG_GROUPS, SG_DIM)), full((SG_GROUPS, SG_CHUNK, SG_CHUNK)),
                  full((SG_CHUNK, 128))],
        out_specs=[pl.BlockSpec((tm, hw), lambda i: (i, 0)),
                   pl.BlockSpec((tm, 3 * hw), lambda i: (i, 0)),
                   full((1, DN_DIM)), full((SG_GROUPS, SG_DIM)), full((SG_GROUPS, SG_CHUNK, SG_CHUNK)),
                   full((SG_CHUNK, 128))],
        out_shape=[jax.ShapeDtypeStruct((T, hw), F32), jax.ShapeDtypeStruct((T, 3 * hw), BF16),
                   jax.ShapeDtypeStruct((1, DN_DIM), F32), jax.ShapeDtypeStruct((SG_GROUPS, SG_DIM), F32),
                   jax.ShapeDtypeStruct((SG_GROUPS, SG_CHUNK, SG_CHUNK), F32),
                   jax.ShapeDtypeStruct((SG_CHUNK, 128), F32)],
        compiler_params=_params(("arbitrary",)),
    )(dmix, o, proj, proj, proj, dn_norm, sg_norm, sg_w, sg_bt)


def _window_sums(h, sign):
    sums, s, w = {}, h, 1
    while w < POOL_WINDOWS[-1]:
        s = s + _shift_rows(s, sign * w)
        w *= 2
        sums[w] = s
    return sums


def _pool_counts(t_global):
    return [jnp.minimum(t_global + 1, win).astype(F32) for win in POOL_WINDOWS]


def _pooled_groups(ext_h, row0, tm):
    sums = _window_sums(ext_h, 1)
    t_global = row0 + lax.broadcasted_iota(jnp.int32, (tm, 1), 0)
    counts = _pool_counts(t_global)
    out = []
    for gi, win in enumerate(POOL_WINDOWS):
        cols = slice(gi * POOL_DIM, (gi + 1) * POOL_DIM)
        out.append(sums[win][POOL_HALO:, cols] / counts[gi] - ext_h[POOL_HALO:, cols])
    return out


def _pool_fwd(x, nw, pool_w, pool_scale, layer, name):
    T, D = x.shape
    tm = _tile(T, 256)
    hb = tm // POOL_HALO

    def body(x_ref, xp_ref, n_ref, w_ref, s_ref, xo_ref):
        i = pl.program_id(0)
        prev = jnp.where(i == 0, 0.0, xp_ref[...])
        ext = jnp.concatenate([prev, x_ref[...]], axis=0)
        xhat, _ = _rms_stats(ext)
        pooled = _pooled_groups(xhat * n_ref[...], i * tm, tm)
        for gi in range(len(POOL_WINDOWS)):
            cols = slice(gi * POOL_DIM, (gi + 1) * POOL_DIM)
            xo_ref[:, cols] = x_ref[:, cols] + _dot(_bf(pooled[gi]), w_ref[gi]) * s_ref[:, cols]

    return pl.pallas_call(
        body, name=name, grid=(T // tm,),
        in_specs=[pl.BlockSpec((tm, D), lambda i: (i, 0)),
                  pl.BlockSpec((POOL_HALO, D), lambda i: (jnp.maximum(i * hb - 1, 0), 0)),
                  pl.BlockSpec((None, 1, D), lambda i: (layer, 0, 0)),
                  pl.BlockSpec(pool_w.shape, lambda i: (0, 0, 0)),
                  pl.BlockSpec((1, D), lambda i: (0, 0))],
        out_specs=pl.BlockSpec((tm, D), lambda i: (i, 0)),
        out_shape=jax.ShapeDtypeStruct((T, D), F32),
        compiler_params=_params(("parallel",)),
    )(x, x, nw, pool_w, pool_scale)


def _pool_bwd(dxo, x, nw, pool_w, pool_scale, layer, name):
    T, D = x.shape
    tm = _tile(T, 256)
    hb = tm // POOL_HALO
    nt = T // tm
    ng = len(POOL_WINDOWS)

    def body(dxo_ref, dxn_ref, x_ref, xp_ref, n_ref, w_ref, s_ref, dx_ref, dw_ref, ds_ref, dn_ref):
        i = pl.program_id(0)

        @pl.when(i == 0)
        def _():
            dw_ref[...] = jnp.zeros_like(dw_ref)
            ds_ref[...] = jnp.zeros_like(ds_ref)
            dn_ref[...] = jnp.zeros_like(dn_ref)

        prev = jnp.where(i == 0, 0.0, xp_ref[...])
        ext = jnp.concatenate([prev, x_ref[...]], axis=0)
        xhat_ext, r_ext = _rms_stats(ext)
        nv = n_ref[...]
        pooled = _pooled_groups(xhat_ext * nv, i * tm, tm)
        dxo = dxo_ref[...]
        scale = s_ref[...]
        dout_ext = jnp.concatenate([dxo, jnp.where(i == nt - 1, 0.0, dxn_ref[...])], axis=0) * scale
        t_ext = i * tm + lax.broadcasted_iota(jnp.int32, (tm + POOL_HALO, 1), 0)
        counts = _pool_counts(t_ext)
        dh_cols, ds_cols = [], []
        for gi, win in enumerate(POOL_WINDOWS):
            cols = slice(gi * POOL_DIM, (gi + 1) * POOL_DIM)
            wg = w_ref[gi]
            pb = _bf(pooled[gi])
            doutb = _bf(dout_ext[:, cols])
            dpooled = _dot_nt(doutb, wg)
            ahead = _window_sums(dpooled / counts[gi], -1)[win]
            dh_cols.append(ahead[:tm, :] - dpooled[:tm, :])
            dw_ref[gi] += _dot_tn(pb, doutb[:tm, :])
            ds_cols.append(jnp.sum(dxo[:, cols] * _dot(pb, wg), axis=0, keepdims=True))
        dh = jnp.concatenate(dh_cols, axis=1)
        xhat, r = xhat_ext[POOL_HALO:, :], r_ext[POOL_HALO:, :]
        dx_ref[...] = dxo + _rms_bwd(dh, xhat, r, nv)
        dn_ref[...] += jnp.sum(dh * xhat, axis=0, keepdims=True)
        ds_ref[...] += jnp.concatenate(ds_cols, axis=1)

    last_halo = T // POOL_HALO - 1
    return pl.pallas_call(
        body, name=name, grid=(nt,),
        in_specs=[pl.BlockSpec((tm, D), lambda i: (i, 0)),
                  pl.BlockSpec((POOL_HALO, D), lambda i: (jnp.minimum((i + 1) * hb, last_halo), 0)),
                  pl.BlockSpec((tm, D), lambda i: (i, 0)),
                  pl.BlockSpec((POOL_HALO, D), lambda i: (jnp.maximum(i * hb - 1, 0), 0)),
                  pl.BlockSpec((None, 1, D), lambda i: (layer, 0, 0)),
                  pl.BlockSpec(pool_w.shape, lambda i: (0, 0, 0)),
                  pl.BlockSpec((1, D), lambda i: (0, 0))],
        out_specs=[pl.BlockSpec((tm, D), lambda i: (i, 0)),
                   pl.BlockSpec((ng, POOL_DIM, POOL_DIM), lambda i: (0, 0, 0)),
                   pl.BlockSpec((1, D), lambda i: (0, 0)),
                   pl.BlockSpec((1, D), lambda i: (0, 0))],
        out_shape=[jax.ShapeDtypeStruct((T, D), F32), jax.ShapeDtypeStruct((ng, POOL_DIM, POOL_DIM), F32),
                   jax.ShapeDtypeStruct((1, D), F32), jax.ShapeDtypeStruct((1, D), F32)],
        compiler_params=_params(("arbitrary",)),
    )(dxo, dxo, x, x, nw, pool_w, pool_scale)


def _loss_head(x, target, fn, name):
    T, D = x.shape
    tm = _tile(T, 512)

    def body(x_ref, t_ref, n_ref, loss_ref, dx_ref, dn_ref):
        @pl.when(pl.program_id(0) == 0)
        def _():
            loss_ref[...] = jnp.zeros_like(loss_ref)
            dn_ref[...] = jnp.zeros_like(dn_ref)

        xhat, r = _rms_stats(x_ref[...])
        nv = n_ref[...]
        err = xhat * nv - t_ref[...]
        part = jnp.sum(jnp.sum(err * err, axis=1, keepdims=True), axis=0, keepdims=True)
        loss_ref[...] += 0.5 * part / D
        dy = err / D
        dx_ref[...] = _rms_bwd(dy, xhat, r, nv)
        dn_ref[...] += jnp.sum(dy * xhat, axis=0, keepdims=True)

    row = pl.BlockSpec((tm, D), lambda i: (i, 0))
    return pl.pallas_call(
        body, name=name, grid=(T // tm,),
        in_specs=[row, row, pl.BlockSpec((1, D), lambda i: (0, 0))],
        out_specs=[pl.BlockSpec((1, 1), lambda i: (0, 0)), row, pl.BlockSpec((1, D), lambda i: (0, 0))],
        out_shape=[jax.ShapeDtypeStruct((1, 1), F32), jax.ShapeDtypeStruct((T, D), F32),
                   jax.ShapeDtypeStruct((1, D), F32)],
        compiler_params=_params(("arbitrary",)),
    )(x, target, fn)


def _adamw(w, g, m, v, name):
    R, C = w.shape
    br = R
    for cand in (512, 256, 128, 64, 32, 16, 8):
        if R % cand == 0 and cand * C * 4 <= 2 * 1024 * 1024:
            br = cand
            break

    def body(w_ref, g_ref, m_ref, v_ref, d_ref, mo_ref, vo_ref):
        gv = g_ref[...]
        m_new = ADAM_B1 * m_ref[...] + (1.0 - ADAM_B1) * gv
        v_new = ADAM_B2 * v_ref[...] + (1.0 - ADAM_B2) * (gv * gv)
        m_hat = m_new / (1.0 - ADAM_B1 ** ADAM_STEP)
        v_hat = v_new / (1.0 - ADAM_B2 ** ADAM_STEP)
        d_ref[...] = -ADAM_LR * (m_hat / (jnp.sqrt(v_hat) + ADAM_EPS) + ADAM_WD * w_ref[...])
        mo_ref[...] = m_new
        vo_ref[...] = v_new

    blk = pl.BlockSpec((br, C), lambda i: (i, 0))
    return pl.pallas_call(
        body, name=name, grid=(R // br,), in_specs=[blk] * 4, out_specs=[blk] * 3,
        out_shape=[jax.ShapeDtypeStruct((R, C), F32)] * 3,
        compiler_params=_params(("parallel",)),
    )(w, g, m, v)


def _mesh_pos():
    return lax.axis_index("x"), lax.axis_index("y"), lax.axis_index("c")


def _other_chips(x, y):
    return [(1 - x, y), (x, 1 - y), (1 - x, 1 - y)]


def _half_of(ref, shape, h):
    if shape[0] == 2:
        return ref.at[h]
    size = shape[1] // 2
    return ref.at[:, pl.ds(h * size, size)]


def _all_gather_chips(shards, split, name):
    n = len(shards)

    def body(*refs):
        ins, outs = refs[:n], refs[n:2 * n]
        send_sems, recv_sems, fwd_send_sems, fwd_recv_sems, local_sems = refs[2 * n:]
        x, y, c = _mesh_pos()
        me = 2 * x + y
        sibling = (x, y, 1 - c)
        chips = _other_chips(x, y)

        def piece(a, ref, h):
            return _half_of(ref, shards[a].shape, h) if split[a] else ref

        local, sent, forwards = [], [], []
        for a in range(n):
            loc = pltpu.make_async_copy(ins[a], outs[a].at[me], local_sems.at[a])
            loc.start()
            local.append(loc)
            for k, (px, py) in enumerate(chips):
                cp = pltpu.make_async_remote_copy(piece(a, ins[a], c), piece(a, outs[a].at[me], c),
                                                  send_sems.at[a, k], recv_sems.at[a, k],
                                                  device_id=(px, py, c), device_id_type=MESH)
                cp.start()
                sent.append(cp)
        for a in range(n):
            for k, (px, py) in enumerate(chips):
                landed = piece(a, outs[a].at[2 * px + py], c)
                pltpu.make_async_remote_copy(landed, landed, send_sems.at[a, k], recv_sems.at[a, k],
                                             device_id=(px, py, c), device_id_type=MESH).wait_recv()
                if split[a]:
                    fwd = pltpu.make_async_remote_copy(landed, landed, fwd_send_sems.at[a, k], fwd_recv_sems.at[a, k],
                                                       device_id=sibling, device_id_type=MESH)
                    fwd.start()
                    forwards.append(fwd)
        for a in range(n):
            if split[a]:
                for k, (px, py) in enumerate(chips):
                    other = piece(a, outs[a].at[2 * px + py], 1 - c)
                    pltpu.make_async_remote_copy(other, other, fwd_send_sems.at[a, k], fwd_recv_sems.at[a, k],
                                                 device_id=sibling, device_id_type=MESH).wait_recv()
        for cp in sent + forwards:
            cp.wait_send()
        for loc in local:
            loc.wait()

    sems = pltpu.SemaphoreType.DMA((n, 3))
    return pl.pallas_call(
        body, name=name, in_specs=[ANY] * n, out_specs=[ANY] * n,
        out_shape=[jax.ShapeDtypeStruct((N_CHIPS,) + s.shape, s.dtype) for s in shards],
        scratch_shapes=[sems, sems, sems, sems, pltpu.SemaphoreType.DMA((n,))],
        compiler_params=pltpu.CompilerParams(has_side_effects=True),
    )(*shards)


def _ffn_weight_grads(hb, dg, du, a, dyb, tag):
    dwg = _matmul_tn(hb, dg, D_MODEL, FF_CHUNK, f"{tag}_dw_gate", stack_n=True)
    dwu = _matmul_tn(hb, du, D_MODEL, FF_CHUNK, f"{tag}_dw_up", stack_n=True)
    dwo = _matmul_tn(a, dyb, FF_CHUNK, D_MODEL, f"{tag}_dw_out")
    return jnp.concatenate([dwg, dwu], axis=0), dwo.reshape(N_CHIPS, D_FF // N_CHIPS, D_MODEL)


def _local_step(x, target, w):
    g = {}
    acts = []
    ffn_w = {1: (w["n1"], w["win1"], w["wout1"]), 2: (w["n2"], w["win2"], w["wout2"])}

    def ffn(xin, which, layer):
        nw, win, wout = ffn_w[which]
        xo, gv, uv, hb = _ffn_fwd(xin, nw, win, wout, layer, f"ffn{which}_l{layer}_fwd")
        acts.append((xin, gv, uv, hb))
        return xo

    x1 = ffn(x, 1, 0)
    hb_mix = _rms_fwd_call(x1, w["nmix"], 0, "ab_norm_fwd")
    proj = _matmul(hb_mix, w["wp"], "ab_in_proj")
    qkv = _conv_fwd(proj, w["conv_w"], "dn_conv_fwd")
    o, sall = _dn_fwd(qkv, proj, w["hp"], "dn_fwd")
    mix = _mix_fwd(o, proj, w["dn_norm"], w["sg_norm"], w["sg_w"], w["sg_bt"], "ab_gate_fwd")
    x2 = _matmul(mix, w["wo"], "ab_out_proj", res=x1)
    x3 = ffn(x2, 2, 0)
    x4 = ffn(x3, 1, 1)
    x5 = _pool_fwd(x4, w["nmix"], w["pool_w"], w["pool_scale"], 1, "pool_fwd")
    x6 = ffn(x5, 2, 1)
    loss, dx, g["fn"] = _loss_head(x6, target, w["fn"], "loss_head")

    dn = {1: [None, None], 2: [None, None]}
    dwin = {1: [None, None], 2: [None, None]}
    dwout = {1: [None, None], 2: [None, None]}

    def ffn_back(dxo, which, layer, saved):
        nw, win, wout = ffn_w[which]
        xin, gv, uv, hb = saved
        tag = f"ffn{which}_l{layer}"
        dxi, dg, du, a, dyb, dnw = _ffn_bwd(dxo, xin, nw, gv, uv, win, wout, layer, f"{tag}_bwd")
        dn[which][layer] = dnw
        dwin[which][layer], dwout[which][layer] = _ffn_weight_grads(hb, dg, du, a, dyb, tag)
        return dxi

    dx = ffn_back(dx, 2, 1, acts[3])
    dx, g["pool_w"], g["pool_scale"], dnmix1 = _pool_bwd(dx, x4, w["nmix"], w["pool_w"], w["pool_scale"], 1, "pool_bwd")
    dx = ffn_back(dx, 1, 1, acts[2])
    dx2 = ffn_back(dx, 2, 0, acts[1])
    dmix = _matmul(dx2, w["wo"], "ab_out_proj_bwd", trans_b=True)
    g["wo"] = _matmul_tn(mix, dx2, D_MODEL, D_MODEL, "ab_out_proj_dw")
    do, dzuv, g["dn_norm"], g["sg_norm"], g["sg_w"], g["sg_bt"] = _mix_bwd(
        dmix, o, proj, w["dn_norm"], w["sg_norm"], w["sg_w"], w["sg_bt"], "ab_gate_bwd")
    dq, dk, dv, dba, g["hp"] = _dn_bwd(qkv, proj, w["hp"], sall, do, "dn_bwd")
    dqkv, g["conv_w"] = _conv_bwd(jnp.concatenate([dq, dk, dv], axis=1), proj, w["conv_w"], "dn_conv_bwd")
    dproj = jnp.concatenate([dqkv, dzuv, dba.astype(BF16)], axis=1)
    dh = _matmul(dproj, w["wp"], "ab_in_proj_bwd", trans_b=True)
    g["wp"] = _matmul_tn(hb_mix, dproj, D_MODEL, 640, "ab_in_proj_dw")
    dx1, dnmix0 = _rms_bwd_call(dh, x1, w["nmix"], dx2, 0, "ab_norm_bwd")
    dx0 = ffn_back(dx1, 1, 0, acts[0])

    g["n1"] = jnp.concatenate(dn[1], axis=0)
    g["n2"] = jnp.concatenate(dn[2], axis=0)
    g["nmix"] = jnp.concatenate([dnmix0, dnmix1], axis=0)
    for which in (1, 2):
        g[f"win{which}"] = dwin[which]
        g[f"wout{which}"] = dwout[which]
    return loss, dx0, g


SHARDED = ("ffn1_w_in", "ffn1_w_out", "ffn2_w_in", "ffn2_w_out", "ab_w_in", "ab_w_out", "pool_w", "dn_conv_w", "pool_scale")
REPLICATED = ("ffn_norm1", "mix_norm", "ffn_norm2", "dn_a_log", "dn_dt_bias", "dn_out_norm", "sg_norm", "sg_w", "sg_b", "final_norm")
QKVZ = 4 * DN_HEADS * DN_DIM
N_GATES = 2 * DN_HEADS
IN_PROJ = QKVZ + N_GATES + 2 * SG_GROUPS * SG_DIM


def _kernel_layouts(gathered, rep):
    per_layer = lambda a: a.reshape(a.shape[0], 1, D_MODEL)
    w = {"n1": per_layer(rep["ffn_norm1"]), "nmix": per_layer(rep["mix_norm"]), "n2": per_layer(rep["ffn_norm2"]),
         "win1": gathered["ffn1_w_in"], "wout1": gathered["ffn1_w_out"],
         "win2": gathered["ffn2_w_in"], "wout2": gathered["ffn2_w_out"]}
    ab_in = jnp.transpose(gathered["ab_w_in"][:, 0], (1, 0, 2)).reshape(D_MODEL, IN_PROJ)
    w["wp"] = jnp.concatenate([ab_in[:, :QKVZ], ab_in[:, QKVZ + N_GATES:], ab_in[:, QKVZ:QKVZ + N_GATES],
                               jnp.zeros((D_MODEL, PROJ_W - IN_PROJ), ab_in.dtype)], axis=1)
    w["conv_w"] = jnp.transpose(gathered["dn_conv_w"][:, 0], (1, 0, 2)).reshape(DN_CONV, 3 * DN_HEADS * DN_DIM)
    hp = jnp.zeros((8, 128), F32)
    hp = hp.at[0, DN_HEADS:N_GATES].set(rep["dn_a_log"][0]).at[1, DN_HEADS:N_GATES].set(rep["dn_dt_bias"][0])
    w["hp"] = hp
    w["dn_norm"] = rep["dn_out_norm"]
    w["sg_norm"] = rep["sg_norm"][0]
    w["sg_w"] = rep["sg_w"][0]
    w["sg_bt"] = jnp.zeros((SG_CHUNK, 128), F32).at[:, :SG_GROUPS].set(rep["sg_b"][0].T)
    w["wo"] = gathered["ab_w_out"][:, 0].reshape(D_MODEL, D_MODEL)
    w["pool_w"] = jnp.transpose(gathered["pool_w"][:, 0], (1, 0, 2, 3)).reshape(len(POOL_WINDOWS), POOL_DIM, POOL_DIM)
    w["pool_scale"] = gathered["pool_scale"].reshape(1, D_MODEL)
    w["fn"] = rep["final_norm"].reshape(1, D_MODEL)
    return w


def _grads_by_chip(g):
    wp = g["wp"]
    ab_in = jnp.concatenate([wp[:, :QKVZ], wp[:, IN_PROJ - N_GATES:IN_PROJ], wp[:, QKVZ:IN_PROJ - N_GATES]], axis=1)
    nw = len(POOL_WINDOWS)
    one_layer = {
        "ab_w_in": jnp.transpose(ab_in.reshape(D_MODEL, N_CHIPS, IN_PROJ // N_CHIPS), (1, 0, 2)),
        "ab_w_out": g["wo"].reshape(N_CHIPS, D_MODEL // N_CHIPS, D_MODEL),
        "pool_w": jnp.transpose(g["pool_w"].reshape(nw, N_CHIPS, POOL_DIM // N_CHIPS, POOL_DIM), (1, 0, 2, 3)),
        "dn_conv_w": jnp.transpose(g["conv_w"].reshape(DN_CONV, N_CHIPS, -1), (1, 0, 2)),
        "pool_scale": g["pool_scale"].reshape(N_CHIPS, D_MODEL // N_CHIPS),
    }
    sharded = {}
    for n, per_layer in (("ffn1_w_in", g["win1"]), ("ffn1_w_out", g["wout1"]), ("ffn2_w_in", g["win2"]), ("ffn2_w_out", g["wout2"])):
        sharded[n] = [a.reshape(N_CHIPS, -1) for a in per_layer]
    for n, a in one_layer.items():
        halves = a.reshape(N_CHIPS, 2, -1)
        sharded[n] = [halves[:, 0], halves[:, 1]]
    rep = {
        "ffn_norm1": g["n1"], "mix_norm": g["nmix"], "ffn_norm2": g["n2"],
        "dn_a_log": g["hp"][0:1, DN_HEADS:N_GATES], "dn_dt_bias": g["hp"][1:2, DN_HEADS:N_GATES],
        "dn_out_norm": g["dn_norm"], "sg_norm": g["sg_norm"][None], "sg_w": g["sg_w"][None],
        "sg_b": g["sg_bt"][:, :SG_GROUPS].T[None], "final_norm": g["fn"].reshape(D_MODEL),
    }
    return sharded, rep


def _piece_rows(n_elems):
    rows = -(-n_elems // PACK_LANES)
    return -(-rows // 8) * 8


def _half_sizes(shapes):
    sizes = []
    for n in SHARDED:
        size = 1
        for d in shapes[n]:
            size *= d
        sizes.append(size // 2)
    return sizes


def _pack_rows(shapes):
    rows = sum(_piece_rows(s) for s in _half_sizes(shapes))
    return -(-rows // PACK_ROW_BLOCK) * PACK_ROW_BLOCK


def _pack_by_half(sharded, shapes):
    rows = _pack_rows(shapes)
    pieces = []
    for half in range(2):
        used = 0
        for n in SHARDED:
            flat = sharded[n][half]
            pr = _piece_rows(flat.shape[1])
            flat = jnp.pad(flat, ((0, 0), (0, pr * PACK_LANES - flat.shape[1])))
            pieces.append(flat.reshape(N_CHIPS, pr, PACK_LANES))
            used += pr
        pieces.append(jnp.zeros((N_CHIPS, rows - used, PACK_LANES), F32))
    return jnp.concatenate(pieces, axis=1).reshape(N_CHIPS, 2, rows, PACK_LANES)


def _unpack_halves(full, shapes):
    out, off = {}, 0
    for n, half in zip(SHARDED, _half_sizes(shapes)):
        pr = _piece_rows(half)
        out[n] = full[:, off:off + pr].reshape(2, -1)[:, :half].reshape(shapes[n])
        off += pr
    return out


def _swap_with_sibling(pack, name):
    nchip, _, rows, lanes = pack.shape

    def body(pack_ref, recv_ref, send_sem, recv_sem):
        x, y, c = _mesh_pos()
        cp = pltpu.make_async_remote_copy(pack_ref.at[:, 1 - c], recv_ref, send_sem, recv_sem,
                                          device_id=(x, y, 1 - c), device_id_type=MESH)
        cp.start()
        cp.wait()

    return pl.pallas_call(
        body, name=name, in_specs=[ANY], out_specs=ANY,
        out_shape=jax.ShapeDtypeStruct((nchip, rows, lanes), pack.dtype),
        scratch_shapes=[pltpu.SemaphoreType.DMA, pltpu.SemaphoreType.DMA],
        compiler_params=pltpu.CompilerParams(has_side_effects=True),
    )(pack)


def _add_pair(pack, recv, core, name):
    nchip, _, rows, lanes = pack.shape

    def body(c_ref, a_ref, b_ref, o32_ref, o16_ref):
        s = a_ref[...] + b_ref[...]
        o32_ref[...] = s
        o16_ref[...] = _bf(s)

    blk = pl.BlockSpec((None, PACK_ROW_BLOCK, lanes), lambda p, i, c: (p, i, 0))
    return pl.pallas_call(
        body, name=name,
        grid_spec=pltpu.PrefetchScalarGridSpec(
            num_scalar_prefetch=1, grid=(nchip, rows // PACK_ROW_BLOCK),
            in_specs=[pl.BlockSpec((None, None, PACK_ROW_BLOCK, lanes), lambda p, i, c: (p, c[0], i, 0)), blk],
            out_specs=[blk, blk]),
        out_shape=[jax.ShapeDtypeStruct((nchip, rows, lanes), F32), jax.ShapeDtypeStruct((nchip, rows, lanes), BF16)],
        compiler_params=_params(("parallel", "parallel")),
    )(core, pack, recv)


def _scatter_to_chips(part16, name):
    nchip, rows, lanes = part16.shape

    def body(src_ref, recv_ref, send_sems, recv_sems):
        x, y, c = _mesh_pos()
        copies = []
        for k, (px, py) in enumerate(_other_chips(x, y)):
            cp = pltpu.make_async_remote_copy(src_ref.at[2 * px + py], recv_ref.at[k], send_sems.at[k], recv_sems.at[k],
                                              device_id=(px, py, c), device_id_type=MESH)
            cp.start()
            copies.append(cp)
        for cp in copies:
            cp.wait()

    return pl.pallas_call(
        body, name=name, in_specs=[ANY], out_specs=ANY,
        out_shape=jax.ShapeDtypeStruct((nchip - 1, rows, lanes), part16.dtype),
        scratch_shapes=[pltpu.SemaphoreType.DMA((nchip - 1,)), pltpu.SemaphoreType.DMA((nchip - 1,))],
        compiler_params=pltpu.CompilerParams(has_side_effects=True),
    )(part16)


def _sum_chips(part32, recv16, chip, name):
    nchip, rows, lanes = part32.shape

    def body(p_ref, own_ref, r_ref, o_ref):
        s = own_ref[...]
        for k in range(nchip - 1):
            s = s + r_ref[k].astype(F32)
        o_ref[...] = s

    return pl.pallas_call(
        body, name=name,
        grid_spec=pltpu.PrefetchScalarGridSpec(
            num_scalar_prefetch=1, grid=(rows // PACK_ROW_BLOCK,),
            in_specs=[pl.BlockSpec((None, PACK_ROW_BLOCK, lanes), lambda i, p: (p[0], i, 0)),
                      pl.BlockSpec((nchip - 1, PACK_ROW_BLOCK, lanes), lambda i, p: (0, i, 0))],
            out_specs=pl.BlockSpec((PACK_ROW_BLOCK, lanes), lambda i, p: (i, 0))),
        out_shape=jax.ShapeDtypeStruct((rows, lanes), F32),
        compiler_params=_params(("parallel",)),
    )(chip, part32, recv16)


def _share_with_sibling(half, name):
    rows, lanes = half.shape

    def body(h_ref, full_ref, send_sem, recv_sem, local_sem):
        x, y, c = _mesh_pos()
        loc = pltpu.make_async_copy(h_ref, full_ref.at[c], local_sem)
        loc.start()
        cp = pltpu.make_async_remote_copy(h_ref, full_ref.at[c], send_sem, recv_sem,
                                          device_id=(x, y, 1 - c), device_id_type=MESH)
        cp.start()
        pltpu.make_async_remote_copy(h_ref, full_ref.at[1 - c], send_sem, recv_sem,
                                     device_id=(x, y, 1 - c), device_id_type=MESH).wait_recv()
        cp.wait_send()
        loc.wait()

    return pl.pallas_call(
        body, name=name, in_specs=[ANY], out_specs=ANY,
        out_shape=jax.ShapeDtypeStruct((2, rows, lanes), half.dtype),
        scratch_shapes=[pltpu.SemaphoreType.DMA, pltpu.SemaphoreType.DMA, pltpu.SemaphoreType.DMA],
        compiler_params=pltpu.CompilerParams(has_side_effects=True),
    )(half)


def _reduce_sharded(sharded, shapes):
    x, y, c = _mesh_pos()
    core = jnp.reshape(c, (1,)).astype(jnp.int32)
    chip = jnp.reshape(2 * x + y, (1,)).astype(jnp.int32)
    pack = _pack_by_half(sharded, shapes)
    recv = _swap_with_sibling(pack, "grad_pair_swap")
    part32, part16 = _add_pair(pack, recv, core, "grad_pair_add")
    recv16 = _scatter_to_chips(part16, "grad_chip_scatter")
    half = _sum_chips(part32, recv16, chip, "grad_chip_sum")
    full = _share_with_sibling(half, "grad_pair_share")
    return _unpack_halves(full, shapes)


def _pack_small(vals):
    parts = []
    for n in REPLICATED:
        flat = vals[n].reshape(-1)
        rows = -(-flat.shape[0] // 128)
        rows = -(-rows // 8) * 8
        parts.append(jnp.pad(flat, (0, rows * 128 - flat.shape[0])).reshape(rows, 128))
    return jnp.concatenate(parts, axis=0)


def _unpack_small(pack, like):
    out, off = {}, 0
    for n in REPLICATED:
        size = like[n].size
        rows = -(-size // 128)
        rows = -(-rows // 8) * 8
        out[n] = pack[off:off + rows].reshape(-1)[:size].reshape(like[n].shape)
        off += rows
    return out


def _all_to_all_small(pack, name):
    rows, lanes = pack.shape
    flips = [(dx, dy, dc) for dx in (0, 1) for dy in (0, 1) for dc in (0, 1)][1:]

    def body(src_ref, out_ref, send_sems, recv_sems, local_sem):
        x, y, c = _mesh_pos()
        me = 4 * x + 2 * y + c
        loc = pltpu.make_async_copy(src_ref, out_ref.at[me], local_sem)
        loc.start()
        copies = []
        for k, (dx, dy, dc) in enumerate(flips):
            peer = (x ^ dx, y ^ dy, c ^ dc)
            cp = pltpu.make_async_remote_copy(src_ref, out_ref.at[me], send_sems.at[k], recv_sems.at[k],
                                              device_id=peer, device_id_type=MESH)
            cp.start()
            copies.append(cp)
        for k, (dx, dy, dc) in enumerate(flips):
            peer = (x ^ dx, y ^ dy, c ^ dc)
            pltpu.make_async_remote_copy(src_ref, out_ref.at[4 * peer[0] + 2 * peer[1] + peer[2]], send_sems.at[k],
                                         recv_sems.at[k], device_id=peer, device_id_type=MESH).wait_recv()
        for cp in copies:
            cp.wait_send()
        loc.wait()

    return pl.pallas_call(
        body, name=name, in_specs=[ANY], out_specs=ANY,
        out_shape=jax.ShapeDtypeStruct((8, rows, lanes), pack.dtype),
        scratch_shapes=[pltpu.SemaphoreType.DMA((7,)), pltpu.SemaphoreType.DMA((7,)), pltpu.SemaphoreType.DMA],
        compiler_params=pltpu.CompilerParams(has_side_effects=True),
    )(pack)


def _sum_devices(stack, name):
    ndev, rows, lanes = stack.shape

    def body(s_ref, o_ref):
        s = s_ref[0]
        for d in range(1, ndev):
            s = s + s_ref[d]
        o_ref[...] = s

    return pl.pallas_call(
        body, name=name, grid=(1,),
        in_specs=[pl.BlockSpec((ndev, rows, lanes), lambda i: (0, 0, 0))],
        out_specs=pl.BlockSpec((rows, lanes), lambda i: (0, 0)),
        out_shape=jax.ShapeDtypeStruct((rows, lanes), F32),
    )(stack)


WEIGHT_ORDER = ("ffn_norm1", "ffn1_w_in", "ffn1_w_out", "mix_norm", "ffn_norm2", "ffn2_w_in", "ffn2_w_out", "ab_w_in",
                "dn_conv_w", "dn_a_log", "dn_dt_bias", "dn_out_norm", "sg_norm", "sg_w", "sg_b", "ab_w_out", "pool_w",
                "pool_scale", "final_norm")
MATRICES = ("ffn1_w_in", "ffn1_w_out", "ffn2_w_in", "ffn2_w_out", "ab_w_in", "ab_w_out", "pool_w")


def _as_2d(a):
    return a.reshape(-1, a.shape[-1])


def kernel(x, ffn_norm1, ffn1_w_in, ffn1_w_out, mix_norm, ffn_norm2, ffn2_w_in, ffn2_w_out, ab_w_in, dn_conv_w, dn_a_log, dn_dt_bias, dn_out_norm, sg_norm, sg_w, sg_b, ab_w_out, pool_w, pool_scale, final_norm, loss_target, m_ffn_norm1, m_ffn1_w_in, m_ffn1_w_out, m_mix_norm, m_ffn_norm2, m_ffn2_w_in, m_ffn2_w_out, m_ab_w_in, m_dn_conv_w, m_dn_a_log, m_dn_dt_bias, m_dn_out_norm, m_sg_norm, m_sg_w, m_sg_b, m_ab_w_out, m_pool_w, m_pool_scale, m_final_norm, v_ffn_norm1, v_ffn1_w_in, v_ffn1_w_out, v_mix_norm, v_ffn_norm2, v_ffn2_w_in, v_ffn2_w_out, v_ab_w_in, v_dn_conv_w, v_dn_a_log, v_dn_dt_bias, v_dn_out_norm, v_sg_norm, v_sg_w, v_sg_b, v_ab_w_out, v_pool_w, v_pool_scale, v_final_norm):
    given = dict(locals())
    wts = {n: given[n] for n in WEIGHT_ORDER}
    mom_m = {n: given["m_" + n] for n in WEIGHT_ORDER}
    mom_v = {n: given["v_" + n] for n in WEIGHT_ORDER}

    shards = [wts[n].astype(BF16) if n in MATRICES else wts[n] for n in SHARDED]
    gathered = dict(zip(SHARDED, _all_gather_chips(shards, [n in MATRICES for n in SHARDED], "weight_all_gather")))
    rep = {n: wts[n] for n in REPLICATED}
    w = _kernel_layouts(gathered, rep)

    loss, dx, g = _local_step(x[0], loss_target[0], w)
    g_sharded, g_rep = _grads_by_chip(g)

    grads = _reduce_sharded(g_sharded, {n: wts[n].shape for n in SHARDED})
    small = _sum_devices(_all_to_all_small(_pack_small(g_rep), "grad_small_exchange"), "grad_small_sum")
    grads.update(_unpack_small(small, rep))

    delta, new_m, new_v = {}, {}, {}
    for n in SHARDED:
        d, m1, v1 = _adamw(_as_2d(wts[n]), _as_2d(grads[n]), _as_2d(mom_m[n]), _as_2d(mom_v[n]), f"adamw_{n}")
        delta[n], new_m[n], new_v[n] = (t.reshape(wts[n].shape) for t in (d, m1, v1))
    d, m1, v1 = _adamw(_pack_small(rep), small, _pack_small({n: mom_m[n] for n in REPLICATED}),
                       _pack_small({n: mom_v[n] for n in REPLICATED}), "adamw_replicated")
    for tgt, packed in ((delta, d), (new_m, m1), (new_v, v1)):
        tgt.update(_unpack_small(packed, rep))

    total = lax.psum(loss[0, 0], ("x", "y", "c"))
    outs = [total, dx[None]]
    for group in (grads, delta, new_m, new_v):
        outs.extend(group[n] for n in WEIGHT_ORDER)
    return tuple(outs)
```

```python
import functools

import jax
import jax.numpy as jnp
from jax import lax
from jax.experimental import pallas as pl
from jax.experimental.pallas import tpu as pltpu

F32, BF16 = jnp.float32, jnp.bfloat16
NORM_EPS = 1e-6
D_MODEL = 1024
D_FF = 2816
N_CHIPS = 4
FF_CHUNK = 2 * D_FF // N_CHIPS
DN_HEADS, DN_DIM, DN_CHUNK, DN_CONV = 4, 128, 64, 4
DN_PREP_CHUNKS = 2
DN_SCAN_CHUNKS = 8
SG_GROUPS, SG_DIM, SG_CHUNK = 4, 128, 128
POOL_WINDOWS = (2, 4, 8, 16)
POOL_DIM = 256
POOL_HALO = 16
CONV_HALO = 8
PROJ_W = 3200
BA_BLOCK = 3072 // 128
ADAM_LR, ADAM_B1, ADAM_B2, ADAM_EPS, ADAM_WD, ADAM_STEP = 0.001, 0.9, 0.999, 1e-08, 0.01, 10
VMEM_BIG = 52 * 1024 * 1024
PACK_LANES = 1024
PACK_ROW_BLOCK = 256
MESH = pl.DeviceIdType.MESH
HI = lax.Precision.HIGHEST
ANY = pl.BlockSpec(memory_space=pl.ANY)


def _params(sem=None, vmem=None):
    return pltpu.CompilerParams(dimension_semantics=sem, vmem_limit_bytes=vmem)


def _dot(a, b):
    return jnp.dot(a, b, preferred_element_type=F32)


def _dot_nt(a, b):
    return lax.dot_general(a, b, (((1,), (1,)), ((), ())), preferred_element_type=F32)


def _dot_tn(a, b):
    return lax.dot_general(a, b, (((0,), (0,)), ((), ())), preferred_element_type=F32)


def _dot_hi(a, b):
    return jnp.dot(a, b, preferred_element_type=F32, precision=HI)


def _dot_mid(a, b):
    return jnp.dot(a, b, preferred_element_type=F32, precision=lax.Precision.HIGH)


def _bf(a):
    return a.astype(BF16)


def _rms_stats(x):
    r = lax.rsqrt(jnp.mean(x * x, axis=-1, keepdims=True) + NORM_EPS)
    return x * r, r


def _rms_bwd(dh, xhat, r, w):
    dhn = dh * w
    return r * (dhn - xhat * jnp.mean(dhn * xhat, axis=-1, keepdims=True))


def _sigmoid(x):
    return jax.nn.sigmoid(x)


def _silu_grad(x, s):
    return s * (1.0 + x * (1.0 - s))


def _gelu(x):
    return 0.5 * x * (1.0 + lax.erf(x * 0.7071067811865476))


def _gelu_grad(x):
    return 0.5 * (1.0 + lax.erf(x * 0.7071067811865476)) + x * jnp.exp(-0.5 * x * x) * 0.3989422804014327


def _softplus(x):
    return jnp.maximum(x, 0.0) + jnp.log(1.0 + jnp.exp(-jnp.abs(x)))


def _tile(n, pref):
    t = min(n, pref)
    assert n % t == 0, (n, t)
    return t


def _ffn_fwd(x, nw, win, wout, layer, name):
    T, D = x.shape
    tm = _tile(T, 512)
    nj = N_CHIPS // 2

    def body(x_ref, n_ref, wg_ref, wu_ref, wo_ref, xo_ref, g_ref, u_ref, hb_ref, h_s, acc_s):
        j = pl.program_id(1)

        @pl.when(j == 0)
        def _():
            xhat, _ = _rms_stats(x_ref[...])
            h = _bf(xhat * n_ref[...])
            h_s[...] = h
            hb_ref[...] = h
            acc_s[...] = jnp.zeros_like(acc_s)

        h = h_s[...]
        g = _dot(h, wg_ref[...])
        u = _dot(h, wu_ref[...])
        a = _bf(g * _sigmoid(g) * u)
        acc_s[...] += _dot(a, wo_ref[...].reshape(FF_CHUNK, D))
        g_ref[...] = _bf(g)
        u_ref[...] = _bf(u)

        @pl.when(j == nj - 1)
        def _():
            xo_ref[...] = x_ref[...] + 0.5 * acc_s[...]

    return pl.pallas_call(
        body, name=name, grid=(T // tm, nj),
        in_specs=[pl.BlockSpec((tm, D), lambda i, j: (i, 0)),
                  pl.BlockSpec((None, 1, D), lambda i, j: (layer, 0, 0)),
                  pl.BlockSpec((None, None, D, FF_CHUNK), lambda i, j: (j, layer, 0, 0)),
                  pl.BlockSpec((None, None, D, FF_CHUNK), lambda i, j: (nj + j, layer, 0, 0)),
                  pl.BlockSpec((2, None, FF_CHUNK // 2, D), lambda i, j: (j, layer, 0, 0))],
        out_specs=[pl.BlockSpec((tm, D), lambda i, j: (i, 0)),
                   pl.BlockSpec((tm, FF_CHUNK), lambda i, j: (i, j)),
                   pl.BlockSpec((tm, FF_CHUNK), lambda i, j: (i, j)),
                   pl.BlockSpec((tm, D), lambda i, j: (i, 0))],
        out_shape=[jax.ShapeDtypeStruct((T, D), F32), jax.ShapeDtypeStruct((T, D_FF), BF16),
                   jax.ShapeDtypeStruct((T, D_FF), BF16), jax.ShapeDtypeStruct((T, D), BF16)],
        scratch_shapes=[pltpu.VMEM((tm, D), BF16), pltpu.VMEM((tm, D), F32)],
        compiler_params=_params(("arbitrary", "arbitrary"), VMEM_BIG),
    )(x, nw, win, win, wout)


def _ffn_bwd(dxo, x, nw, g, u, win, wout, layer, name):
    T, D = x.shape
    tm = _tile(T, 256)
    nj = N_CHIPS // 2

    def body(dxo_ref, x_ref, n_ref, g_ref, u_ref, wg_ref, wu_ref, wo_ref,
             dx_ref, dg_ref, du_ref, a_ref, dyb_ref, dn_ref, dyb_s, dh_s):
        i, j = pl.program_id(0), pl.program_id(1)

        @pl.when(j == 0)
        def _():
            dyb = _bf(0.5 * dxo_ref[...])
            dyb_s[...] = dyb
            dyb_ref[...] = dyb
            dh_s[...] = jnp.zeros_like(dh_s)

        @pl.when((i == 0) & (j == 0))
        def _():
            dn_ref[...] = jnp.zeros_like(dn_ref)

        da = _dot_nt(dyb_s[...], wo_ref[...].reshape(FF_CHUNK, D))
        gv = g_ref[...].astype(F32)
        uv = u_ref[...].astype(F32)
        sg = _sigmoid(gv)
        sl = gv * sg
        dgb = _bf(da * uv * _silu_grad(gv, sg))
        dub = _bf(da * sl)
        a_ref[...] = _bf(sl * uv)
        dg_ref[...] = dgb
        du_ref[...] = dub
        dh_s[...] += _dot_nt(dgb, wg_ref[...]) + _dot_nt(dub, wu_ref[...])

        @pl.when(j == nj - 1)
        def _():
            xhat, r = _rms_stats(x_ref[...])
            dh = dh_s[...]
            dx_ref[...] = dxo_ref[...] + _rms_bwd(dh, xhat, r, n_ref[...])
            dn_ref[...] += jnp.sum(dh * xhat, axis=0, keepdims=True)

    return pl.pallas_call(
        body, name=name, grid=(T // tm, nj),
        in_specs=[pl.BlockSpec((tm, D), lambda i, j: (i, 0)),
                  pl.BlockSpec((tm, D), lambda i, j: (i, 0)),
                  pl.BlockSpec((None, 1, D), lambda i, j: (layer, 0, 0)),
                  pl.BlockSpec((tm, FF_CHUNK), lambda i, j: (i, j)),
                  pl.BlockSpec((tm, FF_CHUNK), lambda i, j: (i, j)),
                  pl.BlockSpec((None, None, D, FF_CHUNK), lambda i, j: (j, layer, 0, 0)),
                  pl.BlockSpec((None, None, D, FF_CHUNK), lambda i, j: (nj + j, layer, 0, 0)),
                  pl.BlockSpec((2, None, FF_CHUNK // 2, D), lambda i, j: (j, layer, 0, 0))],
        out_specs=[pl.BlockSpec((tm, D), lambda i, j: (i, 0)),
                   pl.BlockSpec((tm, FF_CHUNK), lambda i, j: (i, j)),
                   pl.BlockSpec((tm, FF_CHUNK), lambda i, j: (i, j)),
                   pl.BlockSpec((tm, FF_CHUNK), lambda i, j: (i, j)),
                   pl.BlockSpec((tm, D), lambda i, j: (i, 0)),
                   pl.BlockSpec((1, D), lambda i, j: (0, 0))],
        out_shape=[jax.ShapeDtypeStruct((T, D), F32), jax.ShapeDtypeStruct((T, D_FF), BF16),
                   jax.ShapeDtypeStruct((T, D_FF), BF16), jax.ShapeDtypeStruct((T, D_FF), BF16),
                   jax.ShapeDtypeStruct((T, D), BF16), jax.ShapeDtypeStruct((1, D), F32)],
        scratch_shapes=[pltpu.VMEM((tm, D), BF16), pltpu.VMEM((tm, D), F32)],
        compiler_params=_params(("arbitrary", "arbitrary"), VMEM_BIG),
    )(dxo, x, nw, g, u, win, win, wout)


def _matmul_tn(a, b, bm, bn, name, stack_n=False):
    T, M = a.shape
    N = b.shape[1]
    tk = _tile(T, 1024)
    bm, bn = _tile(M, bm), _tile(N, bn)

    def body(a_ref, b_ref, o_ref):
        @pl.when(pl.program_id(2) == 0)
        def _():
            o_ref[...] = jnp.zeros_like(o_ref)

        o_ref[...] += _dot_tn(_bf(a_ref[...]), _bf(b_ref[...]))

    if stack_n:
        out_spec = pl.BlockSpec((None, bm, bn), lambda m, n, k: (n, m, 0))
        out_shape = jax.ShapeDtypeStruct((N // bn, M, bn), F32)
    else:
        out_spec = pl.BlockSpec((bm, bn), lambda m, n, k: (m, n))
        out_shape = jax.ShapeDtypeStruct((M, N), F32)
    return pl.pallas_call(
        body, name=name, grid=(M // bm, N // bn, T // tk),
        in_specs=[pl.BlockSpec((tk, bm), lambda m, n, k: (k, m)),
                  pl.BlockSpec((tk, bn), lambda m, n, k: (k, n))],
        out_specs=out_spec, out_shape=out_shape,
        compiler_params=_params(("parallel", "parallel", "arbitrary"), VMEM_BIG),
    )(a, b)


def _matmul(a, b, name, trans_b=False, res=None, out_dtype=F32):
    T, K = a.shape
    N = b.shape[0] if trans_b else b.shape[1]
    tm = _tile(T, 512)

    def body(*refs):
        a_ref, b_ref = refs[0], refs[1]
        o_ref = refs[-1]
        av, bv = _bf(a_ref[...]), _bf(b_ref[...])
        acc = _dot_nt(av, bv) if trans_b else _dot(av, bv)
        if res is not None:
            acc = acc + refs[2][...]
        o_ref[...] = acc.astype(out_dtype)

    in_specs = [pl.BlockSpec((tm, K), lambda i: (i, 0)), pl.BlockSpec(b.shape, lambda i: (0, 0))]
    args = [a, b]
    if res is not None:
        in_specs.append(pl.BlockSpec((tm, N), lambda i: (i, 0)))
        args.append(res)
    return pl.pallas_call(
        body, name=name, grid=(T // tm,), in_specs=in_specs,
        out_specs=pl.BlockSpec((tm, N), lambda i: (i, 0)),
        out_shape=jax.ShapeDtypeStruct((T, N), out_dtype),
        compiler_params=_params(("parallel",), VMEM_BIG),
    )(*args)


def _rms_fwd_call(x, nw, layer, name):
    T, D = x.shape
    tm = _tile(T, 512)

    def body(x_ref, n_ref, o_ref):
        xhat, _ = _rms_stats(x_ref[...])
        o_ref[...] = _bf(xhat * n_ref[...])

    return pl.pallas_call(
        body, name=name, grid=(T // tm,),
        in_specs=[pl.BlockSpec((tm, D), lambda i: (i, 0)), pl.BlockSpec((None, 1, D), lambda i: (layer, 0, 0))],
        out_specs=pl.BlockSpec((tm, D), lambda i: (i, 0)),
        out_shape=jax.ShapeDtypeStruct((T, D), BF16),
        compiler_params=_params(("parallel",)),
    )(x, nw)


def _rms_bwd_call(dh, x, nw, dres, layer, name):
    T, D = x.shape
    tm = _tile(T, 512)

    def body(dh_ref, x_ref, n_ref, dr_ref, dx_ref, dn_ref):
        @pl.when(pl.program_id(0) == 0)
        def _():
            dn_ref[...] = jnp.zeros_like(dn_ref)

        xhat, r = _rms_stats(x_ref[...])
        dh_v = dh_ref[...]
        dx_ref[...] = dr_ref[...] + _rms_bwd(dh_v, xhat, r, n_ref[...])
        dn_ref[...] += jnp.sum(dh_v * xhat, axis=0, keepdims=True)

    row = pl.BlockSpec((tm, D), lambda i: (i, 0))
    return pl.pallas_call(
        body, name=name, grid=(T // tm,),
        in_specs=[row, row, pl.BlockSpec((None, 1, D), lambda i: (layer, 0, 0)), row],
        out_specs=[row, pl.BlockSpec((1, D), lambda i: (0, 0))],
        out_shape=[jax.ShapeDtypeStruct((T, D), F32), jax.ShapeDtypeStruct((1, D), F32)],
        compiler_params=_params(("arbitrary",)),
    )(dh, x, nw, dres)


def _shift_rows(x, s):
    n = x.shape[0]
    s = s % n
    return x if s == 0 else pltpu.roll(x, s, 0)


def _conv_fwd(proj, conv_w, name):
    T = proj.shape[0]
    C = 3 * DN_HEADS * DN_DIM
    cb = 512
    tm = _tile(T, 512)
    hb = tm // CONV_HALO

    def body(x_ref, xp_ref, w_ref, o_ref):
        i = pl.program_id(1)
        prev = jnp.where(i == 0, 0.0, xp_ref[...])
        ext = jnp.concatenate([prev, x_ref[...]], axis=0)
        w = w_ref[...]
        y = ext * w[DN_CONV - 1:DN_CONV, :]
        for k in range(DN_CONV - 1):
            y = y + _shift_rows(ext, DN_CONV - 1 - k) * w[k:k + 1, :]
        y = y[CONV_HALO:, :]
        o_ref[...] = y * _sigmoid(y)

    return pl.pallas_call(
        body, name=name, grid=(C // cb, T // tm),
        in_specs=[pl.BlockSpec((tm, cb), lambda c, i: (i, c)),
                  pl.BlockSpec((CONV_HALO, cb), lambda c, i: (jnp.maximum(i * hb - 1, 0), c)),
                  pl.BlockSpec((DN_CONV, cb), lambda c, i: (0, c))],
        out_specs=pl.BlockSpec((tm, cb), lambda c, i: (i, c)),
        out_shape=jax.ShapeDtypeStruct((T, C), F32),
        compiler_params=_params(("parallel", "parallel")),
    )(proj, proj, conv_w)


def _conv_bwd(dy, proj, conv_w, name):
    T = proj.shape[0]
    C = 3 * DN_HEADS * DN_DIM
    cb = 512
    tm = _tile(T, 512)
    hb = tm // CONV_HALO
    nt = T // tm

    def body(x_ref, xp_ref, xn_ref, dy_ref, dyn_ref, w_ref, dx_ref, dw_ref):
        i = pl.program_id(1)

        @pl.when(i == 0)
        def _():
            dw_ref[...] = jnp.zeros_like(dw_ref)

        prev = jnp.where(i == 0, 0.0, xp_ref[...])
        ext = jnp.concatenate([prev, x_ref[...], xn_ref[...]], axis=0)
        dy_ext = jnp.concatenate([jnp.zeros((CONV_HALO, cb), F32), dy_ref[...],
                                  jnp.where(i == nt - 1, 0.0, dyn_ref[...])], axis=0)
        w = w_ref[...]
        shifted = [_shift_rows(ext, DN_CONV - 1 - k) for k in range(DN_CONV)]
        y = shifted[0] * w[0:1, :]
        for k in range(1, DN_CONV):
            y = y + shifted[k] * w[k:k + 1, :]
        s = _sigmoid(y)
        dpre = dy_ext * _silu_grad(y, s)
        dx = dpre * w[DN_CONV - 1:DN_CONV, :]
        for k in range(DN_CONV - 1):
            dx = dx + _shift_rows(dpre, -(DN_CONV - 1 - k)) * w[k:k + 1, :]
        dx_ref[...] = _bf(dx[CONV_HALO:CONV_HALO + tm, :])
        rows = [jnp.sum((dpre * shifted[k])[CONV_HALO:CONV_HALO + tm, :], axis=0, keepdims=True) for k in range(DN_CONV)]
        dw_ref[...] += jnp.concatenate(rows, axis=0)

    last_halo = T // CONV_HALO - 1
    return pl.pallas_call(
        body, name=name, grid=(C // cb, nt),
        in_specs=[pl.BlockSpec((tm, cb), lambda c, i: (i, c)),
                  pl.BlockSpec((CONV_HALO, cb), lambda c, i: (jnp.maximum(i * hb - 1, 0), c)),
                  pl.BlockSpec((CONV_HALO, cb), lambda c, i: (jnp.minimum((i + 1) * hb, last_halo), c)),
                  pl.BlockSpec((tm, cb), lambda c, i: (i, c)),
                  pl.BlockSpec((CONV_HALO, cb), lambda c, i: (jnp.minimum((i + 1) * hb, last_halo), c)),
                  pl.BlockSpec((DN_CONV, cb), lambda c, i: (0, c))],
        out_specs=[pl.BlockSpec((tm, cb), lambda c, i: (i, c)),
                   pl.BlockSpec((DN_CONV, cb), lambda c, i: (0, c))],
        out_shape=[jax.ShapeDtypeStruct((T, C), BF16), jax.ShapeDtypeStruct((DN_CONV, C), F32)],
        compiler_params=_params(("parallel", "arbitrary")),
    )(proj, proj, proj, dy, dy, conv_w)


def _unit_lower_inverse(low, eye):
    x = -low
    inv = eye + x
    p = x
    for _ in range(5):
        p = _dot_mid(p, p)
        inv = inv + _dot_mid(inv, p)
    return inv


def _l2_unit(x):
    r = lax.rsqrt(jnp.sum(x * x, axis=-1, keepdims=True) + NORM_EPS)
    return x * r, r


class _ChunkMasks:
    def __init__(self):
        c = DN_CHUNK
        row = lax.broadcasted_iota(jnp.int32, (c, c), 0)
        col = lax.broadcasted_iota(jnp.int32, (c, c), 1)
        self.lower, self.strict_lower = row >= col, row > col
        self.upper, self.strict_upper = row <= col, row < col
        self.eye = (row == col).astype(F32)
        self.ones = jnp.ones((c, c), F32)


def _dn_gates(ba, hp):
    coef = -jnp.exp(hp[0:1, :])
    pre = ba + hp[1:2, :]
    return _sigmoid(ba), coef * _softplus(pre), coef, pre


def _dn_local(mk, qraw, kraw, vh, bc, gc):
    f = {}
    f["qn"], f["rq"] = _l2_unit(qraw)
    qh = f["qn"] * (DN_DIM ** -0.5)
    kh, f["rk"] = _l2_unit(kraw)
    gr = jnp.broadcast_to(gc, (DN_CHUNK, DN_CHUNK)).T
    dec = jnp.where(mk.lower, jnp.exp(jnp.where(mk.lower, gc - gr, 0.0)), 0.0)
    kb = kh * bc
    mkk = _dot_nt(_bf(kb), _bf(kh))
    inv = _unit_lower_inverse(jnp.where(mk.strict_lower, mkk * dec, 0.0), mk.eye)
    eg = jnp.exp(gc)
    sol = _dot_mid(inv, jnp.concatenate([vh * bc, kb * eg], axis=1))
    mqk = _dot_nt(_bf(qh), _bf(kh))
    etl = jnp.exp(gc[DN_CHUNK - 1:DN_CHUNK, :] - gc)
    f.update(qh=qh, kh=kh, gr=gr, dec=dec, kb=kb, mkk=mkk, eg=eg, u=sol[:, :DN_DIM], w=sol[:, DN_DIM:], mqk=mqk,
             attn=mqk * dec, etl=etl, qd=qh * eg, kt=kh * etl)
    return f


def _dn_specs(rows, rev=None):
    at = (lambda n: n) if rev is None else rev
    hw = DN_HEADS * DN_DIM
    return dict(
        qkv=[pl.BlockSpec((rows, hw), lambda n, j=j: (at(n), j)) for j in range(3)],
        ba=pl.BlockSpec((rows, 128), lambda n: (at(n), BA_BLOCK)),
        hp=pl.BlockSpec((8, 128), lambda n: (0, 0)),
        tok=pl.BlockSpec((rows, hw), lambda n: (at(n), 0)),
        attn=pl.BlockSpec((rows, DN_HEADS * DN_CHUNK), lambda n: (at(n), 0)),
        gate=pl.BlockSpec((rows // DN_CHUNK, 8, 128), lambda n: (at(n), 0, 0)),
        state=pl.BlockSpec((rows // DN_CHUNK, DN_HEADS, DN_DIM, DN_DIM), lambda n: (at(n), 0, 0, 0)),
    )


def _dn_prep(qkv, proj, hp, name):
    T = qkv.shape[0]
    n_chunks = T // DN_CHUNK
    group = min(DN_PREP_CHUNKS, n_chunks)
    rows = group * DN_CHUNK
    hw = DN_HEADS * DN_DIM

    def body(q_ref, k_ref, v_ref, ba_ref, hp_ref, u_ref, w_ref, p_ref, qd_ref, kt_ref, gl_ref):
        mk = _ChunkMasks()
        hp_v = hp_ref[...]
        for j in range(group):
            rs = slice(j * DN_CHUNK, (j + 1) * DN_CHUNK)
            beta_t, graw_t, _, _ = _dn_gates(ba_ref[rs, :], hp_v)
            gcum_t = _dot_hi(mk.lower.astype(F32), graw_t)
            gl_ref[j] = jnp.broadcast_to(gcum_t[DN_CHUNK - 1:DN_CHUNK, :], (8, 128))
            for h in range(DN_HEADS):
                sl = slice(h * DN_DIM, (h + 1) * DN_DIM)
                f = _dn_local(mk, q_ref[rs, sl], k_ref[rs, sl], v_ref[rs, sl], beta_t[:, h:h + 1],
                              gcum_t[:, DN_HEADS + h:DN_HEADS + h + 1])
                u_ref[rs, sl] = f["u"]
                w_ref[rs, sl] = _bf(f["w"])
                p_ref[rs, h * DN_CHUNK:(h + 1) * DN_CHUNK] = _bf(f["attn"])
                qd_ref[rs, sl] = _bf(f["qd"])
                kt_ref[rs, sl] = _bf(f["kt"])

    sp = _dn_specs(rows)
    tok16 = jax.ShapeDtypeStruct((T, hw), BF16)
    return pl.pallas_call(
        body, name=name, grid=(n_chunks // group,),
        in_specs=sp["qkv"] + [sp["ba"], sp["hp"]],
        out_specs=[sp["tok"], sp["tok"], sp["attn"], sp["tok"], sp["tok"], sp["gate"]],
        out_shape=[jax.ShapeDtypeStruct((T, hw), F32), tok16, jax.ShapeDtypeStruct((T, DN_HEADS * DN_CHUNK), BF16),
                   tok16, tok16, jax.ShapeDtypeStruct((n_chunks, 8, 128), F32)],
        compiler_params=_params(("parallel",)),
    )(qkv, qkv, qkv, proj, hp)


def _dn_scan(u, w, p, qd, kt, gl, name):
    T = u.shape[0]
    n_chunks = T // DN_CHUNK
    group = min(DN_SCAN_CHUNKS, n_chunks)
    rows = group * DN_CHUNK
    hw = DN_HEADS * DN_DIM

    def body(u_ref, w_ref, p_ref, qd_ref, kt_ref, gl_ref, o_ref, vn_ref, sall_ref, s_s):
        @pl.when(pl.program_id(0) == 0)
        def _():
            s_s[...] = jnp.zeros_like(s_s)

        state = [s_s[h] for h in range(DN_HEADS)]
        for j in range(group):
            rs = slice(j * DN_CHUNK, (j + 1) * DN_CHUNK)
            for h in range(DN_HEADS):
                sl = slice(h * DN_DIM, (h + 1) * DN_DIM)
                sall_ref[j, h] = state[h]
                sb = _bf(state[h])
                vnb = _bf(u_ref[rs, sl] - _dot(w_ref[rs, sl], sb))
                vn_ref[rs, sl] = vnb
                o_ref[rs, sl] = _dot(qd_ref[rs, sl], sb) + _dot(p_ref[rs, h * DN_CHUNK:(h + 1) * DN_CHUNK], vnb)
                egl = jnp.exp(gl_ref[j, 0:1, DN_HEADS + h:DN_HEADS + h + 1])
                state[h] = state[h] * egl + _dot_tn(kt_ref[rs, sl], vnb)
        for h in range(DN_HEADS):
            s_s[h] = state[h]

    sp = _dn_specs(rows)
    return pl.pallas_call(
        body, name=name, grid=(n_chunks // group,),
        in_specs=[sp["tok"], sp["tok"], sp["attn"], sp["tok"], sp["tok"], sp["gate"]],
        out_specs=[sp["tok"], sp["tok"], sp["state"]],
        out_shape=[jax.ShapeDtypeStruct((T, hw), F32), jax.ShapeDtypeStruct((T, hw), BF16),
                   jax.ShapeDtypeStruct((n_chunks, DN_HEADS, DN_DIM, DN_DIM), F32)],
        scratch_shapes=[pltpu.VMEM((DN_HEADS, DN_DIM, DN_DIM), F32)],
        compiler_params=_params(("arbitrary",)),
    )(u, w, p, qd, kt, gl)


def _dn_scan_bwd(w, p, qd, kt, gl, vn, sall, do, name):
    T = w.shape[0]
    n_chunks = T // DN_CHUNK
    group = min(DN_SCAN_CHUNKS, n_chunks)
    rows = group * DN_CHUNK
    hw = DN_HEADS * DN_DIM
    last = n_chunks // group - 1

    def body(w_ref, p_ref, qd_ref, kt_ref, gl_ref, vn_ref, sall_ref, do_ref, dvn_ref, dkt_ref, dgl_ref, ds_s):
        @pl.when(pl.program_id(0) == 0)
        def _():
            ds_s[...] = jnp.zeros_like(ds_s)

        lane = lax.broadcasted_iota(jnp.int32, (8, 128), 1)
        d_state = [ds_s[h] for h in range(DN_HEADS)]
        for j in reversed(range(group)):
            rs = slice(j * DN_CHUNK, (j + 1) * DN_CHUNK)
            dgl_tile = jnp.zeros((8, 128), F32)
            for h in range(DN_HEADS):
                sl = slice(h * DN_DIM, (h + 1) * DN_DIM)
                d_out = _bf(do_ref[rs, sl])
                d_new = d_state[h]
                d_newb = _bf(d_new)
                d_vn = _dot_tn(p_ref[rs, h * DN_CHUNK:(h + 1) * DN_CHUNK], d_out) + _dot(kt_ref[rs, sl], d_newb)
                dvn_ref[rs, sl] = d_vn
                dkt_ref[rs, sl] = _dot_nt(vn_ref[rs, sl], d_newb)
                egl = jnp.exp(gl_ref[j, 0:1, DN_HEADS + h:DN_HEADS + h + 1])
                prod = jnp.sum(d_new * sall_ref[j, h], axis=1, keepdims=True)
                dgl_tile = jnp.where(lane == DN_HEADS + h, jnp.sum(prod, axis=0, keepdims=True) * egl, dgl_tile)
                d_state[h] = d_new * egl + _dot_tn(qd_ref[rs, sl], d_out) - _dot_tn(w_ref[rs, sl], _bf(d_vn))
            dgl_ref[j] = dgl_tile
        for h in range(DN_HEADS):
            ds_s[h] = d_state[h]

    sp = _dn_specs(rows, rev=lambda n: last - n)
    return pl.pallas_call(
        body, name=name, grid=(n_chunks // group,),
        in_specs=[sp["tok"], sp["attn"], sp["tok"], sp["tok"], sp["gate"], sp["tok"], sp["state"], sp["tok"]],
        out_specs=[sp["tok"], sp["tok"], sp["gate"]],
        out_shape=[jax.ShapeDtypeStruct((T, hw), F32), jax.ShapeDtypeStruct((T, hw), F32),
                   jax.ShapeDtypeStruct((n_chunks, 8, 128), F32)],
        scratch_shapes=[pltpu.VMEM((DN_HEADS, DN_DIM, DN_DIM), F32)],
        compiler_params=_params(("arbitrary",)),
    )(w, p, qd, kt, gl, vn, sall, do)


def _dn_prep_bwd(qkv, proj, hp, sall, vn, do, dvn, dkt, dgl, name):
    T = qkv.shape[0]
    n_chunks = T // DN_CHUNK
    group = min(DN_PREP_CHUNKS, n_chunks)
    rows = group * DN_CHUNK
    hw = DN_HEADS * DN_DIM

    def rowsum(x):
        return jnp.sum(x, axis=1, keepdims=True)

    def body(q_ref, k_ref, v_ref, ba_ref, hp_ref, sall_ref, vn_ref, do_ref, dvn_ref, dkt_ref, dgl_ref,
             dq_ref, dk_ref, dv_ref, dba_ref, dhp_ref):
        @pl.when(pl.program_id(0) == 0)
        def _():
            dhp_ref[...] = jnp.zeros_like(dhp_ref)

        mk = _ChunkMasks()
        hp_v = hp_ref[...]
        total = jnp.zeros((8, 128), F32)
        for j in range(group):
            rs = slice(j * DN_CHUNK, (j + 1) * DN_CHUNK)
            total = total + one_chunk(mk, hp_v, *(r.at[rs, :] for r in (q_ref, k_ref, v_ref, ba_ref)), sall_ref.at[j],
                                      *(r.at[rs, :] for r in (vn_ref, do_ref, dvn_ref, dkt_ref)), dgl_ref.at[j],
                                      *(r.at[rs, :] for r in (dq_ref, dk_ref, dv_ref, dba_ref)))
        dhp_ref[...] += total

    def one_chunk(mk, hp_v, q_ref, k_ref, v_ref, ba_ref, state_ref, vn_ref, do_ref, dvn_ref, dkt_ref, dgl_ref,
                  dq_ref, dk_ref, dv_ref, dba_ref):
        ba = ba_ref[...]
        beta_t, graw_t, coef, pre = _dn_gates(ba, hp_v)
        gcum_t = _dot_hi(mk.lower.astype(F32), graw_t)
        lane = lax.broadcasted_iota(jnp.int32, (DN_CHUNK, 128), 1)
        rowi = lax.broadcasted_iota(jnp.int32, (DN_CHUNK, 1), 0)
        dgcum_t = jnp.zeros((DN_CHUNK, 128), F32)
        dbeta_t = jnp.zeros((DN_CHUNK, 128), F32)
        for h in range(DN_HEADS):
            sl = slice(h * DN_DIM, (h + 1) * DN_DIM)
            gc = gcum_t[:, DN_HEADS + h:DN_HEADS + h + 1]
            bc = beta_t[:, h:h + 1]
            state = state_ref[h]
            vh = v_ref[:, sl]
            f = _dn_local(mk, q_ref[:, sl], k_ref[:, sl], vh, bc, gc)
            qh, kh, kb, dec, eg, etl = f["qh"], f["kh"], f["kb"], f["dec"], f["eg"], f["etl"]
            u, w, qd, kt = f["u"], f["w"], f["qd"], f["kt"]
            qb, kbf, kbb = _bf(qh), _bf(kh), _bf(kb)
            dec_t = jnp.where(mk.upper, jnp.exp(jnp.where(mk.upper, f["gr"] - gc, 0.0)), 0.0)
            mkk_t = _dot_nt(kbf, kbb)
            inv_t = _unit_lower_inverse(jnp.where(mk.strict_upper, mkk_t * dec_t, 0.0), mk.eye)
            mqk_t = _dot_nt(kbf, qb)

            d_out = _bf(do_ref[:, sl])
            sb, vnb = _bf(state), vn_ref[:, sl]
            d_qd = _dot_nt(d_out, sb)
            d_attn = _dot_nt(d_out, vnb)
            d_attn_t = _dot_nt(vnb, d_out)
            d_vn = dvn_ref[:, sl]
            d_kt = dkt_ref[:, sl]
            d_w = -_dot_nt(_bf(d_vn), sb)
            d_glast = dgl_ref[0:1, DN_HEADS + h:DN_HEADS + h + 1]
            d_rhs = _dot_mid(inv_t, jnp.concatenate([d_vn, d_w], axis=1))
            d_bu, d_bw = d_rhs[:, :DN_DIM], d_rhs[:, DN_DIM:]
            ub, wb, d_bub, d_bwb = _bf(u), _bf(w), _bf(d_bu), _bf(d_bw)
            d_low = -(_dot_nt(d_bub, ub) + _dot_nt(d_bwb, wb))
            d_low_t = -(_dot_nt(ub, d_bub) + _dot_nt(wb, d_bwb))
            d_mkk = jnp.where(mk.strict_lower, d_low * dec, 0.0)
            d_mkk_t = jnp.where(mk.strict_upper, d_low_t * dec_t, 0.0)
            d_mqk = jnp.where(mk.lower, d_attn * dec, 0.0)
            d_mqk_t = jnp.where(mk.upper, d_attn_t * dec_t, 0.0)
            bw = kb * eg
            d_kb = _dot(_bf(d_mkk), kbf) + d_bw * eg
            d_k = _dot(_bf(d_mkk_t), kbb) + _dot(_bf(d_mqk_t), qb) + d_kt * etl + d_kb * bc
            d_q = _dot(_bf(d_mqk), kbf) + d_qd * eg
            d_beta = rowsum(d_kb * kh) + rowsum(d_bu * vh)
            dv_ref[:, sl] = d_bu * bc
            e_mat = d_mkk * f["mkk"] + d_mqk * f["mqk"]
            e_mat_t = d_mkk_t * mkk_t + d_mqk_t * mqk_t
            kt_term = rowsum(d_kt * kt)
            d_g = rowsum(e_mat) - rowsum(e_mat_t) + rowsum(d_qd * qd) + rowsum(d_bw * bw) - kt_term
            d_glast = d_glast + jnp.sum(kt_term, axis=0, keepdims=True)
            d_g = d_g + jnp.where(rowi == DN_CHUNK - 1, d_glast, 0.0)
            qn = f["qn"]
            d_qs = d_q * (DN_DIM ** -0.5)
            dq_ref[:, sl] = f["rq"] * (d_qs - qn * rowsum(d_qs * qn))
            dk_ref[:, sl] = f["rk"] * (d_k - kh * rowsum(d_k * kh))
            dgcum_t = jnp.where(lane == DN_HEADS + h, d_g, dgcum_t)
            dbeta_t = jnp.where(lane == h, d_beta, dbeta_t)
        dgraw_t = _dot_hi(mk.upper.astype(F32), dgcum_t)
        sp = _sigmoid(pre)
        d_pre = dgraw_t * coef * sp
        dba_ref[...] = jnp.where(lane < DN_HEADS, dbeta_t * beta_t * (1.0 - beta_t),
                                 jnp.where(lane < 2 * DN_HEADS, d_pre, 0.0))
        in_g = (lane >= DN_HEADS) & (lane < 2 * DN_HEADS)
        d_alog = jnp.sum(jnp.where(in_g, dgraw_t * graw_t, 0.0), axis=0, keepdims=True)
        d_dtb = jnp.sum(jnp.where(in_g, d_pre, 0.0), axis=0, keepdims=True)
        return jnp.concatenate([d_alog, d_dtb, jnp.zeros((6, 128), F32)], axis=0)

    sp = _dn_specs(rows)
    return pl.pallas_call(
        body, name=name, grid=(n_chunks // group,),
        in_specs=sp["qkv"] + [sp["ba"], sp["hp"], sp["state"], sp["tok"], sp["tok"], sp["tok"], sp["tok"], sp["gate"]],
        out_specs=[sp["tok"], sp["tok"], sp["tok"], pl.BlockSpec((rows, 128), lambda n: (n, 0)), sp["hp"]],
        out_shape=[jax.ShapeDtypeStruct((T, hw), F32)] * 3
        + [jax.ShapeDtypeStruct((T, 128), F32), jax.ShapeDtypeStruct((8, 128), F32)],
        compiler_params=_params(("arbitrary",)),
    )(qkv, qkv, qkv, proj, hp, sall, vn, do, dvn, dkt, dgl)


def _mix_fwd(o, proj, dn_norm, sg_norm, sg_w, sg_bt, name):
    T = o.shape[0]
    tm = _tile(T, 512)
    hw = DN_HEADS * DN_DIM
    nc = tm // SG_CHUNK

    def body(o_ref, z_ref, su_ref, sv_ref, dnn_ref, sgn_ref, sgw_ref, sgb_ref, mix_ref):
        dnn = dnn_ref[...]
        for h in range(DN_HEADS):
            sl = slice(h * DN_DIM, (h + 1) * DN_DIM)
            xhat, _ = _rms_stats(o_ref[:, sl])
            z = z_ref[:, sl]
            mix_ref[:, sl] = _bf(xhat * dnn * (z * _sigmoid(z)))
        tri = lax.broadcasted_iota(jnp.int32, (SG_CHUNK, SG_CHUNK), 0) >= lax.broadcasted_iota(jnp.int32, (SG_CHUNK, SG_CHUNK), 1)
        for g in range(SG_GROUPS):
            sl = slice(g * SG_DIM, (g + 1) * SG_DIM)
            xhat, _ = _rms_stats(_gelu(sv_ref[:, sl]))
            svn = _bf(xhat * sgn_ref[g:g + 1, :])
            sua = _gelu(su_ref[:, sl])
            wt = _bf(jnp.where(tri, sgw_ref[g], 0.0))
            bias = sgb_ref[:, g:g + 1]
            for c in range(nc):
                rows = slice(c * SG_CHUNK, (c + 1) * SG_CHUNK)
                mixed = _dot(wt, svn[rows, :]) + bias
                mix_ref[rows, hw + g * SG_DIM:hw + (g + 1) * SG_DIM] = _bf(sua[rows, :] * mixed)

    full = lambda shape: pl.BlockSpec(shape, lambda i: (0,) * len(shape))
    return pl.pallas_call(
        body, name=name, grid=(T // tm,),
        in_specs=[pl.BlockSpec((tm, hw), lambda i: (i, 0)),
                  pl.BlockSpec((tm, hw), lambda i: (i, 3)),
                  pl.BlockSpec((tm, hw), lambda i: (i, 4)),
                  pl.BlockSpec((tm, hw), lambda i: (i, 5)),
                  full((1, DN_DIM)), full((SG_GROUPS, SG_DIM)), full((SG_GROUPS, SG_CHUNK, SG_CHUNK)),
                  full((SG_CHUNK, 128))],
        out_specs=pl.BlockSpec((tm, 2 * hw), lambda i: (i, 0)),
        out_shape=jax.ShapeDtypeStruct((T, 2 * hw), BF16),
        compiler_params=_params(("parallel",)),
    )(o, proj, proj, proj, dn_norm, sg_norm, sg_w, sg_bt)


def _mix_bwd(dmix, o, proj, dn_norm, sg_norm, sg_w, sg_bt, name):
    T = o.shape[0]
    tm = _tile(T, 512)
    hw = DN_HEADS * DN_DIM
    nc = tm // SG_CHUNK

    def body(dm_ref, o_ref, z_ref, su_ref, sv_ref, dnn_ref, sgn_ref, sgw_ref, sgb_ref,
             do_ref, dz_ref, ddnn_ref, dsgn_ref, dsgw_ref, dsgb_ref):
        @pl.when(pl.program_id(0) == 0)
        def _():
            ddnn_ref[...] = jnp.zeros_like(ddnn_ref)
            dsgn_ref[...] = jnp.zeros_like(dsgn_ref)
            dsgw_ref[...] = jnp.zeros_like(dsgw_ref)
            dsgb_ref[...] = jnp.zeros_like(dsgb_ref)

        dnn = dnn_ref[...]
        ddnn = jnp.zeros((1, DN_DIM), F32)
        for h in range(DN_HEADS):
            sl = slice(h * DN_DIM, (h + 1) * DN_DIM)
            xhat, r = _rms_stats(o_ref[:, sl])
            z = z_ref[:, sl]
            sz = _sigmoid(z)
            doa = dm_ref[:, sl]
            dyn = doa * (z * sz)
            dz_ref[:, sl] = _bf(doa * xhat * dnn * _silu_grad(z, sz))
            do_ref[:, sl] = _rms_bwd(dyn, xhat, r, dnn)
            ddnn = ddnn + jnp.sum(dyn * xhat, axis=0, keepdims=True)
        ddnn_ref[...] += ddnn
        tri = lax.broadcasted_iota(jnp.int32, (SG_CHUNK, SG_CHUNK), 0) >= lax.broadcasted_iota(jnp.int32, (SG_CHUNK, SG_CHUNK), 1)
        lane = lax.broadcasted_iota(jnp.int32, (SG_CHUNK, 128), 1)
        dsgb = jnp.zeros((SG_CHUNK, 128), F32)
        dsgn_rows = []
        for g in range(SG_GROUPS):
            sl = slice(g * SG_DIM, (g + 1) * SG_DIM)
            sv = sv_ref[:, sl]
            su = su_ref[:, sl]
            xhat, r = _rms_stats(_gelu(sv))
            sgn = sgn_ref[g:g + 1, :]
            svn = _bf(xhat * sgn)
            sua = _gelu(su)
            wt = _bf(jnp.where(tri, sgw_ref[g], 0.0))
            bias = sgb_ref[:, g:g + 1]
            dw = jnp.zeros((SG_CHUNK, SG_CHUNK), F32)
            db = jnp.zeros((SG_CHUNK, 1), F32)
            dsua, dsvn = [], []
            for c in range(nc):
                rows = slice(c * SG_CHUNK, (c + 1) * SG_CHUNK)
                mixed = _dot(wt, svn[rows, :]) + bias
                dob = dm_ref[rows, hw + g * SG_DIM:hw + (g + 1) * SG_DIM]
                dsua.append(dob * mixed)
                dmixed = dob * sua[rows, :]
                dmb = _bf(dmixed)
                dsvn.append(_dot_tn(wt, dmb))
                dw = dw + _dot_nt(dmb, svn[rows, :])
                db = db + jnp.sum(dmixed, axis=1, keepdims=True)
            dsua = jnp.concatenate(dsua, axis=0) if nc > 1 else dsua[0]
            dsvn = jnp.concatenate(dsvn, axis=0) if nc > 1 else dsvn[0]
            dz_ref[:, hw + g * SG_DIM:hw + (g + 1) * SG_DIM] = _bf(dsua * _gelu_grad(su))
            dz_ref[:, 2 * hw + g * SG_DIM:2 * hw + (g + 1) * SG_DIM] = _bf(_rms_bwd(dsvn, xhat, r, sgn) * _gelu_grad(sv))
            dsgn_rows.append(jnp.sum(dsvn * xhat, axis=0, keepdims=True))
            dsgw_ref[g] += jnp.where(tri, dw, 0.0)
            dsgb = jnp.where(lane == g, db, dsgb)
        dsgn_ref[...] += jnp.concatenate(dsgn_rows, axis=0)
        dsgb_ref[...] += dsgb

    full = lambda shape: pl.BlockSpec(shape, lambda i: (0,) * len(shape))
    return pl.pallas_call(
        body, name=name, grid=(T // tm,),
        in_specs=[pl.BlockSpec((tm, 2 * hw), lambda i: (i, 0)),
                  pl.BlockSpec((tm, hw), lambda i: (i, 0)),
                  pl.BlockSpec((tm, hw), lambda i: (i, 3)),
                  pl.BlockSpec((tm, hw), lambda i: (i, 4)),
                  pl.BlockSpec((tm, hw), lambda i: (i, 5)),
                  full((1, DN_DIM)), full((SG_GROUPS, SG_DIM)), full((SG_GROUPS, SG_CHUNK, SG_CHUNK)),
                  full((SG_CHUNK, 128))],
        out_specs=[pl.BlockSpec((tm, hw), lambda i: (i, 0)),
                   pl.BlockSpec((tm, 3 * hw), lambda i: (i, 0)),
                   full((1, DN_DIM)), full((SG_GROUPS, SG_DIM)), full((SG_GROUPS, SG_CHUNK, SG_CHUNK)),
                   full((SG_CHUNK, 128))],
        out_shape=[jax.ShapeDtypeStruct((T, hw), F32), jax.ShapeDtypeStruct((T, 3 * hw), BF16),
                   jax.ShapeDtypeStruct((1, DN_DIM), F32), jax.ShapeDtypeStruct((SG_GROUPS, SG_DIM), F32),
                   jax.ShapeDtypeStruct((SG_GROUPS, SG_CHUNK, SG_CHUNK), F32),
                   jax.ShapeDtypeStruct((SG_CHUNK, 128), F32)],
        compiler_params=_params(("arbitrary",)),
    )(dmix, o, proj, proj, proj, dn_norm, sg_norm, sg_w, sg_bt)


def _window_sums(h, sign):
    sums, s, w = {}, h, 1
    while w < POOL_WINDOWS[-1]:
        s = s + _shift_rows(s, sign * w)
        w *= 2
        sums[w] = s
    return sums


def _pool_counts(t_global):
    return [jnp.minimum(t_global + 1, win).astype(F32) for win in POOL_WINDOWS]


def _pooled_groups(ext_h, row0, tm):
    sums = _window_sums(ext_h, 1)
    t_global = row0 + lax.broadcasted_iota(jnp.int32, (tm, 1), 0)
    counts = _pool_counts(t_global)
    out = []
    for gi, win in enumerate(POOL_WINDOWS):
        cols = slice(gi * POOL_DIM, (gi + 1) * POOL_DIM)
        out.append(sums[win][POOL_HALO:, cols] / counts[gi] - ext_h[POOL_HALO:, cols])
    return out


def _pool_fwd(x, nw, pool_w, pool_scale, layer, name):
    T, D = x.shape
    tm = _tile(T, 256)
    hb = tm // POOL_HALO

    def body(x_ref, xp_ref, n_ref, w_ref, s_ref, xo_ref):
        i = pl.program_id(0)
        prev = jnp.where(i == 0, 0.0, xp_ref[...])
        ext = jnp.concatenate([prev, x_ref[...]], axis=0)
        xhat, _ = _rms_stats(ext)
        pooled = _pooled_groups(xhat * n_ref[...], i * tm, tm)
        for gi in range(len(POOL_WINDOWS)):
            cols = slice(gi * POOL_DIM, (gi + 1) * POOL_DIM)
            xo_ref[:, cols] = x_ref[:, cols] + _dot(_bf(pooled[gi]), w_ref[gi]) * s_ref[:, cols]

    return pl.pallas_call(
        body, name=name, grid=(T // tm,),
        in_specs=[pl.BlockSpec((tm, D), lambda i: (i, 0)),
                  pl.BlockSpec((POOL_HALO, D), lambda i: (jnp.maximum(i * hb - 1, 0), 0)),
                  pl.BlockSpec((None, 1, D), lambda i: (layer, 0, 0)),
                  pl.BlockSpec(pool_w.shape, lambda i: (0, 0, 0)),
                  pl.BlockSpec((1, D), lambda i: (0, 0))],
        out_specs=pl.BlockSpec((tm, D), lambda i: (i, 0)),
        out_shape=jax.ShapeDtypeStruct((T, D), F32),
        compiler_params=_params(("parallel",)),
    )(x, x, nw, pool_w, pool_scale)


def _pool_bwd(dxo, x, nw, pool_w, pool_scale, layer, name):
    T, D = x.shape
    tm = _tile(T, 256)
    hb = tm // POOL_HALO
    nt = T // tm
    ng = len(POOL_WINDOWS)

    def body(dxo_ref, dxn_ref, x_ref, xp_ref, n_ref, w_ref, s_ref, dx_ref, dw_ref, ds_ref, dn_ref):
        i = pl.program_id(0)

        @pl.when(i == 0)
        def _():
            dw_ref[...] = jnp.zeros_like(dw_ref)
            ds_ref[...] = jnp.zeros_like(ds_ref)
            dn_ref[...] = jnp.zeros_like(dn_ref)

        prev = jnp.where(i == 0, 0.0, xp_ref[...])
        ext = jnp.concatenate([prev, x_ref[...]], axis=0)
        xhat_ext, r_ext = _rms_stats(ext)
        nv = n_ref[...]
        pooled = _pooled_groups(xhat_ext * nv, i * tm, tm)
        dxo = dxo_ref[...]
        scale = s_ref[...]
        dout_ext = jnp.concatenate([dxo, jnp.where(i == nt - 1, 0.0, dxn_ref[...])], axis=0) * scale
        t_ext = i * tm + lax.broadcasted_iota(jnp.int32, (tm + POOL_HALO, 1), 0)
        counts = _pool_counts(t_ext)
        dh_cols, ds_cols = [], []
        for gi, win in enumerate(POOL_WINDOWS):
            cols = slice(gi * POOL_DIM, (gi + 1) * POOL_DIM)
            wg = w_ref[gi]
            pb = _bf(pooled[gi])
            doutb = _bf(dout_ext[:, cols])
            dpooled = _dot_nt(doutb, wg)
            ahead = _window_sums(dpooled / counts[gi], -1)[win]
            dh_cols.append(ahead[:tm, :] - dpooled[:tm, :])
            dw_ref[gi] += _dot_tn(pb, doutb[:tm, :])
            ds_cols.append(jnp.sum(dxo[:, cols] * _dot(pb, wg), axis=0, keepdims=True))
        dh = jnp.concatenate(dh_cols, axis=1)
        xhat, r = xhat_ext[POOL_HALO:, :], r_ext[POOL_HALO:, :]
        dx_ref[...] = dxo + _rms_bwd(dh, xhat, r, nv)
        dn_ref[...] += jnp.sum(dh * xhat, axis=0, keepdims=True)
        ds_ref[...] += jnp.concatenate(ds_cols, axis=1)

    last_halo = T // POOL_HALO - 1
    return pl.pallas_call(
        body, name=name, grid=(nt,),
        in_specs=[pl.BlockSpec((tm, D), lambda i: (i, 0)),
                  pl.BlockSpec((POOL_HALO, D), lambda i: (jnp.minimum((i + 1) * hb, last_halo), 0)),
                  pl.BlockSpec((tm, D), lambda i: (i, 0)),
                  pl.BlockSpec((POOL_HALO, D), lambda i: (jnp.maximum(i * hb - 1, 0), 0)),
                  pl.BlockSpec((None, 1, D), lambda i: (layer, 0, 0)),
                  pl.BlockSpec(pool_w.shape, lambda i: (0, 0, 0)),
                  pl.BlockSpec((1, D), lambda i: (0, 0))],
        out_specs=[pl.BlockSpec((tm, D), lambda i: (i, 0)),
                   pl.BlockSpec((ng, POOL_DIM, POOL_DIM), lambda i: (0, 0, 0)),
                   pl.BlockSpec((1, D), lambda i: (0, 0)),
                   pl.BlockSpec((1, D), lambda i: (0, 0))],
        out_shape=[jax.ShapeDtypeStruct((T, D), F32), jax.ShapeDtypeStruct((ng, POOL_DIM, POOL_DIM), F32),
                   jax.ShapeDtypeStruct((1, D), F32), jax.ShapeDtypeStruct((1, D), F32)],
        compiler_params=_params(("arbitrary",)),
    )(dxo, dxo, x, x, nw, pool_w, pool_scale)


def _loss_head(x, target, fn, name):
    T, D = x.shape
    tm = _tile(T, 512)

    def body(x_ref, t_ref, n_ref, loss_ref, dx_ref, dn_ref):
        @pl.when(pl.program_id(0) == 0)
        def _():
            loss_ref[...] = jnp.zeros_like(loss_ref)
            dn_ref[...] = jnp.zeros_like(dn_ref)

        xhat, r = _rms_stats(x_ref[...])
        nv = n_ref[...]
        err = xhat * nv - t_ref[...]
        part = jnp.sum(jnp.sum(err * err, axis=1, keepdims=True), axis=0, keepdims=True)
        loss_ref[...] += 0.5 * part / D
        dy = err / D
        dx_ref[...] = _rms_bwd(dy, xhat, r, nv)
        dn_ref[...] += jnp.sum(dy * xhat, axis=0, keepdims=True)

    row = pl.BlockSpec((tm, D), lambda i: (i, 0))
    return pl.pallas_call(
        body, name=name, grid=(T // tm,),
        in_specs=[row, row, pl.BlockSpec((1, D), lambda i: (0, 0))],
        out_specs=[pl.BlockSpec((1, 1), lambda i: (0, 0)), row, pl.BlockSpec((1, D), lambda i: (0, 0))],
        out_shape=[jax.ShapeDtypeStruct((1, 1), F32), jax.ShapeDtypeStruct((T, D), F32),
                   jax.ShapeDtypeStruct((1, D), F32)],
        compiler_params=_params(("arbitrary",)),
    )(x, target, fn)


def _adamw(w, g, m, v, name):
    R, C = w.shape
    br = R
    for cand in (512, 256, 128, 64, 32, 16, 8):
        if R % cand == 0 and cand * C * 4 <= 2 * 1024 * 1024:
            br = cand
            break

    def body(w_ref, g_ref, m_ref, v_ref, d_ref, mo_ref, vo_ref):
        gv = g_ref[...]
        m_new = ADAM_B1 * m_ref[...] + (1.0 - ADAM_B1) * gv
        v_new = ADAM_B2 * v_ref[...] + (1.0 - ADAM_B2) * (gv * gv)
        m_hat = m_new / (1.0 - ADAM_B1 ** ADAM_STEP)
        v_hat = v_new / (1.0 - ADAM_B2 ** ADAM_STEP)
        d_ref[...] = -ADAM_LR * (m_hat / (jnp.sqrt(v_hat) + ADAM_EPS) + ADAM_WD * w_ref[...])
        mo_ref[...] = m_new
        vo_ref[...] = v_new

    blk = pl.BlockSpec((br, C), lambda i: (i, 0))
    return pl.pallas_call(
        body, name=name, grid=(R // br,), in_specs=[blk] * 4, out_specs=[blk] * 3,
        out_shape=[jax.ShapeDtypeStruct((R, C), F32)] * 3,
        compiler_params=_params(("parallel",)),
    )(w, g, m, v)


def _mesh_pos():
    return lax.axis_index("x"), lax.axis_index("y"), lax.axis_index("c")


def _other_chips(x, y):
    return [(1 - x, y), (x, 1 - y), (1 - x, 1 - y)]


def _half_of(ref, shape, h):
    if shape[0] == 2:
        return ref.at[h]
    size = shape[1] // 2
    return ref.at[:, pl.ds(h * size, size)]


def _all_gather_chips(shards, split, name):
    n = len(shards)

    def body(*refs):
        ins, outs = refs[:n], refs[n:2 * n]
        send_sems, recv_sems, fwd_send_sems, fwd_recv_sems = refs[2 * n:]
        x, y, c = _mesh_pos()
        me = 2 * x + y
        sibling = (x, y, 1 - c)
        chips = _other_chips(x, y)

        def piece(a, ref, h):
            return _half_of(ref, shards[a].shape, h) if split[a] else ref

        sent, forwards = [], []
        for a in range(n):
            for k, (px, py) in enumerate(chips):
                cp = pltpu.make_async_remote_copy(piece(a, ins[a], c), piece(a, outs[a].at[me], c),
                                                  send_sems.at[a, k], recv_sems.at[a, k],
                                                  device_id=(px, py, c), device_id_type=MESH)
                cp.start()
                sent.append(cp)
        for a in range(n):
            for k, (px, py) in enumerate(chips):
                landed = piece(a, outs[a].at[2 * px + py], c)
                pltpu.make_async_remote_copy(landed, landed, send_sems.at[a, k], recv_sems.at[a, k],
                                             device_id=(px, py, c), device_id_type=MESH).wait_recv()
                if split[a]:
                    fwd = pltpu.make_async_remote_copy(landed, landed, fwd_send_sems.at[a, k], fwd_recv_sems.at[a, k],
                                                       device_id=sibling, device_id_type=MESH)
                    fwd.start()
                    forwards.append(fwd)
        for a in range(n):
            if split[a]:
                for k, (px, py) in enumerate(chips):
                    other = piece(a, outs[a].at[2 * px + py], 1 - c)
                    pltpu.make_async_remote_copy(other, other, fwd_send_sems.at[a, k], fwd_recv_sems.at[a, k],
                                                 device_id=sibling, device_id_type=MESH).wait_recv()
        for cp in sent + forwards:
            cp.wait_send()

    sems = pltpu.SemaphoreType.DMA((n, 3))
    gathered = pl.pallas_call(
        body, name=name, in_specs=[ANY] * n, out_specs=[ANY] * n,
        out_shape=[jax.ShapeDtypeStruct((N_CHIPS,) + s.shape, s.dtype) for s in shards],
        scratch_shapes=[sems, sems, sems, sems],
        compiler_params=pltpu.CompilerParams(has_side_effects=True),
    )(*shards)
    x, y, _ = _mesh_pos()
    return [lax.dynamic_update_index_in_dim(g, s, 2 * x + y, 0) for g, s in zip(gathered, shards)]


def _ffn_weight_grads(hb, dg, du, a, dyb, tag):
    dwg = _matmul_tn(hb, dg, D_MODEL, FF_CHUNK, f"{tag}_dw_gate", stack_n=True)
    dwu = _matmul_tn(hb, du, D_MODEL, FF_CHUNK, f"{tag}_dw_up", stack_n=True)
    dwo = _matmul_tn(a, dyb, FF_CHUNK, D_MODEL, f"{tag}_dw_out")
    return jnp.concatenate([dwg, dwu], axis=0), dwo.reshape(N_CHIPS, D_FF // N_CHIPS, D_MODEL)


def _local_step(x, target, w):
    g = {}
    acts = []
    ffn_w = {1: (w["n1"], w["win1"], w["wout1"]), 2: (w["n2"], w["win2"], w["wout2"])}

    def ffn(xin, which, layer):
        nw, win, wout = ffn_w[which]
        xo, gv, uv, hb = _ffn_fwd(xin, nw, win, wout, layer, f"ffn{which}_l{layer}_fwd")
        acts.append((xin, gv, uv, hb))
        return xo

    x1 = ffn(x, 1, 0)
    hb_mix = _rms_fwd_call(x1, w["nmix"], 0, "ab_norm_fwd")
    proj = _matmul(hb_mix, w["wp"], "ab_in_proj")
    qkv = _conv_fwd(proj, w["conv_w"], "dn_conv_fwd")
    dn_u, dn_w, dn_p, dn_qd, dn_kt, dn_gl = _dn_prep(qkv, proj, w["hp"], "dn_prep")
    o, dn_vn, sall = _dn_scan(dn_u, dn_w, dn_p, dn_qd, dn_kt, dn_gl, "dn_scan")
    mix = _mix_fwd(o, proj, w["dn_norm"], w["sg_norm"], w["sg_w"], w["sg_bt"], "ab_gate_fwd")
    x2 = _matmul(mix, w["wo"], "ab_out_proj", res=x1)
    x3 = ffn(x2, 2, 0)
    x4 = ffn(x3, 1, 1)
    x5 = _pool_fwd(x4, w["nmix"], w["pool_w"], w["pool_scale"], 1, "pool_fwd")
    x6 = ffn(x5, 2, 1)
    loss, dx, g["fn"] = _loss_head(x6, target, w["fn"], "loss_head")

    dn = {1: [None, None], 2: [None, None]}
    dwin = {1: [None, None], 2: [None, None]}
    dwout = {1: [None, None], 2: [None, None]}

    def ffn_back(dxo, which, layer, saved):
        nw, win, wout = ffn_w[which]
        xin, gv, uv, hb = saved
        tag = f"ffn{which}_l{layer}"
        dxi, dg, du, a, dyb, dnw = _ffn_bwd(dxo, xin, nw, gv, uv, win, wout, layer, f"{tag}_bwd")
        dn[which][layer] = dnw
        dwin[which][layer], dwout[which][layer] = _ffn_weight_grads(hb, dg, du, a, dyb, tag)
        return dxi

    dx = ffn_back(dx, 2, 1, acts[3])
    dx, g["pool_w"], g["pool_scale"], dnmix1 = _pool_bwd(dx, x4, w["nmix"], w["pool_w"], w["pool_scale"], 1, "pool_bwd")
    dx = ffn_back(dx, 1, 1, acts[2])
    dx2 = ffn_back(dx, 2, 0, acts[1])
    dmix = _matmul(dx2, w["wo"], "ab_out_proj_bwd", trans_b=True)
    g["wo"] = _matmul_tn(mix, dx2, D_MODEL, D_MODEL, "ab_out_proj_dw")
    do, dzuv, g["dn_norm"], g["sg_norm"], g["sg_w"], g["sg_bt"] = _mix_bwd(
        dmix, o, proj, w["dn_norm"], w["sg_norm"], w["sg_w"], w["sg_bt"], "ab_gate_bwd")
    dvn, dkt, dgl = _dn_scan_bwd(dn_w, dn_p, dn_qd, dn_kt, dn_gl, dn_vn, sall, do, "dn_scan_bwd")
    dq, dk, dv, dba, g["hp"] = _dn_prep_bwd(qkv, proj, w["hp"], sall, dn_vn, do, dvn, dkt, dgl, "dn_prep_bwd")
    dqkv, g["conv_w"] = _conv_bwd(jnp.concatenate([dq, dk, dv], axis=1), proj, w["conv_w"], "dn_conv_bwd")
    dproj = jnp.concatenate([dqkv, dzuv, dba.astype(BF16)], axis=1)
    dh = _matmul(dproj, w["wp"], "ab_in_proj_bwd", trans_b=True)
    g["wp"] = _matmul_tn(hb_mix, dproj, D_MODEL, 640, "ab_in_proj_dw")
    dx1, dnmix0 = _rms_bwd_call(dh, x1, w["nmix"], dx2, 0, "ab_norm_bwd")
    dx0 = ffn_back(dx1, 1, 0, acts[0])

    g["n1"] = jnp.concatenate(dn[1], axis=0)
    g["n2"] = jnp.concatenate(dn[2], axis=0)
    g["nmix"] = jnp.concatenate([dnmix0, dnmix1], axis=0)
    for which in (1, 2):
        g[f"win{which}"] = dwin[which]
        g[f"wout{which}"] = dwout[which]
    return loss, dx0, g


SHARDED = ("ffn1_w_in", "ffn1_w_out", "ffn2_w_in", "ffn2_w_out", "ab_w_in", "ab_w_out", "pool_w", "dn_conv_w", "pool_scale")
REPLICATED = ("ffn_norm1", "mix_norm", "ffn_norm2", "dn_a_log", "dn_dt_bias", "dn_out_norm", "sg_norm", "sg_w", "sg_b", "final_norm")
QKVZ = 4 * DN_HEADS * DN_DIM
N_GATES = 2 * DN_HEADS
IN_PROJ = QKVZ + N_GATES + 2 * SG_GROUPS * SG_DIM


def _kernel_layouts(gathered, rep):
    per_layer = lambda a: a.reshape(a.shape[0], 1, D_MODEL)
    w = {"n1": per_layer(rep["ffn_norm1"]), "nmix": per_layer(rep["mix_norm"]), "n2": per_layer(rep["ffn_norm2"]),
         "win1": gathered["ffn1_w_in"], "wout1": gathered["ffn1_w_out"],
         "win2": gathered["ffn2_w_in"], "wout2": gathered["ffn2_w_out"]}
    ab_in = jnp.transpose(gathered["ab_w_in"][:, 0], (1, 0, 2)).reshape(D_MODEL, IN_PROJ)
    w["wp"] = jnp.concatenate([ab_in[:, :QKVZ], ab_in[:, QKVZ + N_GATES:], ab_in[:, QKVZ:QKVZ + N_GATES],
                               jnp.zeros((D_MODEL, PROJ_W - IN_PROJ), ab_in.dtype)], axis=1)
    w["conv_w"] = jnp.transpose(gathered["dn_conv_w"][:, 0], (1, 0, 2)).reshape(DN_CONV, 3 * DN_HEADS * DN_DIM)
    hp = jnp.zeros((8, 128), F32)
    hp = hp.at[0, DN_HEADS:N_GATES].set(rep["dn_a_log"][0]).at[1, DN_HEADS:N_GATES].set(rep["dn_dt_bias"][0])
    w["hp"] = hp
    w["dn_norm"] = rep["dn_out_norm"]
    w["sg_norm"] = rep["sg_norm"][0]
    w["sg_w"] = rep["sg_w"][0]
    w["sg_bt"] = jnp.zeros((SG_CHUNK, 128), F32).at[:, :SG_GROUPS].set(rep["sg_b"][0].T)
    w["wo"] = gathered["ab_w_out"][:, 0].reshape(D_MODEL, D_MODEL)
    w["pool_w"] = jnp.transpose(gathered["pool_w"][:, 0], (1, 0, 2, 3)).reshape(len(POOL_WINDOWS), POOL_DIM, POOL_DIM)
    w["pool_scale"] = gathered["pool_scale"].reshape(1, D_MODEL)
    w["fn"] = rep["final_norm"].reshape(1, D_MODEL)
    return w


def _grads_by_chip(g):
    wp = g["wp"]
    ab_in = jnp.concatenate([wp[:, :QKVZ], wp[:, IN_PROJ - N_GATES:IN_PROJ], wp[:, QKVZ:IN_PROJ - N_GATES]], axis=1)
    nw = len(POOL_WINDOWS)
    one_layer = {
        "ab_w_in": jnp.transpose(ab_in.reshape(D_MODEL, N_CHIPS, IN_PROJ // N_CHIPS), (1, 0, 2)),
        "ab_w_out": g["wo"].reshape(N_CHIPS, D_MODEL // N_CHIPS, D_MODEL),
        "pool_w": jnp.transpose(g["pool_w"].reshape(nw, N_CHIPS, POOL_DIM // N_CHIPS, POOL_DIM), (1, 0, 2, 3)),
        "dn_conv_w": jnp.transpose(g["conv_w"].reshape(DN_CONV, N_CHIPS, -1), (1, 0, 2)),
        "pool_scale": g["pool_scale"].reshape(N_CHIPS, D_MODEL // N_CHIPS),
    }
    sharded = {}
    for n, per_layer in (("ffn1_w_in", g["win1"]), ("ffn1_w_out", g["wout1"]), ("ffn2_w_in", g["win2"]), ("ffn2_w_out", g["wout2"])):
        sharded[n] = [a.reshape(N_CHIPS, -1) for a in per_layer]
    for n, a in one_layer.items():
        halves = a.reshape(N_CHIPS, 2, -1)
        sharded[n] = [halves[:, 0], halves[:, 1]]
    rep = {
        "ffn_norm1": g["n1"], "mix_norm": g["nmix"], "ffn_norm2": g["n2"],
        "dn_a_log": g["hp"][0:1, DN_HEADS:N_GATES], "dn_dt_bias": g["hp"][1:2, DN_HEADS:N_GATES],
        "dn_out_norm": g["dn_norm"], "sg_norm": g["sg_norm"][None], "sg_w": g["sg_w"][None],
        "sg_b": g["sg_bt"][:, :SG_GROUPS].T[None], "final_norm": g["fn"].reshape(D_MODEL),
    }
    return sharded, rep


def _piece_rows(n_elems):
    rows = -(-n_elems // PACK_LANES)
    return -(-rows // 8) * 8


def _half_sizes(shapes):
    sizes = []
    for n in SHARDED:
        size = 1
        for d in shapes[n]:
            size *= d
        sizes.append(size // 2)
    return sizes


def _pack_rows(shapes):
    rows = sum(_piece_rows(s) for s in _half_sizes(shapes))
    return -(-rows // PACK_ROW_BLOCK) * PACK_ROW_BLOCK


def _pack_by_half(sharded, shapes):
    rows = _pack_rows(shapes)
    pieces = []
    for half in range(2):
        used = 0
        for n in SHARDED:
            flat = sharded[n][half]
            pr = _piece_rows(flat.shape[1])
            flat = jnp.pad(flat, ((0, 0), (0, pr * PACK_LANES - flat.shape[1])))
            pieces.append(flat.reshape(N_CHIPS, pr, PACK_LANES))
            used += pr
        pieces.append(jnp.zeros((N_CHIPS, rows - used, PACK_LANES), F32))
    return jnp.concatenate(pieces, axis=1).reshape(N_CHIPS, 2, rows, PACK_LANES)


def _unpack_halves(mine, other, core, shapes):
    out, off = {}, 0
    for n, half in zip(SHARDED, _half_sizes(shapes)):
        pr = _piece_rows(half)
        a, b = mine[off:off + pr], other[off:off + pr]
        both = jnp.stack([jnp.where(core == 0, a, b), jnp.where(core == 0, b, a)])
        out[n] = both.reshape(2, -1)[:, :half].reshape(shapes[n])
        off += pr
    return out


def _swap_with_sibling(pack, name):
    nchip, _, rows, lanes = pack.shape

    def body(pack_ref, recv_ref, send_sem, recv_sem):
        x, y, c = _mesh_pos()
        cp = pltpu.make_async_remote_copy(pack_ref.at[:, 1 - c], recv_ref, send_sem, recv_sem,
                                          device_id=(x, y, 1 - c), device_id_type=MESH)
        cp.start()
        cp.wait()

    return pl.pallas_call(
        body, name=name, in_specs=[ANY], out_specs=ANY,
        out_shape=jax.ShapeDtypeStruct((nchip, rows, lanes), pack.dtype),
        scratch_shapes=[pltpu.SemaphoreType.DMA, pltpu.SemaphoreType.DMA],
        compiler_params=pltpu.CompilerParams(has_side_effects=True),
    )(pack)


def _add_pair(pack, recv, core, name):
    nchip, _, rows, lanes = pack.shape

    def body(c_ref, a_ref, b_ref, o32_ref, o16_ref):
        s = a_ref[...] + b_ref[...]
        o32_ref[...] = s
        o16_ref[...] = _bf(s)

    blk = pl.BlockSpec((None, PACK_ROW_BLOCK, lanes), lambda p, i, c: (p, i, 0))
    return pl.pallas_call(
        body, name=name,
        grid_spec=pltpu.PrefetchScalarGridSpec(
            num_scalar_prefetch=1, grid=(nchip, rows // PACK_ROW_BLOCK),
            in_specs=[pl.BlockSpec((None, None, PACK_ROW_BLOCK, lanes), lambda p, i, c: (p, c[0], i, 0)), blk],
            out_specs=[blk, blk]),
        out_shape=[jax.ShapeDtypeStruct((nchip, rows, lanes), F32), jax.ShapeDtypeStruct((nchip, rows, lanes), BF16)],
        compiler_params=_params(("parallel", "parallel")),
    )(core, pack, recv)


def _scatter_to_chips(part16, name):
    nchip, rows, lanes = part16.shape

    def body(src_ref, recv_ref, send_sems, recv_sems):
        x, y, c = _mesh_pos()
        copies = []
        for k, (px, py) in enumerate(_other_chips(x, y)):
            cp = pltpu.make_async_remote_copy(src_ref.at[2 * px + py], recv_ref.at[k], send_sems.at[k], recv_sems.at[k],
                                              device_id=(px, py, c), device_id_type=MESH)
            cp.start()
            copies.append(cp)
        for cp in copies:
            cp.wait()

    return pl.pallas_call(
        body, name=name, in_specs=[ANY], out_specs=ANY,
        out_shape=jax.ShapeDtypeStruct((nchip - 1, rows, lanes), part16.dtype),
        scratch_shapes=[pltpu.SemaphoreType.DMA((nchip - 1,)), pltpu.SemaphoreType.DMA((nchip - 1,))],
        compiler_params=pltpu.CompilerParams(has_side_effects=True),
    )(part16)


def _sum_chips(part32, recv16, chip, name):
    nchip, rows, lanes = part32.shape

    def body(p_ref, own_ref, r_ref, o_ref):
        s = own_ref[...]
        for k in range(nchip - 1):
            s = s + r_ref[k].astype(F32)
        o_ref[...] = s

    return pl.pallas_call(
        body, name=name,
        grid_spec=pltpu.PrefetchScalarGridSpec(
            num_scalar_prefetch=1, grid=(rows // PACK_ROW_BLOCK,),
            in_specs=[pl.BlockSpec((None, PACK_ROW_BLOCK, lanes), lambda i, p: (p[0], i, 0)),
                      pl.BlockSpec((nchip - 1, PACK_ROW_BLOCK, lanes), lambda i, p: (0, i, 0))],
            out_specs=pl.BlockSpec((PACK_ROW_BLOCK, lanes), lambda i, p: (i, 0))),
        out_shape=jax.ShapeDtypeStruct((rows, lanes), F32),
        compiler_params=_params(("parallel",)),
    )(chip, part32, recv16)


def _share_with_sibling(half, name):
    rows, lanes = half.shape

    def body(h_ref, other_ref, send_sem, recv_sem):
        x, y, c = _mesh_pos()
        cp = pltpu.make_async_remote_copy(h_ref, other_ref, send_sem, recv_sem,
                                          device_id=(x, y, 1 - c), device_id_type=MESH)
        cp.start()
        cp.wait()

    return pl.pallas_call(
        body, name=name, in_specs=[ANY], out_specs=ANY,
        out_shape=jax.ShapeDtypeStruct((rows, lanes), half.dtype),
        scratch_shapes=[pltpu.SemaphoreType.DMA, pltpu.SemaphoreType.DMA],
        compiler_params=pltpu.CompilerParams(has_side_effects=True),
    )(half)


def _reduce_sharded(sharded, shapes):
    x, y, c = _mesh_pos()
    core = jnp.reshape(c, (1,)).astype(jnp.int32)
    chip = jnp.reshape(2 * x + y, (1,)).astype(jnp.int32)
    pack = _pack_by_half(sharded, shapes)
    recv = _swap_with_sibling(pack, "grad_pair_swap")
    part32, part16 = _add_pair(pack, recv, core, "grad_pair_add")
    recv16 = _scatter_to_chips(part16, "grad_chip_scatter")
    half = _sum_chips(part32, recv16, chip, "grad_chip_sum")
    other = _share_with_sibling(half, "grad_pair_share")
    return _unpack_halves(half, other, c, shapes)


def _pack_small(vals):
    parts = []
    for n in REPLICATED:
        flat = vals[n].reshape(-1)
        rows = -(-flat.shape[0] // 128)
        rows = -(-rows // 8) * 8
        parts.append(jnp.pad(flat, (0, rows * 128 - flat.shape[0])).reshape(rows, 128))
    return jnp.concatenate(parts, axis=0)


def _unpack_small(pack, like):
    out, off = {}, 0
    for n in REPLICATED:
        size = like[n].size
        rows = -(-size // 128)
        rows = -(-rows // 8) * 8
        out[n] = pack[off:off + rows].reshape(-1)[:size].reshape(like[n].shape)
        off += rows
    return out


def _all_to_all_small(pack, name):
    rows, lanes = pack.shape
    flips = [(dx, dy, dc) for dx in (0, 1) for dy in (0, 1) for dc in (0, 1)][1:]

    def body(src_ref, out_ref, send_sems, recv_sems, local_sem):
        x, y, c = _mesh_pos()
        me = 4 * x + 2 * y + c
        loc = pltpu.make_async_copy(src_ref, out_ref.at[me], local_sem)
        loc.start()
        copies = []
        for k, (dx, dy, dc) in enumerate(flips):
            peer = (x ^ dx, y ^ dy, c ^ dc)
            cp = pltpu.make_async_remote_copy(src_ref, out_ref.at[me], send_sems.at[k], recv_sems.at[k],
                                              device_id=peer, device_id_type=MESH)
            cp.start()
            copies.append(cp)
        for k, (dx, dy, dc) in enumerate(flips):
            peer = (x ^ dx, y ^ dy, c ^ dc)
            pltpu.make_async_remote_copy(src_ref, out_ref.at[4 * peer[0] + 2 * peer[1] + peer[2]], send_sems.at[k],
                                         recv_sems.at[k], device_id=peer, device_id_type=MESH).wait_recv()
        for cp in copies:
            cp.wait_send()
        loc.wait()

    return pl.pallas_call(
        body, name=name, in_specs=[ANY], out_specs=ANY,
        out_shape=jax.ShapeDtypeStruct((8, rows, lanes), pack.dtype),
        scratch_shapes=[pltpu.SemaphoreType.DMA((7,)), pltpu.SemaphoreType.DMA((7,)), pltpu.SemaphoreType.DMA],
        compiler_params=pltpu.CompilerParams(has_side_effects=True),
    )(pack)


def _sum_devices(stack, name):
    ndev, rows, lanes = stack.shape

    def body(s_ref, o_ref):
        s = s_ref[0]
        for d in range(1, ndev):
            s = s + s_ref[d]
        o_ref[...] = s

    return pl.pallas_call(
        body, name=name, grid=(1,),
        in_specs=[pl.BlockSpec((ndev, rows, lanes), lambda i: (0, 0, 0))],
        out_specs=pl.BlockSpec((rows, lanes), lambda i: (0, 0)),
        out_shape=jax.ShapeDtypeStruct((rows, lanes), F32),
    )(stack)


WEIGHT_ORDER = ("ffn_norm1", "ffn1_w_in", "ffn1_w_out", "mix_norm", "ffn_norm2", "ffn2_w_in", "ffn2_w_out", "ab_w_in",
                "dn_conv_w", "dn_a_log", "dn_dt_bias", "dn_out_norm", "sg_norm", "sg_w", "sg_b", "ab_w_out", "pool_w",
                "pool_scale", "final_norm")
MATRICES = ("ffn1_w_in", "ffn1_w_out", "ffn2_w_in", "ffn2_w_out", "ab_w_in", "ab_w_out", "pool_w")


def _as_2d(a):
    return a.reshape(-1, a.shape[-1])


def kernel(x, ffn_norm1, ffn1_w_in, ffn1_w_out, mix_norm, ffn_norm2, ffn2_w_in, ffn2_w_out, ab_w_in, dn_conv_w, dn_a_log, dn_dt_bias, dn_out_norm, sg_norm, sg_w, sg_b, ab_w_out, pool_w, pool_scale, final_norm, loss_target, m_ffn_norm1, m_ffn1_w_in, m_ffn1_w_out, m_mix_norm, m_ffn_norm2, m_ffn2_w_in, m_ffn2_w_out, m_ab_w_in, m_dn_conv_w, m_dn_a_log, m_dn_dt_bias, m_dn_out_norm, m_sg_norm, m_sg_w, m_sg_b, m_ab_w_out, m_pool_w, m_pool_scale, m_final_norm, v_ffn_norm1, v_ffn1_w_in, v_ffn1_w_out, v_mix_norm, v_ffn_norm2, v_ffn2_w_in, v_ffn2_w_out, v_ab_w_in, v_dn_conv_w, v_dn_a_log, v_dn_dt_bias, v_dn_out_norm, v_sg_norm, v_sg_w, v_sg_b, v_ab_w_out, v_pool_w, v_pool_scale, v_final_norm):
    given = dict(locals())
    wts = {n: given[n] for n in WEIGHT_ORDER}
    mom_m = {n: given["m_" + n] for n in WEIGHT_ORDER}
    mom_v = {n: given["v_" + n] for n in WEIGHT_ORDER}

    shards = [wts[n].astype(BF16) if n in MATRICES else wts[n] for n in SHARDED]
    gathered = dict(zip(SHARDED, _all_gather_chips(shards, [n in MATRICES for n in SHARDED], "weight_all_gather")))
    rep = {n: wts[n] for n in REPLICATED}
    w = _kernel_layouts(gathered, rep)

    loss, dx, g = _local_step(x[0], loss_target[0], w)
    g_sharded, g_rep = _grads_by_chip(g)

    grads = _reduce_sharded(g_sharded, {n: wts[n].shape for n in SHARDED})
    small = _sum_devices(_all_to_all_small(_pack_small(g_rep), "grad_small_exchange"), "grad_small_sum")
    grads.update(_unpack_small(small, rep))

    delta, new_m, new_v = {}, {}, {}
    for n in SHARDED:
        d, m1, v1 = _adamw(_as_2d(wts[n]), _as_2d(grads[n]), _as_2d(mom_m[n]), _as_2d(mom_v[n]), f"adamw_{n}")
        delta[n], new_m[n], new_v[n] = (t.reshape(wts[n].shape) for t in (d, m1, v1))
    d, m1, v1 = _adamw(_pack_small(rep), small, _pack_small({n: mom_m[n] for n in REPLICATED}),
                       _pack_small({n: mom_v[n] for n in REPLICATED}), "adamw_replicated")
    for tgt, packed in ((delta, d), (new_m, m1), (new_v, v1)):
        tgt.update(_unpack_small(packed, rep))

    total = lax.psum(loss[0, 0], ("x", "y", "c"))
    outs = [total, dx[None]]
    for group in (grads, delta, new_m, new_v):
        outs.extend(group[n] for n in WEIGHT_ORDER)
    return tuple(outs)
```

```python
import functools

import jax
import jax.numpy as jnp
from jax import lax
from jax.experimental import pallas as pl
from jax.experimental.pallas import tpu as pltpu

F32, BF16 = jnp.float32, jnp.bfloat16
NORM_EPS = 1e-6
D_MODEL = 1024
D_FF = 2816
N_CHIPS = 4
FF_CHUNK = 2 * D_FF // N_CHIPS
DN_HEADS, DN_DIM, DN_CHUNK, DN_CONV = 4, 128, 64, 4
DN_BLOCK = 2 * DN_CHUNK
DN_PREP_CHUNKS = 4
DN_SCAN_CHUNKS = 8
SG_GROUPS, SG_DIM, SG_CHUNK = 4, 128, 128
POOL_WINDOWS = (2, 4, 8, 16)
POOL_DIM = 256
POOL_HALO = 16
CONV_HALO = 8
PROJ_W = 3200
BA_BLOCK = 3072 // 128
ADAM_LR, ADAM_B1, ADAM_B2, ADAM_EPS, ADAM_WD, ADAM_STEP = 0.001, 0.9, 0.999, 1e-08, 0.01, 10
VMEM_BIG = 52 * 1024 * 1024
FFN_FWD_ROWS = 512
FFN_BWD_ROWS = 256
PACK_LANES = 1024
PACK_ROW_BLOCK = 256
MESH = pl.DeviceIdType.MESH
HI = lax.Precision.HIGHEST
ANY = pl.BlockSpec(memory_space=pl.ANY)


def _params(sem=None, vmem=None):
    return pltpu.CompilerParams(dimension_semantics=sem, vmem_limit_bytes=vmem)


def _dot(a, b):
    return jnp.dot(a, b, preferred_element_type=F32)


def _dot_nt(a, b):
    return lax.dot_general(a, b, (((1,), (1,)), ((), ())), preferred_element_type=F32)


def _dot_tn(a, b):
    return lax.dot_general(a, b, (((0,), (0,)), ((), ())), preferred_element_type=F32)


def _dot_hi(a, b):
    return jnp.dot(a, b, preferred_element_type=F32, precision=HI)


def _dot_mid(a, b):
    return jnp.dot(a, b, preferred_element_type=F32, precision=lax.Precision.HIGH)


def _bf(a):
    return a.astype(BF16)


def _rms_stats(x):
    r = lax.rsqrt(jnp.mean(x * x, axis=-1, keepdims=True) + NORM_EPS)
    return x * r, r


def _rms_bwd(dh, xhat, r, w):
    dhn = dh * w
    return r * (dhn - xhat * jnp.mean(dhn * xhat, axis=-1, keepdims=True))


def _sigmoid(x):
    return jax.nn.sigmoid(x)


def _silu_grad(x, s):
    return s * (1.0 + x * (1.0 - s))


def _gelu(x):
    return 0.5 * x * (1.0 + lax.erf(x * 0.7071067811865476))


def _gelu_grad(x):
    return 0.5 * (1.0 + lax.erf(x * 0.7071067811865476)) + x * jnp.exp(-0.5 * x * x) * 0.3989422804014327


def _softplus(x):
    return jnp.maximum(x, 0.0) + jnp.log(1.0 + jnp.exp(-jnp.abs(x)))


def _tile(n, pref):
    t = min(n, pref)
    assert n % t == 0, (n, t)
    return t


def _ffn_weight_specs(layer):
    once = pl.Buffered(1)
    return [pl.BlockSpec((N_CHIPS, None, D_MODEL, FF_CHUNK), lambda i: (0, layer, 0, 0), pipeline_mode=once),
            pl.BlockSpec((N_CHIPS, None, D_FF // N_CHIPS, D_MODEL), lambda i: (0, layer, 0, 0), pipeline_mode=once)]


def _ffn_fwd(x, nw, win, wout, layer, name):
    T, D = x.shape
    tm = _tile(T, FFN_FWD_ROWS)
    nj = N_CHIPS // 2

    def body(x_ref, n_ref, win_ref, wo_ref, xo_ref, g_ref, u_ref, hb_ref):
        xv = x_ref[...]
        xhat, _ = _rms_stats(xv)
        h = _bf(xhat * n_ref[...])
        hb_ref[...] = h
        acc = None
        for j in range(nj):
            cols = slice(j * FF_CHUNK, (j + 1) * FF_CHUNK)
            g = _dot(h, win_ref[j])
            u = _dot(h, win_ref[nj + j])
            g_ref[:, cols] = _bf(g)
            u_ref[:, cols] = _bf(u)
            part = _dot(_bf(g * _sigmoid(g) * u), wo_ref[2 * j:2 * j + 2].reshape(FF_CHUNK, D))
            acc = part if acc is None else acc + part
        xo_ref[...] = xv + 0.5 * acc

    row = pl.BlockSpec((tm, D), lambda i: (i, 0))
    wide = pl.BlockSpec((tm, D_FF), lambda i: (i, 0))
    return pl.pallas_call(
        body, name=name, grid=(T // tm,),
        in_specs=[row, pl.BlockSpec((None, 1, D), lambda i: (layer, 0, 0))] + _ffn_weight_specs(layer),
        out_specs=[row, wide, wide, row],
        out_shape=[jax.ShapeDtypeStruct((T, D), F32), jax.ShapeDtypeStruct((T, D_FF), BF16),
                   jax.ShapeDtypeStruct((T, D_FF), BF16), jax.ShapeDtypeStruct((T, D), BF16)],
        compiler_params=_params(("parallel",), VMEM_BIG),
    )(x, nw, win, wout)


def _ffn_bwd(dxo, x, nw, g, u, win, wout, layer, name):
    T, D = x.shape
    tm = _tile(T, FFN_BWD_ROWS)
    nj = N_CHIPS // 2

    def body(dxo_ref, x_ref, n_ref, g_ref, u_ref, win_ref, wo_ref, dx_ref, dg_ref, du_ref, a_ref, dyb_ref, dn_ref):
        @pl.when(pl.program_id(0) == 0)
        def _():
            dn_ref[...] = jnp.zeros_like(dn_ref)

        dxo = dxo_ref[...]
        dyb = _bf(0.5 * dxo)
        dyb_ref[...] = dyb
        dh = None
        for j in range(nj):
            cols = slice(j * FF_CHUNK, (j + 1) * FF_CHUNK)
            da = _dot_nt(dyb, wo_ref[2 * j:2 * j + 2].reshape(FF_CHUNK, D))
            gv = g_ref[:, cols].astype(F32)
            uv = u_ref[:, cols].astype(F32)
            sg = _sigmoid(gv)
            sl = gv * sg
            dgb = _bf(da * uv * _silu_grad(gv, sg))
            dub = _bf(da * sl)
            a_ref[:, cols] = _bf(sl * uv)
            dg_ref[:, cols] = dgb
            du_ref[:, cols] = dub
            part = _dot_nt(dgb, win_ref[j]) + _dot_nt(dub, win_ref[nj + j])
            dh = part if dh is None else dh + part
        xhat, r = _rms_stats(x_ref[...])
        dx_ref[...] = dxo + _rms_bwd(dh, xhat, r, n_ref[...])
        dn_ref[...] += jnp.sum(dh * xhat, axis=0, keepdims=True)

    row = pl.BlockSpec((tm, D), lambda i: (i, 0))
    wide = pl.BlockSpec((tm, D_FF), lambda i: (i, 0))
    return pl.pallas_call(
        body, name=name, grid=(T // tm,),
        in_specs=[row, row, pl.BlockSpec((None, 1, D), lambda i: (layer, 0, 0)), wide, wide] + _ffn_weight_specs(layer),
        out_specs=[row, wide, wide, wide, row, pl.BlockSpec((1, D), lambda i: (0, 0))],
        out_shape=[jax.ShapeDtypeStruct((T, D), F32), jax.ShapeDtypeStruct((T, D_FF), BF16),
                   jax.ShapeDtypeStruct((T, D_FF), BF16), jax.ShapeDtypeStruct((T, D_FF), BF16),
                   jax.ShapeDtypeStruct((T, D), BF16), jax.ShapeDtypeStruct((1, D), F32)],
        compiler_params=_params(("arbitrary",), VMEM_BIG),
    )(dxo, x, nw, g, u, win, wout)


def _matmul_tn(a, b, bm, bn, name, stack_n=False):
    T, M = a.shape
    N = b.shape[1]
    tk = _tile(T, 1024)
    bm, bn = _tile(M, bm), _tile(N, bn)

    def body(a_ref, b_ref, o_ref):
        @pl.when(pl.program_id(2) == 0)
        def _():
            o_ref[...] = jnp.zeros_like(o_ref)

        o_ref[...] += _dot_tn(_bf(a_ref[...]), _bf(b_ref[...]))

    if stack_n:
        out_spec = pl.BlockSpec((None, bm, bn), lambda m, n, k: (n, m, 0))
        out_shape = jax.ShapeDtypeStruct((N // bn, M, bn), F32)
    else:
        out_spec = pl.BlockSpec((bm, bn), lambda m, n, k: (m, n))
        out_shape = jax.ShapeDtypeStruct((M, N), F32)
    return pl.pallas_call(
        body, name=name, grid=(M // bm, N // bn, T // tk),
        in_specs=[pl.BlockSpec((tk, bm), lambda m, n, k: (k, m)),
                  pl.BlockSpec((tk, bn), lambda m, n, k: (k, n))],
        out_specs=out_spec, out_shape=out_shape,
        compiler_params=_params(("parallel", "parallel", "arbitrary"), VMEM_BIG),
    )(a, b)


def _matmul(a, b, name, trans_b=False, res=None, out_dtype=F32):
    T, K = a.shape
    N = b.shape[0] if trans_b else b.shape[1]
    tm = _tile(T, 512)

    def body(*refs):
        a_ref, b_ref = refs[0], refs[1]
        o_ref = refs[-1]
        av, bv = _bf(a_ref[...]), _bf(b_ref[...])
        acc = _dot_nt(av, bv) if trans_b else _dot(av, bv)
        if res is not None:
            acc = acc + refs[2][...]
        o_ref[...] = acc.astype(out_dtype)

    in_specs = [pl.BlockSpec((tm, K), lambda i: (i, 0)), pl.BlockSpec(b.shape, lambda i: (0, 0))]
    args = [a, b]
    if res is not None:
        in_specs.append(pl.BlockSpec((tm, N), lambda i: (i, 0)))
        args.append(res)
    return pl.pallas_call(
        body, name=name, grid=(T // tm,), in_specs=in_specs,
        out_specs=pl.BlockSpec((tm, N), lambda i: (i, 0)),
        out_shape=jax.ShapeDtypeStruct((T, N), out_dtype),
        compiler_params=_params(("parallel",), VMEM_BIG),
    )(*args)


def _rms_fwd_call(x, nw, layer, name):
    T, D = x.shape
    tm = _tile(T, 512)

    def body(x_ref, n_ref, o_ref):
        xhat, _ = _rms_stats(x_ref[...])
        o_ref[...] = _bf(xhat * n_ref[...])

    return pl.pallas_call(
        body, name=name, grid=(T // tm,),
        in_specs=[pl.BlockSpec((tm, D), lambda i: (i, 0)), pl.BlockSpec((None, 1, D), lambda i: (layer, 0, 0))],
        out_specs=pl.BlockSpec((tm, D), lambda i: (i, 0)),
        out_shape=jax.ShapeDtypeStruct((T, D), BF16),
        compiler_params=_params(("parallel",)),
    )(x, nw)


def _rms_bwd_call(dh, x, nw, dres, layer, name):
    T, D = x.shape
    tm = _tile(T, 512)

    def body(dh_ref, x_ref, n_ref, dr_ref, dx_ref, dn_ref):
        @pl.when(pl.program_id(0) == 0)
        def _():
            dn_ref[...] = jnp.zeros_like(dn_ref)

        xhat, r = _rms_stats(x_ref[...])
        dh_v = dh_ref[...]
        dx_ref[...] = dr_ref[...] + _rms_bwd(dh_v, xhat, r, n_ref[...])
        dn_ref[...] += jnp.sum(dh_v * xhat, axis=0, keepdims=True)

    row = pl.BlockSpec((tm, D), lambda i: (i, 0))
    return pl.pallas_call(
        body, name=name, grid=(T // tm,),
        in_specs=[row, row, pl.BlockSpec((None, 1, D), lambda i: (layer, 0, 0)), row],
        out_specs=[row, pl.BlockSpec((1, D), lambda i: (0, 0))],
        out_shape=[jax.ShapeDtypeStruct((T, D), F32), jax.ShapeDtypeStruct((1, D), F32)],
        compiler_params=_params(("arbitrary",)),
    )(dh, x, nw, dres)


def _shift_rows(x, s):
    n = x.shape[0]
    s = s % n
    return x if s == 0 else pltpu.roll(x, s, 0)


def _conv_fwd(proj, conv_w, name):
    T = proj.shape[0]
    C = 3 * DN_HEADS * DN_DIM
    cb = 512
    tm = _tile(T, 512)
    hb = tm // CONV_HALO

    def body(x_ref, xp_ref, w_ref, o_ref):
        i = pl.program_id(1)
        prev = jnp.where(i == 0, 0.0, xp_ref[...])
        ext = jnp.concatenate([prev, x_ref[...]], axis=0)
        w = w_ref[...]
        y = ext * w[DN_CONV - 1:DN_CONV, :]
        for k in range(DN_CONV - 1):
            y = y + _shift_rows(ext, DN_CONV - 1 - k) * w[k:k + 1, :]
        y = y[CONV_HALO:, :]
        o_ref[...] = y * _sigmoid(y)

    return pl.pallas_call(
        body, name=name, grid=(C // cb, T // tm),
        in_specs=[pl.BlockSpec((tm, cb), lambda c, i: (i, c)),
                  pl.BlockSpec((CONV_HALO, cb), lambda c, i: (jnp.maximum(i * hb - 1, 0), c)),
                  pl.BlockSpec((DN_CONV, cb), lambda c, i: (0, c))],
        out_specs=pl.BlockSpec((tm, cb), lambda c, i: (i, c)),
        out_shape=jax.ShapeDtypeStruct((T, C), F32),
        compiler_params=_params(("parallel", "parallel")),
    )(proj, proj, conv_w)


def _conv_bwd(dy, proj, conv_w, name):
    T = proj.shape[0]
    C = 3 * DN_HEADS * DN_DIM
    cb = 512
    tm = _tile(T, 512)
    hb = tm // CONV_HALO
    nt = T // tm

    def body(x_ref, xp_ref, xn_ref, dy_ref, dyn_ref, w_ref, dx_ref, dw_ref):
        i = pl.program_id(1)

        @pl.when(i == 0)
        def _():
            dw_ref[...] = jnp.zeros_like(dw_ref)

        prev = jnp.where(i == 0, 0.0, xp_ref[...])
        ext = jnp.concatenate([prev, x_ref[...], xn_ref[...]], axis=0)
        dy_ext = jnp.concatenate([jnp.zeros((CONV_HALO, cb), F32), dy_ref[...],
                                  jnp.where(i == nt - 1, 0.0, dyn_ref[...])], axis=0)
        w = w_ref[...]
        shifted = [_shift_rows(ext, DN_CONV - 1 - k) for k in range(DN_CONV)]
        y = shifted[0] * w[0:1, :]
        for k in range(1, DN_CONV):
            y = y + shifted[k] * w[k:k + 1, :]
        s = _sigmoid(y)
        dpre = dy_ext * _silu_grad(y, s)
        dx = dpre * w[DN_CONV - 1:DN_CONV, :]
        for k in range(DN_CONV - 1):
            dx = dx + _shift_rows(dpre, -(DN_CONV - 1 - k)) * w[k:k + 1, :]
        dx_ref[...] = _bf(dx[CONV_HALO:CONV_HALO + tm, :])
        rows = [jnp.sum((dpre * shifted[k])[CONV_HALO:CONV_HALO + tm, :], axis=0, keepdims=True) for k in range(DN_CONV)]
        dw_ref[...] += jnp.concatenate(rows, axis=0)

    last_halo = T // CONV_HALO - 1
    return pl.pallas_call(
        body, name=name, grid=(C // cb, nt),
        in_specs=[pl.BlockSpec((tm, cb), lambda c, i: (i, c)),
                  pl.BlockSpec((CONV_HALO, cb), lambda c, i: (jnp.maximum(i * hb - 1, 0), c)),
                  pl.BlockSpec((CONV_HALO, cb), lambda c, i: (jnp.minimum((i + 1) * hb, last_halo), c)),
                  pl.BlockSpec((tm, cb), lambda c, i: (i, c)),
                  pl.BlockSpec((CONV_HALO, cb), lambda c, i: (jnp.minimum((i + 1) * hb, last_halo), c)),
                  pl.BlockSpec((DN_CONV, cb), lambda c, i: (0, c))],
        out_specs=[pl.BlockSpec((tm, cb), lambda c, i: (i, c)),
                   pl.BlockSpec((DN_CONV, cb), lambda c, i: (0, c))],
        out_shape=[jax.ShapeDtypeStruct((T, C), BF16), jax.ShapeDtypeStruct((DN_CONV, C), F32)],
        compiler_params=_params(("parallel", "arbitrary")),
    )(proj, proj, proj, dy, dy, conv_w)


def _unit_lower_inverse(low, eye):
    p1 = -low
    p2 = _dot_mid(p1, p1)
    p4 = _dot_mid(p2, p2)
    a = eye + p1 + p2 + _dot_mid(p1, p2)
    p8 = _dot_mid(p4, p4)
    p16 = _dot_mid(p8, p8)
    b = eye + p4 + p8 + _dot_mid(p4, p8)
    p32 = _dot_mid(p16, p16)
    ab = _dot_mid(a, b)
    c = eye + p16 + p32 + _dot_mid(p16, p32)
    return _dot_mid(ab, c)


def _l2_unit(x):
    r = lax.rsqrt(jnp.sum(x * x, axis=-1, keepdims=True) + NORM_EPS)
    return x * r, r


class _BlockMasks:
    def __init__(self):
        n = DN_BLOCK
        row = lax.broadcasted_iota(jnp.int32, (n, n), 0)
        col = lax.broadcasted_iota(jnp.int32, (n, n), 1)
        same = (row // DN_CHUNK) == (col // DN_CHUNK)
        self.lower, self.strict_lower = same & (row >= col), same & (row > col)
        self.upper, self.strict_upper = same & (row <= col), same & (row < col)
        self.eye = (row == col).astype(F32)
        self.first = lax.broadcasted_iota(jnp.int32, (n, 1), 0) < DN_CHUNK


def _dn_gates(ba, hp):
    coef = -jnp.exp(hp[0:1, :])
    pre = ba + hp[1:2, :]
    return _sigmoid(ba), coef * _softplus(pre), coef, pre


def _dn_block_gates(mk, ba, hp):
    assert DN_BLOCK == 2 * DN_CHUNK
    beta_t, graw_t, coef, pre = _dn_gates(ba, hp)
    gcum_t = _dot_hi(mk.lower.astype(F32), graw_t)
    gl_t = jnp.where(mk.first, gcum_t[DN_CHUNK - 1:DN_CHUNK, :], gcum_t[DN_BLOCK - 1:DN_BLOCK, :])
    return beta_t, gcum_t, gl_t, graw_t, coef, pre


def _dn_local(mk, qraw, kraw, vh, bc, gc, gl):
    f = {}
    f["qn"], f["rq"] = _l2_unit(qraw)
    qh = f["qn"] * (DN_DIM ** -0.5)
    kh, f["rk"] = _l2_unit(kraw)
    gr = jnp.broadcast_to(gc, (DN_BLOCK, DN_BLOCK)).T
    dec = jnp.where(mk.lower, jnp.exp(jnp.where(mk.lower, gc - gr, 0.0)), 0.0)
    kb = kh * bc
    mkk = _dot_nt(_bf(kb), _bf(kh))
    inv = _unit_lower_inverse(jnp.where(mk.strict_lower, mkk * dec, 0.0), mk.eye)
    eg = jnp.exp(gc)
    sol = _dot_mid(inv, jnp.concatenate([vh * bc, kb * eg], axis=1))
    mqk = _dot_nt(_bf(qh), _bf(kh))
    etl = jnp.exp(gl - gc)
    f.update(qh=qh, kh=kh, gr=gr, dec=dec, kb=kb, mkk=mkk, eg=eg, u=sol[:, :DN_DIM], w=sol[:, DN_DIM:], mqk=mqk,
             attn=mqk * dec, etl=etl, qd=qh * eg, kt=kh * etl)
    return f


def _dn_specs(rows, rev=None):
    at = (lambda n: n) if rev is None else rev
    hw = DN_HEADS * DN_DIM
    return dict(
        qkv=[pl.BlockSpec((rows, hw), lambda n, j=j: (at(n), j)) for j in range(3)],
        ba=pl.BlockSpec((rows, 128), lambda n: (at(n), BA_BLOCK)),
        hp=pl.BlockSpec((8, 128), lambda n: (0, 0)),
        tok=pl.BlockSpec((rows, hw), lambda n: (at(n), 0)),
        attn=pl.BlockSpec((rows, DN_HEADS * DN_CHUNK), lambda n: (at(n), 0)),
        gate=pl.BlockSpec((rows // DN_CHUNK, 8, 128), lambda n: (at(n), 0, 0)),
        state=pl.BlockSpec((rows // DN_CHUNK, DN_HEADS, DN_DIM, DN_DIM), lambda n: (at(n), 0, 0, 0)),
    )


def _dn_prep(qkv, proj, hp, name):
    T = qkv.shape[0]
    n_chunks = T // DN_CHUNK
    blocks = max(1, min(DN_PREP_CHUNKS, n_chunks) * DN_CHUNK // DN_BLOCK)
    group = blocks * DN_BLOCK // DN_CHUNK
    rows = blocks * DN_BLOCK
    hw = DN_HEADS * DN_DIM

    def body(q_ref, k_ref, v_ref, ba_ref, hp_ref, u_ref, w_ref, p_ref, qd_ref, kt_ref, gl_ref):
        mk = _BlockMasks()
        hp_v = hp_ref[...]
        for j in range(blocks):
            rs = slice(j * DN_BLOCK, (j + 1) * DN_BLOCK)
            beta_t, gcum_t, gl_t = _dn_block_gates(mk, ba_ref[rs, :], hp_v)[:3]
            for c in range(DN_BLOCK // DN_CHUNK):
                gl_ref[j * (DN_BLOCK // DN_CHUNK) + c] = jnp.broadcast_to(gl_t[c * DN_CHUNK:c * DN_CHUNK + 1, :], (8, 128))
            for h in range(DN_HEADS):
                sl = slice(h * DN_DIM, (h + 1) * DN_DIM)
                gate = slice(DN_HEADS + h, DN_HEADS + h + 1)
                f = _dn_local(mk, q_ref[rs, sl], k_ref[rs, sl], v_ref[rs, sl], beta_t[:, h:h + 1], gcum_t[:, gate], gl_t[:, gate])
                u_ref[rs, sl] = f["u"]
                w_ref[rs, sl] = _bf(f["w"])
                for c in range(DN_BLOCK // DN_CHUNK):
                    cr = slice(c * DN_CHUNK, (c + 1) * DN_CHUNK)
                    p_ref[j * DN_BLOCK + c * DN_CHUNK:j * DN_BLOCK + (c + 1) * DN_CHUNK, h * DN_CHUNK:(h + 1) * DN_CHUNK] = _bf(f["attn"][cr, cr])
                qd_ref[rs, sl] = _bf(f["qd"])
                kt_ref[rs, sl] = _bf(f["kt"])

    sp = _dn_specs(rows)
    tok16 = jax.ShapeDtypeStruct((T, hw), BF16)
    return pl.pallas_call(
        body, name=name, grid=(n_chunks // group,),
        in_specs=sp["qkv"] + [sp["ba"], sp["hp"]],
        out_specs=[sp["tok"], sp["tok"], sp["attn"], sp["tok"], sp["tok"], sp["gate"]],
        out_shape=[jax.ShapeDtypeStruct((T, hw), F32), tok16, jax.ShapeDtypeStruct((T, DN_HEADS * DN_CHUNK), BF16),
                   tok16, tok16, jax.ShapeDtypeStruct((n_chunks, 8, 128), F32)],
        compiler_params=_params(("parallel",)),
    )(qkv, qkv, qkv, proj, hp)


def _dn_scan(u, w, p, qd, kt, gl, name):
    T = u.shape[0]
    n_chunks = T // DN_CHUNK
    group = min(DN_SCAN_CHUNKS, n_chunks)
    rows = group * DN_CHUNK
    hw = DN_HEADS * DN_DIM

    def body(u_ref, w_ref, p_ref, qd_ref, kt_ref, gl_ref, o_ref, vn_ref, sall_ref, s_s):
        @pl.when(pl.program_id(0) == 0)
        def _():
            s_s[...] = jnp.zeros_like(s_s)

        state = [s_s[h] for h in range(DN_HEADS)]
        for j in range(group):
            rs = slice(j * DN_CHUNK, (j + 1) * DN_CHUNK)
            for h in range(DN_HEADS):
                sl = slice(h * DN_DIM, (h + 1) * DN_DIM)
                sall_ref[j, h] = state[h]
                sb = _bf(state[h])
                vnb = _bf(u_ref[rs, sl] - _dot(w_ref[rs, sl], sb))
                vn_ref[rs, sl] = vnb
                o_ref[rs, sl] = _dot(qd_ref[rs, sl], sb) + _dot(p_ref[rs, h * DN_CHUNK:(h + 1) * DN_CHUNK], vnb)
                egl = jnp.exp(gl_ref[j, 0:1, DN_HEADS + h:DN_HEADS + h + 1])
                state[h] = state[h] * egl + _dot_tn(kt_ref[rs, sl], vnb)
        for h in range(DN_HEADS):
            s_s[h] = state[h]

    sp = _dn_specs(rows)
    return pl.pallas_call(
        body, name=name, grid=(n_chunks // group,),
        in_specs=[sp["tok"], sp["tok"], sp["attn"], sp["tok"], sp["tok"], sp["gate"]],
        out_specs=[sp["tok"], sp["tok"], sp["state"]],
        out_shape=[jax.ShapeDtypeStruct((T, hw), F32), jax.ShapeDtypeStruct((T, hw), BF16),
                   jax.ShapeDtypeStruct((n_chunks, DN_HEADS, DN_DIM, DN_DIM), F32)],
        scratch_shapes=[pltpu.VMEM((DN_HEADS, DN_DIM, DN_DIM), F32)],
        compiler_params=_params(("arbitrary",)),
    )(u, w, p, qd, kt, gl)


def _dn_scan_bwd(w, p, qd, kt, gl, vn, sall, do, name):
    T = w.shape[0]
    n_chunks = T // DN_CHUNK
    group = min(DN_SCAN_CHUNKS, n_chunks)
    rows = group * DN_CHUNK
    hw = DN_HEADS * DN_DIM
    last = n_chunks // group - 1

    def body(w_ref, p_ref, qd_ref, kt_ref, gl_ref, vn_ref, sall_ref, do_ref, dvn_ref, dkt_ref, dgl_ref, ds_s):
        @pl.when(pl.program_id(0) == 0)
        def _():
            ds_s[...] = jnp.zeros_like(ds_s)

        lane = lax.broadcasted_iota(jnp.int32, (8, 128), 1)
        d_state = [ds_s[h] for h in range(DN_HEADS)]
        for j in reversed(range(group)):
            rs = slice(j * DN_CHUNK, (j + 1) * DN_CHUNK)
            dgl_tile = jnp.zeros((8, 128), F32)
            for h in range(DN_HEADS):
                sl = slice(h * DN_DIM, (h + 1) * DN_DIM)
                d_out = _bf(do_ref[rs, sl])
                d_new = d_state[h]
                d_newb = _bf(d_new)
                d_vn = _dot_tn(p_ref[rs, h * DN_CHUNK:(h + 1) * DN_CHUNK], d_out) + _dot(kt_ref[rs, sl], d_newb)
                dvn_ref[rs, sl] = d_vn
                dkt_ref[rs, sl] = _dot_nt(vn_ref[rs, sl], d_newb)
                egl = jnp.exp(gl_ref[j, 0:1, DN_HEADS + h:DN_HEADS + h + 1])
                prod = jnp.sum(d_new * sall_ref[j, h], axis=1, keepdims=True)
                dgl_tile = jnp.where(lane == DN_HEADS + h, jnp.sum(prod, axis=0, keepdims=True) * egl, dgl_tile)
                d_state[h] = d_new * egl + _dot_tn(qd_ref[rs, sl], d_out) - _dot_tn(w_ref[rs, sl], _bf(d_vn))
            dgl_ref[j] = dgl_tile
        for h in range(DN_HEADS):
            ds_s[h] = d_state[h]

    sp = _dn_specs(rows, rev=lambda n: last - n)
    return pl.pallas_call(
        body, name=name, grid=(n_chunks // group,),
        in_specs=[sp["tok"], sp["attn"], sp["tok"], sp["tok"], sp["gate"], sp["tok"], sp["state"], sp["tok"]],
        out_specs=[sp["tok"], sp["tok"], sp["gate"]],
        out_shape=[jax.ShapeDtypeStruct((T, hw), F32), jax.ShapeDtypeStruct((T, hw), F32),
                   jax.ShapeDtypeStruct((n_chunks, 8, 128), F32)],
        scratch_shapes=[pltpu.VMEM((DN_HEADS, DN_DIM, DN_DIM), F32)],
        compiler_params=_params(("arbitrary",)),
    )(w, p, qd, kt, gl, vn, sall, do)


def _dn_prep_bwd(qkv, proj, hp, sall, vn, do, dvn, dkt, dgl, name):
    T = qkv.shape[0]
    n_chunks = T // DN_CHUNK
    blocks = max(1, min(DN_PREP_CHUNKS, n_chunks) * DN_CHUNK // DN_BLOCK)
    per_block = DN_BLOCK // DN_CHUNK
    group = blocks * per_block
    rows = blocks * DN_BLOCK
    hw = DN_HEADS * DN_DIM
    first_rows, second_rows = slice(0, DN_CHUNK), slice(DN_CHUNK, DN_BLOCK)

    def rowsum(x):
        return jnp.sum(x, axis=1, keepdims=True)

    def by_chunk(x, s0, s1, fn):
        return jnp.concatenate([fn(x[first_rows], s0), fn(x[second_rows], s1)], axis=0)

    def body(q_ref, k_ref, v_ref, ba_ref, hp_ref, sall_ref, vn_ref, do_ref, dvn_ref, dkt_ref, dgl_ref,
             dqkv_ref, dba_ref, dhp_ref):
        @pl.when(pl.program_id(0) == 0)
        def _():
            dhp_ref[...] = jnp.zeros_like(dhp_ref)

        mk = _BlockMasks()
        hp_v = hp_ref[...]
        total = jnp.zeros((8, 128), F32)
        for j in range(blocks):
            rs = slice(j * DN_BLOCK, (j + 1) * DN_BLOCK)
            total = total + one_block(mk, hp_v, *(r.at[rs, :] for r in (q_ref, k_ref, v_ref, ba_ref)),
                                      sall_ref.at[pl.ds(j * per_block, per_block)],
                                      *(r.at[rs, :] for r in (vn_ref, do_ref, dvn_ref, dkt_ref)),
                                      dgl_ref.at[pl.ds(j * per_block, per_block)],
                                      *(dqkv_ref.at[rs, pl.ds(i * hw, hw)] for i in range(3)), dba_ref.at[rs, :])
        dhp_ref[...] += total

    def one_block(mk, hp_v, q_ref, k_ref, v_ref, ba_ref, state_ref, vn_ref, do_ref, dvn_ref, dkt_ref, dgl_ref,
                  dq_ref, dk_ref, dv_ref, dba_ref):
        ba = ba_ref[...]
        beta_t, gcum_t, gl_t, graw_t, coef, pre = _dn_block_gates(mk, ba, hp_v)
        lane = lax.broadcasted_iota(jnp.int32, (DN_BLOCK, 128), 1)
        rowi = lax.broadcasted_iota(jnp.int32, (DN_BLOCK, 1), 0)
        dgcum_t = jnp.zeros((DN_BLOCK, 128), F32)
        dbeta_t = jnp.zeros((DN_BLOCK, 128), F32)
        for h in range(DN_HEADS):
            sl = slice(h * DN_DIM, (h + 1) * DN_DIM)
            gate = slice(DN_HEADS + h, DN_HEADS + h + 1)
            gc = gcum_t[:, gate]
            bc = beta_t[:, h:h + 1]
            sb0, sb1 = _bf(state_ref[0, h]), _bf(state_ref[1, h])
            vh = v_ref[:, sl]
            f = _dn_local(mk, q_ref[:, sl], k_ref[:, sl], vh, bc, gc, gl_t[:, gate])
            qh, kh, kb, dec, eg, etl = f["qh"], f["kh"], f["kb"], f["dec"], f["eg"], f["etl"]
            u, w, qd, kt = f["u"], f["w"], f["qd"], f["kt"]
            qb, kbf, kbb = _bf(qh), _bf(kh), _bf(kb)
            dec_t = jnp.where(mk.upper, jnp.exp(jnp.where(mk.upper, f["gr"] - gc, 0.0)), 0.0)
            mkk_t = _dot_nt(kbf, kbb)
            inv_t = _unit_lower_inverse(jnp.where(mk.strict_upper, mkk_t * dec_t, 0.0), mk.eye)
            mqk_t = _dot_nt(kbf, qb)

            d_out = _bf(do_ref[:, sl])
            vnb = vn_ref[:, sl]
            d_qd = by_chunk(d_out, sb0, sb1, _dot_nt)
            d_attn = _dot_nt(d_out, vnb)
            d_attn_t = _dot_nt(vnb, d_out)
            d_vn = dvn_ref[:, sl]
            d_kt = dkt_ref[:, sl]
            d_w = -by_chunk(_bf(d_vn), sb0, sb1, _dot_nt)
            d_rhs = _dot_mid(inv_t, jnp.concatenate([d_vn, d_w], axis=1))
            d_bu, d_bw = d_rhs[:, :DN_DIM], d_rhs[:, DN_DIM:]
            ub, wb, d_bub, d_bwb = _bf(u), _bf(w), _bf(d_bu), _bf(d_bw)
            d_low = -(_dot_nt(d_bub, ub) + _dot_nt(d_bwb, wb))
            d_low_t = -(_dot_nt(ub, d_bub) + _dot_nt(wb, d_bwb))
            d_mkk = jnp.where(mk.strict_lower, d_low * dec, 0.0)
            d_mkk_t = jnp.where(mk.strict_upper, d_low_t * dec_t, 0.0)
            d_mqk = jnp.where(mk.lower, d_attn * dec, 0.0)
            d_mqk_t = jnp.where(mk.upper, d_attn_t * dec_t, 0.0)
            bw = kb * eg
            d_kb = _dot(_bf(d_mkk), kbf) + d_bw * eg
            d_k = _dot(_bf(d_mkk_t), kbb) + _dot(_bf(d_mqk_t), qb) + d_kt * etl + d_kb * bc
            d_q = _dot(_bf(d_mqk), kbf) + d_qd * eg
            d_beta = rowsum(d_kb * kh) + rowsum(d_bu * vh)
            dv_ref[:, sl] = d_bu * bc
            e_mat = d_mkk * f["mkk"] + d_mqk * f["mqk"]
            e_mat_t = d_mkk_t * mkk_t + d_mqk_t * mqk_t
            kt_term = rowsum(d_kt * kt)
            d_g = rowsum(e_mat) - rowsum(e_mat_t) + rowsum(d_qd * qd) + rowsum(d_bw * bw) - kt_term
            for c, chunk_rows in enumerate((mk.first, ~mk.first)):
                d_glast = dgl_ref[c, 0:1, gate] + jnp.sum(jnp.where(chunk_rows, kt_term, 0.0), axis=0, keepdims=True)
                d_g = d_g + jnp.where(rowi == (c + 1) * DN_CHUNK - 1, d_glast, 0.0)
            qn = f["qn"]
            d_qs = d_q * (DN_DIM ** -0.5)
            dq_ref[:, sl] = f["rq"] * (d_qs - qn * rowsum(d_qs * qn))
            dk_ref[:, sl] = f["rk"] * (d_k - kh * rowsum(d_k * kh))
            dgcum_t = jnp.where(lane == DN_HEADS + h, d_g, dgcum_t)
            dbeta_t = jnp.where(lane == h, d_beta, dbeta_t)
        dgraw_t = _dot_hi(mk.upper.astype(F32), dgcum_t)
        sp = _sigmoid(pre)
        d_pre = dgraw_t * coef * sp
        dba_ref[...] = jnp.where(lane < DN_HEADS, dbeta_t * beta_t * (1.0 - beta_t),
                                 jnp.where(lane < 2 * DN_HEADS, d_pre, 0.0))
        in_g = (lane >= DN_HEADS) & (lane < 2 * DN_HEADS)
        d_alog = jnp.sum(jnp.where(in_g, dgraw_t * graw_t, 0.0), axis=0, keepdims=True)
        d_dtb = jnp.sum(jnp.where(in_g, d_pre, 0.0), axis=0, keepdims=True)
        return jnp.concatenate([d_alog, d_dtb, jnp.zeros((6, 128), F32)], axis=0)

    sp = _dn_specs(rows)
    return pl.pallas_call(
        body, name=name, grid=(n_chunks // group,),
        in_specs=sp["qkv"] + [sp["ba"], sp["hp"], sp["state"], sp["tok"], sp["tok"], sp["tok"], sp["tok"], sp["gate"]],
        out_specs=[pl.BlockSpec((rows, 3 * hw), lambda n: (n, 0)), pl.BlockSpec((rows, 128), lambda n: (n, 0)), sp["hp"]],
        out_shape=[jax.ShapeDtypeStruct((T, 3 * hw), F32), jax.ShapeDtypeStruct((T, 128), F32),
                   jax.ShapeDtypeStruct((8, 128), F32)],
        compiler_params=_params(("arbitrary",)),
    )(qkv, qkv, qkv, proj, hp, sall, vn, do, dvn, dkt, dgl)


def _mix_fwd(o, proj, dn_norm, sg_norm, sg_w, sg_bt, name):
    T = o.shape[0]
    tm = _tile(T, 512)
    hw = DN_HEADS * DN_DIM
    nc = tm // SG_CHUNK

    def body(o_ref, z_ref, su_ref, sv_ref, dnn_ref, sgn_ref, sgw_ref, sgb_ref, mix_ref):
        dnn = dnn_ref[...]
        for h in range(DN_HEADS):
            sl = slice(h * DN_DIM, (h + 1) * DN_DIM)
            xhat, _ = _rms_stats(o_ref[:, sl])
            z = z_ref[:, sl]
            mix_ref[:, sl] = _bf(xhat * dnn * (z * _sigmoid(z)))
        tri = lax.broadcasted_iota(jnp.int32, (SG_CHUNK, SG_CHUNK), 0) >= lax.broadcasted_iota(jnp.int32, (SG_CHUNK, SG_CHUNK), 1)
        for g in range(SG_GROUPS):
            sl = slice(g * SG_DIM, (g + 1) * SG_DIM)
            xhat, _ = _rms_stats(_gelu(sv_ref[:, sl]))
            svn = _bf(xhat * sgn_ref[g:g + 1, :])
            sua = _gelu(su_ref[:, sl])
            wt = _bf(jnp.where(tri, sgw_ref[g], 0.0))
            bias = sgb_ref[:, g:g + 1]
            for c in range(nc):
                rows = slice(c * SG_CHUNK, (c + 1) * SG_CHUNK)
                mixed = _dot(wt, svn[rows, :]) + bias
                mix_ref[rows, hw + g * SG_DIM:hw + (g + 1) * SG_DIM] = _bf(sua[rows, :] * mixed)

    full = lambda shape: pl.BlockSpec(shape, lambda i: (0,) * len(shape))
    return pl.pallas_call(
        body, name=name, grid=(T // tm,),
        in_specs=[pl.BlockSpec((tm, hw), lambda i: (i, 0)),
                  pl.BlockSpec((tm, hw), lambda i: (i, 3)),
                  pl.BlockSpec((tm, hw), lambda i: (i, 4)),
                  pl.BlockSpec((tm, hw), lambda i: (i, 5)),
                  full((1, DN_DIM)), full((SG_GROUPS, SG_DIM)), full((SG_GROUPS, SG_CHUNK, SG_CHUNK)),
                  full((SG_CHUNK, 128))],
        out_specs=pl.BlockSpec((tm, 2 * hw), lambda i: (i, 0)),
        out_shape=jax.ShapeDtypeStruct((T, 2 * hw), BF16),
        compiler_params=_params(("parallel",)),
    )(o, proj, proj, proj, dn_norm, sg_norm, sg_w, sg_bt)


def _mix_bwd(dmix, o, proj, dn_norm, sg_norm, sg_w, sg_bt, name):
    T = o.shape[0]
    tm = _tile(T, 512)
    hw = DN_HEADS * DN_DIM
    nc = tm // SG_CHUNK

    def body(dm_ref, o_ref, z_ref, su_ref, sv_ref, dnn_ref, sgn_ref, sgw_ref, sgb_ref,
             do_ref, dz_ref, ddnn_ref, dsgn_ref, dsgw_ref, dsgb_ref):
        @pl.when(pl.program_id(0) == 0)
        def _():
            ddnn_ref[...] = jnp.zeros_like(ddnn_ref)
            dsgn_ref[...] = jnp.zeros_like(dsgn_ref)
            dsgw_ref[...] = jnp.zeros_like(dsgw_ref)
            dsgb_ref[...] = jnp.zeros_like(dsgb_ref)

        dnn = dnn_ref[...]
        ddnn = jnp.zeros((1, DN_DIM), F32)
        for h in range(DN_HEADS):
            sl = slice(h * DN_DIM, (h + 1) * DN_DIM)
            xhat, r = _rms_stats(o_ref[:, sl])
            z = z_ref[:, sl]
            sz = _sigmoid(z)
            doa = dm_ref[:, sl]
            dyn = doa * (z * sz)
            dz_ref[:, sl] = _bf(doa * xhat * dnn * _silu_grad(z, sz))
            do_ref[:, sl] = _rms_bwd(dyn, xhat, r, dnn)
            ddnn = ddnn + jnp.sum(dyn * xhat, axis=0, keepdims=True)
        ddnn_ref[...] += ddnn
        tri = lax.broadcasted_iota(jnp.int32, (SG_CHUNK, SG_CHUNK), 0) >= lax.broadcasted_iota(jnp.int32, (SG_CHUNK, SG_CHUNK), 1)
        lane = lax.broadcasted_iota(jnp.int32, (SG_CHUNK, 128), 1)
        dsgb = jnp.zeros((SG_CHUNK, 128), F32)
        dsgn_rows = []
        for g in range(SG_GROUPS):
            sl = slice(g * SG_DIM, (g + 1) * SG_DIM)
            sv = sv_ref[:, sl]
            su = su_ref[:, sl]
            xhat, r = _rms_stats(_gelu(sv))
            sgn = sgn_ref[g:g + 1, :]
            svn = _bf(xhat * sgn)
            sua = _gelu(su)
            wt = _bf(jnp.where(tri, sgw_ref[g], 0.0))
            bias = sgb_ref[:, g:g + 1]
            dw = jnp.zeros((SG_CHUNK, SG_CHUNK), F32)
            db = jnp.zeros((SG_CHUNK, 1), F32)
            dsua, dsvn = [], []
            for c in range(nc):
                rows = slice(c * SG_CHUNK, (c + 1) * SG_CHUNK)
                mixed = _dot(wt, svn[rows, :]) + bias
                dob = dm_ref[rows, hw + g * SG_DIM:hw + (g + 1) * SG_DIM]
                dsua.append(dob * mixed)
                dmixed = dob * sua[rows, :]
                dmb = _bf(dmixed)
                dsvn.append(_dot_tn(wt, dmb))
                dw = dw + _dot_nt(dmb, svn[rows, :])
                db = db + jnp.sum(dmixed, axis=1, keepdims=True)
            dsua = jnp.concatenate(dsua, axis=0) if nc > 1 else dsua[0]
            dsvn = jnp.concatenate(dsvn, axis=0) if nc > 1 else dsvn[0]
            dz_ref[:, hw + g * SG_DIM:hw + (g + 1) * SG_DIM] = _bf(dsua * _gelu_grad(su))
            dz_ref[:, 2 * hw + g * SG_DIM:2 * hw + (g + 1) * SG_DIM] = _bf(_rms_bwd(dsvn, xhat, r, sgn) * _gelu_grad(sv))
            dsgn_rows.append(jnp.sum(dsvn * xhat, axis=0, keepdims=True))
            dsgw_ref[g] += jnp.where(tri, dw, 0.0)
            dsgb = jnp.where(lane == g, db, dsgb)
        dsgn_ref[...] += jnp.concatenate(dsgn_rows, axis=0)
        dsgb_ref[...] += dsgb

    full = lambda shape: pl.BlockSpec(shape, lambda i: (0,) * len(shape))
    return pl.pallas_call(
        body, name=name, grid=(T // tm,),
        in_specs=[pl.BlockSpec((tm, 2 * hw), lambda i: (i, 0)),
                  pl.BlockSpec((tm, hw), lambda i: (i, 0)),
                  pl.BlockSpec((tm, hw), lambda i: (i, 3)),
                  pl.BlockSpec((tm, hw), lambda i: (i, 4)),
                  pl.BlockSpec((tm, hw), lambda i: (i, 5)),
                  full((1, DN_DIM)), full((SG_GROUPS, SG_DIM)), full((SG_GROUPS, SG_CHUNK, SG_CHUNK)),
                  full((SG_CHUNK, 128))],
        out_specs=[pl.BlockSpec((tm, hw), lambda i: (i, 0)),
                   pl.BlockSpec((tm, 3 * hw), lambda i: (i, 0)),
                   full((1, DN_DIM)), full((SG_GROUPS, SG_DIM)), full((SG_GROUPS, SG_CHUNK, SG_CHUNK)),
                   full((SG_CHUNK, 128))],
        out_shape=[jax.ShapeDtypeStruct((T, hw), F32), jax.ShapeDtypeStruct((T, 3 * hw), BF16),
                   jax.ShapeDtypeStruct((1, DN_DIM), F32), jax.ShapeDtypeStruct((SG_GROUPS, SG_DIM), F32),
                   jax.ShapeDtypeStruct((SG_GROUPS, SG_CHUNK, SG_CHUNK), F32),
                   jax.ShapeDtypeStruct((SG_CHUNK, 128), F32)],
        compiler_params=_params(("arbitrary",)),
    )(dmix, o, proj, proj, proj, dn_norm, sg_norm, sg_w, sg_bt)


def _window_sums(h, sign):
    sums, s, w = {}, h, 1
    while w < POOL_WINDOWS[-1]:
        s = s + _shift_rows(s, sign * w)
        w *= 2
        sums[w] = s
    return sums


def _pool_counts(t_global):
    return [jnp.minimum(t_global + 1, win).astype(F32) for win in POOL_WINDOWS]


def _pooled_groups(ext_h, row0, tm):
    sums = _window_sums(ext_h, 1)
    t_global = row0 + lax.broadcasted_iota(jnp.int32, (tm, 1), 0)
    counts = _pool_counts(t_global)
    out = []
    for gi, win in enumerate(POOL_WINDOWS):
        cols = slice(gi * POOL_DIM, (gi + 1) * POOL_DIM)
        out.append(sums[win][POOL_HALO:, cols] / counts[gi] - ext_h[POOL_HALO:, cols])
    return out


def _pool_fwd(x, nw, pool_w, pool_scale, layer, name):
    T, D = x.shape
    tm = _tile(T, 256)
    hb = tm // POOL_HALO

    def body(x_ref, xp_ref, n_ref, w_ref, s_ref, xo_ref):
        i = pl.program_id(0)
        prev = jnp.where(i == 0, 0.0, xp_ref[...])
        ext = jnp.concatenate([prev, x_ref[...]], axis=0)
        xhat, _ = _rms_stats(ext)
        pooled = _pooled_groups(xhat * n_ref[...], i * tm, tm)
        for gi in range(len(POOL_WINDOWS)):
            cols = slice(gi * POOL_DIM, (gi + 1) * POOL_DIM)
            xo_ref[:, cols] = x_ref[:, cols] + _dot(_bf(pooled[gi]), w_ref[gi]) * s_ref[:, cols]

    return pl.pallas_call(
        body, name=name, grid=(T // tm,),
        in_specs=[pl.BlockSpec((tm, D), lambda i: (i, 0)),
                  pl.BlockSpec((POOL_HALO, D), lambda i: (jnp.maximum(i * hb - 1, 0), 0)),
                  pl.BlockSpec((None, 1, D), lambda i: (layer, 0, 0)),
                  pl.BlockSpec(pool_w.shape, lambda i: (0, 0, 0)),
                  pl.BlockSpec((1, D), lambda i: (0, 0))],
        out_specs=pl.BlockSpec((tm, D), lambda i: (i, 0)),
        out_shape=jax.ShapeDtypeStruct((T, D), F32),
        compiler_params=_params(("parallel",)),
    )(x, x, nw, pool_w, pool_scale)


def _pool_bwd(dxo, x, nw, pool_w, pool_scale, layer, name):
    T, D = x.shape
    tm = _tile(T, 256)
    hb = tm // POOL_HALO
    nt = T // tm
    ng = len(POOL_WINDOWS)

    def body(dxo_ref, dxn_ref, x_ref, xp_ref, n_ref, w_ref, s_ref, dx_ref, dw_ref, ds_ref, dn_ref):
        i = pl.program_id(0)

        @pl.when(i == 0)
        def _():
            dw_ref[...] = jnp.zeros_like(dw_ref)
            ds_ref[...] = jnp.zeros_like(ds_ref)
            dn_ref[...] = jnp.zeros_like(dn_ref)

        prev = jnp.where(i == 0, 0.0, xp_ref[...])
        ext = jnp.concatenate([prev, x_ref[...]], axis=0)
        xhat_ext, r_ext = _rms_stats(ext)
        nv = n_ref[...]
        pooled = _pooled_groups(xhat_ext * nv, i * tm, tm)
        dxo = dxo_ref[...]
        scale = s_ref[...]
        dout_ext = jnp.concatenate([dxo, jnp.where(i == nt - 1, 0.0, dxn_ref[...])], axis=0) * scale
        t_ext = i * tm + lax.broadcasted_iota(jnp.int32, (tm + POOL_HALO, 1), 0)
        counts = _pool_counts(t_ext)
        dh_cols, ds_cols = [], []
        for gi, win in enumerate(POOL_WINDOWS):
            cols = slice(gi * POOL_DIM, (gi + 1) * POOL_DIM)
            wg = w_ref[gi]
            pb = _bf(pooled[gi])
            doutb = _bf(dout_ext[:, cols])
            dpooled = _dot_nt(doutb, wg)
            ahead = _window_sums(dpooled / counts[gi], -1)[win]
            dh_cols.append(ahead[:tm, :] - dpooled[:tm, :])
            dw_ref[gi] += _dot_tn(pb, doutb[:tm, :])
            ds_cols.append(jnp.sum(dxo[:, cols] * _dot(pb, wg), axis=0, keepdims=True))
        dh = jnp.concatenate(dh_cols, axis=1)
        xhat, r = xhat_ext[POOL_HALO:, :], r_ext[POOL_HALO:, :]
        dx_ref[...] = dxo + _rms_bwd(dh, xhat, r, nv)
        dn_ref[...] += jnp.sum(dh * xhat, axis=0, keepdims=True)
        ds_ref[...] += jnp.concatenate(ds_cols, axis=1)

    last_halo = T // POOL_HALO - 1
    return pl.pallas_call(
        body, name=name, grid=(nt,),
        in_specs=[pl.BlockSpec((tm, D), lambda i: (i, 0)),
                  pl.BlockSpec((POOL_HALO, D), lambda i: (jnp.minimum((i + 1) * hb, last_halo), 0)),
                  pl.BlockSpec((tm, D), lambda i: (i, 0)),
                  pl.BlockSpec((POOL_HALO, D), lambda i: (jnp.maximum(i * hb - 1, 0), 0)),
                  pl.BlockSpec((None, 1, D), lambda i: (layer, 0, 0)),
                  pl.BlockSpec(pool_w.shape, lambda i: (0, 0, 0)),
                  pl.BlockSpec((1, D), lambda i: (0, 0))],
        out_specs=[pl.BlockSpec((tm, D), lambda i: (i, 0)),
                   pl.BlockSpec((ng, POOL_DIM, POOL_DIM), lambda i: (0, 0, 0)),
                   pl.BlockSpec((1, D), lambda i: (0, 0)),
                   pl.BlockSpec((1, D), lambda i: (0, 0))],
        out_shape=[jax.ShapeDtypeStruct((T, D), F32), jax.ShapeDtypeStruct((ng, POOL_DIM, POOL_DIM), F32),
                   jax.ShapeDtypeStruct((1, D), F32), jax.ShapeDtypeStruct((1, D), F32)],
        compiler_params=_params(("arbitrary",)),
    )(dxo, dxo, x, x, nw, pool_w, pool_scale)


def _loss_head(x, target, fn, name):
    T, D = x.shape
    tm = _tile(T, 512)

    def body(x_ref, t_ref, n_ref, loss_ref, dx_ref, dn_ref):
        @pl.when(pl.program_id(0) == 0)
        def _():
            loss_ref[...] = jnp.zeros_like(loss_ref)
            dn_ref[...] = jnp.zeros_like(dn_ref)

        xhat, r = _rms_stats(x_ref[...])
        nv = n_ref[...]
        err = xhat * nv - t_ref[...]
        part = jnp.sum(jnp.sum(err * err, axis=1, keepdims=True), axis=0, keepdims=True)
        loss_ref[...] += 0.5 * part / D
        dy = err / D
        dx_ref[...] = _rms_bwd(dy, xhat, r, nv)
        dn_ref[...] += jnp.sum(dy * xhat, axis=0, keepdims=True)

    row = pl.BlockSpec((tm, D), lambda i: (i, 0))
    return pl.pallas_call(
        body, name=name, grid=(T // tm,),
        in_specs=[row, row, pl.BlockSpec((1, D), lambda i: (0, 0))],
        out_specs=[pl.BlockSpec((1, 1), lambda i: (0, 0)), row, pl.BlockSpec((1, D), lambda i: (0, 0))],
        out_shape=[jax.ShapeDtypeStruct((1, 1), F32), jax.ShapeDtypeStruct((T, D), F32),
                   jax.ShapeDtypeStruct((1, D), F32)],
        compiler_params=_params(("arbitrary",)),
    )(x, target, fn)


def _adamw(w, g, m, v, name):
    R, C = w.shape
    br = R
    for cand in (512, 256, 128, 64, 32, 16, 8):
        if R % cand == 0 and cand * C * 4 <= 2 * 1024 * 1024:
            br = cand
            break

    def body(w_ref, g_ref, m_ref, v_ref, d_ref, mo_ref, vo_ref):
        gv = g_ref[...]
        m_new = ADAM_B1 * m_ref[...] + (1.0 - ADAM_B1) * gv
        v_new = ADAM_B2 * v_ref[...] + (1.0 - ADAM_B2) * (gv * gv)
        m_hat = m_new / (1.0 - ADAM_B1 ** ADAM_STEP)
        v_hat = v_new / (1.0 - ADAM_B2 ** ADAM_STEP)
        d_ref[...] = -ADAM_LR * (m_hat / (jnp.sqrt(v_hat) + ADAM_EPS) + ADAM_WD * w_ref[...])
        mo_ref[...] = m_new
        vo_ref[...] = v_new

    blk = pl.BlockSpec((br, C), lambda i: (i, 0))
    return pl.pallas_call(
        body, name=name, grid=(R // br,), in_specs=[blk] * 4, out_specs=[blk] * 3,
        out_shape=[jax.ShapeDtypeStruct((R, C), F32)] * 3,
        compiler_params=_params(("parallel",)),
    )(w, g, m, v)


def _mesh_pos():
    return lax.axis_index("x"), lax.axis_index("y"), lax.axis_index("c")


def _other_chips(x, y):
    return [(1 - x, y), (x, 1 - y), (1 - x, 1 - y)]


def _half_of(ref, shape, h):
    if shape[0] == 2:
        return ref.at[h]
    size = shape[1] // 2
    return ref.at[:, pl.ds(h * size, size)]


def _all_gather_chips(shards, split, name):
    n = len(shards)

    def body(*refs):
        ins, outs = refs[:n], refs[n:2 * n]
        send_sems, recv_sems, fwd_send_sems, fwd_recv_sems = refs[2 * n:]
        x, y, c = _mesh_pos()
        me = 2 * x + y
        sibling = (x, y, 1 - c)
        chips = _other_chips(x, y)

        def piece(a, ref, h):
            return _half_of(ref, shards[a].shape, h) if split[a] else ref

        sent, forwards = [], []
        for a in range(n):
            for k, (px, py) in enumerate(chips):
                cp = pltpu.make_async_remote_copy(piece(a, ins[a], c), piece(a, outs[a].at[me], c),
                                                  send_sems.at[a, k], recv_sems.at[a, k],
                                                  device_id=(px, py, c), device_id_type=MESH)
                cp.start()
                sent.append(cp)
        for a in range(n):
            for k, (px, py) in enumerate(chips):
                landed = piece(a, outs[a].at[2 * px + py], c)
                pltpu.make_async_remote_copy(landed, landed, send_sems.at[a, k], recv_sems.at[a, k],
                                             device_id=(px, py, c), device_id_type=MESH).wait_recv()
                if split[a]:
                    fwd = pltpu.make_async_remote_copy(landed, landed, fwd_send_sems.at[a, k], fwd_recv_sems.at[a, k],
                                                       device_id=sibling, device_id_type=MESH)
                    fwd.start()
                    forwards.append(fwd)
        for a in range(n):
            if split[a]:
                for k, (px, py) in enumerate(chips):
                    other = piece(a, outs[a].at[2 * px + py], 1 - c)
                    pltpu.make_async_remote_copy(other, other, fwd_send_sems.at[a, k], fwd_recv_sems.at[a, k],
                                                 device_id=sibling, device_id_type=MESH).wait_recv()
        for cp in sent + forwards:
            cp.wait_send()

    sems = pltpu.SemaphoreType.DMA((n, 3))
    gathered = pl.pallas_call(
        body, name=name, in_specs=[ANY] * n, out_specs=[ANY] * n,
        out_shape=[jax.ShapeDtypeStruct((N_CHIPS,) + s.shape, s.dtype) for s in shards],
        scratch_shapes=[sems, sems, sems, sems],
        compiler_params=pltpu.CompilerParams(has_side_effects=True),
    )(*shards)
    x, y, _ = _mesh_pos()
    return [lax.dynamic_update_index_in_dim(g, s, 2 * x + y, 0) for g, s in zip(gathered, shards)]


def _ffn_weight_grads(hb, dg, du, a, dyb, tag):
    dwg = _matmul_tn(hb, dg, D_MODEL, FF_CHUNK, f"{tag}_dw_gate", stack_n=True)
    dwu = _matmul_tn(hb, du, D_MODEL, FF_CHUNK, f"{tag}_dw_up", stack_n=True)
    dwo = _matmul_tn(a, dyb, FF_CHUNK, D_MODEL, f"{tag}_dw_out")
    return jnp.concatenate([dwg, dwu], axis=0), dwo.reshape(N_CHIPS, D_FF // N_CHIPS, D_MODEL)


def _local_step(x, target, w):
    g = {}
    acts = []
    ffn_w = {1: (w["n1"], w["win1"], w["wout1"]), 2: (w["n2"], w["win2"], w["wout2"])}

    def ffn(xin, which, layer):
        nw, win, wout = ffn_w[which]
        xo, gv, uv, hb = _ffn_fwd(xin, nw, win, wout, layer, f"ffn{which}_l{layer}_fwd")
        acts.append((xin, gv, uv, hb))
        return xo

    x1 = ffn(x, 1, 0)
    hb_mix = _rms_fwd_call(x1, w["nmix"], 0, "ab_norm_fwd")
    proj = _matmul(hb_mix, w["wp"], "ab_in_proj")
    qkv = _conv_fwd(proj, w["conv_w"], "dn_conv_fwd")
    dn_u, dn_w, dn_p, dn_qd, dn_kt, dn_gl = _dn_prep(qkv, proj, w["hp"], "dn_prep")
    o, dn_vn, sall = _dn_scan(dn_u, dn_w, dn_p, dn_qd, dn_kt, dn_gl, "dn_scan")
    mix = _mix_fwd(o, proj, w["dn_norm"], w["sg_norm"], w["sg_w"], w["sg_bt"], "ab_gate_fwd")
    x2 = _matmul(mix, w["wo"], "ab_out_proj", res=x1)
    x3 = ffn(x2, 2, 0)
    x4 = ffn(x3, 1, 1)
    x5 = _pool_fwd(x4, w["nmix"], w["pool_w"], w["pool_scale"], 1, "pool_fwd")
    x6 = ffn(x5, 2, 1)
    loss, dx, g["fn"] = _loss_head(x6, target, w["fn"], "loss_head")

    dn = {1: [None, None], 2: [None, None]}
    dwin = {1: [None, None], 2: [None, None]}
    dwout = {1: [None, None], 2: [None, None]}

    def ffn_back(dxo, which, layer, saved):
        nw, win, wout = ffn_w[which]
        xin, gv, uv, hb = saved
        tag = f"ffn{which}_l{layer}"
        dxi, dg, du, a, dyb, dnw = _ffn_bwd(dxo, xin, nw, gv, uv, win, wout, layer, f"{tag}_bwd")
        dn[which][layer] = dnw
        dwin[which][layer], dwout[which][layer] = _ffn_weight_grads(hb, dg, du, a, dyb, tag)
        return dxi

    dx = ffn_back(dx, 2, 1, acts[3])
    dx, g["pool_w"], g["pool_scale"], dnmix1 = _pool_bwd(dx, x4, w["nmix"], w["pool_w"], w["pool_scale"], 1, "pool_bwd")
    dx = ffn_back(dx, 1, 1, acts[2])
    dx2 = ffn_back(dx, 2, 0, acts[1])
    dmix = _matmul(dx2, w["wo"], "ab_out_proj_bwd", trans_b=True)
    g["wo"] = _matmul_tn(mix, dx2, D_MODEL, D_MODEL, "ab_out_proj_dw")
    do, dzuv, g["dn_norm"], g["sg_norm"], g["sg_w"], g["sg_bt"] = _mix_bwd(
        dmix, o, proj, w["dn_norm"], w["sg_norm"], w["sg_w"], w["sg_bt"], "ab_gate_bwd")
    dvn, dkt, dgl = _dn_scan_bwd(dn_w, dn_p, dn_qd, dn_kt, dn_gl, dn_vn, sall, do, "dn_scan_bwd")
    dqkv_act, dba, g["hp"] = _dn_prep_bwd(qkv, proj, w["hp"], sall, dn_vn, do, dvn, dkt, dgl, "dn_prep_bwd")
    dqkv, g["conv_w"] = _conv_bwd(dqkv_act, proj, w["conv_w"], "dn_conv_bwd")
    dproj = jnp.concatenate([dqkv, dzuv, dba.astype(BF16)], axis=1)
    dh = _matmul(dproj, w["wp"], "ab_in_proj_bwd", trans_b=True)
    g["wp"] = _matmul_tn(hb_mix, dproj, D_MODEL, 640, "ab_in_proj_dw")
    dx1, dnmix0 = _rms_bwd_call(dh, x1, w["nmix"], dx2, 0, "ab_norm_bwd")
    dx0 = ffn_back(dx1, 1, 0, acts[0])

    g["n1"] = jnp.concatenate(dn[1], axis=0)
    g["n2"] = jnp.concatenate(dn[2], axis=0)
    g["nmix"] = jnp.concatenate([dnmix0, dnmix1], axis=0)
    for which in (1, 2):
        g[f"win{which}"] = dwin[which]
        g[f"wout{which}"] = dwout[which]
    return loss, dx0, g


SHARDED = ("ffn1_w_in", "ffn1_w_out", "ffn2_w_in", "ffn2_w_out", "ab_w_in", "ab_w_out", "pool_w", "dn_conv_w", "pool_scale")
REPLICATED = ("ffn_norm1", "mix_norm", "ffn_norm2", "dn_a_log", "dn_dt_bias", "dn_out_norm", "sg_norm", "sg_w", "sg_b", "final_norm")
QKVZ = 4 * DN_HEADS * DN_DIM
N_GATES = 2 * DN_HEADS
IN_PROJ = QKVZ + N_GATES + 2 * SG_GROUPS * SG_DIM


def _kernel_layouts(gathered, rep):
    per_layer = lambda a: a.reshape(a.shape[0], 1, D_MODEL)
    w = {"n1": per_layer(rep["ffn_norm1"]), "nmix": per_layer(rep["mix_norm"]), "n2": per_layer(rep["ffn_norm2"]),
         "win1": gathered["ffn1_w_in"], "wout1": gathered["ffn1_w_out"],
         "win2": gathered["ffn2_w_in"], "wout2": gathered["ffn2_w_out"]}
    ab_in = jnp.transpose(gathered["ab_w_in"][:, 0], (1, 0, 2)).reshape(D_MODEL, IN_PROJ)
    w["wp"] = jnp.concatenate([ab_in[:, :QKVZ], ab_in[:, QKVZ + N_GATES:], ab_in[:, QKVZ:QKVZ + N_GATES],
                               jnp.zeros((D_MODEL, PROJ_W - IN_PROJ), ab_in.dtype)], axis=1)
    w["conv_w"] = jnp.transpose(gathered["dn_conv_w"][:, 0], (1, 0, 2)).reshape(DN_CONV, 3 * DN_HEADS * DN_DIM)
    hp = jnp.zeros((8, 128), F32)
    hp = hp.at[0, DN_HEADS:N_GATES].set(rep["dn_a_log"][0]).at[1, DN_HEADS:N_GATES].set(rep["dn_dt_bias"][0])
    w["hp"] = hp
    w["dn_norm"] = rep["dn_out_norm"]
    w["sg_norm"] = rep["sg_norm"][0]
    w["sg_w"] = rep["sg_w"][0]
    w["sg_bt"] = jnp.zeros((SG_CHUNK, 128), F32).at[:, :SG_GROUPS].set(rep["sg_b"][0].T)
    w["wo"] = gathered["ab_w_out"][:, 0].reshape(D_MODEL, D_MODEL)
    w["pool_w"] = jnp.transpose(gathered["pool_w"][:, 0], (1, 0, 2, 3)).reshape(len(POOL_WINDOWS), POOL_DIM, POOL_DIM)
    w["pool_scale"] = gathered["pool_scale"].reshape(1, D_MODEL)
    w["fn"] = rep["final_norm"].reshape(1, D_MODEL)
    return w


def _grads_by_chip(g):
    wp = g["wp"]
    ab_in = jnp.concatenate([wp[:, :QKVZ], wp[:, IN_PROJ - N_GATES:IN_PROJ], wp[:, QKVZ:IN_PROJ - N_GATES]], axis=1)
    nw = len(POOL_WINDOWS)
    one_layer = {
        "ab_w_in": jnp.transpose(ab_in.reshape(D_MODEL, N_CHIPS, IN_PROJ // N_CHIPS), (1, 0, 2)),
        "ab_w_out": g["wo"].reshape(N_CHIPS, D_MODEL // N_CHIPS, D_MODEL),
        "pool_w": jnp.transpose(g["pool_w"].reshape(nw, N_CHIPS, POOL_DIM // N_CHIPS, POOL_DIM), (1, 0, 2, 3)),
        "dn_conv_w": jnp.transpose(g["conv_w"].reshape(DN_CONV, N_CHIPS, -1), (1, 0, 2)),
        "pool_scale": g["pool_scale"].reshape(N_CHIPS, D_MODEL // N_CHIPS),
    }
    sharded = {}
    for n, per_layer in (("ffn1_w_in", g["win1"]), ("ffn1_w_out", g["wout1"]), ("ffn2_w_in", g["win2"]), ("ffn2_w_out", g["wout2"])):
        sharded[n] = [a.reshape(N_CHIPS, -1) for a in per_layer]
    for n, a in one_layer.items():
        halves = a.reshape(N_CHIPS, 2, -1)
        sharded[n] = [halves[:, 0], halves[:, 1]]
    rep = {
        "ffn_norm1": g["n1"], "mix_norm": g["nmix"], "ffn_norm2": g["n2"],
        "dn_a_log": g["hp"][0:1, DN_HEADS:N_GATES], "dn_dt_bias": g["hp"][1:2, DN_HEADS:N_GATES],
        "dn_out_norm": g["dn_norm"], "sg_norm": g["sg_norm"][None], "sg_w": g["sg_w"][None],
        "sg_b": g["sg_bt"][:, :SG_GROUPS].T[None], "final_norm": g["fn"].reshape(D_MODEL),
    }
    return sharded, rep


def _piece_rows(n_elems):
    rows = -(-n_elems // PACK_LANES)
    return -(-rows // 8) * 8


def _half_sizes(shapes):
    sizes = []
    for n in SHARDED:
        size = 1
        for d in shapes[n]:
            size *= d
        sizes.append(size // 2)
    return sizes


def _pack_rows(shapes):
    rows = sum(_piece_rows(s) for s in _half_sizes(shapes))
    return -(-rows // PACK_ROW_BLOCK) * PACK_ROW_BLOCK


def _pack_by_half(sharded, shapes):
    rows = _pack_rows(shapes)
    pieces = []
    for half in range(2):
        used = 0
        for n in SHARDED:
            flat = sharded[n][half]
            pr = _piece_rows(flat.shape[1])
            flat = jnp.pad(flat, ((0, 0), (0, pr * PACK_LANES - flat.shape[1])))
            pieces.append(flat.reshape(N_CHIPS, pr, PACK_LANES))
            used += pr
        pieces.append(jnp.zeros((N_CHIPS, rows - used, PACK_LANES), F32))
    return jnp.concatenate(pieces, axis=1).reshape(N_CHIPS, 2, rows, PACK_LANES)


def _unpack_halves(mine, other, core, shapes):
    out, off = {}, 0
    for n, half in zip(SHARDED, _half_sizes(shapes)):
        pr = _piece_rows(half)
        a, b = mine[off:off + pr], other[off:off + pr]
        both = jnp.stack([jnp.where(core == 0, a, b), jnp.where(core == 0, b, a)])
        out[n] = both.reshape(2, -1)[:, :half].reshape(shapes[n])
        off += pr
    return out


def _swap_with_sibling(pack, name):
    nchip, _, rows, lanes = pack.shape

    def body(pack_ref, recv_ref, send_sem, recv_sem):
        x, y, c = _mesh_pos()
        cp = pltpu.make_async_remote_copy(pack_ref.at[:, 1 - c], recv_ref, send_sem, recv_sem,
                                          device_id=(x, y, 1 - c), device_id_type=MESH)
        cp.start()
        cp.wait()

    return pl.pallas_call(
        body, name=name, in_specs=[ANY], out_specs=ANY,
        out_shape=jax.ShapeDtypeStruct((nchip, rows, lanes), pack.dtype),
        scratch_shapes=[pltpu.SemaphoreType.DMA, pltpu.SemaphoreType.DMA],
        compiler_params=pltpu.CompilerParams(has_side_effects=True),
    )(pack)


def _add_pair(pack, recv, core, name):
    nchip, _, rows, lanes = pack.shape

    def body(c_ref, a_ref, b_ref, o32_ref, o16_ref):
        s = a_ref[...] + b_ref[...]
        o32_ref[...] = s
        o16_ref[...] = _bf(s)

    blk = pl.BlockSpec((None, PACK_ROW_BLOCK, lanes), lambda p, i, c: (p, i, 0))
    return pl.pallas_call(
        body, name=name,
        grid_spec=pltpu.PrefetchScalarGridSpec(
            num_scalar_prefetch=1, grid=(nchip, rows // PACK_ROW_BLOCK),
            in_specs=[pl.BlockSpec((None, None, PACK_ROW_BLOCK, lanes), lambda p, i, c: (p, c[0], i, 0)), blk],
            out_specs=[blk, blk]),
        out_shape=[jax.ShapeDtypeStruct((nchip, rows, lanes), F32), jax.ShapeDtypeStruct((nchip, rows, lanes), BF16)],
        compiler_params=_params(("parallel", "parallel")),
    )(core, pack, recv)


def _scatter_to_chips(part16, name):
    nchip, rows, lanes = part16.shape

    def body(src_ref, recv_ref, send_sems, recv_sems):
        x, y, c = _mesh_pos()
        copies = []
        for k, (px, py) in enumerate(_other_chips(x, y)):
            cp = pltpu.make_async_remote_copy(src_ref.at[2 * px + py], recv_ref.at[k], send_sems.at[k], recv_sems.at[k],
                                              device_id=(px, py, c), device_id_type=MESH)
            cp.start()
            copies.append(cp)
        for cp in copies:
            cp.wait()

    return pl.pallas_call(
        body, name=name, in_specs=[ANY], out_specs=ANY,
        out_shape=jax.ShapeDtypeStruct((nchip - 1, rows, lanes), part16.dtype),
        scratch_shapes=[pltpu.SemaphoreType.DMA((nchip - 1,)), pltpu.SemaphoreType.DMA((nchip - 1,))],
        compiler_params=pltpu.CompilerParams(has_side_effects=True),
    )(part16)


def _sum_chips(part32, recv16, chip, name):
    nchip, rows, lanes = part32.shape

    def body(p_ref, own_ref, r_ref, o_ref):
        s = own_ref[...]
        for k in range(nchip - 1):
            s = s + r_ref[k].astype(F32)
        o_ref[...] = s

    return pl.pallas_call(
        body, name=name,
        grid_spec=pltpu.PrefetchScalarGridSpec(
            num_scalar_prefetch=1, grid=(rows // PACK_ROW_BLOCK,),
            in_specs=[pl.BlockSpec((None, PACK_ROW_BLOCK, lanes), lambda i, p: (p[0], i, 0)),
                      pl.BlockSpec((nchip - 1, PACK_ROW_BLOCK, lanes), lambda i, p: (0, i, 0))],
            out_specs=pl.BlockSpec((PACK_ROW_BLOCK, lanes), lambda i, p: (i, 0))),
        out_shape=jax.ShapeDtypeStruct((rows, lanes), F32),
        compiler_params=_params(("parallel",)),
    )(chip, part32, recv16)


def _share_with_sibling(half, name):
    rows, lanes = half.shape

    def body(h_ref, other_ref, send_sem, recv_sem):
        x, y, c = _mesh_pos()
        cp = pltpu.make_async_remote_copy(h_ref, other_ref, send_sem, recv_sem,
                                          device_id=(x, y, 1 - c), device_id_type=MESH)
        cp.start()
        cp.wait()

    return pl.pallas_call(
        body, name=name, in_specs=[ANY], out_specs=ANY,
        out_shape=jax.ShapeDtypeStruct((rows, lanes), half.dtype),
        scratch_shapes=[pltpu.SemaphoreType.DMA, pltpu.SemaphoreType.DMA],
        compiler_params=pltpu.CompilerParams(has_side_effects=True),
    )(half)


def _reduce_sharded(sharded, shapes):
    x, y, c = _mesh_pos()
    core = jnp.reshape(c, (1,)).astype(jnp.int32)
    chip = jnp.reshape(2 * x + y, (1,)).astype(jnp.int32)
    pack = _pack_by_half(sharded, shapes)
    recv = _swap_with_sibling(pack, "grad_pair_swap")
    part32, part16 = _add_pair(pack, recv, core, "grad_pair_add")
    recv16 = _scatter_to_chips(part16, "grad_chip_scatter")
    half = _sum_chips(part32, recv16, chip, "grad_chip_sum")
    other = _share_with_sibling(half, "grad_pair_share")
    return _unpack_halves(half, other, c, shapes)


def _pack_small(vals):
    parts = []
    for n in REPLICATED:
        flat = vals[n].reshape(-1)
        rows = -(-flat.shape[0] // 128)
        rows = -(-rows // 8) * 8
        parts.append(jnp.pad(flat, (0, rows * 128 - flat.shape[0])).reshape(rows, 128))
    return jnp.concatenate(parts, axis=0)


def _unpack_small(pack, like):
    out, off = {}, 0
    for n in REPLICATED:
        size = like[n].size
        rows = -(-size // 128)
        rows = -(-rows // 8) * 8
        out[n] = pack[off:off + rows].reshape(-1)[:size].reshape(like[n].shape)
        off += rows
    return out


def _all_to_all_small(pack, name):
    rows, lanes = pack.shape
    flips = [(dx, dy, dc) for dx in (0, 1) for dy in (0, 1) for dc in (0, 1)][1:]

    def body(src_ref, out_ref, send_sems, recv_sems, local_sem):
        x, y, c = _mesh_pos()
        me = 4 * x + 2 * y + c
        loc = pltpu.make_async_copy(src_ref, out_ref.at[me], local_sem)
        loc.start()
        copies = []
        for k, (dx, dy, dc) in enumerate(flips):
            peer = (x ^ dx, y ^ dy, c ^ dc)
            cp = pltpu.make_async_remote_copy(src_ref, out_ref.at[me], send_sems.at[k], recv_sems.at[k],
                                              device_id=peer, device_id_type=MESH)
            cp.start()
            copies.append(cp)
        for k, (dx, dy, dc) in enumerate(flips):
            peer = (x ^ dx, y ^ dy, c ^ dc)
            pltpu.make_async_remote_copy(src_ref, out_ref.at[4 * peer[0] + 2 * peer[1] + peer[2]], send_sems.at[k],
                                         recv_sems.at[k], device_id=peer, device_id_type=MESH).wait_recv()
        for cp in copies:
            cp.wait_send()
        loc.wait()

    return pl.pallas_call(
        body, name=name, in_specs=[ANY], out_specs=ANY,
        out_shape=jax.ShapeDtypeStruct((8, rows, lanes), pack.dtype),
        scratch_shapes=[pltpu.SemaphoreType.DMA((7,)), pltpu.SemaphoreType.DMA((7,)), pltpu.SemaphoreType.DMA],
        compiler_params=pltpu.CompilerParams(has_side_effects=True),
    )(pack)


def _sum_devices(stack, name):
    ndev, rows, lanes = stack.shape

    def body(s_ref, o_ref):
        s = s_ref[0]
        for d in range(1, ndev):
            s = s + s_ref[d]
        o_ref[...] = s

    return pl.pallas_call(
        body, name=name, grid=(1,),
        in_specs=[pl.BlockSpec((ndev, rows, lanes), lambda i: (0, 0, 0))],
        out_specs=pl.BlockSpec((rows, lanes), lambda i: (0, 0)),
        out_shape=jax.ShapeDtypeStruct((rows, lanes), F32),
    )(stack)


WEIGHT_ORDER = ("ffn_norm1", "ffn1_w_in", "ffn1_w_out", "mix_norm", "ffn_norm2", "ffn2_w_in", "ffn2_w_out", "ab_w_in",
                "dn_conv_w", "dn_a_log", "dn_dt_bias", "dn_out_norm", "sg_norm", "sg_w", "sg_b", "ab_w_out", "pool_w",
                "pool_scale", "final_norm")
MATRICES = ("ffn1_w_in", "ffn1_w_out", "ffn2_w_in", "ffn2_w_out", "ab_w_in", "ab_w_out", "pool_w")


def _as_2d(a):
    return a.reshape(-1, a.shape[-1])


def kernel(x, ffn_norm1, ffn1_w_in, ffn1_w_out, mix_norm, ffn_norm2, ffn2_w_in, ffn2_w_out, ab_w_in, dn_conv_w, dn_a_log, dn_dt_bias, dn_out_norm, sg_norm, sg_w, sg_b, ab_w_out, pool_w, pool_scale, final_norm, loss_target, m_ffn_norm1, m_ffn1_w_in, m_ffn1_w_out, m_mix_norm, m_ffn_norm2, m_ffn2_w_in, m_ffn2_w_out, m_ab_w_in, m_dn_conv_w, m_dn_a_log, m_dn_dt_bias, m_dn_out_norm, m_sg_norm, m_sg_w, m_sg_b, m_ab_w_out, m_pool_w, m_pool_scale, m_final_norm, v_ffn_norm1, v_ffn1_w_in, v_ffn1_w_out, v_mix_norm, v_ffn_norm2, v_ffn2_w_in, v_ffn2_w_out, v_ab_w_in, v_dn_conv_w, v_dn_a_log, v_dn_dt_bias, v_dn_out_norm, v_sg_norm, v_sg_w, v_sg_b, v_ab_w_out, v_pool_w, v_pool_scale, v_final_norm):
    given = dict(locals())
    wts = {n: given[n] for n in WEIGHT_ORDER}
    mom_m = {n: given["m_" + n] for n in WEIGHT_ORDER}
    mom_v = {n: given["v_" + n] for n in WEIGHT_ORDER}

    shards = [wts[n].astype(BF16) if n in MATRICES else wts[n] for n in SHARDED]
    gathered = dict(zip(SHARDED, _all_gather_chips(shards, [n in MATRICES for n in SHARDED], "weight_all_gather")))
    rep = {n: wts[n] for n in REPLICATED}
    w = _kernel_layouts(gathered, rep)

    loss, dx, g = _local_step(x[0], loss_target[0], w)
    g_sharded, g_rep = _grads_by_chip(g)

    grads = _reduce_sharded(g_sharded, {n: wts[n].shape for n in SHARDED})
    small = _sum_devices(_all_to_all_small(_pack_small(g_rep), "grad_small_exchange"), "grad_small_sum")
    grads.update(_unpack_small(small, rep))

    delta, new_m, new_v = {}, {}, {}
    for n in SHARDED:
        d, m1, v1 = _adamw(_as_2d(wts[n]), _as_2d(grads[n]), _as_2d(mom_m[n]), _as_2d(mom_v[n]), f"adamw_{n}")
        delta[n], new_m[n], new_v[n] = (t.reshape(wts[n].shape) for t in (d, m1, v1))
    d, m1, v1 = _adamw(_pack_small(rep), small, _pack_small({n: mom_m[n] for n in REPLICATED}),
                       _pack_small({n: mom_v[n] for n in REPLICATED}), "adamw_replicated")
    for tgt, packed in ((delta, d), (new_m, m1), (new_v, v1)):
        tgt.update(_unpack_small(packed, rep))

    total = lax.psum(loss[0, 0], ("x", "y", "c"))
    outs = [total, dx[None]]
    for group in (grads, delta, new_m, new_v):
        outs.extend(group[n] for n in WEIGHT_ORDER)
    return tuple(outs)
```

```python
import functools

import jax
import jax.numpy as jnp
from jax import lax
from jax.experimental import pallas as pl
from jax.experimental.pallas import tpu as pltpu

F32, BF16 = jnp.float32, jnp.bfloat16
NORM_EPS = 1e-6
D_MODEL = 1024
D_FF = 2816
N_CHIPS = 4
FF_CHUNK = 2 * D_FF // N_CHIPS
DN_HEADS, DN_DIM, DN_CHUNK, DN_CONV = 4, 128, 64, 4
DN_BLOCK = 2 * DN_CHUNK
DN_PREP_CHUNKS = 4
DN_SCAN_CHUNKS = 8
SG_GROUPS, SG_DIM, SG_CHUNK = 4, 128, 128
POOL_WINDOWS = (2, 4, 8, 16)
POOL_DIM = 256
POOL_HALO = 16
CONV_HALO = 8
PROJ_W = 3200
BA_BLOCK = 3072 // 128
ADAM_LR, ADAM_B1, ADAM_B2, ADAM_EPS, ADAM_WD, ADAM_STEP = 0.001, 0.9, 0.999, 1e-08, 0.01, 10
VMEM_BIG = 52 * 1024 * 1024
FFN_FWD_ROWS = 512
FFN_BWD_ROWS = 256
PACK_LANES = 1024
PACK_ROW_BLOCK = 256
MESH = pl.DeviceIdType.MESH
HI = lax.Precision.HIGHEST
ANY = pl.BlockSpec(memory_space=pl.ANY)


def _params(sem=None, vmem=None):
    return pltpu.CompilerParams(dimension_semantics=sem, vmem_limit_bytes=vmem)


def _dot(a, b):
    return jnp.dot(a, b, preferred_element_type=F32)


def _dot_nt(a, b):
    return lax.dot_general(a, b, (((1,), (1,)), ((), ())), preferred_element_type=F32)


def _dot_tn(a, b):
    return lax.dot_general(a, b, (((0,), (0,)), ((), ())), preferred_element_type=F32)


def _dot_hi(a, b):
    return jnp.dot(a, b, preferred_element_type=F32, precision=HI)


def _dot_mid(a, b):
    return jnp.dot(a, b, preferred_element_type=F32, precision=lax.Precision.HIGH)


def _bf(a):
    return a.astype(BF16)


def _rms_stats(x):
    r = lax.rsqrt(jnp.mean(x * x, axis=-1, keepdims=True) + NORM_EPS)
    return x * r, r


def _rms_bwd(dh, xhat, r, w):
    dhn = dh * w
    return r * (dhn - xhat * jnp.mean(dhn * xhat, axis=-1, keepdims=True))


def _sigmoid(x):
    return jax.nn.sigmoid(x)


def _silu_grad(x, s):
    return s * (1.0 + x * (1.0 - s))


def _gelu(x):
    return 0.5 * x * (1.0 + lax.erf(x * 0.7071067811865476))


def _gelu_grad(x):
    return 0.5 * (1.0 + lax.erf(x * 0.7071067811865476)) + x * jnp.exp(-0.5 * x * x) * 0.3989422804014327


def _softplus(x):
    return jnp.maximum(x, 0.0) + jnp.log(1.0 + jnp.exp(-jnp.abs(x)))


def _tile(n, pref):
    t = min(n, pref)
    assert n % t == 0, (n, t)
    return t


def _ffn_weight_specs(layer):
    once = pl.Buffered(1)
    return [pl.BlockSpec((N_CHIPS, None, D_MODEL, FF_CHUNK), lambda i: (0, layer, 0, 0), pipeline_mode=once),
            pl.BlockSpec((N_CHIPS, None, D_FF // N_CHIPS, D_MODEL), lambda i: (0, layer, 0, 0), pipeline_mode=once)]


def _ffn_fwd(x, nw, win, wout, layer, name):
    T, D = x.shape
    tm = _tile(T, FFN_FWD_ROWS)
    nj = N_CHIPS // 2

    def body(x_ref, n_ref, win_ref, wo_ref, xo_ref, g_ref, u_ref, hb_ref):
        xv = x_ref[...]
        xhat, _ = _rms_stats(xv)
        h = _bf(xhat * n_ref[...])
        hb_ref[...] = h
        acc = None
        for j in range(nj):
            cols = slice(j * FF_CHUNK, (j + 1) * FF_CHUNK)
            g = _dot(h, win_ref[j])
            u = _dot(h, win_ref[nj + j])
            g_ref[:, cols] = _bf(g)
            u_ref[:, cols] = _bf(u)
            part = _dot(_bf(g * _sigmoid(g) * u), wo_ref[2 * j:2 * j + 2].reshape(FF_CHUNK, D))
            acc = part if acc is None else acc + part
        xo_ref[...] = xv + 0.5 * acc

    row = pl.BlockSpec((tm, D), lambda i: (i, 0))
    wide = pl.BlockSpec((tm, D_FF), lambda i: (i, 0))
    return pl.pallas_call(
        body, name=name, grid=(T // tm,),
        in_specs=[row, pl.BlockSpec((None, 1, D), lambda i: (layer, 0, 0))] + _ffn_weight_specs(layer),
        out_specs=[row, wide, wide, row],
        out_shape=[jax.ShapeDtypeStruct((T, D), F32), jax.ShapeDtypeStruct((T, D_FF), BF16),
                   jax.ShapeDtypeStruct((T, D_FF), BF16), jax.ShapeDtypeStruct((T, D), BF16)],
        compiler_params=_params(("parallel",), VMEM_BIG),
    )(x, nw, win, wout)


def _ffn_bwd(dxo, x, nw, g, u, win, wout, layer, name):
    T, D = x.shape
    tm = _tile(T, FFN_BWD_ROWS)
    nj = N_CHIPS // 2

    def body(dxo_ref, x_ref, n_ref, g_ref, u_ref, win_ref, wo_ref, dx_ref, dg_ref, du_ref, a_ref, dyb_ref, dn_ref):
        @pl.when(pl.program_id(0) == 0)
        def _():
            dn_ref[...] = jnp.zeros_like(dn_ref)

        dxo = dxo_ref[...]
        dyb = _bf(0.5 * dxo)
        dyb_ref[...] = dyb
        dh = None
        for j in range(nj):
            cols = slice(j * FF_CHUNK, (j + 1) * FF_CHUNK)
            da = _dot_nt(dyb, wo_ref[2 * j:2 * j + 2].reshape(FF_CHUNK, D))
            gv = g_ref[:, cols].astype(F32)
            uv = u_ref[:, cols].astype(F32)
            sg = _sigmoid(gv)
            sl = gv * sg
            dgb = _bf(da * uv * _silu_grad(gv, sg))
            dub = _bf(da * sl)
            a_ref[:, cols] = _bf(sl * uv)
            dg_ref[:, cols] = dgb
            du_ref[:, cols] = dub
            part = _dot_nt(dgb, win_ref[j]) + _dot_nt(dub, win_ref[nj + j])
            dh = part if dh is None else dh + part
        xhat, r = _rms_stats(x_ref[...])
        dx_ref[...] = dxo + _rms_bwd(dh, xhat, r, n_ref[...])
        dn_ref[...] += jnp.sum(dh * xhat, axis=0, keepdims=True)

    row = pl.BlockSpec((tm, D), lambda i: (i, 0))
    wide = pl.BlockSpec((tm, D_FF), lambda i: (i, 0))
    return pl.pallas_call(
        body, name=name, grid=(T // tm,),
        in_specs=[row, row, pl.BlockSpec((None, 1, D), lambda i: (layer, 0, 0)), wide, wide] + _ffn_weight_specs(layer),
        out_specs=[row, wide, wide, wide, row, pl.BlockSpec((1, D), lambda i: (0, 0))],
        out_shape=[jax.ShapeDtypeStruct((T, D), F32), jax.ShapeDtypeStruct((T, D_FF), BF16),
                   jax.ShapeDtypeStruct((T, D_FF), BF16), jax.ShapeDtypeStruct((T, D_FF), BF16),
                   jax.ShapeDtypeStruct((T, D), BF16), jax.ShapeDtypeStruct((1, D), F32)],
        compiler_params=_params(("arbitrary",), VMEM_BIG),
    )(dxo, x, nw, g, u, win, wout)


def _matmul_tn(a, b, bm, bn, name, stack_n=False):
    T, M = a.shape
    N = b.shape[1]
    tk = _tile(T, 1024)
    bm, bn = _tile(M, bm), _tile(N, bn)

    def body(a_ref, b_ref, o_ref):
        @pl.when(pl.program_id(2) == 0)
        def _():
            o_ref[...] = jnp.zeros_like(o_ref)

        o_ref[...] += _dot_tn(_bf(a_ref[...]), _bf(b_ref[...]))

    if stack_n:
        out_spec = pl.BlockSpec((None, bm, bn), lambda m, n, k: (n, m, 0))
        out_shape = jax.ShapeDtypeStruct((N // bn, M, bn), F32)
    else:
        out_spec = pl.BlockSpec((bm, bn), lambda m, n, k: (m, n))
        out_shape = jax.ShapeDtypeStruct((M, N), F32)
    return pl.pallas_call(
        body, name=name, grid=(M // bm, N // bn, T // tk),
        in_specs=[pl.BlockSpec((tk, bm), lambda m, n, k: (k, m)),
                  pl.BlockSpec((tk, bn), lambda m, n, k: (k, n))],
        out_specs=out_spec, out_shape=out_shape,
        compiler_params=_params(("parallel", "parallel", "arbitrary"), VMEM_BIG),
    )(a, b)


def _matmul(a, b, name, trans_b=False, res=None, out_dtype=F32):
    T, K = a.shape
    N = b.shape[0] if trans_b else b.shape[1]
    tm = _tile(T, 512)

    def body(*refs):
        a_ref, b_ref = refs[0], refs[1]
        o_ref = refs[-1]
        av, bv = _bf(a_ref[...]), _bf(b_ref[...])
        acc = _dot_nt(av, bv) if trans_b else _dot(av, bv)
        if res is not None:
            acc = acc + refs[2][...]
        o_ref[...] = acc.astype(out_dtype)

    in_specs = [pl.BlockSpec((tm, K), lambda i: (i, 0)), pl.BlockSpec(b.shape, lambda i: (0, 0))]
    args = [a, b]
    if res is not None:
        in_specs.append(pl.BlockSpec((tm, N), lambda i: (i, 0)))
        args.append(res)
    return pl.pallas_call(
        body, name=name, grid=(T // tm,), in_specs=in_specs,
        out_specs=pl.BlockSpec((tm, N), lambda i: (i, 0)),
        out_shape=jax.ShapeDtypeStruct((T, N), out_dtype),
        compiler_params=_params(("parallel",), VMEM_BIG),
    )(*args)


def _rms_fwd_call(x, nw, layer, name):
    T, D = x.shape
    tm = _tile(T, 512)

    def body(x_ref, n_ref, o_ref):
        xhat, _ = _rms_stats(x_ref[...])
        o_ref[...] = _bf(xhat * n_ref[...])

    return pl.pallas_call(
        body, name=name, grid=(T // tm,),
        in_specs=[pl.BlockSpec((tm, D), lambda i: (i, 0)), pl.BlockSpec((None, 1, D), lambda i: (layer, 0, 0))],
        out_specs=pl.BlockSpec((tm, D), lambda i: (i, 0)),
        out_shape=jax.ShapeDtypeStruct((T, D), BF16),
        compiler_params=_params(("parallel",)),
    )(x, nw)


def _rms_bwd_call(dh, x, nw, dres, layer, name):
    T, D = x.shape
    tm = _tile(T, 512)

    def body(dh_ref, x_ref, n_ref, dr_ref, dx_ref, dn_ref):
        @pl.when(pl.program_id(0) == 0)
        def _():
            dn_ref[...] = jnp.zeros_like(dn_ref)

        xhat, r = _rms_stats(x_ref[...])
        dh_v = dh_ref[...]
        dx_ref[...] = dr_ref[...] + _rms_bwd(dh_v, xhat, r, n_ref[...])
        dn_ref[...] += jnp.sum(dh_v * xhat, axis=0, keepdims=True)

    row = pl.BlockSpec((tm, D), lambda i: (i, 0))
    return pl.pallas_call(
        body, name=name, grid=(T // tm,),
        in_specs=[row, row, pl.BlockSpec((None, 1, D), lambda i: (layer, 0, 0)), row],
        out_specs=[row, pl.BlockSpec((1, D), lambda i: (0, 0))],
        out_shape=[jax.ShapeDtypeStruct((T, D), F32), jax.ShapeDtypeStruct((1, D), F32)],
        compiler_params=_params(("arbitrary",)),
    )(dh, x, nw, dres)


def _shift_rows(x, s):
    n = x.shape[0]
    s = s % n
    return x if s == 0 else pltpu.roll(x, s, 0)


def _conv_fwd(proj, conv_w, name):
    T = proj.shape[0]
    C = 3 * DN_HEADS * DN_DIM
    cb = 512
    tm = _tile(T, 512)
    hb = tm // CONV_HALO

    def body(x_ref, xp_ref, w_ref, o_ref):
        i = pl.program_id(1)
        prev = jnp.where(i == 0, 0.0, xp_ref[...])
        ext = jnp.concatenate([prev, x_ref[...]], axis=0)
        w = w_ref[...]
        y = ext * w[DN_CONV - 1:DN_CONV, :]
        for k in range(DN_CONV - 1):
            y = y + _shift_rows(ext, DN_CONV - 1 - k) * w[k:k + 1, :]
        y = y[CONV_HALO:, :]
        o_ref[...] = y * _sigmoid(y)

    return pl.pallas_call(
        body, name=name, grid=(C // cb, T // tm),
        in_specs=[pl.BlockSpec((tm, cb), lambda c, i: (i, c)),
                  pl.BlockSpec((CONV_HALO, cb), lambda c, i: (jnp.maximum(i * hb - 1, 0), c)),
                  pl.BlockSpec((DN_CONV, cb), lambda c, i: (0, c))],
        out_specs=pl.BlockSpec((tm, cb), lambda c, i: (i, c)),
        out_shape=jax.ShapeDtypeStruct((T, C), F32),
        compiler_params=_params(("parallel", "parallel")),
    )(proj, proj, conv_w)


def _conv_bwd(dy, proj, conv_w, name):
    T = proj.shape[0]
    C = 3 * DN_HEADS * DN_DIM
    cb = 512
    tm = _tile(T, 512)
    hb = tm // CONV_HALO
    nt = T // tm

    def body(x_ref, xp_ref, xn_ref, dy_ref, dyn_ref, w_ref, dx_ref, dw_ref):
        i = pl.program_id(1)

        @pl.when(i == 0)
        def _():
            dw_ref[...] = jnp.zeros_like(dw_ref)

        prev = jnp.where(i == 0, 0.0, xp_ref[...])
        ext = jnp.concatenate([prev, x_ref[...], xn_ref[...]], axis=0)
        dy_ext = jnp.concatenate([jnp.zeros((CONV_HALO, cb), F32), dy_ref[...],
                                  jnp.where(i == nt - 1, 0.0, dyn_ref[...])], axis=0)
        w = w_ref[...]
        shifted = [_shift_rows(ext, DN_CONV - 1 - k) for k in range(DN_CONV)]
        y = shifted[0] * w[0:1, :]
        for k in range(1, DN_CONV):
            y = y + shifted[k] * w[k:k + 1, :]
        s = _sigmoid(y)
        dpre = dy_ext * _silu_grad(y, s)
        dx = dpre * w[DN_CONV - 1:DN_CONV, :]
        for k in range(DN_CONV - 1):
            dx = dx + _shift_rows(dpre, -(DN_CONV - 1 - k)) * w[k:k + 1, :]
        dx_ref[...] = _bf(dx[CONV_HALO:CONV_HALO + tm, :])
        rows = [jnp.sum((dpre * shifted[k])[CONV_HALO:CONV_HALO + tm, :], axis=0, keepdims=True) for k in range(DN_CONV)]
        dw_ref[...] += jnp.concatenate(rows, axis=0)

    last_halo = T // CONV_HALO - 1
    return pl.pallas_call(
        body, name=name, grid=(C // cb, nt),
        in_specs=[pl.BlockSpec((tm, cb), lambda c, i: (i, c)),
                  pl.BlockSpec((CONV_HALO, cb), lambda c, i: (jnp.maximum(i * hb - 1, 0), c)),
                  pl.BlockSpec((CONV_HALO, cb), lambda c, i: (jnp.minimum((i + 1) * hb, last_halo), c)),
                  pl.BlockSpec((tm, cb), lambda c, i: (i, c)),
                  pl.BlockSpec((CONV_HALO, cb), lambda c, i: (jnp.minimum((i + 1) * hb, last_halo), c)),
                  pl.BlockSpec((DN_CONV, cb), lambda c, i: (0, c))],
        out_specs=[pl.BlockSpec((tm, cb), lambda c, i: (i, c)),
                   pl.BlockSpec((DN_CONV, cb), lambda c, i: (0, c))],
        out_shape=[jax.ShapeDtypeStruct((T, C), BF16), jax.ShapeDtypeStruct((DN_CONV, C), F32)],
        compiler_params=_params(("parallel", "arbitrary")),
    )(proj, proj, proj, dy, dy, conv_w)


def _unit_lower_inverse(low, eye):
    p1 = -low
    p2 = _dot_mid(p1, p1)
    p4 = _dot_mid(p2, p2)
    a = eye + p1 + p2 + _dot_mid(p1, p2)
    p8 = _dot_mid(p4, p4)
    p16 = _dot_mid(p8, p8)
    b = eye + p4 + p8 + _dot_mid(p4, p8)
    p32 = _dot_mid(p16, p16)
    ab = _dot_mid(a, b)
    c = eye + p16 + p32 + _dot_mid(p16, p32)
    return _dot_mid(ab, c)


def _l2_unit(x):
    r = lax.rsqrt(jnp.sum(x * x, axis=-1, keepdims=True) + NORM_EPS)
    return x * r, r


class _BlockMasks:
    def __init__(self):
        n = DN_BLOCK
        row = lax.broadcasted_iota(jnp.int32, (n, n), 0)
        col = lax.broadcasted_iota(jnp.int32, (n, n), 1)
        same = (row // DN_CHUNK) == (col // DN_CHUNK)
        self.lower, self.strict_lower = same & (row >= col), same & (row > col)
        self.upper, self.strict_upper = same & (row <= col), same & (row < col)
        self.eye = (row == col).astype(F32)
        self.first = lax.broadcasted_iota(jnp.int32, (n, 1), 0) < DN_CHUNK


def _dn_gates(ba, hp):
    coef = -jnp.exp(hp[0:1, :])
    pre = ba + hp[1:2, :]
    return _sigmoid(ba), coef * _softplus(pre), coef, pre


def _dn_block_gates(mk, ba, hp):
    assert DN_BLOCK == 2 * DN_CHUNK
    beta_t, graw_t, coef, pre = _dn_gates(ba, hp)
    gcum_t = _dot_hi(mk.lower.astype(F32), graw_t)
    gl_t = jnp.where(mk.first, gcum_t[DN_CHUNK - 1:DN_CHUNK, :], gcum_t[DN_BLOCK - 1:DN_BLOCK, :])
    return beta_t, gcum_t, gl_t, graw_t, coef, pre


def _dn_local(mk, qraw, kraw, bc, gc, gl):
    f = {}
    f["qn"], f["rq"] = _l2_unit(qraw)
    qh = f["qn"] * (DN_DIM ** -0.5)
    kh, f["rk"] = _l2_unit(kraw)
    gr = jnp.broadcast_to(gc, (DN_BLOCK, DN_BLOCK)).T
    dec = jnp.where(mk.lower, jnp.exp(jnp.where(mk.lower, gc - gr, 0.0)), 0.0)
    kb = kh * bc
    mkk = _dot_nt(_bf(kb), _bf(kh))
    eg = jnp.exp(gc)
    mqk = _dot_nt(_bf(qh), _bf(kh))
    etl = jnp.exp(gl - gc)
    f.update(qh=qh, kh=kh, gr=gr, dec=dec, kb=kb, mkk=mkk, eg=eg, mqk=mqk, attn=mqk * dec, etl=etl, qd=qh * eg, kt=kh * etl)
    return f


def _dn_specs(rows, rev=None):
    at = (lambda n: n) if rev is None else rev
    hw = DN_HEADS * DN_DIM
    return dict(
        qkv=[pl.BlockSpec((rows, hw), lambda n, j=j: (at(n), j)) for j in range(3)],
        ba=pl.BlockSpec((rows, 128), lambda n: (at(n), BA_BLOCK)),
        hp=pl.BlockSpec((8, 128), lambda n: (0, 0)),
        tok=pl.BlockSpec((rows, hw), lambda n: (at(n), 0)),
        attn=pl.BlockSpec((rows, DN_HEADS * DN_CHUNK), lambda n: (at(n), 0)),
        gate=pl.BlockSpec((rows // DN_CHUNK, 8, 128), lambda n: (at(n), 0, 0)),
        state=pl.BlockSpec((rows // DN_CHUNK, DN_HEADS, DN_DIM, DN_DIM), lambda n: (at(n), 0, 0, 0)),
    )


def _dn_prep(qkv, proj, hp, name):
    T = qkv.shape[0]
    n_chunks = T // DN_CHUNK
    blocks = max(1, min(DN_PREP_CHUNKS, n_chunks) * DN_CHUNK // DN_BLOCK)
    group = blocks * DN_BLOCK // DN_CHUNK
    rows = blocks * DN_BLOCK
    hw = DN_HEADS * DN_DIM

    def body(q_ref, k_ref, v_ref, ba_ref, hp_ref, u_ref, w_ref, p_ref, qd_ref, kt_ref, gl_ref, inv_ref):
        mk = _BlockMasks()
        hp_v = hp_ref[...]
        for j in range(blocks):
            rs = slice(j * DN_BLOCK, (j + 1) * DN_BLOCK)
            beta_t, gcum_t, gl_t = _dn_block_gates(mk, ba_ref[rs, :], hp_v)[:3]
            for c in range(DN_BLOCK // DN_CHUNK):
                gl_ref[j * (DN_BLOCK // DN_CHUNK) + c] = jnp.broadcast_to(gl_t[c * DN_CHUNK:c * DN_CHUNK + 1, :], (8, 128))
            for h in range(DN_HEADS):
                sl = slice(h * DN_DIM, (h + 1) * DN_DIM)
                gate = slice(DN_HEADS + h, DN_HEADS + h + 1)
                bc = beta_t[:, h:h + 1]
                f = _dn_local(mk, q_ref[rs, sl], k_ref[rs, sl], bc, gcum_t[:, gate], gl_t[:, gate])
                inv = _unit_lower_inverse(jnp.where(mk.strict_lower, f["mkk"] * f["dec"], 0.0), mk.eye)
                inv_ref[rs, sl] = inv
                sol = _dot_mid(inv, jnp.concatenate([v_ref[rs, sl] * bc, f["kb"] * f["eg"]], axis=1))
                u_ref[rs, sl] = sol[:, :DN_DIM]
                w_ref[rs, sl] = _bf(sol[:, DN_DIM:])
                for c in range(DN_BLOCK // DN_CHUNK):
                    cr = slice(c * DN_CHUNK, (c + 1) * DN_CHUNK)
                    p_ref[j * DN_BLOCK + c * DN_CHUNK:j * DN_BLOCK + (c + 1) * DN_CHUNK, h * DN_CHUNK:(h + 1) * DN_CHUNK] = _bf(f["attn"][cr, cr])
                qd_ref[rs, sl] = _bf(f["qd"])
                kt_ref[rs, sl] = _bf(f["kt"])

    sp = _dn_specs(rows)
    tok16 = jax.ShapeDtypeStruct((T, hw), BF16)
    return pl.pallas_call(
        body, name=name, grid=(n_chunks // group,),
        in_specs=sp["qkv"] + [sp["ba"], sp["hp"]],
        out_specs=[sp["tok"], sp["tok"], sp["attn"], sp["tok"], sp["tok"], sp["gate"], sp["tok"]],
        out_shape=[jax.ShapeDtypeStruct((T, hw), F32), tok16, jax.ShapeDtypeStruct((T, DN_HEADS * DN_CHUNK), BF16),
                   tok16, tok16, jax.ShapeDtypeStruct((n_chunks, 8, 128), F32), jax.ShapeDtypeStruct((T, hw), F32)],
        compiler_params=_params(("parallel",)),
    )(qkv, qkv, qkv, proj, hp)


def _dn_scan(u, w, p, qd, kt, gl, name):
    T = u.shape[0]
    n_chunks = T // DN_CHUNK
    group = min(DN_SCAN_CHUNKS, n_chunks)
    rows = group * DN_CHUNK
    hw = DN_HEADS * DN_DIM

    def body(u_ref, w_ref, p_ref, qd_ref, kt_ref, gl_ref, o_ref, vn_ref, sall_ref, s_s):
        @pl.when(pl.program_id(0) == 0)
        def _():
            s_s[...] = jnp.zeros_like(s_s)

        state = [s_s[h] for h in range(DN_HEADS)]
        for j in range(group):
            rs = slice(j * DN_CHUNK, (j + 1) * DN_CHUNK)
            for h in range(DN_HEADS):
                sl = slice(h * DN_DIM, (h + 1) * DN_DIM)
                sall_ref[j, h] = state[h]
                sb = _bf(state[h])
                vnb = _bf(u_ref[rs, sl] - _dot(w_ref[rs, sl], sb))
                vn_ref[rs, sl] = vnb
                o_ref[rs, sl] = _dot(qd_ref[rs, sl], sb) + _dot(p_ref[rs, h * DN_CHUNK:(h + 1) * DN_CHUNK], vnb)
                egl = jnp.exp(gl_ref[j, 0:1, DN_HEADS + h:DN_HEADS + h + 1])
                state[h] = state[h] * egl + _dot_tn(kt_ref[rs, sl], vnb)
        for h in range(DN_HEADS):
            s_s[h] = state[h]

    sp = _dn_specs(rows)
    return pl.pallas_call(
        body, name=name, grid=(n_chunks // group,),
        in_specs=[sp["tok"], sp["tok"], sp["attn"], sp["tok"], sp["tok"], sp["gate"]],
        out_specs=[sp["tok"], sp["tok"], sp["state"]],
        out_shape=[jax.ShapeDtypeStruct((T, hw), F32), jax.ShapeDtypeStruct((T, hw), BF16),
                   jax.ShapeDtypeStruct((n_chunks, DN_HEADS, DN_DIM, DN_DIM), F32)],
        scratch_shapes=[pltpu.VMEM((DN_HEADS, DN_DIM, DN_DIM), F32)],
        compiler_params=_params(("arbitrary",)),
    )(u, w, p, qd, kt, gl)


def _dn_scan_bwd(w, p, qd, kt, gl, vn, sall, do, name):
    T = w.shape[0]
    n_chunks = T // DN_CHUNK
    group = min(DN_SCAN_CHUNKS, n_chunks)
    rows = group * DN_CHUNK
    hw = DN_HEADS * DN_DIM
    last = n_chunks // group - 1

    def body(w_ref, p_ref, qd_ref, kt_ref, gl_ref, vn_ref, sall_ref, do_ref, dvn_ref, dkt_ref, dgl_ref, ds_s):
        @pl.when(pl.program_id(0) == 0)
        def _():
            ds_s[...] = jnp.zeros_like(ds_s)

        lane = lax.broadcasted_iota(jnp.int32, (8, 128), 1)
        d_state = [ds_s[h] for h in range(DN_HEADS)]
        for j in reversed(range(group)):
            rs = slice(j * DN_CHUNK, (j + 1) * DN_CHUNK)
            dgl_tile = jnp.zeros((8, 128), F32)
            for h in range(DN_HEADS):
                sl = slice(h * DN_DIM, (h + 1) * DN_DIM)
                d_out = _bf(do_ref[rs, sl])
                d_new = d_state[h]
                d_newb = _bf(d_new)
                d_vn = _dot_tn(p_ref[rs, h * DN_CHUNK:(h + 1) * DN_CHUNK], d_out) + _dot(kt_ref[rs, sl], d_newb)
                dvn_ref[rs, sl] = d_vn
                dkt_ref[rs, sl] = _dot_nt(vn_ref[rs, sl], d_newb)
                egl = jnp.exp(gl_ref[j, 0:1, DN_HEADS + h:DN_HEADS + h + 1])
                prod = jnp.sum(d_new * sall_ref[j, h], axis=1, keepdims=True)
                dgl_tile = jnp.where(lane == DN_HEADS + h, jnp.sum(prod, axis=0, keepdims=True) * egl, dgl_tile)
                d_state[h] = d_new * egl + _dot_tn(qd_ref[rs, sl], d_out) - _dot_tn(w_ref[rs, sl], _bf(d_vn))
            dgl_ref[j] = dgl_tile
        for h in range(DN_HEADS):
            ds_s[h] = d_state[h]

    sp = _dn_specs(rows, rev=lambda n: last - n)
    return pl.pallas_call(
        body, name=name, grid=(n_chunks // group,),
        in_specs=[sp["tok"], sp["attn"], sp["tok"], sp["tok"], sp["gate"], sp["tok"], sp["state"], sp["tok"]],
        out_specs=[sp["tok"], sp["tok"], sp["gate"]],
        out_shape=[jax.ShapeDtypeStruct((T, hw), F32), jax.ShapeDtypeStruct((T, hw), F32),
                   jax.ShapeDtypeStruct((n_chunks, 8, 128), F32)],
        scratch_shapes=[pltpu.VMEM((DN_HEADS, DN_DIM, DN_DIM), F32)],
        compiler_params=_params(("arbitrary",)),
    )(w, p, qd, kt, gl, vn, sall, do)


def _dn_prep_bwd(qkv, proj, hp, sall, vn, do, dvn, dkt, dgl, inv, u, w, name):
    T = qkv.shape[0]
    n_chunks = T // DN_CHUNK
    blocks = max(1, min(DN_PREP_CHUNKS, n_chunks) * DN_CHUNK // DN_BLOCK)
    per_block = DN_BLOCK // DN_CHUNK
    group = blocks * per_block
    rows = blocks * DN_BLOCK
    hw = DN_HEADS * DN_DIM
    first_rows, second_rows = slice(0, DN_CHUNK), slice(DN_CHUNK, DN_BLOCK)

    def rowsum(x):
        return jnp.sum(x, axis=1, keepdims=True)

    def by_chunk(x, s0, s1, fn):
        return jnp.concatenate([fn(x[first_rows], s0), fn(x[second_rows], s1)], axis=0)

    def body(q_ref, k_ref, v_ref, ba_ref, hp_ref, sall_ref, vn_ref, do_ref, dvn_ref, dkt_ref, dgl_ref,
             inv_ref, u_ref, w_ref, dqkv_ref, dba_ref, dhp_ref):
        @pl.when(pl.program_id(0) == 0)
        def _():
            dhp_ref[...] = jnp.zeros_like(dhp_ref)

        mk = _BlockMasks()
        hp_v = hp_ref[...]
        total = jnp.zeros((8, 128), F32)
        for j in range(blocks):
            rs = slice(j * DN_BLOCK, (j + 1) * DN_BLOCK)
            total = total + one_block(mk, hp_v, *(r.at[rs, :] for r in (q_ref, k_ref, v_ref, ba_ref)),
                                      sall_ref.at[pl.ds(j * per_block, per_block)],
                                      *(r.at[rs, :] for r in (vn_ref, do_ref, dvn_ref, dkt_ref)),
                                      dgl_ref.at[pl.ds(j * per_block, per_block)],
                                      *(r.at[rs, :] for r in (inv_ref, u_ref, w_ref)),
                                      *(dqkv_ref.at[rs, pl.ds(i * hw, hw)] for i in range(3)), dba_ref.at[rs, :])
        dhp_ref[...] += total

    def one_block(mk, hp_v, q_ref, k_ref, v_ref, ba_ref, state_ref, vn_ref, do_ref, dvn_ref, dkt_ref, dgl_ref,
                  inv_ref, u_ref, w_ref, dq_ref, dk_ref, dv_ref, dba_ref):
        ba = ba_ref[...]
        beta_t, gcum_t, gl_t, graw_t, coef, pre = _dn_block_gates(mk, ba, hp_v)
        lane = lax.broadcasted_iota(jnp.int32, (DN_BLOCK, 128), 1)
        rowi = lax.broadcasted_iota(jnp.int32, (DN_BLOCK, 1), 0)
        dgcum_t = jnp.zeros((DN_BLOCK, 128), F32)
        dbeta_t = jnp.zeros((DN_BLOCK, 128), F32)
        for h in range(DN_HEADS):
            sl = slice(h * DN_DIM, (h + 1) * DN_DIM)
            gate = slice(DN_HEADS + h, DN_HEADS + h + 1)
            gc = gcum_t[:, gate]
            bc = beta_t[:, h:h + 1]
            sb0, sb1 = _bf(state_ref[0, h]), _bf(state_ref[1, h])
            vh = v_ref[:, sl]
            f = _dn_local(mk, q_ref[:, sl], k_ref[:, sl], bc, gc, gl_t[:, gate])
            qh, kh, kb, dec, eg, etl = f["qh"], f["kh"], f["kb"], f["dec"], f["eg"], f["etl"]
            qd, kt = f["qd"], f["kt"]
            qb, kbf, kbb = _bf(qh), _bf(kh), _bf(kb)
            dec_t = jnp.where(mk.upper, jnp.exp(jnp.where(mk.upper, f["gr"] - gc, 0.0)), 0.0)
            mkk_t = f["mkk"].T
            inv_t = inv_ref[:, sl].T
            mqk_t = f["mqk"].T

            d_out = _bf(do_ref[:, sl])
            vnb = vn_ref[:, sl]
            d_qd = by_chunk(d_out, sb0, sb1, _dot_nt)
            d_attn = _dot_nt(d_out, vnb)
            d_attn_t = _dot_nt(vnb, d_out)
            d_vn = dvn_ref[:, sl]
            d_kt = dkt_ref[:, sl]
            d_w = -by_chunk(_bf(d_vn), sb0, sb1, _dot_nt)
            d_rhs = _dot_mid(inv_t, jnp.concatenate([d_vn, d_w], axis=1))
            d_bu, d_bw = d_rhs[:, :DN_DIM], d_rhs[:, DN_DIM:]
            ub, wb, d_bub, d_bwb = _bf(u_ref[:, sl]), w_ref[:, sl], _bf(d_bu), _bf(d_bw)
            d_low = -(_dot_nt(d_bub, ub) + _dot_nt(d_bwb, wb))
            d_low_t = -(_dot_nt(ub, d_bub) + _dot_nt(wb, d_bwb))
            d_mkk = jnp.where(mk.strict_lower, d_low * dec, 0.0)
            d_mkk_t = jnp.where(mk.strict_upper, d_low_t * dec_t, 0.0)
            d_mqk = jnp.where(mk.lower, d_attn * dec, 0.0)
            d_mqk_t = jnp.where(mk.upper, d_attn_t * dec_t, 0.0)
            bw = kb * eg
            d_kb = _dot(_bf(d_mkk), kbf) + d_bw * eg
            d_k = _dot(_bf(d_mkk_t), kbb) + _dot(_bf(d_mqk_t), qb) + d_kt * etl + d_kb * bc
            d_q = _dot(_bf(d_mqk), kbf) + d_qd * eg
            d_beta = rowsum(d_kb * kh) + rowsum(d_bu * vh)
            dv_ref[:, sl] = d_bu * bc
            e_mat = d_mkk * f["mkk"] + d_mqk * f["mqk"]
            e_mat_t = d_mkk_t * mkk_t + d_mqk_t * mqk_t
            kt_term = rowsum(d_kt * kt)
            d_g = rowsum(e_mat) - rowsum(e_mat_t) + rowsum(d_qd * qd) + rowsum(d_bw * bw) - kt_term
            for c, chunk_rows in enumerate((mk.first, ~mk.first)):
                d_glast = dgl_ref[c, 0:1, gate] + jnp.sum(jnp.where(chunk_rows, kt_term, 0.0), axis=0, keepdims=True)
                d_g = d_g + jnp.where(rowi == (c + 1) * DN_CHUNK - 1, d_glast, 0.0)
            qn = f["qn"]
            d_qs = d_q * (DN_DIM ** -0.5)
            dq_ref[:, sl] = f["rq"] * (d_qs - qn * rowsum(d_qs * qn))
            dk_ref[:, sl] = f["rk"] * (d_k - kh * rowsum(d_k * kh))
            dgcum_t = jnp.where(lane == DN_HEADS + h, d_g, dgcum_t)
            dbeta_t = jnp.where(lane == h, d_beta, dbeta_t)
        dgraw_t = _dot_hi(mk.upper.astype(F32), dgcum_t)
        sp = _sigmoid(pre)
        d_pre = dgraw_t * coef * sp
        dba_ref[...] = jnp.where(lane < DN_HEADS, dbeta_t * beta_t * (1.0 - beta_t),
                                 jnp.where(lane < 2 * DN_HEADS, d_pre, 0.0))
        in_g = (lane >= DN_HEADS) & (lane < 2 * DN_HEADS)
        d_alog = jnp.sum(jnp.where(in_g, dgraw_t * graw_t, 0.0), axis=0, keepdims=True)
        d_dtb = jnp.sum(jnp.where(in_g, d_pre, 0.0), axis=0, keepdims=True)
        return jnp.concatenate([d_alog, d_dtb, jnp.zeros((6, 128), F32)], axis=0)

    sp = _dn_specs(rows)
    return pl.pallas_call(
        body, name=name, grid=(n_chunks // group,),
        in_specs=sp["qkv"] + [sp["ba"], sp["hp"], sp["state"]] + [sp["tok"]] * 4 + [sp["gate"]] + [sp["tok"]] * 3,
        out_specs=[pl.BlockSpec((rows, 3 * hw), lambda n: (n, 0)), pl.BlockSpec((rows, 128), lambda n: (n, 0)), sp["hp"]],
        out_shape=[jax.ShapeDtypeStruct((T, 3 * hw), F32), jax.ShapeDtypeStruct((T, 128), F32),
                   jax.ShapeDtypeStruct((8, 128), F32)],
        compiler_params=_params(("arbitrary",)),
    )(qkv, qkv, qkv, proj, hp, sall, vn, do, dvn, dkt, dgl, inv, u, w)


def _mix_fwd(o, proj, dn_norm, sg_norm, sg_w, sg_bt, name):
    T = o.shape[0]
    tm = _tile(T, 512)
    hw = DN_HEADS * DN_DIM
    nc = tm // SG_CHUNK

    def body(o_ref, z_ref, su_ref, sv_ref, dnn_ref, sgn_ref, sgw_ref, sgb_ref, mix_ref):
        dnn = dnn_ref[...]
        for h in range(DN_HEADS):
            sl = slice(h * DN_DIM, (h + 1) * DN_DIM)
            xhat, _ = _rms_stats(o_ref[:, sl])
            z = z_ref[:, sl]
            mix_ref[:, sl] = _bf(xhat * dnn * (z * _sigmoid(z)))
        tri = lax.broadcasted_iota(jnp.int32, (SG_CHUNK, SG_CHUNK), 0) >= lax.broadcasted_iota(jnp.int32, (SG_CHUNK, SG_CHUNK), 1)
        for g in range(SG_GROUPS):
            sl = slice(g * SG_DIM, (g + 1) * SG_DIM)
            xhat, _ = _rms_stats(_gelu(sv_ref[:, sl]))
            svn = _bf(xhat * sgn_ref[g:g + 1, :])
            sua = _gelu(su_ref[:, sl])
            wt = _bf(jnp.where(tri, sgw_ref[g], 0.0))
            bias = sgb_ref[:, g:g + 1]
            for c in range(nc):
                rows = slice(c * SG_CHUNK, (c + 1) * SG_CHUNK)
                mixed = _dot(wt, svn[rows, :]) + bias
                mix_ref[rows, hw + g * SG_DIM:hw + (g + 1) * SG_DIM] = _bf(sua[rows, :] * mixed)

    full = lambda shape: pl.BlockSpec(shape, lambda i: (0,) * len(shape))
    return pl.pallas_call(
        body, name=name, grid=(T // tm,),
        in_specs=[pl.BlockSpec((tm, hw), lambda i: (i, 0)),
                  pl.BlockSpec((tm, hw), lambda i: (i, 3)),
                  pl.BlockSpec((tm, hw), lambda i: (i, 4)),
                  pl.BlockSpec((tm, hw), lambda i: (i, 5)),
                  full((1, DN_DIM)), full((SG_GROUPS, SG_DIM)), full((SG_GROUPS, SG_CHUNK, SG_CHUNK)),
                  full((SG_CHUNK, 128))],
        out_specs=pl.BlockSpec((tm, 2 * hw), lambda i: (i, 0)),
        out_shape=jax.ShapeDtypeStruct((T, 2 * hw), BF16),
        compiler_params=_params(("parallel",)),
    )(o, proj, proj, proj, dn_norm, sg_norm, sg_w, sg_bt)


def _mix_bwd(dmix, o, proj, dn_norm, sg_norm, sg_w, sg_bt, name):
    T = o.shape[0]
    tm = _tile(T, 512)
    hw = DN_HEADS * DN_DIM
    nc = tm // SG_CHUNK

    def body(dm_ref, o_ref, z_ref, su_ref, sv_ref, dnn_ref, sgn_ref, sgw_ref, sgb_ref,
             do_ref, dz_ref, ddnn_ref, dsgn_ref, dsgw_ref, dsgb_ref):
        @pl.when(pl.program_id(0) == 0)
        def _():
            ddnn_ref[...] = jnp.zeros_like(ddnn_ref)
            dsgn_ref[...] = jnp.zeros_like(dsgn_ref)
            dsgw_ref[...] = jnp.zeros_like(dsgw_ref)
            dsgb_ref[...] = jnp.zeros_like(dsgb_ref)

        dnn = dnn_ref[...]
        ddnn = jnp.zeros((1, DN_DIM), F32)
        for h in range(DN_HEADS):
            sl = slice(h * DN_DIM, (h + 1) * DN_DIM)
            xhat, r = _rms_stats(o_ref[:, sl])
            z = z_ref[:, sl]
            sz = _sigmoid(z)
            doa = dm_ref[:, sl]
            dyn = doa * (z * sz)
            dz_ref[:, sl] = _bf(doa * xhat * dnn * _silu_grad(z, sz))
            do_ref[:, sl] = _rms_bwd(dyn, xhat, r, dnn)
            ddnn = ddnn + jnp.sum(dyn * xhat, axis=0, keepdims=True)
        ddnn_ref[...] += ddnn
        tri = lax.broadcasted_iota(jnp.int32, (SG_CHUNK, SG_CHUNK), 0) >= lax.broadcasted_iota(jnp.int32, (SG_CHUNK, SG_CHUNK), 1)
        lane = lax.broadcasted_iota(jnp.int32, (SG_CHUNK, 128), 1)
        dsgb = jnp.zeros((SG_CHUNK, 128), F32)
        dsgn_rows = []
        for g in range(SG_GROUPS):
            sl = slice(g * SG_DIM, (g + 1) * SG_DIM)
            sv = sv_ref[:, sl]
            su = su_ref[:, sl]
            xhat, r = _rms_stats(_gelu(sv))
            sgn = sgn_ref[g:g + 1, :]
            svn = _bf(xhat * sgn)
            sua = _gelu(su)
            wt = _bf(jnp.where(tri, sgw_ref[g], 0.0))
            bias = sgb_ref[:, g:g + 1]
            dw = jnp.zeros((SG_CHUNK, SG_CHUNK), F32)
            db = jnp.zeros((SG_CHUNK, 1), F32)
            dsua, dsvn = [], []
            for c in range(nc):
                rows = slice(c * SG_CHUNK, (c + 1) * SG_CHUNK)
                mixed = _dot(wt, svn[rows, :]) + bias
                dob = dm_ref[rows, hw + g * SG_DIM:hw + (g + 1) * SG_DIM]
                dsua.append(dob * mixed)
                dmixed = dob * sua[rows, :]
                dmb = _bf(dmixed)
                dsvn.append(_dot_tn(wt, dmb))
                dw = dw + _dot_nt(dmb, svn[rows, :])
                db = db + jnp.sum(dmixed, axis=1, keepdims=True)
            dsua = jnp.concatenate(dsua, axis=0) if nc > 1 else dsua[0]
            dsvn = jnp.concatenate(dsvn, axis=0) if nc > 1 else dsvn[0]
            dz_ref[:, hw + g * SG_DIM:hw + (g + 1) * SG_DIM] = _bf(dsua * _gelu_grad(su))
            dz_ref[:, 2 * hw + g * SG_DIM:2 * hw + (g + 1) * SG_DIM] = _bf(_rms_bwd(dsvn, xhat, r, sgn) * _gelu_grad(sv))
            dsgn_rows.append(jnp.sum(dsvn * xhat, axis=0, keepdims=True))
            dsgw_ref[g] += jnp.where(tri, dw, 0.0)
            dsgb = jnp.where(lane == g, db, dsgb)
        dsgn_ref[...] += jnp.concatenate(dsgn_rows, axis=0)
        dsgb_ref[...] += dsgb

    full = lambda shape: pl.BlockSpec(shape, lambda i: (0,) * len(shape))
    return pl.pallas_call(
        body, name=name, grid=(T // tm,),
        in_specs=[pl.BlockSpec((tm, 2 * hw), lambda i: (i, 0)),
                  pl.BlockSpec((tm, hw), lambda i: (i, 0)),
                  pl.BlockSpec((tm, hw), lambda i: (i, 3)),
                  pl.BlockSpec((tm, hw), lambda i: (i, 4)),
                  pl.BlockSpec((tm, hw), lambda i: (i, 5)),
                  full((1, DN_DIM)), full((SG_GROUPS, SG_DIM)), full((SG_GROUPS, SG_CHUNK, SG_CHUNK)),
                  full((SG_CHUNK, 128))],
        out_specs=[pl.BlockSpec((tm, hw), lambda i: (i, 0)),
                   pl.BlockSpec((tm, 3 * hw), lambda i: (i, 0)),
                   full((1, DN_DIM)), full((SG_GROUPS, SG_DIM)), full((SG_GROUPS, SG_CHUNK, SG_CHUNK)),
                   full((SG_CHUNK, 128))],
        out_shape=[jax.ShapeDtypeStruct((T, hw), F32), jax.ShapeDtypeStruct((T, 3 * hw), BF16),
                   jax.ShapeDtypeStruct((1, DN_DIM), F32), jax.ShapeDtypeStruct((SG_GROUPS, SG_DIM), F32),
                   jax.ShapeDtypeStruct((SG_GROUPS, SG_CHUNK, SG_CHUNK), F32),
                   jax.ShapeDtypeStruct((SG_CHUNK, 128), F32)],
        compiler_params=_params(("arbitrary",)),
    )(dmix, o, proj, proj, proj, dn_norm, sg_norm, sg_w, sg_bt)


def _window_sums(h, sign):
    sums, s, w = {}, h, 1
    while w < POOL_WINDOWS[-1]:
        s = s + _shift_rows(s, sign * w)
        w *= 2
        sums[w] = s
    return sums


def _pool_counts(t_global):
    return [jnp.minimum(t_global + 1, win).astype(F32) for win in POOL_WINDOWS]


def _pooled_groups(ext_h, row0, tm):
    sums = _window_sums(ext_h, 1)
    t_global = row0 + lax.broadcasted_iota(jnp.int32, (tm, 1), 0)
    counts = _pool_counts(t_global)
    out = []
    for gi, win in enumerate(POOL_WINDOWS):
        cols = slice(gi * POOL_DIM, (gi + 1) * POOL_DIM)
        out.append(sums[win][POOL_HALO:, cols] / counts[gi] - ext_h[POOL_HALO:, cols])
    return out


def _pool_fwd(x, nw, pool_w, pool_scale, layer, name):
    T, D = x.shape
    tm = _tile(T, 256)
    hb = tm // POOL_HALO

    def body(x_ref, xp_ref, n_ref, w_ref, s_ref, xo_ref):
        i = pl.program_id(0)
        prev = jnp.where(i == 0, 0.0, xp_ref[...])
        ext = jnp.concatenate([prev, x_ref[...]], axis=0)
        xhat, _ = _rms_stats(ext)
        pooled = _pooled_groups(xhat * n_ref[...], i * tm, tm)
        for gi in range(len(POOL_WINDOWS)):
            cols = slice(gi * POOL_DIM, (gi + 1) * POOL_DIM)
            xo_ref[:, cols] = x_ref[:, cols] + _dot(_bf(pooled[gi]), w_ref[gi]) * s_ref[:, cols]

    return pl.pallas_call(
        body, name=name, grid=(T // tm,),
        in_specs=[pl.BlockSpec((tm, D), lambda i: (i, 0)),
                  pl.BlockSpec((POOL_HALO, D), lambda i: (jnp.maximum(i * hb - 1, 0), 0)),
                  pl.BlockSpec((None, 1, D), lambda i: (layer, 0, 0)),
                  pl.BlockSpec(pool_w.shape, lambda i: (0, 0, 0)),
                  pl.BlockSpec((1, D), lambda i: (0, 0))],
        out_specs=pl.BlockSpec((tm, D), lambda i: (i, 0)),
        out_shape=jax.ShapeDtypeStruct((T, D), F32),
        compiler_params=_params(("parallel",)),
    )(x, x, nw, pool_w, pool_scale)


def _pool_bwd(dxo, x, nw, pool_w, pool_scale, layer, name):
    T, D = x.shape
    tm = _tile(T, 256)
    hb = tm // POOL_HALO
    nt = T // tm
    ng = len(POOL_WINDOWS)

    def body(dxo_ref, dxn_ref, x_ref, xp_ref, n_ref, w_ref, s_ref, dx_ref, dw_ref, ds_ref, dn_ref):
        i = pl.program_id(0)

        @pl.when(i == 0)
        def _():
            dw_ref[...] = jnp.zeros_like(dw_ref)
            ds_ref[...] = jnp.zeros_like(ds_ref)
            dn_ref[...] = jnp.zeros_like(dn_ref)

        prev = jnp.where(i == 0, 0.0, xp_ref[...])
        ext = jnp.concatenate([prev, x_ref[...]], axis=0)
        xhat_ext, r_ext = _rms_stats(ext)
        nv = n_ref[...]
        pooled = _pooled_groups(xhat_ext * nv, i * tm, tm)
        dxo = dxo_ref[...]
        scale = s_ref[...]
        dout_ext = jnp.concatenate([dxo, jnp.where(i == nt - 1, 0.0, dxn_ref[...])], axis=0) * scale
        t_ext = i * tm + lax.broadcasted_iota(jnp.int32, (tm + POOL_HALO, 1), 0)
        counts = _pool_counts(t_ext)
        dh_cols, ds_cols = [], []
        for gi, win in enumerate(POOL_WINDOWS):
            cols = slice(gi * POOL_DIM, (gi + 1) * POOL_DIM)
            wg = w_ref[gi]
            pb = _bf(pooled[gi])
            doutb = _bf(dout_ext[:, cols])
            dpooled = _dot_nt(doutb, wg)
            ahead = _window_sums(dpooled / counts[gi], -1)[win]
            dh_cols.append(ahead[:tm, :] - dpooled[:tm, :])
            dw_ref[gi] += _dot_tn(pb, doutb[:tm, :])
            ds_cols.append(jnp.sum(dxo[:, cols] * _dot(pb, wg), axis=0, keepdims=True))
        dh = jnp.concatenate(dh_cols, axis=1)
        xhat, r = xhat_ext[POOL_HALO:, :], r_ext[POOL_HALO:, :]
        dx_ref[...] = dxo + _rms_bwd(dh, xhat, r, nv)
        dn_ref[...] += jnp.sum(dh * xhat, axis=0, keepdims=True)
        ds_ref[...] += jnp.concatenate(ds_cols, axis=1)

    last_halo = T // POOL_HALO - 1
    return pl.pallas_call(
        body, name=name, grid=(nt,),
        in_specs=[pl.BlockSpec((tm, D), lambda i: (i, 0)),
                  pl.BlockSpec((POOL_HALO, D), lambda i: (jnp.minimum((i + 1) * hb, last_halo), 0)),
                  pl.BlockSpec((tm, D), lambda i: (i, 0)),
                  pl.BlockSpec((POOL_HALO, D), lambda i: (jnp.maximum(i * hb - 1, 0), 0)),
                  pl.BlockSpec((None, 1, D), lambda i: (layer, 0, 0)),
                  pl.BlockSpec(pool_w.shape, lambda i: (0, 0, 0)),
                  pl.BlockSpec((1, D), lambda i: (0, 0))],
        out_specs=[pl.BlockSpec((tm, D), lambda i: (i, 0)),
                   pl.BlockSpec((ng, POOL_DIM, POOL_DIM), lambda i: (0, 0, 0)),
                   pl.BlockSpec((1, D), lambda i: (0, 0)),
                   pl.BlockSpec((1, D), lambda i: (0, 0))],
        out_shape=[jax.ShapeDtypeStruct((T, D), F32), jax.ShapeDtypeStruct((ng, POOL_DIM, POOL_DIM), F32),
                   jax.ShapeDtypeStruct((1, D), F32), jax.ShapeDtypeStruct((1, D), F32)],
        compiler_params=_params(("arbitrary",)),
    )(dxo, dxo, x, x, nw, pool_w, pool_scale)


def _loss_head(x, target, fn, name):
    T, D = x.shape
    tm = _tile(T, 512)

    def body(x_ref, t_ref, n_ref, loss_ref, dx_ref, dn_ref):
        @pl.when(pl.program_id(0) == 0)
        def _():
            loss_ref[...] = jnp.zeros_like(loss_ref)
            dn_ref[...] = jnp.zeros_like(dn_ref)

        xhat, r = _rms_stats(x_ref[...])
        nv = n_ref[...]
        err = xhat * nv - t_ref[...]
        part = jnp.sum(jnp.sum(err * err, axis=1, keepdims=True), axis=0, keepdims=True)
        loss_ref[...] += 0.5 * part / D
        dy = err / D
        dx_ref[...] = _rms_bwd(dy, xhat, r, nv)
        dn_ref[...] += jnp.sum(dy * xhat, axis=0, keepdims=True)

    row = pl.BlockSpec((tm, D), lambda i: (i, 0))
    return pl.pallas_call(
        body, name=name, grid=(T // tm,),
        in_specs=[row, row, pl.BlockSpec((1, D), lambda i: (0, 0))],
        out_specs=[pl.BlockSpec((1, 1), lambda i: (0, 0)), row, pl.BlockSpec((1, D), lambda i: (0, 0))],
        out_shape=[jax.ShapeDtypeStruct((1, 1), F32), jax.ShapeDtypeStruct((T, D), F32),
                   jax.ShapeDtypeStruct((1, D), F32)],
        compiler_params=_params(("arbitrary",)),
    )(x, target, fn)


def _adamw(w, g, m, v, name):
    R, C = w.shape
    br = R
    for cand in (512, 256, 128, 64, 32, 16, 8):
        if R % cand == 0 and cand * C * 4 <= 2 * 1024 * 1024:
            br = cand
            break

    def body(w_ref, g_ref, m_ref, v_ref, d_ref, mo_ref, vo_ref):
        gv = g_ref[...]
        m_new = ADAM_B1 * m_ref[...] + (1.0 - ADAM_B1) * gv
        v_new = ADAM_B2 * v_ref[...] + (1.0 - ADAM_B2) * (gv * gv)
        m_hat = m_new / (1.0 - ADAM_B1 ** ADAM_STEP)
        v_hat = v_new / (1.0 - ADAM_B2 ** ADAM_STEP)
        d_ref[...] = -ADAM_LR * (m_hat / (jnp.sqrt(v_hat) + ADAM_EPS) + ADAM_WD * w_ref[...])
        mo_ref[...] = m_new
        vo_ref[...] = v_new

    blk = pl.BlockSpec((br, C), lambda i: (i, 0))
    return pl.pallas_call(
        body, name=name, grid=(R // br,), in_specs=[blk] * 4, out_specs=[blk] * 3,
        out_shape=[jax.ShapeDtypeStruct((R, C), F32)] * 3,
        compiler_params=_params(("parallel",)),
    )(w, g, m, v)


def _mesh_pos():
    return lax.axis_index("x"), lax.axis_index("y"), lax.axis_index("c")


def _other_chips(x, y):
    return [(1 - x, y), (x, 1 - y), (1 - x, 1 - y)]


def _half_of(ref, shape, h):
    if shape[0] == 2:
        return ref.at[h]
    size = shape[1] // 2
    return ref.at[:, pl.ds(h * size, size)]


def _all_gather_chips(shards, split, name):
    n = len(shards)

    def body(*refs):
        ins, outs = refs[:n], refs[n:2 * n]
        send_sems, recv_sems, fwd_send_sems, fwd_recv_sems = refs[2 * n:]
        x, y, c = _mesh_pos()
        me = 2 * x + y
        sibling = (x, y, 1 - c)
        chips = _other_chips(x, y)

        def piece(a, ref, h):
            return _half_of(ref, shards[a].shape, h) if split[a] else ref

        sent, forwards = [], []
        for a in range(n):
            for k, (px, py) in enumerate(chips):
                cp = pltpu.make_async_remote_copy(piece(a, ins[a], c), piece(a, outs[a].at[me], c),
                                                  send_sems.at[a, k], recv_sems.at[a, k],
                                                  device_id=(px, py, c), device_id_type=MESH)
                cp.start()
                sent.append(cp)
        for a in range(n):
            for k, (px, py) in enumerate(chips):
                landed = piece(a, outs[a].at[2 * px + py], c)
                pltpu.make_async_remote_copy(landed, landed, send_sems.at[a, k], recv_sems.at[a, k],
                                             device_id=(px, py, c), device_id_type=MESH).wait_recv()
                if split[a]:
                    fwd = pltpu.make_async_remote_copy(landed, landed, fwd_send_sems.at[a, k], fwd_recv_sems.at[a, k],
                                                       device_id=sibling, device_id_type=MESH)
                    fwd.start()
                    forwards.append(fwd)
        for a in range(n):
            if split[a]:
                for k, (px, py) in enumerate(chips):
                    other = piece(a, outs[a].at[2 * px + py], 1 - c)
                    pltpu.make_async_remote_copy(other, other, fwd_send_sems.at[a, k], fwd_recv_sems.at[a, k],
                                                 device_id=sibling, device_id_type=MESH).wait_recv()
        for cp in sent + forwards:
            cp.wait_send()

    sems = pltpu.SemaphoreType.DMA((n, 3))
    gathered = pl.pallas_call(
        body, name=name, in_specs=[ANY] * n, out_specs=[ANY] * n,
        out_shape=[jax.ShapeDtypeStruct((N_CHIPS,) + s.shape, s.dtype) for s in shards],
        scratch_shapes=[sems, sems, sems, sems],
        compiler_params=pltpu.CompilerParams(has_side_effects=True),
    )(*shards)
    x, y, _ = _mesh_pos()
    return [lax.dynamic_update_index_in_dim(g, s, 2 * x + y, 0) for g, s in zip(gathered, shards)]


def _ffn_weight_grads(hb, dg, du, a, dyb, tag):
    dwg = _matmul_tn(hb, dg, D_MODEL, FF_CHUNK, f"{tag}_dw_gate", stack_n=True)
    dwu = _matmul_tn(hb, du, D_MODEL, FF_CHUNK, f"{tag}_dw_up", stack_n=True)
    dwo = _matmul_tn(a, dyb, FF_CHUNK, D_MODEL, f"{tag}_dw_out")
    return jnp.concatenate([dwg, dwu], axis=0), dwo.reshape(N_CHIPS, D_FF // N_CHIPS, D_MODEL)


def _local_step(x, target, w):
    g = {}
    acts = []
    ffn_w = {1: (w["n1"], w["win1"], w["wout1"]), 2: (w["n2"], w["win2"], w["wout2"])}

    def ffn(xin, which, layer):
        nw, win, wout = ffn_w[which]
        xo, gv, uv, hb = _ffn_fwd(xin, nw, win, wout, layer, f"ffn{which}_l{layer}_fwd")
        acts.append((xin, gv, uv, hb))
        return xo

    x1 = ffn(x, 1, 0)
    hb_mix = _rms_fwd_call(x1, w["nmix"], 0, "ab_norm_fwd")
    proj = _matmul(hb_mix, w["wp"], "ab_in_proj")
    qkv = _conv_fwd(proj, w["conv_w"], "dn_conv_fwd")
    dn_u, dn_w, dn_p, dn_qd, dn_kt, dn_gl, dn_inv = _dn_prep(qkv, proj, w["hp"], "dn_prep")
    o, dn_vn, sall = _dn_scan(dn_u, dn_w, dn_p, dn_qd, dn_kt, dn_gl, "dn_scan")
    mix = _mix_fwd(o, proj, w["dn_norm"], w["sg_norm"], w["sg_w"], w["sg_bt"], "ab_gate_fwd")
    x2 = _matmul(mix, w["wo"], "ab_out_proj", res=x1)
    x3 = ffn(x2, 2, 0)
    x4 = ffn(x3, 1, 1)
    x5 = _pool_fwd(x4, w["nmix"], w["pool_w"], w["pool_scale"], 1, "pool_fwd")
    x6 = ffn(x5, 2, 1)
    loss, dx, g["fn"] = _loss_head(x6, target, w["fn"], "loss_head")

    dn = {1: [None, None], 2: [None, None]}
    dwin = {1: [None, None], 2: [None, None]}
    dwout = {1: [None, None], 2: [None, None]}

    def ffn_back(dxo, which, layer, saved):
        nw, win, wout = ffn_w[which]
        xin, gv, uv, hb = saved
        tag = f"ffn{which}_l{layer}"
        dxi, dg, du, a, dyb, dnw = _ffn_bwd(dxo, xin, nw, gv, uv, win, wout, layer, f"{tag}_bwd")
        dn[which][layer] = dnw
        dwin[which][layer], dwout[which][layer] = _ffn_weight_grads(hb, dg, du, a, dyb, tag)
        return dxi

    dx = ffn_back(dx, 2, 1, acts[3])
    dx, g["pool_w"], g["pool_scale"], dnmix1 = _pool_bwd(dx, x4, w["nmix"], w["pool_w"], w["pool_scale"], 1, "pool_bwd")
    dx = ffn_back(dx, 1, 1, acts[2])
    dx2 = ffn_back(dx, 2, 0, acts[1])
    dmix = _matmul(dx2, w["wo"], "ab_out_proj_bwd", trans_b=True)
    g["wo"] = _matmul_tn(mix, dx2, D_MODEL, D_MODEL, "ab_out_proj_dw")
    do, dzuv, g["dn_norm"], g["sg_norm"], g["sg_w"], g["sg_bt"] = _mix_bwd(
        dmix, o, proj, w["dn_norm"], w["sg_norm"], w["sg_w"], w["sg_bt"], "ab_gate_bwd")
    dvn, dkt, dgl = _dn_scan_bwd(dn_w, dn_p, dn_qd, dn_kt, dn_gl, dn_vn, sall, do, "dn_scan_bwd")
    dqkv_act, dba, g["hp"] = _dn_prep_bwd(qkv, proj, w["hp"], sall, dn_vn, do, dvn, dkt, dgl, dn_inv, dn_u, dn_w,
                                          "dn_prep_bwd")
    dqkv, g["conv_w"] = _conv_bwd(dqkv_act, proj, w["conv_w"], "dn_conv_bwd")
    dproj = jnp.concatenate([dqkv, dzuv, dba.astype(BF16)], axis=1)
    dh = _matmul(dproj, w["wp"], "ab_in_proj_bwd", trans_b=True)
    g["wp"] = _matmul_tn(hb_mix, dproj, D_MODEL, 640, "ab_in_proj_dw")
    dx1, dnmix0 = _rms_bwd_call(dh, x1, w["nmix"], dx2, 0, "ab_norm_bwd")
    dx0 = ffn_back(dx1, 1, 0, acts[0])

    g["n1"] = jnp.concatenate(dn[1], axis=0)
    g["n2"] = jnp.concatenate(dn[2], axis=0)
    g["nmix"] = jnp.concatenate([dnmix0, dnmix1], axis=0)
    for which in (1, 2):
        g[f"win{which}"] = dwin[which]
        g[f"wout{which}"] = dwout[which]
    return loss, dx0, g


SHARDED = ("ffn1_w_in", "ffn1_w_out", "ffn2_w_in", "ffn2_w_out", "ab_w_in", "ab_w_out", "pool_w", "dn_conv_w", "pool_scale")
REPLICATED = ("ffn_norm1", "mix_norm", "ffn_norm2", "dn_a_log", "dn_dt_bias", "dn_out_norm", "sg_norm", "sg_w", "sg_b", "final_norm")
QKVZ = 4 * DN_HEADS * DN_DIM
N_GATES = 2 * DN_HEADS
IN_PROJ = QKVZ + N_GATES + 2 * SG_GROUPS * SG_DIM


def _kernel_layouts(gathered, rep):
    per_layer = lambda a: a.reshape(a.shape[0], 1, D_MODEL)
    w = {"n1": per_layer(rep["ffn_norm1"]), "nmix": per_layer(rep["mix_norm"]), "n2": per_layer(rep["ffn_norm2"]),
         "win1": gathered["ffn1_w_in"], "wout1": gathered["ffn1_w_out"],
         "win2": gathered["ffn2_w_in"], "wout2": gathered["ffn2_w_out"]}
    ab_in = jnp.transpose(gathered["ab_w_in"][:, 0], (1, 0, 2)).reshape(D_MODEL, IN_PROJ)
    w["wp"] = jnp.concatenate([ab_in[:, :QKVZ], ab_in[:, QKVZ + N_GATES:], ab_in[:, QKVZ:QKVZ + N_GATES],
                               jnp.zeros((D_MODEL, PROJ_W - IN_PROJ), ab_in.dtype)], axis=1)
    w["conv_w"] = jnp.transpose(gathered["dn_conv_w"][:, 0], (1, 0, 2)).reshape(DN_CONV, 3 * DN_HEADS * DN_DIM)
    hp = jnp.zeros((8, 128), F32)
    hp = hp.at[0, DN_HEADS:N_GATES].set(rep["dn_a_log"][0]).at[1, DN_HEADS:N_GATES].set(rep["dn_dt_bias"][0])
    w["hp"] = hp
    w["dn_norm"] = rep["dn_out_norm"]
    w["sg_norm"] = rep["sg_norm"][0]
    w["sg_w"] = rep["sg_w"][0]
    w["sg_bt"] = jnp.zeros((SG_CHUNK, 128), F32).at[:, :SG_GROUPS].set(rep["sg_b"][0].T)
    w["wo"] = gathered["ab_w_out"][:, 0].reshape(D_MODEL, D_MODEL)
    w["pool_w"] = jnp.transpose(gathered["pool_w"][:, 0], (1, 0, 2, 3)).reshape(len(POOL_WINDOWS), POOL_DIM, POOL_DIM)
    w["pool_scale"] = gathered["pool_scale"].reshape(1, D_MODEL)
    w["fn"] = rep["final_norm"].reshape(1, D_MODEL)
    return w


def _grads_by_chip(g):
    wp = g["wp"]
    ab_in = jnp.concatenate([wp[:, :QKVZ], wp[:, IN_PROJ - N_GATES:IN_PROJ], wp[:, QKVZ:IN_PROJ - N_GATES]], axis=1)
    nw = len(POOL_WINDOWS)
    one_layer = {
        "ab_w_in": jnp.transpose(ab_in.reshape(D_MODEL, N_CHIPS, IN_PROJ // N_CHIPS), (1, 0, 2)),
        "ab_w_out": g["wo"].reshape(N_CHIPS, D_MODEL // N_CHIPS, D_MODEL),
        "pool_w": jnp.transpose(g["pool_w"].reshape(nw, N_CHIPS, POOL_DIM // N_CHIPS, POOL_DIM), (1, 0, 2, 3)),
        "dn_conv_w": jnp.transpose(g["conv_w"].reshape(DN_CONV, N_CHIPS, -1), (1, 0, 2)),
        "pool_scale": g["pool_scale"].reshape(N_CHIPS, D_MODEL // N_CHIPS),
    }
    sharded = {}
    for n, per_layer in (("ffn1_w_in", g["win1"]), ("ffn1_w_out", g["wout1"]), ("ffn2_w_in", g["win2"]), ("ffn2_w_out", g["wout2"])):
        sharded[n] = [a.reshape(N_CHIPS, -1) for a in per_layer]
    for n, a in one_layer.items():
        halves = a.reshape(N_CHIPS, 2, -1)
        sharded[n] = [halves[:, 0], halves[:, 1]]
    rep = {
        "ffn_norm1": g["n1"], "mix_norm": g["nmix"], "ffn_norm2": g["n2"],
        "dn_a_log": g["hp"][0:1, DN_HEADS:N_GATES], "dn_dt_bias": g["hp"][1:2, DN_HEADS:N_GATES],
        "dn_out_norm": g["dn_norm"], "sg_norm": g["sg_norm"][None], "sg_w": g["sg_w"][None],
        "sg_b": g["sg_bt"][:, :SG_GROUPS].T[None], "final_norm": g["fn"].reshape(D_MODEL),
    }
    return sharded, rep


def _piece_rows(n_elems):
    rows = -(-n_elems // PACK_LANES)
    return -(-rows // 8) * 8


def _half_sizes(shapes):
    sizes = []
    for n in SHARDED:
        size = 1
        for d in shapes[n]:
            size *= d
        sizes.append(size // 2)
    return sizes


def _pack_rows(shapes):
    rows = sum(_piece_rows(s) for s in _half_sizes(shapes))
    return -(-rows // PACK_ROW_BLOCK) * PACK_ROW_BLOCK


def _pack_by_half(sharded, shapes):
    rows = _pack_rows(shapes)
    pieces = []
    for half in range(2):
        used = 0
        for n in SHARDED:
            flat = sharded[n][half]
            pr = _piece_rows(flat.shape[1])
            flat = jnp.pad(flat, ((0, 0), (0, pr * PACK_LANES - flat.shape[1])))
            pieces.append(flat.reshape(N_CHIPS, pr, PACK_LANES))
            used += pr
        pieces.append(jnp.zeros((N_CHIPS, rows - used, PACK_LANES), F32))
    return jnp.concatenate(pieces, axis=1).reshape(N_CHIPS, 2, rows, PACK_LANES)


def _unpack_halves(mine, other, core, shapes):
    out, off = {}, 0
    for n, half in zip(SHARDED, _half_sizes(shapes)):
        pr = _piece_rows(half)
        a, b = mine[off:off + pr], other[off:off + pr]
        both = jnp.stack([jnp.where(core == 0, a, b), jnp.where(core == 0, b, a)])
        out[n] = both.reshape(2, -1)[:, :half].reshape(shapes[n])
        off += pr
    return out


def _swap_with_sibling(pack, name):
    nchip, _, rows, lanes = pack.shape

    def body(pack_ref, recv_ref, send_sem, recv_sem):
        x, y, c = _mesh_pos()
        cp = pltpu.make_async_remote_copy(pack_ref.at[:, 1 - c], recv_ref, send_sem, recv_sem,
                                          device_id=(x, y, 1 - c), device_id_type=MESH)
        cp.start()
        cp.wait()

    return pl.pallas_call(
        body, name=name, in_specs=[ANY], out_specs=ANY,
        out_shape=jax.ShapeDtypeStruct((nchip, rows, lanes), pack.dtype),
        scratch_shapes=[pltpu.SemaphoreType.DMA, pltpu.SemaphoreType.DMA],
        compiler_params=pltpu.CompilerParams(has_side_effects=True),
    )(pack)


def _add_pair(pack, recv, core, name):
    nchip, _, rows, lanes = pack.shape

    def body(c_ref, a_ref, b_ref, o32_ref, o16_ref):
        s = a_ref[...] + b_ref[...]
        o32_ref[...] = s
        o16_ref[...] = _bf(s)

    blk = pl.BlockSpec((None, PACK_ROW_BLOCK, lanes), lambda p, i, c: (p, i, 0))
    return pl.pallas_call(
        body, name=name,
        grid_spec=pltpu.PrefetchScalarGridSpec(
            num_scalar_prefetch=1, grid=(nchip, rows // PACK_ROW_BLOCK),
            in_specs=[pl.BlockSpec((None, None, PACK_ROW_BLOCK, lanes), lambda p, i, c: (p, c[0], i, 0)), blk],
            out_specs=[blk, blk]),
        out_shape=[jax.ShapeDtypeStruct((nchip, rows, lanes), F32), jax.ShapeDtypeStruct((nchip, rows, lanes), BF16)],
        compiler_params=_params(("parallel", "parallel")),
    )(core, pack, recv)


def _scatter_to_chips(part16, name):
    nchip, rows, lanes = part16.shape

    def body(src_ref, recv_ref, send_sems, recv_sems):
        x, y, c = _mesh_pos()
        copies = []
        for k, (px, py) in enumerate(_other_chips(x, y)):
            cp = pltpu.make_async_remote_copy(src_ref.at[2 * px + py], recv_ref.at[k], send_sems.at[k], recv_sems.at[k],
                                              device_id=(px, py, c), device_id_type=MESH)
            cp.start()
            copies.append(cp)
        for cp in copies:
            cp.wait()

    return pl.pallas_call(
        body, name=name, in_specs=[ANY], out_specs=ANY,
        out_shape=jax.ShapeDtypeStruct((nchip - 1, rows, lanes), part16.dtype),
        scratch_shapes=[pltpu.SemaphoreType.DMA((nchip - 1,)), pltpu.SemaphoreType.DMA((nchip - 1,))],
        compiler_params=pltpu.CompilerParams(has_side_effects=True),
    )(part16)


def _sum_chips(part32, recv16, chip, name):
    nchip, rows, lanes = part32.shape

    def body(p_ref, own_ref, r_ref, o_ref):
        s = own_ref[...]
        for k in range(nchip - 1):
            s = s + r_ref[k].astype(F32)
        o_ref[...] = s

    return pl.pallas_call(
        body, name=name,
        grid_spec=pltpu.PrefetchScalarGridSpec(
            num_scalar_prefetch=1, grid=(rows // PACK_ROW_BLOCK,),
            in_specs=[pl.BlockSpec((None, PACK_ROW_BLOCK, lanes), lambda i, p: (p[0], i, 0)),
                      pl.BlockSpec((nchip - 1, PACK_ROW_BLOCK, lanes), lambda i, p: (0, i, 0))],
            out_specs=pl.BlockSpec((PACK_ROW_BLOCK, lanes), lambda i, p: (i, 0))),
        out_shape=jax.ShapeDtypeStruct((rows, lanes), F32),
        compiler_params=_params(("parallel",)),
    )(chip, part32, recv16)


def _share_with_sibling(half, name):
    rows, lanes = half.shape

    def body(h_ref, other_ref, send_sem, recv_sem):
        x, y, c = _mesh_pos()
        cp = pltpu.make_async_remote_copy(h_ref, other_ref, send_sem, recv_sem,
                                          device_id=(x, y, 1 - c), device_id_type=MESH)
        cp.start()
        cp.wait()

    return pl.pallas_call(
        body, name=name, in_specs=[ANY], out_specs=ANY,
        out_shape=jax.ShapeDtypeStruct((rows, lanes), half.dtype),
        scratch_shapes=[pltpu.SemaphoreType.DMA, pltpu.SemaphoreType.DMA],
        compiler_params=pltpu.CompilerParams(has_side_effects=True),
    )(half)


def _reduce_sharded(sharded, shapes):
    x, y, c = _mesh_pos()
    core = jnp.reshape(c, (1,)).astype(jnp.int32)
    chip = jnp.reshape(2 * x + y, (1,)).astype(jnp.int32)
    pack = _pack_by_half(sharded, shapes)
    recv = _swap_with_sibling(pack, "grad_pair_swap")
    part32, part16 = _add_pair(pack, recv, core, "grad_pair_add")
    recv16 = _scatter_to_chips(part16, "grad_chip_scatter")
    half = _sum_chips(part32, recv16, chip, "grad_chip_sum")
    other = _share_with_sibling(half, "grad_pair_share")
    return _unpack_halves(half, other, c, shapes)


def _pack_small(vals):
    parts = []
    for n in REPLICATED:
        flat = vals[n].reshape(-1)
        rows = -(-flat.shape[0] // 128)
        rows = -(-rows // 8) * 8
        parts.append(jnp.pad(flat, (0, rows * 128 - flat.shape[0])).reshape(rows, 128))
    return jnp.concatenate(parts, axis=0)


def _unpack_small(pack, like):
    out, off = {}, 0
    for n in REPLICATED:
        size = like[n].size
        rows = -(-size // 128)
        rows = -(-rows // 8) * 8
        out[n] = pack[off:off + rows].reshape(-1)[:size].reshape(like[n].shape)
        off += rows
    return out


def _all_to_all_small(pack, name):
    rows, lanes = pack.shape
    flips = [(dx, dy, dc) for dx in (0, 1) for dy in (0, 1) for dc in (0, 1)][1:]

    def body(src_ref, out_ref, send_sems, recv_sems, local_sem):
        x, y, c = _mesh_pos()
        me = 4 * x + 2 * y + c
        loc = pltpu.make_async_copy(src_ref, out_ref.at[me], local_sem)
        loc.start()
        copies = []
        for k, (dx, dy, dc) in enumerate(flips):
            peer = (x ^ dx, y ^ dy, c ^ dc)
            cp = pltpu.make_async_remote_copy(src_ref, out_ref.at[me], send_sems.at[k], recv_sems.at[k],
                                              device_id=peer, device_id_type=MESH)
            cp.start()
            copies.append(cp)
        for k, (dx, dy, dc) in enumerate(flips):
            peer = (x ^ dx, y ^ dy, c ^ dc)
            pltpu.make_async_remote_copy(src_ref, out_ref.at[4 * peer[0] + 2 * peer[1] + peer[2]], send_sems.at[k],
                                         recv_sems.at[k], device_id=peer, device_id_type=MESH).wait_recv()
        for cp in copies:
            cp.wait_send()
        loc.wait()

    return pl.pallas_call(
        body, name=name, in_specs=[ANY], out_specs=ANY,
        out_shape=jax.ShapeDtypeStruct((8, rows, lanes), pack.dtype),
        scratch_shapes=[pltpu.SemaphoreType.DMA((7,)), pltpu.SemaphoreType.DMA((7,)), pltpu.SemaphoreType.DMA],
        compiler_params=pltpu.CompilerParams(has_side_effects=True),
    )(pack)


def _sum_devices(stack, name):
    ndev, rows, lanes = stack.shape

    def body(s_ref, o_ref):
        s = s_ref[0]
        for d in range(1, ndev):
            s = s + s_ref[d]
        o_ref[...] = s

    return pl.pallas_call(
        body, name=name, grid=(1,),
        in_specs=[pl.BlockSpec((ndev, rows, lanes), lambda i: (0, 0, 0))],
        out_specs=pl.BlockSpec((rows, lanes), lambda i: (0, 0)),
        out_shape=jax.ShapeDtypeStruct((rows, lanes), F32),
    )(stack)


WEIGHT_ORDER = ("ffn_norm1", "ffn1_w_in", "ffn1_w_out", "mix_norm", "ffn_norm2", "ffn2_w_in", "ffn2_w_out", "ab_w_in",
                "dn_conv_w", "dn_a_log", "dn_dt_bias", "dn_out_norm", "sg_norm", "sg_w", "sg_b", "ab_w_out", "pool_w",
                "pool_scale", "final_norm")
MATRICES = ("ffn1_w_in", "ffn1_w_out", "ffn2_w_in", "ffn2_w_out", "ab_w_in", "ab_w_out", "pool_w")


def _as_2d(a):
    return a.reshape(-1, a.shape[-1])


def kernel(x, ffn_norm1, ffn1_w_in, ffn1_w_out, mix_norm, ffn_norm2, ffn2_w_in, ffn2_w_out, ab_w_in, dn_conv_w, dn_a_log, dn_dt_bias, dn_out_norm, sg_norm, sg_w, sg_b, ab_w_out, pool_w, pool_scale, final_norm, loss_target, m_ffn_norm1, m_ffn1_w_in, m_ffn1_w_out, m_mix_norm, m_ffn_norm2, m_ffn2_w_in, m_ffn2_w_out, m_ab_w_in, m_dn_conv_w, m_dn_a_log, m_dn_dt_bias, m_dn_out_norm, m_sg_norm, m_sg_w, m_sg_b, m_ab_w_out, m_pool_w, m_pool_scale, m_final_norm, v_ffn_norm1, v_ffn1_w_in, v_ffn1_w_out, v_mix_norm, v_ffn_norm2, v_ffn2_w_in, v_ffn2_w_out, v_ab_w_in, v_dn_conv_w, v_dn_a_log, v_dn_dt_bias, v_dn_out_norm, v_sg_norm, v_sg_w, v_sg_b, v_ab_w_out, v_pool_w, v_pool_scale, v_final_norm):
    given = dict(locals())
    wts = {n: given[n] for n in WEIGHT_ORDER}
    mom_m = {n: given["m_" + n] for n in WEIGHT_ORDER}
    mom_v = {n: given["v_" + n] for n in WEIGHT_ORDER}

    shards = [wts[n].astype(BF16) if n in MATRICES else wts[n] for n in SHARDED]
    gathered = dict(zip(SHARDED, _all_gather_chips(shards, [n in MATRICES for n in SHARDED], "weight_all_gather")))
    rep = {n: wts[n] for n in REPLICATED}
    w = _kernel_layouts(gathered, rep)

    loss, dx, g = _local_step(x[0], loss_target[0], w)
    g_sharded, g_rep = _grads_by_chip(g)

    grads = _reduce_sharded(g_sharded, {n: wts[n].shape for n in SHARDED})
    small = _sum_devices(_all_to_all_small(_pack_small(g_rep), "grad_small_exchange"), "grad_small_sum")
    grads.update(_unpack_small(small, rep))

    delta, new_m, new_v = {}, {}, {}
    for n in SHARDED:
        d, m1, v1 = _adamw(_as_2d(wts[n]), _as_2d(grads[n]), _as_2d(mom_m[n]), _as_2d(mom_v[n]), f"adamw_{n}")
        delta[n], new_m[n], new_v[n] = (t.reshape(wts[n].shape) for t in (d, m1, v1))
    d, m1, v1 = _adamw(_pack_small(rep), small, _pack_small({n: mom_m[n] for n in REPLICATED}),
                       _pack_small({n: mom_v[n] for n in REPLICATED}), "adamw_replicated")
    for tgt, packed in ((delta, d), (new_m, m1), (new_v, v1)):
        tgt.update(_unpack_small(packed, rep))

    total = lax.psum(loss[0, 0], ("x", "y", "c"))
    outs = [total, dx[None]]
    for group in (grads, delta, new_m, new_v):
        outs.extend(group[n] for n in WEIGHT_ORDER)
    return tuple(outs)
```

```python
import functools

import jax
import jax.numpy as jnp
from jax import lax
from jax.experimental import pallas as pl
from jax.experimental.pallas import tpu as pltpu

F32, BF16 = jnp.float32, jnp.bfloat16
NORM_EPS = 1e-6
D_MODEL = 1024
D_FF = 2816
N_CHIPS = 4
FF_CHUNK = 2 * D_FF // N_CHIPS
DN_HEADS, DN_DIM, DN_CHUNK, DN_CONV = 4, 128, 64, 4
DN_BLOCK = 2 * DN_CHUNK
DN_PREP_CHUNKS = 4
DN_SCAN_CHUNKS = 8
SG_GROUPS, SG_DIM, SG_CHUNK = 4, 128, 128
POOL_WINDOWS = (2, 4, 8, 16)
POOL_DIM = 256
POOL_HALO = 16
CONV_HALO = 8
PROJ_W = 3200
BA_BLOCK = 3072 // 128
ADAM_LR, ADAM_B1, ADAM_B2, ADAM_EPS, ADAM_WD, ADAM_STEP = 0.001, 0.9, 0.999, 1e-08, 0.01, 10
VMEM_BIG = 52 * 1024 * 1024
FFN_FWD_ROWS = 512
FFN_BWD_ROWS = 256
PACK_LANES = 1024
PACK_ROW_BLOCK = 256
MESH = pl.DeviceIdType.MESH
HI = lax.Precision.HIGHEST
ANY = pl.BlockSpec(memory_space=pl.ANY)


def _params(sem=None, vmem=None):
    return pltpu.CompilerParams(dimension_semantics=sem, vmem_limit_bytes=vmem)


def _dot(a, b):
    return jnp.dot(a, b, preferred_element_type=F32)


def _dot_nt(a, b):
    return lax.dot_general(a, b, (((1,), (1,)), ((), ())), preferred_element_type=F32)


def _dot_tn(a, b):
    return lax.dot_general(a, b, (((0,), (0,)), ((), ())), preferred_element_type=F32)


def _dot_hi(a, b):
    return jnp.dot(a, b, preferred_element_type=F32, precision=HI)


def _dot_mid(a, b):
    return jnp.dot(a, b, preferred_element_type=F32, precision=lax.Precision.HIGH)


def _bf(a):
    return a.astype(BF16)


def _rms_stats(x):
    r = lax.rsqrt(jnp.mean(x * x, axis=-1, keepdims=True) + NORM_EPS)
    return x * r, r


def _rms_bwd(dh, xhat, r, w):
    dhn = dh * w
    return r * (dhn - xhat * jnp.mean(dhn * xhat, axis=-1, keepdims=True))


def _sigmoid(x):
    return jax.nn.sigmoid(x)


def _silu_grad(x, s):
    return s * (1.0 + x * (1.0 - s))


def _gelu(x):
    return 0.5 * x * (1.0 + lax.erf(x * 0.7071067811865476))


def _gelu_grad(x):
    return 0.5 * (1.0 + lax.erf(x * 0.7071067811865476)) + x * jnp.exp(-0.5 * x * x) * 0.3989422804014327


def _softplus(x):
    return jnp.maximum(x, 0.0) + jnp.log(1.0 + jnp.exp(-jnp.abs(x)))


def _tile(n, pref):
    t = min(n, pref)
    assert n % t == 0, (n, t)
    return t


def _ffn_weight_specs():
    once = pl.Buffered(1)
    return [pl.BlockSpec((N_CHIPS, D_MODEL, FF_CHUNK), lambda i: (0, 0, 0), pipeline_mode=once),
            pl.BlockSpec((N_CHIPS, D_FF // N_CHIPS, D_MODEL), lambda i: (0, 0, 0), pipeline_mode=once)]


def _ffn_fwd(x, nw, win, wout, layer, name):
    T, D = x.shape
    tm = _tile(T, FFN_FWD_ROWS)
    nj = N_CHIPS // 2

    def body(x_ref, n_ref, win_ref, wo_ref, xo_ref, g_ref, u_ref, hb_ref):
        xv = x_ref[...]
        xhat, _ = _rms_stats(xv)
        h = _bf(xhat * n_ref[...])
        hb_ref[...] = h
        acc = None
        for j in range(nj):
            cols = slice(j * FF_CHUNK, (j + 1) * FF_CHUNK)
            g = _dot(h, win_ref[j])
            u = _dot(h, win_ref[nj + j])
            g_ref[:, cols] = _bf(g)
            u_ref[:, cols] = _bf(u)
            part = _dot(_bf(g * _sigmoid(g) * u), wo_ref[2 * j:2 * j + 2].reshape(FF_CHUNK, D))
            acc = part if acc is None else acc + part
        xo_ref[...] = xv + 0.5 * acc

    row = pl.BlockSpec((tm, D), lambda i: (i, 0))
    wide = pl.BlockSpec((tm, D_FF), lambda i: (i, 0))
    return pl.pallas_call(
        body, name=name, grid=(T // tm,),
        in_specs=[row, pl.BlockSpec((None, 1, D), lambda i: (layer, 0, 0))] + _ffn_weight_specs(),
        out_specs=[row, wide, wide, row],
        out_shape=[jax.ShapeDtypeStruct((T, D), F32), jax.ShapeDtypeStruct((T, D_FF), BF16),
                   jax.ShapeDtypeStruct((T, D_FF), BF16), jax.ShapeDtypeStruct((T, D), BF16)],
        compiler_params=_params(("parallel",), VMEM_BIG),
    )(x, nw, win, wout)


def _ffn_bwd(dxo, x, nw, g, u, win, wout, layer, name):
    T, D = x.shape
    tm = _tile(T, FFN_BWD_ROWS)
    nj = N_CHIPS // 2

    def body(dxo_ref, x_ref, n_ref, g_ref, u_ref, win_ref, wo_ref, dx_ref, dg_ref, du_ref, a_ref, dyb_ref, dn_ref):
        @pl.when(pl.program_id(0) == 0)
        def _():
            dn_ref[...] = jnp.zeros_like(dn_ref)

        dxo = dxo_ref[...]
        dyb = _bf(0.5 * dxo)
        dyb_ref[...] = dyb
        dh = None
        for j in range(nj):
            cols = slice(j * FF_CHUNK, (j + 1) * FF_CHUNK)
            da = _dot_nt(dyb, wo_ref[2 * j:2 * j + 2].reshape(FF_CHUNK, D))
            gv = g_ref[:, cols].astype(F32)
            uv = u_ref[:, cols].astype(F32)
            sg = _sigmoid(gv)
            sl = gv * sg
            dgb = _bf(da * uv * _silu_grad(gv, sg))
            dub = _bf(da * sl)
            a_ref[:, cols] = _bf(sl * uv)
            dg_ref[:, cols] = dgb
            du_ref[:, cols] = dub
            part = _dot_nt(dgb, win_ref[j]) + _dot_nt(dub, win_ref[nj + j])
            dh = part if dh is None else dh + part
        xhat, r = _rms_stats(x_ref[...])
        dx_ref[...] = dxo + _rms_bwd(dh, xhat, r, n_ref[...])
        dn_ref[...] += jnp.sum(dh * xhat, axis=0, keepdims=True)

    row = pl.BlockSpec((tm, D), lambda i: (i, 0))
    wide = pl.BlockSpec((tm, D_FF), lambda i: (i, 0))
    return pl.pallas_call(
        body, name=name, grid=(T // tm,),
        in_specs=[row, row, pl.BlockSpec((None, 1, D), lambda i: (layer, 0, 0)), wide, wide] + _ffn_weight_specs(),
        out_specs=[row, wide, wide, wide, row, pl.BlockSpec((1, D), lambda i: (0, 0))],
        out_shape=[jax.ShapeDtypeStruct((T, D), F32), jax.ShapeDtypeStruct((T, D_FF), BF16),
                   jax.ShapeDtypeStruct((T, D_FF), BF16), jax.ShapeDtypeStruct((T, D_FF), BF16),
                   jax.ShapeDtypeStruct((T, D), BF16), jax.ShapeDtypeStruct((1, D), F32)],
        compiler_params=_params(("arbitrary",), VMEM_BIG),
    )(dxo, x, nw, g, u, win, wout)


def _matmul_tn(a, b, bm, bn, name, stack_n=False):
    T, M = a.shape
    N = b.shape[1]
    tk = _tile(T, 1024)
    bm, bn = _tile(M, bm), _tile(N, bn)

    def body(a_ref, b_ref, o_ref):
        @pl.when(pl.program_id(2) == 0)
        def _():
            o_ref[...] = jnp.zeros_like(o_ref)

        o_ref[...] += _dot_tn(_bf(a_ref[...]), _bf(b_ref[...]))

    if stack_n:
        out_spec = pl.BlockSpec((None, bm, bn), lambda m, n, k: (n, m, 0))
        out_shape = jax.ShapeDtypeStruct((N // bn, M, bn), F32)
    else:
        out_spec = pl.BlockSpec((bm, bn), lambda m, n, k: (m, n))
        out_shape = jax.ShapeDtypeStruct((M, N), F32)
    return pl.pallas_call(
        body, name=name, grid=(M // bm, N // bn, T // tk),
        in_specs=[pl.BlockSpec((tk, bm), lambda m, n, k: (k, m)),
                  pl.BlockSpec((tk, bn), lambda m, n, k: (k, n))],
        out_specs=out_spec, out_shape=out_shape,
        compiler_params=_params(("parallel", "parallel", "arbitrary"), VMEM_BIG),
    )(a, b)


def _matmul(a, b, name, trans_b=False, res=None, out_dtype=F32):
    T, K = a.shape
    N = b.shape[0] if trans_b else b.shape[1]
    tm = _tile(T, 512)

    def body(*refs):
        a_ref, b_ref = refs[0], refs[1]
        o_ref = refs[-1]
        av, bv = _bf(a_ref[...]), _bf(b_ref[...])
        acc = _dot_nt(av, bv) if trans_b else _dot(av, bv)
        if res is not None:
            acc = acc + refs[2][...]
        o_ref[...] = acc.astype(out_dtype)

    in_specs = [pl.BlockSpec((tm, K), lambda i: (i, 0)), pl.BlockSpec(b.shape, lambda i: (0, 0))]
    args = [a, b]
    if res is not None:
        in_specs.append(pl.BlockSpec((tm, N), lambda i: (i, 0)))
        args.append(res)
    return pl.pallas_call(
        body, name=name, grid=(T // tm,), in_specs=in_specs,
        out_specs=pl.BlockSpec((tm, N), lambda i: (i, 0)),
        out_shape=jax.ShapeDtypeStruct((T, N), out_dtype),
        compiler_params=_params(("parallel",), VMEM_BIG),
    )(*args)


def _rms_fwd_call(x, nw, layer, name):
    T, D = x.shape
    tm = _tile(T, 512)

    def body(x_ref, n_ref, o_ref):
        xhat, _ = _rms_stats(x_ref[...])
        o_ref[...] = _bf(xhat * n_ref[...])

    return pl.pallas_call(
        body, name=name, grid=(T // tm,),
        in_specs=[pl.BlockSpec((tm, D), lambda i: (i, 0)), pl.BlockSpec((None, 1, D), lambda i: (layer, 0, 0))],
        out_specs=pl.BlockSpec((tm, D), lambda i: (i, 0)),
        out_shape=jax.ShapeDtypeStruct((T, D), BF16),
        compiler_params=_params(("parallel",)),
    )(x, nw)


def _rms_bwd_call(dh, x, nw, dres, layer, name):
    T, D = x.shape
    tm = _tile(T, 512)

    def body(dh_ref, x_ref, n_ref, dr_ref, dx_ref, dn_ref):
        @pl.when(pl.program_id(0) == 0)
        def _():
            dn_ref[...] = jnp.zeros_like(dn_ref)

        xhat, r = _rms_stats(x_ref[...])
        dh_v = dh_ref[...]
        dx_ref[...] = dr_ref[...] + _rms_bwd(dh_v, xhat, r, n_ref[...])
        dn_ref[...] += jnp.sum(dh_v * xhat, axis=0, keepdims=True)

    row = pl.BlockSpec((tm, D), lambda i: (i, 0))
    return pl.pallas_call(
        body, name=name, grid=(T // tm,),
        in_specs=[row, row, pl.BlockSpec((None, 1, D), lambda i: (layer, 0, 0)), row],
        out_specs=[row, pl.BlockSpec((1, D), lambda i: (0, 0))],
        out_shape=[jax.ShapeDtypeStruct((T, D), F32), jax.ShapeDtypeStruct((1, D), F32)],
        compiler_params=_params(("arbitrary",)),
    )(dh, x, nw, dres)


def _shift_rows(x, s):
    n = x.shape[0]
    s = s % n
    return x if s == 0 else pltpu.roll(x, s, 0)


def _conv_fwd(proj, conv_w, name):
    T = proj.shape[0]
    C = 3 * DN_HEADS * DN_DIM
    cb = 512
    tm = _tile(T, 512)
    hb = tm // CONV_HALO

    def body(x_ref, xp_ref, w_ref, o_ref):
        i = pl.program_id(1)
        prev = jnp.where(i == 0, 0.0, xp_ref[...])
        ext = jnp.concatenate([prev, x_ref[...]], axis=0)
        w = w_ref[...]
        y = ext * w[DN_CONV - 1:DN_CONV, :]
        for k in range(DN_CONV - 1):
            y = y + _shift_rows(ext, DN_CONV - 1 - k) * w[k:k + 1, :]
        y = y[CONV_HALO:, :]
        o_ref[...] = y * _sigmoid(y)

    return pl.pallas_call(
        body, name=name, grid=(C // cb, T // tm),
        in_specs=[pl.BlockSpec((tm, cb), lambda c, i: (i, c)),
                  pl.BlockSpec((CONV_HALO, cb), lambda c, i: (jnp.maximum(i * hb - 1, 0), c)),
                  pl.BlockSpec((DN_CONV, cb), lambda c, i: (0, c))],
        out_specs=pl.BlockSpec((tm, cb), lambda c, i: (i, c)),
        out_shape=jax.ShapeDtypeStruct((T, C), F32),
        compiler_params=_params(("parallel", "parallel")),
    )(proj, proj, conv_w)


def _conv_bwd(dy, proj, conv_w, name):
    T = proj.shape[0]
    C = 3 * DN_HEADS * DN_DIM
    cb = 512
    tm = _tile(T, 512)
    hb = tm // CONV_HALO
    nt = T // tm

    def body(x_ref, xp_ref, xn_ref, dy_ref, dyn_ref, w_ref, dx_ref, dw_ref):
        i = pl.program_id(1)

        @pl.when(i == 0)
        def _():
            dw_ref[...] = jnp.zeros_like(dw_ref)

        prev = jnp.where(i == 0, 0.0, xp_ref[...])
        ext = jnp.concatenate([prev, x_ref[...], xn_ref[...]], axis=0)
        dy_ext = jnp.concatenate([jnp.zeros((CONV_HALO, cb), F32), dy_ref[...],
                                  jnp.where(i == nt - 1, 0.0, dyn_ref[...])], axis=0)
        w = w_ref[...]
        shifted = [_shift_rows(ext, DN_CONV - 1 - k) for k in range(DN_CONV)]
        y = shifted[0] * w[0:1, :]
        for k in range(1, DN_CONV):
            y = y + shifted[k] * w[k:k + 1, :]
        s = _sigmoid(y)
        dpre = dy_ext * _silu_grad(y, s)
        dx = dpre * w[DN_CONV - 1:DN_CONV, :]
        for k in range(DN_CONV - 1):
            dx = dx + _shift_rows(dpre, -(DN_CONV - 1 - k)) * w[k:k + 1, :]
        dx_ref[...] = _bf(dx[CONV_HALO:CONV_HALO + tm, :])
        rows = [jnp.sum((dpre * shifted[k])[CONV_HALO:CONV_HALO + tm, :], axis=0, keepdims=True) for k in range(DN_CONV)]
        dw_ref[...] += jnp.concatenate(rows, axis=0)

    last_halo = T // CONV_HALO - 1
    return pl.pallas_call(
        body, name=name, grid=(C // cb, nt),
        in_specs=[pl.BlockSpec((tm, cb), lambda c, i: (i, c)),
                  pl.BlockSpec((CONV_HALO, cb), lambda c, i: (jnp.maximum(i * hb - 1, 0), c)),
                  pl.BlockSpec((CONV_HALO, cb), lambda c, i: (jnp.minimum((i + 1) * hb, last_halo), c)),
                  pl.BlockSpec((tm, cb), lambda c, i: (i, c)),
                  pl.BlockSpec((CONV_HALO, cb), lambda c, i: (jnp.minimum((i + 1) * hb, last_halo), c)),
                  pl.BlockSpec((DN_CONV, cb), lambda c, i: (0, c))],
        out_specs=[pl.BlockSpec((tm, cb), lambda c, i: (i, c)),
                   pl.BlockSpec((DN_CONV, cb), lambda c, i: (0, c))],
        out_shape=[jax.ShapeDtypeStruct((T, C), BF16), jax.ShapeDtypeStruct((DN_CONV, C), F32)],
        compiler_params=_params(("parallel", "arbitrary")),
    )(proj, proj, proj, dy, dy, conv_w)


def _unit_lower_inverse(low, eye):
    p1 = -low
    p2 = _dot_mid(p1, p1)
    p4 = _dot_mid(p2, p2)
    a = eye + p1 + p2 + _dot_mid(p1, p2)
    p8 = _dot_mid(p4, p4)
    p16 = _dot_mid(p8, p8)
    b = eye + p4 + p8 + _dot_mid(p4, p8)
    p32 = _dot_mid(p16, p16)
    ab = _dot_mid(a, b)
    c = eye + p16 + p32 + _dot_mid(p16, p32)
    return _dot_mid(ab, c)


def _l2_unit(x):
    r = lax.rsqrt(jnp.sum(x * x, axis=-1, keepdims=True) + NORM_EPS)
    return x * r, r


class _BlockMasks:
    def __init__(self):
        n = DN_BLOCK
        row = lax.broadcasted_iota(jnp.int32, (n, n), 0)
        col = lax.broadcasted_iota(jnp.int32, (n, n), 1)
        same = (row // DN_CHUNK) == (col // DN_CHUNK)
        self.lower, self.strict_lower = same & (row >= col), same & (row > col)
        self.upper, self.strict_upper = same & (row <= col), same & (row < col)
        self.eye = (row == col).astype(F32)
        self.first = lax.broadcasted_iota(jnp.int32, (n, 1), 0) < DN_CHUNK


def _dn_gates(ba, hp):
    coef = -jnp.exp(hp[0:1, :])
    pre = ba + hp[1:2, :]
    return _sigmoid(ba), coef * _softplus(pre), coef, pre


def _dn_block_gates(mk, ba, hp):
    assert DN_BLOCK == 2 * DN_CHUNK
    beta_t, graw_t, coef, pre = _dn_gates(ba, hp)
    gcum_t = _dot_hi(mk.lower.astype(F32), graw_t)
    gl_t = jnp.where(mk.first, gcum_t[DN_CHUNK - 1:DN_CHUNK, :], gcum_t[DN_BLOCK - 1:DN_BLOCK, :])
    return beta_t, gcum_t, gl_t, graw_t, coef, pre


def _dn_local(mk, qraw, kraw, bc, gc, gl):
    f = {}
    f["qn"], f["rq"] = _l2_unit(qraw)
    qh = f["qn"] * (DN_DIM ** -0.5)
    kh, f["rk"] = _l2_unit(kraw)
    gr = jnp.broadcast_to(gc, (DN_BLOCK, DN_BLOCK)).T
    dec = jnp.where(mk.lower, jnp.exp(jnp.where(mk.lower, gc - gr, 0.0)), 0.0)
    kb = kh * bc
    mkk = _dot_nt(_bf(kb), _bf(kh))
    eg = jnp.exp(gc)
    mqk = _dot_nt(_bf(qh), _bf(kh))
    etl = jnp.exp(gl - gc)
    f.update(qh=qh, kh=kh, gr=gr, dec=dec, kb=kb, mkk=mkk, eg=eg, mqk=mqk, attn=mqk * dec, etl=etl, qd=qh * eg, kt=kh * etl)
    return f


def _dn_specs(rows, rev=None):
    at = (lambda n: n) if rev is None else rev
    hw = DN_HEADS * DN_DIM
    return dict(
        qkv=[pl.BlockSpec((rows, hw), lambda n, j=j: (at(n), j)) for j in range(3)],
        ba=pl.BlockSpec((rows, 128), lambda n: (at(n), BA_BLOCK)),
        hp=pl.BlockSpec((8, 128), lambda n: (0, 0)),
        tok=pl.BlockSpec((rows, hw), lambda n: (at(n), 0)),
        attn=pl.BlockSpec((rows, DN_HEADS * DN_CHUNK), lambda n: (at(n), 0)),
        gate=pl.BlockSpec((rows // DN_CHUNK, 8, 128), lambda n: (at(n), 0, 0)),
        state=pl.BlockSpec((rows // DN_CHUNK, DN_HEADS, DN_DIM, DN_DIM), lambda n: (at(n), 0, 0, 0)),
    )


def _call_with_exchange(body, exchange, *, name, steps, in_specs, out_specs, out_shape, args, vmem=None):
    if exchange is None:
        res = pl.pallas_call(body, name=name, grid=(steps,), in_specs=in_specs, out_specs=out_specs, out_shape=out_shape,
                             compiler_params=_params(("arbitrary",), vmem))(*args)
        return list(res), None
    n_in, n_out, m = len(in_specs), len(out_specs), len(exchange.operands)

    def hosted(*refs):
        ins, ex_ins = refs[:n_in], refs[n_in:n_in + m]
        outs, ex_outs = refs[n_in + m:n_in + m + n_out], refs[n_in + m + n_out:n_in + 2 * m + n_out]
        sems = refs[n_in + 2 * m + n_out:]

        @pl.when(pl.program_id(0) == 0)
        def _():
            exchange.start(ex_ins, ex_outs, sems)

        body(*ins, *outs)

        @pl.when(pl.program_id(0) == steps - 1)
        def _():
            exchange.finish(ex_ins, ex_outs, sems)

    res = pl.pallas_call(
        hosted, name=name, grid=(steps,), in_specs=list(in_specs) + [ANY] * m, out_specs=list(out_specs) + [ANY] * m,
        out_shape=list(out_shape) + list(exchange.out_shape), scratch_shapes=exchange.scratch,
        compiler_params=pltpu.CompilerParams(dimension_semantics=("arbitrary",), vmem_limit_bytes=vmem, has_side_effects=True),
    )(*args, *exchange.operands)
    return list(res[:n_out]), exchange.finalize(list(res[n_out:]))


def _dn_prep(qkv, proj, hp, name, exchange=None):
    T = qkv.shape[0]
    n_chunks = T // DN_CHUNK
    blocks = max(1, min(DN_PREP_CHUNKS, n_chunks) * DN_CHUNK // DN_BLOCK)
    group = blocks * DN_BLOCK // DN_CHUNK
    rows = blocks * DN_BLOCK
    hw = DN_HEADS * DN_DIM

    def body(q_ref, k_ref, v_ref, ba_ref, hp_ref, u_ref, w_ref, p_ref, qd_ref, kt_ref, gl_ref, inv_ref):
        mk = _BlockMasks()
        hp_v = hp_ref[...]
        for j in range(blocks):
            rs = slice(j * DN_BLOCK, (j + 1) * DN_BLOCK)
            beta_t, gcum_t, gl_t = _dn_block_gates(mk, ba_ref[rs, :], hp_v)[:3]
            for c in range(DN_BLOCK // DN_CHUNK):
                gl_ref[j * (DN_BLOCK // DN_CHUNK) + c] = jnp.broadcast_to(gl_t[c * DN_CHUNK:c * DN_CHUNK + 1, :], (8, 128))
            for h in range(DN_HEADS):
                sl = slice(h * DN_DIM, (h + 1) * DN_DIM)
                gate = slice(DN_HEADS + h, DN_HEADS + h + 1)
                bc = beta_t[:, h:h + 1]
                f = _dn_local(mk, q_ref[rs, sl], k_ref[rs, sl], bc, gcum_t[:, gate], gl_t[:, gate])
                inv = _unit_lower_inverse(jnp.where(mk.strict_lower, f["mkk"] * f["dec"], 0.0), mk.eye)
                inv_ref[rs, sl] = inv
                sol = _dot_mid(inv, jnp.concatenate([v_ref[rs, sl] * bc, f["kb"] * f["eg"]], axis=1))
                u_ref[rs, sl] = sol[:, :DN_DIM]
                w_ref[rs, sl] = _bf(sol[:, DN_DIM:])
                for c in range(DN_BLOCK // DN_CHUNK):
                    cr = slice(c * DN_CHUNK, (c + 1) * DN_CHUNK)
                    p_ref[j * DN_BLOCK + c * DN_CHUNK:j * DN_BLOCK + (c + 1) * DN_CHUNK, h * DN_CHUNK:(h + 1) * DN_CHUNK] = _bf(f["attn"][cr, cr])
                qd_ref[rs, sl] = _bf(f["qd"])
                kt_ref[rs, sl] = _bf(f["kt"])

    sp = _dn_specs(rows)
    tok16 = jax.ShapeDtypeStruct((T, hw), BF16)
    return _call_with_exchange(
        body, exchange, name=name, steps=n_chunks // group,
        in_specs=sp["qkv"] + [sp["ba"], sp["hp"]],
        out_specs=[sp["tok"], sp["tok"], sp["attn"], sp["tok"], sp["tok"], sp["gate"], sp["tok"]],
        out_shape=[jax.ShapeDtypeStruct((T, hw), F32), tok16, jax.ShapeDtypeStruct((T, DN_HEADS * DN_CHUNK), BF16),
                   tok16, tok16, jax.ShapeDtypeStruct((n_chunks, 8, 128), F32), jax.ShapeDtypeStruct((T, hw), F32)],
        args=(qkv, qkv, qkv, proj, hp))


def _dn_scan(u, w, p, qd, kt, gl, name):
    T = u.shape[0]
    n_chunks = T // DN_CHUNK
    group = min(DN_SCAN_CHUNKS, n_chunks)
    rows = group * DN_CHUNK
    hw = DN_HEADS * DN_DIM

    def body(u_ref, w_ref, p_ref, qd_ref, kt_ref, gl_ref, o_ref, vn_ref, sall_ref, s_s):
        @pl.when(pl.program_id(0) == 0)
        def _():
            s_s[...] = jnp.zeros_like(s_s)

        state = [s_s[h] for h in range(DN_HEADS)]
        for j in range(group):
            rs = slice(j * DN_CHUNK, (j + 1) * DN_CHUNK)
            for h in range(DN_HEADS):
                sl = slice(h * DN_DIM, (h + 1) * DN_DIM)
                sall_ref[j, h] = state[h]
                sb = _bf(state[h])
                vnb = _bf(u_ref[rs, sl] - _dot(w_ref[rs, sl], sb))
                vn_ref[rs, sl] = vnb
                o_ref[rs, sl] = _dot(qd_ref[rs, sl], sb) + _dot(p_ref[rs, h * DN_CHUNK:(h + 1) * DN_CHUNK], vnb)
                egl = jnp.exp(gl_ref[j, 0:1, DN_HEADS + h:DN_HEADS + h + 1])
                state[h] = state[h] * egl + _dot_tn(kt_ref[rs, sl], vnb)
        for h in range(DN_HEADS):
            s_s[h] = state[h]

    sp = _dn_specs(rows)
    return pl.pallas_call(
        body, name=name, grid=(n_chunks // group,),
        in_specs=[sp["tok"], sp["tok"], sp["attn"], sp["tok"], sp["tok"], sp["gate"]],
        out_specs=[sp["tok"], sp["tok"], sp["state"]],
        out_shape=[jax.ShapeDtypeStruct((T, hw), F32), jax.ShapeDtypeStruct((T, hw), BF16),
                   jax.ShapeDtypeStruct((n_chunks, DN_HEADS, DN_DIM, DN_DIM), F32)],
        scratch_shapes=[pltpu.VMEM((DN_HEADS, DN_DIM, DN_DIM), F32)],
        compiler_params=_params(("arbitrary",)),
    )(u, w, p, qd, kt, gl)


def _dn_scan_bwd(w, p, qd, kt, gl, vn, sall, do, name):
    T = w.shape[0]
    n_chunks = T // DN_CHUNK
    group = min(DN_SCAN_CHUNKS, n_chunks)
    rows = group * DN_CHUNK
    hw = DN_HEADS * DN_DIM
    last = n_chunks // group - 1

    def body(w_ref, p_ref, qd_ref, kt_ref, gl_ref, vn_ref, sall_ref, do_ref, dvn_ref, dkt_ref, dgl_ref, ds_s):
        @pl.when(pl.program_id(0) == 0)
        def _():
            ds_s[...] = jnp.zeros_like(ds_s)

        lane = lax.broadcasted_iota(jnp.int32, (8, 128), 1)
        d_state = [ds_s[h] for h in range(DN_HEADS)]
        for j in reversed(range(group)):
            rs = slice(j * DN_CHUNK, (j + 1) * DN_CHUNK)
            dgl_tile = jnp.zeros((8, 128), F32)
            for h in range(DN_HEADS):
                sl = slice(h * DN_DIM, (h + 1) * DN_DIM)
                d_out = _bf(do_ref[rs, sl])
                d_new = d_state[h]
                d_newb = _bf(d_new)
                d_vn = _dot_tn(p_ref[rs, h * DN_CHUNK:(h + 1) * DN_CHUNK], d_out) + _dot(kt_ref[rs, sl], d_newb)
                dvn_ref[rs, sl] = d_vn
                dkt_ref[rs, sl] = _dot_nt(vn_ref[rs, sl], d_newb)
                egl = jnp.exp(gl_ref[j, 0:1, DN_HEADS + h:DN_HEADS + h + 1])
                prod = jnp.sum(d_new * sall_ref[j, h], axis=1, keepdims=True)
                dgl_tile = jnp.where(lane == DN_HEADS + h, jnp.sum(prod, axis=0, keepdims=True) * egl, dgl_tile)
                d_state[h] = d_new * egl + _dot_tn(qd_ref[rs, sl], d_out) - _dot_tn(w_ref[rs, sl], _bf(d_vn))
            dgl_ref[j] = dgl_tile
        for h in range(DN_HEADS):
            ds_s[h] = d_state[h]

    sp = _dn_specs(rows, rev=lambda n: last - n)
    return pl.pallas_call(
        body, name=name, grid=(n_chunks // group,),
        in_specs=[sp["tok"], sp["attn"], sp["tok"], sp["tok"], sp["gate"], sp["tok"], sp["state"], sp["tok"]],
        out_specs=[sp["tok"], sp["tok"], sp["gate"]],
        out_shape=[jax.ShapeDtypeStruct((T, hw), F32), jax.ShapeDtypeStruct((T, hw), F32),
                   jax.ShapeDtypeStruct((n_chunks, 8, 128), F32)],
        scratch_shapes=[pltpu.VMEM((DN_HEADS, DN_DIM, DN_DIM), F32)],
        compiler_params=_params(("arbitrary",)),
    )(w, p, qd, kt, gl, vn, sall, do)


def _dn_prep_bwd(qkv, proj, hp, sall, vn, do, dvn, dkt, dgl, inv, u, w, name):
    T = qkv.shape[0]
    n_chunks = T // DN_CHUNK
    blocks = max(1, min(DN_PREP_CHUNKS, n_chunks) * DN_CHUNK // DN_BLOCK)
    per_block = DN_BLOCK // DN_CHUNK
    group = blocks * per_block
    rows = blocks * DN_BLOCK
    hw = DN_HEADS * DN_DIM
    first_rows, second_rows = slice(0, DN_CHUNK), slice(DN_CHUNK, DN_BLOCK)

    def rowsum(x):
        return jnp.sum(x, axis=1, keepdims=True)

    def by_chunk(x, s0, s1, fn):
        return jnp.concatenate([fn(x[first_rows], s0), fn(x[second_rows], s1)], axis=0)

    def body(q_ref, k_ref, v_ref, ba_ref, hp_ref, sall_ref, vn_ref, do_ref, dvn_ref, dkt_ref, dgl_ref,
             inv_ref, u_ref, w_ref, dqkv_ref, dba_ref, dhp_ref):
        @pl.when(pl.program_id(0) == 0)
        def _():
            dhp_ref[...] = jnp.zeros_like(dhp_ref)

        mk = _BlockMasks()
        hp_v = hp_ref[...]
        total = jnp.zeros((8, 128), F32)
        for j in range(blocks):
            rs = slice(j * DN_BLOCK, (j + 1) * DN_BLOCK)
            total = total + one_block(mk, hp_v, *(r.at[rs, :] for r in (q_ref, k_ref, v_ref, ba_ref)),
                                      sall_ref.at[pl.ds(j * per_block, per_block)],
                                      *(r.at[rs, :] for r in (vn_ref, do_ref, dvn_ref, dkt_ref)),
                                      dgl_ref.at[pl.ds(j * per_block, per_block)],
                                      *(r.at[rs, :] for r in (inv_ref, u_ref, w_ref)),
                                      *(dqkv_ref.at[rs, pl.ds(i * hw, hw)] for i in range(3)), dba_ref.at[rs, :])
        dhp_ref[...] += total

    def one_block(mk, hp_v, q_ref, k_ref, v_ref, ba_ref, state_ref, vn_ref, do_ref, dvn_ref, dkt_ref, dgl_ref,
                  inv_ref, u_ref, w_ref, dq_ref, dk_ref, dv_ref, dba_ref):
        ba = ba_ref[...]
        beta_t, gcum_t, gl_t, graw_t, coef, pre = _dn_block_gates(mk, ba, hp_v)
        lane = lax.broadcasted_iota(jnp.int32, (DN_BLOCK, 128), 1)
        rowi = lax.broadcasted_iota(jnp.int32, (DN_BLOCK, 1), 0)
        dgcum_t = jnp.zeros((DN_BLOCK, 128), F32)
        dbeta_t = jnp.zeros((DN_BLOCK, 128), F32)
        for h in range(DN_HEADS):
            sl = slice(h * DN_DIM, (h + 1) * DN_DIM)
            gate = slice(DN_HEADS + h, DN_HEADS + h + 1)
            gc = gcum_t[:, gate]
            bc = beta_t[:, h:h + 1]
            sb0, sb1 = _bf(state_ref[0, h]), _bf(state_ref[1, h])
            vh = v_ref[:, sl]
            f = _dn_local(mk, q_ref[:, sl], k_ref[:, sl], bc, gc, gl_t[:, gate])
            qh, kh, kb, dec, eg, etl = f["qh"], f["kh"], f["kb"], f["dec"], f["eg"], f["etl"]
            qd, kt = f["qd"], f["kt"]
            qb, kbf, kbb = _bf(qh), _bf(kh), _bf(kb)
            dec_t = jnp.where(mk.upper, jnp.exp(jnp.where(mk.upper, f["gr"] - gc, 0.0)), 0.0)
            mkk_t = f["mkk"].T
            inv_t = inv_ref[:, sl].T
            mqk_t = f["mqk"].T

            d_out = _bf(do_ref[:, sl])
            vnb = vn_ref[:, sl]
            d_qd = by_chunk(d_out, sb0, sb1, _dot_nt)
            d_attn = _dot_nt(d_out, vnb)
            d_attn_t = _dot_nt(vnb, d_out)
            d_vn = dvn_ref[:, sl]
            d_kt = dkt_ref[:, sl]
            d_w = -by_chunk(_bf(d_vn), sb0, sb1, _dot_nt)
            d_rhs = _dot_mid(inv_t, jnp.concatenate([d_vn, d_w], axis=1))
            d_bu, d_bw = d_rhs[:, :DN_DIM], d_rhs[:, DN_DIM:]
            ub, wb, d_bub, d_bwb = _bf(u_ref[:, sl]), w_ref[:, sl], _bf(d_bu), _bf(d_bw)
            d_low = -(_dot_nt(d_bub, ub) + _dot_nt(d_bwb, wb))
            d_low_t = -(_dot_nt(ub, d_bub) + _dot_nt(wb, d_bwb))
            d_mkk = jnp.where(mk.strict_lower, d_low * dec, 0.0)
            d_mkk_t = jnp.where(mk.strict_upper, d_low_t * dec_t, 0.0)
            d_mqk = jnp.where(mk.lower, d_attn * dec, 0.0)
            d_mqk_t = jnp.where(mk.upper, d_attn_t * dec_t, 0.0)
            bw = kb * eg
            d_kb = _dot(_bf(d_mkk), kbf) + d_bw * eg
            d_k = _dot(_bf(d_mkk_t), kbb) + _dot(_bf(d_mqk_t), qb) + d_kt * etl + d_kb * bc
            d_q = _dot(_bf(d_mqk), kbf) + d_qd * eg
            d_beta = rowsum(d_kb * kh) + rowsum(d_bu * vh)
            dv_ref[:, sl] = d_bu * bc
            e_mat = d_mkk * f["mkk"] + d_mqk * f["mqk"]
            e_mat_t = d_mkk_t * mkk_t + d_mqk_t * mqk_t
            kt_term = rowsum(d_kt * kt)
            d_g = rowsum(e_mat) - rowsum(e_mat_t) + rowsum(d_qd * qd) + rowsum(d_bw * bw) - kt_term
            for c, chunk_rows in enumerate((mk.first, ~mk.first)):
                d_glast = dgl_ref[c, 0:1, gate] + jnp.sum(jnp.where(chunk_rows, kt_term, 0.0), axis=0, keepdims=True)
                d_g = d_g + jnp.where(rowi == (c + 1) * DN_CHUNK - 1, d_glast, 0.0)
            qn = f["qn"]
            d_qs = d_q * (DN_DIM ** -0.5)
            dq_ref[:, sl] = f["rq"] * (d_qs - qn * rowsum(d_qs * qn))
            dk_ref[:, sl] = f["rk"] * (d_k - kh * rowsum(d_k * kh))
            dgcum_t = jnp.where(lane == DN_HEADS + h, d_g, dgcum_t)
            dbeta_t = jnp.where(lane == h, d_beta, dbeta_t)
        dgraw_t = _dot_hi(mk.upper.astype(F32), dgcum_t)
        sp = _sigmoid(pre)
        d_pre = dgraw_t * coef * sp
        dba_ref[...] = jnp.where(lane < DN_HEADS, dbeta_t * beta_t * (1.0 - beta_t),
                                 jnp.where(lane < 2 * DN_HEADS, d_pre, 0.0))
        in_g = (lane >= DN_HEADS) & (lane < 2 * DN_HEADS)
        d_alog = jnp.sum(jnp.where(in_g, dgraw_t * graw_t, 0.0), axis=0, keepdims=True)
        d_dtb = jnp.sum(jnp.where(in_g, d_pre, 0.0), axis=0, keepdims=True)
        return jnp.concatenate([d_alog, d_dtb, jnp.zeros((6, 128), F32)], axis=0)

    sp = _dn_specs(rows)
    return pl.pallas_call(
        body, name=name, grid=(n_chunks // group,),
        in_specs=sp["qkv"] + [sp["ba"], sp["hp"], sp["state"]] + [sp["tok"]] * 4 + [sp["gate"]] + [sp["tok"]] * 3,
        out_specs=[pl.BlockSpec((rows, 3 * hw), lambda n: (n, 0)), pl.BlockSpec((rows, 128), lambda n: (n, 0)), sp["hp"]],
        out_shape=[jax.ShapeDtypeStruct((T, 3 * hw), F32), jax.ShapeDtypeStruct((T, 128), F32),
                   jax.ShapeDtypeStruct((8, 128), F32)],
        compiler_params=_params(("arbitrary",)),
    )(qkv, qkv, qkv, proj, hp, sall, vn, do, dvn, dkt, dgl, inv, u, w)


def _mix_fwd(o, proj, dn_norm, sg_norm, sg_w, sg_bt, name):
    T = o.shape[0]
    tm = _tile(T, 512)
    hw = DN_HEADS * DN_DIM
    nc = tm // SG_CHUNK

    def body(o_ref, z_ref, su_ref, sv_ref, dnn_ref, sgn_ref, sgw_ref, sgb_ref, mix_ref):
        dnn = dnn_ref[...]
        for h in range(DN_HEADS):
            sl = slice(h * DN_DIM, (h + 1) * DN_DIM)
            xhat, _ = _rms_stats(o_ref[:, sl])
            z = z_ref[:, sl]
            mix_ref[:, sl] = _bf(xhat * dnn * (z * _sigmoid(z)))
        tri = lax.broadcasted_iota(jnp.int32, (SG_CHUNK, SG_CHUNK), 0) >= lax.broadcasted_iota(jnp.int32, (SG_CHUNK, SG_CHUNK), 1)
        for g in range(SG_GROUPS):
            sl = slice(g * SG_DIM, (g + 1) * SG_DIM)
            xhat, _ = _rms_stats(_gelu(sv_ref[:, sl]))
            svn = _bf(xhat * sgn_ref[g:g + 1, :])
            sua = _gelu(su_ref[:, sl])
            wt = _bf(jnp.where(tri, sgw_ref[g], 0.0))
            bias = sgb_ref[:, g:g + 1]
            for c in range(nc):
                rows = slice(c * SG_CHUNK, (c + 1) * SG_CHUNK)
                mixed = _dot(wt, svn[rows, :]) + bias
                mix_ref[rows, hw + g * SG_DIM:hw + (g + 1) * SG_DIM] = _bf(sua[rows, :] * mixed)

    full = lambda shape: pl.BlockSpec(shape, lambda i: (0,) * len(shape))
    return pl.pallas_call(
        body, name=name, grid=(T // tm,),
        in_specs=[pl.BlockSpec((tm, hw), lambda i: (i, 0)),
                  pl.BlockSpec((tm, hw), lambda i: (i, 3)),
                  pl.BlockSpec((tm, hw), lambda i: (i, 4)),
                  pl.BlockSpec((tm, hw), lambda i: (i, 5)),
                  full((1, DN_DIM)), full((SG_GROUPS, SG_DIM)), full((SG_GROUPS, SG_CHUNK, SG_CHUNK)),
                  full((SG_CHUNK, 128))],
        out_specs=pl.BlockSpec((tm, 2 * hw), lambda i: (i, 0)),
        out_shape=jax.ShapeDtypeStruct((T, 2 * hw), BF16),
        compiler_params=_params(("parallel",)),
    )(o, proj, proj, proj, dn_norm, sg_norm, sg_w, sg_bt)


def _mix_bwd(dmix, o, proj, dn_norm, sg_norm, sg_w, sg_bt, name):
    T = o.shape[0]
    tm = _tile(T, 512)
    hw = DN_HEADS * DN_DIM
    nc = tm // SG_CHUNK

    def body(dm_ref, o_ref, z_ref, su_ref, sv_ref, dnn_ref, sgn_ref, sgw_ref, sgb_ref,
             do_ref, dz_ref, ddnn_ref, dsgn_ref, dsgw_ref, dsgb_ref):
        @pl.when(pl.program_id(0) == 0)
        def _():
            ddnn_ref[...] = jnp.zeros_like(ddnn_ref)
            dsgn_ref[...] = jnp.zeros_like(dsgn_ref)
            dsgw_ref[...] = jnp.zeros_like(dsgw_ref)
            dsgb_ref[...] = jnp.zeros_like(dsgb_ref)

        dnn = dnn_ref[...]
        ddnn = jnp.zeros((1, DN_DIM), F32)
        for h in range(DN_HEADS):
            sl = slice(h * DN_DIM, (h + 1) * DN_DIM)
            xhat, r = _rms_stats(o_ref[:, sl])
            z = z_ref[:, sl]
            sz = _sigmoid(z)
            doa = dm_ref[:, sl]
            dyn = doa * (z * sz)
            dz_ref[:, sl] = _bf(doa * xhat * dnn * _silu_grad(z, sz))
            do_ref[:, sl] = _rms_bwd(dyn, xhat, r, dnn)
            ddnn = ddnn + jnp.sum(dyn * xhat, axis=0, keepdims=True)
        ddnn_ref[...] += ddnn
        tri = lax.broadcasted_iota(jnp.int32, (SG_CHUNK, SG_CHUNK), 0) >= lax.broadcasted_iota(jnp.int32, (SG_CHUNK, SG_CHUNK), 1)
        lane = lax.broadcasted_iota(jnp.int32, (SG_CHUNK, 128), 1)
        dsgb = jnp.zeros((SG_CHUNK, 128), F32)
        dsgn_rows = []
        for g in range(SG_GROUPS):
            sl = slice(g * SG_DIM, (g + 1) * SG_DIM)
            sv = sv_ref[:, sl]
            su = su_ref[:, sl]
            xhat, r = _rms_stats(_gelu(sv))
            sgn = sgn_ref[g:g + 1, :]
            svn = _bf(xhat * sgn)
            sua = _gelu(su)
            wt = _bf(jnp.where(tri, sgw_ref[g], 0.0))
            bias = sgb_ref[:, g:g + 1]
            dw = jnp.zeros((SG_CHUNK, SG_CHUNK), F32)
            db = jnp.zeros((SG_CHUNK, 1), F32)
            dsua, dsvn = [], []
            for c in range(nc):
                rows = slice(c * SG_CHUNK, (c + 1) * SG_CHUNK)
                mixed = _dot(wt, svn[rows, :]) + bias
                dob = dm_ref[rows, hw + g * SG_DIM:hw + (g + 1) * SG_DIM]
                dsua.append(dob * mixed)
                dmixed = dob * sua[rows, :]
                dmb = _bf(dmixed)
                dsvn.append(_dot_tn(wt, dmb))
                dw = dw + _dot_nt(dmb, svn[rows, :])
                db = db + jnp.sum(dmixed, axis=1, keepdims=True)
            dsua = jnp.concatenate(dsua, axis=0) if nc > 1 else dsua[0]
            dsvn = jnp.concatenate(dsvn, axis=0) if nc > 1 else dsvn[0]
            dz_ref[:, hw + g * SG_DIM:hw + (g + 1) * SG_DIM] = _bf(dsua * _gelu_grad(su))
            dz_ref[:, 2 * hw + g * SG_DIM:2 * hw + (g + 1) * SG_DIM] = _bf(_rms_bwd(dsvn, xhat, r, sgn) * _gelu_grad(sv))
            dsgn_rows.append(jnp.sum(dsvn * xhat, axis=0, keepdims=True))
            dsgw_ref[g] += jnp.where(tri, dw, 0.0)
            dsgb = jnp.where(lane == g, db, dsgb)
        dsgn_ref[...] += jnp.concatenate(dsgn_rows, axis=0)
        dsgb_ref[...] += dsgb

    full = lambda shape: pl.BlockSpec(shape, lambda i: (0,) * len(shape))
    return pl.pallas_call(
        body, name=name, grid=(T // tm,),
        in_specs=[pl.BlockSpec((tm, 2 * hw), lambda i: (i, 0)),
                  pl.BlockSpec((tm, hw), lambda i: (i, 0)),
                  pl.BlockSpec((tm, hw), lambda i: (i, 3)),
                  pl.BlockSpec((tm, hw), lambda i: (i, 4)),
                  pl.BlockSpec((tm, hw), lambda i: (i, 5)),
                  full((1, DN_DIM)), full((SG_GROUPS, SG_DIM)), full((SG_GROUPS, SG_CHUNK, SG_CHUNK)),
                  full((SG_CHUNK, 128))],
        out_specs=[pl.BlockSpec((tm, hw), lambda i: (i, 0)),
                   pl.BlockSpec((tm, 3 * hw), lambda i: (i, 0)),
                   full((1, DN_DIM)), full((SG_GROUPS, SG_DIM)), full((SG_GROUPS, SG_CHUNK, SG_CHUNK)),
                   full((SG_CHUNK, 128))],
        out_shape=[jax.ShapeDtypeStruct((T, hw), F32), jax.ShapeDtypeStruct((T, 3 * hw), BF16),
                   jax.ShapeDtypeStruct((1, DN_DIM), F32), jax.ShapeDtypeStruct((SG_GROUPS, SG_DIM), F32),
                   jax.ShapeDtypeStruct((SG_GROUPS, SG_CHUNK, SG_CHUNK), F32),
                   jax.ShapeDtypeStruct((SG_CHUNK, 128), F32)],
        compiler_params=_params(("arbitrary",)),
    )(dmix, o, proj, proj, proj, dn_norm, sg_norm, sg_w, sg_bt)


def _window_sums(h, sign):
    sums, s, w = {}, h, 1
    while w < POOL_WINDOWS[-1]:
        s = s + _shift_rows(s, sign * w)
        w *= 2
        sums[w] = s
    return sums


def _pool_counts(t_global):
    return [jnp.minimum(t_global + 1, win).astype(F32) for win in POOL_WINDOWS]


def _pooled_groups(ext_h, row0, tm):
    sums = _window_sums(ext_h, 1)
    t_global = row0 + lax.broadcasted_iota(jnp.int32, (tm, 1), 0)
    counts = _pool_counts(t_global)
    out = []
    for gi, win in enumerate(POOL_WINDOWS):
        cols = slice(gi * POOL_DIM, (gi + 1) * POOL_DIM)
        out.append(sums[win][POOL_HALO:, cols] / counts[gi] - ext_h[POOL_HALO:, cols])
    return out


def _pool_fwd(x, nw, pool_w, pool_scale, layer, name):
    T, D = x.shape
    tm = _tile(T, 256)
    hb = tm // POOL_HALO

    def body(x_ref, xp_ref, n_ref, w_ref, s_ref, xo_ref):
        i = pl.program_id(0)
        prev = jnp.where(i == 0, 0.0, xp_ref[...])
        ext = jnp.concatenate([prev, x_ref[...]], axis=0)
        xhat, _ = _rms_stats(ext)
        pooled = _pooled_groups(xhat * n_ref[...], i * tm, tm)
        for gi in range(len(POOL_WINDOWS)):
            cols = slice(gi * POOL_DIM, (gi + 1) * POOL_DIM)
            xo_ref[:, cols] = x_ref[:, cols] + _dot(_bf(pooled[gi]), w_ref[gi]) * s_ref[:, cols]

    return pl.pallas_call(
        body, name=name, grid=(T // tm,),
        in_specs=[pl.BlockSpec((tm, D), lambda i: (i, 0)),
                  pl.BlockSpec((POOL_HALO, D), lambda i: (jnp.maximum(i * hb - 1, 0), 0)),
                  pl.BlockSpec((None, 1, D), lambda i: (layer, 0, 0)),
                  pl.BlockSpec(pool_w.shape, lambda i: (0, 0, 0)),
                  pl.BlockSpec((1, D), lambda i: (0, 0))],
        out_specs=pl.BlockSpec((tm, D), lambda i: (i, 0)),
        out_shape=jax.ShapeDtypeStruct((T, D), F32),
        compiler_params=_params(("parallel",)),
    )(x, x, nw, pool_w, pool_scale)


def _pool_bwd(dxo, x, nw, pool_w, pool_scale, layer, name):
    T, D = x.shape
    tm = _tile(T, 256)
    hb = tm // POOL_HALO
    nt = T // tm
    ng = len(POOL_WINDOWS)

    def body(dxo_ref, dxn_ref, x_ref, xp_ref, n_ref, w_ref, s_ref, dx_ref, dw_ref, ds_ref, dn_ref):
        i = pl.program_id(0)

        @pl.when(i == 0)
        def _():
            dw_ref[...] = jnp.zeros_like(dw_ref)
            ds_ref[...] = jnp.zeros_like(ds_ref)
            dn_ref[...] = jnp.zeros_like(dn_ref)

        prev = jnp.where(i == 0, 0.0, xp_ref[...])
        ext = jnp.concatenate([prev, x_ref[...]], axis=0)
        xhat_ext, r_ext = _rms_stats(ext)
        nv = n_ref[...]
        pooled = _pooled_groups(xhat_ext * nv, i * tm, tm)
        dxo = dxo_ref[...]
        scale = s_ref[...]
        dout_ext = jnp.concatenate([dxo, jnp.where(i == nt - 1, 0.0, dxn_ref[...])], axis=0) * scale
        t_ext = i * tm + lax.broadcasted_iota(jnp.int32, (tm + POOL_HALO, 1), 0)
        counts = _pool_counts(t_ext)
        dh_cols, ds_cols = [], []
        for gi, win in enumerate(POOL_WINDOWS):
            cols = slice(gi * POOL_DIM, (gi + 1) * POOL_DIM)
            wg = w_ref[gi]
            pb = _bf(pooled[gi])
            doutb = _bf(dout_ext[:, cols])
            dpooled = _dot_nt(doutb, wg)
            ahead = _window_sums(dpooled / counts[gi], -1)[win]
            dh_cols.append(ahead[:tm, :] - dpooled[:tm, :])
            dw_ref[gi] += _dot_tn(pb, doutb[:tm, :])
            ds_cols.append(jnp.sum(dxo[:, cols] * _dot(pb, wg), axis=0, keepdims=True))
        dh = jnp.concatenate(dh_cols, axis=1)
        xhat, r = xhat_ext[POOL_HALO:, :], r_ext[POOL_HALO:, :]
        dx_ref[...] = dxo + _rms_bwd(dh, xhat, r, nv)
        dn_ref[...] += jnp.sum(dh * xhat, axis=0, keepdims=True)
        ds_ref[...] += jnp.concatenate(ds_cols, axis=1)

    last_halo = T // POOL_HALO - 1
    return pl.pallas_call(
        body, name=name, grid=(nt,),
        in_specs=[pl.BlockSpec((tm, D), lambda i: (i, 0)),
                  pl.BlockSpec((POOL_HALO, D), lambda i: (jnp.minimum((i + 1) * hb, last_halo), 0)),
                  pl.BlockSpec((tm, D), lambda i: (i, 0)),
                  pl.BlockSpec((POOL_HALO, D), lambda i: (jnp.maximum(i * hb - 1, 0), 0)),
                  pl.BlockSpec((None, 1, D), lambda i: (layer, 0, 0)),
                  pl.BlockSpec(pool_w.shape, lambda i: (0, 0, 0)),
                  pl.BlockSpec((1, D), lambda i: (0, 0))],
        out_specs=[pl.BlockSpec((tm, D), lambda i: (i, 0)),
                   pl.BlockSpec((ng, POOL_DIM, POOL_DIM), lambda i: (0, 0, 0)),
                   pl.BlockSpec((1, D), lambda i: (0, 0)),
                   pl.BlockSpec((1, D), lambda i: (0, 0))],
        out_shape=[jax.ShapeDtypeStruct((T, D), F32), jax.ShapeDtypeStruct((ng, POOL_DIM, POOL_DIM), F32),
                   jax.ShapeDtypeStruct((1, D), F32), jax.ShapeDtypeStruct((1, D), F32)],
        compiler_params=_params(("arbitrary",)),
    )(dxo, dxo, x, x, nw, pool_w, pool_scale)


def _loss_head(x, target, fn, name):
    T, D = x.shape
    tm = _tile(T, 512)

    def body(x_ref, t_ref, n_ref, loss_ref, dx_ref, dn_ref):
        @pl.when(pl.program_id(0) == 0)
        def _():
            loss_ref[...] = jnp.zeros_like(loss_ref)
            dn_ref[...] = jnp.zeros_like(dn_ref)

        xhat, r = _rms_stats(x_ref[...])
        nv = n_ref[...]
        err = xhat * nv - t_ref[...]
        part = jnp.sum(jnp.sum(err * err, axis=1, keepdims=True), axis=0, keepdims=True)
        loss_ref[...] += 0.5 * part / D
        dy = err / D
        dx_ref[...] = _rms_bwd(dy, xhat, r, nv)
        dn_ref[...] += jnp.sum(dy * xhat, axis=0, keepdims=True)

    row = pl.BlockSpec((tm, D), lambda i: (i, 0))
    return pl.pallas_call(
        body, name=name, grid=(T // tm,),
        in_specs=[row, row, pl.BlockSpec((1, D), lambda i: (0, 0))],
        out_specs=[pl.BlockSpec((1, 1), lambda i: (0, 0)), row, pl.BlockSpec((1, D), lambda i: (0, 0))],
        out_shape=[jax.ShapeDtypeStruct((1, 1), F32), jax.ShapeDtypeStruct((T, D), F32),
                   jax.ShapeDtypeStruct((1, D), F32)],
        compiler_params=_params(("arbitrary",)),
    )(x, target, fn)


def _adamw(w, g, m, v, name):
    R, C = w.shape
    br = R
    for cand in (512, 256, 128, 64, 32, 16, 8):
        if R % cand == 0 and cand * C * 4 <= 2 * 1024 * 1024:
            br = cand
            break

    def body(w_ref, g_ref, m_ref, v_ref, d_ref, mo_ref, vo_ref):
        gv = g_ref[...]
        m_new = ADAM_B1 * m_ref[...] + (1.0 - ADAM_B1) * gv
        v_new = ADAM_B2 * v_ref[...] + (1.0 - ADAM_B2) * (gv * gv)
        m_hat = m_new / (1.0 - ADAM_B1 ** ADAM_STEP)
        v_hat = v_new / (1.0 - ADAM_B2 ** ADAM_STEP)
        d_ref[...] = -ADAM_LR * (m_hat / (jnp.sqrt(v_hat) + ADAM_EPS) + ADAM_WD * w_ref[...])
        mo_ref[...] = m_new
        vo_ref[...] = v_new

    blk = pl.BlockSpec((br, C), lambda i: (i, 0))
    return pl.pallas_call(
        body, name=name, grid=(R // br,), in_specs=[blk] * 4, out_specs=[blk] * 3,
        out_shape=[jax.ShapeDtypeStruct((R, C), F32)] * 3,
        compiler_params=_params(("parallel",)),
    )(w, g, m, v)


def _mesh_pos():
    return lax.axis_index("x"), lax.axis_index("y"), lax.axis_index("c")


def _other_chips(x, y):
    return [(1 - x, y), (x, 1 - y), (1 - x, 1 - y)]


def _half_of(ref, shape, h):
    size = shape[0] // 2
    return ref.at[pl.ds(h * size, size)]


class _ChipGather:
    def __init__(self, shards, split):
        self.shards, self.split = list(shards), list(split)
        self.operands = self.shards
        n = len(self.shards)
        self.out_shape = [jax.ShapeDtypeStruct((N_CHIPS,) + s.shape, s.dtype) for s in self.shards]
        self.scratch = [pltpu.SemaphoreType.DMA((n, 3))] * 4

    def _piece(self, a, ref, h):
        return _half_of(ref, self.shards[a].shape, h) if self.split[a] else ref

    def start(self, ins, outs, sems):
        send_sems, recv_sems = sems[0], sems[1]
        x, y, c = _mesh_pos()
        me = 2 * x + y
        for a in range(len(ins)):
            for k, (px, py) in enumerate(_other_chips(x, y)):
                pltpu.make_async_remote_copy(self._piece(a, ins[a], c), self._piece(a, outs[a].at[me], c),
                                             send_sems.at[a, k], recv_sems.at[a, k],
                                             device_id=(px, py, c), device_id_type=MESH).start()

    def finish(self, ins, outs, sems):
        send_sems, recv_sems, fwd_send_sems, fwd_recv_sems = sems
        x, y, c = _mesh_pos()
        sibling = (x, y, 1 - c)
        chips = _other_chips(x, y)
        n = len(ins)
        forwards = []
        for a in range(n):
            for k, (px, py) in enumerate(chips):
                landed = self._piece(a, outs[a].at[2 * px + py], c)
                pltpu.make_async_remote_copy(landed, landed, send_sems.at[a, k], recv_sems.at[a, k],
                                             device_id=(px, py, c), device_id_type=MESH).wait_recv()
                if self.split[a]:
                    fwd = pltpu.make_async_remote_copy(landed, landed, fwd_send_sems.at[a, k], fwd_recv_sems.at[a, k],
                                                       device_id=sibling, device_id_type=MESH)
                    fwd.start()
                    forwards.append(fwd)
        for a in range(n):
            if self.split[a]:
                for k, (px, py) in enumerate(chips):
                    other = self._piece(a, outs[a].at[2 * px + py], 1 - c)
                    pltpu.make_async_remote_copy(other, other, fwd_send_sems.at[a, k], fwd_recv_sems.at[a, k],
                                                 device_id=sibling, device_id_type=MESH).wait_recv()
        for a in range(n):
            for k, (px, py) in enumerate(chips):
                sent = self._piece(a, ins[a], c)
                pltpu.make_async_remote_copy(sent, sent, send_sems.at[a, k], recv_sems.at[a, k],
                                             device_id=(px, py, c), device_id_type=MESH).wait_send()
        for fwd in forwards:
            fwd.wait_send()

    def finalize(self, gathered):
        x, y, _ = _mesh_pos()
        return [lax.dynamic_update_index_in_dim(g, s, 2 * x + y, 0) for g, s in zip(gathered, self.shards)]

    def run(self, name):
        n = len(self.shards)

        def body(*refs):
            ins, outs, sems = refs[:n], refs[n:2 * n], refs[2 * n:]
            self.start(ins, outs, sems)
            self.finish(ins, outs, sems)

        gathered = pl.pallas_call(
            body, name=name, in_specs=[ANY] * n, out_specs=[ANY] * n, out_shape=self.out_shape,
            scratch_shapes=self.scratch, compiler_params=pltpu.CompilerParams(has_side_effects=True),
        )(*self.shards)
        return self.finalize(gathered)


def _ffn_weight_grads(hb, dg, du, a, dyb, tag):
    dwg = _matmul_tn(hb, dg, D_MODEL, FF_CHUNK, f"{tag}_dw_gate", stack_n=True)
    dwu = _matmul_tn(hb, du, D_MODEL, FF_CHUNK, f"{tag}_dw_up", stack_n=True)
    dwo = _matmul_tn(a, dyb, FF_CHUNK, D_MODEL, f"{tag}_dw_out")
    return jnp.concatenate([dwg, dwu], axis=0), dwo.reshape(N_CHIPS, D_FF // N_CHIPS, D_MODEL)


def _local_step(x, target, w, late=None):
    g = {}
    acts = []
    w = dict(w)

    def ffn_weights(which, layer):
        return w[f"n{which}"], w[f"win{which}_l{layer}"], w[f"wout{which}_l{layer}"]

    def ffn(xin, which, layer):
        xo, gv, uv, hb = _ffn_fwd(xin, *ffn_weights(which, layer), layer, f"ffn{which}_l{layer}_fwd")
        acts.append((xin, gv, uv, hb))
        return xo

    x1 = ffn(x, 1, 0)
    hb_mix = _rms_fwd_call(x1, w["nmix"], 0, "ab_norm_fwd")
    proj = _matmul(hb_mix, w["wp"], "ab_in_proj")
    qkv = _conv_fwd(proj, w["conv_w"], "dn_conv_fwd")
    (dn_u, dn_w, dn_p, dn_qd, dn_kt, dn_gl, dn_inv), arrived = _dn_prep(qkv, proj, w["hp"], "dn_prep",
                                                                       None if late is None else late[0])
    if late is not None:
        w.update(late[1](arrived))
    o, dn_vn, sall = _dn_scan(dn_u, dn_w, dn_p, dn_qd, dn_kt, dn_gl, "dn_scan")
    mix = _mix_fwd(o, proj, w["dn_norm"], w["sg_norm"], w["sg_w"], w["sg_bt"], "ab_gate_fwd")
    x2 = _matmul(mix, w["wo"], "ab_out_proj", res=x1)
    x3 = ffn(x2, 2, 0)
    x4 = ffn(x3, 1, 1)
    x5 = _pool_fwd(x4, w["nmix"], w["pool_w"], w["pool_scale"], 1, "pool_fwd")
    x6 = ffn(x5, 2, 1)
    loss, dx, g["fn"] = _loss_head(x6, target, w["fn"], "loss_head")

    dn = {1: [None, None], 2: [None, None]}
    dwin = {1: [None, None], 2: [None, None]}
    dwout = {1: [None, None], 2: [None, None]}

    def ffn_back(dxo, which, layer, saved):
        nw, win, wout = ffn_weights(which, layer)
        xin, gv, uv, hb = saved
        tag = f"ffn{which}_l{layer}"
        dxi, dg, du, a, dyb, dnw = _ffn_bwd(dxo, xin, nw, gv, uv, win, wout, layer, f"{tag}_bwd")
        dn[which][layer] = dnw
        dwin[which][layer], dwout[which][layer] = _ffn_weight_grads(hb, dg, du, a, dyb, tag)
        return dxi

    dx = ffn_back(dx, 2, 1, acts[3])
    dx, g["pool_w"], g["pool_scale"], dnmix1 = _pool_bwd(dx, x4, w["nmix"], w["pool_w"], w["pool_scale"], 1, "pool_bwd")
    dx = ffn_back(dx, 1, 1, acts[2])
    dx2 = ffn_back(dx, 2, 0, acts[1])
    dmix = _matmul(dx2, w["wo"], "ab_out_proj_bwd", trans_b=True)
    g["wo"] = _matmul_tn(mix, dx2, D_MODEL, D_MODEL, "ab_out_proj_dw")
    do, dzuv, g["dn_norm"], g["sg_norm"], g["sg_w"], g["sg_bt"] = _mix_bwd(
        dmix, o, proj, w["dn_norm"], w["sg_norm"], w["sg_w"], w["sg_bt"], "ab_gate_bwd")
    dvn, dkt, dgl = _dn_scan_bwd(dn_w, dn_p, dn_qd, dn_kt, dn_gl, dn_vn, sall, do, "dn_scan_bwd")
    dqkv_act, dba, g["hp"] = _dn_prep_bwd(qkv, proj, w["hp"], sall, dn_vn, do, dvn, dkt, dgl, dn_inv, dn_u, dn_w,
                                          "dn_prep_bwd")
    dqkv, g["conv_w"] = _conv_bwd(dqkv_act, proj, w["conv_w"], "dn_conv_bwd")
    dproj = jnp.concatenate([dqkv, dzuv, dba.astype(BF16)], axis=1)
    dh = _matmul(dproj, w["wp"], "ab_in_proj_bwd", trans_b=True)
    g["wp"] = _matmul_tn(hb_mix, dproj, D_MODEL, 640, "ab_in_proj_dw")
    dx1, dnmix0 = _rms_bwd_call(dh, x1, w["nmix"], dx2, 0, "ab_norm_bwd")
    dx0 = ffn_back(dx1, 1, 0, acts[0])

    g["n1"] = jnp.concatenate(dn[1], axis=0)
    g["n2"] = jnp.concatenate(dn[2], axis=0)
    g["nmix"] = jnp.concatenate([dnmix0, dnmix1], axis=0)
    for which in (1, 2):
        g[f"win{which}"] = dwin[which]
        g[f"wout{which}"] = dwout[which]
    return loss, dx0, g


SHARDED = ("ffn1_w_in", "ffn1_w_out", "ffn2_w_in", "ffn2_w_out", "ab_w_in", "ab_w_out", "pool_w", "dn_conv_w", "pool_scale")
REPLICATED = ("ffn_norm1", "mix_norm", "ffn_norm2", "dn_a_log", "dn_dt_bias", "dn_out_norm", "sg_norm", "sg_w", "sg_b", "final_norm")
QKVZ = 4 * DN_HEADS * DN_DIM
N_GATES = 2 * DN_HEADS
IN_PROJ = QKVZ + N_GATES + 2 * SG_GROUPS * SG_DIM


def _shard_pieces(wts, keys):
    out = []
    for n, layer in keys:
        a = wts[n][0 if layer is None else layer]
        a = a[None] if a.ndim == 1 else a
        out.append(a.astype(BF16) if n in MATRICES else a)
    return out


def _replicated_layouts(rep):
    per_layer = lambda a: a.reshape(a.shape[0], 1, D_MODEL)
    w = {"n1": per_layer(rep["ffn_norm1"]), "nmix": per_layer(rep["mix_norm"]), "n2": per_layer(rep["ffn_norm2"])}
    hp = jnp.zeros((8, 128), F32)
    w["hp"] = hp.at[0, DN_HEADS:N_GATES].set(rep["dn_a_log"][0]).at[1, DN_HEADS:N_GATES].set(rep["dn_dt_bias"][0])
    w["dn_norm"] = rep["dn_out_norm"]
    w["sg_norm"] = rep["sg_norm"][0]
    w["sg_w"] = rep["sg_w"][0]
    w["sg_bt"] = jnp.zeros((SG_CHUNK, 128), F32).at[:, :SG_GROUPS].set(rep["sg_b"][0].T)
    w["fn"] = rep["final_norm"].reshape(1, D_MODEL)
    return w


def _layouts_from(gathered):
    w = {}
    for (n, layer), a in gathered.items():
        if n in ("ffn1_w_in", "ffn2_w_in"):
            w[f"win{n[3]}_l{layer}"] = a
        elif n in ("ffn1_w_out", "ffn2_w_out"):
            w[f"wout{n[3]}_l{layer}"] = a
        elif n == "ab_w_in":
            ab_in = jnp.transpose(a, (1, 0, 2)).reshape(D_MODEL, IN_PROJ)
            w["wp"] = jnp.concatenate([ab_in[:, :QKVZ], ab_in[:, QKVZ + N_GATES:], ab_in[:, QKVZ:QKVZ + N_GATES],
                                       jnp.zeros((D_MODEL, PROJ_W - IN_PROJ), ab_in.dtype)], axis=1)
        elif n == "dn_conv_w":
            w["conv_w"] = jnp.transpose(a, (1, 0, 2)).reshape(DN_CONV, 3 * DN_HEADS * DN_DIM)
        elif n == "ab_w_out":
            w["wo"] = a.reshape(D_MODEL, D_MODEL)
        elif n == "pool_w":
            w["pool_w"] = jnp.transpose(a, (1, 0, 2, 3)).reshape(len(POOL_WINDOWS), POOL_DIM, POOL_DIM)
        elif n == "pool_scale":
            w["pool_scale"] = a.reshape(1, D_MODEL)
    return w


def _grads_by_chip(g):
    wp = g["wp"]
    ab_in = jnp.concatenate([wp[:, :QKVZ], wp[:, IN_PROJ - N_GATES:IN_PROJ], wp[:, QKVZ:IN_PROJ - N_GATES]], axis=1)
    nw = len(POOL_WINDOWS)
    one_layer = {
        "ab_w_in": jnp.transpose(ab_in.reshape(D_MODEL, N_CHIPS, IN_PROJ // N_CHIPS), (1, 0, 2)),
        "ab_w_out": g["wo"].reshape(N_CHIPS, D_MODEL // N_CHIPS, D_MODEL),
        "pool_w": jnp.transpose(g["pool_w"].reshape(nw, N_CHIPS, POOL_DIM // N_CHIPS, POOL_DIM), (1, 0, 2, 3)),
        "dn_conv_w": jnp.transpose(g["conv_w"].reshape(DN_CONV, N_CHIPS, -1), (1, 0, 2)),
        "pool_scale": g["pool_scale"].reshape(N_CHIPS, D_MODEL // N_CHIPS),
    }
    sharded = {}
    for n, per_layer in (("ffn1_w_in", g["win1"]), ("ffn1_w_out", g["wout1"]), ("ffn2_w_in", g["win2"]), ("ffn2_w_out", g["wout2"])):
        sharded[n] = [a.reshape(N_CHIPS, -1) for a in per_layer]
    for n, a in one_layer.items():
        halves = a.reshape(N_CHIPS, 2, -1)
        sharded[n] = [halves[:, 0], halves[:, 1]]
    rep = {
        "ffn_norm1": g["n1"], "mix_norm": g["nmix"], "ffn_norm2": g["n2"],
        "dn_a_log": g["hp"][0:1, DN_HEADS:N_GATES], "dn_dt_bias": g["hp"][1:2, DN_HEADS:N_GATES],
        "dn_out_norm": g["dn_norm"], "sg_norm": g["sg_norm"][None], "sg_w": g["sg_w"][None],
        "sg_b": g["sg_bt"][:, :SG_GROUPS].T[None], "final_norm": g["fn"].reshape(D_MODEL),
    }
    return sharded, rep


def _piece_rows(n_elems):
    rows = -(-n_elems // PACK_LANES)
    return -(-rows // 8) * 8


def _half_sizes(shapes):
    sizes = []
    for n in SHARDED:
        size = 1
        for d in shapes[n]:
            size *= d
        sizes.append(size // 2)
    return sizes


def _pack_rows(shapes):
    rows = sum(_piece_rows(s) for s in _half_sizes(shapes))
    return -(-rows // PACK_ROW_BLOCK) * PACK_ROW_BLOCK


def _pack_by_half(sharded, shapes):
    rows = _pack_rows(shapes)
    pieces = []
    for half in range(2):
        used = 0
        for n in SHARDED:
            flat = sharded[n][half]
            pr = _piece_rows(flat.shape[1])
            flat = jnp.pad(flat, ((0, 0), (0, pr * PACK_LANES - flat.shape[1])))
            pieces.append(flat.reshape(N_CHIPS, pr, PACK_LANES))
            used += pr
        pieces.append(jnp.zeros((N_CHIPS, rows - used, PACK_LANES), F32))
    return jnp.concatenate(pieces, axis=1).reshape(N_CHIPS, 2, rows, PACK_LANES)


def _unpack_halves(mine, other, core, shapes):
    out, off = {}, 0
    for n, half in zip(SHARDED, _half_sizes(shapes)):
        pr = _piece_rows(half)
        a, b = mine[off:off + pr], other[off:off + pr]
        both = jnp.stack([jnp.where(core == 0, a, b), jnp.where(core == 0, b, a)])
        out[n] = both.reshape(2, -1)[:, :half].reshape(shapes[n])
        off += pr
    return out


def _swap_with_sibling(pack, name):
    nchip, _, rows, lanes = pack.shape

    def body(pack_ref, recv_ref, send_sem, recv_sem):
        x, y, c = _mesh_pos()
        cp = pltpu.make_async_remote_copy(pack_ref.at[:, 1 - c], recv_ref, send_sem, recv_sem,
                                          device_id=(x, y, 1 - c), device_id_type=MESH)
        cp.start()
        cp.wait()

    return pl.pallas_call(
        body, name=name, in_specs=[ANY], out_specs=ANY,
        out_shape=jax.ShapeDtypeStruct((nchip, rows, lanes), pack.dtype),
        scratch_shapes=[pltpu.SemaphoreType.DMA, pltpu.SemaphoreType.DMA],
        compiler_params=pltpu.CompilerParams(has_side_effects=True),
    )(pack)


def _add_pair(pack, recv, core, name):
    nchip, _, rows, lanes = pack.shape

    def body(c_ref, a_ref, b_ref, o32_ref, o16_ref):
        s = a_ref[...] + b_ref[...]
        o32_ref[...] = s
        o16_ref[...] = _bf(s)

    blk = pl.BlockSpec((None, PACK_ROW_BLOCK, lanes), lambda p, i, c: (p, i, 0))
    return pl.pallas_call(
        body, name=name,
        grid_spec=pltpu.PrefetchScalarGridSpec(
            num_scalar_prefetch=1, grid=(nchip, rows // PACK_ROW_BLOCK),
            in_specs=[pl.BlockSpec((None, None, PACK_ROW_BLOCK, lanes), lambda p, i, c: (p, c[0], i, 0)), blk],
            out_specs=[blk, blk]),
        out_shape=[jax.ShapeDtypeStruct((nchip, rows, lanes), F32), jax.ShapeDtypeStruct((nchip, rows, lanes), BF16)],
        compiler_params=_params(("parallel", "parallel")),
    )(core, pack, recv)


def _scatter_to_chips(part16, name):
    nchip, rows, lanes = part16.shape

    def body(src_ref, recv_ref, send_sems, recv_sems):
        x, y, c = _mesh_pos()
        copies = []
        for k, (px, py) in enumerate(_other_chips(x, y)):
            cp = pltpu.make_async_remote_copy(src_ref.at[2 * px + py], recv_ref.at[k], send_sems.at[k], recv_sems.at[k],
                                              device_id=(px, py, c), device_id_type=MESH)
            cp.start()
            copies.append(cp)
        for cp in copies:
            cp.wait()

    return pl.pallas_call(
        body, name=name, in_specs=[ANY], out_specs=ANY,
        out_shape=jax.ShapeDtypeStruct((nchip - 1, rows, lanes), part16.dtype),
        scratch_shapes=[pltpu.SemaphoreType.DMA((nchip - 1,)), pltpu.SemaphoreType.DMA((nchip - 1,))],
        compiler_params=pltpu.CompilerParams(has_side_effects=True),
    )(part16)


def _sum_chips(part32, recv16, chip, name):
    nchip, rows, lanes = part32.shape

    def body(p_ref, own_ref, r_ref, o_ref):
        s = own_ref[...]
        for k in range(nchip - 1):
            s = s + r_ref[k].astype(F32)
        o_ref[...] = s

    return pl.pallas_call(
        body, name=name,
        grid_spec=pltpu.PrefetchScalarGridSpec(
            num_scalar_prefetch=1, grid=(rows // PACK_ROW_BLOCK,),
            in_specs=[pl.BlockSpec((None, PACK_ROW_BLOCK, lanes), lambda i, p: (p[0], i, 0)),
                      pl.BlockSpec((nchip - 1, PACK_ROW_BLOCK, lanes), lambda i, p: (0, i, 0))],
            out_specs=pl.BlockSpec((PACK_ROW_BLOCK, lanes), lambda i, p: (i, 0))),
        out_shape=jax.ShapeDtypeStruct((rows, lanes), F32),
        compiler_params=_params(("parallel",)),
    )(chip, part32, recv16)


def _share_with_sibling(half, name):
    rows, lanes = half.shape

    def body(h_ref, other_ref, send_sem, recv_sem):
        x, y, c = _mesh_pos()
        cp = pltpu.make_async_remote_copy(h_ref, other_ref, send_sem, recv_sem,
                                          device_id=(x, y, 1 - c), device_id_type=MESH)
        cp.start()
        cp.wait()

    return pl.pallas_call(
        body, name=name, in_specs=[ANY], out_specs=ANY,
        out_shape=jax.ShapeDtypeStruct((rows, lanes), half.dtype),
        scratch_shapes=[pltpu.SemaphoreType.DMA, pltpu.SemaphoreType.DMA],
        compiler_params=pltpu.CompilerParams(has_side_effects=True),
    )(half)


def _reduce_sharded(sharded, shapes):
    x, y, c = _mesh_pos()
    core = jnp.reshape(c, (1,)).astype(jnp.int32)
    chip = jnp.reshape(2 * x + y, (1,)).astype(jnp.int32)
    pack = _pack_by_half(sharded, shapes)
    recv = _swap_with_sibling(pack, "grad_pair_swap")
    part32, part16 = _add_pair(pack, recv, core, "grad_pair_add")
    recv16 = _scatter_to_chips(part16, "grad_chip_scatter")
    half = _sum_chips(part32, recv16, chip, "grad_chip_sum")
    other = _share_with_sibling(half, "grad_pair_share")
    return _unpack_halves(half, other, c, shapes)


def _pack_small(vals):
    parts = []
    for n in REPLICATED:
        flat = vals[n].reshape(-1)
        rows = -(-flat.shape[0] // 128)
        rows = -(-rows // 8) * 8
        parts.append(jnp.pad(flat, (0, rows * 128 - flat.shape[0])).reshape(rows, 128))
    return jnp.concatenate(parts, axis=0)


def _unpack_small(pack, like):
    out, off = {}, 0
    for n in REPLICATED:
        size = like[n].size
        rows = -(-size // 128)
        rows = -(-rows // 8) * 8
        out[n] = pack[off:off + rows].reshape(-1)[:size].reshape(like[n].shape)
        off += rows
    return out


def _all_to_all_small(pack, name):
    rows, lanes = pack.shape
    flips = [(dx, dy, dc) for dx in (0, 1) for dy in (0, 1) for dc in (0, 1)][1:]

    def body(src_ref, out_ref, send_sems, recv_sems, local_sem):
        x, y, c = _mesh_pos()
        me = 4 * x + 2 * y + c
        loc = pltpu.make_async_copy(src_ref, out_ref.at[me], local_sem)
        loc.start()
        copies = []
        for k, (dx, dy, dc) in enumerate(flips):
            peer = (x ^ dx, y ^ dy, c ^ dc)
            cp = pltpu.make_async_remote_copy(src_ref, out_ref.at[me], send_sems.at[k], recv_sems.at[k],
                                              device_id=peer, device_id_type=MESH)
            cp.start()
            copies.append(cp)
        for k, (dx, dy, dc) in enumerate(flips):
            peer = (x ^ dx, y ^ dy, c ^ dc)
            pltpu.make_async_remote_copy(src_ref, out_ref.at[4 * peer[0] + 2 * peer[1] + peer[2]], send_sems.at[k],
                                         recv_sems.at[k], device_id=peer, device_id_type=MESH).wait_recv()
        for cp in copies:
            cp.wait_send()
        loc.wait()

    return pl.pallas_call(
        body, name=name, in_specs=[ANY], out_specs=ANY,
        out_shape=jax.ShapeDtypeStruct((8, rows, lanes), pack.dtype),
        scratch_shapes=[pltpu.SemaphoreType.DMA((7,)), pltpu.SemaphoreType.DMA((7,)), pltpu.SemaphoreType.DMA],
        compiler_params=pltpu.CompilerParams(has_side_effects=True),
    )(pack)


def _sum_devices(stack, name):
    ndev, rows, lanes = stack.shape

    def body(s_ref, o_ref):
        s = s_ref[0]
        for d in range(1, ndev):
            s = s + s_ref[d]
        o_ref[...] = s

    return pl.pallas_call(
        body, name=name, grid=(1,),
        in_specs=[pl.BlockSpec((ndev, rows, lanes), lambda i: (0, 0, 0))],
        out_specs=pl.BlockSpec((rows, lanes), lambda i: (0, 0)),
        out_shape=jax.ShapeDtypeStruct((rows, lanes), F32),
    )(stack)


WEIGHT_ORDER = ("ffn_norm1", "ffn1_w_in", "ffn1_w_out", "mix_norm", "ffn_norm2", "ffn2_w_in", "ffn2_w_out", "ab_w_in",
                "dn_conv_w", "dn_a_log", "dn_dt_bias", "dn_out_norm", "sg_norm", "sg_w", "sg_b", "ab_w_out", "pool_w",
                "pool_scale", "final_norm")
MATRICES = ("ffn1_w_in", "ffn1_w_out", "ffn2_w_in", "ffn2_w_out", "ab_w_in", "ab_w_out", "pool_w")
GATHER_FIRST = (("ffn1_w_in", 0), ("ffn1_w_out", 0), ("ab_w_in", None), ("dn_conv_w", None), ("ab_w_out", None))
GATHER_LATER = (("ffn2_w_in", 0), ("ffn2_w_out", 0), ("ffn1_w_in", 1), ("ffn1_w_out", 1), ("pool_w", None),
                ("pool_scale", None), ("ffn2_w_in", 1), ("ffn2_w_out", 1))


def _as_2d(a):
    return a.reshape(-1, a.shape[-1])


def kernel(x, ffn_norm1, ffn1_w_in, ffn1_w_out, mix_norm, ffn_norm2, ffn2_w_in, ffn2_w_out, ab_w_in, dn_conv_w, dn_a_log, dn_dt_bias, dn_out_norm, sg_norm, sg_w, sg_b, ab_w_out, pool_w, pool_scale, final_norm, loss_target, m_ffn_norm1, m_ffn1_w_in, m_ffn1_w_out, m_mix_norm, m_ffn_norm2, m_ffn2_w_in, m_ffn2_w_out, m_ab_w_in, m_dn_conv_w, m_dn_a_log, m_dn_dt_bias, m_dn_out_norm, m_sg_norm, m_sg_w, m_sg_b, m_ab_w_out, m_pool_w, m_pool_scale, m_final_norm, v_ffn_norm1, v_ffn1_w_in, v_ffn1_w_out, v_mix_norm, v_ffn_norm2, v_ffn2_w_in, v_ffn2_w_out, v_ab_w_in, v_dn_conv_w, v_dn_a_log, v_dn_dt_bias, v_dn_out_norm, v_sg_norm, v_sg_w, v_sg_b, v_ab_w_out, v_pool_w, v_pool_scale, v_final_norm):
    given = dict(locals())
    wts = {n: given[n] for n in WEIGHT_ORDER}
    mom_m = {n: given["m_" + n] for n in WEIGHT_ORDER}
    mom_v = {n: given["v_" + n] for n in WEIGHT_ORDER}

    rep = {n: wts[n] for n in REPLICATED}
    first = _ChipGather(_shard_pieces(wts, GATHER_FIRST), [n in MATRICES for n, _ in GATHER_FIRST])
    w = {**_replicated_layouts(rep), **_layouts_from(dict(zip(GATHER_FIRST, first.run("weight_gather_first"))))}
    later = _ChipGather(_shard_pieces(wts, GATHER_LATER), [n in MATRICES for n, _ in GATHER_LATER])

    loss, dx, g = _local_step(x[0], loss_target[0], w,
                              late=(later, lambda arrived: _layouts_from(dict(zip(GATHER_LATER, arrived)))))
    g_sharded, g_rep = _grads_by_chip(g)

    grads = _reduce_sharded(g_sharded, {n: wts[n].shape for n in SHARDED})
    small = _sum_devices(_all_to_all_small(_pack_small(g_rep), "grad_small_exchange"), "grad_small_sum")
    grads.update(_unpack_small(small, rep))

    delta, new_m, new_v = {}, {}, {}
    for n in SHARDED:
        d, m1, v1 = _adamw(_as_2d(wts[n]), _as_2d(grads[n]), _as_2d(mom_m[n]), _as_2d(mom_v[n]), f"adamw_{n}")
        delta[n], new_m[n], new_v[n] = (t.reshape(wts[n].shape) for t in (d, m1, v1))
    d, m1, v1 = _adamw(_pack_small(rep), small, _pack_small({n: mom_m[n] for n in REPLICATED}),
                       _pack_small({n: mom_v[n] for n in REPLICATED}), "adamw_replicated")
    for tgt, packed in ((delta, d), (new_m, m1), (new_v, v1)):
        tgt.update(_unpack_small(packed, rep))

    total = lax.psum(loss[0, 0], ("x", "y", "c"))
    outs = [total, dx[None]]
    for group in (grads, delta, new_m, new_v):
        outs.extend(group[n] for n in WEIGHT_ORDER)
    return tuple(outs)
```

```python
import functools

import jax
import jax.numpy as jnp
from jax import lax
from jax.experimental import pallas as pl
from jax.experimental.pallas import tpu as pltpu

F32, BF16 = jnp.float32, jnp.bfloat16
NORM_EPS = 1e-6
D_MODEL = 1024
D_FF = 2816
N_CHIPS = 4
FF_CHUNK = 2 * D_FF // N_CHIPS
DN_HEADS, DN_DIM, DN_CHUNK, DN_CONV = 4, 128, 64, 4
DN_BLOCK = 2 * DN_CHUNK
DN_PREP_CHUNKS = 4
DN_SCAN_CHUNKS = 8
SG_GROUPS, SG_DIM, SG_CHUNK = 4, 128, 128
POOL_WINDOWS = (2, 4, 8, 16)
POOL_DIM = 256
POOL_HALO = 16
CONV_HALO = 8
PROJ_W = 3200
BA_BLOCK = 3072 // 128
ADAM_LR, ADAM_B1, ADAM_B2, ADAM_EPS, ADAM_WD, ADAM_STEP = 0.001, 0.9, 0.999, 1e-08, 0.01, 10
VMEM_BIG = 52 * 1024 * 1024
FFN_FWD_ROWS = 512
FFN_BWD_ROWS = 256
PACK_LANES = 1024
PACK_ROW_BLOCK = 256
MESH = pl.DeviceIdType.MESH
HI = lax.Precision.HIGHEST
ANY = pl.BlockSpec(memory_space=pl.ANY)


def _params(sem=None, vmem=None):
    return pltpu.CompilerParams(dimension_semantics=sem, vmem_limit_bytes=vmem)


def _dot(a, b):
    return jnp.dot(a, b, preferred_element_type=F32)


def _dot_nt(a, b):
    return lax.dot_general(a, b, (((1,), (1,)), ((), ())), preferred_element_type=F32)


def _dot_tn(a, b):
    return lax.dot_general(a, b, (((0,), (0,)), ((), ())), preferred_element_type=F32)


def _dot_hi(a, b):
    return jnp.dot(a, b, preferred_element_type=F32, precision=HI)


def _dot_mid(a, b):
    return jnp.dot(a, b, preferred_element_type=F32, precision=lax.Precision.HIGH)


def _bf(a):
    return a.astype(BF16)


def _rms_stats(x):
    r = lax.rsqrt(jnp.mean(x * x, axis=-1, keepdims=True) + NORM_EPS)
    return x * r, r


def _rms_bwd(dh, xhat, r, w):
    dhn = dh * w
    return r * (dhn - xhat * jnp.mean(dhn * xhat, axis=-1, keepdims=True))


def _sigmoid(x):
    return jax.nn.sigmoid(x)


def _silu_grad(x, s):
    return s * (1.0 + x * (1.0 - s))


def _gelu(x):
    return 0.5 * x * (1.0 + lax.erf(x * 0.7071067811865476))


def _gelu_grad(x):
    return 0.5 * (1.0 + lax.erf(x * 0.7071067811865476)) + x * jnp.exp(-0.5 * x * x) * 0.3989422804014327


def _softplus(x):
    return jnp.maximum(x, 0.0) + jnp.log(1.0 + jnp.exp(-jnp.abs(x)))


def _tile(n, pref):
    t = min(n, pref)
    assert n % t == 0, (n, t)
    return t


def _ffn_weight_specs():
    once = pl.Buffered(1)
    return [pl.BlockSpec((N_CHIPS, D_MODEL, FF_CHUNK), lambda i: (0, 0, 0), pipeline_mode=once),
            pl.BlockSpec((N_CHIPS, D_FF // N_CHIPS, D_MODEL), lambda i: (0, 0, 0), pipeline_mode=once)]


def _ffn_fwd(x, nw, win, wout, layer, name):
    T, D = x.shape
    tm = _tile(T, FFN_FWD_ROWS)
    nj = N_CHIPS // 2

    def body(x_ref, n_ref, win_ref, wo_ref, xo_ref, g_ref, u_ref, hb_ref):
        xv = x_ref[...]
        xhat, _ = _rms_stats(xv)
        h = _bf(xhat * n_ref[...])
        hb_ref[...] = h
        acc = None
        for j in range(nj):
            cols = slice(j * FF_CHUNK, (j + 1) * FF_CHUNK)
            g = _dot(h, win_ref[j])
            u = _dot(h, win_ref[nj + j])
            g_ref[:, cols] = _bf(g)
            u_ref[:, cols] = _bf(u)
            part = _dot(_bf(g * _sigmoid(g) * u), wo_ref[2 * j:2 * j + 2].reshape(FF_CHUNK, D))
            acc = part if acc is None else acc + part
        xo_ref[...] = xv + 0.5 * acc

    row = pl.BlockSpec((tm, D), lambda i: (i, 0))
    wide = pl.BlockSpec((tm, D_FF), lambda i: (i, 0))
    return pl.pallas_call(
        body, name=name, grid=(T // tm,),
        in_specs=[row, pl.BlockSpec((None, 1, D), lambda i: (layer, 0, 0))] + _ffn_weight_specs(),
        out_specs=[row, wide, wide, row],
        out_shape=[jax.ShapeDtypeStruct((T, D), F32), jax.ShapeDtypeStruct((T, D_FF), BF16),
                   jax.ShapeDtypeStruct((T, D_FF), BF16), jax.ShapeDtypeStruct((T, D), BF16)],
        compiler_params=_params(("parallel",), VMEM_BIG),
    )(x, nw, win, wout)


def _ffn_bwd(dxo, x, nw, g, u, win, wout, layer, name, exchange=None):
    T, D = x.shape
    tm = _tile(T, FFN_BWD_ROWS)
    nj = N_CHIPS // 2

    def body(dxo_ref, x_ref, n_ref, g_ref, u_ref, win_ref, wo_ref, dx_ref, dg_ref, du_ref, a_ref, dyb_ref, dn_ref):
        @pl.when(pl.program_id(0) == 0)
        def _():
            dn_ref[...] = jnp.zeros_like(dn_ref)

        dxo = dxo_ref[...]
        dyb = _bf(0.5 * dxo)
        dyb_ref[...] = dyb
        dh = None
        for j in range(nj):
            cols = slice(j * FF_CHUNK, (j + 1) * FF_CHUNK)
            da = _dot_nt(dyb, wo_ref[2 * j:2 * j + 2].reshape(FF_CHUNK, D))
            gv = g_ref[:, cols].astype(F32)
            uv = u_ref[:, cols].astype(F32)
            sg = _sigmoid(gv)
            sl = gv * sg
            dgb = _bf(da * uv * _silu_grad(gv, sg))
            dub = _bf(da * sl)
            a_ref[:, cols] = _bf(sl * uv)
            dg_ref[:, cols] = dgb
            du_ref[:, cols] = dub
            part = _dot_nt(dgb, win_ref[j]) + _dot_nt(dub, win_ref[nj + j])
            dh = part if dh is None else dh + part
        xhat, r = _rms_stats(x_ref[...])
        dx_ref[...] = dxo + _rms_bwd(dh, xhat, r, n_ref[...])
        dn_ref[...] += jnp.sum(dh * xhat, axis=0, keepdims=True)

    row = pl.BlockSpec((tm, D), lambda i: (i, 0))
    wide = pl.BlockSpec((tm, D_FF), lambda i: (i, 0))
    return _call_with_exchange(
        body, exchange, name=name, steps=T // tm, vmem=VMEM_BIG,
        in_specs=[row, row, pl.BlockSpec((None, 1, D), lambda i: (layer, 0, 0)), wide, wide] + _ffn_weight_specs(),
        out_specs=[row, wide, wide, wide, row, pl.BlockSpec((1, D), lambda i: (0, 0))],
        out_shape=[jax.ShapeDtypeStruct((T, D), F32), jax.ShapeDtypeStruct((T, D_FF), BF16),
                   jax.ShapeDtypeStruct((T, D_FF), BF16), jax.ShapeDtypeStruct((T, D_FF), BF16),
                   jax.ShapeDtypeStruct((T, D), BF16), jax.ShapeDtypeStruct((1, D), F32)],
        args=(dxo, x, nw, g, u, win, wout))


def _matmul_tn(a, b, bm, bn, name, stack_n=False):
    T, M = a.shape
    N = b.shape[1]
    tk = _tile(T, 1024)
    bm, bn = _tile(M, bm), _tile(N, bn)

    def body(a_ref, b_ref, o_ref):
        @pl.when(pl.program_id(2) == 0)
        def _():
            o_ref[...] = jnp.zeros_like(o_ref)

        o_ref[...] += _dot_tn(_bf(a_ref[...]), _bf(b_ref[...]))

    if stack_n:
        out_spec = pl.BlockSpec((None, bm, bn), lambda m, n, k: (n, m, 0))
        out_shape = jax.ShapeDtypeStruct((N // bn, M, bn), F32)
    else:
        out_spec = pl.BlockSpec((bm, bn), lambda m, n, k: (m, n))
        out_shape = jax.ShapeDtypeStruct((M, N), F32)
    return pl.pallas_call(
        body, name=name, grid=(M // bm, N // bn, T // tk),
        in_specs=[pl.BlockSpec((tk, bm), lambda m, n, k: (k, m)),
                  pl.BlockSpec((tk, bn), lambda m, n, k: (k, n))],
        out_specs=out_spec, out_shape=out_shape,
        compiler_params=_params(("parallel", "parallel", "arbitrary"), VMEM_BIG),
    )(a, b)


def _matmul(a, b, name, trans_b=False, res=None, out_dtype=F32):
    T, K = a.shape
    N = b.shape[0] if trans_b else b.shape[1]
    tm = _tile(T, 512)

    def body(*refs):
        a_ref, b_ref = refs[0], refs[1]
        o_ref = refs[-1]
        av, bv = _bf(a_ref[...]), _bf(b_ref[...])
        acc = _dot_nt(av, bv) if trans_b else _dot(av, bv)
        if res is not None:
            acc = acc + refs[2][...]
        o_ref[...] = acc.astype(out_dtype)

    in_specs = [pl.BlockSpec((tm, K), lambda i: (i, 0)), pl.BlockSpec(b.shape, lambda i: (0, 0))]
    args = [a, b]
    if res is not None:
        in_specs.append(pl.BlockSpec((tm, N), lambda i: (i, 0)))
        args.append(res)
    return pl.pallas_call(
        body, name=name, grid=(T // tm,), in_specs=in_specs,
        out_specs=pl.BlockSpec((tm, N), lambda i: (i, 0)),
        out_shape=jax.ShapeDtypeStruct((T, N), out_dtype),
        compiler_params=_params(("parallel",), VMEM_BIG),
    )(*args)


def _rms_fwd_call(x, nw, layer, name):
    T, D = x.shape
    tm = _tile(T, 512)

    def body(x_ref, n_ref, o_ref):
        xhat, _ = _rms_stats(x_ref[...])
        o_ref[...] = _bf(xhat * n_ref[...])

    return pl.pallas_call(
        body, name=name, grid=(T // tm,),
        in_specs=[pl.BlockSpec((tm, D), lambda i: (i, 0)), pl.BlockSpec((None, 1, D), lambda i: (layer, 0, 0))],
        out_specs=pl.BlockSpec((tm, D), lambda i: (i, 0)),
        out_shape=jax.ShapeDtypeStruct((T, D), BF16),
        compiler_params=_params(("parallel",)),
    )(x, nw)


def _rms_bwd_call(dh, x, nw, dres, layer, name):
    T, D = x.shape
    tm = _tile(T, 512)

    def body(dh_ref, x_ref, n_ref, dr_ref, dx_ref, dn_ref):
        @pl.when(pl.program_id(0) == 0)
        def _():
            dn_ref[...] = jnp.zeros_like(dn_ref)

        xhat, r = _rms_stats(x_ref[...])
        dh_v = dh_ref[...]
        dx_ref[...] = dr_ref[...] + _rms_bwd(dh_v, xhat, r, n_ref[...])
        dn_ref[...] += jnp.sum(dh_v * xhat, axis=0, keepdims=True)

    row = pl.BlockSpec((tm, D), lambda i: (i, 0))
    return pl.pallas_call(
        body, name=name, grid=(T // tm,),
        in_specs=[row, row, pl.BlockSpec((None, 1, D), lambda i: (layer, 0, 0)), row],
        out_specs=[row, pl.BlockSpec((1, D), lambda i: (0, 0))],
        out_shape=[jax.ShapeDtypeStruct((T, D), F32), jax.ShapeDtypeStruct((1, D), F32)],
        compiler_params=_params(("arbitrary",)),
    )(dh, x, nw, dres)


def _shift_rows(x, s):
    n = x.shape[0]
    s = s % n
    return x if s == 0 else pltpu.roll(x, s, 0)


def _conv_fwd(proj, conv_w, name):
    T = proj.shape[0]
    C = 3 * DN_HEADS * DN_DIM
    cb = 512
    tm = _tile(T, 512)
    hb = tm // CONV_HALO

    def body(x_ref, xp_ref, w_ref, o_ref):
        i = pl.program_id(1)
        prev = jnp.where(i == 0, 0.0, xp_ref[...])
        ext = jnp.concatenate([prev, x_ref[...]], axis=0)
        w = w_ref[...]
        y = ext * w[DN_CONV - 1:DN_CONV, :]
        for k in range(DN_CONV - 1):
            y = y + _shift_rows(ext, DN_CONV - 1 - k) * w[k:k + 1, :]
        y = y[CONV_HALO:, :]
        o_ref[...] = y * _sigmoid(y)

    return pl.pallas_call(
        body, name=name, grid=(C // cb, T // tm),
        in_specs=[pl.BlockSpec((tm, cb), lambda c, i: (i, c)),
                  pl.BlockSpec((CONV_HALO, cb), lambda c, i: (jnp.maximum(i * hb - 1, 0), c)),
                  pl.BlockSpec((DN_CONV, cb), lambda c, i: (0, c))],
        out_specs=pl.BlockSpec((tm, cb), lambda c, i: (i, c)),
        out_shape=jax.ShapeDtypeStruct((T, C), F32),
        compiler_params=_params(("parallel", "parallel")),
    )(proj, proj, conv_w)


def _conv_bwd(dy, proj, conv_w, name):
    T = proj.shape[0]
    C = 3 * DN_HEADS * DN_DIM
    cb = 512
    tm = _tile(T, 512)
    hb = tm // CONV_HALO
    nt = T // tm

    def body(x_ref, xp_ref, xn_ref, dy_ref, dyn_ref, w_ref, dx_ref, dw_ref):
        i = pl.program_id(1)

        @pl.when(i == 0)
        def _():
            dw_ref[...] = jnp.zeros_like(dw_ref)

        prev = jnp.where(i == 0, 0.0, xp_ref[...])
        ext = jnp.concatenate([prev, x_ref[...], xn_ref[...]], axis=0)
        dy_ext = jnp.concatenate([jnp.zeros((CONV_HALO, cb), F32), dy_ref[...],
                                  jnp.where(i == nt - 1, 0.0, dyn_ref[...])], axis=0)
        w = w_ref[...]
        shifted = [_shift_rows(ext, DN_CONV - 1 - k) for k in range(DN_CONV)]
        y = shifted[0] * w[0:1, :]
        for k in range(1, DN_CONV):
            y = y + shifted[k] * w[k:k + 1, :]
        s = _sigmoid(y)
        dpre = dy_ext * _silu_grad(y, s)
        dx = dpre * w[DN_CONV - 1:DN_CONV, :]
        for k in range(DN_CONV - 1):
            dx = dx + _shift_rows(dpre, -(DN_CONV - 1 - k)) * w[k:k + 1, :]
        dx_ref[...] = _bf(dx[CONV_HALO:CONV_HALO + tm, :])
        rows = [jnp.sum((dpre * shifted[k])[CONV_HALO:CONV_HALO + tm, :], axis=0, keepdims=True) for k in range(DN_CONV)]
        dw_ref[...] += jnp.concatenate(rows, axis=0)

    last_halo = T // CONV_HALO - 1
    return pl.pallas_call(
        body, name=name, grid=(C // cb, nt),
        in_specs=[pl.BlockSpec((tm, cb), lambda c, i: (i, c)),
                  pl.BlockSpec((CONV_HALO, cb), lambda c, i: (jnp.maximum(i * hb - 1, 0), c)),
                  pl.BlockSpec((CONV_HALO, cb), lambda c, i: (jnp.minimum((i + 1) * hb, last_halo), c)),
                  pl.BlockSpec((tm, cb), lambda c, i: (i, c)),
                  pl.BlockSpec((CONV_HALO, cb), lambda c, i: (jnp.minimum((i + 1) * hb, last_halo), c)),
                  pl.BlockSpec((DN_CONV, cb), lambda c, i: (0, c))],
        out_specs=[pl.BlockSpec((tm, cb), lambda c, i: (i, c)),
                   pl.BlockSpec((DN_CONV, cb), lambda c, i: (0, c))],
        out_shape=[jax.ShapeDtypeStruct((T, C), BF16), jax.ShapeDtypeStruct((DN_CONV, C), F32)],
        compiler_params=_params(("parallel", "arbitrary")),
    )(proj, proj, proj, dy, dy, conv_w)


def _unit_lower_inverse(low, eye):
    p1 = -low
    p2 = _dot_mid(p1, p1)
    p4 = _dot_mid(p2, p2)
    a = eye + p1 + p2 + _dot_mid(p1, p2)
    p8 = _dot_mid(p4, p4)
    p16 = _dot_mid(p8, p8)
    b = eye + p4 + p8 + _dot_mid(p4, p8)
    p32 = _dot_mid(p16, p16)
    ab = _dot_mid(a, b)
    c = eye + p16 + p32 + _dot_mid(p16, p32)
    return _dot_mid(ab, c)


def _l2_unit(x):
    r = lax.rsqrt(jnp.sum(x * x, axis=-1, keepdims=True) + NORM_EPS)
    return x * r, r


class _BlockMasks:
    def __init__(self):
        n = DN_BLOCK
        row = lax.broadcasted_iota(jnp.int32, (n, n), 0)
        col = lax.broadcasted_iota(jnp.int32, (n, n), 1)
        same = (row // DN_CHUNK) == (col // DN_CHUNK)
        self.lower, self.strict_lower = same & (row >= col), same & (row > col)
        self.upper, self.strict_upper = same & (row <= col), same & (row < col)
        self.eye = (row == col).astype(F32)
        self.first = lax.broadcasted_iota(jnp.int32, (n, 1), 0) < DN_CHUNK


def _dn_gates(ba, hp):
    coef = -jnp.exp(hp[0:1, :])
    pre = ba + hp[1:2, :]
    return _sigmoid(ba), coef * _softplus(pre), coef, pre


def _dn_block_gates(mk, ba, hp):
    assert DN_BLOCK == 2 * DN_CHUNK
    beta_t, graw_t, coef, pre = _dn_gates(ba, hp)
    gcum_t = _dot_hi(mk.lower.astype(F32), graw_t)
    gl_t = jnp.where(mk.first, gcum_t[DN_CHUNK - 1:DN_CHUNK, :], gcum_t[DN_BLOCK - 1:DN_BLOCK, :])
    return beta_t, gcum_t, gl_t, graw_t, coef, pre


def _dn_local(mk, qraw, kraw, bc, gc, gl):
    f = {}
    f["qn"], f["rq"] = _l2_unit(qraw)
    qh = f["qn"] * (DN_DIM ** -0.5)
    kh, f["rk"] = _l2_unit(kraw)
    gr = jnp.broadcast_to(gc, (DN_BLOCK, DN_BLOCK)).T
    dec = jnp.where(mk.lower, jnp.exp(jnp.where(mk.lower, gc - gr, 0.0)), 0.0)
    kb = kh * bc
    mkk = _dot_nt(_bf(kb), _bf(kh))
    eg = jnp.exp(gc)
    mqk = _dot_nt(_bf(qh), _bf(kh))
    etl = jnp.exp(gl - gc)
    f.update(qh=qh, kh=kh, gr=gr, dec=dec, kb=kb, mkk=mkk, eg=eg, mqk=mqk, attn=mqk * dec, etl=etl, qd=qh * eg, kt=kh * etl)
    return f


def _dn_specs(rows, rev=None):
    at = (lambda n: n) if rev is None else rev
    hw = DN_HEADS * DN_DIM
    return dict(
        qkv=[pl.BlockSpec((rows, hw), lambda n, j=j: (at(n), j)) for j in range(3)],
        ba=pl.BlockSpec((rows, 128), lambda n: (at(n), BA_BLOCK)),
        hp=pl.BlockSpec((8, 128), lambda n: (0, 0)),
        tok=pl.BlockSpec((rows, hw), lambda n: (at(n), 0)),
        attn=pl.BlockSpec((rows, DN_HEADS * DN_CHUNK), lambda n: (at(n), 0)),
        gate=pl.BlockSpec((rows // DN_CHUNK, 8, 128), lambda n: (at(n), 0, 0)),
        state=pl.BlockSpec((rows // DN_CHUNK, DN_HEADS, DN_DIM, DN_DIM), lambda n: (at(n), 0, 0, 0)),
    )


def _call_with_exchange(body, exchange, *, name, steps, in_specs, out_specs, out_shape, args, vmem=None):
    if exchange is None:
        res = pl.pallas_call(body, name=name, grid=(steps,), in_specs=in_specs, out_specs=out_specs, out_shape=out_shape,
                             compiler_params=_params(("arbitrary",), vmem))(*args)
        return list(res), None
    n_in, n_out, m = len(in_specs), len(out_specs), len(exchange.operands)

    def hosted(*refs):
        ins, ex_ins = refs[:n_in], refs[n_in:n_in + m]
        outs, ex_outs = refs[n_in + m:n_in + m + n_out], refs[n_in + m + n_out:n_in + 2 * m + n_out]
        sems = refs[n_in + 2 * m + n_out:]

        @pl.when(pl.program_id(0) == 0)
        def _():
            exchange.start(ex_ins, ex_outs, sems)

        body(*ins, *outs)

        @pl.when(pl.program_id(0) == steps - 1)
        def _():
            exchange.finish(ex_ins, ex_outs, sems)

    res = pl.pallas_call(
        hosted, name=name, grid=(steps,), in_specs=list(in_specs) + [ANY] * m, out_specs=list(out_specs) + [ANY] * m,
        out_shape=list(out_shape) + list(exchange.out_shape), scratch_shapes=exchange.scratch,
        compiler_params=pltpu.CompilerParams(dimension_semantics=("arbitrary",), vmem_limit_bytes=vmem, has_side_effects=True),
    )(*args, *exchange.operands)
    return list(res[:n_out]), exchange.finalize(list(res[n_out:]))


def _dn_prep(qkv, proj, hp, name, exchange=None):
    T = qkv.shape[0]
    n_chunks = T // DN_CHUNK
    blocks = max(1, min(DN_PREP_CHUNKS, n_chunks) * DN_CHUNK // DN_BLOCK)
    group = blocks * DN_BLOCK // DN_CHUNK
    rows = blocks * DN_BLOCK
    hw = DN_HEADS * DN_DIM

    def body(q_ref, k_ref, v_ref, ba_ref, hp_ref, u_ref, w_ref, p_ref, qd_ref, kt_ref, gl_ref, inv_ref):
        mk = _BlockMasks()
        hp_v = hp_ref[...]
        for j in range(blocks):
            rs = slice(j * DN_BLOCK, (j + 1) * DN_BLOCK)
            beta_t, gcum_t, gl_t = _dn_block_gates(mk, ba_ref[rs, :], hp_v)[:3]
            for c in range(DN_BLOCK // DN_CHUNK):
                gl_ref[j * (DN_BLOCK // DN_CHUNK) + c] = jnp.broadcast_to(gl_t[c * DN_CHUNK:c * DN_CHUNK + 1, :], (8, 128))
            for h in range(DN_HEADS):
                sl = slice(h * DN_DIM, (h + 1) * DN_DIM)
                gate = slice(DN_HEADS + h, DN_HEADS + h + 1)
                bc = beta_t[:, h:h + 1]
                f = _dn_local(mk, q_ref[rs, sl], k_ref[rs, sl], bc, gcum_t[:, gate], gl_t[:, gate])
                inv = _unit_lower_inverse(jnp.where(mk.strict_lower, f["mkk"] * f["dec"], 0.0), mk.eye)
                inv_ref[rs, sl] = inv
                sol = _dot_mid(inv, jnp.concatenate([v_ref[rs, sl] * bc, f["kb"] * f["eg"]], axis=1))
                u_ref[rs, sl] = sol[:, :DN_DIM]
                w_ref[rs, sl] = _bf(sol[:, DN_DIM:])
                for c in range(DN_BLOCK // DN_CHUNK):
                    cr = slice(c * DN_CHUNK, (c + 1) * DN_CHUNK)
                    p_ref[j * DN_BLOCK + c * DN_CHUNK:j * DN_BLOCK + (c + 1) * DN_CHUNK, h * DN_CHUNK:(h + 1) * DN_CHUNK] = _bf(f["attn"][cr, cr])
                qd_ref[rs, sl] = _bf(f["qd"])
                kt_ref[rs, sl] = _bf(f["kt"])

    sp = _dn_specs(rows)
    tok16 = jax.ShapeDtypeStruct((T, hw), BF16)
    return _call_with_exchange(
        body, exchange, name=name, steps=n_chunks // group,
        in_specs=sp["qkv"] + [sp["ba"], sp["hp"]],
        out_specs=[sp["tok"], sp["tok"], sp["attn"], sp["tok"], sp["tok"], sp["gate"], sp["tok"]],
        out_shape=[jax.ShapeDtypeStruct((T, hw), F32), tok16, jax.ShapeDtypeStruct((T, DN_HEADS * DN_CHUNK), BF16),
                   tok16, tok16, jax.ShapeDtypeStruct((n_chunks, 8, 128), F32), jax.ShapeDtypeStruct((T, hw), F32)],
        args=(qkv, qkv, qkv, proj, hp))


def _dn_scan(u, w, p, qd, kt, gl, name):
    T = u.shape[0]
    n_chunks = T // DN_CHUNK
    group = min(DN_SCAN_CHUNKS, n_chunks)
    rows = group * DN_CHUNK
    hw = DN_HEADS * DN_DIM

    def body(u_ref, w_ref, p_ref, qd_ref, kt_ref, gl_ref, o_ref, vn_ref, sall_ref, s_s):
        @pl.when(pl.program_id(0) == 0)
        def _():
            s_s[...] = jnp.zeros_like(s_s)

        state = [s_s[h] for h in range(DN_HEADS)]
        for j in range(group):
            rs = slice(j * DN_CHUNK, (j + 1) * DN_CHUNK)
            for h in range(DN_HEADS):
                sl = slice(h * DN_DIM, (h + 1) * DN_DIM)
                sall_ref[j, h] = state[h]
                sb = _bf(state[h])
                vnb = _bf(u_ref[rs, sl] - _dot(w_ref[rs, sl], sb))
                vn_ref[rs, sl] = vnb
                o_ref[rs, sl] = _dot(qd_ref[rs, sl], sb) + _dot(p_ref[rs, h * DN_CHUNK:(h + 1) * DN_CHUNK], vnb)
                egl = jnp.exp(gl_ref[j, 0:1, DN_HEADS + h:DN_HEADS + h + 1])
                state[h] = state[h] * egl + _dot_tn(kt_ref[rs, sl], vnb)
        for h in range(DN_HEADS):
            s_s[h] = state[h]

    sp = _dn_specs(rows)
    return pl.pallas_call(
        body, name=name, grid=(n_chunks // group,),
        in_specs=[sp["tok"], sp["tok"], sp["attn"], sp["tok"], sp["tok"], sp["gate"]],
        out_specs=[sp["tok"], sp["tok"], sp["state"]],
        out_shape=[jax.ShapeDtypeStruct((T, hw), F32), jax.ShapeDtypeStruct((T, hw), BF16),
                   jax.ShapeDtypeStruct((n_chunks, DN_HEADS, DN_DIM, DN_DIM), F32)],
        scratch_shapes=[pltpu.VMEM((DN_HEADS, DN_DIM, DN_DIM), F32)],
        compiler_params=_params(("arbitrary",)),
    )(u, w, p, qd, kt, gl)


def _dn_scan_bwd(w, p, qd, kt, gl, vn, sall, do, name):
    T = w.shape[0]
    n_chunks = T // DN_CHUNK
    group = min(DN_SCAN_CHUNKS, n_chunks)
    rows = group * DN_CHUNK
    hw = DN_HEADS * DN_DIM
    last = n_chunks // group - 1

    def body(w_ref, p_ref, qd_ref, kt_ref, gl_ref, vn_ref, sall_ref, do_ref, dvn_ref, dkt_ref, dgl_ref, ds_s):
        @pl.when(pl.program_id(0) == 0)
        def _():
            ds_s[...] = jnp.zeros_like(ds_s)

        lane = lax.broadcasted_iota(jnp.int32, (8, 128), 1)
        d_state = [ds_s[h] for h in range(DN_HEADS)]
        for j in reversed(range(group)):
            rs = slice(j * DN_CHUNK, (j + 1) * DN_CHUNK)
            dgl_tile = jnp.zeros((8, 128), F32)
            for h in range(DN_HEADS):
                sl = slice(h * DN_DIM, (h + 1) * DN_DIM)
                d_out = _bf(do_ref[rs, sl])
                d_new = d_state[h]
                d_newb = _bf(d_new)
                d_vn = _dot_tn(p_ref[rs, h * DN_CHUNK:(h + 1) * DN_CHUNK], d_out) + _dot(kt_ref[rs, sl], d_newb)
                dvn_ref[rs, sl] = d_vn
                dkt_ref[rs, sl] = _dot_nt(vn_ref[rs, sl], d_newb)
                egl = jnp.exp(gl_ref[j, 0:1, DN_HEADS + h:DN_HEADS + h + 1])
                prod = jnp.sum(d_new * sall_ref[j, h], axis=1, keepdims=True)
                dgl_tile = jnp.where(lane == DN_HEADS + h, jnp.sum(prod, axis=0, keepdims=True) * egl, dgl_tile)
                d_state[h] = d_new * egl + _dot_tn(qd_ref[rs, sl], d_out) - _dot_tn(w_ref[rs, sl], _bf(d_vn))
            dgl_ref[j] = dgl_tile
        for h in range(DN_HEADS):
            ds_s[h] = d_state[h]

    sp = _dn_specs(rows, rev=lambda n: last - n)
    return pl.pallas_call(
        body, name=name, grid=(n_chunks // group,),
        in_specs=[sp["tok"], sp["attn"], sp["tok"], sp["tok"], sp["gate"], sp["tok"], sp["state"], sp["tok"]],
        out_specs=[sp["tok"], sp["tok"], sp["gate"]],
        out_shape=[jax.ShapeDtypeStruct((T, hw), F32), jax.ShapeDtypeStruct((T, hw), F32),
                   jax.ShapeDtypeStruct((n_chunks, 8, 128), F32)],
        scratch_shapes=[pltpu.VMEM((DN_HEADS, DN_DIM, DN_DIM), F32)],
        compiler_params=_params(("arbitrary",)),
    )(w, p, qd, kt, gl, vn, sall, do)


def _dn_prep_bwd(qkv, proj, hp, sall, vn, do, dvn, dkt, dgl, inv, u, w, name, exchange=None):
    T = qkv.shape[0]
    n_chunks = T // DN_CHUNK
    blocks = max(1, min(DN_PREP_CHUNKS, n_chunks) * DN_CHUNK // DN_BLOCK)
    per_block = DN_BLOCK // DN_CHUNK
    group = blocks * per_block
    rows = blocks * DN_BLOCK
    hw = DN_HEADS * DN_DIM
    first_rows, second_rows = slice(0, DN_CHUNK), slice(DN_CHUNK, DN_BLOCK)

    def rowsum(x):
        return jnp.sum(x, axis=1, keepdims=True)

    def by_chunk(x, s0, s1, fn):
        return jnp.concatenate([fn(x[first_rows], s0), fn(x[second_rows], s1)], axis=0)

    def body(q_ref, k_ref, v_ref, ba_ref, hp_ref, sall_ref, vn_ref, do_ref, dvn_ref, dkt_ref, dgl_ref,
             inv_ref, u_ref, w_ref, dqkv_ref, dba_ref, dhp_ref):
        @pl.when(pl.program_id(0) == 0)
        def _():
            dhp_ref[...] = jnp.zeros_like(dhp_ref)

        mk = _BlockMasks()
        hp_v = hp_ref[...]
        total = jnp.zeros((8, 128), F32)
        for j in range(blocks):
            rs = slice(j * DN_BLOCK, (j + 1) * DN_BLOCK)
            total = total + one_block(mk, hp_v, *(r.at[rs, :] for r in (q_ref, k_ref, v_ref, ba_ref)),
                                      sall_ref.at[pl.ds(j * per_block, per_block)],
                                      *(r.at[rs, :] for r in (vn_ref, do_ref, dvn_ref, dkt_ref)),
                                      dgl_ref.at[pl.ds(j * per_block, per_block)],
                                      *(r.at[rs, :] for r in (inv_ref, u_ref, w_ref)),
                                      *(dqkv_ref.at[rs, pl.ds(i * hw, hw)] for i in range(3)), dba_ref.at[rs, :])
        dhp_ref[...] += total

    def one_block(mk, hp_v, q_ref, k_ref, v_ref, ba_ref, state_ref, vn_ref, do_ref, dvn_ref, dkt_ref, dgl_ref,
                  inv_ref, u_ref, w_ref, dq_ref, dk_ref, dv_ref, dba_ref):
        ba = ba_ref[...]
        beta_t, gcum_t, gl_t, graw_t, coef, pre = _dn_block_gates(mk, ba, hp_v)
        lane = lax.broadcasted_iota(jnp.int32, (DN_BLOCK, 128), 1)
        rowi = lax.broadcasted_iota(jnp.int32, (DN_BLOCK, 1), 0)
        dgcum_t = jnp.zeros((DN_BLOCK, 128), F32)
        dbeta_t = jnp.zeros((DN_BLOCK, 128), F32)
        for h in range(DN_HEADS):
            sl = slice(h * DN_DIM, (h + 1) * DN_DIM)
            gate = slice(DN_HEADS + h, DN_HEADS + h + 1)
            gc = gcum_t[:, gate]
            bc = beta_t[:, h:h + 1]
            sb0, sb1 = _bf(state_ref[0, h]), _bf(state_ref[1, h])
            vh = v_ref[:, sl]
            f = _dn_local(mk, q_ref[:, sl], k_ref[:, sl], bc, gc, gl_t[:, gate])
            qh, kh, kb, dec, eg, etl = f["qh"], f["kh"], f["kb"], f["dec"], f["eg"], f["etl"]
            qd, kt = f["qd"], f["kt"]
            qb, kbf, kbb = _bf(qh), _bf(kh), _bf(kb)
            dec_t = jnp.where(mk.upper, jnp.exp(jnp.where(mk.upper, f["gr"] - gc, 0.0)), 0.0)
            mkk_t = f["mkk"].T
            inv_t = inv_ref[:, sl].T
            mqk_t = f["mqk"].T

            d_out = _bf(do_ref[:, sl])
            vnb = vn_ref[:, sl]
            d_qd = by_chunk(d_out, sb0, sb1, _dot_nt)
            d_attn = _dot_nt(d_out, vnb)
            d_attn_t = _dot_nt(vnb, d_out)
            d_vn = dvn_ref[:, sl]
            d_kt = dkt_ref[:, sl]
            d_w = -by_chunk(_bf(d_vn), sb0, sb1, _dot_nt)
            d_rhs = _dot_mid(inv_t, jnp.concatenate([d_vn, d_w], axis=1))
            d_bu, d_bw = d_rhs[:, :DN_DIM], d_rhs[:, DN_DIM:]
            ub, wb, d_bub, d_bwb = _bf(u_ref[:, sl]), w_ref[:, sl], _bf(d_bu), _bf(d_bw)
            d_low = -(_dot_nt(d_bub, ub) + _dot_nt(d_bwb, wb))
            d_low_t = -(_dot_nt(ub, d_bub) + _dot_nt(wb, d_bwb))
            d_mkk = jnp.where(mk.strict_lower, d_low * dec, 0.0)
            d_mkk_t = jnp.where(mk.strict_upper, d_low_t * dec_t, 0.0)
            d_mqk = jnp.where(mk.lower, d_attn * dec, 0.0)
            d_mqk_t = jnp.where(mk.upper, d_attn_t * dec_t, 0.0)
            bw = kb * eg
            d_kb = _dot(_bf(d_mkk), kbf) + d_bw * eg
            d_k = _dot(_bf(d_mkk_t), kbb) + _dot(_bf(d_mqk_t), qb) + d_kt * etl + d_kb * bc
            d_q = _dot(_bf(d_mqk), kbf) + d_qd * eg
            d_beta = rowsum(d_kb * kh) + rowsum(d_bu * vh)
            dv_ref[:, sl] = d_bu * bc
            e_mat = d_mkk * f["mkk"] + d_mqk * f["mqk"]
            e_mat_t = d_mkk_t * mkk_t + d_mqk_t * mqk_t
            kt_term = rowsum(d_kt * kt)
            d_g = rowsum(e_mat) - rowsum(e_mat_t) + rowsum(d_qd * qd) + rowsum(d_bw * bw) - kt_term
            for c, chunk_rows in enumerate((mk.first, ~mk.first)):
                d_glast = dgl_ref[c, 0:1, gate] + jnp.sum(jnp.where(chunk_rows, kt_term, 0.0), axis=0, keepdims=True)
                d_g = d_g + jnp.where(rowi == (c + 1) * DN_CHUNK - 1, d_glast, 0.0)
            qn = f["qn"]
            d_qs = d_q * (DN_DIM ** -0.5)
            dq_ref[:, sl] = f["rq"] * (d_qs - qn * rowsum(d_qs * qn))
            dk_ref[:, sl] = f["rk"] * (d_k - kh * rowsum(d_k * kh))
            dgcum_t = jnp.where(lane == DN_HEADS + h, d_g, dgcum_t)
            dbeta_t = jnp.where(lane == h, d_beta, dbeta_t)
        dgraw_t = _dot_hi(mk.upper.astype(F32), dgcum_t)
        sp = _sigmoid(pre)
        d_pre = dgraw_t * coef * sp
        dba_ref[...] = jnp.where(lane < DN_HEADS, dbeta_t * beta_t * (1.0 - beta_t),
                                 jnp.where(lane < 2 * DN_HEADS, d_pre, 0.0))
        in_g = (lane >= DN_HEADS) & (lane < 2 * DN_HEADS)
        d_alog = jnp.sum(jnp.where(in_g, dgraw_t * graw_t, 0.0), axis=0, keepdims=True)
        d_dtb = jnp.sum(jnp.where(in_g, d_pre, 0.0), axis=0, keepdims=True)
        return jnp.concatenate([d_alog, d_dtb, jnp.zeros((6, 128), F32)], axis=0)

    sp = _dn_specs(rows)
    return _call_with_exchange(
        body, exchange, name=name, steps=n_chunks // group,
        in_specs=sp["qkv"] + [sp["ba"], sp["hp"], sp["state"]] + [sp["tok"]] * 4 + [sp["gate"]] + [sp["tok"]] * 3,
        out_specs=[pl.BlockSpec((rows, 3 * hw), lambda n: (n, 0)), pl.BlockSpec((rows, 128), lambda n: (n, 0)), sp["hp"]],
        out_shape=[jax.ShapeDtypeStruct((T, 3 * hw), F32), jax.ShapeDtypeStruct((T, 128), F32),
                   jax.ShapeDtypeStruct((8, 128), F32)],
        args=(qkv, qkv, qkv, proj, hp, sall, vn, do, dvn, dkt, dgl, inv, u, w))


def _mix_fwd(o, proj, dn_norm, sg_norm, sg_w, sg_bt, name):
    T = o.shape[0]
    tm = _tile(T, 512)
    hw = DN_HEADS * DN_DIM
    nc = tm // SG_CHUNK

    def body(o_ref, z_ref, su_ref, sv_ref, dnn_ref, sgn_ref, sgw_ref, sgb_ref, mix_ref):
        dnn = dnn_ref[...]
        for h in range(DN_HEADS):
            sl = slice(h * DN_DIM, (h + 1) * DN_DIM)
            xhat, _ = _rms_stats(o_ref[:, sl])
            z = z_ref[:, sl]
            mix_ref[:, sl] = _bf(xhat * dnn * (z * _sigmoid(z)))
        tri = lax.broadcasted_iota(jnp.int32, (SG_CHUNK, SG_CHUNK), 0) >= lax.broadcasted_iota(jnp.int32, (SG_CHUNK, SG_CHUNK), 1)
        for g in range(SG_GROUPS):
            sl = slice(g * SG_DIM, (g + 1) * SG_DIM)
            xhat, _ = _rms_stats(_gelu(sv_ref[:, sl]))
            svn = _bf(xhat * sgn_ref[g:g + 1, :])
            sua = _gelu(su_ref[:, sl])
            wt = _bf(jnp.where(tri, sgw_ref[g], 0.0))
            bias = sgb_ref[:, g:g + 1]
            for c in range(nc):
                rows = slice(c * SG_CHUNK, (c + 1) * SG_CHUNK)
                mixed = _dot(wt, svn[rows, :]) + bias
                mix_ref[rows, hw + g * SG_DIM:hw + (g + 1) * SG_DIM] = _bf(sua[rows, :] * mixed)

    full = lambda shape: pl.BlockSpec(shape, lambda i: (0,) * len(shape))
    return pl.pallas_call(
        body, name=name, grid=(T // tm,),
        in_specs=[pl.BlockSpec((tm, hw), lambda i: (i, 0)),
                  pl.BlockSpec((tm, hw), lambda i: (i, 3)),
                  pl.BlockSpec((tm, hw), lambda i: (i, 4)),
                  pl.BlockSpec((tm, hw), lambda i: (i, 5)),
                  full((1, DN_DIM)), full((SG_GROUPS, SG_DIM)), full((SG_GROUPS, SG_CHUNK, SG_CHUNK)),
                  full((SG_CHUNK, 128))],
        out_specs=pl.BlockSpec((tm, 2 * hw), lambda i: (i, 0)),
        out_shape=jax.ShapeDtypeStruct((T, 2 * hw), BF16),
        compiler_params=_params(("parallel",)),
    )(o, proj, proj, proj, dn_norm, sg_norm, sg_w, sg_bt)


def _mix_bwd(dmix, o, proj, dn_norm, sg_norm, sg_w, sg_bt, name):
    T = o.shape[0]
    tm = _tile(T, 512)
    hw = DN_HEADS * DN_DIM
    nc = tm // SG_CHUNK

    def body(dm_ref, o_ref, z_ref, su_ref, sv_ref, dnn_ref, sgn_ref, sgw_ref, sgb_ref,
             do_ref, dz_ref, ddnn_ref, dsgn_ref, dsgw_ref, dsgb_ref):
        @pl.when(pl.program_id(0) == 0)
        def _():
            ddnn_ref[...] = jnp.zeros_like(ddnn_ref)
            dsgn_ref[...] = jnp.zeros_like(dsgn_ref)
            dsgw_ref[...] = jnp.zeros_like(dsgw_ref)
            dsgb_ref[...] = jnp.zeros_like(dsgb_ref)

        dnn = dnn_ref[...]
        ddnn = jnp.zeros((1, DN_DIM), F32)
        for h in range(DN_HEADS):
            sl = slice(h * DN_DIM, (h + 1) * DN_DIM)
            xhat, r = _rms_stats(o_ref[:, sl])
            z = z_ref[:, sl]
            sz = _sigmoid(z)
            doa = dm_ref[:, sl]
            dyn = doa * (z * sz)
            dz_ref[:, sl] = _bf(doa * xhat * dnn * _silu_grad(z, sz))
            do_ref[:, sl] = _rms_bwd(dyn, xhat, r, dnn)
            ddnn = ddnn + jnp.sum(dyn * xhat, axis=0, keepdims=True)
        ddnn_ref[...] += ddnn
        tri = lax.broadcasted_iota(jnp.int32, (SG_CHUNK, SG_CHUNK), 0) >= lax.broadcasted_iota(jnp.int32, (SG_CHUNK, SG_CHUNK), 1)
        lane = lax.broadcasted_iota(jnp.int32, (SG_CHUNK, 128), 1)
        dsgb = jnp.zeros((SG_CHUNK, 128), F32)
        dsgn_rows = []
        for g in range(SG_GROUPS):
            sl = slice(g * SG_DIM, (g + 1) * SG_DIM)
            sv = sv_ref[:, sl]
            su = su_ref[:, sl]
            xhat, r = _rms_stats(_gelu(sv))
            sgn = sgn_ref[g:g + 1, :]
            svn = _bf(xhat * sgn)
            sua = _gelu(su)
            wt = _bf(jnp.where(tri, sgw_ref[g], 0.0))
            bias = sgb_ref[:, g:g + 1]
            dw = jnp.zeros((SG_CHUNK, SG_CHUNK), F32)
            db = jnp.zeros((SG_CHUNK, 1), F32)
            dsua, dsvn = [], []
            for c in range(nc):
                rows = slice(c * SG_CHUNK, (c + 1) * SG_CHUNK)
                mixed = _dot(wt, svn[rows, :]) + bias
                dob = dm_ref[rows, hw + g * SG_DIM:hw + (g + 1) * SG_DIM]
                dsua.append(dob * mixed)
                dmixed = dob * sua[rows, :]
                dmb = _bf(dmixed)
                dsvn.append(_dot_tn(wt, dmb))
                dw = dw + _dot_nt(dmb, svn[rows, :])
                db = db + jnp.sum(dmixed, axis=1, keepdims=True)
            dsua = jnp.concatenate(dsua, axis=0) if nc > 1 else dsua[0]
            dsvn = jnp.concatenate(dsvn, axis=0) if nc > 1 else dsvn[0]
            dz_ref[:, hw + g * SG_DIM:hw + (g + 1) * SG_DIM] = _bf(dsua * _gelu_grad(su))
            dz_ref[:, 2 * hw + g * SG_DIM:2 * hw + (g + 1) * SG_DIM] = _bf(_rms_bwd(dsvn, xhat, r, sgn) * _gelu_grad(sv))
            dsgn_rows.append(jnp.sum(dsvn * xhat, axis=0, keepdims=True))
            dsgw_ref[g] += jnp.where(tri, dw, 0.0)
            dsgb = jnp.where(lane == g, db, dsgb)
        dsgn_ref[...] += jnp.concatenate(dsgn_rows, axis=0)
        dsgb_ref[...] += dsgb

    full = lambda shape: pl.BlockSpec(shape, lambda i: (0,) * len(shape))
    return pl.pallas_call(
        body, name=name, grid=(T // tm,),
        in_specs=[pl.BlockSpec((tm, 2 * hw), lambda i: (i, 0)),
                  pl.BlockSpec((tm, hw), lambda i: (i, 0)),
                  pl.BlockSpec((tm, hw), lambda i: (i, 3)),
                  pl.BlockSpec((tm, hw), lambda i: (i, 4)),
                  pl.BlockSpec((tm, hw), lambda i: (i, 5)),
                  full((1, DN_DIM)), full((SG_GROUPS, SG_DIM)), full((SG_GROUPS, SG_CHUNK, SG_CHUNK)),
                  full((SG_CHUNK, 128))],
        out_specs=[pl.BlockSpec((tm, hw), lambda i: (i, 0)),
                   pl.BlockSpec((tm, 3 * hw), lambda i: (i, 0)),
                   full((1, DN_DIM)), full((SG_GROUPS, SG_DIM)), full((SG_GROUPS, SG_CHUNK, SG_CHUNK)),
                   full((SG_CHUNK, 128))],
        out_shape=[jax.ShapeDtypeStruct((T, hw), F32), jax.ShapeDtypeStruct((T, 3 * hw), BF16),
                   jax.ShapeDtypeStruct((1, DN_DIM), F32), jax.ShapeDtypeStruct((SG_GROUPS, SG_DIM), F32),
                   jax.ShapeDtypeStruct((SG_GROUPS, SG_CHUNK, SG_CHUNK), F32),
                   jax.ShapeDtypeStruct((SG_CHUNK, 128), F32)],
        compiler_params=_params(("arbitrary",)),
    )(dmix, o, proj, proj, proj, dn_norm, sg_norm, sg_w, sg_bt)


def _window_sums(h, sign):
    sums, s, w = {}, h, 1
    while w < POOL_WINDOWS[-1]:
        s = s + _shift_rows(s, sign * w)
        w *= 2
        sums[w] = s
    return sums


def _pool_counts(t_global):
    return [jnp.minimum(t_global + 1, win).astype(F32) for win in POOL_WINDOWS]


def _pooled_groups(ext_h, row0, tm):
    sums = _window_sums(ext_h, 1)
    t_global = row0 + lax.broadcasted_iota(jnp.int32, (tm, 1), 0)
    counts = _pool_counts(t_global)
    out = []
    for gi, win in enumerate(POOL_WINDOWS):
        cols = slice(gi * POOL_DIM, (gi + 1) * POOL_DIM)
        out.append(sums[win][POOL_HALO:, cols] / counts[gi] - ext_h[POOL_HALO:, cols])
    return out


def _pool_fwd(x, nw, pool_w, pool_scale, layer, name):
    T, D = x.shape
    tm = _tile(T, 256)
    hb = tm // POOL_HALO

    def body(x_ref, xp_ref, n_ref, w_ref, s_ref, xo_ref):
        i = pl.program_id(0)
        prev = jnp.where(i == 0, 0.0, xp_ref[...])
        ext = jnp.concatenate([prev, x_ref[...]], axis=0)
        xhat, _ = _rms_stats(ext)
        pooled = _pooled_groups(xhat * n_ref[...], i * tm, tm)
        for gi in range(len(POOL_WINDOWS)):
            cols = slice(gi * POOL_DIM, (gi + 1) * POOL_DIM)
            xo_ref[:, cols] = x_ref[:, cols] + _dot(_bf(pooled[gi]), w_ref[gi]) * s_ref[:, cols]

    return pl.pallas_call(
        body, name=name, grid=(T // tm,),
        in_specs=[pl.BlockSpec((tm, D), lambda i: (i, 0)),
                  pl.BlockSpec((POOL_HALO, D), lambda i: (jnp.maximum(i * hb - 1, 0), 0)),
                  pl.BlockSpec((None, 1, D), lambda i: (layer, 0, 0)),
                  pl.BlockSpec(pool_w.shape, lambda i: (0, 0, 0)),
                  pl.BlockSpec((1, D), lambda i: (0, 0))],
        out_specs=pl.BlockSpec((tm, D), lambda i: (i, 0)),
        out_shape=jax.ShapeDtypeStruct((T, D), F32),
        compiler_params=_params(("parallel",)),
    )(x, x, nw, pool_w, pool_scale)


def _pool_bwd(dxo, x, nw, pool_w, pool_scale, layer, name):
    T, D = x.shape
    tm = _tile(T, 256)
    hb = tm // POOL_HALO
    nt = T // tm
    ng = len(POOL_WINDOWS)

    def body(dxo_ref, dxn_ref, x_ref, xp_ref, n_ref, w_ref, s_ref, dx_ref, dw_ref, ds_ref, dn_ref):
        i = pl.program_id(0)

        @pl.when(i == 0)
        def _():
            dw_ref[...] = jnp.zeros_like(dw_ref)
            ds_ref[...] = jnp.zeros_like(ds_ref)
            dn_ref[...] = jnp.zeros_like(dn_ref)

        prev = jnp.where(i == 0, 0.0, xp_ref[...])
        ext = jnp.concatenate([prev, x_ref[...]], axis=0)
        xhat_ext, r_ext = _rms_stats(ext)
        nv = n_ref[...]
        pooled = _pooled_groups(xhat_ext * nv, i * tm, tm)
        dxo = dxo_ref[...]
        scale = s_ref[...]
        dout_ext = jnp.concatenate([dxo, jnp.where(i == nt - 1, 0.0, dxn_ref[...])], axis=0) * scale
        t_ext = i * tm + lax.broadcasted_iota(jnp.int32, (tm + POOL_HALO, 1), 0)
        counts = _pool_counts(t_ext)
        dh_cols, ds_cols = [], []
        for gi, win in enumerate(POOL_WINDOWS):
            cols = slice(gi * POOL_DIM, (gi + 1) * POOL_DIM)
            wg = w_ref[gi]
            pb = _bf(pooled[gi])
            doutb = _bf(dout_ext[:, cols])
            dpooled = _dot_nt(doutb, wg)
            ahead = _window_sums(dpooled / counts[gi], -1)[win]
            dh_cols.append(ahead[:tm, :] - dpooled[:tm, :])
            dw_ref[gi] += _dot_tn(pb, doutb[:tm, :])
            ds_cols.append(jnp.sum(dxo[:, cols] * _dot(pb, wg), axis=0, keepdims=True))
        dh = jnp.concatenate(dh_cols, axis=1)
        xhat, r = xhat_ext[POOL_HALO:, :], r_ext[POOL_HALO:, :]
        dx_ref[...] = dxo + _rms_bwd(dh, xhat, r, nv)
        dn_ref[...] += jnp.sum(dh * xhat, axis=0, keepdims=True)
        ds_ref[...] += jnp.concatenate(ds_cols, axis=1)

    last_halo = T // POOL_HALO - 1
    return pl.pallas_call(
        body, name=name, grid=(nt,),
        in_specs=[pl.BlockSpec((tm, D), lambda i: (i, 0)),
                  pl.BlockSpec((POOL_HALO, D), lambda i: (jnp.minimum((i + 1) * hb, last_halo), 0)),
                  pl.BlockSpec((tm, D), lambda i: (i, 0)),
                  pl.BlockSpec((POOL_HALO, D), lambda i: (jnp.maximum(i * hb - 1, 0), 0)),
                  pl.BlockSpec((None, 1, D), lambda i: (layer, 0, 0)),
                  pl.BlockSpec(pool_w.shape, lambda i: (0, 0, 0)),
                  pl.BlockSpec((1, D), lambda i: (0, 0))],
        out_specs=[pl.BlockSpec((tm, D), lambda i: (i, 0)),
                   pl.BlockSpec((ng, POOL_DIM, POOL_DIM), lambda i: (0, 0, 0)),
                   pl.BlockSpec((1, D), lambda i: (0, 0)),
                   pl.BlockSpec((1, D), lambda i: (0, 0))],
        out_shape=[jax.ShapeDtypeStruct((T, D), F32), jax.ShapeDtypeStruct((ng, POOL_DIM, POOL_DIM), F32),
                   jax.ShapeDtypeStruct((1, D), F32), jax.ShapeDtypeStruct((1, D), F32)],
        compiler_params=_params(("arbitrary",)),
    )(dxo, dxo, x, x, nw, pool_w, pool_scale)


def _loss_head(x, target, fn, name):
    T, D = x.shape
    tm = _tile(T, 512)

    def body(x_ref, t_ref, n_ref, loss_ref, dx_ref, dn_ref):
        @pl.when(pl.program_id(0) == 0)
        def _():
            loss_ref[...] = jnp.zeros_like(loss_ref)
            dn_ref[...] = jnp.zeros_like(dn_ref)

        xhat, r = _rms_stats(x_ref[...])
        nv = n_ref[...]
        err = xhat * nv - t_ref[...]
        part = jnp.sum(jnp.sum(err * err, axis=1, keepdims=True), axis=0, keepdims=True)
        loss_ref[...] += 0.5 * part / D
        dy = err / D
        dx_ref[...] = _rms_bwd(dy, xhat, r, nv)
        dn_ref[...] += jnp.sum(dy * xhat, axis=0, keepdims=True)

    row = pl.BlockSpec((tm, D), lambda i: (i, 0))
    return pl.pallas_call(
        body, name=name, grid=(T // tm,),
        in_specs=[row, row, pl.BlockSpec((1, D), lambda i: (0, 0))],
        out_specs=[pl.BlockSpec((1, 1), lambda i: (0, 0)), row, pl.BlockSpec((1, D), lambda i: (0, 0))],
        out_shape=[jax.ShapeDtypeStruct((1, 1), F32), jax.ShapeDtypeStruct((T, D), F32),
                   jax.ShapeDtypeStruct((1, D), F32)],
        compiler_params=_params(("arbitrary",)),
    )(x, target, fn)


def _adamw(w, g, m, v, name):
    R, C = w.shape
    br = R
    for cand in (512, 256, 128, 64, 32, 16, 8):
        if R % cand == 0 and cand * C * 4 <= 2 * 1024 * 1024:
            br = cand
            break

    def body(w_ref, g_ref, m_ref, v_ref, d_ref, mo_ref, vo_ref):
        gv = g_ref[...]
        m_new = ADAM_B1 * m_ref[...] + (1.0 - ADAM_B1) * gv
        v_new = ADAM_B2 * v_ref[...] + (1.0 - ADAM_B2) * (gv * gv)
        m_hat = m_new / (1.0 - ADAM_B1 ** ADAM_STEP)
        v_hat = v_new / (1.0 - ADAM_B2 ** ADAM_STEP)
        d_ref[...] = -ADAM_LR * (m_hat / (jnp.sqrt(v_hat) + ADAM_EPS) + ADAM_WD * w_ref[...])
        mo_ref[...] = m_new
        vo_ref[...] = v_new

    blk = pl.BlockSpec((br, C), lambda i: (i, 0))
    return pl.pallas_call(
        body, name=name, grid=(R // br,), in_specs=[blk] * 4, out_specs=[blk] * 3,
        out_shape=[jax.ShapeDtypeStruct((R, C), F32)] * 3,
        compiler_params=_params(("parallel",)),
    )(w, g, m, v)


def _mesh_pos():
    return lax.axis_index("x"), lax.axis_index("y"), lax.axis_index("c")


def _other_chips(x, y):
    return [(1 - x, y), (x, 1 - y), (1 - x, 1 - y)]


def _half_of(ref, shape, h):
    size = shape[0] // 2
    return ref.at[pl.ds(h * size, size)]


class _ChipGather:
    def __init__(self, shards, split):
        self.shards, self.split = list(shards), list(split)
        self.operands = self.shards
        n = len(self.shards)
        self.out_shape = [jax.ShapeDtypeStruct((N_CHIPS,) + s.shape, s.dtype) for s in self.shards]
        self.scratch = [pltpu.SemaphoreType.DMA((n, 3))] * 4

    def _piece(self, a, ref, h):
        return _half_of(ref, self.shards[a].shape, h) if self.split[a] else ref

    def start(self, ins, outs, sems):
        send_sems, recv_sems = sems[0], sems[1]
        x, y, c = _mesh_pos()
        me = 2 * x + y
        for a in range(len(ins)):
            for k, (px, py) in enumerate(_other_chips(x, y)):
                pltpu.make_async_remote_copy(self._piece(a, ins[a], c), self._piece(a, outs[a].at[me], c),
                                             send_sems.at[a, k], recv_sems.at[a, k],
                                             device_id=(px, py, c), device_id_type=MESH).start()

    def finish(self, ins, outs, sems):
        send_sems, recv_sems, fwd_send_sems, fwd_recv_sems = sems
        x, y, c = _mesh_pos()
        sibling = (x, y, 1 - c)
        chips = _other_chips(x, y)
        n = len(ins)
        forwards = []
        for a in range(n):
            for k, (px, py) in enumerate(chips):
                landed = self._piece(a, outs[a].at[2 * px + py], c)
                pltpu.make_async_remote_copy(landed, landed, send_sems.at[a, k], recv_sems.at[a, k],
                                             device_id=(px, py, c), device_id_type=MESH).wait_recv()
                if self.split[a]:
                    fwd = pltpu.make_async_remote_copy(landed, landed, fwd_send_sems.at[a, k], fwd_recv_sems.at[a, k],
                                                       device_id=sibling, device_id_type=MESH)
                    fwd.start()
                    forwards.append(fwd)
        for a in range(n):
            if self.split[a]:
                for k, (px, py) in enumerate(chips):
                    other = self._piece(a, outs[a].at[2 * px + py], 1 - c)
                    pltpu.make_async_remote_copy(other, other, fwd_send_sems.at[a, k], fwd_recv_sems.at[a, k],
                                                 device_id=sibling, device_id_type=MESH).wait_recv()
        for a in range(n):
            for k, (px, py) in enumerate(chips):
                sent = self._piece(a, ins[a], c)
                pltpu.make_async_remote_copy(sent, sent, send_sems.at[a, k], recv_sems.at[a, k],
                                             device_id=(px, py, c), device_id_type=MESH).wait_send()
        for fwd in forwards:
            fwd.wait_send()

    def finalize(self, gathered):
        x, y, _ = _mesh_pos()
        return [lax.dynamic_update_index_in_dim(g, s, 2 * x + y, 0) for g, s in zip(gathered, self.shards)]

    def run(self, name):
        n = len(self.shards)

        def body(*refs):
            ins, outs, sems = refs[:n], refs[n:2 * n], refs[2 * n:]
            self.start(ins, outs, sems)
            self.finish(ins, outs, sems)

        gathered = pl.pallas_call(
            body, name=name, in_specs=[ANY] * n, out_specs=[ANY] * n, out_shape=self.out_shape,
            scratch_shapes=self.scratch, compiler_params=pltpu.CompilerParams(has_side_effects=True),
        )(*self.shards)
        return self.finalize(gathered)


def _ffn_weight_grads(hb, dg, du, a, dyb, tag):
    dwg = _matmul_tn(hb, dg, D_MODEL, FF_CHUNK, f"{tag}_dw_gate", stack_n=True)
    dwu = _matmul_tn(hb, du, D_MODEL, FF_CHUNK, f"{tag}_dw_up", stack_n=True)
    dwo = _matmul_tn(a, dyb, FF_CHUNK, D_MODEL, f"{tag}_dw_out")
    return jnp.concatenate([dwg, dwu], axis=0), dwo.reshape(N_CHIPS, D_FF // N_CHIPS, D_MODEL)


def _local_step(x, target, w, late=None, reduce=False):
    g = {}
    acts = []
    w = dict(w)

    def ffn_weights(which, layer):
        return w[f"n{which}"], w[f"win{which}_l{layer}"], w[f"wout{which}_l{layer}"]

    def ffn(xin, which, layer):
        xo, gv, uv, hb = _ffn_fwd(xin, *ffn_weights(which, layer), layer, f"ffn{which}_l{layer}_fwd")
        acts.append((xin, gv, uv, hb))
        return xo

    x1 = ffn(x, 1, 0)
    hb_mix = _rms_fwd_call(x1, w["nmix"], 0, "ab_norm_fwd")
    proj = _matmul(hb_mix, w["wp"], "ab_in_proj")
    qkv = _conv_fwd(proj, w["conv_w"], "dn_conv_fwd")
    (dn_u, dn_w, dn_p, dn_qd, dn_kt, dn_gl, dn_inv), arrived = _dn_prep(qkv, proj, w["hp"], "dn_prep",
                                                                       None if late is None else late[0])
    if late is not None:
        w.update(late[1](arrived))
    o, dn_vn, sall = _dn_scan(dn_u, dn_w, dn_p, dn_qd, dn_kt, dn_gl, "dn_scan")
    mix = _mix_fwd(o, proj, w["dn_norm"], w["sg_norm"], w["sg_w"], w["sg_bt"], "ab_gate_fwd")
    x2 = _matmul(mix, w["wo"], "ab_out_proj", res=x1)
    x3 = ffn(x2, 2, 0)
    x4 = ffn(x3, 1, 1)
    x5 = _pool_fwd(x4, w["nmix"], w["pool_w"], w["pool_scale"], 1, "pool_fwd")
    x6 = ffn(x5, 2, 1)
    loss, dx, g["fn"] = _loss_head(x6, target, w["fn"], "loss_head")

    dn = {1: [None, None], 2: [None, None]}
    dwin = {1: [None, None], 2: [None, None]}
    dwout = {1: [None, None], 2: [None, None]}

    def ffn_back(dxo, which, layer, saved, exchange=None):
        nw, win, wout = ffn_weights(which, layer)
        xin, gv, uv, hb = saved
        tag = f"ffn{which}_l{layer}"
        (dxi, dg, du, a, dyb, dnw), arrived = _ffn_bwd(dxo, xin, nw, gv, uv, win, wout, layer, f"{tag}_bwd", exchange)
        dn[which][layer] = dnw
        dwin[which][layer], dwout[which][layer] = _ffn_weight_grads(hb, dg, du, a, dyb, tag)
        return dxi, arrived

    for which in (1, 2):
        g[f"win{which}"] = dwin[which]
        g[f"wout{which}"] = dwout[which]
    reduced = {}

    def open_round(tag, keys):
        have = _sharded_grads(g)
        return _GradRound(tag, {k: have[k] for k in keys})

    dx, _ = ffn_back(dx, 2, 1, acts[3])
    dx, g["pool_w"], g["pool_scale"], dnmix1 = _pool_bwd(dx, x4, w["nmix"], w["pool_w"], w["pool_scale"], 1, "pool_bwd")
    dx, _ = ffn_back(dx, 1, 1, acts[2])
    round_a = open_round("a", REDUCE_ROUNDS[0]) if reduce else None
    dx2, arrived = ffn_back(dx, 2, 0, acts[1], round_a.scatter if reduce else None)
    if reduce:
        reduced.update(round_a.finish(arrived))
    round_b = open_round("b", REDUCE_ROUNDS[1]) if reduce else None
    dmix = _matmul(dx2, w["wo"], "ab_out_proj_bwd", trans_b=True)
    g["wo"] = _matmul_tn(mix, dx2, D_MODEL, D_MODEL, "ab_out_proj_dw")
    do, dzuv, g["dn_norm"], g["sg_norm"], g["sg_w"], g["sg_bt"] = _mix_bwd(
        dmix, o, proj, w["dn_norm"], w["sg_norm"], w["sg_w"], w["sg_bt"], "ab_gate_bwd")
    dvn, dkt, dgl = _dn_scan_bwd(dn_w, dn_p, dn_qd, dn_kt, dn_gl, dn_vn, sall, do, "dn_scan_bwd")
    (dqkv_act, dba, g["hp"]), arrived = _dn_prep_bwd(qkv, proj, w["hp"], sall, dn_vn, do, dvn, dkt, dgl, dn_inv, dn_u, dn_w,
                                                     "dn_prep_bwd", round_b.scatter if reduce else None)
    if reduce:
        reduced.update(round_b.finish(arrived))
    dqkv, g["conv_w"] = _conv_bwd(dqkv_act, proj, w["conv_w"], "dn_conv_bwd")
    dproj = jnp.concatenate([dqkv, dzuv, dba.astype(BF16)], axis=1)
    dh = _matmul(dproj, w["wp"], "ab_in_proj_bwd", trans_b=True)
    g["wp"] = _matmul_tn(hb_mix, dproj, D_MODEL, 640, "ab_in_proj_dw")
    dx1, dnmix0 = _rms_bwd_call(dh, x1, w["nmix"], dx2, 0, "ab_norm_bwd")
    dx0, _ = ffn_back(dx1, 1, 0, acts[0])
    if reduce:
        round_c = open_round("c", REDUCE_ROUNDS[2])
        reduced.update(round_c.finish(round_c.scatter.run("grad_c_chip_scatter")))

    g["n1"] = jnp.concatenate(dn[1], axis=0)
    g["n2"] = jnp.concatenate(dn[2], axis=0)
    g["nmix"] = jnp.concatenate([dnmix0, dnmix1], axis=0)
    return loss, dx0, g, reduced


SHARDED = ("ffn1_w_in", "ffn1_w_out", "ffn2_w_in", "ffn2_w_out", "ab_w_in", "ab_w_out", "pool_w", "dn_conv_w", "pool_scale")
REPLICATED = ("ffn_norm1", "mix_norm", "ffn_norm2", "dn_a_log", "dn_dt_bias", "dn_out_norm", "sg_norm", "sg_w", "sg_b", "final_norm")
QKVZ = 4 * DN_HEADS * DN_DIM
N_GATES = 2 * DN_HEADS
IN_PROJ = QKVZ + N_GATES + 2 * SG_GROUPS * SG_DIM


def _shard_pieces(wts, keys):
    out = []
    for n, layer in keys:
        a = wts[n][0 if layer is None else layer]
        a = a[None] if a.ndim == 1 else a
        out.append(a.astype(BF16) if n in MATRICES else a)
    return out


def _replicated_layouts(rep):
    per_layer = lambda a: a.reshape(a.shape[0], 1, D_MODEL)
    w = {"n1": per_layer(rep["ffn_norm1"]), "nmix": per_layer(rep["mix_norm"]), "n2": per_layer(rep["ffn_norm2"])}
    hp = jnp.zeros((8, 128), F32)
    w["hp"] = hp.at[0, DN_HEADS:N_GATES].set(rep["dn_a_log"][0]).at[1, DN_HEADS:N_GATES].set(rep["dn_dt_bias"][0])
    w["dn_norm"] = rep["dn_out_norm"]
    w["sg_norm"] = rep["sg_norm"][0]
    w["sg_w"] = rep["sg_w"][0]
    w["sg_bt"] = jnp.zeros((SG_CHUNK, 128), F32).at[:, :SG_GROUPS].set(rep["sg_b"][0].T)
    w["fn"] = rep["final_norm"].reshape(1, D_MODEL)
    return w


def _layouts_from(gathered):
    w = {}
    for (n, layer), a in gathered.items():
        if n in ("ffn1_w_in", "ffn2_w_in"):
            w[f"win{n[3]}_l{layer}"] = a
        elif n in ("ffn1_w_out", "ffn2_w_out"):
            w[f"wout{n[3]}_l{layer}"] = a
        elif n == "ab_w_in":
            ab_in = jnp.transpose(a, (1, 0, 2)).reshape(D_MODEL, IN_PROJ)
            w["wp"] = jnp.concatenate([ab_in[:, :QKVZ], ab_in[:, QKVZ + N_GATES:], ab_in[:, QKVZ:QKVZ + N_GATES],
                                       jnp.zeros((D_MODEL, PROJ_W - IN_PROJ), ab_in.dtype)], axis=1)
        elif n == "dn_conv_w":
            w["conv_w"] = jnp.transpose(a, (1, 0, 2)).reshape(DN_CONV, 3 * DN_HEADS * DN_DIM)
        elif n == "ab_w_out":
            w["wo"] = a.reshape(D_MODEL, D_MODEL)
        elif n == "pool_w":
            w["pool_w"] = jnp.transpose(a, (1, 0, 2, 3)).reshape(len(POOL_WINDOWS), POOL_DIM, POOL_DIM)
        elif n == "pool_scale":
            w["pool_scale"] = a.reshape(1, D_MODEL)
    return w


def _sharded_grads(g):
    nw = len(POOL_WINDOWS)
    sharded = {}
    for n, key in (("ffn1_w_in", "win1"), ("ffn1_w_out", "wout1"), ("ffn2_w_in", "win2"), ("ffn2_w_out", "wout2")):
        for layer, a in enumerate(g.get(key, ())):
            if a is not None:
                sharded[(n, layer)] = a
    if "wp" in g:
        wp = g["wp"]
        ab_in = jnp.concatenate([wp[:, :QKVZ], wp[:, IN_PROJ - N_GATES:IN_PROJ], wp[:, QKVZ:IN_PROJ - N_GATES]], axis=1)
        sharded[("ab_w_in", None)] = jnp.transpose(ab_in.reshape(D_MODEL, N_CHIPS, IN_PROJ // N_CHIPS), (1, 0, 2))
    if "wo" in g:
        sharded[("ab_w_out", None)] = g["wo"].reshape(N_CHIPS, D_MODEL // N_CHIPS, D_MODEL)
    if "pool_w" in g:
        sharded[("pool_w", None)] = jnp.transpose(g["pool_w"].reshape(nw, N_CHIPS, POOL_DIM // N_CHIPS, POOL_DIM), (1, 0, 2, 3))
    if "conv_w" in g:
        sharded[("dn_conv_w", None)] = jnp.transpose(g["conv_w"].reshape(DN_CONV, N_CHIPS, -1), (1, 0, 2))
    if "pool_scale" in g:
        sharded[("pool_scale", None)] = g["pool_scale"].reshape(N_CHIPS, 1, D_MODEL // N_CHIPS)
    return sharded


def _replicated_grads(g):
    rep = {
        "ffn_norm1": g["n1"], "mix_norm": g["nmix"], "ffn_norm2": g["n2"],
        "dn_a_log": g["hp"][0:1, DN_HEADS:N_GATES], "dn_dt_bias": g["hp"][1:2, DN_HEADS:N_GATES],
        "dn_out_norm": g["dn_norm"], "sg_norm": g["sg_norm"][None], "sg_w": g["sg_w"][None],
        "sg_b": g["sg_bt"][:, :SG_GROUPS].T[None], "final_norm": g["fn"].reshape(D_MODEL),
    }
    return rep


def _piece_rows(n_elems):
    rows = -(-n_elems // PACK_LANES)
    return -(-rows // 8) * 8


def _half_size(shape):
    size = 1
    for d in shape:
        size *= d
    return size // 2


def _pack_by_half(pieces):
    used = sum(_piece_rows(_half_size(a.shape[1:])) for a in pieces)
    rows = -(-used // PACK_ROW_BLOCK) * PACK_ROW_BLOCK
    parts = []
    for half in range(2):
        for a in pieces:
            flat = a.reshape(N_CHIPS, 2, -1)[:, half]
            pr = _piece_rows(flat.shape[1])
            flat = jnp.pad(flat, ((0, 0), (0, pr * PACK_LANES - flat.shape[1])))
            parts.append(flat.reshape(N_CHIPS, pr, PACK_LANES))
        parts.append(jnp.zeros((N_CHIPS, rows - used, PACK_LANES), F32))
    return jnp.concatenate(parts, axis=1).reshape(N_CHIPS, 2, rows, PACK_LANES)


def _unpack_halves(mine, other, core, shapes):
    out, off = [], 0
    for shape in shapes:
        half = _half_size(shape)
        pr = _piece_rows(half)
        a, b = mine[off:off + pr], other[off:off + pr]
        both = jnp.stack([jnp.where(core == 0, a, b), jnp.where(core == 0, b, a)])
        out.append(both.reshape(2, -1)[:, :half].reshape(shape))
        off += pr
    return out


def _swap_with_sibling(pack, name):
    nchip, _, rows, lanes = pack.shape

    def body(pack_ref, recv_ref, send_sem, recv_sem):
        x, y, c = _mesh_pos()
        cp = pltpu.make_async_remote_copy(pack_ref.at[:, 1 - c], recv_ref, send_sem, recv_sem,
                                          device_id=(x, y, 1 - c), device_id_type=MESH)
        cp.start()
        cp.wait()

    return pl.pallas_call(
        body, name=name, in_specs=[ANY], out_specs=ANY,
        out_shape=jax.ShapeDtypeStruct((nchip, rows, lanes), pack.dtype),
        scratch_shapes=[pltpu.SemaphoreType.DMA, pltpu.SemaphoreType.DMA],
        compiler_params=pltpu.CompilerParams(has_side_effects=True),
    )(pack)


def _add_pair(pack, recv, core, name):
    nchip, _, rows, lanes = pack.shape

    def body(c_ref, a_ref, b_ref, o32_ref, o16_ref):
        s = a_ref[...] + b_ref[...]
        o32_ref[...] = s
        o16_ref[...] = _bf(s)

    blk = pl.BlockSpec((None, PACK_ROW_BLOCK, lanes), lambda p, i, c: (p, i, 0))
    return pl.pallas_call(
        body, name=name,
        grid_spec=pltpu.PrefetchScalarGridSpec(
            num_scalar_prefetch=1, grid=(nchip, rows // PACK_ROW_BLOCK),
            in_specs=[pl.BlockSpec((None, None, PACK_ROW_BLOCK, lanes), lambda p, i, c: (p, c[0], i, 0)), blk],
            out_specs=[blk, blk]),
        out_shape=[jax.ShapeDtypeStruct((nchip, rows, lanes), F32), jax.ShapeDtypeStruct((nchip, rows, lanes), BF16)],
        compiler_params=_params(("parallel", "parallel")),
    )(core, pack, recv)


class _ChipScatter:
    def __init__(self, part16):
        nchip, rows, lanes = part16.shape
        self.operands = [part16]
        self.out_shape = [jax.ShapeDtypeStruct((nchip - 1, rows, lanes), part16.dtype)]
        self.scratch = [pltpu.SemaphoreType.DMA((nchip - 1,))] * 2

    def _copies(self, ins, outs, sems):
        x, y, c = _mesh_pos()
        return [pltpu.make_async_remote_copy(ins[0].at[2 * px + py], outs[0].at[k], sems[0].at[k], sems[1].at[k],
                                             device_id=(px, py, c), device_id_type=MESH)
                for k, (px, py) in enumerate(_other_chips(x, y))]

    def start(self, ins, outs, sems):
        for cp in self._copies(ins, outs, sems):
            cp.start()

    def finish(self, ins, outs, sems):
        for cp in self._copies(ins, outs, sems):
            cp.wait()

    def finalize(self, results):
        return results[0]

    def run(self, name):
        def body(src_ref, recv_ref, send_sems, recv_sems):
            self.start([src_ref], [recv_ref], [send_sems, recv_sems])
            self.finish([src_ref], [recv_ref], [send_sems, recv_sems])

        return pl.pallas_call(
            body, name=name, in_specs=[ANY], out_specs=ANY, out_shape=self.out_shape[0], scratch_shapes=self.scratch,
            compiler_params=pltpu.CompilerParams(has_side_effects=True),
        )(*self.operands)


def _sum_chips(part32, recv16, chip, name):
    nchip, rows, lanes = part32.shape

    def body(p_ref, own_ref, r_ref, o_ref):
        s = own_ref[...]
        for k in range(nchip - 1):
            s = s + r_ref[k].astype(F32)
        o_ref[...] = s

    return pl.pallas_call(
        body, name=name,
        grid_spec=pltpu.PrefetchScalarGridSpec(
            num_scalar_prefetch=1, grid=(rows // PACK_ROW_BLOCK,),
            in_specs=[pl.BlockSpec((None, PACK_ROW_BLOCK, lanes), lambda i, p: (p[0], i, 0)),
                      pl.BlockSpec((nchip - 1, PACK_ROW_BLOCK, lanes), lambda i, p: (0, i, 0))],
            out_specs=pl.BlockSpec((PACK_ROW_BLOCK, lanes), lambda i, p: (i, 0))),
        out_shape=jax.ShapeDtypeStruct((rows, lanes), F32),
        compiler_params=_params(("parallel",)),
    )(chip, part32, recv16)


def _share_with_sibling(half, name):
    rows, lanes = half.shape

    def body(h_ref, other_ref, send_sem, recv_sem):
        x, y, c = _mesh_pos()
        cp = pltpu.make_async_remote_copy(h_ref, other_ref, send_sem, recv_sem,
                                          device_id=(x, y, 1 - c), device_id_type=MESH)
        cp.start()
        cp.wait()

    return pl.pallas_call(
        body, name=name, in_specs=[ANY], out_specs=ANY,
        out_shape=jax.ShapeDtypeStruct((rows, lanes), half.dtype),
        scratch_shapes=[pltpu.SemaphoreType.DMA, pltpu.SemaphoreType.DMA],
        compiler_params=pltpu.CompilerParams(has_side_effects=True),
    )(half)


class _GradRound:
    def __init__(self, tag, pieces):
        self.tag, self.keys = tag, list(pieces)
        self.shapes = [pieces[k].shape[1:] for k in self.keys]
        _, _, c = _mesh_pos()
        pack = _pack_by_half([pieces[k] for k in self.keys])
        recv = _swap_with_sibling(pack, f"grad_{tag}_pair_swap")
        self.part32, part16 = _add_pair(pack, recv, jnp.reshape(c, (1,)).astype(jnp.int32), f"grad_{tag}_pair_add")
        self.scatter = _ChipScatter(part16)

    def finish(self, recv16):
        x, y, c = _mesh_pos()
        chip = jnp.reshape(2 * x + y, (1,)).astype(jnp.int32)
        half = _sum_chips(self.part32, recv16, chip, f"grad_{self.tag}_chip_sum")
        other = _share_with_sibling(half, f"grad_{self.tag}_pair_share")
        return dict(zip(self.keys, _unpack_halves(half, other, c, self.shapes)))


def _pack_small(vals):
    parts = []
    for n in REPLICATED:
        flat = vals[n].reshape(-1)
        rows = -(-flat.shape[0] // 128)
        rows = -(-rows // 8) * 8
        parts.append(jnp.pad(flat, (0, rows * 128 - flat.shape[0])).reshape(rows, 128))
    return jnp.concatenate(parts, axis=0)


def _unpack_small(pack, like):
    out, off = {}, 0
    for n in REPLICATED:
        size = like[n].size
        rows = -(-size // 128)
        rows = -(-rows // 8) * 8
        out[n] = pack[off:off + rows].reshape(-1)[:size].reshape(like[n].shape)
        off += rows
    return out


def _all_to_all_small(pack, name):
    rows, lanes = pack.shape
    flips = [(dx, dy, dc) for dx in (0, 1) for dy in (0, 1) for dc in (0, 1)][1:]

    def body(src_ref, out_ref, send_sems, recv_sems, local_sem):
        x, y, c = _mesh_pos()
        me = 4 * x + 2 * y + c
        loc = pltpu.make_async_copy(src_ref, out_ref.at[me], local_sem)
        loc.start()
        copies = []
        for k, (dx, dy, dc) in enumerate(flips):
            peer = (x ^ dx, y ^ dy, c ^ dc)
            cp = pltpu.make_async_remote_copy(src_ref, out_ref.at[me], send_sems.at[k], recv_sems.at[k],
                                              device_id=peer, device_id_type=MESH)
            cp.start()
            copies.append(cp)
        for k, (dx, dy, dc) in enumerate(flips):
            peer = (x ^ dx, y ^ dy, c ^ dc)
            pltpu.make_async_remote_copy(src_ref, out_ref.at[4 * peer[0] + 2 * peer[1] + peer[2]], send_sems.at[k],
                                         recv_sems.at[k], device_id=peer, device_id_type=MESH).wait_recv()
        for cp in copies:
            cp.wait_send()
        loc.wait()

    return pl.pallas_call(
        body, name=name, in_specs=[ANY], out_specs=ANY,
        out_shape=jax.ShapeDtypeStruct((8, rows, lanes), pack.dtype),
        scratch_shapes=[pltpu.SemaphoreType.DMA((7,)), pltpu.SemaphoreType.DMA((7,)), pltpu.SemaphoreType.DMA],
        compiler_params=pltpu.CompilerParams(has_side_effects=True),
    )(pack)


def _sum_devices(stack, name):
    ndev, rows, lanes = stack.shape

    def body(s_ref, o_ref):
        s = s_ref[0]
        for d in range(1, ndev):
            s = s + s_ref[d]
        o_ref[...] = s

    return pl.pallas_call(
        body, name=name, grid=(1,),
        in_specs=[pl.BlockSpec((ndev, rows, lanes), lambda i: (0, 0, 0))],
        out_specs=pl.BlockSpec((rows, lanes), lambda i: (0, 0)),
        out_shape=jax.ShapeDtypeStruct((rows, lanes), F32),
    )(stack)


WEIGHT_ORDER = ("ffn_norm1", "ffn1_w_in", "ffn1_w_out", "mix_norm", "ffn_norm2", "ffn2_w_in", "ffn2_w_out", "ab_w_in",
                "dn_conv_w", "dn_a_log", "dn_dt_bias", "dn_out_norm", "sg_norm", "sg_w", "sg_b", "ab_w_out", "pool_w",
                "pool_scale", "final_norm")
MATRICES = ("ffn1_w_in", "ffn1_w_out", "ffn2_w_in", "ffn2_w_out", "ab_w_in", "ab_w_out", "pool_w")
GATHER_FIRST = (("ffn1_w_in", 0), ("ffn1_w_out", 0), ("ab_w_in", None), ("dn_conv_w", None), ("ab_w_out", None))
GATHER_LATER = (("ffn2_w_in", 0), ("ffn2_w_out", 0), ("ffn1_w_in", 1), ("ffn1_w_out", 1), ("pool_w", None),
                ("pool_scale", None), ("ffn2_w_in", 1), ("ffn2_w_out", 1))
REDUCE_ROUNDS = ((("ffn2_w_in", 1), ("ffn2_w_out", 1), ("ffn1_w_in", 1), ("ffn1_w_out", 1), ("pool_w", None), ("pool_scale", None)),
                 (("ffn2_w_in", 0), ("ffn2_w_out", 0)),
                 (("ffn1_w_in", 0), ("ffn1_w_out", 0), ("ab_w_out", None), ("ab_w_in", None), ("dn_conv_w", None)))


def _as_2d(a):
    return a.reshape(-1, a.shape[-1])


def kernel(x, ffn_norm1, ffn1_w_in, ffn1_w_out, mix_norm, ffn_norm2, ffn2_w_in, ffn2_w_out, ab_w_in, dn_conv_w, dn_a_log, dn_dt_bias, dn_out_norm, sg_norm, sg_w, sg_b, ab_w_out, pool_w, pool_scale, final_norm, loss_target, m_ffn_norm1, m_ffn1_w_in, m_ffn1_w_out, m_mix_norm, m_ffn_norm2, m_ffn2_w_in, m_ffn2_w_out, m_ab_w_in, m_dn_conv_w, m_dn_a_log, m_dn_dt_bias, m_dn_out_norm, m_sg_norm, m_sg_w, m_sg_b, m_ab_w_out, m_pool_w, m_pool_scale, m_final_norm, v_ffn_norm1, v_ffn1_w_in, v_ffn1_w_out, v_mix_norm, v_ffn_norm2, v_ffn2_w_in, v_ffn2_w_out, v_ab_w_in, v_dn_conv_w, v_dn_a_log, v_dn_dt_bias, v_dn_out_norm, v_sg_norm, v_sg_w, v_sg_b, v_ab_w_out, v_pool_w, v_pool_scale, v_final_norm):
    given = dict(locals())
    wts = {n: given[n] for n in WEIGHT_ORDER}
    mom_m = {n: given["m_" + n] for n in WEIGHT_ORDER}
    mom_v = {n: given["v_" + n] for n in WEIGHT_ORDER}

    rep = {n: wts[n] for n in REPLICATED}
    first = _ChipGather(_shard_pieces(wts, GATHER_FIRST), [n in MATRICES for n, _ in GATHER_FIRST])
    w = {**_replicated_layouts(rep), **_layouts_from(dict(zip(GATHER_FIRST, first.run("weight_gather_first"))))}
    later = _ChipGather(_shard_pieces(wts, GATHER_LATER), [n in MATRICES for n, _ in GATHER_LATER])

    loss, dx, g, reduced = _local_step(x[0], loss_target[0], w, reduce=True,
                                       late=(later, lambda arrived: _layouts_from(dict(zip(GATHER_LATER, arrived)))))
    g_rep = _replicated_grads(g)
    grads = {}
    for n in SHARDED:
        layers = [reduced[(n, layer)] for layer in range(wts[n].shape[0])] if (n, 0) in reduced else [reduced[(n, None)]]
        grads[n] = jnp.stack(layers).reshape(wts[n].shape)
    small = _sum_devices(_all_to_all_small(_pack_small(g_rep), "grad_small_exchange"), "grad_small_sum")
    grads.update(_unpack_small(small, rep))

    delta, new_m, new_v = {}, {}, {}
    for n in SHARDED:
        d, m1, v1 = _adamw(_as_2d(wts[n]), _as_2d(grads[n]), _as_2d(mom_m[n]), _as_2d(mom_v[n]), f"adamw_{n}")
        delta[n], new_m[n], new_v[n] = (t.reshape(wts[n].shape) for t in (d, m1, v1))
    d, m1, v1 = _adamw(_pack_small(rep), small, _pack_small({n: mom_m[n] for n in REPLICATED}),
                       _pack_small({n: mom_v[n] for n in REPLICATED}), "adamw_replicated")
    for tgt, packed in ((delta, d), (new_m, m1), (new_v, v1)):
        tgt.update(_unpack_small(packed, rep))

    total = lax.psum(loss[0, 0], ("x", "y", "c"))
    outs = [total, dx[None]]
    for group in (grads, delta, new_m, new_v):
        outs.extend(group[n] for n in WEIGHT_ORDER)
    return tuple(outs)
```

```python
import functools

import jax
import jax.numpy as jnp
from jax import lax
from jax.experimental import pallas as pl
from jax.experimental.pallas import tpu as pltpu

F32, BF16 = jnp.float32, jnp.bfloat16
NORM_EPS = 1e-6
D_MODEL = 1024
D_FF = 2816
N_CHIPS = 4
FF_CHUNK = 2 * D_FF // N_CHIPS
DN_HEADS, DN_DIM, DN_CHUNK, DN_CONV = 4, 128, 64, 4
DN_BLOCK = 2 * DN_CHUNK
DN_PREP_CHUNKS = 4
DN_SCAN_CHUNKS = 8
SG_GROUPS, SG_DIM, SG_CHUNK = 4, 128, 128
POOL_WINDOWS = (2, 4, 8, 16)
POOL_DIM = 256
POOL_HALO = 16
CONV_HALO = 8
PROJ_W = 3200
BA_BLOCK = 3072 // 128
ADAM_LR, ADAM_B1, ADAM_B2, ADAM_EPS, ADAM_WD, ADAM_STEP = 0.001, 0.9, 0.999, 1e-08, 0.01, 10
VMEM_BIG = 52 * 1024 * 1024
FFN_FWD_ROWS = 512
FFN_BWD_ROWS = 256
PACK_LANES = 1024
PACK_ROW_BLOCK = 256
MESH = pl.DeviceIdType.MESH
HI = lax.Precision.HIGHEST
ANY = pl.BlockSpec(memory_space=pl.ANY)


def _params(sem=None, vmem=None):
    return pltpu.CompilerParams(dimension_semantics=sem, vmem_limit_bytes=vmem)


def _dot(a, b):
    return jnp.dot(a, b, preferred_element_type=F32)


def _dot_nt(a, b):
    return lax.dot_general(a, b, (((1,), (1,)), ((), ())), preferred_element_type=F32)


def _dot_tn(a, b):
    return lax.dot_general(a, b, (((0,), (0,)), ((), ())), preferred_element_type=F32)


def _dot_hi(a, b):
    return jnp.dot(a, b, preferred_element_type=F32, precision=HI)


def _dot_mid(a, b):
    return jnp.dot(a, b, preferred_element_type=F32, precision=lax.Precision.HIGH)


def _bf(a):
    return a.astype(BF16)


def _rms_stats(x):
    r = lax.rsqrt(jnp.mean(x * x, axis=-1, keepdims=True) + NORM_EPS)
    return x * r, r


def _rms_bwd(dh, xhat, r, w):
    dhn = dh * w
    return r * (dhn - xhat * jnp.mean(dhn * xhat, axis=-1, keepdims=True))


def _sigmoid(x):
    return jax.nn.sigmoid(x)


def _silu_grad(x, s):
    return s * (1.0 + x * (1.0 - s))


def _gelu(x):
    return 0.5 * x * (1.0 + lax.erf(x * 0.7071067811865476))


def _gelu_grad(x):
    return 0.5 * (1.0 + lax.erf(x * 0.7071067811865476)) + x * jnp.exp(-0.5 * x * x) * 0.3989422804014327


def _softplus(x):
    return jnp.maximum(x, 0.0) + jnp.log(1.0 + jnp.exp(-jnp.abs(x)))


def _tile(n, pref):
    t = min(n, pref)
    assert n % t == 0, (n, t)
    return t


def _ffn_weight_specs():
    once = pl.Buffered(1)
    return [pl.BlockSpec((N_CHIPS, D_MODEL, FF_CHUNK), lambda i: (0, 0, 0), pipeline_mode=once),
            pl.BlockSpec((N_CHIPS, D_FF // N_CHIPS, D_MODEL), lambda i: (0, 0, 0), pipeline_mode=once)]


def _ffn_fwd(x, nw, win, wout, layer, name):
    T, D = x.shape
    tm = _tile(T, FFN_FWD_ROWS)
    nj = N_CHIPS // 2

    def body(x_ref, n_ref, win_ref, wo_ref, xo_ref, g_ref, u_ref, hb_ref):
        xv = x_ref[...]
        xhat, _ = _rms_stats(xv)
        h = _bf(xhat * n_ref[...])
        hb_ref[...] = h
        acc = None
        for j in range(nj):
            cols = slice(j * FF_CHUNK, (j + 1) * FF_CHUNK)
            g = _dot(h, win_ref[j])
            u = _dot(h, win_ref[nj + j])
            g_ref[:, cols] = _bf(g)
            u_ref[:, cols] = _bf(u)
            part = _dot(_bf(g * _sigmoid(g) * u), wo_ref[2 * j:2 * j + 2].reshape(FF_CHUNK, D))
            acc = part if acc is None else acc + part
        xo_ref[...] = xv + 0.5 * acc

    row = pl.BlockSpec((tm, D), lambda i: (i, 0))
    wide = pl.BlockSpec((tm, D_FF), lambda i: (i, 0))
    return pl.pallas_call(
        body, name=name, grid=(T // tm,),
        in_specs=[row, pl.BlockSpec((None, 1, D), lambda i: (layer, 0, 0))] + _ffn_weight_specs(),
        out_specs=[row, wide, wide, row],
        out_shape=[jax.ShapeDtypeStruct((T, D), F32), jax.ShapeDtypeStruct((T, D_FF), BF16),
                   jax.ShapeDtypeStruct((T, D_FF), BF16), jax.ShapeDtypeStruct((T, D), BF16)],
        compiler_params=_params(("parallel",), VMEM_BIG),
    )(x, nw, win, wout)


def _ffn_bwd(dxo, x, nw, g, u, win, wout, layer, name, exchange=None):
    T, D = x.shape
    tm = _tile(T, FFN_BWD_ROWS)
    nj = N_CHIPS // 2

    def body(dxo_ref, x_ref, n_ref, g_ref, u_ref, win_ref, wo_ref, dx_ref, dg_ref, du_ref, a_ref, dyb_ref, dn_ref):
        @pl.when(pl.program_id(0) == 0)
        def _():
            dn_ref[...] = jnp.zeros_like(dn_ref)

        dxo = dxo_ref[...]
        dyb = _bf(0.5 * dxo)
        dyb_ref[...] = dyb
        dh = None
        for j in range(nj):
            cols = slice(j * FF_CHUNK, (j + 1) * FF_CHUNK)
            da = _dot_nt(dyb, wo_ref[2 * j:2 * j + 2].reshape(FF_CHUNK, D))
            gv = g_ref[:, cols].astype(F32)
            uv = u_ref[:, cols].astype(F32)
            sg = _sigmoid(gv)
            sl = gv * sg
            dgb = _bf(da * uv * _silu_grad(gv, sg))
            dub = _bf(da * sl)
            a_ref[:, cols] = _bf(sl * uv)
            dg_ref[:, cols] = dgb
            du_ref[:, cols] = dub
            part = _dot_nt(dgb, win_ref[j]) + _dot_nt(dub, win_ref[nj + j])
            dh = part if dh is None else dh + part
        xhat, r = _rms_stats(x_ref[...])
        dx_ref[...] = dxo + _rms_bwd(dh, xhat, r, n_ref[...])
        dn_ref[...] += jnp.sum(dh * xhat, axis=0, keepdims=True)

    row = pl.BlockSpec((tm, D), lambda i: (i, 0))
    wide = pl.BlockSpec((tm, D_FF), lambda i: (i, 0))
    return _call_with_exchange(
        body, exchange, name=name, steps=T // tm, vmem=VMEM_BIG,
        in_specs=[row, row, pl.BlockSpec((None, 1, D), lambda i: (layer, 0, 0)), wide, wide] + _ffn_weight_specs(),
        out_specs=[row, wide, wide, wide, row, pl.BlockSpec((1, D), lambda i: (0, 0))],
        out_shape=[jax.ShapeDtypeStruct((T, D), F32), jax.ShapeDtypeStruct((T, D_FF), BF16),
                   jax.ShapeDtypeStruct((T, D_FF), BF16), jax.ShapeDtypeStruct((T, D_FF), BF16),
                   jax.ShapeDtypeStruct((T, D), BF16), jax.ShapeDtypeStruct((1, D), F32)],
        args=(dxo, x, nw, g, u, win, wout))


def _matmul_tn(a, b, bm, bn, name, stack_n=False):
    T, M = a.shape
    N = b.shape[1]
    tk = _tile(T, 1024)
    bm, bn = _tile(M, bm), _tile(N, bn)

    def body(a_ref, b_ref, o_ref):
        @pl.when(pl.program_id(2) == 0)
        def _():
            o_ref[...] = jnp.zeros_like(o_ref)

        o_ref[...] += _dot_tn(_bf(a_ref[...]), _bf(b_ref[...]))

    if stack_n:
        out_spec = pl.BlockSpec((None, bm, bn), lambda m, n, k: (n, m, 0))
        out_shape = jax.ShapeDtypeStruct((N // bn, M, bn), F32)
    else:
        out_spec = pl.BlockSpec((bm, bn), lambda m, n, k: (m, n))
        out_shape = jax.ShapeDtypeStruct((M, N), F32)
    return pl.pallas_call(
        body, name=name, grid=(M // bm, N // bn, T // tk),
        in_specs=[pl.BlockSpec((tk, bm), lambda m, n, k: (k, m)),
                  pl.BlockSpec((tk, bn), lambda m, n, k: (k, n))],
        out_specs=out_spec, out_shape=out_shape,
        compiler_params=_params(("parallel", "parallel", "arbitrary"), VMEM_BIG),
    )(a, b)


def _matmul(a, b, name, trans_b=False, res=None, out_dtype=F32):
    T, K = a.shape
    N = b.shape[0] if trans_b else b.shape[1]
    tm = _tile(T, 512)

    def body(*refs):
        a_ref, b_ref = refs[0], refs[1]
        o_ref = refs[-1]
        av, bv = _bf(a_ref[...]), _bf(b_ref[...])
        acc = _dot_nt(av, bv) if trans_b else _dot(av, bv)
        if res is not None:
            acc = acc + refs[2][...]
        o_ref[...] = acc.astype(out_dtype)

    in_specs = [pl.BlockSpec((tm, K), lambda i: (i, 0)), pl.BlockSpec(b.shape, lambda i: (0, 0))]
    args = [a, b]
    if res is not None:
        in_specs.append(pl.BlockSpec((tm, N), lambda i: (i, 0)))
        args.append(res)
    return pl.pallas_call(
        body, name=name, grid=(T // tm,), in_specs=in_specs,
        out_specs=pl.BlockSpec((tm, N), lambda i: (i, 0)),
        out_shape=jax.ShapeDtypeStruct((T, N), out_dtype),
        compiler_params=_params(("parallel",), VMEM_BIG),
    )(*args)


def _rms_fwd_call(x, nw, layer, name):
    T, D = x.shape
    tm = _tile(T, 512)

    def body(x_ref, n_ref, o_ref):
        xhat, _ = _rms_stats(x_ref[...])
        o_ref[...] = _bf(xhat * n_ref[...])

    return pl.pallas_call(
        body, name=name, grid=(T // tm,),
        in_specs=[pl.BlockSpec((tm, D), lambda i: (i, 0)), pl.BlockSpec((None, 1, D), lambda i: (layer, 0, 0))],
        out_specs=pl.BlockSpec((tm, D), lambda i: (i, 0)),
        out_shape=jax.ShapeDtypeStruct((T, D), BF16),
        compiler_params=_params(("parallel",)),
    )(x, nw)


def _rms_bwd_call(dh, x, nw, dres, layer, name):
    T, D = x.shape
    tm = _tile(T, 512)

    def body(dh_ref, x_ref, n_ref, dr_ref, dx_ref, dn_ref):
        @pl.when(pl.program_id(0) == 0)
        def _():
            dn_ref[...] = jnp.zeros_like(dn_ref)

        xhat, r = _rms_stats(x_ref[...])
        dh_v = dh_ref[...]
        dx_ref[...] = dr_ref[...] + _rms_bwd(dh_v, xhat, r, n_ref[...])
        dn_ref[...] += jnp.sum(dh_v * xhat, axis=0, keepdims=True)

    row = pl.BlockSpec((tm, D), lambda i: (i, 0))
    return pl.pallas_call(
        body, name=name, grid=(T // tm,),
        in_specs=[row, row, pl.BlockSpec((None, 1, D), lambda i: (layer, 0, 0)), row],
        out_specs=[row, pl.BlockSpec((1, D), lambda i: (0, 0))],
        out_shape=[jax.ShapeDtypeStruct((T, D), F32), jax.ShapeDtypeStruct((1, D), F32)],
        compiler_params=_params(("arbitrary",)),
    )(dh, x, nw, dres)


def _shift_rows(x, s):
    n = x.shape[0]
    s = s % n
    return x if s == 0 else pltpu.roll(x, s, 0)


def _conv_fwd(proj, conv_w, name):
    T = proj.shape[0]
    C = 3 * DN_HEADS * DN_DIM
    cb = 512
    tm = _tile(T, 512)
    hb = tm // CONV_HALO

    def body(x_ref, xp_ref, w_ref, o_ref):
        i = pl.program_id(1)
        prev = jnp.where(i == 0, 0.0, xp_ref[...])
        ext = jnp.concatenate([prev, x_ref[...]], axis=0)
        w = w_ref[...]
        y = ext * w[DN_CONV - 1:DN_CONV, :]
        for k in range(DN_CONV - 1):
            y = y + _shift_rows(ext, DN_CONV - 1 - k) * w[k:k + 1, :]
        y = y[CONV_HALO:, :]
        o_ref[...] = y * _sigmoid(y)

    return pl.pallas_call(
        body, name=name, grid=(C // cb, T // tm),
        in_specs=[pl.BlockSpec((tm, cb), lambda c, i: (i, c)),
                  pl.BlockSpec((CONV_HALO, cb), lambda c, i: (jnp.maximum(i * hb - 1, 0), c)),
                  pl.BlockSpec((DN_CONV, cb), lambda c, i: (0, c))],
        out_specs=pl.BlockSpec((tm, cb), lambda c, i: (i, c)),
        out_shape=jax.ShapeDtypeStruct((T, C), F32),
        compiler_params=_params(("parallel", "parallel")),
    )(proj, proj, conv_w)


def _conv_bwd(dy, proj, conv_w, name):
    T = proj.shape[0]
    C = 3 * DN_HEADS * DN_DIM
    cb = 512
    tm = _tile(T, 512)
    hb = tm // CONV_HALO
    nt = T // tm

    def body(x_ref, xp_ref, xn_ref, dy_ref, dyn_ref, w_ref, dx_ref, dw_ref):
        i = pl.program_id(1)

        @pl.when(i == 0)
        def _():
            dw_ref[...] = jnp.zeros_like(dw_ref)

        prev = jnp.where(i == 0, 0.0, xp_ref[...])
        ext = jnp.concatenate([prev, x_ref[...], xn_ref[...]], axis=0)
        dy_ext = jnp.concatenate([jnp.zeros((CONV_HALO, cb), F32), dy_ref[...],
                                  jnp.where(i == nt - 1, 0.0, dyn_ref[...])], axis=0)
        w = w_ref[...]
        shifted = [_shift_rows(ext, DN_CONV - 1 - k) for k in range(DN_CONV)]
        y = shifted[0] * w[0:1, :]
        for k in range(1, DN_CONV):
            y = y + shifted[k] * w[k:k + 1, :]
        s = _sigmoid(y)
        dpre = dy_ext * _silu_grad(y, s)
        dx = dpre * w[DN_CONV - 1:DN_CONV, :]
        for k in range(DN_CONV - 1):
            dx = dx + _shift_rows(dpre, -(DN_CONV - 1 - k)) * w[k:k + 1, :]
        dx_ref[...] = _bf(dx[CONV_HALO:CONV_HALO + tm, :])
        rows = [jnp.sum((dpre * shifted[k])[CONV_HALO:CONV_HALO + tm, :], axis=0, keepdims=True) for k in range(DN_CONV)]
        dw_ref[...] += jnp.concatenate(rows, axis=0)

    last_halo = T // CONV_HALO - 1
    return pl.pallas_call(
        body, name=name, grid=(C // cb, nt),
        in_specs=[pl.BlockSpec((tm, cb), lambda c, i: (i, c)),
                  pl.BlockSpec((CONV_HALO, cb), lambda c, i: (jnp.maximum(i * hb - 1, 0), c)),
                  pl.BlockSpec((CONV_HALO, cb), lambda c, i: (jnp.minimum((i + 1) * hb, last_halo), c)),
                  pl.BlockSpec((tm, cb), lambda c, i: (i, c)),
                  pl.BlockSpec((CONV_HALO, cb), lambda c, i: (jnp.minimum((i + 1) * hb, last_halo), c)),
                  pl.BlockSpec((DN_CONV, cb), lambda c, i: (0, c))],
        out_specs=[pl.BlockSpec((tm, cb), lambda c, i: (i, c)),
                   pl.BlockSpec((DN_CONV, cb), lambda c, i: (0, c))],
        out_shape=[jax.ShapeDtypeStruct((T, C), BF16), jax.ShapeDtypeStruct((DN_CONV, C), F32)],
        compiler_params=_params(("parallel", "arbitrary")),
    )(proj, proj, proj, dy, dy, conv_w)


def _unit_lower_inverse(low, eye):
    p1 = -low
    p2 = _dot_mid(p1, p1)
    p4 = _dot_mid(p2, p2)
    a = eye + p1 + p2 + _dot_mid(p1, p2)
    p8 = _dot_mid(p4, p4)
    p16 = _dot_mid(p8, p8)
    b = eye + p4 + p8 + _dot_mid(p4, p8)
    p32 = _dot_mid(p16, p16)
    ab = _dot_mid(a, b)
    c = eye + p16 + p32 + _dot_mid(p16, p32)
    return _dot_mid(ab, c)


def _l2_unit(x):
    r = lax.rsqrt(jnp.sum(x * x, axis=-1, keepdims=True) + NORM_EPS)
    return x * r, r


class _BlockMasks:
    def __init__(self):
        n = DN_BLOCK
        row = lax.broadcasted_iota(jnp.int32, (n, n), 0)
        col = lax.broadcasted_iota(jnp.int32, (n, n), 1)
        same = (row // DN_CHUNK) == (col // DN_CHUNK)
        self.lower, self.strict_lower = same & (row >= col), same & (row > col)
        self.upper, self.strict_upper = same & (row <= col), same & (row < col)
        self.eye = (row == col).astype(F32)
        self.first = lax.broadcasted_iota(jnp.int32, (n, 1), 0) < DN_CHUNK


def _dn_gates(ba, hp):
    coef = -jnp.exp(hp[0:1, :])
    pre = ba + hp[1:2, :]
    return _sigmoid(ba), coef * _softplus(pre), coef, pre


def _dn_block_gates(mk, ba, hp):
    assert DN_BLOCK == 2 * DN_CHUNK
    beta_t, graw_t, coef, pre = _dn_gates(ba, hp)
    gcum_t = _dot_hi(mk.lower.astype(F32), graw_t)
    gl_t = jnp.where(mk.first, gcum_t[DN_CHUNK - 1:DN_CHUNK, :], gcum_t[DN_BLOCK - 1:DN_BLOCK, :])
    return beta_t, gcum_t, gl_t, graw_t, coef, pre


def _dn_local(mk, qraw, kraw, bc, gc, gl):
    f = {}
    f["qn"], f["rq"] = _l2_unit(qraw)
    qh = f["qn"] * (DN_DIM ** -0.5)
    kh, f["rk"] = _l2_unit(kraw)
    gr = jnp.broadcast_to(gc, (DN_BLOCK, DN_BLOCK)).T
    dec = jnp.where(mk.lower, jnp.exp(jnp.where(mk.lower, gc - gr, 0.0)), 0.0)
    kb = kh * bc
    mkk = _dot_nt(_bf(kb), _bf(kh))
    eg = jnp.exp(gc)
    mqk = _dot_nt(_bf(qh), _bf(kh))
    etl = jnp.exp(gl - gc)
    f.update(qh=qh, kh=kh, gr=gr, dec=dec, kb=kb, mkk=mkk, eg=eg, mqk=mqk, attn=mqk * dec, etl=etl, qd=qh * eg, kt=kh * etl)
    return f


def _dn_specs(rows, rev=None):
    at = (lambda n: n) if rev is None else rev
    hw = DN_HEADS * DN_DIM
    return dict(
        qkv=[pl.BlockSpec((rows, hw), lambda n, j=j: (at(n), j)) for j in range(3)],
        ba=pl.BlockSpec((rows, 128), lambda n: (at(n), BA_BLOCK)),
        hp=pl.BlockSpec((8, 128), lambda n: (0, 0)),
        tok=pl.BlockSpec((rows, hw), lambda n: (at(n), 0)),
        attn=pl.BlockSpec((rows, DN_HEADS * DN_CHUNK), lambda n: (at(n), 0)),
        gate=pl.BlockSpec((rows // DN_CHUNK, 8, 128), lambda n: (at(n), 0, 0)),
        state=pl.BlockSpec((rows // DN_CHUNK, DN_HEADS, DN_DIM, DN_DIM), lambda n: (at(n), 0, 0, 0)),
    )


def _call_with_exchange(body, exchange, *, name, steps, in_specs, out_specs, out_shape, args, vmem=None):
    if exchange is None:
        res = pl.pallas_call(body, name=name, grid=(steps,), in_specs=in_specs, out_specs=out_specs, out_shape=out_shape,
                             compiler_params=_params(("arbitrary",), vmem))(*args)
        return list(res), None
    n_in, n_out, m = len(in_specs), len(out_specs), len(exchange.operands)

    def hosted(*refs):
        ins, ex_ins = refs[:n_in], refs[n_in:n_in + m]
        outs, ex_outs = refs[n_in + m:n_in + m + n_out], refs[n_in + m + n_out:n_in + 2 * m + n_out]
        sems = refs[n_in + 2 * m + n_out:]

        @pl.when(pl.program_id(0) == 0)
        def _():
            exchange.start(ex_ins, ex_outs, sems)

        body(*ins, *outs)

        @pl.when(pl.program_id(0) == steps - 1)
        def _():
            exchange.finish(ex_ins, ex_outs, sems)

    res = pl.pallas_call(
        hosted, name=name, grid=(steps,), in_specs=list(in_specs) + [ANY] * m, out_specs=list(out_specs) + [ANY] * m,
        out_shape=list(out_shape) + list(exchange.out_shape), scratch_shapes=exchange.scratch,
        compiler_params=pltpu.CompilerParams(dimension_semantics=("arbitrary",), vmem_limit_bytes=vmem, has_side_effects=True),
    )(*args, *exchange.operands)
    return list(res[:n_out]), exchange.finalize(list(res[n_out:]))


def _dn_prep(qkv, proj, hp, name, exchange=None):
    T = qkv.shape[0]
    n_chunks = T // DN_CHUNK
    blocks = max(1, min(DN_PREP_CHUNKS, n_chunks) * DN_CHUNK // DN_BLOCK)
    group = blocks * DN_BLOCK // DN_CHUNK
    rows = blocks * DN_BLOCK
    hw = DN_HEADS * DN_DIM

    def body(q_ref, k_ref, v_ref, ba_ref, hp_ref, u_ref, w_ref, p_ref, qd_ref, kt_ref, gl_ref, inv_ref):
        mk = _BlockMasks()
        hp_v = hp_ref[...]
        for j in range(blocks):
            rs = slice(j * DN_BLOCK, (j + 1) * DN_BLOCK)
            beta_t, gcum_t, gl_t = _dn_block_gates(mk, ba_ref[rs, :], hp_v)[:3]
            for c in range(DN_BLOCK // DN_CHUNK):
                gl_ref[j * (DN_BLOCK // DN_CHUNK) + c] = jnp.broadcast_to(gl_t[c * DN_CHUNK:c * DN_CHUNK + 1, :], (8, 128))
            for h in range(DN_HEADS):
                sl = slice(h * DN_DIM, (h + 1) * DN_DIM)
                gate = slice(DN_HEADS + h, DN_HEADS + h + 1)
                bc = beta_t[:, h:h + 1]
                f = _dn_local(mk, q_ref[rs, sl], k_ref[rs, sl], bc, gcum_t[:, gate], gl_t[:, gate])
                inv = _unit_lower_inverse(jnp.where(mk.strict_lower, f["mkk"] * f["dec"], 0.0), mk.eye)
                inv_ref[rs, sl] = inv
                sol = _dot_mid(inv, jnp.concatenate([v_ref[rs, sl] * bc, f["kb"] * f["eg"]], axis=1))
                u_ref[rs, sl] = sol[:, :DN_DIM]
                w_ref[rs, sl] = _bf(sol[:, DN_DIM:])
                for c in range(DN_BLOCK // DN_CHUNK):
                    cr = slice(c * DN_CHUNK, (c + 1) * DN_CHUNK)
                    p_ref[j * DN_BLOCK + c * DN_CHUNK:j * DN_BLOCK + (c + 1) * DN_CHUNK, h * DN_CHUNK:(h + 1) * DN_CHUNK] = _bf(f["attn"][cr, cr])
                qd_ref[rs, sl] = _bf(f["qd"])
                kt_ref[rs, sl] = _bf(f["kt"])

    sp = _dn_specs(rows)
    tok16 = jax.ShapeDtypeStruct((T, hw), BF16)
    return _call_with_exchange(
        body, exchange, name=name, steps=n_chunks // group,
        in_specs=sp["qkv"] + [sp["ba"], sp["hp"]],
        out_specs=[sp["tok"], sp["tok"], sp["attn"], sp["tok"], sp["tok"], sp["gate"], sp["tok"]],
        out_shape=[jax.ShapeDtypeStruct((T, hw), F32), tok16, jax.ShapeDtypeStruct((T, DN_HEADS * DN_CHUNK), BF16),
                   tok16, tok16, jax.ShapeDtypeStruct((n_chunks, 8, 128), F32), jax.ShapeDtypeStruct((T, hw), F32)],
        args=(qkv, qkv, qkv, proj, hp))


def _dn_scan(u, w, p, qd, kt, gl, name):
    T = u.shape[0]
    n_chunks = T // DN_CHUNK
    group = min(DN_SCAN_CHUNKS, n_chunks)
    rows = group * DN_CHUNK
    hw = DN_HEADS * DN_DIM

    def body(u_ref, w_ref, p_ref, qd_ref, kt_ref, gl_ref, o_ref, vn_ref, sall_ref, s_s):
        @pl.when(pl.program_id(0) == 0)
        def _():
            s_s[...] = jnp.zeros_like(s_s)

        state = [s_s[h] for h in range(DN_HEADS)]
        for j in range(group):
            rs = slice(j * DN_CHUNK, (j + 1) * DN_CHUNK)
            for h in range(DN_HEADS):
                sl = slice(h * DN_DIM, (h + 1) * DN_DIM)
                sall_ref[j, h] = state[h]
                sb = _bf(state[h])
                vnb = _bf(u_ref[rs, sl] - _dot(w_ref[rs, sl], sb))
                vn_ref[rs, sl] = vnb
                o_ref[rs, sl] = _dot(qd_ref[rs, sl], sb) + _dot(p_ref[rs, h * DN_CHUNK:(h + 1) * DN_CHUNK], vnb)
                egl = jnp.exp(gl_ref[j, 0:1, DN_HEADS + h:DN_HEADS + h + 1])
                state[h] = state[h] * egl + _dot_tn(kt_ref[rs, sl], vnb)
        for h in range(DN_HEADS):
            s_s[h] = state[h]

    sp = _dn_specs(rows)
    return pl.pallas_call(
        body, name=name, grid=(n_chunks // group,),
        in_specs=[sp["tok"], sp["tok"], sp["attn"], sp["tok"], sp["tok"], sp["gate"]],
        out_specs=[sp["tok"], sp["tok"], sp["state"]],
        out_shape=[jax.ShapeDtypeStruct((T, hw), F32), jax.ShapeDtypeStruct((T, hw), BF16),
                   jax.ShapeDtypeStruct((n_chunks, DN_HEADS, DN_DIM, DN_DIM), F32)],
        scratch_shapes=[pltpu.VMEM((DN_HEADS, DN_DIM, DN_DIM), F32)],
        compiler_params=_params(("arbitrary",)),
    )(u, w, p, qd, kt, gl)


def _dn_scan_bwd(w, p, qd, kt, gl, vn, sall, do, name):
    T = w.shape[0]
    n_chunks = T // DN_CHUNK
    group = min(DN_SCAN_CHUNKS, n_chunks)
    rows = group * DN_CHUNK
    hw = DN_HEADS * DN_DIM
    last = n_chunks // group - 1

    def body(w_ref, p_ref, qd_ref, kt_ref, gl_ref, vn_ref, sall_ref, do_ref, dvn_ref, dkt_ref, dgl_ref, ds_s):
        @pl.when(pl.program_id(0) == 0)
        def _():
            ds_s[...] = jnp.zeros_like(ds_s)

        lane = lax.broadcasted_iota(jnp.int32, (8, 128), 1)
        d_state = [ds_s[h] for h in range(DN_HEADS)]
        for j in reversed(range(group)):
            rs = slice(j * DN_CHUNK, (j + 1) * DN_CHUNK)
            dgl_tile = jnp.zeros((8, 128), F32)
            for h in range(DN_HEADS):
                sl = slice(h * DN_DIM, (h + 1) * DN_DIM)
                d_out = _bf(do_ref[rs, sl])
                d_new = d_state[h]
                d_newb = _bf(d_new)
                d_vn = _dot_tn(p_ref[rs, h * DN_CHUNK:(h + 1) * DN_CHUNK], d_out) + _dot(kt_ref[rs, sl], d_newb)
                dvn_ref[rs, sl] = d_vn
                dkt_ref[rs, sl] = _dot_nt(vn_ref[rs, sl], d_newb)
                egl = jnp.exp(gl_ref[j, 0:1, DN_HEADS + h:DN_HEADS + h + 1])
                prod = jnp.sum(d_new * sall_ref[j, h], axis=1, keepdims=True)
                dgl_tile = jnp.where(lane == DN_HEADS + h, jnp.sum(prod, axis=0, keepdims=True) * egl, dgl_tile)
                d_state[h] = d_new * egl + _dot_tn(qd_ref[rs, sl], d_out) - _dot_tn(w_ref[rs, sl], _bf(d_vn))
            dgl_ref[j] = dgl_tile
        for h in range(DN_HEADS):
            ds_s[h] = d_state[h]

    sp = _dn_specs(rows, rev=lambda n: last - n)
    return pl.pallas_call(
        body, name=name, grid=(n_chunks // group,),
        in_specs=[sp["tok"], sp["attn"], sp["tok"], sp["tok"], sp["gate"], sp["tok"], sp["state"], sp["tok"]],
        out_specs=[sp["tok"], sp["tok"], sp["gate"]],
        out_shape=[jax.ShapeDtypeStruct((T, hw), F32), jax.ShapeDtypeStruct((T, hw), F32),
                   jax.ShapeDtypeStruct((n_chunks, 8, 128), F32)],
        scratch_shapes=[pltpu.VMEM((DN_HEADS, DN_DIM, DN_DIM), F32)],
        compiler_params=_params(("arbitrary",)),
    )(w, p, qd, kt, gl, vn, sall, do)


def _dn_prep_bwd(qkv, proj, hp, sall, vn, do, dvn, dkt, dgl, inv, u, w, name, exchange=None):
    T = qkv.shape[0]
    n_chunks = T // DN_CHUNK
    blocks = max(1, min(DN_PREP_CHUNKS, n_chunks) * DN_CHUNK // DN_BLOCK)
    per_block = DN_BLOCK // DN_CHUNK
    group = blocks * per_block
    rows = blocks * DN_BLOCK
    hw = DN_HEADS * DN_DIM
    first_rows, second_rows = slice(0, DN_CHUNK), slice(DN_CHUNK, DN_BLOCK)

    def rowsum(x):
        return jnp.sum(x, axis=1, keepdims=True)

    def by_chunk(x, s0, s1, fn):
        return jnp.concatenate([fn(x[first_rows], s0), fn(x[second_rows], s1)], axis=0)

    def body(q_ref, k_ref, v_ref, ba_ref, hp_ref, sall_ref, vn_ref, do_ref, dvn_ref, dkt_ref, dgl_ref,
             inv_ref, u_ref, w_ref, dqkv_ref, dba_ref, dhp_ref):
        @pl.when(pl.program_id(0) == 0)
        def _():
            dhp_ref[...] = jnp.zeros_like(dhp_ref)

        mk = _BlockMasks()
        hp_v = hp_ref[...]
        total = jnp.zeros((8, 128), F32)
        for j in range(blocks):
            rs = slice(j * DN_BLOCK, (j + 1) * DN_BLOCK)
            total = total + one_block(mk, hp_v, *(r.at[rs, :] for r in (q_ref, k_ref, v_ref, ba_ref)),
                                      sall_ref.at[pl.ds(j * per_block, per_block)],
                                      *(r.at[rs, :] for r in (vn_ref, do_ref, dvn_ref, dkt_ref)),
                                      dgl_ref.at[pl.ds(j * per_block, per_block)],
                                      *(r.at[rs, :] for r in (inv_ref, u_ref, w_ref)),
                                      *(dqkv_ref.at[rs, pl.ds(i * hw, hw)] for i in range(3)), dba_ref.at[rs, :])
        dhp_ref[...] += total

    def one_block(mk, hp_v, q_ref, k_ref, v_ref, ba_ref, state_ref, vn_ref, do_ref, dvn_ref, dkt_ref, dgl_ref,
                  inv_ref, u_ref, w_ref, dq_ref, dk_ref, dv_ref, dba_ref):
        ba = ba_ref[...]
        beta_t, gcum_t, gl_t, graw_t, coef, pre = _dn_block_gates(mk, ba, hp_v)
        lane = lax.broadcasted_iota(jnp.int32, (DN_BLOCK, 128), 1)
        rowi = lax.broadcasted_iota(jnp.int32, (DN_BLOCK, 1), 0)
        dgcum_t = jnp.zeros((DN_BLOCK, 128), F32)
        dbeta_t = jnp.zeros((DN_BLOCK, 128), F32)
        for h in range(DN_HEADS):
            sl = slice(h * DN_DIM, (h + 1) * DN_DIM)
            gate = slice(DN_HEADS + h, DN_HEADS + h + 1)
            gc = gcum_t[:, gate]
            bc = beta_t[:, h:h + 1]
            sb0, sb1 = _bf(state_ref[0, h]), _bf(state_ref[1, h])
            vh = v_ref[:, sl]
            f = _dn_local(mk, q_ref[:, sl], k_ref[:, sl], bc, gc, gl_t[:, gate])
            qh, kh, kb, dec, eg, etl = f["qh"], f["kh"], f["kb"], f["dec"], f["eg"], f["etl"]
            qd, kt = f["qd"], f["kt"]
            qb, kbf, kbb = _bf(qh), _bf(kh), _bf(kb)
            dec_t = jnp.where(mk.upper, jnp.exp(jnp.where(mk.upper, f["gr"] - gc, 0.0)), 0.0)
            mkk_t = f["mkk"].T
            inv_t = inv_ref[:, sl].T
            mqk_t = f["mqk"].T

            d_out = _bf(do_ref[:, sl])
            vnb = vn_ref[:, sl]
            d_qd = by_chunk(d_out, sb0, sb1, _dot_nt)
            d_attn = _dot_nt(d_out, vnb)
            d_attn_t = _dot_nt(vnb, d_out)
            d_vn = dvn_ref[:, sl]
            d_kt = dkt_ref[:, sl]
            d_w = -by_chunk(_bf(d_vn), sb0, sb1, _dot_nt)
            d_rhs = _dot_mid(inv_t, jnp.concatenate([d_vn, d_w], axis=1))
            d_bu, d_bw = d_rhs[:, :DN_DIM], d_rhs[:, DN_DIM:]
            ub, wb, d_bub, d_bwb = _bf(u_ref[:, sl]), w_ref[:, sl], _bf(d_bu), _bf(d_bw)
            d_low = -(_dot_nt(d_bub, ub) + _dot_nt(d_bwb, wb))
            d_low_t = -(_dot_nt(ub, d_bub) + _dot_nt(wb, d_bwb))
            d_mkk = jnp.where(mk.strict_lower, d_low * dec, 0.0)
            d_mkk_t = jnp.where(mk.strict_upper, d_low_t * dec_t, 0.0)
            d_mqk = jnp.where(mk.lower, d_attn * dec, 0.0)
            d_mqk_t = jnp.where(mk.upper, d_attn_t * dec_t, 0.0)
            bw = kb * eg
            d_kb = _dot(_bf(d_mkk), kbf) + d_bw * eg
            d_k = _dot(_bf(d_mkk_t), kbb) + _dot(_bf(d_mqk_t), qb) + d_kt * etl + d_kb * bc
            d_q = _dot(_bf(d_mqk), kbf) + d_qd * eg
            d_beta = rowsum(d_kb * kh) + rowsum(d_bu * vh)
            dv_ref[:, sl] = d_bu * bc
            e_mat = d_mkk * f["mkk"] + d_mqk * f["mqk"]
            e_mat_t = d_mkk_t * mkk_t + d_mqk_t * mqk_t
            kt_term = rowsum(d_kt * kt)
            d_g = rowsum(e_mat) - rowsum(e_mat_t) + rowsum(d_qd * qd) + rowsum(d_bw * bw) - kt_term
            for c, chunk_rows in enumerate((mk.first, ~mk.first)):
                d_glast = dgl_ref[c, 0:1, gate] + jnp.sum(jnp.where(chunk_rows, kt_term, 0.0), axis=0, keepdims=True)
                d_g = d_g + jnp.where(rowi == (c + 1) * DN_CHUNK - 1, d_glast, 0.0)
            qn = f["qn"]
            d_qs = d_q * (DN_DIM ** -0.5)
            dq_ref[:, sl] = f["rq"] * (d_qs - qn * rowsum(d_qs * qn))
            dk_ref[:, sl] = f["rk"] * (d_k - kh * rowsum(d_k * kh))
            dgcum_t = jnp.where(lane == DN_HEADS + h, d_g, dgcum_t)
            dbeta_t = jnp.where(lane == h, d_beta, dbeta_t)
        dgraw_t = _dot_hi(mk.upper.astype(F32), dgcum_t)
        sp = _sigmoid(pre)
        d_pre = dgraw_t * coef * sp
        dba_ref[...] = jnp.where(lane < DN_HEADS, dbeta_t * beta_t * (1.0 - beta_t),
                                 jnp.where(lane < 2 * DN_HEADS, d_pre, 0.0))
        in_g = (lane >= DN_HEADS) & (lane < 2 * DN_HEADS)
        d_alog = jnp.sum(jnp.where(in_g, dgraw_t * graw_t, 0.0), axis=0, keepdims=True)
        d_dtb = jnp.sum(jnp.where(in_g, d_pre, 0.0), axis=0, keepdims=True)
        return jnp.concatenate([d_alog, d_dtb, jnp.zeros((6, 128), F32)], axis=0)

    sp = _dn_specs(rows)
    return _call_with_exchange(
        body, exchange, name=name, steps=n_chunks // group,
        in_specs=sp["qkv"] + [sp["ba"], sp["hp"], sp["state"]] + [sp["tok"]] * 4 + [sp["gate"]] + [sp["tok"]] * 3,
        out_specs=[pl.BlockSpec((rows, 3 * hw), lambda n: (n, 0)), pl.BlockSpec((rows, 128), lambda n: (n, 0)), sp["hp"]],
        out_shape=[jax.ShapeDtypeStruct((T, 3 * hw), F32), jax.ShapeDtypeStruct((T, 128), F32),
                   jax.ShapeDtypeStruct((8, 128), F32)],
        args=(qkv, qkv, qkv, proj, hp, sall, vn, do, dvn, dkt, dgl, inv, u, w))


def _mix_fwd(o, proj, dn_norm, sg_norm, sg_w, sg_bt, name):
    T = o.shape[0]
    tm = _tile(T, 512)
    hw = DN_HEADS * DN_DIM
    nc = tm // SG_CHUNK

    def body(o_ref, z_ref, su_ref, sv_ref, dnn_ref, sgn_ref, sgw_ref, sgb_ref, mix_ref):
        dnn = dnn_ref[...]
        for h in range(DN_HEADS):
            sl = slice(h * DN_DIM, (h + 1) * DN_DIM)
            xhat, _ = _rms_stats(o_ref[:, sl])
            z = z_ref[:, sl]
            mix_ref[:, sl] = _bf(xhat * dnn * (z * _sigmoid(z)))
        tri = lax.broadcasted_iota(jnp.int32, (SG_CHUNK, SG_CHUNK), 0) >= lax.broadcasted_iota(jnp.int32, (SG_CHUNK, SG_CHUNK), 1)
        for g in range(SG_GROUPS):
            sl = slice(g * SG_DIM, (g + 1) * SG_DIM)
            xhat, _ = _rms_stats(_gelu(sv_ref[:, sl]))
            svn = _bf(xhat * sgn_ref[g:g + 1, :])
            sua = _gelu(su_ref[:, sl])
            wt = _bf(jnp.where(tri, sgw_ref[g], 0.0))
            bias = sgb_ref[:, g:g + 1]
            for c in range(nc):
                rows = slice(c * SG_CHUNK, (c + 1) * SG_CHUNK)
                mixed = _dot(wt, svn[rows, :]) + bias
                mix_ref[rows, hw + g * SG_DIM:hw + (g + 1) * SG_DIM] = _bf(sua[rows, :] * mixed)

    full = lambda shape: pl.BlockSpec(shape, lambda i: (0,) * len(shape))
    return pl.pallas_call(
        body, name=name, grid=(T // tm,),
        in_specs=[pl.BlockSpec((tm, hw), lambda i: (i, 0)),
                  pl.BlockSpec((tm, hw), lambda i: (i, 3)),
                  pl.BlockSpec((tm, hw), lambda i: (i, 4)),
                  pl.BlockSpec((tm, hw), lambda i: (i, 5)),
                  full((1, DN_DIM)), full((SG_GROUPS, SG_DIM)), full((SG_GROUPS, SG_CHUNK, SG_CHUNK)),
                  full((SG_CHUNK, 128))],
        out_specs=pl.BlockSpec((tm, 2 * hw), lambda i: (i, 0)),
        out_shape=jax.ShapeDtypeStruct((T, 2 * hw), BF16),
        compiler_params=_params(("parallel",)),
    )(o, proj, proj, proj, dn_norm, sg_norm, sg_w, sg_bt)


def _mix_bwd(dmix, o, proj, dn_norm, sg_norm, sg_w, sg_bt, name):
    T = o.shape[0]
    tm = _tile(T, 512)
    hw = DN_HEADS * DN_DIM
    nc = tm // SG_CHUNK

    def body(dm_ref, o_ref, z_ref, su_ref, sv_ref, dnn_ref, sgn_ref, sgw_ref, sgb_ref,
             do_ref, dz_ref, ddnn_ref, dsgn_ref, dsgw_ref, dsgb_ref):
        @pl.when(pl.program_id(0) == 0)
        def _():
            ddnn_ref[...] = jnp.zeros_like(ddnn_ref)
            dsgn_ref[...] = jnp.zeros_like(dsgn_ref)
            dsgw_ref[...] = jnp.zeros_like(dsgw_ref)
            dsgb_ref[...] = jnp.zeros_like(dsgb_ref)

        dnn = dnn_ref[...]
        ddnn = jnp.zeros((1, DN_DIM), F32)
        for h in range(DN_HEADS):
            sl = slice(h * DN_DIM, (h + 1) * DN_DIM)
            xhat, r = _rms_stats(o_ref[:, sl])
            z = z_ref[:, sl]
            sz = _sigmoid(z)
            doa = dm_ref[:, sl]
            dyn = doa * (z * sz)
            dz_ref[:, sl] = _bf(doa * xhat * dnn * _silu_grad(z, sz))
            do_ref[:, sl] = _rms_bwd(dyn, xhat, r, dnn)
            ddnn = ddnn + jnp.sum(dyn * xhat, axis=0, keepdims=True)
        ddnn_ref[...] += ddnn
        tri = lax.broadcasted_iota(jnp.int32, (SG_CHUNK, SG_CHUNK), 0) >= lax.broadcasted_iota(jnp.int32, (SG_CHUNK, SG_CHUNK), 1)
        lane = lax.broadcasted_iota(jnp.int32, (SG_CHUNK, 128), 1)
        dsgb = jnp.zeros((SG_CHUNK, 128), F32)
        dsgn_rows = []
        for g in range(SG_GROUPS):
            sl = slice(g * SG_DIM, (g + 1) * SG_DIM)
            sv = sv_ref[:, sl]
            su = su_ref[:, sl]
            xhat, r = _rms_stats(_gelu(sv))
            sgn = sgn_ref[g:g + 1, :]
            svn = _bf(xhat * sgn)
            sua = _gelu(su)
            wt = _bf(jnp.where(tri, sgw_ref[g], 0.0))
            bias = sgb_ref[:, g:g + 1]
            dw = jnp.zeros((SG_CHUNK, SG_CHUNK), F32)
            db = jnp.zeros((SG_CHUNK, 1), F32)
            dsua, dsvn = [], []
            for c in range(nc):
                rows = slice(c * SG_CHUNK, (c + 1) * SG_CHUNK)
                mixed = _dot(wt, svn[rows, :]) + bias
                dob = dm_ref[rows, hw + g * SG_DIM:hw + (g + 1) * SG_DIM]
                dsua.append(dob * mixed)
                dmixed = dob * sua[rows, :]
                dmb = _bf(dmixed)
                dsvn.append(_dot_tn(wt, dmb))
                dw = dw + _dot_nt(dmb, svn[rows, :])
                db = db + jnp.sum(dmixed, axis=1, keepdims=True)
            dsua = jnp.concatenate(dsua, axis=0) if nc > 1 else dsua[0]
            dsvn = jnp.concatenate(dsvn, axis=0) if nc > 1 else dsvn[0]
            dz_ref[:, hw + g * SG_DIM:hw + (g + 1) * SG_DIM] = _bf(dsua * _gelu_grad(su))
            dz_ref[:, 2 * hw + g * SG_DIM:2 * hw + (g + 1) * SG_DIM] = _bf(_rms_bwd(dsvn, xhat, r, sgn) * _gelu_grad(sv))
            dsgn_rows.append(jnp.sum(dsvn * xhat, axis=0, keepdims=True))
            dsgw_ref[g] += jnp.where(tri, dw, 0.0)
            dsgb = jnp.where(lane == g, db, dsgb)
        dsgn_ref[...] += jnp.concatenate(dsgn_rows, axis=0)
        dsgb_ref[...] += dsgb

    full = lambda shape: pl.BlockSpec(shape, lambda i: (0,) * len(shape))
    return pl.pallas_call(
        body, name=name, grid=(T // tm,),
        in_specs=[pl.BlockSpec((tm, 2 * hw), lambda i: (i, 0)),
                  pl.BlockSpec((tm, hw), lambda i: (i, 0)),
                  pl.BlockSpec((tm, hw), lambda i: (i, 3)),
                  pl.BlockSpec((tm, hw), lambda i: (i, 4)),
                  pl.BlockSpec((tm, hw), lambda i: (i, 5)),
                  full((1, DN_DIM)), full((SG_GROUPS, SG_DIM)), full((SG_GROUPS, SG_CHUNK, SG_CHUNK)),
                  full((SG_CHUNK, 128))],
        out_specs=[pl.BlockSpec((tm, hw), lambda i: (i, 0)),
                   pl.BlockSpec((tm, 3 * hw), lambda i: (i, 0)),
                   full((1, DN_DIM)), full((SG_GROUPS, SG_DIM)), full((SG_GROUPS, SG_CHUNK, SG_CHUNK)),
                   full((SG_CHUNK, 128))],
        out_shape=[jax.ShapeDtypeStruct((T, hw), F32), jax.ShapeDtypeStruct((T, 3 * hw), BF16),
                   jax.ShapeDtypeStruct((1, DN_DIM), F32), jax.ShapeDtypeStruct((SG_GROUPS, SG_DIM), F32),
                   jax.ShapeDtypeStruct((SG_GROUPS, SG_CHUNK, SG_CHUNK), F32),
                   jax.ShapeDtypeStruct((SG_CHUNK, 128), F32)],
        compiler_params=_params(("arbitrary",)),
    )(dmix, o, proj, proj, proj, dn_norm, sg_norm, sg_w, sg_bt)


def _window_sums(h, sign):
    sums, s, w = {}, h, 1
    while w < POOL_WINDOWS[-1]:
        s = s + _shift_rows(s, sign * w)
        w *= 2
        sums[w] = s
    return sums


def _pool_counts(t_global):
    return [jnp.minimum(t_global + 1, win).astype(F32) for win in POOL_WINDOWS]


def _pooled_groups(ext_h, row0, tm):
    sums = _window_sums(ext_h, 1)
    t_global = row0 + lax.broadcasted_iota(jnp.int32, (tm, 1), 0)
    counts = _pool_counts(t_global)
    out = []
    for gi, win in enumerate(POOL_WINDOWS):
        cols = slice(gi * POOL_DIM, (gi + 1) * POOL_DIM)
        out.append(sums[win][POOL_HALO:, cols] / counts[gi] - ext_h[POOL_HALO:, cols])
    return out


def _pool_fwd(x, nw, pool_w, pool_scale, layer, name):
    T, D = x.shape
    tm = _tile(T, 256)
    hb = tm // POOL_HALO

    def body(x_ref, xp_ref, n_ref, w_ref, s_ref, xo_ref):
        i = pl.program_id(0)
        prev = jnp.where(i == 0, 0.0, xp_ref[...])
        ext = jnp.concatenate([prev, x_ref[...]], axis=0)
        xhat, _ = _rms_stats(ext)
        pooled = _pooled_groups(xhat * n_ref[...], i * tm, tm)
        for gi in range(len(POOL_WINDOWS)):
            cols = slice(gi * POOL_DIM, (gi + 1) * POOL_DIM)
            xo_ref[:, cols] = x_ref[:, cols] + _dot(_bf(pooled[gi]), w_ref[gi]) * s_ref[:, cols]

    return pl.pallas_call(
        body, name=name, grid=(T // tm,),
        in_specs=[pl.BlockSpec((tm, D), lambda i: (i, 0)),
                  pl.BlockSpec((POOL_HALO, D), lambda i: (jnp.maximum(i * hb - 1, 0), 0)),
                  pl.BlockSpec((None, 1, D), lambda i: (layer, 0, 0)),
                  pl.BlockSpec(pool_w.shape, lambda i: (0, 0, 0)),
                  pl.BlockSpec((1, D), lambda i: (0, 0))],
        out_specs=pl.BlockSpec((tm, D), lambda i: (i, 0)),
        out_shape=jax.ShapeDtypeStruct((T, D), F32),
        compiler_params=_params(("parallel",)),
    )(x, x, nw, pool_w, pool_scale)


def _pool_bwd(dxo, x, nw, pool_w, pool_scale, layer, name):
    T, D = x.shape
    tm = _tile(T, 256)
    hb = tm // POOL_HALO
    nt = T // tm
    ng = len(POOL_WINDOWS)

    def body(dxo_ref, dxn_ref, x_ref, xp_ref, n_ref, w_ref, s_ref, dx_ref, dw_ref, ds_ref, dn_ref):
        i = pl.program_id(0)

        @pl.when(i == 0)
        def _():
            dw_ref[...] = jnp.zeros_like(dw_ref)
            ds_ref[...] = jnp.zeros_like(ds_ref)
            dn_ref[...] = jnp.zeros_like(dn_ref)

        prev = jnp.where(i == 0, 0.0, xp_ref[...])
        ext = jnp.concatenate([prev, x_ref[...]], axis=0)
        xhat_ext, r_ext = _rms_stats(ext)
        nv = n_ref[...]
        pooled = _pooled_groups(xhat_ext * nv, i * tm, tm)
        dxo = dxo_ref[...]
        scale = s_ref[...]
        dout_ext = jnp.concatenate([dxo, jnp.where(i == nt - 1, 0.0, dxn_ref[...])], axis=0) * scale
        t_ext = i * tm + lax.broadcasted_iota(jnp.int32, (tm + POOL_HALO, 1), 0)
        counts = _pool_counts(t_ext)
        dh_cols, ds_cols = [], []
        for gi, win in enumerate(POOL_WINDOWS):
            cols = slice(gi * POOL_DIM, (gi + 1) * POOL_DIM)
            wg = w_ref[gi]
            pb = _bf(pooled[gi])
            doutb = _bf(dout_ext[:, cols])
            dpooled = _dot_nt(doutb, wg)
            ahead = _window_sums(dpooled / counts[gi], -1)[win]
            dh_cols.append(ahead[:tm, :] - dpooled[:tm, :])
            dw_ref[gi] += _dot_tn(pb, doutb[:tm, :])
            ds_cols.append(jnp.sum(dxo[:, cols] * _dot(pb, wg), axis=0, keepdims=True))
        dh = jnp.concatenate(dh_cols, axis=1)
        xhat, r = xhat_ext[POOL_HALO:, :], r_ext[POOL_HALO:, :]
        dx_ref[...] = dxo + _rms_bwd(dh, xhat, r, nv)
        dn_ref[...] += jnp.sum(dh * xhat, axis=0, keepdims=True)
        ds_ref[...] += jnp.concatenate(ds_cols, axis=1)

    last_halo = T // POOL_HALO - 1
    return pl.pallas_call(
        body, name=name, grid=(nt,),
        in_specs=[pl.BlockSpec((tm, D), lambda i: (i, 0)),
                  pl.BlockSpec((POOL_HALO, D), lambda i: (jnp.minimum((i + 1) * hb, last_halo), 0)),
                  pl.BlockSpec((tm, D), lambda i: (i, 0)),
                  pl.BlockSpec((POOL_HALO, D), lambda i: (jnp.maximum(i * hb - 1, 0), 0)),
                  pl.BlockSpec((None, 1, D), lambda i: (layer, 0, 0)),
                  pl.BlockSpec(pool_w.shape, lambda i: (0, 0, 0)),
                  pl.BlockSpec((1, D), lambda i: (0, 0))],
        out_specs=[pl.BlockSpec((tm, D), lambda i: (i, 0)),
                   pl.BlockSpec((ng, POOL_DIM, POOL_DIM), lambda i: (0, 0, 0)),
                   pl.BlockSpec((1, D), lambda i: (0, 0)),
                   pl.BlockSpec((1, D), lambda i: (0, 0))],
        out_shape=[jax.ShapeDtypeStruct((T, D), F32), jax.ShapeDtypeStruct((ng, POOL_DIM, POOL_DIM), F32),
                   jax.ShapeDtypeStruct((1, D), F32), jax.ShapeDtypeStruct((1, D), F32)],
        compiler_params=_params(("arbitrary",)),
    )(dxo, dxo, x, x, nw, pool_w, pool_scale)


def _loss_head(x, target, fn, name):
    T, D = x.shape
    tm = _tile(T, 512)

    def body(x_ref, t_ref, n_ref, loss_ref, dx_ref, dn_ref):
        @pl.when(pl.program_id(0) == 0)
        def _():
            loss_ref[...] = jnp.zeros_like(loss_ref)
            dn_ref[...] = jnp.zeros_like(dn_ref)

        xhat, r = _rms_stats(x_ref[...])
        nv = n_ref[...]
        err = xhat * nv - t_ref[...]
        part = jnp.sum(jnp.sum(err * err, axis=1, keepdims=True), axis=0, keepdims=True)
        loss_ref[...] += 0.5 * part / D
        dy = err / D
        dx_ref[...] = _rms_bwd(dy, xhat, r, nv)
        dn_ref[...] += jnp.sum(dy * xhat, axis=0, keepdims=True)

    row = pl.BlockSpec((tm, D), lambda i: (i, 0))
    return pl.pallas_call(
        body, name=name, grid=(T // tm,),
        in_specs=[row, row, pl.BlockSpec((1, D), lambda i: (0, 0))],
        out_specs=[pl.BlockSpec((1, 1), lambda i: (0, 0)), row, pl.BlockSpec((1, D), lambda i: (0, 0))],
        out_shape=[jax.ShapeDtypeStruct((1, 1), F32), jax.ShapeDtypeStruct((T, D), F32),
                   jax.ShapeDtypeStruct((1, D), F32)],
        compiler_params=_params(("arbitrary",)),
    )(x, target, fn)


def _adamw(w, g, m, v, name):
    R, C = w.shape
    br = R
    for cand in (512, 256, 128, 64, 32, 16, 8):
        if R % cand == 0 and cand * C * 4 <= 2 * 1024 * 1024:
            br = cand
            break

    def body(w_ref, g_ref, m_ref, v_ref, d_ref, mo_ref, vo_ref):
        gv = g_ref[...]
        m_new = ADAM_B1 * m_ref[...] + (1.0 - ADAM_B1) * gv
        v_new = ADAM_B2 * v_ref[...] + (1.0 - ADAM_B2) * (gv * gv)
        m_hat = m_new / (1.0 - ADAM_B1 ** ADAM_STEP)
        v_hat = v_new / (1.0 - ADAM_B2 ** ADAM_STEP)
        d_ref[...] = -ADAM_LR * (m_hat / (jnp.sqrt(v_hat) + ADAM_EPS) + ADAM_WD * w_ref[...])
        mo_ref[...] = m_new
        vo_ref[...] = v_new

    blk = pl.BlockSpec((br, C), lambda i: (i, 0))
    return pl.pallas_call(
        body, name=name, grid=(R // br,), in_specs=[blk] * 4, out_specs=[blk] * 3,
        out_shape=[jax.ShapeDtypeStruct((R, C), F32)] * 3,
        compiler_params=_params(("parallel",)),
    )(w, g, m, v)


def _mesh_pos():
    return lax.axis_index("x"), lax.axis_index("y"), lax.axis_index("c")


def _other_chips(x, y):
    return [(1 - x, y), (x, 1 - y), (1 - x, 1 - y)]


def _half_of(ref, shape, h):
    size = shape[0] // 2
    return ref.at[pl.ds(h * size, size)]


class _ChipGather:
    def __init__(self, shards, split):
        self.shards, self.split = list(shards), list(split)
        self.operands = self.shards
        n = len(self.shards)
        self.out_shape = [jax.ShapeDtypeStruct((N_CHIPS,) + s.shape, s.dtype) for s in self.shards]
        self.scratch = [pltpu.SemaphoreType.DMA((n, 3))] * 4

    def _piece(self, a, ref, h):
        return _half_of(ref, self.shards[a].shape, h) if self.split[a] else ref

    def start(self, ins, outs, sems):
        send_sems, recv_sems = sems[0], sems[1]
        x, y, c = _mesh_pos()
        me = 2 * x + y
        for a in range(len(ins)):
            for k, (px, py) in enumerate(_other_chips(x, y)):
                pltpu.make_async_remote_copy(self._piece(a, ins[a], c), self._piece(a, outs[a].at[me], c),
                                             send_sems.at[a, k], recv_sems.at[a, k],
                                             device_id=(px, py, c), device_id_type=MESH).start()

    def finish(self, ins, outs, sems):
        send_sems, recv_sems, fwd_send_sems, fwd_recv_sems = sems
        x, y, c = _mesh_pos()
        sibling = (x, y, 1 - c)
        chips = _other_chips(x, y)
        n = len(ins)
        forwards = []
        for a in range(n):
            for k, (px, py) in enumerate(chips):
                landed = self._piece(a, outs[a].at[2 * px + py], c)
                pltpu.make_async_remote_copy(landed, landed, send_sems.at[a, k], recv_sems.at[a, k],
                                             device_id=(px, py, c), device_id_type=MESH).wait_recv()
                if self.split[a]:
                    fwd = pltpu.make_async_remote_copy(landed, landed, fwd_send_sems.at[a, k], fwd_recv_sems.at[a, k],
                                                       device_id=sibling, device_id_type=MESH)
                    fwd.start()
                    forwards.append(fwd)
        for a in range(n):
            if self.split[a]:
                for k, (px, py) in enumerate(chips):
                    other = self._piece(a, outs[a].at[2 * px + py], 1 - c)
                    pltpu.make_async_remote_copy(other, other, fwd_send_sems.at[a, k], fwd_recv_sems.at[a, k],
                                                 device_id=sibling, device_id_type=MESH).wait_recv()
        for a in range(n):
            for k, (px, py) in enumerate(chips):
                sent = self._piece(a, ins[a], c)
                pltpu.make_async_remote_copy(sent, sent, send_sems.at[a, k], recv_sems.at[a, k],
                                             device_id=(px, py, c), device_id_type=MESH).wait_send()
        for fwd in forwards:
            fwd.wait_send()

    def finalize(self, gathered):
        x, y, _ = _mesh_pos()
        return [lax.dynamic_update_index_in_dim(g, s, 2 * x + y, 0) for g, s in zip(gathered, self.shards)]

    def run(self, name):
        n = len(self.shards)

        def body(*refs):
            ins, outs, sems = refs[:n], refs[n:2 * n], refs[2 * n:]
            self.start(ins, outs, sems)
            self.finish(ins, outs, sems)

        gathered = pl.pallas_call(
            body, name=name, in_specs=[ANY] * n, out_specs=[ANY] * n, out_shape=self.out_shape,
            scratch_shapes=self.scratch, compiler_params=pltpu.CompilerParams(has_side_effects=True),
        )(*self.shards)
        return self.finalize(gathered)


def _ffn_weight_grads(hb, dg, du, a, dyb, tag):
    dwg = _matmul_tn(hb, dg, D_MODEL, FF_CHUNK, f"{tag}_dw_gate", stack_n=True)
    dwu = _matmul_tn(hb, du, D_MODEL, FF_CHUNK, f"{tag}_dw_up", stack_n=True)
    dwo = _matmul_tn(a, dyb, FF_CHUNK, D_MODEL, f"{tag}_dw_out")
    return jnp.concatenate([dwg, dwu], axis=0), dwo.reshape(N_CHIPS, D_FF // N_CHIPS, D_MODEL)


def _local_step(x, target, w, late=None, reduce=False):
    g = {}
    acts = []
    w = dict(w)

    def ffn_weights(which, layer):
        return w[f"n{which}"], w[f"win{which}_l{layer}"], w[f"wout{which}_l{layer}"]

    def ffn(xin, which, layer):
        xo, gv, uv, hb = _ffn_fwd(xin, *ffn_weights(which, layer), layer, f"ffn{which}_l{layer}_fwd")
        acts.append((xin, gv, uv, hb))
        return xo

    x1 = ffn(x, 1, 0)
    hb_mix = _rms_fwd_call(x1, w["nmix"], 0, "ab_norm_fwd")
    proj = _matmul(hb_mix, w["wp"], "ab_in_proj")
    qkv = _conv_fwd(proj, w["conv_w"], "dn_conv_fwd")
    (dn_u, dn_w, dn_p, dn_qd, dn_kt, dn_gl, dn_inv), arrived = _dn_prep(qkv, proj, w["hp"], "dn_prep",
                                                                       None if late is None else late[0])
    if late is not None:
        w.update(late[1](arrived))
    o, dn_vn, sall = _dn_scan(dn_u, dn_w, dn_p, dn_qd, dn_kt, dn_gl, "dn_scan")
    mix = _mix_fwd(o, proj, w["dn_norm"], w["sg_norm"], w["sg_w"], w["sg_bt"], "ab_gate_fwd")
    x2 = _matmul(mix, w["wo"], "ab_out_proj", res=x1)
    x3 = ffn(x2, 2, 0)
    x4 = ffn(x3, 1, 1)
    x5 = _pool_fwd(x4, w["nmix"], w["pool_w"], w["pool_scale"], 1, "pool_fwd")
    x6 = ffn(x5, 2, 1)
    loss, dx, g["fn"] = _loss_head(x6, target, w["fn"], "loss_head")

    dn = {1: [None, None], 2: [None, None]}
    dwin = {1: [None, None], 2: [None, None]}
    dwout = {1: [None, None], 2: [None, None]}

    def ffn_back(dxo, which, layer, saved, exchange=None):
        nw, win, wout = ffn_weights(which, layer)
        xin, gv, uv, hb = saved
        tag = f"ffn{which}_l{layer}"
        (dxi, dg, du, a, dyb, dnw), arrived = _ffn_bwd(dxo, xin, nw, gv, uv, win, wout, layer, f"{tag}_bwd", exchange)
        dn[which][layer] = dnw
        dwin[which][layer], dwout[which][layer] = _ffn_weight_grads(hb, dg, du, a, dyb, tag)
        return dxi, arrived

    for which in (1, 2):
        g[f"win{which}"] = dwin[which]
        g[f"wout{which}"] = dwout[which]
    reduced = {}

    def open_round(tag, keys):
        have = _sharded_grads(g)
        return _GradRound(tag, {k: have[k] for k in keys})

    dx, _ = ffn_back(dx, 2, 1, acts[3])
    dx, g["pool_w"], g["pool_scale"], dnmix1 = _pool_bwd(dx, x4, w["nmix"], w["pool_w"], w["pool_scale"], 1, "pool_bwd")
    dx, _ = ffn_back(dx, 1, 1, acts[2])
    round_a = open_round("a", REDUCE_ROUNDS[0]) if reduce else None
    dx2, arrived = ffn_back(dx, 2, 0, acts[1], round_a.scatter if reduce else None)
    if reduce:
        reduced.update(round_a.finish(arrived))
    round_b = open_round("b", REDUCE_ROUNDS[1]) if reduce else None
    dmix = _matmul(dx2, w["wo"], "ab_out_proj_bwd", trans_b=True)
    g["wo"] = _matmul_tn(mix, dx2, D_MODEL, D_MODEL, "ab_out_proj_dw")
    do, dzuv, g["dn_norm"], g["sg_norm"], g["sg_w"], g["sg_bt"] = _mix_bwd(
        dmix, o, proj, w["dn_norm"], w["sg_norm"], w["sg_w"], w["sg_bt"], "ab_gate_bwd")
    dvn, dkt, dgl = _dn_scan_bwd(dn_w, dn_p, dn_qd, dn_kt, dn_gl, dn_vn, sall, do, "dn_scan_bwd")
    (dqkv_act, dba, g["hp"]), arrived = _dn_prep_bwd(qkv, proj, w["hp"], sall, dn_vn, do, dvn, dkt, dgl, dn_inv, dn_u, dn_w,
                                                     "dn_prep_bwd", round_b.scatter if reduce else None)
    if reduce:
        reduced.update(round_b.finish(arrived))
    dqkv, g["conv_w"] = _conv_bwd(dqkv_act, proj, w["conv_w"], "dn_conv_bwd")
    dproj = jnp.concatenate([dqkv, dzuv, dba.astype(BF16)], axis=1)
    dh = _matmul(dproj, w["wp"], "ab_in_proj_bwd", trans_b=True)
    g["wp"] = _matmul_tn(hb_mix, dproj, D_MODEL, 640, "ab_in_proj_dw")
    dx1, dnmix0 = _rms_bwd_call(dh, x1, w["nmix"], dx2, 0, "ab_norm_bwd")
    dx0, _ = ffn_back(dx1, 1, 0, acts[0])
    if reduce:
        round_c = open_round("c", REDUCE_ROUNDS[2])
        reduced.update(round_c.finish(round_c.scatter.run("grad_c_chip_scatter")))

    g["n1"] = jnp.concatenate(dn[1], axis=0)
    g["n2"] = jnp.concatenate(dn[2], axis=0)
    g["nmix"] = jnp.concatenate([dnmix0, dnmix1], axis=0)
    return loss, dx0, g, reduced


SHARDED = ("ffn1_w_in", "ffn1_w_out", "ffn2_w_in", "ffn2_w_out", "ab_w_in", "ab_w_out", "pool_w", "dn_conv_w", "pool_scale")
REPLICATED = ("ffn_norm1", "mix_norm", "ffn_norm2", "dn_a_log", "dn_dt_bias", "dn_out_norm", "sg_norm", "sg_w", "sg_b", "final_norm")
QKVZ = 4 * DN_HEADS * DN_DIM
N_GATES = 2 * DN_HEADS
IN_PROJ = QKVZ + N_GATES + 2 * SG_GROUPS * SG_DIM


def _shard_pieces(wts, keys):
    out = []
    for n, layer in keys:
        a = wts[n][0 if layer is None else layer]
        a = a[None] if a.ndim == 1 else a
        out.append(a.astype(BF16) if n in MATRICES else a)
    return out


def _replicated_layouts(rep):
    per_layer = lambda a: a.reshape(a.shape[0], 1, D_MODEL)
    w = {"n1": per_layer(rep["ffn_norm1"]), "nmix": per_layer(rep["mix_norm"]), "n2": per_layer(rep["ffn_norm2"])}
    hp = jnp.zeros((8, 128), F32)
    w["hp"] = hp.at[0, DN_HEADS:N_GATES].set(rep["dn_a_log"][0]).at[1, DN_HEADS:N_GATES].set(rep["dn_dt_bias"][0])
    w["dn_norm"] = rep["dn_out_norm"]
    w["sg_norm"] = rep["sg_norm"][0]
    w["sg_w"] = rep["sg_w"][0]
    w["sg_bt"] = jnp.zeros((SG_CHUNK, 128), F32).at[:, :SG_GROUPS].set(rep["sg_b"][0].T)
    w["fn"] = rep["final_norm"].reshape(1, D_MODEL)
    return w


def _layouts_from(gathered):
    w = {}
    for (n, layer), a in gathered.items():
        if n in ("ffn1_w_in", "ffn2_w_in"):
            w[f"win{n[3]}_l{layer}"] = a
        elif n in ("ffn1_w_out", "ffn2_w_out"):
            w[f"wout{n[3]}_l{layer}"] = a
        elif n == "ab_w_in":
            ab_in = jnp.transpose(a, (1, 0, 2)).reshape(D_MODEL, IN_PROJ)
            w["wp"] = jnp.concatenate([ab_in[:, :QKVZ], ab_in[:, QKVZ + N_GATES:], ab_in[:, QKVZ:QKVZ + N_GATES],
                                       jnp.zeros((D_MODEL, PROJ_W - IN_PROJ), ab_in.dtype)], axis=1)
        elif n == "dn_conv_w":
            w["conv_w"] = jnp.transpose(a, (1, 0, 2)).reshape(DN_CONV, 3 * DN_HEADS * DN_DIM)
        elif n == "ab_w_out":
            w["wo"] = a.reshape(D_MODEL, D_MODEL)
        elif n == "pool_w":
            w["pool_w"] = jnp.transpose(a, (1, 0, 2, 3)).reshape(len(POOL_WINDOWS), POOL_DIM, POOL_DIM)
        elif n == "pool_scale":
            w["pool_scale"] = a.reshape(1, D_MODEL)
    return w


def _sharded_grads(g):
    nw = len(POOL_WINDOWS)
    sharded = {}
    for n, key in (("ffn1_w_in", "win1"), ("ffn1_w_out", "wout1"), ("ffn2_w_in", "win2"), ("ffn2_w_out", "wout2")):
        for layer, a in enumerate(g.get(key, ())):
            if a is not None:
                sharded[(n, layer)] = a
    if "wp" in g:
        wp = g["wp"]
        ab_in = jnp.concatenate([wp[:, :QKVZ], wp[:, IN_PROJ - N_GATES:IN_PROJ], wp[:, QKVZ:IN_PROJ - N_GATES]], axis=1)
        sharded[("ab_w_in", None)] = jnp.transpose(ab_in.reshape(D_MODEL, N_CHIPS, IN_PROJ // N_CHIPS), (1, 0, 2))
    if "wo" in g:
        sharded[("ab_w_out", None)] = g["wo"].reshape(N_CHIPS, D_MODEL // N_CHIPS, D_MODEL)
    if "pool_w" in g:
        sharded[("pool_w", None)] = jnp.transpose(g["pool_w"].reshape(nw, N_CHIPS, POOL_DIM // N_CHIPS, POOL_DIM), (1, 0, 2, 3))
    if "conv_w" in g:
        sharded[("dn_conv_w", None)] = jnp.transpose(g["conv_w"].reshape(DN_CONV, N_CHIPS, -1), (1, 0, 2))
    if "pool_scale" in g:
        sharded[("pool_scale", None)] = g["pool_scale"].reshape(N_CHIPS, 1, D_MODEL // N_CHIPS)
    return sharded


def _replicated_grads(g):
    rep = {
        "ffn_norm1": g["n1"], "mix_norm": g["nmix"], "ffn_norm2": g["n2"],
        "dn_a_log": g["hp"][0:1, DN_HEADS:N_GATES], "dn_dt_bias": g["hp"][1:2, DN_HEADS:N_GATES],
        "dn_out_norm": g["dn_norm"], "sg_norm": g["sg_norm"][None], "sg_w": g["sg_w"][None],
        "sg_b": g["sg_bt"][:, :SG_GROUPS].T[None], "final_norm": g["fn"].reshape(D_MODEL),
    }
    return rep


def _as_halves(a):
    shape = a.shape[1:]
    if len(shape) >= 2 and shape[0] % 2 == 0:
        return a.reshape(N_CHIPS, 2, -1, shape[-1])
    return a.reshape(N_CHIPS, 2, 1, -1)


def _row_block(rows):
    for cand in (256, 176, 128, 64, 32, 16):
        if rows % cand == 0:
            return cand
    return rows


def _from_halves(mine, other, core, shape):
    both = jnp.stack([jnp.where(core == 0, mine, other), jnp.where(core == 0, other, mine)])
    return both.reshape(shape)


def _swap_with_sibling(packs, name):
    n = len(packs)

    def body(*refs):
        ins, outs, send_sems, recv_sems = refs[:n], refs[n:2 * n], refs[2 * n], refs[2 * n + 1]
        x, y, c = _mesh_pos()
        copies = [pltpu.make_async_remote_copy(ins[k].at[:, 1 - c], outs[k], send_sems.at[k], recv_sems.at[k],
                                               device_id=(x, y, 1 - c), device_id_type=MESH) for k in range(n)]
        for cp in copies:
            cp.start()
        for cp in copies:
            cp.wait()

    return pl.pallas_call(
        body, name=name, in_specs=[ANY] * n, out_specs=[ANY] * n,
        out_shape=[jax.ShapeDtypeStruct((p.shape[0],) + p.shape[2:], p.dtype) for p in packs],
        scratch_shapes=[pltpu.SemaphoreType.DMA((n,)), pltpu.SemaphoreType.DMA((n,))],
        compiler_params=pltpu.CompilerParams(has_side_effects=True),
    )(*packs)


def _add_pair(pack, recv, core, name):
    nchip, _, rows, lanes = pack.shape
    rb = _row_block(rows)

    def body(c_ref, a_ref, b_ref, o32_ref, o16_ref):
        s = a_ref[...] + b_ref[...]
        o32_ref[...] = s
        o16_ref[...] = _bf(s)

    blk = pl.BlockSpec((None, rb, lanes), lambda p, i, c: (p, i, 0))
    return pl.pallas_call(
        body, name=name,
        grid_spec=pltpu.PrefetchScalarGridSpec(
            num_scalar_prefetch=1, grid=(nchip, rows // rb),
            in_specs=[pl.BlockSpec((None, None, rb, lanes), lambda p, i, c: (p, c[0], i, 0)), blk],
            out_specs=[blk, blk]),
        out_shape=[jax.ShapeDtypeStruct((nchip, rows, lanes), F32), jax.ShapeDtypeStruct((nchip, rows, lanes), BF16)],
        compiler_params=_params(("parallel", "parallel")),
    )(core, pack, recv)


class _ChipScatter:
    def __init__(self, parts16):
        self.operands = list(parts16)
        n = len(self.operands)
        self.out_shape = [jax.ShapeDtypeStruct((N_CHIPS - 1,) + p.shape[1:], p.dtype) for p in self.operands]
        self.scratch = [pltpu.SemaphoreType.DMA((n, N_CHIPS - 1))] * 2

    def _copies(self, ins, outs, sems):
        x, y, c = _mesh_pos()
        return [pltpu.make_async_remote_copy(ins[a].at[2 * px + py], outs[a].at[k], sems[0].at[a, k], sems[1].at[a, k],
                                             device_id=(px, py, c), device_id_type=MESH)
                for a in range(len(ins)) for k, (px, py) in enumerate(_other_chips(x, y))]

    def start(self, ins, outs, sems):
        for cp in self._copies(ins, outs, sems):
            cp.start()

    def finish(self, ins, outs, sems):
        for cp in self._copies(ins, outs, sems):
            cp.wait()

    def finalize(self, results):
        return results

    def run(self, name):
        n = len(self.operands)

        def body(*refs):
            ins, outs, sems = refs[:n], refs[n:2 * n], refs[2 * n:]
            self.start(ins, outs, sems)
            self.finish(ins, outs, sems)

        return pl.pallas_call(
            body, name=name, in_specs=[ANY] * n, out_specs=[ANY] * n, out_shape=self.out_shape, scratch_shapes=self.scratch,
            compiler_params=pltpu.CompilerParams(has_side_effects=True),
        )(*self.operands)


def _sum_chips(part32, recv16, chip, name):
    nchip, rows, lanes = part32.shape
    rb = _row_block(rows)

    def body(p_ref, own_ref, r_ref, o_ref):
        s = own_ref[...]
        for k in range(nchip - 1):
            s = s + r_ref[k].astype(F32)
        o_ref[...] = s

    return pl.pallas_call(
        body, name=name,
        grid_spec=pltpu.PrefetchScalarGridSpec(
            num_scalar_prefetch=1, grid=(rows // rb,),
            in_specs=[pl.BlockSpec((None, rb, lanes), lambda i, p: (p[0], i, 0)),
                      pl.BlockSpec((nchip - 1, rb, lanes), lambda i, p: (0, i, 0))],
            out_specs=pl.BlockSpec((rb, lanes), lambda i, p: (i, 0))),
        out_shape=jax.ShapeDtypeStruct((rows, lanes), F32),
        compiler_params=_params(("parallel",)),
    )(chip, part32, recv16)


def _share_with_sibling(halves, name):
    n = len(halves)

    def body(*refs):
        ins, outs, send_sems, recv_sems = refs[:n], refs[n:2 * n], refs[2 * n], refs[2 * n + 1]
        x, y, c = _mesh_pos()
        copies = [pltpu.make_async_remote_copy(ins[k], outs[k], send_sems.at[k], recv_sems.at[k],
                                               device_id=(x, y, 1 - c), device_id_type=MESH) for k in range(n)]
        for cp in copies:
            cp.start()
        for cp in copies:
            cp.wait()

    return pl.pallas_call(
        body, name=name, in_specs=[ANY] * n, out_specs=[ANY] * n,
        out_shape=[jax.ShapeDtypeStruct(h.shape, h.dtype) for h in halves],
        scratch_shapes=[pltpu.SemaphoreType.DMA((n,)), pltpu.SemaphoreType.DMA((n,))],
        compiler_params=pltpu.CompilerParams(has_side_effects=True),
    )(*halves)


class _GradRound:
    def __init__(self, tag, pieces):
        self.tag, self.keys = tag, list(pieces)
        self.shapes = [pieces[k].shape[1:] for k in self.keys]
        _, _, c = _mesh_pos()
        core = jnp.reshape(c, (1,)).astype(jnp.int32)
        packs = [_as_halves(pieces[k]) for k in self.keys]
        recvs = _swap_with_sibling(packs, f"grad_{tag}_pair_swap")
        sums = [_add_pair(p, r, core, f"grad_{tag}_pair_add_{i}") for i, (p, r) in enumerate(zip(packs, recvs))]
        self.parts32 = [s[0] for s in sums]
        self.scatter = _ChipScatter([s[1] for s in sums])

    def finish(self, recvs16):
        x, y, c = _mesh_pos()
        chip = jnp.reshape(2 * x + y, (1,)).astype(jnp.int32)
        halves = [_sum_chips(p, r, chip, f"grad_{self.tag}_chip_sum_{i}") for i, (p, r) in enumerate(zip(self.parts32, recvs16))]
        others = _share_with_sibling(halves, f"grad_{self.tag}_pair_share")
        return {k: _from_halves(h, o, c, shape) for k, h, o, shape in zip(self.keys, halves, others, self.shapes)}


def _pack_small(vals):
    parts = []
    for n in REPLICATED:
        flat = vals[n].reshape(-1)
        rows = -(-flat.shape[0] // 128)
        rows = -(-rows // 8) * 8
        parts.append(jnp.pad(flat, (0, rows * 128 - flat.shape[0])).reshape(rows, 128))
    return jnp.concatenate(parts, axis=0)


def _unpack_small(pack, like):
    out, off = {}, 0
    for n in REPLICATED:
        size = like[n].size
        rows = -(-size // 128)
        rows = -(-rows // 8) * 8
        out[n] = pack[off:off + rows].reshape(-1)[:size].reshape(like[n].shape)
        off += rows
    return out


def _all_to_all_small(pack, name):
    rows, lanes = pack.shape
    flips = [(dx, dy, dc) for dx in (0, 1) for dy in (0, 1) for dc in (0, 1)][1:]

    def body(src_ref, out_ref, send_sems, recv_sems, local_sem):
        x, y, c = _mesh_pos()
        me = 4 * x + 2 * y + c
        loc = pltpu.make_async_copy(src_ref, out_ref.at[me], local_sem)
        loc.start()
        copies = []
        for k, (dx, dy, dc) in enumerate(flips):
            peer = (x ^ dx, y ^ dy, c ^ dc)
            cp = pltpu.make_async_remote_copy(src_ref, out_ref.at[me], send_sems.at[k], recv_sems.at[k],
                                              device_id=peer, device_id_type=MESH)
            cp.start()
            copies.append(cp)
        for k, (dx, dy, dc) in enumerate(flips):
            peer = (x ^ dx, y ^ dy, c ^ dc)
            pltpu.make_async_remote_copy(src_ref, out_ref.at[4 * peer[0] + 2 * peer[1] + peer[2]], send_sems.at[k],
                                         recv_sems.at[k], device_id=peer, device_id_type=MESH).wait_recv()
        for cp in copies:
            cp.wait_send()
        loc.wait()

    return pl.pallas_call(
        body, name=name, in_specs=[ANY], out_specs=ANY,
        out_shape=jax.ShapeDtypeStruct((8, rows, lanes), pack.dtype),
        scratch_shapes=[pltpu.SemaphoreType.DMA((7,)), pltpu.SemaphoreType.DMA((7,)), pltpu.SemaphoreType.DMA],
        compiler_params=pltpu.CompilerParams(has_side_effects=True),
    )(pack)


def _sum_devices(stack, name):
    ndev, rows, lanes = stack.shape

    def body(s_ref, o_ref):
        s = s_ref[0]
        for d in range(1, ndev):
            s = s + s_ref[d]
        o_ref[...] = s

    return pl.pallas_call(
        body, name=name, grid=(1,),
        in_specs=[pl.BlockSpec((ndev, rows, lanes), lambda i: (0, 0, 0))],
        out_specs=pl.BlockSpec((rows, lanes), lambda i: (0, 0)),
        out_shape=jax.ShapeDtypeStruct((rows, lanes), F32),
    )(stack)


WEIGHT_ORDER = ("ffn_norm1", "ffn1_w_in", "ffn1_w_out", "mix_norm", "ffn_norm2", "ffn2_w_in", "ffn2_w_out", "ab_w_in",
                "dn_conv_w", "dn_a_log", "dn_dt_bias", "dn_out_norm", "sg_norm", "sg_w", "sg_b", "ab_w_out", "pool_w",
                "pool_scale", "final_norm")
MATRICES = ("ffn1_w_in", "ffn1_w_out", "ffn2_w_in", "ffn2_w_out", "ab_w_in", "ab_w_out", "pool_w")
GATHER_FIRST = (("ffn1_w_in", 0), ("ffn1_w_out", 0), ("ab_w_in", None), ("dn_conv_w", None), ("ab_w_out", None))
GATHER_LATER = (("ffn2_w_in", 0), ("ffn2_w_out", 0), ("ffn1_w_in", 1), ("ffn1_w_out", 1), ("pool_w", None),
                ("pool_scale", None), ("ffn2_w_in", 1), ("ffn2_w_out", 1))
REDUCE_ROUNDS = ((("ffn2_w_in", 1), ("ffn2_w_out", 1), ("ffn1_w_in", 1), ("ffn1_w_out", 1), ("pool_w", None), ("pool_scale", None)),
                 (("ffn2_w_in", 0), ("ffn2_w_out", 0)),
                 (("ffn1_w_in", 0), ("ffn1_w_out", 0), ("ab_w_out", None), ("ab_w_in", None), ("dn_conv_w", None)))


def _as_2d(a):
    return a.reshape(-1, a.shape[-1])


def kernel(x, ffn_norm1, ffn1_w_in, ffn1_w_out, mix_norm, ffn_norm2, ffn2_w_in, ffn2_w_out, ab_w_in, dn_conv_w, dn_a_log, dn_dt_bias, dn_out_norm, sg_norm, sg_w, sg_b, ab_w_out, pool_w, pool_scale, final_norm, loss_target, m_ffn_norm1, m_ffn1_w_in, m_ffn1_w_out, m_mix_norm, m_ffn_norm2, m_ffn2_w_in, m_ffn2_w_out, m_ab_w_in, m_dn_conv_w, m_dn_a_log, m_dn_dt_bias, m_dn_out_norm, m_sg_norm, m_sg_w, m_sg_b, m_ab_w_out, m_pool_w, m_pool_scale, m_final_norm, v_ffn_norm1, v_ffn1_w_in, v_ffn1_w_out, v_mix_norm, v_ffn_norm2, v_ffn2_w_in, v_ffn2_w_out, v_ab_w_in, v_dn_conv_w, v_dn_a_log, v_dn_dt_bias, v_dn_out_norm, v_sg_norm, v_sg_w, v_sg_b, v_ab_w_out, v_pool_w, v_pool_scale, v_final_norm):
    given = dict(locals())
    wts = {n: given[n] for n in WEIGHT_ORDER}
    mom_m = {n: given["m_" + n] for n in WEIGHT_ORDER}
    mom_v = {n: given["v_" + n] for n in WEIGHT_ORDER}

    rep = {n: wts[n] for n in REPLICATED}
    first = _ChipGather(_shard_pieces(wts, GATHER_FIRST), [n in MATRICES for n, _ in GATHER_FIRST])
    w = {**_replicated_layouts(rep), **_layouts_from(dict(zip(GATHER_FIRST, first.run("weight_gather_first"))))}
    later = _ChipGather(_shard_pieces(wts, GATHER_LATER), [n in MATRICES for n, _ in GATHER_LATER])

    loss, dx, g, reduced = _local_step(x[0], loss_target[0], w, reduce=True,
                                       late=(later, lambda arrived: _layouts_from(dict(zip(GATHER_LATER, arrived)))))
    g_rep = _replicated_grads(g)
    grads = {}
    for n in SHARDED:
        layers = [reduced[(n, layer)] for layer in range(wts[n].shape[0])] if (n, 0) in reduced else [reduced[(n, None)]]
        grads[n] = jnp.stack(layers).reshape(wts[n].shape)
    small = _sum_devices(_all_to_all_small(_pack_small(g_rep), "grad_small_exchange"), "grad_small_sum")
    grads.update(_unpack_small(small, rep))

    delta, new_m, new_v = {}, {}, {}
    for n in SHARDED:
        d, m1, v1 = _adamw(_as_2d(wts[n]), _as_2d(grads[n]), _as_2d(mom_m[n]), _as_2d(mom_v[n]), f"adamw_{n}")
        delta[n], new_m[n], new_v[n] = (t.reshape(wts[n].shape) for t in (d, m1, v1))
    d, m1, v1 = _adamw(_pack_small(rep), small, _pack_small({n: mom_m[n] for n in REPLICATED}),
                       _pack_small({n: mom_v[n] for n in REPLICATED}), "adamw_replicated")
    for tgt, packed in ((delta, d), (new_m, m1), (new_v, v1)):
        tgt.update(_unpack_small(packed, rep))

    total = lax.psum(loss[0, 0], ("x", "y", "c"))
    outs = [total, dx[None]]
    for group in (grads, delta, new_m, new_v):
        outs.extend(group[n] for n in WEIGHT_ORDER)
    return tuple(outs)
```

```python
import functools

import jax
import jax.numpy as jnp
from jax import lax
from jax.experimental import pallas as pl
from jax.experimental.pallas import tpu as pltpu

F32, BF16 = jnp.float32, jnp.bfloat16
NORM_EPS = 1e-6
D_MODEL = 1024
D_FF = 2816
N_CHIPS = 4
FF_CHUNK = 2 * D_FF // N_CHIPS
DN_HEADS, DN_DIM, DN_CHUNK, DN_CONV = 4, 128, 64, 4
DN_BLOCK = 2 * DN_CHUNK
DN_PREP_CHUNKS = 4
DN_SCAN_CHUNKS = 8
SG_GROUPS, SG_DIM, SG_CHUNK = 4, 128, 128
POOL_WINDOWS = (2, 4, 8, 16)
POOL_DIM = 256
POOL_HALO = 16
CONV_HALO = 8
PROJ_W = 3200
BA_BLOCK = 3072 // 128
ADAM_LR, ADAM_B1, ADAM_B2, ADAM_EPS, ADAM_WD, ADAM_STEP = 0.001, 0.9, 0.999, 1e-08, 0.01, 10
VMEM_BIG = 52 * 1024 * 1024
FFN_FWD_ROWS = 512
FFN_BWD_ROWS = 256
PACK_LANES = 1024
PACK_ROW_BLOCK = 256
MESH = pl.DeviceIdType.MESH
HI = lax.Precision.HIGHEST
ANY = pl.BlockSpec(memory_space=pl.ANY)


def _params(sem=None, vmem=None):
    return pltpu.CompilerParams(dimension_semantics=sem, vmem_limit_bytes=vmem)


def _dot(a, b):
    return jnp.dot(a, b, preferred_element_type=F32)


def _dot_nt(a, b):
    return lax.dot_general(a, b, (((1,), (1,)), ((), ())), preferred_element_type=F32)


def _dot_tn(a, b):
    return lax.dot_general(a, b, (((0,), (0,)), ((), ())), preferred_element_type=F32)


def _dot_hi(a, b):
    return jnp.dot(a, b, preferred_element_type=F32, precision=HI)


def _dot_mid(a, b):
    return jnp.dot(a, b, preferred_element_type=F32, precision=lax.Precision.HIGH)


def _bf(a):
    return a.astype(BF16)


def _rms_stats(x):
    r = lax.rsqrt(jnp.mean(x * x, axis=-1, keepdims=True) + NORM_EPS)
    return x * r, r


def _rms_bwd(dh, xhat, r, w):
    dhn = dh * w
    return r * (dhn - xhat * jnp.mean(dhn * xhat, axis=-1, keepdims=True))


def _sigmoid(x):
    return jax.nn.sigmoid(x)


def _silu_grad(x, s):
    return s * (1.0 + x * (1.0 - s))


def _gelu(x):
    return 0.5 * x * (1.0 + lax.erf(x * 0.7071067811865476))


def _gelu_grad(x):
    return 0.5 * (1.0 + lax.erf(x * 0.7071067811865476)) + x * jnp.exp(-0.5 * x * x) * 0.3989422804014327


def _softplus(x):
    return jnp.maximum(x, 0.0) + jnp.log(1.0 + jnp.exp(-jnp.abs(x)))


def _tile(n, pref):
    t = min(n, pref)
    assert n % t == 0, (n, t)
    return t


def _ffn_weight_specs():
    once = pl.Buffered(1)
    return [pl.BlockSpec((N_CHIPS, D_MODEL, FF_CHUNK), lambda i: (0, 0, 0), pipeline_mode=once),
            pl.BlockSpec((N_CHIPS, D_FF // N_CHIPS, D_MODEL), lambda i: (0, 0, 0), pipeline_mode=once)]


def _ffn_fwd(x, nw, win, wout, layer, name, exchange=None):
    T, D = x.shape
    tm = _tile(T, FFN_FWD_ROWS)
    nj = N_CHIPS // 2

    def body(x_ref, n_ref, win_ref, wo_ref, xo_ref, g_ref, u_ref, hb_ref):
        xv = x_ref[...]
        xhat, _ = _rms_stats(xv)
        h = _bf(xhat * n_ref[...])
        hb_ref[...] = h
        acc = None
        for j in range(nj):
            cols = slice(j * FF_CHUNK, (j + 1) * FF_CHUNK)
            g = _dot(h, win_ref[j])
            u = _dot(h, win_ref[nj + j])
            g_ref[:, cols] = _bf(g)
            u_ref[:, cols] = _bf(u)
            part = _dot(_bf(g * _sigmoid(g) * u), wo_ref[2 * j:2 * j + 2].reshape(FF_CHUNK, D))
            acc = part if acc is None else acc + part
        xo_ref[...] = xv + 0.5 * acc

    row = pl.BlockSpec((tm, D), lambda i: (i, 0))
    wide = pl.BlockSpec((tm, D_FF), lambda i: (i, 0))
    return _call_with_exchange(
        body, exchange, name=name, steps=T // tm, vmem=VMEM_BIG,
        in_specs=[row, pl.BlockSpec((None, 1, D), lambda i: (layer, 0, 0))] + _ffn_weight_specs(),
        out_specs=[row, wide, wide, row],
        out_shape=[jax.ShapeDtypeStruct((T, D), F32), jax.ShapeDtypeStruct((T, D_FF), BF16),
                   jax.ShapeDtypeStruct((T, D_FF), BF16), jax.ShapeDtypeStruct((T, D), BF16)],
        args=(x, nw, win, wout))


def _ffn_bwd(dxo, x, nw, g, u, win, wout, layer, name, exchange=None):
    T, D = x.shape
    tm = _tile(T, FFN_BWD_ROWS)
    nj = N_CHIPS // 2

    def body(dxo_ref, x_ref, n_ref, g_ref, u_ref, win_ref, wo_ref, dx_ref, dg_ref, du_ref, a_ref, dyb_ref, dn_ref):
        @pl.when(pl.program_id(0) == 0)
        def _():
            dn_ref[...] = jnp.zeros_like(dn_ref)

        dxo = dxo_ref[...]
        dyb = _bf(0.5 * dxo)
        dyb_ref[...] = dyb
        dh = None
        for j in range(nj):
            cols = slice(j * FF_CHUNK, (j + 1) * FF_CHUNK)
            da = _dot_nt(dyb, wo_ref[2 * j:2 * j + 2].reshape(FF_CHUNK, D))
            gv = g_ref[:, cols].astype(F32)
            uv = u_ref[:, cols].astype(F32)
            sg = _sigmoid(gv)
            sl = gv * sg
            dgb = _bf(da * uv * _silu_grad(gv, sg))
            dub = _bf(da * sl)
            a_ref[:, cols] = _bf(sl * uv)
            dg_ref[:, cols] = dgb
            du_ref[:, cols] = dub
            part = _dot_nt(dgb, win_ref[j]) + _dot_nt(dub, win_ref[nj + j])
            dh = part if dh is None else dh + part
        xhat, r = _rms_stats(x_ref[...])
        dx_ref[...] = dxo + _rms_bwd(dh, xhat, r, n_ref[...])
        dn_ref[...] += jnp.sum(dh * xhat, axis=0, keepdims=True)

    row = pl.BlockSpec((tm, D), lambda i: (i, 0))
    wide = pl.BlockSpec((tm, D_FF), lambda i: (i, 0))
    return _call_with_exchange(
        body, exchange, name=name, steps=T // tm, vmem=VMEM_BIG,
        in_specs=[row, row, pl.BlockSpec((None, 1, D), lambda i: (layer, 0, 0)), wide, wide] + _ffn_weight_specs(),
        out_specs=[row, wide, wide, wide, row, pl.BlockSpec((1, D), lambda i: (0, 0))],
        out_shape=[jax.ShapeDtypeStruct((T, D), F32), jax.ShapeDtypeStruct((T, D_FF), BF16),
                   jax.ShapeDtypeStruct((T, D_FF), BF16), jax.ShapeDtypeStruct((T, D_FF), BF16),
                   jax.ShapeDtypeStruct((T, D), BF16), jax.ShapeDtypeStruct((1, D), F32)],
        args=(dxo, x, nw, g, u, win, wout))


def _matmul_tn(a, b, bm, bn, name, stack_n=False):
    T, M = a.shape
    N = b.shape[1]
    tk = _tile(T, 1024)
    bm, bn = _tile(M, bm), _tile(N, bn)

    def body(a_ref, b_ref, o_ref):
        @pl.when(pl.program_id(2) == 0)
        def _():
            o_ref[...] = jnp.zeros_like(o_ref)

        o_ref[...] += _dot_tn(_bf(a_ref[...]), _bf(b_ref[...]))

    if stack_n:
        out_spec = pl.BlockSpec((None, bm, bn), lambda m, n, k: (n, m, 0))
        out_shape = jax.ShapeDtypeStruct((N // bn, M, bn), F32)
    else:
        out_spec = pl.BlockSpec((bm, bn), lambda m, n, k: (m, n))
        out_shape = jax.ShapeDtypeStruct((M, N), F32)
    return pl.pallas_call(
        body, name=name, grid=(M // bm, N // bn, T // tk),
        in_specs=[pl.BlockSpec((tk, bm), lambda m, n, k: (k, m)),
                  pl.BlockSpec((tk, bn), lambda m, n, k: (k, n))],
        out_specs=out_spec, out_shape=out_shape,
        compiler_params=_params(("parallel", "parallel", "arbitrary"), VMEM_BIG),
    )(a, b)


def _matmul(a, b, name, trans_b=False, res=None, out_dtype=F32):
    T, K = a.shape
    N = b.shape[0] if trans_b else b.shape[1]
    tm = _tile(T, 512)

    def body(*refs):
        a_ref, b_ref = refs[0], refs[1]
        o_ref = refs[-1]
        av, bv = _bf(a_ref[...]), _bf(b_ref[...])
        acc = _dot_nt(av, bv) if trans_b else _dot(av, bv)
        if res is not None:
            acc = acc + refs[2][...]
        o_ref[...] = acc.astype(out_dtype)

    in_specs = [pl.BlockSpec((tm, K), lambda i: (i, 0)), pl.BlockSpec(b.shape, lambda i: (0, 0))]
    args = [a, b]
    if res is not None:
        in_specs.append(pl.BlockSpec((tm, N), lambda i: (i, 0)))
        args.append(res)
    return pl.pallas_call(
        body, name=name, grid=(T // tm,), in_specs=in_specs,
        out_specs=pl.BlockSpec((tm, N), lambda i: (i, 0)),
        out_shape=jax.ShapeDtypeStruct((T, N), out_dtype),
        compiler_params=_params(("parallel",), VMEM_BIG),
    )(*args)


def _rms_fwd_call(x, nw, layer, name):
    T, D = x.shape
    tm = _tile(T, 512)

    def body(x_ref, n_ref, o_ref):
        xhat, _ = _rms_stats(x_ref[...])
        o_ref[...] = _bf(xhat * n_ref[...])

    return pl.pallas_call(
        body, name=name, grid=(T // tm,),
        in_specs=[pl.BlockSpec((tm, D), lambda i: (i, 0)), pl.BlockSpec((None, 1, D), lambda i: (layer, 0, 0))],
        out_specs=pl.BlockSpec((tm, D), lambda i: (i, 0)),
        out_shape=jax.ShapeDtypeStruct((T, D), BF16),
        compiler_params=_params(("parallel",)),
    )(x, nw)


def _rms_bwd_call(dh, x, nw, dres, layer, name):
    T, D = x.shape
    tm = _tile(T, 512)

    def body(dh_ref, x_ref, n_ref, dr_ref, dx_ref, dn_ref):
        @pl.when(pl.program_id(0) == 0)
        def _():
            dn_ref[...] = jnp.zeros_like(dn_ref)

        xhat, r = _rms_stats(x_ref[...])
        dh_v = dh_ref[...]
        dx_ref[...] = dr_ref[...] + _rms_bwd(dh_v, xhat, r, n_ref[...])
        dn_ref[...] += jnp.sum(dh_v * xhat, axis=0, keepdims=True)

    row = pl.BlockSpec((tm, D), lambda i: (i, 0))
    return pl.pallas_call(
        body, name=name, grid=(T // tm,),
        in_specs=[row, row, pl.BlockSpec((None, 1, D), lambda i: (layer, 0, 0)), row],
        out_specs=[row, pl.BlockSpec((1, D), lambda i: (0, 0))],
        out_shape=[jax.ShapeDtypeStruct((T, D), F32), jax.ShapeDtypeStruct((1, D), F32)],
        compiler_params=_params(("arbitrary",)),
    )(dh, x, nw, dres)


def _shift_rows(x, s):
    n = x.shape[0]
    s = s % n
    return x if s == 0 else pltpu.roll(x, s, 0)


def _conv_fwd(proj, conv_w, name):
    T = proj.shape[0]
    C = 3 * DN_HEADS * DN_DIM
    cb = 512
    tm = _tile(T, 512)
    hb = tm // CONV_HALO

    def body(x_ref, xp_ref, w_ref, o_ref):
        i = pl.program_id(1)
        prev = jnp.where(i == 0, 0.0, xp_ref[...])
        ext = jnp.concatenate([prev, x_ref[...]], axis=0)
        w = w_ref[...]
        y = ext * w[DN_CONV - 1:DN_CONV, :]
        for k in range(DN_CONV - 1):
            y = y + _shift_rows(ext, DN_CONV - 1 - k) * w[k:k + 1, :]
        y = y[CONV_HALO:, :]
        o_ref[...] = y * _sigmoid(y)

    return pl.pallas_call(
        body, name=name, grid=(C // cb, T // tm),
        in_specs=[pl.BlockSpec((tm, cb), lambda c, i: (i, c)),
                  pl.BlockSpec((CONV_HALO, cb), lambda c, i: (jnp.maximum(i * hb - 1, 0), c)),
                  pl.BlockSpec((DN_CONV, cb), lambda c, i: (0, c))],
        out_specs=pl.BlockSpec((tm, cb), lambda c, i: (i, c)),
        out_shape=jax.ShapeDtypeStruct((T, C), F32),
        compiler_params=_params(("parallel", "parallel")),
    )(proj, proj, conv_w)


def _conv_bwd(dy, proj, conv_w, name):
    T = proj.shape[0]
    C = 3 * DN_HEADS * DN_DIM
    cb = 512
    tm = _tile(T, 512)
    hb = tm // CONV_HALO
    nt = T // tm

    def body(x_ref, xp_ref, xn_ref, dy_ref, dyn_ref, w_ref, dx_ref, dw_ref):
        i = pl.program_id(1)

        @pl.when(i == 0)
        def _():
            dw_ref[...] = jnp.zeros_like(dw_ref)

        prev = jnp.where(i == 0, 0.0, xp_ref[...])
        ext = jnp.concatenate([prev, x_ref[...], xn_ref[...]], axis=0)
        dy_ext = jnp.concatenate([jnp.zeros((CONV_HALO, cb), F32), dy_ref[...],
                                  jnp.where(i == nt - 1, 0.0, dyn_ref[...])], axis=0)
        w = w_ref[...]
        shifted = [_shift_rows(ext, DN_CONV - 1 - k) for k in range(DN_CONV)]
        y = shifted[0] * w[0:1, :]
        for k in range(1, DN_CONV):
            y = y + shifted[k] * w[k:k + 1, :]
        s = _sigmoid(y)
        dpre = dy_ext * _silu_grad(y, s)
        dx = dpre * w[DN_CONV - 1:DN_CONV, :]
        for k in range(DN_CONV - 1):
            dx = dx + _shift_rows(dpre, -(DN_CONV - 1 - k)) * w[k:k + 1, :]
        dx_ref[...] = _bf(dx[CONV_HALO:CONV_HALO + tm, :])
        rows = [jnp.sum((dpre * shifted[k])[CONV_HALO:CONV_HALO + tm, :], axis=0, keepdims=True) for k in range(DN_CONV)]
        dw_ref[...] += jnp.concatenate(rows, axis=0)

    last_halo = T // CONV_HALO - 1
    return pl.pallas_call(
        body, name=name, grid=(C // cb, nt),
        in_specs=[pl.BlockSpec((tm, cb), lambda c, i: (i, c)),
                  pl.BlockSpec((CONV_HALO, cb), lambda c, i: (jnp.maximum(i * hb - 1, 0), c)),
                  pl.BlockSpec((CONV_HALO, cb), lambda c, i: (jnp.minimum((i + 1) * hb, last_halo), c)),
                  pl.BlockSpec((tm, cb), lambda c, i: (i, c)),
                  pl.BlockSpec((CONV_HALO, cb), lambda c, i: (jnp.minimum((i + 1) * hb, last_halo), c)),
                  pl.BlockSpec((DN_CONV, cb), lambda c, i: (0, c))],
        out_specs=[pl.BlockSpec((tm, cb), lambda c, i: (i, c)),
                   pl.BlockSpec((DN_CONV, cb), lambda c, i: (0, c))],
        out_shape=[jax.ShapeDtypeStruct((T, C), BF16), jax.ShapeDtypeStruct((DN_CONV, C), F32)],
        compiler_params=_params(("parallel", "arbitrary")),
    )(proj, proj, proj, dy, dy, conv_w)


def _unit_lower_inverse(low, eye):
    return _unit_lower_inverses([low], eye)[0]


def _unit_lower_inverses(lows, eye):
    def each(fn, *lists):
        return [fn(*args) for args in zip(*lists)]

    p1 = [-low for low in lows]
    p2 = each(_dot_mid, p1, p1)
    p4 = each(_dot_mid, p2, p2)
    a = each(lambda x, y: eye + x + y + _dot_mid(x, y), p1, p2)
    p8 = each(_dot_mid, p4, p4)
    p16 = each(_dot_mid, p8, p8)
    b = each(lambda x, y: eye + x + y + _dot_mid(x, y), p4, p8)
    p32 = each(_dot_mid, p16, p16)
    ab = each(_dot_mid, a, b)
    c = each(lambda x, y: eye + x + y + _dot_mid(x, y), p16, p32)
    return each(_dot_mid, ab, c)


def _interleave(chains):
    results = [None] * len(chains)
    live = list(range(len(chains)))
    while live:
        for i in list(live):
            try:
                next(chains[i])
            except StopIteration as stop:
                results[i] = stop.value
                live.remove(i)
        yield
    return results


def _run_interleaved(chains):
    rounds = _interleave(chains)
    while True:
        try:
            next(rounds)
        except StopIteration as stop:
            return stop.value


def _l2_unit(x):
    r = lax.rsqrt(jnp.sum(x * x, axis=-1, keepdims=True) + NORM_EPS)
    return x * r, r


class _BlockMasks:
    def __init__(self):
        n = DN_BLOCK
        row = lax.broadcasted_iota(jnp.int32, (n, n), 0)
        col = lax.broadcasted_iota(jnp.int32, (n, n), 1)
        same = (row // DN_CHUNK) == (col // DN_CHUNK)
        self.lower, self.strict_lower = same & (row >= col), same & (row > col)
        self.upper, self.strict_upper = same & (row <= col), same & (row < col)
        self.eye = (row == col).astype(F32)
        self.first = lax.broadcasted_iota(jnp.int32, (n, 1), 0) < DN_CHUNK


def _dn_gates(ba, hp):
    coef = -jnp.exp(hp[0:1, :])
    pre = ba + hp[1:2, :]
    return _sigmoid(ba), coef * _softplus(pre), coef, pre


def _dn_block_gates(mk, ba, hp):
    assert DN_BLOCK == 2 * DN_CHUNK
    beta_t, graw_t, coef, pre = _dn_gates(ba, hp)
    gcum_t = _dot_hi(mk.lower.astype(F32), graw_t)
    gl_t = jnp.where(mk.first, gcum_t[DN_CHUNK - 1:DN_CHUNK, :], gcum_t[DN_BLOCK - 1:DN_BLOCK, :])
    return beta_t, gcum_t, gl_t, graw_t, coef, pre


def _dn_local(mk, qraw, kraw, bc, gc, gl):
    f = {}
    f["qn"], f["rq"] = _l2_unit(qraw)
    qh = f["qn"] * (DN_DIM ** -0.5)
    kh, f["rk"] = _l2_unit(kraw)
    gr = jnp.broadcast_to(gc, (DN_BLOCK, DN_BLOCK)).T
    dec = jnp.where(mk.lower, jnp.exp(jnp.where(mk.lower, gc - gr, 0.0)), 0.0)
    kb = kh * bc
    mkk = _dot_nt(_bf(kb), _bf(kh))
    eg = jnp.exp(gc)
    mqk = _dot_nt(_bf(qh), _bf(kh))
    etl = jnp.exp(gl - gc)
    f.update(qh=qh, kh=kh, gr=gr, dec=dec, kb=kb, mkk=mkk, eg=eg, mqk=mqk, attn=mqk * dec, etl=etl, qd=qh * eg, kt=kh * etl)
    return f


def _dn_specs(rows, rev=None):
    at = (lambda n: n) if rev is None else rev
    hw = DN_HEADS * DN_DIM
    return dict(
        qkv=[pl.BlockSpec((rows, hw), lambda n, j=j: (at(n), j)) for j in range(3)],
        ba=pl.BlockSpec((rows, 128), lambda n: (at(n), BA_BLOCK)),
        hp=pl.BlockSpec((8, 128), lambda n: (0, 0)),
        tok=pl.BlockSpec((rows, hw), lambda n: (at(n), 0)),
        attn=pl.BlockSpec((rows, DN_HEADS * DN_CHUNK), lambda n: (at(n), 0)),
        gate=pl.BlockSpec((rows // DN_CHUNK, 8, 128), lambda n: (at(n), 0, 0)),
        state=pl.BlockSpec((rows // DN_CHUNK, DN_HEADS, DN_DIM, DN_DIM), lambda n: (at(n), 0, 0, 0)),
    )


def _call_with_exchange(body, exchange, *, name, steps, in_specs, out_specs, out_shape, args, vmem=None, scratch_shapes=()):
    if exchange is None:
        res = pl.pallas_call(body, name=name, grid=(steps,), in_specs=in_specs, out_specs=out_specs, out_shape=out_shape,
                             scratch_shapes=list(scratch_shapes), compiler_params=_params(("arbitrary",), vmem))(*args)
        return list(res), None
    n_in, n_out, m, n_scr = len(in_specs), len(out_specs), len(exchange.operands), len(scratch_shapes)

    def hosted(*refs):
        ins, ex_ins = refs[:n_in], refs[n_in:n_in + m]
        outs, ex_outs = refs[n_in + m:n_in + m + n_out], refs[n_in + m + n_out:n_in + 2 * m + n_out]
        scratch, sems = refs[n_in + 2 * m + n_out:n_in + 2 * m + n_out + n_scr], refs[n_in + 2 * m + n_out + n_scr:]

        @pl.when(pl.program_id(0) == 0)
        def _():
            exchange.start(ex_ins, ex_outs, sems)

        body(*ins, *outs, *scratch)

        @pl.when(pl.program_id(0) == steps - 1)
        def _():
            exchange.finish(ex_ins, ex_outs, sems)

    res = pl.pallas_call(
        hosted, name=name, grid=(steps,), in_specs=list(in_specs) + [ANY] * m, out_specs=list(out_specs) + [ANY] * m,
        out_shape=list(out_shape) + list(exchange.out_shape), scratch_shapes=list(scratch_shapes) + list(exchange.scratch),
        compiler_params=pltpu.CompilerParams(dimension_semantics=("arbitrary",), vmem_limit_bytes=vmem, has_side_effects=True),
    )(*args, *exchange.operands)
    return list(res[:n_out]), exchange.finalize(list(res[n_out:]))


def _dn_prep(qkv, proj, hp, name, exchange=None):
    T = qkv.shape[0]
    n_chunks = T // DN_CHUNK
    blocks = max(1, min(DN_PREP_CHUNKS, n_chunks) * DN_CHUNK // DN_BLOCK)
    group = blocks * DN_BLOCK // DN_CHUNK
    rows = blocks * DN_BLOCK
    hw = DN_HEADS * DN_DIM

    def body(q_ref, k_ref, v_ref, ba_ref, hp_ref, u_ref, w_ref, p_ref, qd_ref, kt_ref, gl_ref, inv_ref):
        mk = _BlockMasks()
        hp_v = hp_ref[...]
        chains = []
        for j in range(blocks):
            rs = slice(j * DN_BLOCK, (j + 1) * DN_BLOCK)
            beta_t, gcum_t, gl_t = _dn_block_gates(mk, ba_ref[rs, :], hp_v)[:3]
            for c in range(DN_BLOCK // DN_CHUNK):
                gl_ref[j * (DN_BLOCK // DN_CHUNK) + c] = jnp.broadcast_to(gl_t[c * DN_CHUNK:c * DN_CHUNK + 1, :], (8, 128))
            for h in range(DN_HEADS):
                sl = slice(h * DN_DIM, (h + 1) * DN_DIM)
                gate = slice(DN_HEADS + h, DN_HEADS + h + 1)
                bc = beta_t[:, h:h + 1]
                f = _dn_local(mk, q_ref[rs, sl], k_ref[rs, sl], bc, gcum_t[:, gate], gl_t[:, gate])
                for c in range(DN_BLOCK // DN_CHUNK):
                    cr = slice(c * DN_CHUNK, (c + 1) * DN_CHUNK)
                    p_ref[j * DN_BLOCK + c * DN_CHUNK:j * DN_BLOCK + (c + 1) * DN_CHUNK, h * DN_CHUNK:(h + 1) * DN_CHUNK] = _bf(f["attn"][cr, cr])
                qd_ref[rs, sl] = _bf(f["qd"])
                kt_ref[rs, sl] = _bf(f["kt"])
                chains.append((rs, sl, bc, f))
        invs = _unit_lower_inverses([jnp.where(mk.strict_lower, f["mkk"] * f["dec"], 0.0) for _, _, _, f in chains], mk.eye)
        for (rs, sl, bc, f), inv in zip(chains, invs):
            inv_ref[rs, sl] = inv
            sol = _dot_mid(inv, jnp.concatenate([v_ref[rs, sl] * bc, f["kb"] * f["eg"]], axis=1))
            u_ref[rs, sl] = sol[:, :DN_DIM]
            w_ref[rs, sl] = _bf(sol[:, DN_DIM:])

    sp = _dn_specs(rows)
    tok16 = jax.ShapeDtypeStruct((T, hw), BF16)
    return _call_with_exchange(
        body, exchange, name=name, steps=n_chunks // group,
        in_specs=sp["qkv"] + [sp["ba"], sp["hp"]],
        out_specs=[sp["tok"], sp["tok"], sp["attn"], sp["tok"], sp["tok"], sp["gate"], sp["tok"]],
        out_shape=[jax.ShapeDtypeStruct((T, hw), F32), tok16, jax.ShapeDtypeStruct((T, DN_HEADS * DN_CHUNK), BF16),
                   tok16, tok16, jax.ShapeDtypeStruct((n_chunks, 8, 128), F32), jax.ShapeDtypeStruct((T, hw), F32)],
        args=(qkv, qkv, qkv, proj, hp))


def _dn_scan(u, w, p, qd, kt, gl, name, exchange=None):
    T = u.shape[0]
    n_chunks = T // DN_CHUNK
    group = min(DN_SCAN_CHUNKS, n_chunks)
    rows = group * DN_CHUNK
    hw = DN_HEADS * DN_DIM

    def body(u_ref, w_ref, p_ref, qd_ref, kt_ref, gl_ref, o_ref, vn_ref, sall_ref, s_s):
        @pl.when(pl.program_id(0) == 0)
        def _():
            s_s[...] = jnp.zeros_like(s_s)

        state = [s_s[h] for h in range(DN_HEADS)]
        for j in range(group):
            rs = slice(j * DN_CHUNK, (j + 1) * DN_CHUNK)
            for h in range(DN_HEADS):
                sl = slice(h * DN_DIM, (h + 1) * DN_DIM)
                sall_ref[j, h] = state[h]
                sb = _bf(state[h])
                vnb = _bf(u_ref[rs, sl] - _dot(w_ref[rs, sl], sb))
                vn_ref[rs, sl] = vnb
                o_ref[rs, sl] = _dot(qd_ref[rs, sl], sb) + _dot(p_ref[rs, h * DN_CHUNK:(h + 1) * DN_CHUNK], vnb)
                egl = jnp.exp(gl_ref[j, 0:1, DN_HEADS + h:DN_HEADS + h + 1])
                state[h] = state[h] * egl + _dot_tn(kt_ref[rs, sl], vnb)
        for h in range(DN_HEADS):
            s_s[h] = state[h]

    sp = _dn_specs(rows)
    return _call_with_exchange(
        body, exchange, name=name, steps=n_chunks // group,
        in_specs=[sp["tok"], sp["tok"], sp["attn"], sp["tok"], sp["tok"], sp["gate"]],
        out_specs=[sp["tok"], sp["tok"], sp["state"]],
        out_shape=[jax.ShapeDtypeStruct((T, hw), F32), jax.ShapeDtypeStruct((T, hw), BF16),
                   jax.ShapeDtypeStruct((n_chunks, DN_HEADS, DN_DIM, DN_DIM), F32)],
        scratch_shapes=[pltpu.VMEM((DN_HEADS, DN_DIM, DN_DIM), F32)],
        args=(u, w, p, qd, kt, gl))


def _dn_scan_bwd(w, p, qd, kt, gl, vn, sall, do, name):
    T = w.shape[0]
    n_chunks = T // DN_CHUNK
    group = min(DN_SCAN_CHUNKS, n_chunks)
    rows = group * DN_CHUNK
    hw = DN_HEADS * DN_DIM
    last = n_chunks // group - 1

    def body(w_ref, p_ref, qd_ref, kt_ref, gl_ref, vn_ref, sall_ref, do_ref, dvn_ref, dkt_ref, dgl_ref, ds_s):
        @pl.when(pl.program_id(0) == 0)
        def _():
            ds_s[...] = jnp.zeros_like(ds_s)

        lane = lax.broadcasted_iota(jnp.int32, (8, 128), 1)
        d_state = [ds_s[h] for h in range(DN_HEADS)]
        for j in reversed(range(group)):
            rs = slice(j * DN_CHUNK, (j + 1) * DN_CHUNK)
            dgl_tile = jnp.zeros((8, 128), F32)
            for h in range(DN_HEADS):
                sl = slice(h * DN_DIM, (h + 1) * DN_DIM)
                d_out = _bf(do_ref[rs, sl])
                d_new = d_state[h]
                d_newb = _bf(d_new)
                d_vn = _dot_tn(p_ref[rs, h * DN_CHUNK:(h + 1) * DN_CHUNK], d_out) + _dot(kt_ref[rs, sl], d_newb)
                dvn_ref[rs, sl] = d_vn
                dkt_ref[rs, sl] = _dot_nt(vn_ref[rs, sl], d_newb)
                egl = jnp.exp(gl_ref[j, 0:1, DN_HEADS + h:DN_HEADS + h + 1])
                prod = jnp.sum(d_new * sall_ref[j, h], axis=1, keepdims=True)
                dgl_tile = jnp.where(lane == DN_HEADS + h, jnp.sum(prod, axis=0, keepdims=True) * egl, dgl_tile)
                d_state[h] = d_new * egl + _dot_tn(qd_ref[rs, sl], d_out) - _dot_tn(w_ref[rs, sl], _bf(d_vn))
            dgl_ref[j] = dgl_tile
        for h in range(DN_HEADS):
            ds_s[h] = d_state[h]

    sp = _dn_specs(rows, rev=lambda n: last - n)
    return pl.pallas_call(
        body, name=name, grid=(n_chunks // group,),
        in_specs=[sp["tok"], sp["attn"], sp["tok"], sp["tok"], sp["gate"], sp["tok"], sp["state"], sp["tok"]],
        out_specs=[sp["tok"], sp["tok"], sp["gate"]],
        out_shape=[jax.ShapeDtypeStruct((T, hw), F32), jax.ShapeDtypeStruct((T, hw), F32),
                   jax.ShapeDtypeStruct((n_chunks, 8, 128), F32)],
        scratch_shapes=[pltpu.VMEM((DN_HEADS, DN_DIM, DN_DIM), F32)],
        compiler_params=_params(("arbitrary",)),
    )(w, p, qd, kt, gl, vn, sall, do)


def _dn_prep_bwd(qkv, proj, hp, sall, vn, do, dvn, dkt, dgl, inv, u, w, name, exchange=None):
    T = qkv.shape[0]
    n_chunks = T // DN_CHUNK
    blocks = max(1, min(DN_PREP_CHUNKS, n_chunks) * DN_CHUNK // DN_BLOCK)
    per_block = DN_BLOCK // DN_CHUNK
    group = blocks * per_block
    rows = blocks * DN_BLOCK
    hw = DN_HEADS * DN_DIM
    first_rows, second_rows = slice(0, DN_CHUNK), slice(DN_CHUNK, DN_BLOCK)

    def rowsum(x):
        return jnp.sum(x, axis=1, keepdims=True)

    def by_chunk(x, s0, s1, fn):
        return jnp.concatenate([fn(x[first_rows], s0), fn(x[second_rows], s1)], axis=0)

    def body(q_ref, k_ref, v_ref, ba_ref, hp_ref, sall_ref, vn_ref, do_ref, dvn_ref, dkt_ref, dgl_ref,
             inv_ref, u_ref, w_ref, dqkv_ref, dba_ref, dhp_ref):
        @pl.when(pl.program_id(0) == 0)
        def _():
            dhp_ref[...] = jnp.zeros_like(dhp_ref)

        mk = _BlockMasks()
        hp_v = hp_ref[...]
        chains = []
        for j in range(blocks):
            rs = slice(j * DN_BLOCK, (j + 1) * DN_BLOCK)
            chains.append(one_block(mk, hp_v, *(r.at[rs, :] for r in (q_ref, k_ref, v_ref, ba_ref)),
                                    sall_ref.at[pl.ds(j * per_block, per_block)],
                                    *(r.at[rs, :] for r in (vn_ref, do_ref, dvn_ref, dkt_ref)),
                                    dgl_ref.at[pl.ds(j * per_block, per_block)],
                                    *(r.at[rs, :] for r in (inv_ref, u_ref, w_ref)),
                                    *(dqkv_ref.at[rs, pl.ds(i * hw, hw)] for i in range(3)), dba_ref.at[rs, :]))
        total = jnp.zeros((8, 128), F32)
        for part in _run_interleaved(chains):
            total = total + part
        dhp_ref[...] += total

    def one_block(mk, hp_v, q_ref, k_ref, v_ref, ba_ref, state_ref, vn_ref, do_ref, dvn_ref, dkt_ref, dgl_ref,
                  inv_ref, u_ref, w_ref, dq_ref, dk_ref, dv_ref, dba_ref):
        ba = ba_ref[...]
        beta_t, gcum_t, gl_t, graw_t, coef, pre = _dn_block_gates(mk, ba, hp_v)
        lane = lax.broadcasted_iota(jnp.int32, (DN_BLOCK, 128), 1)
        rowi = lax.broadcasted_iota(jnp.int32, (DN_BLOCK, 1), 0)

        def head(h):
            sl = slice(h * DN_DIM, (h + 1) * DN_DIM)
            gate = slice(DN_HEADS + h, DN_HEADS + h + 1)
            gc = gcum_t[:, gate]
            bc = beta_t[:, h:h + 1]
            sb0, sb1 = _bf(state_ref[0, h]), _bf(state_ref[1, h])
            vh = v_ref[:, sl]
            f = _dn_local(mk, q_ref[:, sl], k_ref[:, sl], bc, gc, gl_t[:, gate])
            yield
            qh, kh, kb, dec, eg, etl = f["qh"], f["kh"], f["kb"], f["dec"], f["eg"], f["etl"]
            qd, kt = f["qd"], f["kt"]
            qb, kbf, kbb = _bf(qh), _bf(kh), _bf(kb)
            dec_t = jnp.where(mk.upper, jnp.exp(jnp.where(mk.upper, f["gr"] - gc, 0.0)), 0.0)
            mkk_t = f["mkk"].T
            inv_t = inv_ref[:, sl].T
            mqk_t = f["mqk"].T

            d_out = _bf(do_ref[:, sl])
            vnb = vn_ref[:, sl]
            d_qd = by_chunk(d_out, sb0, sb1, _dot_nt)
            d_attn = _dot_nt(d_out, vnb)
            d_attn_t = _dot_nt(vnb, d_out)
            d_vn = dvn_ref[:, sl]
            d_kt = dkt_ref[:, sl]
            d_w = -by_chunk(_bf(d_vn), sb0, sb1, _dot_nt)
            yield
            d_rhs = _dot_mid(inv_t, jnp.concatenate([d_vn, d_w], axis=1))
            yield
            d_bu, d_bw = d_rhs[:, :DN_DIM], d_rhs[:, DN_DIM:]
            ub, wb, d_bub, d_bwb = _bf(u_ref[:, sl]), w_ref[:, sl], _bf(d_bu), _bf(d_bw)
            d_low = -(_dot_nt(d_bub, ub) + _dot_nt(d_bwb, wb))
            d_low_t = -(_dot_nt(ub, d_bub) + _dot_nt(wb, d_bwb))
            yield
            d_mkk = jnp.where(mk.strict_lower, d_low * dec, 0.0)
            d_mkk_t = jnp.where(mk.strict_upper, d_low_t * dec_t, 0.0)
            d_mqk = jnp.where(mk.lower, d_attn * dec, 0.0)
            d_mqk_t = jnp.where(mk.upper, d_attn_t * dec_t, 0.0)
            bw = kb * eg
            d_kb = _dot(_bf(d_mkk), kbf) + d_bw * eg
            d_k = _dot(_bf(d_mkk_t), kbb) + _dot(_bf(d_mqk_t), qb) + d_kt * etl + d_kb * bc
            d_q = _dot(_bf(d_mqk), kbf) + d_qd * eg
            yield
            d_beta = rowsum(d_kb * kh) + rowsum(d_bu * vh)
            dv_ref[:, sl] = d_bu * bc
            e_mat = d_mkk * f["mkk"] + d_mqk * f["mqk"]
            e_mat_t = d_mkk_t * mkk_t + d_mqk_t * mqk_t
            kt_term = rowsum(d_kt * kt)
            d_g = rowsum(e_mat) - rowsum(e_mat_t) + rowsum(d_qd * qd) + rowsum(d_bw * bw) - kt_term
            for c, chunk_rows in enumerate((mk.first, ~mk.first)):
                d_glast = dgl_ref[c, 0:1, gate] + jnp.sum(jnp.where(chunk_rows, kt_term, 0.0), axis=0, keepdims=True)
                d_g = d_g + jnp.where(rowi == (c + 1) * DN_CHUNK - 1, d_glast, 0.0)
            qn = f["qn"]
            d_qs = d_q * (DN_DIM ** -0.5)
            dq_ref[:, sl] = f["rq"] * (d_qs - qn * rowsum(d_qs * qn))
            dk_ref[:, sl] = f["rk"] * (d_k - kh * rowsum(d_k * kh))
            return d_g, d_beta

        per_head = yield from _interleave([head(h) for h in range(DN_HEADS)])
        dgcum_t = jnp.zeros((DN_BLOCK, 128), F32)
        dbeta_t = jnp.zeros((DN_BLOCK, 128), F32)
        for h, (d_g, d_beta) in enumerate(per_head):
            dgcum_t = jnp.where(lane == DN_HEADS + h, d_g, dgcum_t)
            dbeta_t = jnp.where(lane == h, d_beta, dbeta_t)
        dgraw_t = _dot_hi(mk.upper.astype(F32), dgcum_t)
        sp = _sigmoid(pre)
        d_pre = dgraw_t * coef * sp
        dba_ref[...] = jnp.where(lane < DN_HEADS, dbeta_t * beta_t * (1.0 - beta_t),
                                 jnp.where(lane < 2 * DN_HEADS, d_pre, 0.0))
        in_g = (lane >= DN_HEADS) & (lane < 2 * DN_HEADS)
        d_alog = jnp.sum(jnp.where(in_g, dgraw_t * graw_t, 0.0), axis=0, keepdims=True)
        d_dtb = jnp.sum(jnp.where(in_g, d_pre, 0.0), axis=0, keepdims=True)
        return jnp.concatenate([d_alog, d_dtb, jnp.zeros((6, 128), F32)], axis=0)

    sp = _dn_specs(rows)
    return _call_with_exchange(
        body, exchange, name=name, steps=n_chunks // group,
        in_specs=sp["qkv"] + [sp["ba"], sp["hp"], sp["state"]] + [sp["tok"]] * 4 + [sp["gate"]] + [sp["tok"]] * 3,
        out_specs=[pl.BlockSpec((rows, 3 * hw), lambda n: (n, 0)), pl.BlockSpec((rows, 128), lambda n: (n, 0)), sp["hp"]],
        out_shape=[jax.ShapeDtypeStruct((T, 3 * hw), F32), jax.ShapeDtypeStruct((T, 128), F32),
                   jax.ShapeDtypeStruct((8, 128), F32)],
        args=(qkv, qkv, qkv, proj, hp, sall, vn, do, dvn, dkt, dgl, inv, u, w))


def _mix_fwd(o, proj, dn_norm, sg_norm, sg_w, sg_bt, name):
    T = o.shape[0]
    tm = _tile(T, 512)
    hw = DN_HEADS * DN_DIM
    nc = tm // SG_CHUNK

    def body(o_ref, z_ref, su_ref, sv_ref, dnn_ref, sgn_ref, sgw_ref, sgb_ref, mix_ref):
        dnn = dnn_ref[...]
        for h in range(DN_HEADS):
            sl = slice(h * DN_DIM, (h + 1) * DN_DIM)
            xhat, _ = _rms_stats(o_ref[:, sl])
            z = z_ref[:, sl]
            mix_ref[:, sl] = _bf(xhat * dnn * (z * _sigmoid(z)))
        tri = lax.broadcasted_iota(jnp.int32, (SG_CHUNK, SG_CHUNK), 0) >= lax.broadcasted_iota(jnp.int32, (SG_CHUNK, SG_CHUNK), 1)
        for g in range(SG_GROUPS):
            sl = slice(g * SG_DIM, (g + 1) * SG_DIM)
            xhat, _ = _rms_stats(_gelu(sv_ref[:, sl]))
            svn = _bf(xhat * sgn_ref[g:g + 1, :])
            sua = _gelu(su_ref[:, sl])
            wt = _bf(jnp.where(tri, sgw_ref[g], 0.0))
            bias = sgb_ref[:, g:g + 1]
            for c in range(nc):
                rows = slice(c * SG_CHUNK, (c + 1) * SG_CHUNK)
                mixed = _dot(wt, svn[rows, :]) + bias
                mix_ref[rows, hw + g * SG_DIM:hw + (g + 1) * SG_DIM] = _bf(sua[rows, :] * mixed)

    full = lambda shape: pl.BlockSpec(shape, lambda i: (0,) * len(shape))
    return pl.pallas_call(
        body, name=name, grid=(T // tm,),
        in_specs=[pl.BlockSpec((tm, hw), lambda i: (i, 0)),
                  pl.BlockSpec((tm, hw), lambda i: (i, 3)),
                  pl.BlockSpec((tm, hw), lambda i: (i, 4)),
                  pl.BlockSpec((tm, hw), lambda i: (i, 5)),
                  full((1, DN_DIM)), full((SG_GROUPS, SG_DIM)), full((SG_GROUPS, SG_CHUNK, SG_CHUNK)),
                  full((SG_CHUNK, 128))],
        out_specs=pl.BlockSpec((tm, 2 * hw), lambda i: (i, 0)),
        out_shape=jax.ShapeDtypeStruct((T, 2 * hw), BF16),
        compiler_params=_params(("parallel",)),
    )(o, proj, proj, proj, dn_norm, sg_norm, sg_w, sg_bt)


def _mix_bwd(dmix, o, proj, dn_norm, sg_norm, sg_w, sg_bt, name):
    T = o.shape[0]
    tm = _tile(T, 512)
    hw = DN_HEADS * DN_DIM
    nc = tm // SG_CHUNK

    def body(dm_ref, o_ref, z_ref, su_ref, sv_ref, dnn_ref, sgn_ref, sgw_ref, sgb_ref,
             do_ref, dz_ref, ddnn_ref, dsgn_ref, dsgw_ref, dsgb_ref):
        @pl.when(pl.program_id(0) == 0)
        def _():
            ddnn_ref[...] = jnp.zeros_like(ddnn_ref)
            dsgn_ref[...] = jnp.zeros_like(dsgn_ref)
            dsgw_ref[...] = jnp.zeros_like(dsgw_ref)
            dsgb_ref[...] = jnp.zeros_like(dsgb_ref)

        dnn = dnn_ref[...]
        ddnn = jnp.zeros((1, DN_DIM), F32)
        for h in range(DN_HEADS):
            sl = slice(h * DN_DIM, (h + 1) * DN_DIM)
            xhat, r = _rms_stats(o_ref[:, sl])
            z = z_ref[:, sl]
            sz = _sigmoid(z)
            doa = dm_ref[:, sl]
            dyn = doa * (z * sz)
            dz_ref[:, sl] = _bf(doa * xhat * dnn * _silu_grad(z, sz))
            do_ref[:, sl] = _rms_bwd(dyn, xhat, r, dnn)
            ddnn = ddnn + jnp.sum(dyn * xhat, axis=0, keepdims=True)
        ddnn_ref[...] += ddnn
        tri = lax.broadcasted_iota(jnp.int32, (SG_CHUNK, SG_CHUNK), 0) >= lax.broadcasted_iota(jnp.int32, (SG_CHUNK, SG_CHUNK), 1)
        lane = lax.broadcasted_iota(jnp.int32, (SG_CHUNK, 128), 1)
        dsgb = jnp.zeros((SG_CHUNK, 128), F32)
        dsgn_rows = []
        for g in range(SG_GROUPS):
            sl = slice(g * SG_DIM, (g + 1) * SG_DIM)
            sv = sv_ref[:, sl]
            su = su_ref[:, sl]
            xhat, r = _rms_stats(_gelu(sv))
            sgn = sgn_ref[g:g + 1, :]
            svn = _bf(xhat * sgn)
            sua = _gelu(su)
            wt = _bf(jnp.where(tri, sgw_ref[g], 0.0))
            bias = sgb_ref[:, g:g + 1]
            dw = jnp.zeros((SG_CHUNK, SG_CHUNK), F32)
            db = jnp.zeros((SG_CHUNK, 1), F32)
            dsua, dsvn = [], []
            for c in range(nc):
                rows = slice(c * SG_CHUNK, (c + 1) * SG_CHUNK)
                mixed = _dot(wt, svn[rows, :]) + bias
                dob = dm_ref[rows, hw + g * SG_DIM:hw + (g + 1) * SG_DIM]
                dsua.append(dob * mixed)
                dmixed = dob * sua[rows, :]
                dmb = _bf(dmixed)
                dsvn.append(_dot_tn(wt, dmb))
                dw = dw + _dot_nt(dmb, svn[rows, :])
                db = db + jnp.sum(dmixed, axis=1, keepdims=True)
            dsua = jnp.concatenate(dsua, axis=0) if nc > 1 else dsua[0]
            dsvn = jnp.concatenate(dsvn, axis=0) if nc > 1 else dsvn[0]
            dz_ref[:, hw + g * SG_DIM:hw + (g + 1) * SG_DIM] = _bf(dsua * _gelu_grad(su))
            dz_ref[:, 2 * hw + g * SG_DIM:2 * hw + (g + 1) * SG_DIM] = _bf(_rms_bwd(dsvn, xhat, r, sgn) * _gelu_grad(sv))
            dsgn_rows.append(jnp.sum(dsvn * xhat, axis=0, keepdims=True))
            dsgw_ref[g] += jnp.where(tri, dw, 0.0)
            dsgb = jnp.where(lane == g, db, dsgb)
        dsgn_ref[...] += jnp.concatenate(dsgn_rows, axis=0)
        dsgb_ref[...] += dsgb

    full = lambda shape: pl.BlockSpec(shape, lambda i: (0,) * len(shape))
    return pl.pallas_call(
        body, name=name, grid=(T // tm,),
        in_specs=[pl.BlockSpec((tm, 2 * hw), lambda i: (i, 0)),
                  pl.BlockSpec((tm, hw), lambda i: (i, 0)),
                  pl.BlockSpec((tm, hw), lambda i: (i, 3)),
                  pl.BlockSpec((tm, hw), lambda i: (i, 4)),
                  pl.BlockSpec((tm, hw), lambda i: (i, 5)),
                  full((1, DN_DIM)), full((SG_GROUPS, SG_DIM)), full((SG_GROUPS, SG_CHUNK, SG_CHUNK)),
                  full((SG_CHUNK, 128))],
        out_specs=[pl.BlockSpec((tm, hw), lambda i: (i, 0)),
                   pl.BlockSpec((tm, 3 * hw), lambda i: (i, 0)),
                   full((1, DN_DIM)), full((SG_GROUPS, SG_DIM)), full((SG_GROUPS, SG_CHUNK, SG_CHUNK)),
                   full((SG_CHUNK, 128))],
        out_shape=[jax.ShapeDtypeStruct((T, hw), F32), jax.ShapeDtypeStruct((T, 3 * hw), BF16),
                   jax.ShapeDtypeStruct((1, DN_DIM), F32), jax.ShapeDtypeStruct((SG_GROUPS, SG_DIM), F32),
                   jax.ShapeDtypeStruct((SG_GROUPS, SG_CHUNK, SG_CHUNK), F32),
                   jax.ShapeDtypeStruct((SG_CHUNK, 128), F32)],
        compiler_params=_params(("arbitrary",)),
    )(dmix, o, proj, proj, proj, dn_norm, sg_norm, sg_w, sg_bt)


def _window_sums(h, sign):
    sums, s, w = {}, h, 1
    while w < POOL_WINDOWS[-1]:
        s = s + _shift_rows(s, sign * w)
        w *= 2
        sums[w] = s
    return sums


def _pool_counts(t_global):
    return [jnp.minimum(t_global + 1, win).astype(F32) for win in POOL_WINDOWS]


def _pooled_groups(ext_h, row0, tm):
    sums = _window_sums(ext_h, 1)
    t_global = row0 + lax.broadcasted_iota(jnp.int32, (tm, 1), 0)
    counts = _pool_counts(t_global)
    out = []
    for gi, win in enumerate(POOL_WINDOWS):
        cols = slice(gi * POOL_DIM, (gi + 1) * POOL_DIM)
        out.append(sums[win][POOL_HALO:, cols] / counts[gi] - ext_h[POOL_HALO:, cols])
    return out


def _pool_fwd(x, nw, pool_w, pool_scale, layer, name):
    T, D = x.shape
    tm = _tile(T, 256)
    hb = tm // POOL_HALO

    def body(x_ref, xp_ref, n_ref, w_ref, s_ref, xo_ref):
        i = pl.program_id(0)
        prev = jnp.where(i == 0, 0.0, xp_ref[...])
        ext = jnp.concatenate([prev, x_ref[...]], axis=0)
        xhat, _ = _rms_stats(ext)
        pooled = _pooled_groups(xhat * n_ref[...], i * tm, tm)
        for gi in range(len(POOL_WINDOWS)):
            cols = slice(gi * POOL_DIM, (gi + 1) * POOL_DIM)
            xo_ref[:, cols] = x_ref[:, cols] + _dot(_bf(pooled[gi]), w_ref[gi]) * s_ref[:, cols]

    return pl.pallas_call(
        body, name=name, grid=(T // tm,),
        in_specs=[pl.BlockSpec((tm, D), lambda i: (i, 0)),
                  pl.BlockSpec((POOL_HALO, D), lambda i: (jnp.maximum(i * hb - 1, 0), 0)),
                  pl.BlockSpec((None, 1, D), lambda i: (layer, 0, 0)),
                  pl.BlockSpec(pool_w.shape, lambda i: (0, 0, 0)),
                  pl.BlockSpec((1, D), lambda i: (0, 0))],
        out_specs=pl.BlockSpec((tm, D), lambda i: (i, 0)),
        out_shape=jax.ShapeDtypeStruct((T, D), F32),
        compiler_params=_params(("parallel",)),
    )(x, x, nw, pool_w, pool_scale)


def _pool_bwd(dxo, x, nw, pool_w, pool_scale, layer, name):
    T, D = x.shape
    tm = _tile(T, 256)
    hb = tm // POOL_HALO
    nt = T // tm
    ng = len(POOL_WINDOWS)

    def body(dxo_ref, dxn_ref, x_ref, xp_ref, n_ref, w_ref, s_ref, dx_ref, dw_ref, ds_ref, dn_ref):
        i = pl.program_id(0)

        @pl.when(i == 0)
        def _():
            dw_ref[...] = jnp.zeros_like(dw_ref)
            ds_ref[...] = jnp.zeros_like(ds_ref)
            dn_ref[...] = jnp.zeros_like(dn_ref)

        prev = jnp.where(i == 0, 0.0, xp_ref[...])
        ext = jnp.concatenate([prev, x_ref[...]], axis=0)
        xhat_ext, r_ext = _rms_stats(ext)
        nv = n_ref[...]
        pooled = _pooled_groups(xhat_ext * nv, i * tm, tm)
        dxo = dxo_ref[...]
        scale = s_ref[...]
        dout_ext = jnp.concatenate([dxo, jnp.where(i == nt - 1, 0.0, dxn_ref[...])], axis=0) * scale
        t_ext = i * tm + lax.broadcasted_iota(jnp.int32, (tm + POOL_HALO, 1), 0)
        counts = _pool_counts(t_ext)
        dh_cols, ds_cols = [], []
        for gi, win in enumerate(POOL_WINDOWS):
            cols = slice(gi * POOL_DIM, (gi + 1) * POOL_DIM)
            wg = w_ref[gi]
            pb = _bf(pooled[gi])
            doutb = _bf(dout_ext[:, cols])
            dpooled = _dot_nt(doutb, wg)
            ahead = _window_sums(dpooled / counts[gi], -1)[win]
            dh_cols.append(ahead[:tm, :] - dpooled[:tm, :])
            dw_ref[gi] += _dot_tn(pb, doutb[:tm, :])
            ds_cols.append(jnp.sum(dxo[:, cols] * _dot(pb, wg), axis=0, keepdims=True))
        dh = jnp.concatenate(dh_cols, axis=1)
        xhat, r = xhat_ext[POOL_HALO:, :], r_ext[POOL_HALO:, :]
        dx_ref[...] = dxo + _rms_bwd(dh, xhat, r, nv)
        dn_ref[...] += jnp.sum(dh * xhat, axis=0, keepdims=True)
        ds_ref[...] += jnp.concatenate(ds_cols, axis=1)

    last_halo = T // POOL_HALO - 1
    return pl.pallas_call(
        body, name=name, grid=(nt,),
        in_specs=[pl.BlockSpec((tm, D), lambda i: (i, 0)),
                  pl.BlockSpec((POOL_HALO, D), lambda i: (jnp.minimum((i + 1) * hb, last_halo), 0)),
                  pl.BlockSpec((tm, D), lambda i: (i, 0)),
                  pl.BlockSpec((POOL_HALO, D), lambda i: (jnp.maximum(i * hb - 1, 0), 0)),
                  pl.BlockSpec((None, 1, D), lambda i: (layer, 0, 0)),
                  pl.BlockSpec(pool_w.shape, lambda i: (0, 0, 0)),
                  pl.BlockSpec((1, D), lambda i: (0, 0))],
        out_specs=[pl.BlockSpec((tm, D), lambda i: (i, 0)),
                   pl.BlockSpec((ng, POOL_DIM, POOL_DIM), lambda i: (0, 0, 0)),
                   pl.BlockSpec((1, D), lambda i: (0, 0)),
                   pl.BlockSpec((1, D), lambda i: (0, 0))],
        out_shape=[jax.ShapeDtypeStruct((T, D), F32), jax.ShapeDtypeStruct((ng, POOL_DIM, POOL_DIM), F32),
                   jax.ShapeDtypeStruct((1, D), F32), jax.ShapeDtypeStruct((1, D), F32)],
        compiler_params=_params(("arbitrary",)),
    )(dxo, dxo, x, x, nw, pool_w, pool_scale)


def _loss_head(x, target, fn, name):
    T, D = x.shape
    tm = _tile(T, 512)

    def body(x_ref, t_ref, n_ref, loss_ref, dx_ref, dn_ref):
        @pl.when(pl.program_id(0) == 0)
        def _():
            loss_ref[...] = jnp.zeros_like(loss_ref)
            dn_ref[...] = jnp.zeros_like(dn_ref)

        xhat, r = _rms_stats(x_ref[...])
        nv = n_ref[...]
        err = xhat * nv - t_ref[...]
        part = jnp.sum(jnp.sum(err * err, axis=1, keepdims=True), axis=0, keepdims=True)
        loss_ref[...] += 0.5 * part / D
        dy = err / D
        dx_ref[...] = _rms_bwd(dy, xhat, r, nv)
        dn_ref[...] += jnp.sum(dy * xhat, axis=0, keepdims=True)

    row = pl.BlockSpec((tm, D), lambda i: (i, 0))
    return pl.pallas_call(
        body, name=name, grid=(T // tm,),
        in_specs=[row, row, pl.BlockSpec((1, D), lambda i: (0, 0))],
        out_specs=[pl.BlockSpec((1, 1), lambda i: (0, 0)), row, pl.BlockSpec((1, D), lambda i: (0, 0))],
        out_shape=[jax.ShapeDtypeStruct((1, 1), F32), jax.ShapeDtypeStruct((T, D), F32),
                   jax.ShapeDtypeStruct((1, D), F32)],
        compiler_params=_params(("arbitrary",)),
    )(x, target, fn)


def _adamw(w, g, m, v, name):
    R, C = w.shape
    br = R
    for cand in (512, 256, 128, 64, 32, 16, 8):
        if R % cand == 0 and cand * C * 4 <= 2 * 1024 * 1024:
            br = cand
            break

    def body(w_ref, g_ref, m_ref, v_ref, d_ref, mo_ref, vo_ref):
        gv = g_ref[...]
        m_new = ADAM_B1 * m_ref[...] + (1.0 - ADAM_B1) * gv
        v_new = ADAM_B2 * v_ref[...] + (1.0 - ADAM_B2) * (gv * gv)
        m_hat = m_new / (1.0 - ADAM_B1 ** ADAM_STEP)
        v_hat = v_new / (1.0 - ADAM_B2 ** ADAM_STEP)
        d_ref[...] = -ADAM_LR * (m_hat / (jnp.sqrt(v_hat) + ADAM_EPS) + ADAM_WD * w_ref[...])
        mo_ref[...] = m_new
        vo_ref[...] = v_new

    blk = pl.BlockSpec((br, C), lambda i: (i, 0))
    return pl.pallas_call(
        body, name=name, grid=(R // br,), in_specs=[blk] * 4, out_specs=[blk] * 3,
        out_shape=[jax.ShapeDtypeStruct((R, C), F32)] * 3,
        compiler_params=_params(("parallel",)),
    )(w, g, m, v)


def _mesh_pos():
    return lax.axis_index("x"), lax.axis_index("y"), lax.axis_index("c")


def _other_chips(x, y):
    return [(1 - x, y), (x, 1 - y), (1 - x, 1 - y)]


def _half_of(ref, shape, h):
    size = shape[0] // 2
    return ref.at[pl.ds(h * size, size)]


class _ChipGather:
    def __init__(self, shards, split):
        self.shards, self.split = list(shards), list(split)
        self.operands = self.shards
        n = len(self.shards)
        self.out_shape = [jax.ShapeDtypeStruct((N_CHIPS,) + s.shape, s.dtype) for s in self.shards]
        self.scratch = [pltpu.SemaphoreType.DMA((n, 3))] * 4

    def _piece(self, a, ref, h):
        return _half_of(ref, self.shards[a].shape, h) if self.split[a] else ref

    def start(self, ins, outs, sems):
        send_sems, recv_sems = sems[0], sems[1]
        x, y, c = _mesh_pos()
        me = 2 * x + y
        for a in range(len(ins)):
            for k, (px, py) in enumerate(_other_chips(x, y)):
                pltpu.make_async_remote_copy(self._piece(a, ins[a], c), self._piece(a, outs[a].at[me], c),
                                             send_sems.at[a, k], recv_sems.at[a, k],
                                             device_id=(px, py, c), device_id_type=MESH).start()

    def finish(self, ins, outs, sems):
        send_sems, recv_sems, fwd_send_sems, fwd_recv_sems = sems
        x, y, c = _mesh_pos()
        sibling = (x, y, 1 - c)
        chips = _other_chips(x, y)
        n = len(ins)
        forwards = []
        for a in range(n):
            for k, (px, py) in enumerate(chips):
                landed = self._piece(a, outs[a].at[2 * px + py], c)
                pltpu.make_async_remote_copy(landed, landed, send_sems.at[a, k], recv_sems.at[a, k],
                                             device_id=(px, py, c), device_id_type=MESH).wait_recv()
                if self.split[a]:
                    fwd = pltpu.make_async_remote_copy(landed, landed, fwd_send_sems.at[a, k], fwd_recv_sems.at[a, k],
                                                       device_id=sibling, device_id_type=MESH)
                    fwd.start()
                    forwards.append(fwd)
        for a in range(n):
            if self.split[a]:
                for k, (px, py) in enumerate(chips):
                    other = self._piece(a, outs[a].at[2 * px + py], 1 - c)
                    pltpu.make_async_remote_copy(other, other, fwd_send_sems.at[a, k], fwd_recv_sems.at[a, k],
                                                 device_id=sibling, device_id_type=MESH).wait_recv()
        for a in range(n):
            for k, (px, py) in enumerate(chips):
                sent = self._piece(a, ins[a], c)
                pltpu.make_async_remote_copy(sent, sent, send_sems.at[a, k], recv_sems.at[a, k],
                                             device_id=(px, py, c), device_id_type=MESH).wait_send()
        for fwd in forwards:
            fwd.wait_send()

    def finalize(self, gathered):
        x, y, _ = _mesh_pos()
        return [lax.dynamic_update_index_in_dim(g, s, 2 * x + y, 0) for g, s in zip(gathered, self.shards)]

    def run(self, name):
        n = len(self.shards)

        def body(*refs):
            ins, outs, sems = refs[:n], refs[n:2 * n], refs[2 * n:]
            self.start(ins, outs, sems)
            self.finish(ins, outs, sems)

        gathered = pl.pallas_call(
            body, name=name, in_specs=[ANY] * n, out_specs=[ANY] * n, out_shape=self.out_shape,
            scratch_shapes=self.scratch, compiler_params=pltpu.CompilerParams(has_side_effects=True),
        )(*self.shards)
        return self.finalize(gathered)


def _ffn_weight_grads(hb, dg, du, a, dyb, tag):
    dwg = _matmul_tn(hb, dg, D_MODEL, FF_CHUNK, f"{tag}_dw_gate", stack_n=True)
    dwu = _matmul_tn(hb, du, D_MODEL, FF_CHUNK, f"{tag}_dw_up", stack_n=True)
    dwo = _matmul_tn(a, dyb, FF_CHUNK, D_MODEL, f"{tag}_dw_out")
    return jnp.concatenate([dwg, dwu], axis=0), dwo.reshape(N_CHIPS, D_FF // N_CHIPS, D_MODEL)


def _local_step(x, target, w, late=None, reduce=False):
    g = {}
    acts = []
    w = dict(w)

    def ffn_weights(which, layer):
        return w[f"n{which}"], w[f"win{which}_l{layer}"], w[f"wout{which}_l{layer}"]

    def hosting(name):
        exchange, layouts = late.get(name, (None, None)) if late else (None, None)
        return exchange, (lambda arrived: w.update(layouts(arrived)) if exchange is not None else None)

    def ffn(xin, which, layer):
        name = f"ffn{which}_l{layer}_fwd"
        exchange, keep = hosting(name)
        (xo, gv, uv, hb), arrived = _ffn_fwd(xin, *ffn_weights(which, layer), layer, name, exchange)
        keep(arrived)
        acts.append((xin, gv, uv, hb))
        return xo

    x1 = ffn(x, 1, 0)
    hb_mix = _rms_fwd_call(x1, w["nmix"], 0, "ab_norm_fwd")
    proj = _matmul(hb_mix, w["wp"], "ab_in_proj")
    qkv = _conv_fwd(proj, w["conv_w"], "dn_conv_fwd")
    exchange, keep = hosting("dn_prep")
    (dn_u, dn_w, dn_p, dn_qd, dn_kt, dn_gl, dn_inv), arrived = _dn_prep(qkv, proj, w["hp"], "dn_prep", exchange)
    keep(arrived)
    exchange, keep = hosting("dn_scan")
    (o, dn_vn, sall), arrived = _dn_scan(dn_u, dn_w, dn_p, dn_qd, dn_kt, dn_gl, "dn_scan", exchange)
    keep(arrived)
    mix = _mix_fwd(o, proj, w["dn_norm"], w["sg_norm"], w["sg_w"], w["sg_bt"], "ab_gate_fwd")
    x2 = _matmul(mix, w["wo"], "ab_out_proj", res=x1)
    x3 = ffn(x2, 2, 0)
    x4 = ffn(x3, 1, 1)
    x5 = _pool_fwd(x4, w["nmix"], w["pool_w"], w["pool_scale"], 1, "pool_fwd")
    x6 = ffn(x5, 2, 1)
    loss, dx, g["fn"] = _loss_head(x6, target, w["fn"], "loss_head")

    dn = {1: [None, None], 2: [None, None]}
    dwin = {1: [None, None], 2: [None, None]}
    dwout = {1: [None, None], 2: [None, None]}

    def ffn_back(dxo, which, layer, saved, exchange=None):
        nw, win, wout = ffn_weights(which, layer)
        xin, gv, uv, hb = saved
        tag = f"ffn{which}_l{layer}"
        (dxi, dg, du, a, dyb, dnw), arrived = _ffn_bwd(dxo, xin, nw, gv, uv, win, wout, layer, f"{tag}_bwd", exchange)
        dn[which][layer] = dnw
        dwin[which][layer], dwout[which][layer] = _ffn_weight_grads(hb, dg, du, a, dyb, tag)
        return dxi, arrived

    for which in (1, 2):
        g[f"win{which}"] = dwin[which]
        g[f"wout{which}"] = dwout[which]
    reduced = {}

    def open_round(tag, keys):
        have = _sharded_grads(g)
        return _GradRound(tag, {k: have[k] for k in keys})

    dx, _ = ffn_back(dx, 2, 1, acts[3])
    dx, g["pool_w"], g["pool_scale"], dnmix1 = _pool_bwd(dx, x4, w["nmix"], w["pool_w"], w["pool_scale"], 1, "pool_bwd")
    dx, _ = ffn_back(dx, 1, 1, acts[2])
    round_a = open_round("a", REDUCE_ROUNDS[0]) if reduce else None
    dx2, arrived = ffn_back(dx, 2, 0, acts[1], round_a.scatter if reduce else None)
    if reduce:
        reduced.update(round_a.finish(arrived))
    round_b = open_round("b", REDUCE_ROUNDS[1]) if reduce else None
    dmix = _matmul(dx2, w["wo"], "ab_out_proj_bwd", trans_b=True)
    g["wo"] = _matmul_tn(mix, dx2, D_MODEL, D_MODEL, "ab_out_proj_dw")
    do, dzuv, g["dn_norm"], g["sg_norm"], g["sg_w"], g["sg_bt"] = _mix_bwd(
        dmix, o, proj, w["dn_norm"], w["sg_norm"], w["sg_w"], w["sg_bt"], "ab_gate_bwd")
    dvn, dkt, dgl = _dn_scan_bwd(dn_w, dn_p, dn_qd, dn_kt, dn_gl, dn_vn, sall, do, "dn_scan_bwd")
    (dqkv_act, dba, g["hp"]), arrived = _dn_prep_bwd(qkv, proj, w["hp"], sall, dn_vn, do, dvn, dkt, dgl, dn_inv, dn_u, dn_w,
                                                     "dn_prep_bwd", round_b.scatter if reduce else None)
    if reduce:
        reduced.update(round_b.finish(arrived))
    dqkv, g["conv_w"] = _conv_bwd(dqkv_act, proj, w["conv_w"], "dn_conv_bwd")
    dproj = jnp.concatenate([dqkv, dzuv, dba.astype(BF16)], axis=1)
    dh = _matmul(dproj, w["wp"], "ab_in_proj_bwd", trans_b=True)
    g["wp"] = _matmul_tn(hb_mix, dproj, D_MODEL, 640, "ab_in_proj_dw")
    dx1, dnmix0 = _rms_bwd_call(dh, x1, w["nmix"], dx2, 0, "ab_norm_bwd")
    dx0, _ = ffn_back(dx1, 1, 0, acts[0])
    if reduce:
        round_c = open_round("c", REDUCE_ROUNDS[2])
        reduced.update(round_c.finish(round_c.scatter.run("grad_c_chip_scatter")))

    g["n1"] = jnp.concatenate(dn[1], axis=0)
    g["n2"] = jnp.concatenate(dn[2], axis=0)
    g["nmix"] = jnp.concatenate([dnmix0, dnmix1], axis=0)
    return loss, dx0, g, reduced


SHARDED = ("ffn1_w_in", "ffn1_w_out", "ffn2_w_in", "ffn2_w_out", "ab_w_in", "ab_w_out", "pool_w", "dn_conv_w", "pool_scale")
REPLICATED = ("ffn_norm1", "mix_norm", "ffn_norm2", "dn_a_log", "dn_dt_bias", "dn_out_norm", "sg_norm", "sg_w", "sg_b", "final_norm")
QKVZ = 4 * DN_HEADS * DN_DIM
N_GATES = 2 * DN_HEADS
IN_PROJ = QKVZ + N_GATES + 2 * SG_GROUPS * SG_DIM


def _shard_pieces(wts, keys):
    out = []
    for n, layer in keys:
        a = wts[n][0 if layer is None else layer]
        a = a[None] if a.ndim == 1 else a
        out.append(a.astype(BF16) if n in MATRICES else a)
    return out


def _replicated_layouts(rep):
    per_layer = lambda a: a.reshape(a.shape[0], 1, D_MODEL)
    w = {"n1": per_layer(rep["ffn_norm1"]), "nmix": per_layer(rep["mix_norm"]), "n2": per_layer(rep["ffn_norm2"])}
    hp = jnp.zeros((8, 128), F32)
    w["hp"] = hp.at[0, DN_HEADS:N_GATES].set(rep["dn_a_log"][0]).at[1, DN_HEADS:N_GATES].set(rep["dn_dt_bias"][0])
    w["dn_norm"] = rep["dn_out_norm"]
    w["sg_norm"] = rep["sg_norm"][0]
    w["sg_w"] = rep["sg_w"][0]
    w["sg_bt"] = jnp.zeros((SG_CHUNK, 128), F32).at[:, :SG_GROUPS].set(rep["sg_b"][0].T)
    w["fn"] = rep["final_norm"].reshape(1, D_MODEL)
    return w


def _layouts_from(gathered):
    w = {}
    for (n, layer), a in gathered.items():
        if n in ("ffn1_w_in", "ffn2_w_in"):
            w[f"win{n[3]}_l{layer}"] = a
        elif n in ("ffn1_w_out", "ffn2_w_out"):
            w[f"wout{n[3]}_l{layer}"] = a
        elif n == "ab_w_in":
            ab_in = jnp.transpose(a, (1, 0, 2)).reshape(D_MODEL, IN_PROJ)
            w["wp"] = jnp.concatenate([ab_in[:, :QKVZ], ab_in[:, QKVZ + N_GATES:], ab_in[:, QKVZ:QKVZ + N_GATES],
                                       jnp.zeros((D_MODEL, PROJ_W - IN_PROJ), ab_in.dtype)], axis=1)
        elif n == "dn_conv_w":
            w["conv_w"] = jnp.transpose(a, (1, 0, 2)).reshape(DN_CONV, 3 * DN_HEADS * DN_DIM)
        elif n == "ab_w_out":
            w["wo"] = a.reshape(D_MODEL, D_MODEL)
        elif n == "pool_w":
            w["pool_w"] = jnp.transpose(a, (1, 0, 2, 3)).reshape(len(POOL_WINDOWS), POOL_DIM, POOL_DIM)
        elif n == "pool_scale":
            w["pool_scale"] = a.reshape(1, D_MODEL)
    return w


def _sharded_grads(g):
    nw = len(POOL_WINDOWS)
    sharded = {}
    for n, key in (("ffn1_w_in", "win1"), ("ffn1_w_out", "wout1"), ("ffn2_w_in", "win2"), ("ffn2_w_out", "wout2")):
        for layer, a in enumerate(g.get(key, ())):
            if a is not None:
                sharded[(n, layer)] = a
    if "wp" in g:
        wp = g["wp"]
        ab_in = jnp.concatenate([wp[:, :QKVZ], wp[:, IN_PROJ - N_GATES:IN_PROJ], wp[:, QKVZ:IN_PROJ - N_GATES]], axis=1)
        sharded[("ab_w_in", None)] = jnp.transpose(ab_in.reshape(D_MODEL, N_CHIPS, IN_PROJ // N_CHIPS), (1, 0, 2))
    if "wo" in g:
        sharded[("ab_w_out", None)] = g["wo"].reshape(N_CHIPS, D_MODEL // N_CHIPS, D_MODEL)
    if "pool_w" in g:
        sharded[("pool_w", None)] = jnp.transpose(g["pool_w"].reshape(nw, N_CHIPS, POOL_DIM // N_CHIPS, POOL_DIM), (1, 0, 2, 3))
    if "conv_w" in g:
        sharded[("dn_conv_w", None)] = jnp.transpose(g["conv_w"].reshape(DN_CONV, N_CHIPS, -1), (1, 0, 2))
    if "pool_scale" in g:
        sharded[("pool_scale", None)] = g["pool_scale"].reshape(N_CHIPS, 1, D_MODEL // N_CHIPS)
    return sharded


def _replicated_grads(g):
    rep = {
        "ffn_norm1": g["n1"], "mix_norm": g["nmix"], "ffn_norm2": g["n2"],
        "dn_a_log": g["hp"][0:1, DN_HEADS:N_GATES], "dn_dt_bias": g["hp"][1:2, DN_HEADS:N_GATES],
        "dn_out_norm": g["dn_norm"], "sg_norm": g["sg_norm"][None], "sg_w": g["sg_w"][None],
        "sg_b": g["sg_bt"][:, :SG_GROUPS].T[None], "final_norm": g["fn"].reshape(D_MODEL),
    }
    return rep


def _as_halves(a):
    shape = a.shape[1:]
    if len(shape) >= 2 and shape[0] % 2 == 0:
        return a.reshape(N_CHIPS, 2, -1, shape[-1])
    return a.reshape(N_CHIPS, 2, 1, -1)


def _row_block(rows):
    for cand in (256, 176, 128, 64, 32, 16):
        if rows % cand == 0:
            return cand
    return rows


def _from_halves(mine, other, core, shape):
    both = jnp.stack([jnp.where(core == 0, mine, other), jnp.where(core == 0, other, mine)])
    return both.reshape(shape)


def _swap_with_sibling(packs, name):
    n = len(packs)

    def body(*refs):
        ins, outs, send_sems, recv_sems = refs[:n], refs[n:2 * n], refs[2 * n], refs[2 * n + 1]
        x, y, c = _mesh_pos()
        copies = [pltpu.make_async_remote_copy(ins[k].at[:, 1 - c], outs[k], send_sems.at[k], recv_sems.at[k],
                                               device_id=(x, y, 1 - c), device_id_type=MESH) for k in range(n)]
        for cp in copies:
            cp.start()
        for cp in copies:
            cp.wait()

    return pl.pallas_call(
        body, name=name, in_specs=[ANY] * n, out_specs=[ANY] * n,
        out_shape=[jax.ShapeDtypeStruct((p.shape[0],) + p.shape[2:], p.dtype) for p in packs],
        scratch_shapes=[pltpu.SemaphoreType.DMA((n,)), pltpu.SemaphoreType.DMA((n,))],
        compiler_params=pltpu.CompilerParams(has_side_effects=True),
    )(*packs)


def _add_pair(pack, recv, core, name):
    nchip, _, rows, lanes = pack.shape
    rb = _row_block(rows)

    def body(c_ref, a_ref, b_ref, o32_ref, o16_ref):
        s = a_ref[...] + b_ref[...]
        o32_ref[...] = s
        o16_ref[...] = _bf(s)

    blk = pl.BlockSpec((None, rb, lanes), lambda p, i, c: (p, i, 0))
    return pl.pallas_call(
        body, name=name,
        grid_spec=pltpu.PrefetchScalarGridSpec(
            num_scalar_prefetch=1, grid=(nchip, rows // rb),
            in_specs=[pl.BlockSpec((None, None, rb, lanes), lambda p, i, c: (p, c[0], i, 0)), blk],
            out_specs=[blk, blk]),
        out_shape=[jax.ShapeDtypeStruct((nchip, rows, lanes), F32), jax.ShapeDtypeStruct((nchip, rows, lanes), BF16)],
        compiler_params=_params(("parallel", "parallel")),
    )(core, pack, recv)


class _ChipScatter:
    def __init__(self, parts16):
        self.operands = list(parts16)
        n = len(self.operands)
        self.out_shape = [jax.ShapeDtypeStruct((N_CHIPS - 1,) + p.shape[1:], p.dtype) for p in self.operands]
        self.scratch = [pltpu.SemaphoreType.DMA((n, N_CHIPS - 1))] * 2

    def _copies(self, ins, outs, sems):
        x, y, c = _mesh_pos()
        return [pltpu.make_async_remote_copy(ins[a].at[2 * px + py], outs[a].at[k], sems[0].at[a, k], sems[1].at[a, k],
                                             device_id=(px, py, c), device_id_type=MESH)
                for a in range(len(ins)) for k, (px, py) in enumerate(_other_chips(x, y))]

    def start(self, ins, outs, sems):
        for cp in self._copies(ins, outs, sems):
            cp.start()

    def finish(self, ins, outs, sems):
        for cp in self._copies(ins, outs, sems):
            cp.wait()

    def finalize(self, results):
        return results

    def run(self, name):
        n = len(self.operands)

        def body(*refs):
            ins, outs, sems = refs[:n], refs[n:2 * n], refs[2 * n:]
            self.start(ins, outs, sems)
            self.finish(ins, outs, sems)

        return pl.pallas_call(
            body, name=name, in_specs=[ANY] * n, out_specs=[ANY] * n, out_shape=self.out_shape, scratch_shapes=self.scratch,
            compiler_params=pltpu.CompilerParams(has_side_effects=True),
        )(*self.operands)


def _sum_chips(part32, recv16, chip, name):
    nchip, rows, lanes = part32.shape
    rb = _row_block(rows)

    def body(p_ref, own_ref, r_ref, o_ref):
        s = own_ref[...]
        for k in range(nchip - 1):
            s = s + r_ref[k].astype(F32)
        o_ref[...] = s

    return pl.pallas_call(
        body, name=name,
        grid_spec=pltpu.PrefetchScalarGridSpec(
            num_scalar_prefetch=1, grid=(rows // rb,),
            in_specs=[pl.BlockSpec((None, rb, lanes), lambda i, p: (p[0], i, 0)),
                      pl.BlockSpec((nchip - 1, rb, lanes), lambda i, p: (0, i, 0))],
            out_specs=pl.BlockSpec((rb, lanes), lambda i, p: (i, 0))),
        out_shape=jax.ShapeDtypeStruct((rows, lanes), F32),
        compiler_params=_params(("parallel",)),
    )(chip, part32, recv16)


def _share_with_sibling(halves, name):
    n = len(halves)

    def body(*refs):
        ins, outs, send_sems, recv_sems = refs[:n], refs[n:2 * n], refs[2 * n], refs[2 * n + 1]
        x, y, c = _mesh_pos()
        copies = [pltpu.make_async_remote_copy(ins[k], outs[k], send_sems.at[k], recv_sems.at[k],
                                               device_id=(x, y, 1 - c), device_id_type=MESH) for k in range(n)]
        for cp in copies:
            cp.start()
        for cp in copies:
            cp.wait()

    return pl.pallas_call(
        body, name=name, in_specs=[ANY] * n, out_specs=[ANY] * n,
        out_shape=[jax.ShapeDtypeStruct(h.shape, h.dtype) for h in halves],
        scratch_shapes=[pltpu.SemaphoreType.DMA((n,)), pltpu.SemaphoreType.DMA((n,))],
        compiler_params=pltpu.CompilerParams(has_side_effects=True),
    )(*halves)


class _GradRound:
    def __init__(self, tag, pieces):
        self.tag, self.keys = tag, list(pieces)
        self.shapes = [pieces[k].shape[1:] for k in self.keys]
        _, _, c = _mesh_pos()
        core = jnp.reshape(c, (1,)).astype(jnp.int32)
        packs = [_as_halves(pieces[k]) for k in self.keys]
        recvs = _swap_with_sibling(packs, f"grad_{tag}_pair_swap")
        sums = [_add_pair(p, r, core, f"grad_{tag}_pair_add_{i}") for i, (p, r) in enumerate(zip(packs, recvs))]
        self.parts32 = [s[0] for s in sums]
        self.scatter = _ChipScatter([s[1] for s in sums])

    def finish(self, recvs16):
        x, y, c = _mesh_pos()
        chip = jnp.reshape(2 * x + y, (1,)).astype(jnp.int32)
        halves = [_sum_chips(p, r, chip, f"grad_{self.tag}_chip_sum_{i}") for i, (p, r) in enumerate(zip(self.parts32, recvs16))]
        others = _share_with_sibling(halves, f"grad_{self.tag}_pair_share")
        return {k: _from_halves(h, o, c, shape) for k, h, o, shape in zip(self.keys, halves, others, self.shapes)}


def _pack_small(vals):
    parts = []
    for n in REPLICATED:
        flat = vals[n].reshape(-1)
        rows = -(-flat.shape[0] // 128)
        rows = -(-rows // 8) * 8
        parts.append(jnp.pad(flat, (0, rows * 128 - flat.shape[0])).reshape(rows, 128))
    return jnp.concatenate(parts, axis=0)


def _unpack_small(pack, like):
    out, off = {}, 0
    for n in REPLICATED:
        size = like[n].size
        rows = -(-size // 128)
        rows = -(-rows // 8) * 8
        out[n] = pack[off:off + rows].reshape(-1)[:size].reshape(like[n].shape)
        off += rows
    return out


def _all_to_all_small(pack, name):
    rows, lanes = pack.shape
    flips = [(dx, dy, dc) for dx in (0, 1) for dy in (0, 1) for dc in (0, 1)][1:]

    def body(src_ref, out_ref, send_sems, recv_sems, local_sem):
        x, y, c = _mesh_pos()
        me = 4 * x + 2 * y + c
        loc = pltpu.make_async_copy(src_ref, out_ref.at[me], local_sem)
        loc.start()
        copies = []
        for k, (dx, dy, dc) in enumerate(flips):
            peer = (x ^ dx, y ^ dy, c ^ dc)
            cp = pltpu.make_async_remote_copy(src_ref, out_ref.at[me], send_sems.at[k], recv_sems.at[k],
                                              device_id=peer, device_id_type=MESH)
            cp.start()
            copies.append(cp)
        for k, (dx, dy, dc) in enumerate(flips):
            peer = (x ^ dx, y ^ dy, c ^ dc)
            pltpu.make_async_remote_copy(src_ref, out_ref.at[4 * peer[0] + 2 * peer[1] + peer[2]], send_sems.at[k],
                                         recv_sems.at[k], device_id=peer, device_id_type=MESH).wait_recv()
        for cp in copies:
            cp.wait_send()
        loc.wait()

    return pl.pallas_call(
        body, name=name, in_specs=[ANY], out_specs=ANY,
        out_shape=jax.ShapeDtypeStruct((8, rows, lanes), pack.dtype),
        scratch_shapes=[pltpu.SemaphoreType.DMA((7,)), pltpu.SemaphoreType.DMA((7,)), pltpu.SemaphoreType.DMA],
        compiler_params=pltpu.CompilerParams(has_side_effects=True),
    )(pack)


def _sum_devices(stack, name):
    ndev, rows, lanes = stack.shape

    def body(s_ref, o_ref):
        s = s_ref[0]
        for d in range(1, ndev):
            s = s + s_ref[d]
        o_ref[...] = s

    return pl.pallas_call(
        body, name=name, grid=(1,),
        in_specs=[pl.BlockSpec((ndev, rows, lanes), lambda i: (0, 0, 0))],
        out_specs=pl.BlockSpec((rows, lanes), lambda i: (0, 0)),
        out_shape=jax.ShapeDtypeStruct((rows, lanes), F32),
    )(stack)


WEIGHT_ORDER = ("ffn_norm1", "ffn1_w_in", "ffn1_w_out", "mix_norm", "ffn_norm2", "ffn2_w_in", "ffn2_w_out", "ab_w_in",
                "dn_conv_w", "dn_a_log", "dn_dt_bias", "dn_out_norm", "sg_norm", "sg_w", "sg_b", "ab_w_out", "pool_w",
                "pool_scale", "final_norm")
MATRICES = ("ffn1_w_in", "ffn1_w_out", "ffn2_w_in", "ffn2_w_out", "ab_w_in", "ab_w_out", "pool_w")
GATHER_FIRST = (("ffn1_w_in", 0), ("ffn1_w_out", 0), ("ab_w_in", None), ("dn_conv_w", None), ("ab_w_out", None))
GATHER_LATER = {"dn_prep": (("ffn2_w_in", 0), ("ffn2_w_out", 0)),
                "dn_scan": (("ffn1_w_in", 1), ("ffn1_w_out", 1)),
                "ffn2_l0_fwd": (("pool_w", None), ("pool_scale", None), ("ffn2_w_in", 1), ("ffn2_w_out", 1))}
REDUCE_ROUNDS = ((("ffn2_w_in", 1), ("ffn2_w_out", 1), ("ffn1_w_in", 1), ("ffn1_w_out", 1), ("pool_w", None), ("pool_scale", None)),
                 (("ffn2_w_in", 0), ("ffn2_w_out", 0)),
                 (("ffn1_w_in", 0), ("ffn1_w_out", 0), ("ab_w_out", None), ("ab_w_in", None), ("dn_conv_w", None)))


def _as_2d(a):
    return a.reshape(-1, a.shape[-1])


def kernel(x, ffn_norm1, ffn1_w_in, ffn1_w_out, mix_norm, ffn_norm2, ffn2_w_in, ffn2_w_out, ab_w_in, dn_conv_w, dn_a_log, dn_dt_bias, dn_out_norm, sg_norm, sg_w, sg_b, ab_w_out, pool_w, pool_scale, final_norm, loss_target, m_ffn_norm1, m_ffn1_w_in, m_ffn1_w_out, m_mix_norm, m_ffn_norm2, m_ffn2_w_in, m_ffn2_w_out, m_ab_w_in, m_dn_conv_w, m_dn_a_log, m_dn_dt_bias, m_dn_out_norm, m_sg_norm, m_sg_w, m_sg_b, m_ab_w_out, m_pool_w, m_pool_scale, m_final_norm, v_ffn_norm1, v_ffn1_w_in, v_ffn1_w_out, v_mix_norm, v_ffn_norm2, v_ffn2_w_in, v_ffn2_w_out, v_ab_w_in, v_dn_conv_w, v_dn_a_log, v_dn_dt_bias, v_dn_out_norm, v_sg_norm, v_sg_w, v_sg_b, v_ab_w_out, v_pool_w, v_pool_scale, v_final_norm):
    given = dict(locals())
    wts = {n: given[n] for n in WEIGHT_ORDER}
    mom_m = {n: given["m_" + n] for n in WEIGHT_ORDER}
    mom_v = {n: given["v_" + n] for n in WEIGHT_ORDER}

    rep = {n: wts[n] for n in REPLICATED}
    first = _ChipGather(_shard_pieces(wts, GATHER_FIRST), [n in MATRICES for n, _ in GATHER_FIRST])
    w = {**_replicated_layouts(rep), **_layouts_from(dict(zip(GATHER_FIRST, first.run("weight_gather_first"))))}
    late = {host: (_ChipGather(_shard_pieces(wts, keys), [n in MATRICES for n, _ in keys]),
                   functools.partial(lambda keys, arrived: _layouts_from(dict(zip(keys, arrived))), keys))
            for host, keys in GATHER_LATER.items()}

    loss, dx, g, reduced = _local_step(x[0], loss_target[0], w, reduce=True, late=late)
    g_rep = _replicated_grads(g)
    grads = {}
    for n in SHARDED:
        layers = [reduced[(n, layer)] for layer in range(wts[n].shape[0])] if (n, 0) in reduced else [reduced[(n, None)]]
        grads[n] = jnp.stack(layers).reshape(wts[n].shape)
    small = _sum_devices(_all_to_all_small(_pack_small(g_rep), "grad_small_exchange"), "grad_small_sum")
    grads.update(_unpack_small(small, rep))

    delta, new_m, new_v = {}, {}, {}
    for n in SHARDED:
        d, m1, v1 = _adamw(_as_2d(wts[n]), _as_2d(grads[n]), _as_2d(mom_m[n]), _as_2d(mom_v[n]), f"adamw_{n}")
        delta[n], new_m[n], new_v[n] = (t.reshape(wts[n].shape) for t in (d, m1, v1))
    d, m1, v1 = _adamw(_pack_small(rep), small, _pack_small({n: mom_m[n] for n in REPLICATED}),
                       _pack_small({n: mom_v[n] for n in REPLICATED}), "adamw_replicated")
    for tgt, packed in ((delta, d), (new_m, m1), (new_v, v1)):
        tgt.update(_unpack_small(packed, rep))

    total = lax.psum(loss[0, 0], ("x", "y", "c"))
    outs = [total, dx[None]]
    for group in (grads, delta, new_m, new_v):
        outs.extend(group[n] for n in WEIGHT_ORDER)
    return tuple(outs)
```

```python
import functools

import jax
import jax.numpy as jnp
from jax import lax
from jax.experimental import pallas as pl
from jax.experimental.pallas import tpu as pltpu

F32, BF16 = jnp.float32, jnp.bfloat16
NORM_EPS = 1e-6
D_MODEL = 1024
D_FF = 2816
N_CHIPS = 4
FF_CHUNK = 2 * D_FF // N_CHIPS
DN_HEADS, DN_DIM, DN_CHUNK, DN_CONV = 4, 128, 64, 4
DN_BLOCK = 2 * DN_CHUNK
DN_PREP_CHUNKS = 4
DN_SCAN_CHUNKS = 8
SG_GROUPS, SG_DIM, SG_CHUNK = 4, 128, 128
POOL_WINDOWS = (2, 4, 8, 16)
POOL_DIM = 256
POOL_HALO = 16
CONV_HALO = 8
PROJ_W = 3200
BA_BLOCK = 3072 // 128
ADAM_LR, ADAM_B1, ADAM_B2, ADAM_EPS, ADAM_WD, ADAM_STEP = 0.001, 0.9, 0.999, 1e-08, 0.01, 10
VMEM_BIG = 52 * 1024 * 1024
FFN_FWD_ROWS = 512
FFN_BWD_ROWS = 256
PACK_LANES = 1024
PACK_ROW_BLOCK = 256
MESH = pl.DeviceIdType.MESH
HI = lax.Precision.HIGHEST
ANY = pl.BlockSpec(memory_space=pl.ANY)


def _params(sem=None, vmem=None):
    return pltpu.CompilerParams(dimension_semantics=sem, vmem_limit_bytes=vmem)


def _dot(a, b):
    return jnp.dot(a, b, preferred_element_type=F32)


def _dot_nt(a, b):
    return lax.dot_general(a, b, (((1,), (1,)), ((), ())), preferred_element_type=F32)


def _dot_tn(a, b):
    return lax.dot_general(a, b, (((0,), (0,)), ((), ())), preferred_element_type=F32)


def _dot_hi(a, b):
    return jnp.dot(a, b, preferred_element_type=F32, precision=HI)


def _dot_mid(a, b):
    return jnp.dot(a, b, preferred_element_type=F32, precision=lax.Precision.HIGH)


def _bf(a):
    return a.astype(BF16)


def _rms_stats(x):
    r = lax.rsqrt(jnp.mean(x * x, axis=-1, keepdims=True) + NORM_EPS)
    return x * r, r


def _rms_bwd(dh, xhat, r, w):
    dhn = dh * w
    return r * (dhn - xhat * jnp.mean(dhn * xhat, axis=-1, keepdims=True))


def _sigmoid(x):
    return jax.nn.sigmoid(x)


def _silu_grad(x, s):
    return s * (1.0 + x * (1.0 - s))


def _gelu(x):
    return 0.5 * x * (1.0 + lax.erf(x * 0.7071067811865476))


def _gelu_grad(x):
    return 0.5 * (1.0 + lax.erf(x * 0.7071067811865476)) + x * jnp.exp(-0.5 * x * x) * 0.3989422804014327


def _softplus(x):
    return jnp.maximum(x, 0.0) + jnp.log(1.0 + jnp.exp(-jnp.abs(x)))


def _tile(n, pref):
    t = min(n, pref)
    assert n % t == 0, (n, t)
    return t


def _ffn_weight_specs():
    once = pl.Buffered(1)
    return [pl.BlockSpec((N_CHIPS, D_MODEL, FF_CHUNK), lambda i: (0, 0, 0), pipeline_mode=once),
            pl.BlockSpec((N_CHIPS, D_FF // N_CHIPS, D_MODEL), lambda i: (0, 0, 0), pipeline_mode=once)]


def _ffn_fwd(x, nw, win, wout, layer, name, exchange=None):
    T, D = x.shape
    tm = _tile(T, FFN_FWD_ROWS)
    nj = N_CHIPS // 2

    def body(x_ref, n_ref, win_ref, wo_ref, xo_ref, g_ref, u_ref, hb_ref):
        xv = x_ref[...]
        xhat, _ = _rms_stats(xv)
        h = _bf(xhat * n_ref[...])
        hb_ref[...] = h
        acc = None
        for j in range(nj):
            cols = slice(j * FF_CHUNK, (j + 1) * FF_CHUNK)
            g = _dot(h, win_ref[j])
            u = _dot(h, win_ref[nj + j])
            g_ref[:, cols] = _bf(g)
            u_ref[:, cols] = _bf(u)
            part = _dot(_bf(g * _sigmoid(g) * u), wo_ref[2 * j:2 * j + 2].reshape(FF_CHUNK, D))
            acc = part if acc is None else acc + part
        xo_ref[...] = xv + 0.5 * acc

    row = pl.BlockSpec((tm, D), lambda i: (i, 0))
    wide = pl.BlockSpec((tm, D_FF), lambda i: (i, 0))
    return _call_with_exchange(
        body, exchange, name=name, steps=T // tm, vmem=VMEM_BIG,
        in_specs=[row, pl.BlockSpec((None, 1, D), lambda i: (layer, 0, 0))] + _ffn_weight_specs(),
        out_specs=[row, wide, wide, row],
        out_shape=[jax.ShapeDtypeStruct((T, D), F32), jax.ShapeDtypeStruct((T, D_FF), BF16),
                   jax.ShapeDtypeStruct((T, D_FF), BF16), jax.ShapeDtypeStruct((T, D), BF16)],
        args=(x, nw, win, wout))


def _ffn_bwd(dxo, x, nw, g, u, win, wout, layer, name, exchange=None):
    T, D = x.shape
    tm = _tile(T, FFN_BWD_ROWS)
    nj = N_CHIPS // 2

    def body(dxo_ref, x_ref, n_ref, g_ref, u_ref, win_ref, wo_ref, dx_ref, dgu_ref, a_ref, dyb_ref, dn_ref):
        @pl.when(pl.program_id(0) == 0)
        def _():
            dn_ref[...] = jnp.zeros_like(dn_ref)

        dxo = dxo_ref[...]
        dyb = _bf(0.5 * dxo)
        dyb_ref[...] = dyb
        dh = None
        for j in range(nj):
            cols = slice(j * FF_CHUNK, (j + 1) * FF_CHUNK)
            da = _dot_nt(dyb, wo_ref[2 * j:2 * j + 2].reshape(FF_CHUNK, D))
            gv = g_ref[:, cols].astype(F32)
            uv = u_ref[:, cols].astype(F32)
            sg = _sigmoid(gv)
            sl = gv * sg
            dgb = _bf(da * uv * _silu_grad(gv, sg))
            dub = _bf(da * sl)
            a_ref[:, cols] = _bf(sl * uv)
            dgu_ref[:, cols] = dgb
            dgu_ref[:, D_FF + j * FF_CHUNK:D_FF + (j + 1) * FF_CHUNK] = dub
            part = _dot_nt(dgb, win_ref[j]) + _dot_nt(dub, win_ref[nj + j])
            dh = part if dh is None else dh + part
        xhat, r = _rms_stats(x_ref[...])
        dx_ref[...] = dxo + _rms_bwd(dh, xhat, r, n_ref[...])
        dn_ref[...] += jnp.sum(dh * xhat, axis=0, keepdims=True)

    row = pl.BlockSpec((tm, D), lambda i: (i, 0))
    wide = pl.BlockSpec((tm, D_FF), lambda i: (i, 0))
    return _call_with_exchange(
        body, exchange, name=name, steps=T // tm, vmem=VMEM_BIG,
        in_specs=[row, row, pl.BlockSpec((None, 1, D), lambda i: (layer, 0, 0)), wide, wide] + _ffn_weight_specs(),
        out_specs=[row, pl.BlockSpec((tm, 2 * D_FF), lambda i: (i, 0)), wide, row, pl.BlockSpec((1, D), lambda i: (0, 0))],
        out_shape=[jax.ShapeDtypeStruct((T, D), F32), jax.ShapeDtypeStruct((T, 2 * D_FF), BF16),
                   jax.ShapeDtypeStruct((T, D_FF), BF16), jax.ShapeDtypeStruct((T, D), BF16),
                   jax.ShapeDtypeStruct((1, D), F32)],
        args=(dxo, x, nw, g, u, win, wout))


def _matmul_tn(a, b, bm, bn, name, stack_n=False):
    T, M = a.shape
    N = b.shape[1]
    tk = _tile(T, 1024)
    bm, bn = _tile(M, bm), _tile(N, bn)

    def body(a_ref, b_ref, o_ref):
        @pl.when(pl.program_id(2) == 0)
        def _():
            o_ref[...] = jnp.zeros_like(o_ref)

        o_ref[...] += _dot_tn(_bf(a_ref[...]), _bf(b_ref[...]))

    if stack_n:
        out_spec = pl.BlockSpec((None, bm, bn), lambda m, n, k: (n, m, 0))
        out_shape = jax.ShapeDtypeStruct((N // bn, M, bn), F32)
    else:
        out_spec = pl.BlockSpec((bm, bn), lambda m, n, k: (m, n))
        out_shape = jax.ShapeDtypeStruct((M, N), F32)
    return pl.pallas_call(
        body, name=name, grid=(M // bm, N // bn, T // tk),
        in_specs=[pl.BlockSpec((tk, bm), lambda m, n, k: (k, m)),
                  pl.BlockSpec((tk, bn), lambda m, n, k: (k, n))],
        out_specs=out_spec, out_shape=out_shape,
        compiler_params=_params(("parallel", "parallel", "arbitrary"), VMEM_BIG),
    )(a, b)


def _matmul(a, b, name, trans_b=False, res=None, out_dtype=F32):
    T, K = a.shape
    N = b.shape[0] if trans_b else b.shape[1]
    tm = _tile(T, 512)

    def body(*refs):
        a_ref, b_ref = refs[0], refs[1]
        o_ref = refs[-1]
        av, bv = _bf(a_ref[...]), _bf(b_ref[...])
        acc = _dot_nt(av, bv) if trans_b else _dot(av, bv)
        if res is not None:
            acc = acc + refs[2][...]
        o_ref[...] = acc.astype(out_dtype)

    in_specs = [pl.BlockSpec((tm, K), lambda i: (i, 0)), pl.BlockSpec(b.shape, lambda i: (0, 0))]
    args = [a, b]
    if res is not None:
        in_specs.append(pl.BlockSpec((tm, N), lambda i: (i, 0)))
        args.append(res)
    return pl.pallas_call(
        body, name=name, grid=(T // tm,), in_specs=in_specs,
        out_specs=pl.BlockSpec((tm, N), lambda i: (i, 0)),
        out_shape=jax.ShapeDtypeStruct((T, N), out_dtype),
        compiler_params=_params(("parallel",), VMEM_BIG),
    )(*args)


def _rms_fwd_call(x, nw, layer, name):
    T, D = x.shape
    tm = _tile(T, 512)

    def body(x_ref, n_ref, o_ref):
        xhat, _ = _rms_stats(x_ref[...])
        o_ref[...] = _bf(xhat * n_ref[...])

    return pl.pallas_call(
        body, name=name, grid=(T // tm,),
        in_specs=[pl.BlockSpec((tm, D), lambda i: (i, 0)), pl.BlockSpec((None, 1, D), lambda i: (layer, 0, 0))],
        out_specs=pl.BlockSpec((tm, D), lambda i: (i, 0)),
        out_shape=jax.ShapeDtypeStruct((T, D), BF16),
        compiler_params=_params(("parallel",)),
    )(x, nw)


def _rms_bwd_call(dh, x, nw, dres, layer, name):
    T, D = x.shape
    tm = _tile(T, 512)

    def body(dh_ref, x_ref, n_ref, dr_ref, dx_ref, dn_ref):
        @pl.when(pl.program_id(0) == 0)
        def _():
            dn_ref[...] = jnp.zeros_like(dn_ref)

        xhat, r = _rms_stats(x_ref[...])
        dh_v = dh_ref[...]
        dx_ref[...] = dr_ref[...] + _rms_bwd(dh_v, xhat, r, n_ref[...])
        dn_ref[...] += jnp.sum(dh_v * xhat, axis=0, keepdims=True)

    row = pl.BlockSpec((tm, D), lambda i: (i, 0))
    return pl.pallas_call(
        body, name=name, grid=(T // tm,),
        in_specs=[row, row, pl.BlockSpec((None, 1, D), lambda i: (layer, 0, 0)), row],
        out_specs=[row, pl.BlockSpec((1, D), lambda i: (0, 0))],
        out_shape=[jax.ShapeDtypeStruct((T, D), F32), jax.ShapeDtypeStruct((1, D), F32)],
        compiler_params=_params(("arbitrary",)),
    )(dh, x, nw, dres)


def _shift_rows(x, s):
    n = x.shape[0]
    s = s % n
    return x if s == 0 else pltpu.roll(x, s, 0)


def _conv_fwd(proj, conv_w, name):
    T = proj.shape[0]
    C = 3 * DN_HEADS * DN_DIM
    cb = 512
    tm = _tile(T, 512)
    hb = tm // CONV_HALO

    def body(x_ref, xp_ref, w_ref, o_ref):
        i = pl.program_id(1)
        prev = jnp.where(i == 0, 0.0, xp_ref[...])
        ext = jnp.concatenate([prev, x_ref[...]], axis=0)
        w = w_ref[...]
        y = ext * w[DN_CONV - 1:DN_CONV, :]
        for k in range(DN_CONV - 1):
            y = y + _shift_rows(ext, DN_CONV - 1 - k) * w[k:k + 1, :]
        y = y[CONV_HALO:, :]
        o_ref[...] = y * _sigmoid(y)

    return pl.pallas_call(
        body, name=name, grid=(C // cb, T // tm),
        in_specs=[pl.BlockSpec((tm, cb), lambda c, i: (i, c)),
                  pl.BlockSpec((CONV_HALO, cb), lambda c, i: (jnp.maximum(i * hb - 1, 0), c)),
                  pl.BlockSpec((DN_CONV, cb), lambda c, i: (0, c))],
        out_specs=pl.BlockSpec((tm, cb), lambda c, i: (i, c)),
        out_shape=jax.ShapeDtypeStruct((T, C), F32),
        compiler_params=_params(("parallel", "parallel")),
    )(proj, proj, conv_w)


def _conv_bwd(dy, proj, conv_w, name):
    T = proj.shape[0]
    C = 3 * DN_HEADS * DN_DIM
    cb = 512
    tm = _tile(T, 512)
    hb = tm // CONV_HALO
    nt = T // tm

    def body(x_ref, xp_ref, xn_ref, dy_ref, dyn_ref, w_ref, dx_ref, dw_ref):
        i = pl.program_id(1)

        @pl.when(i == 0)
        def _():
            dw_ref[...] = jnp.zeros_like(dw_ref)

        prev = jnp.where(i == 0, 0.0, xp_ref[...])
        ext = jnp.concatenate([prev, x_ref[...], xn_ref[...]], axis=0)
        dy_ext = jnp.concatenate([jnp.zeros((CONV_HALO, cb), F32), dy_ref[...],
                                  jnp.where(i == nt - 1, 0.0, dyn_ref[...])], axis=0)
        w = w_ref[...]
        shifted = [_shift_rows(ext, DN_CONV - 1 - k) for k in range(DN_CONV)]
        y = shifted[0] * w[0:1, :]
        for k in range(1, DN_CONV):
            y = y + shifted[k] * w[k:k + 1, :]
        s = _sigmoid(y)
        dpre = dy_ext * _silu_grad(y, s)
        dx = dpre * w[DN_CONV - 1:DN_CONV, :]
        for k in range(DN_CONV - 1):
            dx = dx + _shift_rows(dpre, -(DN_CONV - 1 - k)) * w[k:k + 1, :]
        dx_ref[...] = _bf(dx[CONV_HALO:CONV_HALO + tm, :])
        rows = [jnp.sum((dpre * shifted[k])[CONV_HALO:CONV_HALO + tm, :], axis=0, keepdims=True) for k in range(DN_CONV)]
        dw_ref[...] += jnp.concatenate(rows, axis=0)

    last_halo = T // CONV_HALO - 1
    return pl.pallas_call(
        body, name=name, grid=(C // cb, nt),
        in_specs=[pl.BlockSpec((tm, cb), lambda c, i: (i, c)),
                  pl.BlockSpec((CONV_HALO, cb), lambda c, i: (jnp.maximum(i * hb - 1, 0), c)),
                  pl.BlockSpec((CONV_HALO, cb), lambda c, i: (jnp.minimum((i + 1) * hb, last_halo), c)),
                  pl.BlockSpec((tm, cb), lambda c, i: (i, c)),
                  pl.BlockSpec((CONV_HALO, cb), lambda c, i: (jnp.minimum((i + 1) * hb, last_halo), c)),
                  pl.BlockSpec((DN_CONV, cb), lambda c, i: (0, c))],
        out_specs=[pl.BlockSpec((tm, cb), lambda c, i: (i, c)),
                   pl.BlockSpec((DN_CONV, cb), lambda c, i: (0, c))],
        out_shape=[jax.ShapeDtypeStruct((T, C), BF16), jax.ShapeDtypeStruct((DN_CONV, C), F32)],
        compiler_params=_params(("parallel", "arbitrary")),
    )(proj, proj, proj, dy, dy, conv_w)


def _unit_lower_inverse(low, eye):
    return _unit_lower_inverses([low], eye)[0]


def _unit_lower_inverses(lows, eye):
    def each(fn, *lists):
        return [fn(*args) for args in zip(*lists)]

    p1 = [-low for low in lows]
    p2 = each(_dot_mid, p1, p1)
    p4 = each(_dot_mid, p2, p2)
    a = each(lambda x, y: eye + x + y + _dot_mid(x, y), p1, p2)
    p8 = each(_dot_mid, p4, p4)
    p16 = each(_dot_mid, p8, p8)
    b = each(lambda x, y: eye + x + y + _dot_mid(x, y), p4, p8)
    p32 = each(_dot_mid, p16, p16)
    ab = each(_dot_mid, a, b)
    c = each(lambda x, y: eye + x + y + _dot_mid(x, y), p16, p32)
    return each(_dot_mid, ab, c)


def _interleave(chains):
    results = [None] * len(chains)
    live = list(range(len(chains)))
    while live:
        for i in list(live):
            try:
                next(chains[i])
            except StopIteration as stop:
                results[i] = stop.value
                live.remove(i)
        yield
    return results


def _run_interleaved(chains):
    rounds = _interleave(chains)
    while True:
        try:
            next(rounds)
        except StopIteration as stop:
            return stop.value


def _l2_unit(x):
    r = lax.rsqrt(jnp.sum(x * x, axis=-1, keepdims=True) + NORM_EPS)
    return x * r, r


class _BlockMasks:
    def __init__(self):
        n = DN_BLOCK
        row = lax.broadcasted_iota(jnp.int32, (n, n), 0)
        col = lax.broadcasted_iota(jnp.int32, (n, n), 1)
        same = (row // DN_CHUNK) == (col // DN_CHUNK)
        self.lower, self.strict_lower = same & (row >= col), same & (row > col)
        self.upper, self.strict_upper = same & (row <= col), same & (row < col)
        self.eye = (row == col).astype(F32)
        self.first = lax.broadcasted_iota(jnp.int32, (n, 1), 0) < DN_CHUNK


def _dn_gates(ba, hp):
    coef = -jnp.exp(hp[0:1, :])
    pre = ba + hp[1:2, :]
    return _sigmoid(ba), coef * _softplus(pre), coef, pre


def _dn_block_gates(mk, ba, hp):
    assert DN_BLOCK == 2 * DN_CHUNK
    beta_t, graw_t, coef, pre = _dn_gates(ba, hp)
    gcum_t = _dot_hi(mk.lower.astype(F32), graw_t)
    gl_t = jnp.where(mk.first, gcum_t[DN_CHUNK - 1:DN_CHUNK, :], gcum_t[DN_BLOCK - 1:DN_BLOCK, :])
    return beta_t, gcum_t, gl_t, graw_t, coef, pre


def _dn_local(mk, qraw, kraw, bc, gc, gl):
    f = {}
    f["qn"], f["rq"] = _l2_unit(qraw)
    qh = f["qn"] * (DN_DIM ** -0.5)
    kh, f["rk"] = _l2_unit(kraw)
    gr = jnp.broadcast_to(gc, (DN_BLOCK, DN_BLOCK)).T
    dec = jnp.where(mk.lower, jnp.exp(jnp.where(mk.lower, gc - gr, 0.0)), 0.0)
    kb = kh * bc
    mkk = _dot_nt(_bf(kb), _bf(kh))
    eg = jnp.exp(gc)
    mqk = _dot_nt(_bf(qh), _bf(kh))
    etl = jnp.exp(gl - gc)
    f.update(qh=qh, kh=kh, gr=gr, dec=dec, kb=kb, mkk=mkk, eg=eg, mqk=mqk, attn=mqk * dec, etl=etl, qd=qh * eg, kt=kh * etl)
    return f


def _dn_specs(rows, rev=None):
    at = (lambda n: n) if rev is None else rev
    hw = DN_HEADS * DN_DIM
    return dict(
        qkv=[pl.BlockSpec((rows, hw), lambda n, j=j: (at(n), j)) for j in range(3)],
        ba=pl.BlockSpec((rows, 128), lambda n: (at(n), BA_BLOCK)),
        hp=pl.BlockSpec((8, 128), lambda n: (0, 0)),
        tok=pl.BlockSpec((rows, hw), lambda n: (at(n), 0)),
        attn=pl.BlockSpec((rows, DN_HEADS * DN_CHUNK), lambda n: (at(n), 0)),
        gate=pl.BlockSpec((rows // DN_CHUNK, 8, 128), lambda n: (at(n), 0, 0)),
        state=pl.BlockSpec((rows // DN_CHUNK, DN_HEADS, DN_DIM, DN_DIM), lambda n: (at(n), 0, 0, 0)),
    )


def _call_with_exchange(body, exchange, *, name, steps, in_specs, out_specs, out_shape, args, vmem=None, scratch_shapes=()):
    if exchange is None:
        res = pl.pallas_call(body, name=name, grid=(steps,), in_specs=in_specs, out_specs=out_specs, out_shape=out_shape,
                             scratch_shapes=list(scratch_shapes), compiler_params=_params(("arbitrary",), vmem))(*args)
        return list(res), None
    n_in, n_out, m, n_scr = len(in_specs), len(out_specs), len(exchange.operands), len(scratch_shapes)

    def hosted(*refs):
        ins, ex_ins = refs[:n_in], refs[n_in:n_in + m]
        outs, ex_outs = refs[n_in + m:n_in + m + n_out], refs[n_in + m + n_out:n_in + 2 * m + n_out]
        scratch, sems = refs[n_in + 2 * m + n_out:n_in + 2 * m + n_out + n_scr], refs[n_in + 2 * m + n_out + n_scr:]

        @pl.when(pl.program_id(0) == 0)
        def _():
            exchange.start(ex_ins, ex_outs, sems)

        body(*ins, *outs, *scratch)

        @pl.when(pl.program_id(0) == steps - 1)
        def _():
            exchange.finish(ex_ins, ex_outs, sems)

    res = pl.pallas_call(
        hosted, name=name, grid=(steps,), in_specs=list(in_specs) + [ANY] * m, out_specs=list(out_specs) + [ANY] * m,
        out_shape=list(out_shape) + list(exchange.out_shape), scratch_shapes=list(scratch_shapes) + list(exchange.scratch),
        compiler_params=pltpu.CompilerParams(dimension_semantics=("arbitrary",), vmem_limit_bytes=vmem, has_side_effects=True),
    )(*args, *exchange.operands)
    return list(res[:n_out]), exchange.finalize(list(res[n_out:]))


def _dn_prep(qkv, proj, hp, name, exchange=None):
    T = qkv.shape[0]
    n_chunks = T // DN_CHUNK
    blocks = max(1, min(DN_PREP_CHUNKS, n_chunks) * DN_CHUNK // DN_BLOCK)
    group = blocks * DN_BLOCK // DN_CHUNK
    rows = blocks * DN_BLOCK
    hw = DN_HEADS * DN_DIM

    def body(q_ref, k_ref, v_ref, ba_ref, hp_ref, u_ref, w_ref, p_ref, qd_ref, kt_ref, gl_ref, inv_ref):
        mk = _BlockMasks()
        hp_v = hp_ref[...]
        chains = []
        for j in range(blocks):
            rs = slice(j * DN_BLOCK, (j + 1) * DN_BLOCK)
            beta_t, gcum_t, gl_t = _dn_block_gates(mk, ba_ref[rs, :], hp_v)[:3]
            for c in range(DN_BLOCK // DN_CHUNK):
                gl_ref[j * (DN_BLOCK // DN_CHUNK) + c] = jnp.broadcast_to(gl_t[c * DN_CHUNK:c * DN_CHUNK + 1, :], (8, 128))
            for h in range(DN_HEADS):
                sl = slice(h * DN_DIM, (h + 1) * DN_DIM)
                gate = slice(DN_HEADS + h, DN_HEADS + h + 1)
                bc = beta_t[:, h:h + 1]
                f = _dn_local(mk, q_ref[rs, sl], k_ref[rs, sl], bc, gcum_t[:, gate], gl_t[:, gate])
                for c in range(DN_BLOCK // DN_CHUNK):
                    cr = slice(c * DN_CHUNK, (c + 1) * DN_CHUNK)
                    p_ref[j * DN_BLOCK + c * DN_CHUNK:j * DN_BLOCK + (c + 1) * DN_CHUNK, h * DN_CHUNK:(h + 1) * DN_CHUNK] = _bf(f["attn"][cr, cr])
                qd_ref[rs, sl] = _bf(f["qd"])
                kt_ref[rs, sl] = _bf(f["kt"])
                chains.append((rs, sl, bc, f))
        invs = _unit_lower_inverses([jnp.where(mk.strict_lower, f["mkk"] * f["dec"], 0.0) for _, _, _, f in chains], mk.eye)
        for (rs, sl, bc, f), inv in zip(chains, invs):
            inv_ref[rs, sl] = inv
            sol = _dot_mid(inv, jnp.concatenate([v_ref[rs, sl] * bc, f["kb"] * f["eg"]], axis=1))
            u_ref[rs, sl] = sol[:, :DN_DIM]
            w_ref[rs, sl] = _bf(sol[:, DN_DIM:])

    sp = _dn_specs(rows)
    tok16 = jax.ShapeDtypeStruct((T, hw), BF16)
    return _call_with_exchange(
        body, exchange, name=name, steps=n_chunks // group,
        in_specs=sp["qkv"] + [sp["ba"], sp["hp"]],
        out_specs=[sp["tok"], sp["tok"], sp["attn"], sp["tok"], sp["tok"], sp["gate"], sp["tok"]],
        out_shape=[jax.ShapeDtypeStruct((T, hw), F32), tok16, jax.ShapeDtypeStruct((T, DN_HEADS * DN_CHUNK), BF16),
                   tok16, tok16, jax.ShapeDtypeStruct((n_chunks, 8, 128), F32), jax.ShapeDtypeStruct((T, hw), F32)],
        args=(qkv, qkv, qkv, proj, hp))


def _dn_scan(u, w, p, qd, kt, gl, name, exchange=None):
    T = u.shape[0]
    n_chunks = T // DN_CHUNK
    group = min(DN_SCAN_CHUNKS, n_chunks)
    rows = group * DN_CHUNK
    hw = DN_HEADS * DN_DIM

    def body(u_ref, w_ref, p_ref, qd_ref, kt_ref, gl_ref, o_ref, vn_ref, sall_ref, s_s):
        @pl.when(pl.program_id(0) == 0)
        def _():
            s_s[...] = jnp.zeros_like(s_s)

        state = [s_s[h] for h in range(DN_HEADS)]
        for j in range(group):
            rs = slice(j * DN_CHUNK, (j + 1) * DN_CHUNK)
            for h in range(DN_HEADS):
                sl = slice(h * DN_DIM, (h + 1) * DN_DIM)
                sall_ref[j, h] = state[h]
                sb = _bf(state[h])
                vnb = _bf(u_ref[rs, sl] - _dot(w_ref[rs, sl], sb))
                vn_ref[rs, sl] = vnb
                o_ref[rs, sl] = _dot(qd_ref[rs, sl], sb) + _dot(p_ref[rs, h * DN_CHUNK:(h + 1) * DN_CHUNK], vnb)
                egl = jnp.exp(gl_ref[j, 0:1, DN_HEADS + h:DN_HEADS + h + 1])
                state[h] = state[h] * egl + _dot_tn(kt_ref[rs, sl], vnb)
        for h in range(DN_HEADS):
            s_s[h] = state[h]

    sp = _dn_specs(rows)
    return _call_with_exchange(
        body, exchange, name=name, steps=n_chunks // group,
        in_specs=[sp["tok"], sp["tok"], sp["attn"], sp["tok"], sp["tok"], sp["gate"]],
        out_specs=[sp["tok"], sp["tok"], sp["state"]],
        out_shape=[jax.ShapeDtypeStruct((T, hw), F32), jax.ShapeDtypeStruct((T, hw), BF16),
                   jax.ShapeDtypeStruct((n_chunks, DN_HEADS, DN_DIM, DN_DIM), F32)],
        scratch_shapes=[pltpu.VMEM((DN_HEADS, DN_DIM, DN_DIM), F32)],
        args=(u, w, p, qd, kt, gl))


def _dn_scan_bwd(w, p, qd, kt, gl, vn, sall, do, name):
    T = w.shape[0]
    n_chunks = T // DN_CHUNK
    group = min(DN_SCAN_CHUNKS, n_chunks)
    rows = group * DN_CHUNK
    hw = DN_HEADS * DN_DIM
    last = n_chunks // group - 1

    def body(w_ref, p_ref, qd_ref, kt_ref, gl_ref, vn_ref, sall_ref, do_ref, dvn_ref, dkt_ref, dgl_ref, ds_s):
        @pl.when(pl.program_id(0) == 0)
        def _():
            ds_s[...] = jnp.zeros_like(ds_s)

        lane = lax.broadcasted_iota(jnp.int32, (8, 128), 1)
        d_state = [ds_s[h] for h in range(DN_HEADS)]
        for j in reversed(range(group)):
            rs = slice(j * DN_CHUNK, (j + 1) * DN_CHUNK)
            dgl_tile = jnp.zeros((8, 128), F32)
            for h in range(DN_HEADS):
                sl = slice(h * DN_DIM, (h + 1) * DN_DIM)
                d_out = _bf(do_ref[rs, sl])
                d_new = d_state[h]
                d_newb = _bf(d_new)
                d_vn = _dot_tn(p_ref[rs, h * DN_CHUNK:(h + 1) * DN_CHUNK], d_out) + _dot(kt_ref[rs, sl], d_newb)
                dvn_ref[rs, sl] = d_vn
                dkt_ref[rs, sl] = _dot_nt(vn_ref[rs, sl], d_newb)
                egl = jnp.exp(gl_ref[j, 0:1, DN_HEADS + h:DN_HEADS + h + 1])
                prod = jnp.sum(d_new * sall_ref[j, h], axis=1, keepdims=True)
                dgl_tile = jnp.where(lane == DN_HEADS + h, jnp.sum(prod, axis=0, keepdims=True) * egl, dgl_tile)
                d_state[h] = d_new * egl + _dot_tn(qd_ref[rs, sl], d_out) - _dot_tn(w_ref[rs, sl], _bf(d_vn))
            dgl_ref[j] = dgl_tile
        for h in range(DN_HEADS):
            ds_s[h] = d_state[h]

    sp = _dn_specs(rows, rev=lambda n: last - n)
    return pl.pallas_call(
        body, name=name, grid=(n_chunks // group,),
        in_specs=[sp["tok"], sp["attn"], sp["tok"], sp["tok"], sp["gate"], sp["tok"], sp["state"], sp["tok"]],
        out_specs=[sp["tok"], sp["tok"], sp["gate"]],
        out_shape=[jax.ShapeDtypeStruct((T, hw), F32), jax.ShapeDtypeStruct((T, hw), F32),
                   jax.ShapeDtypeStruct((n_chunks, 8, 128), F32)],
        scratch_shapes=[pltpu.VMEM((DN_HEADS, DN_DIM, DN_DIM), F32)],
        compiler_params=_params(("arbitrary",)),
    )(w, p, qd, kt, gl, vn, sall, do)


def _dn_prep_bwd(qkv, proj, hp, sall, vn, do, dvn, dkt, dgl, inv, u, w, name, exchange=None):
    T = qkv.shape[0]
    n_chunks = T // DN_CHUNK
    blocks = max(1, min(DN_PREP_CHUNKS, n_chunks) * DN_CHUNK // DN_BLOCK)
    per_block = DN_BLOCK // DN_CHUNK
    group = blocks * per_block
    rows = blocks * DN_BLOCK
    hw = DN_HEADS * DN_DIM
    first_rows, second_rows = slice(0, DN_CHUNK), slice(DN_CHUNK, DN_BLOCK)

    def rowsum(x):
        return jnp.sum(x, axis=1, keepdims=True)

    def by_chunk(x, s0, s1, fn):
        return jnp.concatenate([fn(x[first_rows], s0), fn(x[second_rows], s1)], axis=0)

    def body(q_ref, k_ref, v_ref, ba_ref, hp_ref, sall_ref, vn_ref, do_ref, dvn_ref, dkt_ref, dgl_ref,
             inv_ref, u_ref, w_ref, dqkv_ref, dba_ref, dhp_ref):
        @pl.when(pl.program_id(0) == 0)
        def _():
            dhp_ref[...] = jnp.zeros_like(dhp_ref)

        mk = _BlockMasks()
        hp_v = hp_ref[...]
        chains = []
        for j in range(blocks):
            rs = slice(j * DN_BLOCK, (j + 1) * DN_BLOCK)
            chains.append(one_block(mk, hp_v, *(r.at[rs, :] for r in (q_ref, k_ref, v_ref, ba_ref)),
                                    sall_ref.at[pl.ds(j * per_block, per_block)],
                                    *(r.at[rs, :] for r in (vn_ref, do_ref, dvn_ref, dkt_ref)),
                                    dgl_ref.at[pl.ds(j * per_block, per_block)],
                                    *(r.at[rs, :] for r in (inv_ref, u_ref, w_ref)),
                                    *(dqkv_ref.at[rs, pl.ds(i * hw, hw)] for i in range(3)), dba_ref.at[rs, :]))
        total = jnp.zeros((8, 128), F32)
        for part in _run_interleaved(chains):
            total = total + part
        dhp_ref[...] += total

    def one_block(mk, hp_v, q_ref, k_ref, v_ref, ba_ref, state_ref, vn_ref, do_ref, dvn_ref, dkt_ref, dgl_ref,
                  inv_ref, u_ref, w_ref, dq_ref, dk_ref, dv_ref, dba_ref):
        ba = ba_ref[...]
        beta_t, gcum_t, gl_t, graw_t, coef, pre = _dn_block_gates(mk, ba, hp_v)
        lane = lax.broadcasted_iota(jnp.int32, (DN_BLOCK, 128), 1)
        rowi = lax.broadcasted_iota(jnp.int32, (DN_BLOCK, 1), 0)

        def head(h):
            sl = slice(h * DN_DIM, (h + 1) * DN_DIM)
            gate = slice(DN_HEADS + h, DN_HEADS + h + 1)
            gc = gcum_t[:, gate]
            bc = beta_t[:, h:h + 1]
            sb0, sb1 = _bf(state_ref[0, h]), _bf(state_ref[1, h])
            vh = v_ref[:, sl]
            f = _dn_local(mk, q_ref[:, sl], k_ref[:, sl], bc, gc, gl_t[:, gate])
            yield
            qh, kh, kb, dec, eg, etl = f["qh"], f["kh"], f["kb"], f["dec"], f["eg"], f["etl"]
            qd, kt = f["qd"], f["kt"]
            qb, kbf, kbb = _bf(qh), _bf(kh), _bf(kb)
            dec_t = jnp.where(mk.upper, jnp.exp(jnp.where(mk.upper, f["gr"] - gc, 0.0)), 0.0)
            mkk_t = f["mkk"].T
            inv_t = inv_ref[:, sl].T
            mqk_t = f["mqk"].T

            d_out = _bf(do_ref[:, sl])
            vnb = vn_ref[:, sl]
            d_qd = by_chunk(d_out, sb0, sb1, _dot_nt)
            d_attn = _dot_nt(d_out, vnb)
            d_attn_t = _dot_nt(vnb, d_out)
            d_vn = dvn_ref[:, sl]
            d_kt = dkt_ref[:, sl]
            d_w = -by_chunk(_bf(d_vn), sb0, sb1, _dot_nt)
            yield
            d_rhs = _dot_mid(inv_t, jnp.concatenate([d_vn, d_w], axis=1))
            yield
            d_bu, d_bw = d_rhs[:, :DN_DIM], d_rhs[:, DN_DIM:]
            ub, wb, d_bub, d_bwb = _bf(u_ref[:, sl]), w_ref[:, sl], _bf(d_bu), _bf(d_bw)
            d_low = -(_dot_nt(d_bub, ub) + _dot_nt(d_bwb, wb))
            d_low_t = -(_dot_nt(ub, d_bub) + _dot_nt(wb, d_bwb))
            yield
            d_mkk = jnp.where(mk.strict_lower, d_low * dec, 0.0)
            d_mkk_t = jnp.where(mk.strict_upper, d_low_t * dec_t, 0.0)
            d_mqk = jnp.where(mk.lower, d_attn * dec, 0.0)
            d_mqk_t = jnp.where(mk.upper, d_attn_t * dec_t, 0.0)
            bw = kb * eg
            d_kb = _dot(_bf(d_mkk), kbf) + d_bw * eg
            d_k = _dot(_bf(d_mkk_t), kbb) + _dot(_bf(d_mqk_t), qb) + d_kt * etl + d_kb * bc
            d_q = _dot(_bf(d_mqk), kbf) + d_qd * eg
            yield
            d_beta = rowsum(d_kb * kh) + rowsum(d_bu * vh)
            dv_ref[:, sl] = d_bu * bc
            e_mat = d_mkk * f["mkk"] + d_mqk * f["mqk"]
            e_mat_t = d_mkk_t * mkk_t + d_mqk_t * mqk_t
            kt_term = rowsum(d_kt * kt)
            d_g = rowsum(e_mat) - rowsum(e_mat_t) + rowsum(d_qd * qd) + rowsum(d_bw * bw) - kt_term
            for c, chunk_rows in enumerate((mk.first, ~mk.first)):
                d_glast = dgl_ref[c, 0:1, gate] + jnp.sum(jnp.where(chunk_rows, kt_term, 0.0), axis=0, keepdims=True)
                d_g = d_g + jnp.where(rowi == (c + 1) * DN_CHUNK - 1, d_glast, 0.0)
            qn = f["qn"]
            d_qs = d_q * (DN_DIM ** -0.5)
            dq_ref[:, sl] = f["rq"] * (d_qs - qn * rowsum(d_qs * qn))
            dk_ref[:, sl] = f["rk"] * (d_k - kh * rowsum(d_k * kh))
            return d_g, d_beta

        per_head = yield from _interleave([head(h) for h in range(DN_HEADS)])
        dgcum_t = jnp.zeros((DN_BLOCK, 128), F32)
        dbeta_t = jnp.zeros((DN_BLOCK, 128), F32)
        for h, (d_g, d_beta) in enumerate(per_head):
            dgcum_t = jnp.where(lane == DN_HEADS + h, d_g, dgcum_t)
            dbeta_t = jnp.where(lane == h, d_beta, dbeta_t)
        dgraw_t = _dot_hi(mk.upper.astype(F32), dgcum_t)
        sp = _sigmoid(pre)
        d_pre = dgraw_t * coef * sp
        dba_ref[...] = jnp.where(lane < DN_HEADS, dbeta_t * beta_t * (1.0 - beta_t),
                                 jnp.where(lane < 2 * DN_HEADS, d_pre, 0.0))
        in_g = (lane >= DN_HEADS) & (lane < 2 * DN_HEADS)
        d_alog = jnp.sum(jnp.where(in_g, dgraw_t * graw_t, 0.0), axis=0, keepdims=True)
        d_dtb = jnp.sum(jnp.where(in_g, d_pre, 0.0), axis=0, keepdims=True)
        return jnp.concatenate([d_alog, d_dtb, jnp.zeros((6, 128), F32)], axis=0)

    sp = _dn_specs(rows)
    return _call_with_exchange(
        body, exchange, name=name, steps=n_chunks // group,
        in_specs=sp["qkv"] + [sp["ba"], sp["hp"], sp["state"]] + [sp["tok"]] * 4 + [sp["gate"]] + [sp["tok"]] * 3,
        out_specs=[pl.BlockSpec((rows, 3 * hw), lambda n: (n, 0)), pl.BlockSpec((rows, 128), lambda n: (n, 0)), sp["hp"]],
        out_shape=[jax.ShapeDtypeStruct((T, 3 * hw), F32), jax.ShapeDtypeStruct((T, 128), F32),
                   jax.ShapeDtypeStruct((8, 128), F32)],
        args=(qkv, qkv, qkv, proj, hp, sall, vn, do, dvn, dkt, dgl, inv, u, w))


def _mix_fwd(o, proj, dn_norm, sg_norm, sg_w, sg_bt, name):
    T = o.shape[0]
    tm = _tile(T, 512)
    hw = DN_HEADS * DN_DIM
    nc = tm // SG_CHUNK

    def body(o_ref, z_ref, su_ref, sv_ref, dnn_ref, sgn_ref, sgw_ref, sgb_ref, mix_ref):
        dnn = dnn_ref[...]
        for h in range(DN_HEADS):
            sl = slice(h * DN_DIM, (h + 1) * DN_DIM)
            xhat, _ = _rms_stats(o_ref[:, sl])
            z = z_ref[:, sl]
            mix_ref[:, sl] = _bf(xhat * dnn * (z * _sigmoid(z)))
        tri = lax.broadcasted_iota(jnp.int32, (SG_CHUNK, SG_CHUNK), 0) >= lax.broadcasted_iota(jnp.int32, (SG_CHUNK, SG_CHUNK), 1)
        for g in range(SG_GROUPS):
            sl = slice(g * SG_DIM, (g + 1) * SG_DIM)
            xhat, _ = _rms_stats(_gelu(sv_ref[:, sl]))
            svn = _bf(xhat * sgn_ref[g:g + 1, :])
            sua = _gelu(su_ref[:, sl])
            wt = _bf(jnp.where(tri, sgw_ref[g], 0.0))
            bias = sgb_ref[:, g:g + 1]
            for c in range(nc):
                rows = slice(c * SG_CHUNK, (c + 1) * SG_CHUNK)
                mixed = _dot(wt, svn[rows, :]) + bias
                mix_ref[rows, hw + g * SG_DIM:hw + (g + 1) * SG_DIM] = _bf(sua[rows, :] * mixed)

    full = lambda shape: pl.BlockSpec(shape, lambda i: (0,) * len(shape))
    return pl.pallas_call(
        body, name=name, grid=(T // tm,),
        in_specs=[pl.BlockSpec((tm, hw), lambda i: (i, 0)),
                  pl.BlockSpec((tm, hw), lambda i: (i, 3)),
                  pl.BlockSpec((tm, hw), lambda i: (i, 4)),
                  pl.BlockSpec((tm, hw), lambda i: (i, 5)),
                  full((1, DN_DIM)), full((SG_GROUPS, SG_DIM)), full((SG_GROUPS, SG_CHUNK, SG_CHUNK)),
                  full((SG_CHUNK, 128))],
        out_specs=pl.BlockSpec((tm, 2 * hw), lambda i: (i, 0)),
        out_shape=jax.ShapeDtypeStruct((T, 2 * hw), BF16),
        compiler_params=_params(("parallel",)),
    )(o, proj, proj, proj, dn_norm, sg_norm, sg_w, sg_bt)


def _mix_bwd(dmix, o, proj, dn_norm, sg_norm, sg_w, sg_bt, name):
    T = o.shape[0]
    tm = _tile(T, 512)
    hw = DN_HEADS * DN_DIM
    nc = tm // SG_CHUNK

    def body(dm_ref, o_ref, z_ref, su_ref, sv_ref, dnn_ref, sgn_ref, sgw_ref, sgb_ref,
             do_ref, dz_ref, ddnn_ref, dsgn_ref, dsgw_ref, dsgb_ref):
        @pl.when(pl.program_id(0) == 0)
        def _():
            ddnn_ref[...] = jnp.zeros_like(ddnn_ref)
            dsgn_ref[...] = jnp.zeros_like(dsgn_ref)
            dsgw_ref[...] = jnp.zeros_like(dsgw_ref)
            dsgb_ref[...] = jnp.zeros_like(dsgb_ref)

        dnn = dnn_ref[...]
        ddnn = jnp.zeros((1, DN_DIM), F32)
        for h in range(DN_HEADS):
            sl = slice(h * DN_DIM, (h + 1) * DN_DIM)
            xhat, r = _rms_stats(o_ref[:, sl])
            z = z_ref[:, sl]
            sz = _sigmoid(z)
            doa = dm_ref[:, sl]
            dyn = doa * (z * sz)
            dz_ref[:, sl] = _bf(doa * xhat * dnn * _silu_grad(z, sz))
            do_ref[:, sl] = _rms_bwd(dyn, xhat, r, dnn)
            ddnn = ddnn + jnp.sum(dyn * xhat, axis=0, keepdims=True)
        ddnn_ref[...] += ddnn
        tri = lax.broadcasted_iota(jnp.int32, (SG_CHUNK, SG_CHUNK), 0) >= lax.broadcasted_iota(jnp.int32, (SG_CHUNK, SG_CHUNK), 1)
        lane = lax.broadcasted_iota(jnp.int32, (SG_CHUNK, 128), 1)
        dsgb = jnp.zeros((SG_CHUNK, 128), F32)
        dsgn_rows = []
        for g in range(SG_GROUPS):
            sl = slice(g * SG_DIM, (g + 1) * SG_DIM)
            sv = sv_ref[:, sl]
            su = su_ref[:, sl]
            xhat, r = _rms_stats(_gelu(sv))
            sgn = sgn_ref[g:g + 1, :]
            svn = _bf(xhat * sgn)
            sua = _gelu(su)
            wt = _bf(jnp.where(tri, sgw_ref[g], 0.0))
            bias = sgb_ref[:, g:g + 1]
            dw = jnp.zeros((SG_CHUNK, SG_CHUNK), F32)
            db = jnp.zeros((SG_CHUNK, 1), F32)
            dsua, dsvn = [], []
            for c in range(nc):
                rows = slice(c * SG_CHUNK, (c + 1) * SG_CHUNK)
                mixed = _dot(wt, svn[rows, :]) + bias
                dob = dm_ref[rows, hw + g * SG_DIM:hw + (g + 1) * SG_DIM]
                dsua.append(dob * mixed)
                dmixed = dob * sua[rows, :]
                dmb = _bf(dmixed)
                dsvn.append(_dot_tn(wt, dmb))
                dw = dw + _dot_nt(dmb, svn[rows, :])
                db = db + jnp.sum(dmixed, axis=1, keepdims=True)
            dsua = jnp.concatenate(dsua, axis=0) if nc > 1 else dsua[0]
            dsvn = jnp.concatenate(dsvn, axis=0) if nc > 1 else dsvn[0]
            dz_ref[:, hw + g * SG_DIM:hw + (g + 1) * SG_DIM] = _bf(dsua * _gelu_grad(su))
            dz_ref[:, 2 * hw + g * SG_DIM:2 * hw + (g + 1) * SG_DIM] = _bf(_rms_bwd(dsvn, xhat, r, sgn) * _gelu_grad(sv))
            dsgn_rows.append(jnp.sum(dsvn * xhat, axis=0, keepdims=True))
            dsgw_ref[g] += jnp.where(tri, dw, 0.0)
            dsgb = jnp.where(lane == g, db, dsgb)
        dsgn_ref[...] += jnp.concatenate(dsgn_rows, axis=0)
        dsgb_ref[...] += dsgb

    full = lambda shape: pl.BlockSpec(shape, lambda i: (0,) * len(shape))
    return pl.pallas_call(
        body, name=name, grid=(T // tm,),
        in_specs=[pl.BlockSpec((tm, 2 * hw), lambda i: (i, 0)),
                  pl.BlockSpec((tm, hw), lambda i: (i, 0)),
                  pl.BlockSpec((tm, hw), lambda i: (i, 3)),
                  pl.BlockSpec((tm, hw), lambda i: (i, 4)),
                  pl.BlockSpec((tm, hw), lambda i: (i, 5)),
                  full((1, DN_DIM)), full((SG_GROUPS, SG_DIM)), full((SG_GROUPS, SG_CHUNK, SG_CHUNK)),
                  full((SG_CHUNK, 128))],
        out_specs=[pl.BlockSpec((tm, hw), lambda i: (i, 0)),
                   pl.BlockSpec((tm, 3 * hw), lambda i: (i, 0)),
                   full((1, DN_DIM)), full((SG_GROUPS, SG_DIM)), full((SG_GROUPS, SG_CHUNK, SG_CHUNK)),
                   full((SG_CHUNK, 128))],
        out_shape=[jax.ShapeDtypeStruct((T, hw), F32), jax.ShapeDtypeStruct((T, 3 * hw), BF16),
                   jax.ShapeDtypeStruct((1, DN_DIM), F32), jax.ShapeDtypeStruct((SG_GROUPS, SG_DIM), F32),
                   jax.ShapeDtypeStruct((SG_GROUPS, SG_CHUNK, SG_CHUNK), F32),
                   jax.ShapeDtypeStruct((SG_CHUNK, 128), F32)],
        compiler_params=_params(("arbitrary",)),
    )(dmix, o, proj, proj, proj, dn_norm, sg_norm, sg_w, sg_bt)


def _window_sums(h, sign):
    sums, s, w = {}, h, 1
    while w < POOL_WINDOWS[-1]:
        s = s + _shift_rows(s, sign * w)
        w *= 2
        sums[w] = s
    return sums


def _pool_counts(t_global):
    return [jnp.minimum(t_global + 1, win).astype(F32) for win in POOL_WINDOWS]


def _pooled_groups(ext_h, row0, tm):
    sums = _window_sums(ext_h, 1)
    t_global = row0 + lax.broadcasted_iota(jnp.int32, (tm, 1), 0)
    counts = _pool_counts(t_global)
    out = []
    for gi, win in enumerate(POOL_WINDOWS):
        cols = slice(gi * POOL_DIM, (gi + 1) * POOL_DIM)
        out.append(sums[win][POOL_HALO:, cols] / counts[gi] - ext_h[POOL_HALO:, cols])
    return out


def _pool_fwd(x, nw, pool_w, pool_scale, layer, name):
    T, D = x.shape
    tm = _tile(T, 256)
    hb = tm // POOL_HALO

    def body(x_ref, xp_ref, n_ref, w_ref, s_ref, xo_ref):
        i = pl.program_id(0)
        prev = jnp.where(i == 0, 0.0, xp_ref[...])
        ext = jnp.concatenate([prev, x_ref[...]], axis=0)
        xhat, _ = _rms_stats(ext)
        pooled = _pooled_groups(xhat * n_ref[...], i * tm, tm)
        for gi in range(len(POOL_WINDOWS)):
            cols = slice(gi * POOL_DIM, (gi + 1) * POOL_DIM)
            xo_ref[:, cols] = x_ref[:, cols] + _dot(_bf(pooled[gi]), w_ref[gi]) * s_ref[:, cols]

    return pl.pallas_call(
        body, name=name, grid=(T // tm,),
        in_specs=[pl.BlockSpec((tm, D), lambda i: (i, 0)),
                  pl.BlockSpec((POOL_HALO, D), lambda i: (jnp.maximum(i * hb - 1, 0), 0)),
                  pl.BlockSpec((None, 1, D), lambda i: (layer, 0, 0)),
                  pl.BlockSpec(pool_w.shape, lambda i: (0, 0, 0)),
                  pl.BlockSpec((1, D), lambda i: (0, 0))],
        out_specs=pl.BlockSpec((tm, D), lambda i: (i, 0)),
        out_shape=jax.ShapeDtypeStruct((T, D), F32),
        compiler_params=_params(("parallel",)),
    )(x, x, nw, pool_w, pool_scale)


def _pool_bwd(dxo, x, nw, pool_w, pool_scale, layer, name):
    T, D = x.shape
    tm = _tile(T, 256)
    hb = tm // POOL_HALO
    nt = T // tm
    ng = len(POOL_WINDOWS)

    def body(dxo_ref, dxn_ref, x_ref, xp_ref, n_ref, w_ref, s_ref, dx_ref, dw_ref, ds_ref, dn_ref):
        i = pl.program_id(0)

        @pl.when(i == 0)
        def _():
            dw_ref[...] = jnp.zeros_like(dw_ref)
            ds_ref[...] = jnp.zeros_like(ds_ref)
            dn_ref[...] = jnp.zeros_like(dn_ref)

        prev = jnp.where(i == 0, 0.0, xp_ref[...])
        ext = jnp.concatenate([prev, x_ref[...]], axis=0)
        xhat_ext, r_ext = _rms_stats(ext)
        nv = n_ref[...]
        pooled = _pooled_groups(xhat_ext * nv, i * tm, tm)
        dxo = dxo_ref[...]
        scale = s_ref[...]
        dout_ext = jnp.concatenate([dxo, jnp.where(i == nt - 1, 0.0, dxn_ref[...])], axis=0) * scale
        t_ext = i * tm + lax.broadcasted_iota(jnp.int32, (tm + POOL_HALO, 1), 0)
        counts = _pool_counts(t_ext)
        dh_cols, ds_cols = [], []
        for gi, win in enumerate(POOL_WINDOWS):
            cols = slice(gi * POOL_DIM, (gi + 1) * POOL_DIM)
            wg = w_ref[gi]
            pb = _bf(pooled[gi])
            doutb = _bf(dout_ext[:, cols])
            dpooled = _dot_nt(doutb, wg)
            ahead = _window_sums(dpooled / counts[gi], -1)[win]
            dh_cols.append(ahead[:tm, :] - dpooled[:tm, :])
            dw_ref[gi] += _dot_tn(pb, doutb[:tm, :])
            ds_cols.append(jnp.sum(dxo[:, cols] * _dot(pb, wg), axis=0, keepdims=True))
        dh = jnp.concatenate(dh_cols, axis=1)
        xhat, r = xhat_ext[POOL_HALO:, :], r_ext[POOL_HALO:, :]
        dx_ref[...] = dxo + _rms_bwd(dh, xhat, r, nv)
        dn_ref[...] += jnp.sum(dh * xhat, axis=0, keepdims=True)
        ds_ref[...] += jnp.concatenate(ds_cols, axis=1)

    last_halo = T // POOL_HALO - 1
    return pl.pallas_call(
        body, name=name, grid=(nt,),
        in_specs=[pl.BlockSpec((tm, D), lambda i: (i, 0)),
                  pl.BlockSpec((POOL_HALO, D), lambda i: (jnp.minimum((i + 1) * hb, last_halo), 0)),
                  pl.BlockSpec((tm, D), lambda i: (i, 0)),
                  pl.BlockSpec((POOL_HALO, D), lambda i: (jnp.maximum(i * hb - 1, 0), 0)),
                  pl.BlockSpec((None, 1, D), lambda i: (layer, 0, 0)),
                  pl.BlockSpec(pool_w.shape, lambda i: (0, 0, 0)),
                  pl.BlockSpec((1, D), lambda i: (0, 0))],
        out_specs=[pl.BlockSpec((tm, D), lambda i: (i, 0)),
                   pl.BlockSpec((ng, POOL_DIM, POOL_DIM), lambda i: (0, 0, 0)),
                   pl.BlockSpec((1, D), lambda i: (0, 0)),
                   pl.BlockSpec((1, D), lambda i: (0, 0))],
        out_shape=[jax.ShapeDtypeStruct((T, D), F32), jax.ShapeDtypeStruct((ng, POOL_DIM, POOL_DIM), F32),
                   jax.ShapeDtypeStruct((1, D), F32), jax.ShapeDtypeStruct((1, D), F32)],
        compiler_params=_params(("arbitrary",)),
    )(dxo, dxo, x, x, nw, pool_w, pool_scale)


def _loss_head(x, target, fn, name):
    T, D = x.shape
    tm = _tile(T, 512)

    def body(x_ref, t_ref, n_ref, loss_ref, dx_ref, dn_ref):
        @pl.when(pl.program_id(0) == 0)
        def _():
            loss_ref[...] = jnp.zeros_like(loss_ref)
            dn_ref[...] = jnp.zeros_like(dn_ref)

        xhat, r = _rms_stats(x_ref[...])
        nv = n_ref[...]
        err = xhat * nv - t_ref[...]
        part = jnp.sum(jnp.sum(err * err, axis=1, keepdims=True), axis=0, keepdims=True)
        loss_ref[...] += 0.5 * part / D
        dy = err / D
        dx_ref[...] = _rms_bwd(dy, xhat, r, nv)
        dn_ref[...] += jnp.sum(dy * xhat, axis=0, keepdims=True)

    row = pl.BlockSpec((tm, D), lambda i: (i, 0))
    return pl.pallas_call(
        body, name=name, grid=(T // tm,),
        in_specs=[row, row, pl.BlockSpec((1, D), lambda i: (0, 0))],
        out_specs=[pl.BlockSpec((1, 1), lambda i: (0, 0)), row, pl.BlockSpec((1, D), lambda i: (0, 0))],
        out_shape=[jax.ShapeDtypeStruct((1, 1), F32), jax.ShapeDtypeStruct((T, D), F32),
                   jax.ShapeDtypeStruct((1, D), F32)],
        compiler_params=_params(("arbitrary",)),
    )(x, target, fn)


def _adamw(w, g, m, v, name):
    R, C = w.shape
    br = R
    for cand in (512, 256, 128, 64, 32, 16, 8):
        if R % cand == 0 and cand * C * 4 <= 2 * 1024 * 1024:
            br = cand
            break

    def body(w_ref, g_ref, m_ref, v_ref, d_ref, mo_ref, vo_ref):
        gv = g_ref[...]
        m_new = ADAM_B1 * m_ref[...] + (1.0 - ADAM_B1) * gv
        v_new = ADAM_B2 * v_ref[...] + (1.0 - ADAM_B2) * (gv * gv)
        m_hat = m_new / (1.0 - ADAM_B1 ** ADAM_STEP)
        v_hat = v_new / (1.0 - ADAM_B2 ** ADAM_STEP)
        d_ref[...] = -ADAM_LR * (m_hat / (jnp.sqrt(v_hat) + ADAM_EPS) + ADAM_WD * w_ref[...])
        mo_ref[...] = m_new
        vo_ref[...] = v_new

    blk = pl.BlockSpec((br, C), lambda i: (i, 0))
    return pl.pallas_call(
        body, name=name, grid=(R // br,), in_specs=[blk] * 4, out_specs=[blk] * 3,
        out_shape=[jax.ShapeDtypeStruct((R, C), F32)] * 3,
        compiler_params=_params(("parallel",)),
    )(w, g, m, v)


def _mesh_pos():
    return lax.axis_index("x"), lax.axis_index("y"), lax.axis_index("c")


def _other_chips(x, y):
    return [(1 - x, y), (x, 1 - y), (1 - x, 1 - y)]


def _half_of(ref, shape, h):
    size = shape[0] // 2
    return ref.at[pl.ds(h * size, size)]


class _ChipGather:
    def __init__(self, shards, split):
        self.shards, self.split = list(shards), list(split)
        self.operands = self.shards
        n = len(self.shards)
        self.out_shape = [jax.ShapeDtypeStruct((N_CHIPS,) + s.shape, s.dtype) for s in self.shards]
        self.scratch = [pltpu.SemaphoreType.DMA((n, 3))] * 4

    def _piece(self, a, ref, h):
        return _half_of(ref, self.shards[a].shape, h) if self.split[a] else ref

    def start(self, ins, outs, sems):
        send_sems, recv_sems = sems[0], sems[1]
        x, y, c = _mesh_pos()
        me = 2 * x + y
        for a in range(len(ins)):
            for k, (px, py) in enumerate(_other_chips(x, y)):
                pltpu.make_async_remote_copy(self._piece(a, ins[a], c), self._piece(a, outs[a].at[me], c),
                                             send_sems.at[a, k], recv_sems.at[a, k],
                                             device_id=(px, py, c), device_id_type=MESH).start()

    def finish(self, ins, outs, sems):
        send_sems, recv_sems, fwd_send_sems, fwd_recv_sems = sems
        x, y, c = _mesh_pos()
        sibling = (x, y, 1 - c)
        chips = _other_chips(x, y)
        n = len(ins)
        forwards = []
        for a in range(n):
            for k, (px, py) in enumerate(chips):
                landed = self._piece(a, outs[a].at[2 * px + py], c)
                pltpu.make_async_remote_copy(landed, landed, send_sems.at[a, k], recv_sems.at[a, k],
                                             device_id=(px, py, c), device_id_type=MESH).wait_recv()
                if self.split[a]:
                    fwd = pltpu.make_async_remote_copy(landed, landed, fwd_send_sems.at[a, k], fwd_recv_sems.at[a, k],
                                                       device_id=sibling, device_id_type=MESH)
                    fwd.start()
                    forwards.append(fwd)
        for a in range(n):
            if self.split[a]:
                for k, (px, py) in enumerate(chips):
                    other = self._piece(a, outs[a].at[2 * px + py], 1 - c)
                    pltpu.make_async_remote_copy(other, other, fwd_send_sems.at[a, k], fwd_recv_sems.at[a, k],
                                                 device_id=sibling, device_id_type=MESH).wait_recv()
        for a in range(n):
            for k, (px, py) in enumerate(chips):
                sent = self._piece(a, ins[a], c)
                pltpu.make_async_remote_copy(sent, sent, send_sems.at[a, k], recv_sems.at[a, k],
                                             device_id=(px, py, c), device_id_type=MESH).wait_send()
        for fwd in forwards:
            fwd.wait_send()

    def finalize(self, gathered):
        x, y, _ = _mesh_pos()
        return [lax.dynamic_update_index_in_dim(g, s, 2 * x + y, 0) for g, s in zip(gathered, self.shards)]

    def run(self, name):
        n = len(self.shards)

        def body(*refs):
            ins, outs, sems = refs[:n], refs[n:2 * n], refs[2 * n:]
            self.start(ins, outs, sems)
            self.finish(ins, outs, sems)

        gathered = pl.pallas_call(
            body, name=name, in_specs=[ANY] * n, out_specs=[ANY] * n, out_shape=self.out_shape,
            scratch_shapes=self.scratch, compiler_params=pltpu.CompilerParams(has_side_effects=True),
        )(*self.shards)
        return self.finalize(gathered)


def _ffn_weight_grads(hb, dgu, a, dyb, tag):
    dwin = _matmul_tn(hb, dgu, D_MODEL, FF_CHUNK, f"{tag}_dw_in", stack_n=True)
    dwo = _matmul_tn(a, dyb, FF_CHUNK, D_MODEL, f"{tag}_dw_out")
    return dwin, dwo.reshape(N_CHIPS, D_FF // N_CHIPS, D_MODEL)


def _local_step(x, target, w, late=None, reduce=False):
    g = {}
    acts = []
    w = dict(w)

    def ffn_weights(which, layer):
        return w[f"n{which}"], w[f"win{which}_l{layer}"], w[f"wout{which}_l{layer}"]

    def hosting(name):
        exchange, layouts = late.get(name, (None, None)) if late else (None, None)
        return exchange, (lambda arrived: w.update(layouts(arrived)) if exchange is not None else None)

    def ffn(xin, which, layer):
        name = f"ffn{which}_l{layer}_fwd"
        exchange, keep = hosting(name)
        (xo, gv, uv, hb), arrived = _ffn_fwd(xin, *ffn_weights(which, layer), layer, name, exchange)
        keep(arrived)
        acts.append((xin, gv, uv, hb))
        return xo

    x1 = ffn(x, 1, 0)
    hb_mix = _rms_fwd_call(x1, w["nmix"], 0, "ab_norm_fwd")
    proj = _matmul(hb_mix, w["wp"], "ab_in_proj")
    qkv = _conv_fwd(proj, w["conv_w"], "dn_conv_fwd")
    exchange, keep = hosting("dn_prep")
    (dn_u, dn_w, dn_p, dn_qd, dn_kt, dn_gl, dn_inv), arrived = _dn_prep(qkv, proj, w["hp"], "dn_prep", exchange)
    keep(arrived)
    exchange, keep = hosting("dn_scan")
    (o, dn_vn, sall), arrived = _dn_scan(dn_u, dn_w, dn_p, dn_qd, dn_kt, dn_gl, "dn_scan", exchange)
    keep(arrived)
    mix = _mix_fwd(o, proj, w["dn_norm"], w["sg_norm"], w["sg_w"], w["sg_bt"], "ab_gate_fwd")
    x2 = _matmul(mix, w["wo"], "ab_out_proj", res=x1)
    x3 = ffn(x2, 2, 0)
    x4 = ffn(x3, 1, 1)
    x5 = _pool_fwd(x4, w["nmix"], w["pool_w"], w["pool_scale"], 1, "pool_fwd")
    x6 = ffn(x5, 2, 1)
    loss, dx, g["fn"] = _loss_head(x6, target, w["fn"], "loss_head")

    dn = {1: [None, None], 2: [None, None]}
    dwin = {1: [None, None], 2: [None, None]}
    dwout = {1: [None, None], 2: [None, None]}

    def ffn_back(dxo, which, layer, saved, exchange=None):
        nw, win, wout = ffn_weights(which, layer)
        xin, gv, uv, hb = saved
        tag = f"ffn{which}_l{layer}"
        (dxi, dgu, a, dyb, dnw), arrived = _ffn_bwd(dxo, xin, nw, gv, uv, win, wout, layer, f"{tag}_bwd", exchange)
        dn[which][layer] = dnw
        dwin[which][layer], dwout[which][layer] = _ffn_weight_grads(hb, dgu, a, dyb, tag)
        return dxi, arrived

    for which in (1, 2):
        g[f"win{which}"] = dwin[which]
        g[f"wout{which}"] = dwout[which]
    reduced = {}

    def open_round(tag, keys):
        have = _sharded_grads(g)
        return _GradRound(tag, {k: have[k] for k in keys})

    dx, _ = ffn_back(dx, 2, 1, acts[3])
    dx, g["pool_w"], g["pool_scale"], dnmix1 = _pool_bwd(dx, x4, w["nmix"], w["pool_w"], w["pool_scale"], 1, "pool_bwd")
    dx, _ = ffn_back(dx, 1, 1, acts[2])
    round_a = open_round("a", REDUCE_ROUNDS[0]) if reduce else None
    dx2, arrived = ffn_back(dx, 2, 0, acts[1], round_a.scatter if reduce else None)
    if reduce:
        reduced.update(round_a.finish(arrived))
    round_b = open_round("b", REDUCE_ROUNDS[1]) if reduce else None
    dmix = _matmul(dx2, w["wo"], "ab_out_proj_bwd", trans_b=True)
    g["wo"] = _matmul_tn(mix, dx2, D_MODEL, D_MODEL, "ab_out_proj_dw")
    do, dzuv, g["dn_norm"], g["sg_norm"], g["sg_w"], g["sg_bt"] = _mix_bwd(
        dmix, o, proj, w["dn_norm"], w["sg_norm"], w["sg_w"], w["sg_bt"], "ab_gate_bwd")
    dvn, dkt, dgl = _dn_scan_bwd(dn_w, dn_p, dn_qd, dn_kt, dn_gl, dn_vn, sall, do, "dn_scan_bwd")
    (dqkv_act, dba, g["hp"]), arrived = _dn_prep_bwd(qkv, proj, w["hp"], sall, dn_vn, do, dvn, dkt, dgl, dn_inv, dn_u, dn_w,
                                                     "dn_prep_bwd", round_b.scatter if reduce else None)
    if reduce:
        reduced.update(round_b.finish(arrived))
    dqkv, g["conv_w"] = _conv_bwd(dqkv_act, proj, w["conv_w"], "dn_conv_bwd")
    dproj = jnp.concatenate([dqkv, dzuv, dba.astype(BF16)], axis=1)
    dh = _matmul(dproj, w["wp"], "ab_in_proj_bwd", trans_b=True)
    g["wp"] = _matmul_tn(hb_mix, dproj, D_MODEL, 640, "ab_in_proj_dw")
    dx1, dnmix0 = _rms_bwd_call(dh, x1, w["nmix"], dx2, 0, "ab_norm_bwd")
    dx0, _ = ffn_back(dx1, 1, 0, acts[0])
    if reduce:
        round_c = open_round("c", REDUCE_ROUNDS[2])
        reduced.update(round_c.finish(round_c.scatter.run("grad_c_chip_scatter")))

    g["n1"] = jnp.concatenate(dn[1], axis=0)
    g["n2"] = jnp.concatenate(dn[2], axis=0)
    g["nmix"] = jnp.concatenate([dnmix0, dnmix1], axis=0)
    return loss, dx0, g, reduced


SHARDED = ("ffn1_w_in", "ffn1_w_out", "ffn2_w_in", "ffn2_w_out", "ab_w_in", "ab_w_out", "pool_w", "dn_conv_w", "pool_scale")
REPLICATED = ("ffn_norm1", "mix_norm", "ffn_norm2", "dn_a_log", "dn_dt_bias", "dn_out_norm", "sg_norm", "sg_w", "sg_b", "final_norm")
QKVZ = 4 * DN_HEADS * DN_DIM
N_GATES = 2 * DN_HEADS
IN_PROJ = QKVZ + N_GATES + 2 * SG_GROUPS * SG_DIM


def _shard_pieces(wts, keys):
    out = []
    for n, layer in keys:
        a = wts[n][0 if layer is None else layer]
        a = a[None] if a.ndim == 1 else a
        out.append(a.astype(BF16) if n in MATRICES else a)
    return out


def _replicated_layouts(rep):
    per_layer = lambda a: a.reshape(a.shape[0], 1, D_MODEL)
    w = {"n1": per_layer(rep["ffn_norm1"]), "nmix": per_layer(rep["mix_norm"]), "n2": per_layer(rep["ffn_norm2"])}
    hp = jnp.zeros((8, 128), F32)
    w["hp"] = hp.at[0, DN_HEADS:N_GATES].set(rep["dn_a_log"][0]).at[1, DN_HEADS:N_GATES].set(rep["dn_dt_bias"][0])
    w["dn_norm"] = rep["dn_out_norm"]
    w["sg_norm"] = rep["sg_norm"][0]
    w["sg_w"] = rep["sg_w"][0]
    w["sg_bt"] = jnp.zeros((SG_CHUNK, 128), F32).at[:, :SG_GROUPS].set(rep["sg_b"][0].T)
    w["fn"] = rep["final_norm"].reshape(1, D_MODEL)
    return w


def _layouts_from(gathered):
    w = {}
    for (n, layer), a in gathered.items():
        if n in ("ffn1_w_in", "ffn2_w_in"):
            w[f"win{n[3]}_l{layer}"] = a
        elif n in ("ffn1_w_out", "ffn2_w_out"):
            w[f"wout{n[3]}_l{layer}"] = a
        elif n == "ab_w_in":
            ab_in = jnp.transpose(a, (1, 0, 2)).reshape(D_MODEL, IN_PROJ)
            w["wp"] = jnp.concatenate([ab_in[:, :QKVZ], ab_in[:, QKVZ + N_GATES:], ab_in[:, QKVZ:QKVZ + N_GATES],
                                       jnp.zeros((D_MODEL, PROJ_W - IN_PROJ), ab_in.dtype)], axis=1)
        elif n == "dn_conv_w":
            w["conv_w"] = jnp.transpose(a, (1, 0, 2)).reshape(DN_CONV, 3 * DN_HEADS * DN_DIM)
        elif n == "ab_w_out":
            w["wo"] = a.reshape(D_MODEL, D_MODEL)
        elif n == "pool_w":
            w["pool_w"] = jnp.transpose(a, (1, 0, 2, 3)).reshape(len(POOL_WINDOWS), POOL_DIM, POOL_DIM)
        elif n == "pool_scale":
            w["pool_scale"] = a.reshape(1, D_MODEL)
    return w


def _sharded_grads(g):
    nw = len(POOL_WINDOWS)
    sharded = {}
    for n, key in (("ffn1_w_in", "win1"), ("ffn1_w_out", "wout1"), ("ffn2_w_in", "win2"), ("ffn2_w_out", "wout2")):
        for layer, a in enumerate(g.get(key, ())):
            if a is not None:
                sharded[(n, layer)] = a
    if "wp" in g:
        wp = g["wp"]
        ab_in = jnp.concatenate([wp[:, :QKVZ], wp[:, IN_PROJ - N_GATES:IN_PROJ], wp[:, QKVZ:IN_PROJ - N_GATES]], axis=1)
        sharded[("ab_w_in", None)] = jnp.transpose(ab_in.reshape(D_MODEL, N_CHIPS, IN_PROJ // N_CHIPS), (1, 0, 2))
    if "wo" in g:
        sharded[("ab_w_out", None)] = g["wo"].reshape(N_CHIPS, D_MODEL // N_CHIPS, D_MODEL)
    if "pool_w" in g:
        sharded[("pool_w", None)] = jnp.transpose(g["pool_w"].reshape(nw, N_CHIPS, POOL_DIM // N_CHIPS, POOL_DIM), (1, 0, 2, 3))
    if "conv_w" in g:
        sharded[("dn_conv_w", None)] = jnp.transpose(g["conv_w"].reshape(DN_CONV, N_CHIPS, -1), (1, 0, 2))
    if "pool_scale" in g:
        sharded[("pool_scale", None)] = g["pool_scale"].reshape(N_CHIPS, 1, D_MODEL // N_CHIPS)
    return sharded


def _replicated_grads(g):
    rep = {
        "ffn_norm1": g["n1"], "mix_norm": g["nmix"], "ffn_norm2": g["n2"],
        "dn_a_log": g["hp"][0:1, DN_HEADS:N_GATES], "dn_dt_bias": g["hp"][1:2, DN_HEADS:N_GATES],
        "dn_out_norm": g["dn_norm"], "sg_norm": g["sg_norm"][None], "sg_w": g["sg_w"][None],
        "sg_b": g["sg_bt"][:, :SG_GROUPS].T[None], "final_norm": g["fn"].reshape(D_MODEL),
    }
    return rep


def _as_halves(a):
    shape = a.shape[1:]
    if len(shape) >= 2 and shape[0] % 2 == 0:
        return a.reshape(N_CHIPS, 2, -1, shape[-1])
    return a.reshape(N_CHIPS, 2, 1, -1)


def _row_block(rows):
    for cand in (256, 176, 128, 64, 32, 16):
        if rows % cand == 0:
            return cand
    return rows


def _from_halves(mine, other, core, shape):
    both = jnp.stack([jnp.where(core == 0, mine, other), jnp.where(core == 0, other, mine)])
    return both.reshape(shape)


def _swap_with_sibling(packs, name):
    n = len(packs)

    def body(*refs):
        ins, outs, send_sems, recv_sems = refs[:n], refs[n:2 * n], refs[2 * n], refs[2 * n + 1]
        x, y, c = _mesh_pos()
        copies = [pltpu.make_async_remote_copy(ins[k].at[:, 1 - c], outs[k], send_sems.at[k], recv_sems.at[k],
                                               device_id=(x, y, 1 - c), device_id_type=MESH) for k in range(n)]
        for cp in copies:
            cp.start()
        for cp in copies:
            cp.wait()

    return pl.pallas_call(
        body, name=name, in_specs=[ANY] * n, out_specs=[ANY] * n,
        out_shape=[jax.ShapeDtypeStruct((p.shape[0],) + p.shape[2:], p.dtype) for p in packs],
        scratch_shapes=[pltpu.SemaphoreType.DMA((n,)), pltpu.SemaphoreType.DMA((n,))],
        compiler_params=pltpu.CompilerParams(has_side_effects=True),
    )(*packs)


def _add_pair(pack, recv, core, name):
    nchip, _, rows, lanes = pack.shape
    rb = _row_block(rows)

    def body(c_ref, a_ref, b_ref, o32_ref, o16_ref):
        s = a_ref[...] + b_ref[...]
        o32_ref[...] = s
        o16_ref[...] = _bf(s)

    blk = pl.BlockSpec((None, rb, lanes), lambda p, i, c: (p, i, 0))
    return pl.pallas_call(
        body, name=name,
        grid_spec=pltpu.PrefetchScalarGridSpec(
            num_scalar_prefetch=1, grid=(nchip, rows // rb),
            in_specs=[pl.BlockSpec((None, None, rb, lanes), lambda p, i, c: (p, c[0], i, 0)), blk],
            out_specs=[blk, blk]),
        out_shape=[jax.ShapeDtypeStruct((nchip, rows, lanes), F32), jax.ShapeDtypeStruct((nchip, rows, lanes), BF16)],
        compiler_params=_params(("parallel", "parallel")),
    )(core, pack, recv)


class _ChipScatter:
    def __init__(self, parts16):
        self.operands = list(parts16)
        n = len(self.operands)
        self.out_shape = [jax.ShapeDtypeStruct((N_CHIPS - 1,) + p.shape[1:], p.dtype) for p in self.operands]
        self.scratch = [pltpu.SemaphoreType.DMA((n, N_CHIPS - 1))] * 2

    def _copies(self, ins, outs, sems):
        x, y, c = _mesh_pos()
        return [pltpu.make_async_remote_copy(ins[a].at[2 * px + py], outs[a].at[k], sems[0].at[a, k], sems[1].at[a, k],
                                             device_id=(px, py, c), device_id_type=MESH)
                for a in range(len(ins)) for k, (px, py) in enumerate(_other_chips(x, y))]

    def start(self, ins, outs, sems):
        for cp in self._copies(ins, outs, sems):
            cp.start()

    def finish(self, ins, outs, sems):
        for cp in self._copies(ins, outs, sems):
            cp.wait()

    def finalize(self, results):
        return results

    def run(self, name):
        n = len(self.operands)

        def body(*refs):
            ins, outs, sems = refs[:n], refs[n:2 * n], refs[2 * n:]
            self.start(ins, outs, sems)
            self.finish(ins, outs, sems)

        return pl.pallas_call(
            body, name=name, in_specs=[ANY] * n, out_specs=[ANY] * n, out_shape=self.out_shape, scratch_shapes=self.scratch,
            compiler_params=pltpu.CompilerParams(has_side_effects=True),
        )(*self.operands)


def _sum_chips(part32, recv16, chip, name):
    nchip, rows, lanes = part32.shape
    rb = _row_block(rows)

    def body(p_ref, own_ref, r_ref, o_ref):
        s = own_ref[...]
        for k in range(nchip - 1):
            s = s + r_ref[k].astype(F32)
        o_ref[...] = s

    return pl.pallas_call(
        body, name=name,
        grid_spec=pltpu.PrefetchScalarGridSpec(
            num_scalar_prefetch=1, grid=(rows // rb,),
            in_specs=[pl.BlockSpec((None, rb, lanes), lambda i, p: (p[0], i, 0)),
                      pl.BlockSpec((nchip - 1, rb, lanes), lambda i, p: (0, i, 0))],
            out_specs=pl.BlockSpec((rb, lanes), lambda i, p: (i, 0))),
        out_shape=jax.ShapeDtypeStruct((rows, lanes), F32),
        compiler_params=_params(("parallel",)),
    )(chip, part32, recv16)


def _share_with_sibling(halves, name):
    n = len(halves)

    def body(*refs):
        ins, outs, send_sems, recv_sems = refs[:n], refs[n:2 * n], refs[2 * n], refs[2 * n + 1]
        x, y, c = _mesh_pos()
        copies = [pltpu.make_async_remote_copy(ins[k], outs[k], send_sems.at[k], recv_sems.at[k],
                                               device_id=(x, y, 1 - c), device_id_type=MESH) for k in range(n)]
        for cp in copies:
            cp.start()
        for cp in copies:
            cp.wait()

    return pl.pallas_call(
        body, name=name, in_specs=[ANY] * n, out_specs=[ANY] * n,
        out_shape=[jax.ShapeDtypeStruct(h.shape, h.dtype) for h in halves],
        scratch_shapes=[pltpu.SemaphoreType.DMA((n,)), pltpu.SemaphoreType.DMA((n,))],
        compiler_params=pltpu.CompilerParams(has_side_effects=True),
    )(*halves)


class _GradRound:
    def __init__(self, tag, pieces):
        self.tag, self.keys = tag, list(pieces)
        self.shapes = [pieces[k].shape[1:] for k in self.keys]
        _, _, c = _mesh_pos()
        core = jnp.reshape(c, (1,)).astype(jnp.int32)
        packs = [_as_halves(pieces[k]) for k in self.keys]
        recvs = _swap_with_sibling(packs, f"grad_{tag}_pair_swap")
        sums = [_add_pair(p, r, core, f"grad_{tag}_pair_add_{i}") for i, (p, r) in enumerate(zip(packs, recvs))]
        self.parts32 = [s[0] for s in sums]
        self.scatter = _ChipScatter([s[1] for s in sums])

    def finish(self, recvs16):
        x, y, c = _mesh_pos()
        chip = jnp.reshape(2 * x + y, (1,)).astype(jnp.int32)
        halves = [_sum_chips(p, r, chip, f"grad_{self.tag}_chip_sum_{i}") for i, (p, r) in enumerate(zip(self.parts32, recvs16))]
        others = _share_with_sibling(halves, f"grad_{self.tag}_pair_share")
        return {k: _from_halves(h, o, c, shape) for k, h, o, shape in zip(self.keys, halves, others, self.shapes)}


def _pack_small(vals):
    parts = []
    for n in REPLICATED:
        flat = vals[n].reshape(-1)
        rows = -(-flat.shape[0] // 128)
        rows = -(-rows // 8) * 8
        parts.append(jnp.pad(flat, (0, rows * 128 - flat.shape[0])).reshape(rows, 128))
    return jnp.concatenate(parts, axis=0)


def _unpack_small(pack, like):
    out, off = {}, 0
    for n in REPLICATED:
        size = like[n].size
        rows = -(-size // 128)
        rows = -(-rows // 8) * 8
        out[n] = pack[off:off + rows].reshape(-1)[:size].reshape(like[n].shape)
        off += rows
    return out


def _all_to_all_small(pack, name):
    rows, lanes = pack.shape
    flips = [(dx, dy, dc) for dx in (0, 1) for dy in (0, 1) for dc in (0, 1)][1:]

    def body(src_ref, out_ref, send_sems, recv_sems, local_sem):
        x, y, c = _mesh_pos()
        me = 4 * x + 2 * y + c
        loc = pltpu.make_async_copy(src_ref, out_ref.at[me], local_sem)
        loc.start()
        copies = []
        for k, (dx, dy, dc) in enumerate(flips):
            peer = (x ^ dx, y ^ dy, c ^ dc)
            cp = pltpu.make_async_remote_copy(src_ref, out_ref.at[me], send_sems.at[k], recv_sems.at[k],
                                              device_id=peer, device_id_type=MESH)
            cp.start()
            copies.append(cp)
        for k, (dx, dy, dc) in enumerate(flips):
            peer = (x ^ dx, y ^ dy, c ^ dc)
            pltpu.make_async_remote_copy(src_ref, out_ref.at[4 * peer[0] + 2 * peer[1] + peer[2]], send_sems.at[k],
                                         recv_sems.at[k], device_id=peer, device_id_type=MESH).wait_recv()
        for cp in copies:
            cp.wait_send()
        loc.wait()

    return pl.pallas_call(
        body, name=name, in_specs=[ANY], out_specs=ANY,
        out_shape=jax.ShapeDtypeStruct((8, rows, lanes), pack.dtype),
        scratch_shapes=[pltpu.SemaphoreType.DMA((7,)), pltpu.SemaphoreType.DMA((7,)), pltpu.SemaphoreType.DMA],
        compiler_params=pltpu.CompilerParams(has_side_effects=True),
    )(pack)


def _sum_devices(stack, name):
    ndev, rows, lanes = stack.shape

    def body(s_ref, o_ref):
        s = s_ref[0]
        for d in range(1, ndev):
            s = s + s_ref[d]
        o_ref[...] = s

    return pl.pallas_call(
        body, name=name, grid=(1,),
        in_specs=[pl.BlockSpec((ndev, rows, lanes), lambda i: (0, 0, 0))],
        out_specs=pl.BlockSpec((rows, lanes), lambda i: (0, 0)),
        out_shape=jax.ShapeDtypeStruct((rows, lanes), F32),
    )(stack)


WEIGHT_ORDER = ("ffn_norm1", "ffn1_w_in", "ffn1_w_out", "mix_norm", "ffn_norm2", "ffn2_w_in", "ffn2_w_out", "ab_w_in",
                "dn_conv_w", "dn_a_log", "dn_dt_bias", "dn_out_norm", "sg_norm", "sg_w", "sg_b", "ab_w_out", "pool_w",
                "pool_scale", "final_norm")
MATRICES = ("ffn1_w_in", "ffn1_w_out", "ffn2_w_in", "ffn2_w_out", "ab_w_in", "ab_w_out", "pool_w")
GATHER_FIRST = (("ffn1_w_in", 0), ("ffn1_w_out", 0))
GATHER_LATER = {"ffn1_l0_fwd": (("ab_w_in", None), ("dn_conv_w", None), ("ab_w_out", None)),
                "dn_prep": (("ffn2_w_in", 0), ("ffn2_w_out", 0)),
                "dn_scan": (("ffn1_w_in", 1), ("ffn1_w_out", 1)),
                "ffn2_l0_fwd": (("pool_w", None), ("pool_scale", None), ("ffn2_w_in", 1), ("ffn2_w_out", 1))}
REDUCE_ROUNDS = ((("ffn2_w_in", 1), ("ffn2_w_out", 1), ("ffn1_w_in", 1), ("ffn1_w_out", 1), ("pool_w", None), ("pool_scale", None)),
                 (("ffn2_w_in", 0), ("ffn2_w_out", 0)),
                 (("ffn1_w_in", 0), ("ffn1_w_out", 0), ("ab_w_out", None), ("ab_w_in", None), ("dn_conv_w", None)))


def _as_2d(a):
    return a.reshape(-1, a.shape[-1])


def kernel(x, ffn_norm1, ffn1_w_in, ffn1_w_out, mix_norm, ffn_norm2, ffn2_w_in, ffn2_w_out, ab_w_in, dn_conv_w, dn_a_log, dn_dt_bias, dn_out_norm, sg_norm, sg_w, sg_b, ab_w_out, pool_w, pool_scale, final_norm, loss_target, m_ffn_norm1, m_ffn1_w_in, m_ffn1_w_out, m_mix_norm, m_ffn_norm2, m_ffn2_w_in, m_ffn2_w_out, m_ab_w_in, m_dn_conv_w, m_dn_a_log, m_dn_dt_bias, m_dn_out_norm, m_sg_norm, m_sg_w, m_sg_b, m_ab_w_out, m_pool_w, m_pool_scale, m_final_norm, v_ffn_norm1, v_ffn1_w_in, v_ffn1_w_out, v_mix_norm, v_ffn_norm2, v_ffn2_w_in, v_ffn2_w_out, v_ab_w_in, v_dn_conv_w, v_dn_a_log, v_dn_dt_bias, v_dn_out_norm, v_sg_norm, v_sg_w, v_sg_b, v_ab_w_out, v_pool_w, v_pool_scale, v_final_norm):
    given = dict(locals())
    wts = {n: given[n] for n in WEIGHT_ORDER}
    mom_m = {n: given["m_" + n] for n in WEIGHT_ORDER}
    mom_v = {n: given["v_" + n] for n in WEIGHT_ORDER}

    rep = {n: wts[n] for n in REPLICATED}
    first = _ChipGather(_shard_pieces(wts, GATHER_FIRST), [n in MATRICES for n, _ in GATHER_FIRST])
    w = {**_replicated_layouts(rep), **_layouts_from(dict(zip(GATHER_FIRST, first.run("weight_gather_first"))))}
    late = {host: (_ChipGather(_shard_pieces(wts, keys), [n in MATRICES for n, _ in keys]),
                   functools.partial(lambda keys, arrived: _layouts_from(dict(zip(keys, arrived))), keys))
            for host, keys in GATHER_LATER.items()}

    loss, dx, g, reduced = _local_step(x[0], loss_target[0], w, reduce=True, late=late)
    g_rep = _replicated_grads(g)
    grads = {}
    for n in SHARDED:
        layers = [reduced[(n, layer)] for layer in range(wts[n].shape[0])] if (n, 0) in reduced else [reduced[(n, None)]]
        grads[n] = jnp.stack(layers).reshape(wts[n].shape)
    small = _sum_devices(_all_to_all_small(_pack_small(g_rep), "grad_small_exchange"), "grad_small_sum")
    grads.update(_unpack_small(small, rep))

    delta, new_m, new_v = {}, {}, {}
    for n in SHARDED:
        d, m1, v1 = _adamw(_as_2d(wts[n]), _as_2d(grads[n]), _as_2d(mom_m[n]), _as_2d(mom_v[n]), f"adamw_{n}")
        delta[n], new_m[n], new_v[n] = (t.reshape(wts[n].shape) for t in (d, m1, v1))
    d, m1, v1 = _adamw(_pack_small(rep), small, _pack_small({n: mom_m[n] for n in REPLICATED}),
                       _pack_small({n: mom_v[n] for n in REPLICATED}), "adamw_replicated")
    for tgt, packed in ((delta, d), (new_m, m1), (new_v, v1)):
        tgt.update(_unpack_small(packed, rep))

    total = lax.psum(loss[0, 0], ("x", "y", "c"))
    outs = [total, dx[None]]
    for group in (grads, delta, new_m, new_v):
        outs.extend(group[n] for n in WEIGHT_ORDER)
    return tuple(outs)
```

```python
import functools

import jax
import jax.numpy as jnp
from jax import lax
from jax.experimental import pallas as pl
from jax.experimental.pallas import tpu as pltpu

F32, BF16 = jnp.float32, jnp.bfloat16
NORM_EPS = 1e-6
D_MODEL = 1024
D_FF = 2816
N_CHIPS = 4
FF_CHUNK = 2 * D_FF // N_CHIPS
DN_HEADS, DN_DIM, DN_CHUNK, DN_CONV = 4, 128, 64, 4
DN_BLOCK = 2 * DN_CHUNK
DN_PREP_CHUNKS = 4
DN_SCAN_CHUNKS = 8
SG_GROUPS, SG_DIM, SG_CHUNK = 4, 128, 128
POOL_WINDOWS = (2, 4, 8, 16)
POOL_DIM = 256
POOL_HALO = 16
CONV_HALO = 8
PROJ_W = 3200
BA_BLOCK = 3072 // 128
ADAM_LR, ADAM_B1, ADAM_B2, ADAM_EPS, ADAM_WD, ADAM_STEP = 0.001, 0.9, 0.999, 1e-08, 0.01, 10
VMEM_BIG = 52 * 1024 * 1024
FFN_FWD_ROWS = 512
FFN_BWD_ROWS = 256
MESH = pl.DeviceIdType.MESH
HI = lax.Precision.HIGHEST
ANY = pl.BlockSpec(memory_space=pl.ANY)


def _params(sem=None, vmem=None):
    return pltpu.CompilerParams(dimension_semantics=sem, vmem_limit_bytes=vmem)


def _dot(a, b):
    return jnp.dot(a, b, preferred_element_type=F32)


def _dot_nt(a, b):
    return lax.dot_general(a, b, (((1,), (1,)), ((), ())), preferred_element_type=F32)


def _dot_tn(a, b):
    return lax.dot_general(a, b, (((0,), (0,)), ((), ())), preferred_element_type=F32)


def _dot_hi(a, b):
    return jnp.dot(a, b, preferred_element_type=F32, precision=HI)


def _dot_mid(a, b):
    return jnp.dot(a, b, preferred_element_type=F32, precision=lax.Precision.HIGH)


def _bf(a):
    return a.astype(BF16)


def _rms_stats(x):
    r = lax.rsqrt(jnp.mean(x * x, axis=-1, keepdims=True) + NORM_EPS)
    return x * r, r


def _rms_bwd(dh, xhat, r, w):
    dhn = dh * w
    return r * (dhn - xhat * jnp.mean(dhn * xhat, axis=-1, keepdims=True))


def _sigmoid(x):
    return jax.nn.sigmoid(x)


def _silu_grad(x, s):
    return s * (1.0 + x * (1.0 - s))


def _gelu(x):
    return 0.5 * x * (1.0 + lax.erf(x * 0.7071067811865476))


def _gelu_grad(x):
    return 0.5 * (1.0 + lax.erf(x * 0.7071067811865476)) + x * jnp.exp(-0.5 * x * x) * 0.3989422804014327


def _softplus(x):
    return jnp.maximum(x, 0.0) + jnp.log(1.0 + jnp.exp(-jnp.abs(x)))


def _tile(n, pref):
    t = min(n, pref)
    assert n % t == 0, (n, t)
    return t


def _ffn_weight_specs():
    once = pl.Buffered(1)
    return [pl.BlockSpec((N_CHIPS, D_MODEL, FF_CHUNK), lambda i: (0, 0, 0), pipeline_mode=once),
            pl.BlockSpec((N_CHIPS, D_FF // N_CHIPS, D_MODEL), lambda i: (0, 0, 0), pipeline_mode=once)]


def _ffn_fwd(x, nw, win, wout, layer, name, exchange=None):
    T, D = x.shape
    tm = _tile(T, FFN_FWD_ROWS)
    nj = N_CHIPS // 2

    def body(x_ref, n_ref, win_ref, wo_ref, xo_ref, g_ref, u_ref, hb_ref):
        xv = x_ref[...]
        xhat, _ = _rms_stats(xv)
        h = _bf(xhat * n_ref[...])
        hb_ref[...] = h
        acc = None
        for j in range(nj):
            cols = slice(j * FF_CHUNK, (j + 1) * FF_CHUNK)
            g = _dot(h, win_ref[j])
            u = _dot(h, win_ref[nj + j])
            g_ref[:, cols] = _bf(g)
            u_ref[:, cols] = _bf(u)
            part = _dot(_bf(g * _sigmoid(g) * u), wo_ref[2 * j:2 * j + 2].reshape(FF_CHUNK, D))
            acc = part if acc is None else acc + part
        xo_ref[...] = xv + 0.5 * acc

    row = pl.BlockSpec((tm, D), lambda i: (i, 0))
    wide = pl.BlockSpec((tm, D_FF), lambda i: (i, 0))
    return _call_with_exchange(
        body, exchange, name=name, steps=T // tm, vmem=VMEM_BIG,
        in_specs=[row, pl.BlockSpec((None, 1, D), lambda i: (layer, 0, 0))] + _ffn_weight_specs(),
        out_specs=[row, wide, wide, row],
        out_shape=[jax.ShapeDtypeStruct((T, D), F32), jax.ShapeDtypeStruct((T, D_FF), BF16),
                   jax.ShapeDtypeStruct((T, D_FF), BF16), jax.ShapeDtypeStruct((T, D), BF16)],
        args=(x, nw, win, wout))


def _ffn_bwd(dxo, x, nw, g, u, win, wout, layer, name, exchange=None):
    T, D = x.shape
    tm = _tile(T, FFN_BWD_ROWS)
    nj = N_CHIPS // 2

    def body(dxo_ref, x_ref, n_ref, g_ref, u_ref, win_ref, wo_ref, dx_ref, dgu_ref, a_ref, dyb_ref, dn_ref):
        @pl.when(pl.program_id(0) == 0)
        def _():
            dn_ref[...] = jnp.zeros_like(dn_ref)

        dxo = dxo_ref[...]
        dyb = _bf(0.5 * dxo)
        dyb_ref[...] = dyb
        dh = None
        for j in range(nj):
            cols = slice(j * FF_CHUNK, (j + 1) * FF_CHUNK)
            da = _dot_nt(dyb, wo_ref[2 * j:2 * j + 2].reshape(FF_CHUNK, D))
            gv = g_ref[:, cols].astype(F32)
            uv = u_ref[:, cols].astype(F32)
            sg = _sigmoid(gv)
            sl = gv * sg
            dgb = _bf(da * uv * _silu_grad(gv, sg))
            dub = _bf(da * sl)
            a_ref[:, cols] = _bf(sl * uv)
            dgu_ref[:, cols] = dgb
            dgu_ref[:, D_FF + j * FF_CHUNK:D_FF + (j + 1) * FF_CHUNK] = dub
            part = _dot_nt(dgb, win_ref[j]) + _dot_nt(dub, win_ref[nj + j])
            dh = part if dh is None else dh + part
        xhat, r = _rms_stats(x_ref[...])
        dx_ref[...] = dxo + _rms_bwd(dh, xhat, r, n_ref[...])
        dn_ref[...] += jnp.sum(dh * xhat, axis=0, keepdims=True)

    row = pl.BlockSpec((tm, D), lambda i: (i, 0))
    wide = pl.BlockSpec((tm, D_FF), lambda i: (i, 0))
    return _call_with_exchange(
        body, exchange, name=name, steps=T // tm, vmem=VMEM_BIG,
        in_specs=[row, row, pl.BlockSpec((None, 1, D), lambda i: (layer, 0, 0)), wide, wide] + _ffn_weight_specs(),
        out_specs=[row, pl.BlockSpec((tm, 2 * D_FF), lambda i: (i, 0)), wide, row, pl.BlockSpec((1, D), lambda i: (0, 0))],
        out_shape=[jax.ShapeDtypeStruct((T, D), F32), jax.ShapeDtypeStruct((T, 2 * D_FF), BF16),
                   jax.ShapeDtypeStruct((T, D_FF), BF16), jax.ShapeDtypeStruct((T, D), BF16),
                   jax.ShapeDtypeStruct((1, D), F32)],
        args=(dxo, x, nw, g, u, win, wout))


def _matmul_tn(a, b, bm, bn, name, stack_n=False):
    T, M = a.shape
    N = b.shape[1]
    tk = _tile(T, 1024)
    bm, bn = _tile(M, bm), _tile(N, bn)

    def body(a_ref, b_ref, o_ref):
        @pl.when(pl.program_id(2) == 0)
        def _():
            o_ref[...] = jnp.zeros_like(o_ref)

        o_ref[...] += _dot_tn(_bf(a_ref[...]), _bf(b_ref[...]))

    if stack_n:
        out_spec = pl.BlockSpec((None, bm, bn), lambda m, n, k: (n, m, 0))
        out_shape = jax.ShapeDtypeStruct((N // bn, M, bn), F32)
    else:
        out_spec = pl.BlockSpec((bm, bn), lambda m, n, k: (m, n))
        out_shape = jax.ShapeDtypeStruct((M, N), F32)
    return pl.pallas_call(
        body, name=name, grid=(M // bm, N // bn, T // tk),
        in_specs=[pl.BlockSpec((tk, bm), lambda m, n, k: (k, m)),
                  pl.BlockSpec((tk, bn), lambda m, n, k: (k, n))],
        out_specs=out_spec, out_shape=out_shape,
        compiler_params=_params(("parallel", "parallel", "arbitrary"), VMEM_BIG),
    )(a, b)


def _matmul(a, b, name, trans_b=False, res=None, out_dtype=F32):
    T, K = a.shape
    N = b.shape[0] if trans_b else b.shape[1]
    tm = _tile(T, 512)

    def body(*refs):
        a_ref, b_ref = refs[0], refs[1]
        o_ref = refs[-1]
        av, bv = _bf(a_ref[...]), _bf(b_ref[...])
        acc = _dot_nt(av, bv) if trans_b else _dot(av, bv)
        if res is not None:
            acc = acc + refs[2][...]
        o_ref[...] = acc.astype(out_dtype)

    in_specs = [pl.BlockSpec((tm, K), lambda i: (i, 0)), pl.BlockSpec(b.shape, lambda i: (0, 0))]
    args = [a, b]
    if res is not None:
        in_specs.append(pl.BlockSpec((tm, N), lambda i: (i, 0)))
        args.append(res)
    return pl.pallas_call(
        body, name=name, grid=(T // tm,), in_specs=in_specs,
        out_specs=pl.BlockSpec((tm, N), lambda i: (i, 0)),
        out_shape=jax.ShapeDtypeStruct((T, N), out_dtype),
        compiler_params=_params(("parallel",), VMEM_BIG),
    )(*args)


def _rms_fwd_call(x, nw, layer, name):
    T, D = x.shape
    tm = _tile(T, 512)

    def body(x_ref, n_ref, o_ref):
        xhat, _ = _rms_stats(x_ref[...])
        o_ref[...] = _bf(xhat * n_ref[...])

    return pl.pallas_call(
        body, name=name, grid=(T // tm,),
        in_specs=[pl.BlockSpec((tm, D), lambda i: (i, 0)), pl.BlockSpec((None, 1, D), lambda i: (layer, 0, 0))],
        out_specs=pl.BlockSpec((tm, D), lambda i: (i, 0)),
        out_shape=jax.ShapeDtypeStruct((T, D), BF16),
        compiler_params=_params(("parallel",)),
    )(x, nw)


def _rms_bwd_call(dh, x, nw, dres, layer, name):
    T, D = x.shape
    tm = _tile(T, 512)

    def body(dh_ref, x_ref, n_ref, dr_ref, dx_ref, dn_ref):
        @pl.when(pl.program_id(0) == 0)
        def _():
            dn_ref[...] = jnp.zeros_like(dn_ref)

        xhat, r = _rms_stats(x_ref[...])
        dh_v = dh_ref[...]
        dx_ref[...] = dr_ref[...] + _rms_bwd(dh_v, xhat, r, n_ref[...])
        dn_ref[...] += jnp.sum(dh_v * xhat, axis=0, keepdims=True)

    row = pl.BlockSpec((tm, D), lambda i: (i, 0))
    return pl.pallas_call(
        body, name=name, grid=(T // tm,),
        in_specs=[row, row, pl.BlockSpec((None, 1, D), lambda i: (layer, 0, 0)), row],
        out_specs=[row, pl.BlockSpec((1, D), lambda i: (0, 0))],
        out_shape=[jax.ShapeDtypeStruct((T, D), F32), jax.ShapeDtypeStruct((1, D), F32)],
        compiler_params=_params(("arbitrary",)),
    )(dh, x, nw, dres)


def _shift_rows(x, s):
    n = x.shape[0]
    s = s % n
    return x if s == 0 else pltpu.roll(x, s, 0)


def _conv_fwd(proj, conv_w, name):
    T = proj.shape[0]
    C = 3 * DN_HEADS * DN_DIM
    cb = 512
    tm = _tile(T, 512)
    hb = tm // CONV_HALO

    def body(x_ref, xp_ref, w_ref, o_ref):
        i = pl.program_id(1)
        prev = jnp.where(i == 0, 0.0, xp_ref[...])
        ext = jnp.concatenate([prev, x_ref[...]], axis=0)
        w = w_ref[...]
        y = ext * w[DN_CONV - 1:DN_CONV, :]
        for k in range(DN_CONV - 1):
            y = y + _shift_rows(ext, DN_CONV - 1 - k) * w[k:k + 1, :]
        y = y[CONV_HALO:, :]
        o_ref[...] = y * _sigmoid(y)

    return pl.pallas_call(
        body, name=name, grid=(C // cb, T // tm),
        in_specs=[pl.BlockSpec((tm, cb), lambda c, i: (i, c)),
                  pl.BlockSpec((CONV_HALO, cb), lambda c, i: (jnp.maximum(i * hb - 1, 0), c)),
                  pl.BlockSpec((DN_CONV, cb), lambda c, i: (0, c))],
        out_specs=pl.BlockSpec((tm, cb), lambda c, i: (i, c)),
        out_shape=jax.ShapeDtypeStruct((T, C), F32),
        compiler_params=_params(("parallel", "parallel")),
    )(proj, proj, conv_w)


def _conv_bwd(dy, proj, conv_w, name):
    T = proj.shape[0]
    C = 3 * DN_HEADS * DN_DIM
    cb = 512
    tm = _tile(T, 512)
    hb = tm // CONV_HALO
    nt = T // tm

    def body(x_ref, xp_ref, xn_ref, dy_ref, dyn_ref, w_ref, dx_ref, dw_ref):
        i = pl.program_id(1)

        @pl.when(i == 0)
        def _():
            dw_ref[...] = jnp.zeros_like(dw_ref)

        prev = jnp.where(i == 0, 0.0, xp_ref[...])
        ext = jnp.concatenate([prev, x_ref[...], xn_ref[...]], axis=0)
        dy_ext = jnp.concatenate([jnp.zeros((CONV_HALO, cb), F32), dy_ref[...],
                                  jnp.where(i == nt - 1, 0.0, dyn_ref[...])], axis=0)
        w = w_ref[...]
        shifted = [_shift_rows(ext, DN_CONV - 1 - k) for k in range(DN_CONV)]
        y = shifted[0] * w[0:1, :]
        for k in range(1, DN_CONV):
            y = y + shifted[k] * w[k:k + 1, :]
        s = _sigmoid(y)
        dpre = dy_ext * _silu_grad(y, s)
        dx = dpre * w[DN_CONV - 1:DN_CONV, :]
        for k in range(DN_CONV - 1):
            dx = dx + _shift_rows(dpre, -(DN_CONV - 1 - k)) * w[k:k + 1, :]
        dx_ref[...] = _bf(dx[CONV_HALO:CONV_HALO + tm, :])
        rows = [jnp.sum((dpre * shifted[k])[CONV_HALO:CONV_HALO + tm, :], axis=0, keepdims=True) for k in range(DN_CONV)]
        dw_ref[...] += jnp.concatenate(rows, axis=0)

    last_halo = T // CONV_HALO - 1
    return pl.pallas_call(
        body, name=name, grid=(C // cb, nt),
        in_specs=[pl.BlockSpec((tm, cb), lambda c, i: (i, c)),
                  pl.BlockSpec((CONV_HALO, cb), lambda c, i: (jnp.maximum(i * hb - 1, 0), c)),
                  pl.BlockSpec((CONV_HALO, cb), lambda c, i: (jnp.minimum((i + 1) * hb, last_halo), c)),
                  pl.BlockSpec((tm, cb), lambda c, i: (i, c)),
                  pl.BlockSpec((CONV_HALO, cb), lambda c, i: (jnp.minimum((i + 1) * hb, last_halo), c)),
                  pl.BlockSpec((DN_CONV, cb), lambda c, i: (0, c))],
        out_specs=[pl.BlockSpec((tm, cb), lambda c, i: (i, c)),
                   pl.BlockSpec((DN_CONV, cb), lambda c, i: (0, c))],
        out_shape=[jax.ShapeDtypeStruct((T, C), BF16), jax.ShapeDtypeStruct((DN_CONV, C), F32)],
        compiler_params=_params(("parallel", "arbitrary")),
    )(proj, proj, proj, dy, dy, conv_w)


def _unit_lower_inverses(lows, eye):
    def each(fn, *lists):
        return [fn(*args) for args in zip(*lists)]

    p1 = [-low for low in lows]
    p2 = each(_dot_mid, p1, p1)
    p4 = each(_dot_mid, p2, p2)
    a = each(lambda x, y: eye + x + y + _dot_mid(x, y), p1, p2)
    p8 = each(_dot_mid, p4, p4)
    p16 = each(_dot_mid, p8, p8)
    b = each(lambda x, y: eye + x + y + _dot_mid(x, y), p4, p8)
    p32 = each(_dot_mid, p16, p16)
    ab = each(_dot_mid, a, b)
    c = each(lambda x, y: eye + x + y + _dot_mid(x, y), p16, p32)
    return each(_dot_mid, ab, c)


def _interleave(chains):
    results = [None] * len(chains)
    live = list(range(len(chains)))
    while live:
        for i in list(live):
            try:
                next(chains[i])
            except StopIteration as stop:
                results[i] = stop.value
                live.remove(i)
        yield
    return results


def _run_interleaved(chains):
    rounds = _interleave(chains)
    while True:
        try:
            next(rounds)
        except StopIteration as stop:
            return stop.value


def _l2_unit(x):
    r = lax.rsqrt(jnp.sum(x * x, axis=-1, keepdims=True) + NORM_EPS)
    return x * r, r


class _BlockMasks:
    def __init__(self):
        n = DN_BLOCK
        row = lax.broadcasted_iota(jnp.int32, (n, n), 0)
        col = lax.broadcasted_iota(jnp.int32, (n, n), 1)
        same = (row // DN_CHUNK) == (col // DN_CHUNK)
        self.lower, self.strict_lower = same & (row >= col), same & (row > col)
        self.upper, self.strict_upper = same & (row <= col), same & (row < col)
        self.eye = (row == col).astype(F32)
        self.first = lax.broadcasted_iota(jnp.int32, (n, 1), 0) < DN_CHUNK


def _dn_gates(ba, hp):
    coef = -jnp.exp(hp[0:1, :])
    pre = ba + hp[1:2, :]
    return _sigmoid(ba), coef * _softplus(pre), coef, pre


def _dn_block_gates(mk, ba, hp):
    assert DN_BLOCK == 2 * DN_CHUNK
    beta_t, graw_t, coef, pre = _dn_gates(ba, hp)
    gcum_t = _dot_hi(mk.lower.astype(F32), graw_t)
    gl_t = jnp.where(mk.first, gcum_t[DN_CHUNK - 1:DN_CHUNK, :], gcum_t[DN_BLOCK - 1:DN_BLOCK, :])
    return beta_t, gcum_t, gl_t, graw_t, coef, pre


def _dn_local(mk, qraw, kraw, bc, gc, gl):
    f = {}
    f["qn"], f["rq"] = _l2_unit(qraw)
    qh = f["qn"] * (DN_DIM ** -0.5)
    kh, f["rk"] = _l2_unit(kraw)
    gr = jnp.broadcast_to(gc, (DN_BLOCK, DN_BLOCK)).T
    dec = jnp.where(mk.lower, jnp.exp(jnp.where(mk.lower, gc - gr, 0.0)), 0.0)
    kb = kh * bc
    mkk = _dot_nt(_bf(kb), _bf(kh))
    eg = jnp.exp(gc)
    mqk = _dot_nt(_bf(qh), _bf(kh))
    etl = jnp.exp(gl - gc)
    f.update(qh=qh, kh=kh, gr=gr, dec=dec, kb=kb, mkk=mkk, eg=eg, mqk=mqk, attn=mqk * dec, etl=etl, qd=qh * eg, kt=kh * etl)
    return f


def _dn_specs(rows, rev=None):
    at = (lambda n: n) if rev is None else rev
    hw = DN_HEADS * DN_DIM
    return dict(
        qkv=[pl.BlockSpec((rows, hw), lambda n, j=j: (at(n), j)) for j in range(3)],
        ba=pl.BlockSpec((rows, 128), lambda n: (at(n), BA_BLOCK)),
        hp=pl.BlockSpec((8, 128), lambda n: (0, 0)),
        tok=pl.BlockSpec((rows, hw), lambda n: (at(n), 0)),
        attn=pl.BlockSpec((rows, DN_HEADS * DN_CHUNK), lambda n: (at(n), 0)),
        gate=pl.BlockSpec((rows // DN_CHUNK, 8, 128), lambda n: (at(n), 0, 0)),
        state=pl.BlockSpec((rows // DN_CHUNK, DN_HEADS, DN_DIM, DN_DIM), lambda n: (at(n), 0, 0, 0)),
    )


def _call_with_exchange(body, exchange, *, name, steps, in_specs, out_specs, out_shape, args, vmem=None, scratch_shapes=()):
    if exchange is None:
        res = pl.pallas_call(body, name=name, grid=(steps,), in_specs=in_specs, out_specs=out_specs, out_shape=out_shape,
                             scratch_shapes=list(scratch_shapes), compiler_params=_params(("arbitrary",), vmem))(*args)
        return list(res), None
    n_in, n_out, m, n_scr = len(in_specs), len(out_specs), len(exchange.operands), len(scratch_shapes)

    def hosted(*refs):
        ins, ex_ins = refs[:n_in], refs[n_in:n_in + m]
        outs, ex_outs = refs[n_in + m:n_in + m + n_out], refs[n_in + m + n_out:n_in + 2 * m + n_out]
        scratch, sems = refs[n_in + 2 * m + n_out:n_in + 2 * m + n_out + n_scr], refs[n_in + 2 * m + n_out + n_scr:]

        @pl.when(pl.program_id(0) == 0)
        def _():
            exchange.start(ex_ins, ex_outs, sems)

        body(*ins, *outs, *scratch)

        @pl.when(pl.program_id(0) == steps - 1)
        def _():
            exchange.finish(ex_ins, ex_outs, sems)

    res = pl.pallas_call(
        hosted, name=name, grid=(steps,), in_specs=list(in_specs) + [ANY] * m, out_specs=list(out_specs) + [ANY] * m,
        out_shape=list(out_shape) + list(exchange.out_shape), scratch_shapes=list(scratch_shapes) + list(exchange.scratch),
        compiler_params=pltpu.CompilerParams(dimension_semantics=("arbitrary",), vmem_limit_bytes=vmem, has_side_effects=True),
    )(*args, *exchange.operands)
    return list(res[:n_out]), exchange.finalize(list(res[n_out:]))


def _dn_prep(qkv, proj, hp, name, exchange=None):
    T = qkv.shape[0]
    n_chunks = T // DN_CHUNK
    blocks = max(1, min(DN_PREP_CHUNKS, n_chunks) * DN_CHUNK // DN_BLOCK)
    group = blocks * DN_BLOCK // DN_CHUNK
    rows = blocks * DN_BLOCK
    hw = DN_HEADS * DN_DIM

    def body(q_ref, k_ref, v_ref, ba_ref, hp_ref, u_ref, w_ref, p_ref, qd_ref, kt_ref, gl_ref, inv_ref):
        mk = _BlockMasks()
        hp_v = hp_ref[...]
        chains = []
        for j in range(blocks):
            rs = slice(j * DN_BLOCK, (j + 1) * DN_BLOCK)
            beta_t, gcum_t, gl_t = _dn_block_gates(mk, ba_ref[rs, :], hp_v)[:3]
            for c in range(DN_BLOCK // DN_CHUNK):
                gl_ref[j * (DN_BLOCK // DN_CHUNK) + c] = jnp.broadcast_to(gl_t[c * DN_CHUNK:c * DN_CHUNK + 1, :], (8, 128))
            for h in range(DN_HEADS):
                sl = slice(h * DN_DIM, (h + 1) * DN_DIM)
                gate = slice(DN_HEADS + h, DN_HEADS + h + 1)
                bc = beta_t[:, h:h + 1]
                f = _dn_local(mk, q_ref[rs, sl], k_ref[rs, sl], bc, gcum_t[:, gate], gl_t[:, gate])
                for c in range(DN_BLOCK // DN_CHUNK):
                    cr = slice(c * DN_CHUNK, (c + 1) * DN_CHUNK)
                    p_ref[j * DN_BLOCK + c * DN_CHUNK:j * DN_BLOCK + (c + 1) * DN_CHUNK, h * DN_CHUNK:(h + 1) * DN_CHUNK] = _bf(f["attn"][cr, cr])
                qd_ref[rs, sl] = _bf(f["qd"])
                kt_ref[rs, sl] = _bf(f["kt"])
                chains.append((rs, sl, bc, f))
        invs = _unit_lower_inverses([jnp.where(mk.strict_lower, f["mkk"] * f["dec"], 0.0) for _, _, _, f in chains], mk.eye)
        for (rs, sl, bc, f), inv in zip(chains, invs):
            inv_ref[rs, sl] = inv
            sol = _dot_mid(inv, jnp.concatenate([v_ref[rs, sl] * bc, f["kb"] * f["eg"]], axis=1))
            u_ref[rs, sl] = sol[:, :DN_DIM]
            w_ref[rs, sl] = _bf(sol[:, DN_DIM:])

    sp = _dn_specs(rows)
    tok16 = jax.ShapeDtypeStruct((T, hw), BF16)
    return _call_with_exchange(
        body, exchange, name=name, steps=n_chunks // group,
        in_specs=sp["qkv"] + [sp["ba"], sp["hp"]],
        out_specs=[sp["tok"], sp["tok"], sp["attn"], sp["tok"], sp["tok"], sp["gate"], sp["tok"]],
        out_shape=[jax.ShapeDtypeStruct((T, hw), F32), tok16, jax.ShapeDtypeStruct((T, DN_HEADS * DN_CHUNK), BF16),
                   tok16, tok16, jax.ShapeDtypeStruct((n_chunks, 8, 128), F32), jax.ShapeDtypeStruct((T, hw), F32)],
        args=(qkv, qkv, qkv, proj, hp))


def _dn_scan(u, w, p, qd, kt, gl, name, exchange=None):
    T = u.shape[0]
    n_chunks = T // DN_CHUNK
    group = min(DN_SCAN_CHUNKS, n_chunks)
    rows = group * DN_CHUNK
    hw = DN_HEADS * DN_DIM

    def body(u_ref, w_ref, p_ref, qd_ref, kt_ref, gl_ref, o_ref, vn_ref, sall_ref, s_s):
        @pl.when(pl.program_id(0) == 0)
        def _():
            s_s[...] = jnp.zeros_like(s_s)

        state = [s_s[h] for h in range(DN_HEADS)]
        for j in range(group):
            rs = slice(j * DN_CHUNK, (j + 1) * DN_CHUNK)
            for h in range(DN_HEADS):
                sl = slice(h * DN_DIM, (h + 1) * DN_DIM)
                sall_ref[j, h] = state[h]
                sb = _bf(state[h])
                vnb = _bf(u_ref[rs, sl] - _dot(w_ref[rs, sl], sb))
                vn_ref[rs, sl] = vnb
                o_ref[rs, sl] = _dot(qd_ref[rs, sl], sb) + _dot(p_ref[rs, h * DN_CHUNK:(h + 1) * DN_CHUNK], vnb)
                egl = jnp.exp(gl_ref[j, 0:1, DN_HEADS + h:DN_HEADS + h + 1])
                state[h] = state[h] * egl + _dot_tn(kt_ref[rs, sl], vnb)
        for h in range(DN_HEADS):
            s_s[h] = state[h]

    sp = _dn_specs(rows)
    return _call_with_exchange(
        body, exchange, name=name, steps=n_chunks // group,
        in_specs=[sp["tok"], sp["tok"], sp["attn"], sp["tok"], sp["tok"], sp["gate"]],
        out_specs=[sp["tok"], sp["tok"], sp["state"]],
        out_shape=[jax.ShapeDtypeStruct((T, hw), F32), jax.ShapeDtypeStruct((T, hw), BF16),
                   jax.ShapeDtypeStruct((n_chunks, DN_HEADS, DN_DIM, DN_DIM), F32)],
        scratch_shapes=[pltpu.VMEM((DN_HEADS, DN_DIM, DN_DIM), F32)],
        args=(u, w, p, qd, kt, gl))


def _dn_scan_bwd(w, p, qd, kt, gl, vn, sall, do, name, exchange=None):
    T = w.shape[0]
    n_chunks = T // DN_CHUNK
    group = min(DN_SCAN_CHUNKS, n_chunks)
    rows = group * DN_CHUNK
    hw = DN_HEADS * DN_DIM
    last = n_chunks // group - 1

    def body(w_ref, p_ref, qd_ref, kt_ref, gl_ref, vn_ref, sall_ref, do_ref, dvn_ref, dkt_ref, dgl_ref, ds_s):
        @pl.when(pl.program_id(0) == 0)
        def _():
            ds_s[...] = jnp.zeros_like(ds_s)

        lane = lax.broadcasted_iota(jnp.int32, (8, 128), 1)
        d_state = [ds_s[h] for h in range(DN_HEADS)]
        for j in reversed(range(group)):
            rs = slice(j * DN_CHUNK, (j + 1) * DN_CHUNK)
            dgl_tile = jnp.zeros((8, 128), F32)
            for h in range(DN_HEADS):
                sl = slice(h * DN_DIM, (h + 1) * DN_DIM)
                d_out = _bf(do_ref[rs, sl])
                d_new = d_state[h]
                d_newb = _bf(d_new)
                d_vn = _dot_tn(p_ref[rs, h * DN_CHUNK:(h + 1) * DN_CHUNK], d_out) + _dot(kt_ref[rs, sl], d_newb)
                dvn_ref[rs, sl] = d_vn
                dkt_ref[rs, sl] = _dot_nt(vn_ref[rs, sl], d_newb)
                egl = jnp.exp(gl_ref[j, 0:1, DN_HEADS + h:DN_HEADS + h + 1])
                prod = jnp.sum(d_new * sall_ref[j, h], axis=1, keepdims=True)
                dgl_tile = jnp.where(lane == DN_HEADS + h, jnp.sum(prod, axis=0, keepdims=True) * egl, dgl_tile)
                d_state[h] = d_new * egl + _dot_tn(qd_ref[rs, sl], d_out) - _dot_tn(w_ref[rs, sl], _bf(d_vn))
            dgl_ref[j] = dgl_tile
        for h in range(DN_HEADS):
            ds_s[h] = d_state[h]

    sp = _dn_specs(rows, rev=lambda n: last - n)
    return _call_with_exchange(
        body, exchange, name=name, steps=n_chunks // group,
        in_specs=[sp["tok"], sp["attn"], sp["tok"], sp["tok"], sp["gate"], sp["tok"], sp["state"], sp["tok"]],
        out_specs=[sp["tok"], sp["tok"], sp["gate"]],
        out_shape=[jax.ShapeDtypeStruct((T, hw), F32), jax.ShapeDtypeStruct((T, hw), F32),
                   jax.ShapeDtypeStruct((n_chunks, 8, 128), F32)],
        scratch_shapes=[pltpu.VMEM((DN_HEADS, DN_DIM, DN_DIM), F32)],
        args=(w, p, qd, kt, gl, vn, sall, do))


def _dn_prep_bwd(qkv, proj, hp, sall, vn, do, dvn, dkt, dgl, inv, u, w, name, exchange=None):
    T = qkv.shape[0]
    n_chunks = T // DN_CHUNK
    blocks = max(1, min(DN_PREP_CHUNKS, n_chunks) * DN_CHUNK // DN_BLOCK)
    per_block = DN_BLOCK // DN_CHUNK
    group = blocks * per_block
    rows = blocks * DN_BLOCK
    hw = DN_HEADS * DN_DIM
    first_rows, second_rows = slice(0, DN_CHUNK), slice(DN_CHUNK, DN_BLOCK)

    def rowsum(x):
        return jnp.sum(x, axis=1, keepdims=True)

    def by_chunk(x, s0, s1, fn):
        return jnp.concatenate([fn(x[first_rows], s0), fn(x[second_rows], s1)], axis=0)

    def body(q_ref, k_ref, v_ref, ba_ref, hp_ref, sall_ref, vn_ref, do_ref, dvn_ref, dkt_ref, dgl_ref,
             inv_ref, u_ref, w_ref, dqkv_ref, dba_ref, dhp_ref):
        @pl.when(pl.program_id(0) == 0)
        def _():
            dhp_ref[...] = jnp.zeros_like(dhp_ref)

        mk = _BlockMasks()
        hp_v = hp_ref[...]
        chains = []
        for j in range(blocks):
            rs = slice(j * DN_BLOCK, (j + 1) * DN_BLOCK)
            chains.append(one_block(mk, hp_v, *(r.at[rs, :] for r in (q_ref, k_ref, v_ref, ba_ref)),
                                    sall_ref.at[pl.ds(j * per_block, per_block)],
                                    *(r.at[rs, :] for r in (vn_ref, do_ref, dvn_ref, dkt_ref)),
                                    dgl_ref.at[pl.ds(j * per_block, per_block)],
                                    *(r.at[rs, :] for r in (inv_ref, u_ref, w_ref)),
                                    *(dqkv_ref.at[rs, pl.ds(i * hw, hw)] for i in range(3)), dba_ref.at[rs, :]))
        total = jnp.zeros((8, 128), F32)
        for part in _run_interleaved(chains):
            total = total + part
        dhp_ref[...] += total

    def one_block(mk, hp_v, q_ref, k_ref, v_ref, ba_ref, state_ref, vn_ref, do_ref, dvn_ref, dkt_ref, dgl_ref,
                  inv_ref, u_ref, w_ref, dq_ref, dk_ref, dv_ref, dba_ref):
        ba = ba_ref[...]
        beta_t, gcum_t, gl_t, graw_t, coef, pre = _dn_block_gates(mk, ba, hp_v)
        lane = lax.broadcasted_iota(jnp.int32, (DN_BLOCK, 128), 1)
        rowi = lax.broadcasted_iota(jnp.int32, (DN_BLOCK, 1), 0)

        def head(h):
            sl = slice(h * DN_DIM, (h + 1) * DN_DIM)
            gate = slice(DN_HEADS + h, DN_HEADS + h + 1)
            gc = gcum_t[:, gate]
            bc = beta_t[:, h:h + 1]
            sb0, sb1 = _bf(state_ref[0, h]), _bf(state_ref[1, h])
            vh = v_ref[:, sl]
            f = _dn_local(mk, q_ref[:, sl], k_ref[:, sl], bc, gc, gl_t[:, gate])
            yield
            qh, kh, kb, dec, eg, etl = f["qh"], f["kh"], f["kb"], f["dec"], f["eg"], f["etl"]
            qd, kt = f["qd"], f["kt"]
            qb, kbf, kbb = _bf(qh), _bf(kh), _bf(kb)
            dec_t = jnp.where(mk.upper, jnp.exp(jnp.where(mk.upper, f["gr"] - gc, 0.0)), 0.0)
            mkk_t = f["mkk"].T
            inv_t = inv_ref[:, sl].T
            mqk_t = f["mqk"].T

            d_out = _bf(do_ref[:, sl])
            vnb = vn_ref[:, sl]
            d_qd = by_chunk(d_out, sb0, sb1, _dot_nt)
            d_attn = _dot_nt(d_out, vnb)
            d_attn_t = _dot_nt(vnb, d_out)
            d_vn = dvn_ref[:, sl]
            d_kt = dkt_ref[:, sl]
            d_w = -by_chunk(_bf(d_vn), sb0, sb1, _dot_nt)
            yield
            d_rhs = _dot_mid(inv_t, jnp.concatenate([d_vn, d_w], axis=1))
            yield
            d_bu, d_bw = d_rhs[:, :DN_DIM], d_rhs[:, DN_DIM:]
            ub, wb, d_bub, d_bwb = _bf(u_ref[:, sl]), w_ref[:, sl], _bf(d_bu), _bf(d_bw)
            d_low = -(_dot_nt(d_bub, ub) + _dot_nt(d_bwb, wb))
            d_low_t = -(_dot_nt(ub, d_bub) + _dot_nt(wb, d_bwb))
            yield
            d_mkk = jnp.where(mk.strict_lower, d_low * dec, 0.0)
            d_mkk_t = jnp.where(mk.strict_upper, d_low_t * dec_t, 0.0)
            d_mqk = jnp.where(mk.lower, d_attn * dec, 0.0)
            d_mqk_t = jnp.where(mk.upper, d_attn_t * dec_t, 0.0)
            bw = kb * eg
            d_kb = _dot(_bf(d_mkk), kbf) + d_bw * eg
            d_k = _dot(_bf(d_mkk_t), kbb) + _dot(_bf(d_mqk_t), qb) + d_kt * etl + d_kb * bc
            d_q = _dot(_bf(d_mqk), kbf) + d_qd * eg
            yield
            d_beta = rowsum(d_kb * kh) + rowsum(d_bu * vh)
            dv_ref[:, sl] = d_bu * bc
            e_mat = d_mkk * f["mkk"] + d_mqk * f["mqk"]
            e_mat_t = d_mkk_t * mkk_t + d_mqk_t * mqk_t
            kt_term = rowsum(d_kt * kt)
            d_g = rowsum(e_mat) - rowsum(e_mat_t) + rowsum(d_qd * qd) + rowsum(d_bw * bw) - kt_term
            for c, chunk_rows in enumerate((mk.first, ~mk.first)):
                d_glast = dgl_ref[c, 0:1, gate] + jnp.sum(jnp.where(chunk_rows, kt_term, 0.0), axis=0, keepdims=True)
                d_g = d_g + jnp.where(rowi == (c + 1) * DN_CHUNK - 1, d_glast, 0.0)
            qn = f["qn"]
            d_qs = d_q * (DN_DIM ** -0.5)
            dq_ref[:, sl] = f["rq"] * (d_qs - qn * rowsum(d_qs * qn))
            dk_ref[:, sl] = f["rk"] * (d_k - kh * rowsum(d_k * kh))
            return d_g, d_beta

        per_head = yield from _interleave([head(h) for h in range(DN_HEADS)])
        dgcum_t = jnp.zeros((DN_BLOCK, 128), F32)
        dbeta_t = jnp.zeros((DN_BLOCK, 128), F32)
        for h, (d_g, d_beta) in enumerate(per_head):
            dgcum_t = jnp.where(lane == DN_HEADS + h, d_g, dgcum_t)
            dbeta_t = jnp.where(lane == h, d_beta, dbeta_t)
        dgraw_t = _dot_hi(mk.upper.astype(F32), dgcum_t)
        sp = _sigmoid(pre)
        d_pre = dgraw_t * coef * sp
        dba_ref[...] = jnp.where(lane < DN_HEADS, dbeta_t * beta_t * (1.0 - beta_t),
                                 jnp.where(lane < 2 * DN_HEADS, d_pre, 0.0))
        in_g = (lane >= DN_HEADS) & (lane < 2 * DN_HEADS)
        d_alog = jnp.sum(jnp.where(in_g, dgraw_t * graw_t, 0.0), axis=0, keepdims=True)
        d_dtb = jnp.sum(jnp.where(in_g, d_pre, 0.0), axis=0, keepdims=True)
        return jnp.concatenate([d_alog, d_dtb, jnp.zeros((6, 128), F32)], axis=0)

    sp = _dn_specs(rows)
    return _call_with_exchange(
        body, exchange, name=name, steps=n_chunks // group,
        in_specs=sp["qkv"] + [sp["ba"], sp["hp"], sp["state"]] + [sp["tok"]] * 4 + [sp["gate"]] + [sp["tok"]] * 3,
        out_specs=[pl.BlockSpec((rows, 3 * hw), lambda n: (n, 0)), pl.BlockSpec((rows, 128), lambda n: (n, 0)), sp["hp"]],
        out_shape=[jax.ShapeDtypeStruct((T, 3 * hw), F32), jax.ShapeDtypeStruct((T, 128), F32),
                   jax.ShapeDtypeStruct((8, 128), F32)],
        args=(qkv, qkv, qkv, proj, hp, sall, vn, do, dvn, dkt, dgl, inv, u, w))


def _mix_fwd(o, proj, dn_norm, sg_norm, sg_w, sg_bt, name):
    T = o.shape[0]
    tm = _tile(T, 512)
    hw = DN_HEADS * DN_DIM
    nc = tm // SG_CHUNK

    def body(o_ref, z_ref, su_ref, sv_ref, dnn_ref, sgn_ref, sgw_ref, sgb_ref, mix_ref):
        dnn = dnn_ref[...]
        for h in range(DN_HEADS):
            sl = slice(h * DN_DIM, (h + 1) * DN_DIM)
            xhat, _ = _rms_stats(o_ref[:, sl])
            z = z_ref[:, sl]
            mix_ref[:, sl] = _bf(xhat * dnn * (z * _sigmoid(z)))
        tri = lax.broadcasted_iota(jnp.int32, (SG_CHUNK, SG_CHUNK), 0) >= lax.broadcasted_iota(jnp.int32, (SG_CHUNK, SG_CHUNK), 1)
        for g in range(SG_GROUPS):
            sl = slice(g * SG_DIM, (g + 1) * SG_DIM)
            xhat, _ = _rms_stats(_gelu(sv_ref[:, sl]))
            svn = _bf(xhat * sgn_ref[g:g + 1, :])
            sua = _gelu(su_ref[:, sl])
            wt = _bf(jnp.where(tri, sgw_ref[g], 0.0))
            bias = sgb_ref[:, g:g + 1]
            for c in range(nc):
                rows = slice(c * SG_CHUNK, (c + 1) * SG_CHUNK)
                mixed = _dot(wt, svn[rows, :]) + bias
                mix_ref[rows, hw + g * SG_DIM:hw + (g + 1) * SG_DIM] = _bf(sua[rows, :] * mixed)

    full = lambda shape: pl.BlockSpec(shape, lambda i: (0,) * len(shape))
    return pl.pallas_call(
        body, name=name, grid=(T // tm,),
        in_specs=[pl.BlockSpec((tm, hw), lambda i: (i, 0)),
                  pl.BlockSpec((tm, hw), lambda i: (i, 3)),
                  pl.BlockSpec((tm, hw), lambda i: (i, 4)),
                  pl.BlockSpec((tm, hw), lambda i: (i, 5)),
                  full((1, DN_DIM)), full((SG_GROUPS, SG_DIM)), full((SG_GROUPS, SG_CHUNK, SG_CHUNK)),
                  full((SG_CHUNK, 128))],
        out_specs=pl.BlockSpec((tm, 2 * hw), lambda i: (i, 0)),
        out_shape=jax.ShapeDtypeStruct((T, 2 * hw), BF16),
        compiler_params=_params(("parallel",)),
    )(o, proj, proj, proj, dn_norm, sg_norm, sg_w, sg_bt)


def _mix_bwd(dmix, o, proj, dn_norm, sg_norm, sg_w, sg_bt, name):
    T = o.shape[0]
    tm = _tile(T, 512)
    hw = DN_HEADS * DN_DIM
    nc = tm // SG_CHUNK

    def body(dm_ref, o_ref, z_ref, su_ref, sv_ref, dnn_ref, sgn_ref, sgw_ref, sgb_ref,
             do_ref, dz_ref, ddnn_ref, dsgn_ref, dsgw_ref, dsgb_ref):
        @pl.when(pl.program_id(0) == 0)
        def _():
            ddnn_ref[...] = jnp.zeros_like(ddnn_ref)
            dsgn_ref[...] = jnp.zeros_like(dsgn_ref)
            dsgw_ref[...] = jnp.zeros_like(dsgw_ref)
            dsgb_ref[...] = jnp.zeros_like(dsgb_ref)

        dnn = dnn_ref[...]
        ddnn = jnp.zeros((1, DN_DIM), F32)
        for h in range(DN_HEADS):
            sl = slice(h * DN_DIM, (h + 1) * DN_DIM)
            xhat, r = _rms_stats(o_ref[:, sl])
            z = z_ref[:, sl]
            sz = _sigmoid(z)
            doa = dm_ref[:, sl]
            dyn = doa * (z * sz)
            dz_ref[:, sl] = _bf(doa * xhat * dnn * _silu_grad(z, sz))
            do_ref[:, sl] = _rms_bwd(dyn, xhat, r, dnn)
            ddnn = ddnn + jnp.sum(dyn * xhat, axis=0, keepdims=True)
        ddnn_ref[...] += ddnn
        tri = lax.broadcasted_iota(jnp.int32, (SG_CHUNK, SG_CHUNK), 0) >= lax.broadcasted_iota(jnp.int32, (SG_CHUNK, SG_CHUNK), 1)
        lane = lax.broadcasted_iota(jnp.int32, (SG_CHUNK, 128), 1)
        dsgb = jnp.zeros((SG_CHUNK, 128), F32)
        dsgn_rows = []
        for g in range(SG_GROUPS):
            sl = slice(g * SG_DIM, (g + 1) * SG_DIM)
            sv = sv_ref[:, sl]
            su = su_ref[:, sl]
            xhat, r = _rms_stats(_gelu(sv))
            sgn = sgn_ref[g:g + 1, :]
            svn = _bf(xhat * sgn)
            sua = _gelu(su)
            wt = _bf(jnp.where(tri, sgw_ref[g], 0.0))
            bias = sgb_ref[:, g:g + 1]
            dw = jnp.zeros((SG_CHUNK, SG_CHUNK), F32)
            db = jnp.zeros((SG_CHUNK, 1), F32)
            dsua, dsvn = [], []
            for c in range(nc):
                rows = slice(c * SG_CHUNK, (c + 1) * SG_CHUNK)
                mixed = _dot(wt, svn[rows, :]) + bias
                dob = dm_ref[rows, hw + g * SG_DIM:hw + (g + 1) * SG_DIM]
                dsua.append(dob * mixed)
                dmixed = dob * sua[rows, :]
                dmb = _bf(dmixed)
                dsvn.append(_dot_tn(wt, dmb))
                dw = dw + _dot_nt(dmb, svn[rows, :])
                db = db + jnp.sum(dmixed, axis=1, keepdims=True)
            dsua = jnp.concatenate(dsua, axis=0) if nc > 1 else dsua[0]
            dsvn = jnp.concatenate(dsvn, axis=0) if nc > 1 else dsvn[0]
            dz_ref[:, hw + g * SG_DIM:hw + (g + 1) * SG_DIM] = _bf(dsua * _gelu_grad(su))
            dz_ref[:, 2 * hw + g * SG_DIM:2 * hw + (g + 1) * SG_DIM] = _bf(_rms_bwd(dsvn, xhat, r, sgn) * _gelu_grad(sv))
            dsgn_rows.append(jnp.sum(dsvn * xhat, axis=0, keepdims=True))
            dsgw_ref[g] += jnp.where(tri, dw, 0.0)
            dsgb = jnp.where(lane == g, db, dsgb)
        dsgn_ref[...] += jnp.concatenate(dsgn_rows, axis=0)
        dsgb_ref[...] += dsgb

    full = lambda shape: pl.BlockSpec(shape, lambda i: (0,) * len(shape))
    return pl.pallas_call(
        body, name=name, grid=(T // tm,),
        in_specs=[pl.BlockSpec((tm, 2 * hw), lambda i: (i, 0)),
                  pl.BlockSpec((tm, hw), lambda i: (i, 0)),
                  pl.BlockSpec((tm, hw), lambda i: (i, 3)),
                  pl.BlockSpec((tm, hw), lambda i: (i, 4)),
                  pl.BlockSpec((tm, hw), lambda i: (i, 5)),
                  full((1, DN_DIM)), full((SG_GROUPS, SG_DIM)), full((SG_GROUPS, SG_CHUNK, SG_CHUNK)),
                  full((SG_CHUNK, 128))],
        out_specs=[pl.BlockSpec((tm, hw), lambda i: (i, 0)),
                   pl.BlockSpec((tm, 3 * hw), lambda i: (i, 0)),
                   full((1, DN_DIM)), full((SG_GROUPS, SG_DIM)), full((SG_GROUPS, SG_CHUNK, SG_CHUNK)),
                   full((SG_CHUNK, 128))],
        out_shape=[jax.ShapeDtypeStruct((T, hw), F32), jax.ShapeDtypeStruct((T, 3 * hw), BF16),
                   jax.ShapeDtypeStruct((1, DN_DIM), F32), jax.ShapeDtypeStruct((SG_GROUPS, SG_DIM), F32),
                   jax.ShapeDtypeStruct((SG_GROUPS, SG_CHUNK, SG_CHUNK), F32),
                   jax.ShapeDtypeStruct((SG_CHUNK, 128), F32)],
        compiler_params=_params(("arbitrary",)),
    )(dmix, o, proj, proj, proj, dn_norm, sg_norm, sg_w, sg_bt)


def _window_sums(h, sign):
    sums, s, w = {}, h, 1
    while w < POOL_WINDOWS[-1]:
        s = s + _shift_rows(s, sign * w)
        w *= 2
        sums[w] = s
    return sums


def _pool_counts(t_global):
    return [jnp.minimum(t_global + 1, win).astype(F32) for win in POOL_WINDOWS]


def _pooled_groups(ext_h, row0, tm):
    sums = _window_sums(ext_h, 1)
    t_global = row0 + lax.broadcasted_iota(jnp.int32, (tm, 1), 0)
    counts = _pool_counts(t_global)
    out = []
    for gi, win in enumerate(POOL_WINDOWS):
        cols = slice(gi * POOL_DIM, (gi + 1) * POOL_DIM)
        out.append(sums[win][POOL_HALO:, cols] / counts[gi] - ext_h[POOL_HALO:, cols])
    return out


def _pool_fwd(x, nw, pool_w, pool_scale, layer, name):
    T, D = x.shape
    tm = _tile(T, 256)
    hb = tm // POOL_HALO

    def body(x_ref, xp_ref, n_ref, w_ref, s_ref, xo_ref):
        i = pl.program_id(0)
        prev = jnp.where(i == 0, 0.0, xp_ref[...])
        ext = jnp.concatenate([prev, x_ref[...]], axis=0)
        xhat, _ = _rms_stats(ext)
        pooled = _pooled_groups(xhat * n_ref[...], i * tm, tm)
        for gi in range(len(POOL_WINDOWS)):
            cols = slice(gi * POOL_DIM, (gi + 1) * POOL_DIM)
            xo_ref[:, cols] = x_ref[:, cols] + _dot(_bf(pooled[gi]), w_ref[gi]) * s_ref[:, cols]

    return pl.pallas_call(
        body, name=name, grid=(T // tm,),
        in_specs=[pl.BlockSpec((tm, D), lambda i: (i, 0)),
                  pl.BlockSpec((POOL_HALO, D), lambda i: (jnp.maximum(i * hb - 1, 0), 0)),
                  pl.BlockSpec((None, 1, D), lambda i: (layer, 0, 0)),
                  pl.BlockSpec(pool_w.shape, lambda i: (0, 0, 0)),
                  pl.BlockSpec((1, D), lambda i: (0, 0))],
        out_specs=pl.BlockSpec((tm, D), lambda i: (i, 0)),
        out_shape=jax.ShapeDtypeStruct((T, D), F32),
        compiler_params=_params(("parallel",)),
    )(x, x, nw, pool_w, pool_scale)


def _pool_bwd(dxo, x, nw, pool_w, pool_scale, layer, name):
    T, D = x.shape
    tm = _tile(T, 256)
    hb = tm // POOL_HALO
    nt = T // tm
    ng = len(POOL_WINDOWS)

    def body(dxo_ref, dxn_ref, x_ref, xp_ref, n_ref, w_ref, s_ref, dx_ref, dw_ref, ds_ref, dn_ref):
        i = pl.program_id(0)

        @pl.when(i == 0)
        def _():
            dw_ref[...] = jnp.zeros_like(dw_ref)
            ds_ref[...] = jnp.zeros_like(ds_ref)
            dn_ref[...] = jnp.zeros_like(dn_ref)

        prev = jnp.where(i == 0, 0.0, xp_ref[...])
        ext = jnp.concatenate([prev, x_ref[...]], axis=0)
        xhat_ext, r_ext = _rms_stats(ext)
        nv = n_ref[...]
        pooled = _pooled_groups(xhat_ext * nv, i * tm, tm)
        dxo = dxo_ref[...]
        scale = s_ref[...]
        dout_ext = jnp.concatenate([dxo, jnp.where(i == nt - 1, 0.0, dxn_ref[...])], axis=0) * scale
        t_ext = i * tm + lax.broadcasted_iota(jnp.int32, (tm + POOL_HALO, 1), 0)
        counts = _pool_counts(t_ext)
        dh_cols, ds_cols = [], []
        for gi, win in enumerate(POOL_WINDOWS):
            cols = slice(gi * POOL_DIM, (gi + 1) * POOL_DIM)
            wg = w_ref[gi]
            pb = _bf(pooled[gi])
            doutb = _bf(dout_ext[:, cols])
            dpooled = _dot_nt(doutb, wg)
            ahead = _window_sums(dpooled / counts[gi], -1)[win]
            dh_cols.append(ahead[:tm, :] - dpooled[:tm, :])
            dw_ref[gi] += _dot_tn(pb, doutb[:tm, :])
            ds_cols.append(jnp.sum(dxo[:, cols] * _dot(pb, wg), axis=0, keepdims=True))
        dh = jnp.concatenate(dh_cols, axis=1)
        xhat, r = xhat_ext[POOL_HALO:, :], r_ext[POOL_HALO:, :]
        dx_ref[...] = dxo + _rms_bwd(dh, xhat, r, nv)
        dn_ref[...] += jnp.sum(dh * xhat, axis=0, keepdims=True)
        ds_ref[...] += jnp.concatenate(ds_cols, axis=1)

    last_halo = T // POOL_HALO - 1
    return pl.pallas_call(
        body, name=name, grid=(nt,),
        in_specs=[pl.BlockSpec((tm, D), lambda i: (i, 0)),
                  pl.BlockSpec((POOL_HALO, D), lambda i: (jnp.minimum((i + 1) * hb, last_halo), 0)),
                  pl.BlockSpec((tm, D), lambda i: (i, 0)),
                  pl.BlockSpec((POOL_HALO, D), lambda i: (jnp.maximum(i * hb - 1, 0), 0)),
                  pl.BlockSpec((None, 1, D), lambda i: (layer, 0, 0)),
                  pl.BlockSpec(pool_w.shape, lambda i: (0, 0, 0)),
                  pl.BlockSpec((1, D), lambda i: (0, 0))],
        out_specs=[pl.BlockSpec((tm, D), lambda i: (i, 0)),
                   pl.BlockSpec((ng, POOL_DIM, POOL_DIM), lambda i: (0, 0, 0)),
                   pl.BlockSpec((1, D), lambda i: (0, 0)),
                   pl.BlockSpec((1, D), lambda i: (0, 0))],
        out_shape=[jax.ShapeDtypeStruct((T, D), F32), jax.ShapeDtypeStruct((ng, POOL_DIM, POOL_DIM), F32),
                   jax.ShapeDtypeStruct((1, D), F32), jax.ShapeDtypeStruct((1, D), F32)],
        compiler_params=_params(("arbitrary",)),
    )(dxo, dxo, x, x, nw, pool_w, pool_scale)


def _loss_head(x, target, fn, name):
    T, D = x.shape
    tm = _tile(T, 512)

    def body(x_ref, t_ref, n_ref, loss_ref, dx_ref, dn_ref):
        @pl.when(pl.program_id(0) == 0)
        def _():
            loss_ref[...] = jnp.zeros_like(loss_ref)
            dn_ref[...] = jnp.zeros_like(dn_ref)

        xhat, r = _rms_stats(x_ref[...])
        nv = n_ref[...]
        err = xhat * nv - t_ref[...]
        part = jnp.sum(jnp.sum(err * err, axis=1, keepdims=True), axis=0, keepdims=True)
        loss_ref[...] += 0.5 * part / D
        dy = err / D
        dx_ref[...] = _rms_bwd(dy, xhat, r, nv)
        dn_ref[...] += jnp.sum(dy * xhat, axis=0, keepdims=True)

    row = pl.BlockSpec((tm, D), lambda i: (i, 0))
    return pl.pallas_call(
        body, name=name, grid=(T // tm,),
        in_specs=[row, row, pl.BlockSpec((1, D), lambda i: (0, 0))],
        out_specs=[pl.BlockSpec((1, 1), lambda i: (0, 0)), row, pl.BlockSpec((1, D), lambda i: (0, 0))],
        out_shape=[jax.ShapeDtypeStruct((1, 1), F32), jax.ShapeDtypeStruct((T, D), F32),
                   jax.ShapeDtypeStruct((1, D), F32)],
        compiler_params=_params(("arbitrary",)),
    )(x, target, fn)


def _adamw(w, g, m, v, name):
    R, C = w.shape
    br = R
    for cand in (512, 256, 128, 64, 32, 16, 8):
        if R % cand == 0 and cand * C * 4 <= 2 * 1024 * 1024:
            br = cand
            break

    def body(w_ref, g_ref, m_ref, v_ref, d_ref, mo_ref, vo_ref):
        gv = g_ref[...]
        m_new = ADAM_B1 * m_ref[...] + (1.0 - ADAM_B1) * gv
        v_new = ADAM_B2 * v_ref[...] + (1.0 - ADAM_B2) * (gv * gv)
        m_hat = m_new / (1.0 - ADAM_B1 ** ADAM_STEP)
        v_hat = v_new / (1.0 - ADAM_B2 ** ADAM_STEP)
        d_ref[...] = -ADAM_LR * (m_hat / (jnp.sqrt(v_hat) + ADAM_EPS) + ADAM_WD * w_ref[...])
        mo_ref[...] = m_new
        vo_ref[...] = v_new

    blk = pl.BlockSpec((br, C), lambda i: (i, 0))
    return pl.pallas_call(
        body, name=name, grid=(R // br,), in_specs=[blk] * 4, out_specs=[blk] * 3,
        out_shape=[jax.ShapeDtypeStruct((R, C), F32)] * 3,
        compiler_params=_params(("parallel",)),
    )(w, g, m, v)


def _mesh_pos():
    return lax.axis_index("x"), lax.axis_index("y"), lax.axis_index("c")


def _other_chips(x, y):
    return [(1 - x, y), (x, 1 - y), (1 - x, 1 - y)]


def _half_of(ref, shape, h):
    size = shape[0] // 2
    return ref.at[pl.ds(h * size, size)]


class _ChipGather:
    def __init__(self, shards, split):
        self.shards, self.split = list(shards), list(split)
        self.operands = self.shards
        n = len(self.shards)
        self.out_shape = [jax.ShapeDtypeStruct((N_CHIPS,) + s.shape, s.dtype) for s in self.shards]
        self.scratch = [pltpu.SemaphoreType.DMA((n, 3))] * 4

    def _piece(self, a, ref, h):
        return _half_of(ref, self.shards[a].shape, h) if self.split[a] else ref

    def start(self, ins, outs, sems):
        send_sems, recv_sems = sems[0], sems[1]
        x, y, c = _mesh_pos()
        me = 2 * x + y
        for a in range(len(ins)):
            for k, (px, py) in enumerate(_other_chips(x, y)):
                pltpu.make_async_remote_copy(self._piece(a, ins[a], c), self._piece(a, outs[a].at[me], c),
                                             send_sems.at[a, k], recv_sems.at[a, k],
                                             device_id=(px, py, c), device_id_type=MESH).start()

    def finish(self, ins, outs, sems):
        send_sems, recv_sems, fwd_send_sems, fwd_recv_sems = sems
        x, y, c = _mesh_pos()
        sibling = (x, y, 1 - c)
        chips = _other_chips(x, y)
        n = len(ins)
        forwards = []
        for a in range(n):
            for k, (px, py) in enumerate(chips):
                landed = self._piece(a, outs[a].at[2 * px + py], c)
                pltpu.make_async_remote_copy(landed, landed, send_sems.at[a, k], recv_sems.at[a, k],
                                             device_id=(px, py, c), device_id_type=MESH).wait_recv()
                if self.split[a]:
                    fwd = pltpu.make_async_remote_copy(landed, landed, fwd_send_sems.at[a, k], fwd_recv_sems.at[a, k],
                                                       device_id=sibling, device_id_type=MESH)
                    fwd.start()
                    forwards.append(fwd)
        for a in range(n):
            if self.split[a]:
                for k, (px, py) in enumerate(chips):
                    other = self._piece(a, outs[a].at[2 * px + py], 1 - c)
                    pltpu.make_async_remote_copy(other, other, fwd_send_sems.at[a, k], fwd_recv_sems.at[a, k],
                                                 device_id=sibling, device_id_type=MESH).wait_recv()
        for a in range(n):
            for k, (px, py) in enumerate(chips):
                sent = self._piece(a, ins[a], c)
                pltpu.make_async_remote_copy(sent, sent, send_sems.at[a, k], recv_sems.at[a, k],
                                             device_id=(px, py, c), device_id_type=MESH).wait_send()
        for fwd in forwards:
            fwd.wait_send()

    def finalize(self, gathered):
        x, y, _ = _mesh_pos()
        return [lax.dynamic_update_index_in_dim(g, s, 2 * x + y, 0) for g, s in zip(gathered, self.shards)]

    def run(self, name):
        n = len(self.shards)

        def body(*refs):
            ins, outs, sems = refs[:n], refs[n:2 * n], refs[2 * n:]
            self.start(ins, outs, sems)
            self.finish(ins, outs, sems)

        gathered = pl.pallas_call(
            body, name=name, in_specs=[ANY] * n, out_specs=[ANY] * n, out_shape=self.out_shape,
            scratch_shapes=self.scratch, compiler_params=pltpu.CompilerParams(has_side_effects=True),
        )(*self.shards)
        return self.finalize(gathered)


def _ffn_weight_grads(hb, dgu, a, dyb, tag):
    dwin = _matmul_tn(hb, dgu, D_MODEL, FF_CHUNK, f"{tag}_dw_in", stack_n=True)
    dwo = _matmul_tn(a, dyb, FF_CHUNK, D_MODEL, f"{tag}_dw_out")
    return dwin, dwo.reshape(N_CHIPS, D_FF // N_CHIPS, D_MODEL)


def _local_step(x, target, w, late=None, reduce=False):
    g = {}
    acts = []
    w = dict(w)

    def ffn_weights(which, layer):
        return w[f"n{which}"], w[f"win{which}_l{layer}"], w[f"wout{which}_l{layer}"]

    def hosting(name):
        exchange, layouts = late.get(name, (None, None)) if late else (None, None)
        return exchange, (lambda arrived: w.update(layouts(arrived)) if exchange is not None else None)

    def ffn(xin, which, layer):
        name = f"ffn{which}_l{layer}_fwd"
        exchange, keep = hosting(name)
        (xo, gv, uv, hb), arrived = _ffn_fwd(xin, *ffn_weights(which, layer), layer, name, exchange)
        keep(arrived)
        acts.append((xin, gv, uv, hb))
        return xo

    x1 = ffn(x, 1, 0)
    hb_mix = _rms_fwd_call(x1, w["nmix"], 0, "ab_norm_fwd")
    proj = _matmul(hb_mix, w["wp"], "ab_in_proj")
    qkv = _conv_fwd(proj, w["conv_w"], "dn_conv_fwd")
    exchange, keep = hosting("dn_prep")
    (dn_u, dn_w, dn_p, dn_qd, dn_kt, dn_gl, dn_inv), arrived = _dn_prep(qkv, proj, w["hp"], "dn_prep", exchange)
    keep(arrived)
    exchange, keep = hosting("dn_scan")
    (o, dn_vn, sall), arrived = _dn_scan(dn_u, dn_w, dn_p, dn_qd, dn_kt, dn_gl, "dn_scan", exchange)
    keep(arrived)
    mix = _mix_fwd(o, proj, w["dn_norm"], w["sg_norm"], w["sg_w"], w["sg_bt"], "ab_gate_fwd")
    x2 = _matmul(mix, w["wo"], "ab_out_proj", res=x1)
    x3 = ffn(x2, 2, 0)
    x4 = ffn(x3, 1, 1)
    x5 = _pool_fwd(x4, w["nmix"], w["pool_w"], w["pool_scale"], 1, "pool_fwd")
    x6 = ffn(x5, 2, 1)
    loss, dx, g["fn"] = _loss_head(x6, target, w["fn"], "loss_head")

    dn = {1: [None, None], 2: [None, None]}
    dwin = {1: [None, None], 2: [None, None]}
    dwout = {1: [None, None], 2: [None, None]}

    def ffn_back(dxo, which, layer, saved, exchange=None):
        nw, win, wout = ffn_weights(which, layer)
        xin, gv, uv, hb = saved
        tag = f"ffn{which}_l{layer}"
        (dxi, dgu, a, dyb, dnw), arrived = _ffn_bwd(dxo, xin, nw, gv, uv, win, wout, layer, f"{tag}_bwd", exchange)
        dn[which][layer] = dnw
        dwin[which][layer], dwout[which][layer] = _ffn_weight_grads(hb, dgu, a, dyb, tag)
        return dxi, arrived

    for which in (1, 2):
        g[f"win{which}"] = dwin[which]
        g[f"wout{which}"] = dwout[which]
    reduced = {}

    def open_round(tag, keys):
        have = _sharded_grads(g)
        return _GradRound(tag, {k: have[k] for k in keys})

    dx, _ = ffn_back(dx, 2, 1, acts[3])
    dx, g["pool_w"], g["pool_scale"], dnmix1 = _pool_bwd(dx, x4, w["nmix"], w["pool_w"], w["pool_scale"], 1, "pool_bwd")
    dx, _ = ffn_back(dx, 1, 1, acts[2])
    round_a = open_round("a", REDUCE_ROUNDS[0]) if reduce else None
    dx2, arrived = ffn_back(dx, 2, 0, acts[1], round_a.swap if reduce else None)
    if reduce:
        round_a.pair_sum(arrived)
    round_b = open_round("b", REDUCE_ROUNDS[1]) if reduce else None
    dmix = _matmul(dx2, w["wo"], "ab_out_proj_bwd", trans_b=True)
    g["wo"] = _matmul_tn(mix, dx2, D_MODEL, D_MODEL, "ab_out_proj_dw")
    do, dzuv, g["dn_norm"], g["sg_norm"], g["sg_w"], g["sg_bt"] = _mix_bwd(
        dmix, o, proj, w["dn_norm"], w["sg_norm"], w["sg_w"], w["sg_bt"], "ab_gate_bwd")
    (dvn, dkt, dgl), arrived = _dn_scan_bwd(dn_w, dn_p, dn_qd, dn_kt, dn_gl, dn_vn, sall, do, "dn_scan_bwd",
                                            round_b.swap if reduce else None)
    if reduce:
        round_b.pair_sum(arrived)
    (dqkv_act, dba, g["hp"]), arrived = _dn_prep_bwd(qkv, proj, w["hp"], sall, dn_vn, do, dvn, dkt, dgl, dn_inv, dn_u, dn_w,
                                                     "dn_prep_bwd", round_b.scatter if reduce else None)
    if reduce:
        reduced.update(round_b.finish(arrived))
    dqkv, g["conv_w"] = _conv_bwd(dqkv_act, proj, w["conv_w"], "dn_conv_bwd")
    dproj = jnp.concatenate([dqkv, dzuv, dba.astype(BF16)], axis=1)
    dh = _matmul(dproj, w["wp"], "ab_in_proj_bwd", trans_b=True)
    g["wp"] = _matmul_tn(hb_mix, dproj, D_MODEL, 640, "ab_in_proj_dw")
    dx1, dnmix0 = _rms_bwd_call(dh, x1, w["nmix"], dx2, 0, "ab_norm_bwd")
    dx0, arrived = ffn_back(dx1, 1, 0, acts[0], round_a.scatter if reduce else None)
    if reduce:
        reduced.update(round_a.finish(arrived))
        round_c = open_round("c", REDUCE_ROUNDS[2])
        round_c.pair_sum(round_c.swap.run("grad_c_pair_swap"))
        reduced.update(round_c.finish(round_c.scatter.run("grad_c_chip_scatter")))

    g["n1"] = jnp.concatenate(dn[1], axis=0)
    g["n2"] = jnp.concatenate(dn[2], axis=0)
    g["nmix"] = jnp.concatenate([dnmix0, dnmix1], axis=0)
    return loss, dx0, g, reduced


SHARDED = ("ffn1_w_in", "ffn1_w_out", "ffn2_w_in", "ffn2_w_out", "ab_w_in", "ab_w_out", "pool_w", "dn_conv_w", "pool_scale")
REPLICATED = ("ffn_norm1", "mix_norm", "ffn_norm2", "dn_a_log", "dn_dt_bias", "dn_out_norm", "sg_norm", "sg_w", "sg_b", "final_norm")
QKVZ = 4 * DN_HEADS * DN_DIM
N_GATES = 2 * DN_HEADS
IN_PROJ = QKVZ + N_GATES + 2 * SG_GROUPS * SG_DIM


def _shard_pieces(wts, keys):
    out = []
    for n, layer in keys:
        a = wts[n][0 if layer is None else layer]
        a = a[None] if a.ndim == 1 else a
        out.append(a.astype(BF16) if n in MATRICES else a)
    return out


def _replicated_layouts(rep):
    per_layer = lambda a: a.reshape(a.shape[0], 1, D_MODEL)
    w = {"n1": per_layer(rep["ffn_norm1"]), "nmix": per_layer(rep["mix_norm"]), "n2": per_layer(rep["ffn_norm2"])}
    hp = jnp.zeros((8, 128), F32)
    w["hp"] = hp.at[0, DN_HEADS:N_GATES].set(rep["dn_a_log"][0]).at[1, DN_HEADS:N_GATES].set(rep["dn_dt_bias"][0])
    w["dn_norm"] = rep["dn_out_norm"]
    w["sg_norm"] = rep["sg_norm"][0]
    w["sg_w"] = rep["sg_w"][0]
    w["sg_bt"] = jnp.zeros((SG_CHUNK, 128), F32).at[:, :SG_GROUPS].set(rep["sg_b"][0].T)
    w["fn"] = rep["final_norm"].reshape(1, D_MODEL)
    return w


def _layouts_from(gathered):
    w = {}
    for (n, layer), a in gathered.items():
        if n in ("ffn1_w_in", "ffn2_w_in"):
            w[f"win{n[3]}_l{layer}"] = a
        elif n in ("ffn1_w_out", "ffn2_w_out"):
            w[f"wout{n[3]}_l{layer}"] = a
        elif n == "ab_w_in":
            ab_in = jnp.transpose(a, (1, 0, 2)).reshape(D_MODEL, IN_PROJ)
            w["wp"] = jnp.concatenate([ab_in[:, :QKVZ], ab_in[:, QKVZ + N_GATES:], ab_in[:, QKVZ:QKVZ + N_GATES],
                                       jnp.zeros((D_MODEL, PROJ_W - IN_PROJ), ab_in.dtype)], axis=1)
        elif n == "dn_conv_w":
            w["conv_w"] = jnp.transpose(a, (1, 0, 2)).reshape(DN_CONV, 3 * DN_HEADS * DN_DIM)
        elif n == "ab_w_out":
            w["wo"] = a.reshape(D_MODEL, D_MODEL)
        elif n == "pool_w":
            w["pool_w"] = jnp.transpose(a, (1, 0, 2, 3)).reshape(len(POOL_WINDOWS), POOL_DIM, POOL_DIM)
        elif n == "pool_scale":
            w["pool_scale"] = a.reshape(1, D_MODEL)
    return w


def _sharded_grads(g):
    nw = len(POOL_WINDOWS)
    sharded = {}
    for n, key in (("ffn1_w_in", "win1"), ("ffn1_w_out", "wout1"), ("ffn2_w_in", "win2"), ("ffn2_w_out", "wout2")):
        for layer, a in enumerate(g.get(key, ())):
            if a is not None:
                sharded[(n, layer)] = a
    if "wp" in g:
        wp = g["wp"]
        ab_in = jnp.concatenate([wp[:, :QKVZ], wp[:, IN_PROJ - N_GATES:IN_PROJ], wp[:, QKVZ:IN_PROJ - N_GATES]], axis=1)
        sharded[("ab_w_in", None)] = jnp.transpose(ab_in.reshape(D_MODEL, N_CHIPS, IN_PROJ // N_CHIPS), (1, 0, 2))
    if "wo" in g:
        sharded[("ab_w_out", None)] = g["wo"].reshape(N_CHIPS, D_MODEL // N_CHIPS, D_MODEL)
    if "pool_w" in g:
        sharded[("pool_w", None)] = jnp.transpose(g["pool_w"].reshape(nw, N_CHIPS, POOL_DIM // N_CHIPS, POOL_DIM), (1, 0, 2, 3))
    if "conv_w" in g:
        sharded[("dn_conv_w", None)] = jnp.transpose(g["conv_w"].reshape(DN_CONV, N_CHIPS, -1), (1, 0, 2))
    if "pool_scale" in g:
        sharded[("pool_scale", None)] = g["pool_scale"].reshape(N_CHIPS, 1, D_MODEL // N_CHIPS)
    return sharded


def _replicated_grads(g):
    rep = {
        "ffn_norm1": g["n1"], "mix_norm": g["nmix"], "ffn_norm2": g["n2"],
        "dn_a_log": g["hp"][0:1, DN_HEADS:N_GATES], "dn_dt_bias": g["hp"][1:2, DN_HEADS:N_GATES],
        "dn_out_norm": g["dn_norm"], "sg_norm": g["sg_norm"][None], "sg_w": g["sg_w"][None],
        "sg_b": g["sg_bt"][:, :SG_GROUPS].T[None], "final_norm": g["fn"].reshape(D_MODEL),
    }
    return rep


def _as_halves(a):
    shape = a.shape[1:]
    if len(shape) >= 2 and shape[0] % 2 == 0:
        return a.reshape(N_CHIPS, 2, -1, shape[-1])
    return a.reshape(N_CHIPS, 2, 1, -1)


def _row_block(rows):
    for cand in (256, 176, 128, 64, 32, 16):
        if rows % cand == 0:
            return cand
    return rows


def _from_halves(mine, other, core, shape):
    both = jnp.stack([jnp.where(core == 0, mine, other), jnp.where(core == 0, other, mine)])
    return both.reshape(shape)


class _PairSwap:
    def __init__(self, packs):
        self.operands = list(packs)
        n = len(self.operands)
        self.out_shape = [jax.ShapeDtypeStruct((p.shape[0],) + p.shape[2:], p.dtype) for p in self.operands]
        self.scratch = [pltpu.SemaphoreType.DMA((n,))] * 2

    def _copies(self, ins, outs, sems):
        x, y, c = _mesh_pos()
        return [pltpu.make_async_remote_copy(ins[k].at[:, 1 - c], outs[k], sems[0].at[k], sems[1].at[k],
                                             device_id=(x, y, 1 - c), device_id_type=MESH) for k in range(len(ins))]

    def start(self, ins, outs, sems):
        for cp in self._copies(ins, outs, sems):
            cp.start()

    def finish(self, ins, outs, sems):
        for cp in self._copies(ins, outs, sems):
            cp.wait()

    def finalize(self, results):
        return results

    def run(self, name):
        n = len(self.operands)

        def body(*refs):
            ins, outs, sems = refs[:n], refs[n:2 * n], refs[2 * n:]
            self.start(ins, outs, sems)
            self.finish(ins, outs, sems)

        return pl.pallas_call(
            body, name=name, in_specs=[ANY] * n, out_specs=[ANY] * n, out_shape=self.out_shape, scratch_shapes=self.scratch,
            compiler_params=pltpu.CompilerParams(has_side_effects=True),
        )(*self.operands)


def _add_pair(pack, recv, core, name):
    nchip, _, rows, lanes = pack.shape
    rb = _row_block(rows)

    def body(c_ref, a_ref, b_ref, o32_ref, o16_ref):
        s = a_ref[...] + b_ref[...]
        o32_ref[...] = s
        o16_ref[...] = _bf(s)

    blk = pl.BlockSpec((None, rb, lanes), lambda p, i, c: (p, i, 0))
    return pl.pallas_call(
        body, name=name,
        grid_spec=pltpu.PrefetchScalarGridSpec(
            num_scalar_prefetch=1, grid=(nchip, rows // rb),
            in_specs=[pl.BlockSpec((None, None, rb, lanes), lambda p, i, c: (p, c[0], i, 0)), blk],
            out_specs=[blk, blk]),
        out_shape=[jax.ShapeDtypeStruct((nchip, rows, lanes), F32), jax.ShapeDtypeStruct((nchip, rows, lanes), BF16)],
        compiler_params=_params(("parallel", "parallel")),
    )(core, pack, recv)


class _ChipScatter:
    def __init__(self, parts16):
        self.operands = list(parts16)
        n = len(self.operands)
        self.out_shape = [jax.ShapeDtypeStruct((N_CHIPS - 1,) + p.shape[1:], p.dtype) for p in self.operands]
        self.scratch = [pltpu.SemaphoreType.DMA((n, N_CHIPS - 1))] * 2

    def _copies(self, ins, outs, sems):
        x, y, c = _mesh_pos()
        return [pltpu.make_async_remote_copy(ins[a].at[2 * px + py], outs[a].at[k], sems[0].at[a, k], sems[1].at[a, k],
                                             device_id=(px, py, c), device_id_type=MESH)
                for a in range(len(ins)) for k, (px, py) in enumerate(_other_chips(x, y))]

    def start(self, ins, outs, sems):
        for cp in self._copies(ins, outs, sems):
            cp.start()

    def finish(self, ins, outs, sems):
        for cp in self._copies(ins, outs, sems):
            cp.wait()

    def finalize(self, results):
        return results

    def run(self, name):
        n = len(self.operands)

        def body(*refs):
            ins, outs, sems = refs[:n], refs[n:2 * n], refs[2 * n:]
            self.start(ins, outs, sems)
            self.finish(ins, outs, sems)

        return pl.pallas_call(
            body, name=name, in_specs=[ANY] * n, out_specs=[ANY] * n, out_shape=self.out_shape, scratch_shapes=self.scratch,
            compiler_params=pltpu.CompilerParams(has_side_effects=True),
        )(*self.operands)


def _sum_chips(part32, recv16, chip, name):
    nchip, rows, lanes = part32.shape
    rb = _row_block(rows)

    def body(p_ref, own_ref, r_ref, o_ref):
        s = own_ref[...]
        for k in range(nchip - 1):
            s = s + r_ref[k].astype(F32)
        o_ref[...] = s

    return pl.pallas_call(
        body, name=name,
        grid_spec=pltpu.PrefetchScalarGridSpec(
            num_scalar_prefetch=1, grid=(rows // rb,),
            in_specs=[pl.BlockSpec((None, rb, lanes), lambda i, p: (p[0], i, 0)),
                      pl.BlockSpec((nchip - 1, rb, lanes), lambda i, p: (0, i, 0))],
            out_specs=pl.BlockSpec((rb, lanes), lambda i, p: (i, 0))),
        out_shape=jax.ShapeDtypeStruct((rows, lanes), F32),
        compiler_params=_params(("parallel",)),
    )(chip, part32, recv16)


def _share_with_sibling(halves, name):
    n = len(halves)

    def body(*refs):
        ins, outs, send_sems, recv_sems = refs[:n], refs[n:2 * n], refs[2 * n], refs[2 * n + 1]
        x, y, c = _mesh_pos()
        copies = [pltpu.make_async_remote_copy(ins[k], outs[k], send_sems.at[k], recv_sems.at[k],
                                               device_id=(x, y, 1 - c), device_id_type=MESH) for k in range(n)]
        for cp in copies:
            cp.start()
        for cp in copies:
            cp.wait()

    return pl.pallas_call(
        body, name=name, in_specs=[ANY] * n, out_specs=[ANY] * n,
        out_shape=[jax.ShapeDtypeStruct(h.shape, h.dtype) for h in halves],
        scratch_shapes=[pltpu.SemaphoreType.DMA((n,)), pltpu.SemaphoreType.DMA((n,))],
        compiler_params=pltpu.CompilerParams(has_side_effects=True),
    )(*halves)


class _GradRound:
    def __init__(self, tag, pieces):
        self.tag, self.keys = tag, list(pieces)
        self.shapes = [pieces[k].shape[1:] for k in self.keys]
        self.packs = [_as_halves(pieces[k]) for k in self.keys]
        self.swap = _PairSwap(self.packs)

    def pair_sum(self, recvs):
        _, _, c = _mesh_pos()
        core = jnp.reshape(c, (1,)).astype(jnp.int32)
        sums = [_add_pair(p, r, core, f"grad_{self.tag}_pair_add_{i}") for i, (p, r) in enumerate(zip(self.packs, recvs))]
        self.parts32 = [s[0] for s in sums]
        self.scatter = _ChipScatter([s[1] for s in sums])

    def finish(self, recvs16):
        x, y, c = _mesh_pos()
        chip = jnp.reshape(2 * x + y, (1,)).astype(jnp.int32)
        halves = [_sum_chips(p, r, chip, f"grad_{self.tag}_chip_sum_{i}") for i, (p, r) in enumerate(zip(self.parts32, recvs16))]
        others = _share_with_sibling(halves, f"grad_{self.tag}_pair_share")
        return {k: _from_halves(h, o, c, shape) for k, h, o, shape in zip(self.keys, halves, others, self.shapes)}


def _pack_small(vals):
    parts = []
    for n in REPLICATED:
        flat = vals[n].reshape(-1)
        rows = -(-flat.shape[0] // 128)
        rows = -(-rows // 8) * 8
        parts.append(jnp.pad(flat, (0, rows * 128 - flat.shape[0])).reshape(rows, 128))
    return jnp.concatenate(parts, axis=0)


def _unpack_small(pack, like):
    out, off = {}, 0
    for n in REPLICATED:
        size = like[n].size
        rows = -(-size // 128)
        rows = -(-rows // 8) * 8
        out[n] = pack[off:off + rows].reshape(-1)[:size].reshape(like[n].shape)
        off += rows
    return out


def _all_to_all_small(pack, name):
    rows, lanes = pack.shape
    flips = [(dx, dy, dc) for dx in (0, 1) for dy in (0, 1) for dc in (0, 1)][1:]

    def body(src_ref, out_ref, send_sems, recv_sems, local_sem):
        x, y, c = _mesh_pos()
        me = 4 * x + 2 * y + c
        loc = pltpu.make_async_copy(src_ref, out_ref.at[me], local_sem)
        loc.start()
        copies = []
        for k, (dx, dy, dc) in enumerate(flips):
            peer = (x ^ dx, y ^ dy, c ^ dc)
            cp = pltpu.make_async_remote_copy(src_ref, out_ref.at[me], send_sems.at[k], recv_sems.at[k],
                                              device_id=peer, device_id_type=MESH)
            cp.start()
            copies.append(cp)
        for k, (dx, dy, dc) in enumerate(flips):
            peer = (x ^ dx, y ^ dy, c ^ dc)
            pltpu.make_async_remote_copy(src_ref, out_ref.at[4 * peer[0] + 2 * peer[1] + peer[2]], send_sems.at[k],
                                         recv_sems.at[k], device_id=peer, device_id_type=MESH).wait_recv()
        for cp in copies:
            cp.wait_send()
        loc.wait()

    return pl.pallas_call(
        body, name=name, in_specs=[ANY], out_specs=ANY,
        out_shape=jax.ShapeDtypeStruct((8, rows, lanes), pack.dtype),
        scratch_shapes=[pltpu.SemaphoreType.DMA((7,)), pltpu.SemaphoreType.DMA((7,)), pltpu.SemaphoreType.DMA],
        compiler_params=pltpu.CompilerParams(has_side_effects=True),
    )(pack)


def _sum_devices(stack, name):
    ndev, rows, lanes = stack.shape

    def body(s_ref, o_ref):
        s = s_ref[0]
        for d in range(1, ndev):
            s = s + s_ref[d]
        o_ref[...] = s

    return pl.pallas_call(
        body, name=name, grid=(1,),
        in_specs=[pl.BlockSpec((ndev, rows, lanes), lambda i: (0, 0, 0))],
        out_specs=pl.BlockSpec((rows, lanes), lambda i: (0, 0)),
        out_shape=jax.ShapeDtypeStruct((rows, lanes), F32),
    )(stack)


WEIGHT_ORDER = ("ffn_norm1", "ffn1_w_in", "ffn1_w_out", "mix_norm", "ffn_norm2", "ffn2_w_in", "ffn2_w_out", "ab_w_in",
                "dn_conv_w", "dn_a_log", "dn_dt_bias", "dn_out_norm", "sg_norm", "sg_w", "sg_b", "ab_w_out", "pool_w",
                "pool_scale", "final_norm")
MATRICES = ("ffn1_w_in", "ffn1_w_out", "ffn2_w_in", "ffn2_w_out", "ab_w_in", "ab_w_out", "pool_w")
GATHER_FIRST = (("ffn1_w_in", 0), ("ffn1_w_out", 0))
GATHER_LATER = {"ffn1_l0_fwd": (("ab_w_in", None), ("dn_conv_w", None), ("ab_w_out", None)),
                "dn_prep": (("ffn2_w_in", 0), ("ffn2_w_out", 0)),
                "dn_scan": (("ffn1_w_in", 1), ("ffn1_w_out", 1)),
                "ffn2_l0_fwd": (("pool_w", None), ("pool_scale", None), ("ffn2_w_in", 1), ("ffn2_w_out", 1))}
REDUCE_ROUNDS = ((("ffn2_w_in", 1), ("ffn2_w_out", 1), ("ffn1_w_in", 1), ("ffn1_w_out", 1), ("pool_w", None), ("pool_scale", None)),
                 (("ffn2_w_in", 0), ("ffn2_w_out", 0)),
                 (("ffn1_w_in", 0), ("ffn1_w_out", 0), ("ab_w_out", None), ("ab_w_in", None), ("dn_conv_w", None)))


def _as_2d(a):
    return a.reshape(-1, a.shape[-1])


def kernel(x, ffn_norm1, ffn1_w_in, ffn1_w_out, mix_norm, ffn_norm2, ffn2_w_in, ffn2_w_out, ab_w_in, dn_conv_w, dn_a_log, dn_dt_bias, dn_out_norm, sg_norm, sg_w, sg_b, ab_w_out, pool_w, pool_scale, final_norm, loss_target, m_ffn_norm1, m_ffn1_w_in, m_ffn1_w_out, m_mix_norm, m_ffn_norm2, m_ffn2_w_in, m_ffn2_w_out, m_ab_w_in, m_dn_conv_w, m_dn_a_log, m_dn_dt_bias, m_dn_out_norm, m_sg_norm, m_sg_w, m_sg_b, m_ab_w_out, m_pool_w, m_pool_scale, m_final_norm, v_ffn_norm1, v_ffn1_w_in, v_ffn1_w_out, v_mix_norm, v_ffn_norm2, v_ffn2_w_in, v_ffn2_w_out, v_ab_w_in, v_dn_conv_w, v_dn_a_log, v_dn_dt_bias, v_dn_out_norm, v_sg_norm, v_sg_w, v_sg_b, v_ab_w_out, v_pool_w, v_pool_scale, v_final_norm):
    given = dict(locals())
    wts = {n: given[n] for n in WEIGHT_ORDER}
    mom_m = {n: given["m_" + n] for n in WEIGHT_ORDER}
    mom_v = {n: given["v_" + n] for n in WEIGHT_ORDER}

    rep = {n: wts[n] for n in REPLICATED}
    first = _ChipGather(_shard_pieces(wts, GATHER_FIRST), [n in MATRICES for n, _ in GATHER_FIRST])
    w = {**_replicated_layouts(rep), **_layouts_from(dict(zip(GATHER_FIRST, first.run("weight_gather_first"))))}
    late = {host: (_ChipGather(_shard_pieces(wts, keys), [n in MATRICES for n, _ in keys]),
                   functools.partial(lambda keys, arrived: _layouts_from(dict(zip(keys, arrived))), keys))
            for host, keys in GATHER_LATER.items()}

    loss, dx, g, reduced = _local_step(x[0], loss_target[0], w, reduce=True, late=late)
    g_rep = _replicated_grads(g)
    grads = {}
    for n in SHARDED:
        layers = [reduced[(n, layer)] for layer in range(wts[n].shape[0])] if (n, 0) in reduced else [reduced[(n, None)]]
        grads[n] = jnp.stack(layers).reshape(wts[n].shape)
    small = _sum_devices(_all_to_all_small(_pack_small(g_rep), "grad_small_exchange"), "grad_small_sum")
    grads.update(_unpack_small(small, rep))

    delta, new_m, new_v = {}, {}, {}
    for n in SHARDED:
        d, m1, v1 = _adamw(_as_2d(wts[n]), _as_2d(grads[n]), _as_2d(mom_m[n]), _as_2d(mom_v[n]), f"adamw_{n}")
        delta[n], new_m[n], new_v[n] = (t.reshape(wts[n].shape) for t in (d, m1, v1))
    d, m1, v1 = _adamw(_pack_small(rep), small, _pack_small({n: mom_m[n] for n in REPLICATED}),
                       _pack_small({n: mom_v[n] for n in REPLICATED}), "adamw_replicated")
    for tgt, packed in ((delta, d), (new_m, m1), (new_v, v1)):
        tgt.update(_unpack_small(packed, rep))

    total = lax.psum(loss[0, 0], ("x", "y", "c"))
    outs = [total, dx[None]]
    for group in (grads, delta, new_m, new_v):
        outs.extend(group[n] for n in WEIGHT_ORDER)
    return tuple(outs)
```

```python
import functools

import jax
import jax.numpy as jnp
from jax import lax
from jax.experimental import pallas as pl
from jax.experimental.pallas import tpu as pltpu

F32, BF16 = jnp.float32, jnp.bfloat16
NORM_EPS = 1e-6
D_MODEL = 1024
D_FF = 2816
N_CHIPS = 4
FF_CHUNK = 2 * D_FF // N_CHIPS
DN_HEADS, DN_DIM, DN_CHUNK, DN_CONV = 4, 128, 64, 4
DN_BLOCK = 2 * DN_CHUNK
DN_PREP_CHUNKS = 8
DN_SCAN_CHUNKS = 8
SG_GROUPS, SG_DIM, SG_CHUNK = 4, 128, 128
POOL_WINDOWS = (2, 4, 8, 16)
POOL_DIM = 256
POOL_HALO = 16
CONV_HALO = 8
PROJ_W = 3200
BA_BLOCK = 3072 // 128
ADAM_LR, ADAM_B1, ADAM_B2, ADAM_EPS, ADAM_WD, ADAM_STEP = 0.001, 0.9, 0.999, 1e-08, 0.01, 10
VMEM_BIG = 52 * 1024 * 1024
FFN_FWD_ROWS = 512
FFN_BWD_ROWS = 256
MESH = pl.DeviceIdType.MESH
HI = lax.Precision.HIGHEST
ANY = pl.BlockSpec(memory_space=pl.ANY)


def _params(sem=None, vmem=None):
    return pltpu.CompilerParams(dimension_semantics=sem, vmem_limit_bytes=vmem)


def _dot(a, b):
    return jnp.dot(a, b, preferred_element_type=F32)


def _dot_nt(a, b):
    return lax.dot_general(a, b, (((1,), (1,)), ((), ())), preferred_element_type=F32)


def _dot_tn(a, b):
    return lax.dot_general(a, b, (((0,), (0,)), ((), ())), preferred_element_type=F32)


def _dot_hi(a, b):
    return jnp.dot(a, b, preferred_element_type=F32, precision=HI)


def _dot_mid(a, b):
    return jnp.dot(a, b, preferred_element_type=F32, precision=lax.Precision.HIGH)


def _bf(a):
    return a.astype(BF16)


def _rms_stats(x):
    r = lax.rsqrt(jnp.mean(x * x, axis=-1, keepdims=True) + NORM_EPS)
    return x * r, r


def _rms_bwd(dh, xhat, r, w):
    dhn = dh * w
    return r * (dhn - xhat * jnp.mean(dhn * xhat, axis=-1, keepdims=True))


def _sigmoid(x):
    return jax.nn.sigmoid(x)


def _silu_grad(x, s):
    return s * (1.0 + x * (1.0 - s))


def _gelu(x):
    return 0.5 * x * (1.0 + lax.erf(x * 0.7071067811865476))


def _gelu_grad(x):
    return 0.5 * (1.0 + lax.erf(x * 0.7071067811865476)) + x * jnp.exp(-0.5 * x * x) * 0.3989422804014327


def _softplus(x):
    return jnp.maximum(x, 0.0) + jnp.log(1.0 + jnp.exp(-jnp.abs(x)))


def _tile(n, pref):
    t = min(n, pref)
    assert n % t == 0, (n, t)
    return t


def _ffn_weight_specs():
    once = pl.Buffered(1)
    return [pl.BlockSpec((N_CHIPS, D_MODEL, FF_CHUNK), lambda i: (0, 0, 0), pipeline_mode=once),
            pl.BlockSpec((N_CHIPS, D_FF // N_CHIPS, D_MODEL), lambda i: (0, 0, 0), pipeline_mode=once)]


def _ffn_fwd(x, nw, win, wout, layer, name, exchange=None):
    T, D = x.shape
    tm = _tile(T, FFN_FWD_ROWS)
    nj = N_CHIPS // 2

    def body(x_ref, n_ref, win_ref, wo_ref, xo_ref, g_ref, u_ref, hb_ref):
        xv = x_ref[...]
        xhat, _ = _rms_stats(xv)
        h = _bf(xhat * n_ref[...])
        hb_ref[...] = h
        acc = None
        for j in range(nj):
            cols = slice(j * FF_CHUNK, (j + 1) * FF_CHUNK)
            g = _dot(h, win_ref[j])
            u = _dot(h, win_ref[nj + j])
            g_ref[:, cols] = _bf(g)
            u_ref[:, cols] = _bf(u)
            part = _dot(_bf(g * _sigmoid(g) * u), wo_ref[2 * j:2 * j + 2].reshape(FF_CHUNK, D))
            acc = part if acc is None else acc + part
        xo_ref[...] = xv + 0.5 * acc

    row = pl.BlockSpec((tm, D), lambda i: (i, 0))
    wide = pl.BlockSpec((tm, D_FF), lambda i: (i, 0))
    return _call_with_exchange(
        body, exchange, name=name, steps=T // tm, vmem=VMEM_BIG,
        in_specs=[row, pl.BlockSpec((None, 1, D), lambda i: (layer, 0, 0))] + _ffn_weight_specs(),
        out_specs=[row, wide, wide, row],
        out_shape=[jax.ShapeDtypeStruct((T, D), F32), jax.ShapeDtypeStruct((T, D_FF), BF16),
                   jax.ShapeDtypeStruct((T, D_FF), BF16), jax.ShapeDtypeStruct((T, D), BF16)],
        args=(x, nw, win, wout))


def _ffn_bwd(dxo, x, nw, g, u, win, wout, layer, name, exchange=None):
    T, D = x.shape
    tm = _tile(T, FFN_BWD_ROWS)
    nj = N_CHIPS // 2

    def body(dxo_ref, x_ref, n_ref, g_ref, u_ref, win_ref, wo_ref, dx_ref, dgu_ref, a_ref, dyb_ref, dn_ref):
        @pl.when(pl.program_id(0) == 0)
        def _():
            dn_ref[...] = jnp.zeros_like(dn_ref)

        dxo = dxo_ref[...]
        dyb = _bf(0.5 * dxo)
        dyb_ref[...] = dyb
        dh = None
        for j in range(nj):
            cols = slice(j * FF_CHUNK, (j + 1) * FF_CHUNK)
            da = _dot_nt(dyb, wo_ref[2 * j:2 * j + 2].reshape(FF_CHUNK, D))
            gv = g_ref[:, cols].astype(F32)
            uv = u_ref[:, cols].astype(F32)
            sg = _sigmoid(gv)
            sl = gv * sg
            dgb = _bf(da * uv * _silu_grad(gv, sg))
            dub = _bf(da * sl)
            a_ref[:, cols] = _bf(sl * uv)
            dgu_ref[:, cols] = dgb
            dgu_ref[:, D_FF + j * FF_CHUNK:D_FF + (j + 1) * FF_CHUNK] = dub
            part = _dot_nt(dgb, win_ref[j]) + _dot_nt(dub, win_ref[nj + j])
            dh = part if dh is None else dh + part
        xhat, r = _rms_stats(x_ref[...])
        dx_ref[...] = dxo + _rms_bwd(dh, xhat, r, n_ref[...])
        dn_ref[...] += jnp.sum(dh * xhat, axis=0, keepdims=True)

    row = pl.BlockSpec((tm, D), lambda i: (i, 0))
    wide = pl.BlockSpec((tm, D_FF), lambda i: (i, 0))
    return _call_with_exchange(
        body, exchange, name=name, steps=T // tm, vmem=VMEM_BIG,
        in_specs=[row, row, pl.BlockSpec((None, 1, D), lambda i: (layer, 0, 0)), wide, wide] + _ffn_weight_specs(),
        out_specs=[row, pl.BlockSpec((tm, 2 * D_FF), lambda i: (i, 0)), wide, row, pl.BlockSpec((1, D), lambda i: (0, 0))],
        out_shape=[jax.ShapeDtypeStruct((T, D), F32), jax.ShapeDtypeStruct((T, 2 * D_FF), BF16),
                   jax.ShapeDtypeStruct((T, D_FF), BF16), jax.ShapeDtypeStruct((T, D), BF16),
                   jax.ShapeDtypeStruct((1, D), F32)],
        args=(dxo, x, nw, g, u, win, wout))


def _matmul_tn(a, b, bm, bn, name, stack_n=False):
    T, M = a.shape
    N = b.shape[1]
    tk = _tile(T, 1024)
    bm, bn = _tile(M, bm), _tile(N, bn)

    def body(a_ref, b_ref, o_ref):
        @pl.when(pl.program_id(2) == 0)
        def _():
            o_ref[...] = jnp.zeros_like(o_ref)

        o_ref[...] += _dot_tn(_bf(a_ref[...]), _bf(b_ref[...]))

    if stack_n:
        out_spec = pl.BlockSpec((None, bm, bn), lambda m, n, k: (n, m, 0))
        out_shape = jax.ShapeDtypeStruct((N // bn, M, bn), F32)
    else:
        out_spec = pl.BlockSpec((bm, bn), lambda m, n, k: (m, n))
        out_shape = jax.ShapeDtypeStruct((M, N), F32)
    return pl.pallas_call(
        body, name=name, grid=(M // bm, N // bn, T // tk),
        in_specs=[pl.BlockSpec((tk, bm), lambda m, n, k: (k, m)),
                  pl.BlockSpec((tk, bn), lambda m, n, k: (k, n))],
        out_specs=out_spec, out_shape=out_shape,
        compiler_params=_params(("parallel", "parallel", "arbitrary"), VMEM_BIG),
    )(a, b)


def _matmul(a, b, name, trans_b=False, res=None, out_dtype=F32):
    T, K = a.shape
    N = b.shape[0] if trans_b else b.shape[1]
    tm = _tile(T, 512)

    def body(*refs):
        a_ref, b_ref = refs[0], refs[1]
        o_ref = refs[-1]
        av, bv = _bf(a_ref[...]), _bf(b_ref[...])
        acc = _dot_nt(av, bv) if trans_b else _dot(av, bv)
        if res is not None:
            acc = acc + refs[2][...]
        o_ref[...] = acc.astype(out_dtype)

    in_specs = [pl.BlockSpec((tm, K), lambda i: (i, 0)), pl.BlockSpec(b.shape, lambda i: (0, 0))]
    args = [a, b]
    if res is not None:
        in_specs.append(pl.BlockSpec((tm, N), lambda i: (i, 0)))
        args.append(res)
    return pl.pallas_call(
        body, name=name, grid=(T // tm,), in_specs=in_specs,
        out_specs=pl.BlockSpec((tm, N), lambda i: (i, 0)),
        out_shape=jax.ShapeDtypeStruct((T, N), out_dtype),
        compiler_params=_params(("parallel",), VMEM_BIG),
    )(*args)


def _rms_fwd_call(x, nw, layer, name):
    T, D = x.shape
    tm = _tile(T, 512)

    def body(x_ref, n_ref, o_ref):
        xhat, _ = _rms_stats(x_ref[...])
        o_ref[...] = _bf(xhat * n_ref[...])

    return pl.pallas_call(
        body, name=name, grid=(T // tm,),
        in_specs=[pl.BlockSpec((tm, D), lambda i: (i, 0)), pl.BlockSpec((None, 1, D), lambda i: (layer, 0, 0))],
        out_specs=pl.BlockSpec((tm, D), lambda i: (i, 0)),
        out_shape=jax.ShapeDtypeStruct((T, D), BF16),
        compiler_params=_params(("parallel",)),
    )(x, nw)


def _rms_bwd_call(dh, x, nw, dres, layer, name):
    T, D = x.shape
    tm = _tile(T, 512)

    def body(dh_ref, x_ref, n_ref, dr_ref, dx_ref, dn_ref):
        @pl.when(pl.program_id(0) == 0)
        def _():
            dn_ref[...] = jnp.zeros_like(dn_ref)

        xhat, r = _rms_stats(x_ref[...])
        dh_v = dh_ref[...]
        dx_ref[...] = dr_ref[...] + _rms_bwd(dh_v, xhat, r, n_ref[...])
        dn_ref[...] += jnp.sum(dh_v * xhat, axis=0, keepdims=True)

    row = pl.BlockSpec((tm, D), lambda i: (i, 0))
    return pl.pallas_call(
        body, name=name, grid=(T // tm,),
        in_specs=[row, row, pl.BlockSpec((None, 1, D), lambda i: (layer, 0, 0)), row],
        out_specs=[row, pl.BlockSpec((1, D), lambda i: (0, 0))],
        out_shape=[jax.ShapeDtypeStruct((T, D), F32), jax.ShapeDtypeStruct((1, D), F32)],
        compiler_params=_params(("arbitrary",)),
    )(dh, x, nw, dres)


def _shift_rows(x, s):
    n = x.shape[0]
    s = s % n
    return x if s == 0 else pltpu.roll(x, s, 0)


def _conv_fwd(proj, conv_w, name):
    T = proj.shape[0]
    C = 3 * DN_HEADS * DN_DIM
    cb = 512
    tm = _tile(T, 512)
    hb = tm // CONV_HALO

    def body(x_ref, xp_ref, w_ref, o_ref):
        i = pl.program_id(1)
        prev = jnp.where(i == 0, 0.0, xp_ref[...])
        ext = jnp.concatenate([prev, x_ref[...]], axis=0)
        w = w_ref[...]
        y = ext * w[DN_CONV - 1:DN_CONV, :]
        for k in range(DN_CONV - 1):
            y = y + _shift_rows(ext, DN_CONV - 1 - k) * w[k:k + 1, :]
        y = y[CONV_HALO:, :]
        o_ref[...] = y * _sigmoid(y)

    return pl.pallas_call(
        body, name=name, grid=(C // cb, T // tm),
        in_specs=[pl.BlockSpec((tm, cb), lambda c, i: (i, c)),
                  pl.BlockSpec((CONV_HALO, cb), lambda c, i: (jnp.maximum(i * hb - 1, 0), c)),
                  pl.BlockSpec((DN_CONV, cb), lambda c, i: (0, c))],
        out_specs=pl.BlockSpec((tm, cb), lambda c, i: (i, c)),
        out_shape=jax.ShapeDtypeStruct((T, C), F32),
        compiler_params=_params(("parallel", "parallel")),
    )(proj, proj, conv_w)


def _conv_bwd(dy, proj, conv_w, name):
    T = proj.shape[0]
    C = 3 * DN_HEADS * DN_DIM
    cb = 512
    tm = _tile(T, 512)
    hb = tm // CONV_HALO
    nt = T // tm

    def body(x_ref, xp_ref, xn_ref, dy_ref, dyn_ref, w_ref, dx_ref, dw_ref):
        i = pl.program_id(1)

        @pl.when(i == 0)
        def _():
            dw_ref[...] = jnp.zeros_like(dw_ref)

        prev = jnp.where(i == 0, 0.0, xp_ref[...])
        ext = jnp.concatenate([prev, x_ref[...], xn_ref[...]], axis=0)
        dy_ext = jnp.concatenate([jnp.zeros((CONV_HALO, cb), F32), dy_ref[...],
                                  jnp.where(i == nt - 1, 0.0, dyn_ref[...])], axis=0)
        w = w_ref[...]
        shifted = [_shift_rows(ext, DN_CONV - 1 - k) for k in range(DN_CONV)]
        y = shifted[0] * w[0:1, :]
        for k in range(1, DN_CONV):
            y = y + shifted[k] * w[k:k + 1, :]
        s = _sigmoid(y)
        dpre = dy_ext * _silu_grad(y, s)
        dx = dpre * w[DN_CONV - 1:DN_CONV, :]
        for k in range(DN_CONV - 1):
            dx = dx + _shift_rows(dpre, -(DN_CONV - 1 - k)) * w[k:k + 1, :]
        dx_ref[...] = _bf(dx[CONV_HALO:CONV_HALO + tm, :])
        rows = [jnp.sum((dpre * shifted[k])[CONV_HALO:CONV_HALO + tm, :], axis=0, keepdims=True) for k in range(DN_CONV)]
        dw_ref[...] += jnp.concatenate(rows, axis=0)

    last_halo = T // CONV_HALO - 1
    return pl.pallas_call(
        body, name=name, grid=(C // cb, nt),
        in_specs=[pl.BlockSpec((tm, cb), lambda c, i: (i, c)),
                  pl.BlockSpec((CONV_HALO, cb), lambda c, i: (jnp.maximum(i * hb - 1, 0), c)),
                  pl.BlockSpec((CONV_HALO, cb), lambda c, i: (jnp.minimum((i + 1) * hb, last_halo), c)),
                  pl.BlockSpec((tm, cb), lambda c, i: (i, c)),
                  pl.BlockSpec((CONV_HALO, cb), lambda c, i: (jnp.minimum((i + 1) * hb, last_halo), c)),
                  pl.BlockSpec((DN_CONV, cb), lambda c, i: (0, c))],
        out_specs=[pl.BlockSpec((tm, cb), lambda c, i: (i, c)),
                   pl.BlockSpec((DN_CONV, cb), lambda c, i: (0, c))],
        out_shape=[jax.ShapeDtypeStruct((T, C), BF16), jax.ShapeDtypeStruct((DN_CONV, C), F32)],
        compiler_params=_params(("parallel", "arbitrary")),
    )(proj, proj, proj, dy, dy, conv_w)


def _unit_lower_inverses(lows, eye):
    def each(fn, *lists):
        return [fn(*args) for args in zip(*lists)]

    p1 = [-low for low in lows]
    p2 = each(_dot_mid, p1, p1)
    p4 = each(_dot_mid, p2, p2)
    a = each(lambda x, y: eye + x + y + _dot_mid(x, y), p1, p2)
    p8 = each(_dot_mid, p4, p4)
    p16 = each(_dot_mid, p8, p8)
    b = each(lambda x, y: eye + x + y + _dot_mid(x, y), p4, p8)
    p32 = each(_dot_mid, p16, p16)
    ab = each(_dot_mid, a, b)
    c = each(lambda x, y: eye + x + y + _dot_mid(x, y), p16, p32)
    return each(_dot_mid, ab, c)


def _interleave(chains):
    results = [None] * len(chains)
    live = list(range(len(chains)))
    while live:
        for i in list(live):
            try:
                next(chains[i])
            except StopIteration as stop:
                results[i] = stop.value
                live.remove(i)
        yield
    return results


def _run_interleaved(chains):
    rounds = _interleave(chains)
    while True:
        try:
            next(rounds)
        except StopIteration as stop:
            return stop.value


def _l2_unit(x):
    r = lax.rsqrt(jnp.sum(x * x, axis=-1, keepdims=True) + NORM_EPS)
    return x * r, r


class _BlockMasks:
    def __init__(self):
        n = DN_BLOCK
        row = lax.broadcasted_iota(jnp.int32, (n, n), 0)
        col = lax.broadcasted_iota(jnp.int32, (n, n), 1)
        same = (row // DN_CHUNK) == (col // DN_CHUNK)
        self.lower, self.strict_lower = same & (row >= col), same & (row > col)
        self.upper, self.strict_upper = same & (row <= col), same & (row < col)
        self.eye = (row == col).astype(F32)
        self.first = lax.broadcasted_iota(jnp.int32, (n, 1), 0) < DN_CHUNK


def _dn_gates(ba, hp):
    coef = -jnp.exp(hp[0:1, :])
    pre = ba + hp[1:2, :]
    return _sigmoid(ba), coef * _softplus(pre), coef, pre


def _dn_block_gates(mk, ba, hp):
    assert DN_BLOCK == 2 * DN_CHUNK
    beta_t, graw_t, coef, pre = _dn_gates(ba, hp)
    gcum_t = _dot_hi(mk.lower.astype(F32), graw_t)
    gl_t = jnp.where(mk.first, gcum_t[DN_CHUNK - 1:DN_CHUNK, :], gcum_t[DN_BLOCK - 1:DN_BLOCK, :])
    return beta_t, gcum_t, gl_t, graw_t, coef, pre


def _dn_local(mk, qraw, kraw, bc, gc, gl):
    f = {}
    f["qn"], f["rq"] = _l2_unit(qraw)
    qh = f["qn"] * (DN_DIM ** -0.5)
    kh, f["rk"] = _l2_unit(kraw)
    gr = jnp.broadcast_to(gc, (DN_BLOCK, DN_BLOCK)).T
    dec = jnp.where(mk.lower, jnp.exp(jnp.where(mk.lower, gc - gr, 0.0)), 0.0)
    kb = kh * bc
    mkk = _dot_nt(_bf(kb), _bf(kh))
    eg = jnp.exp(gc)
    mqk = _dot_nt(_bf(qh), _bf(kh))
    etl = jnp.exp(gl - gc)
    f.update(qh=qh, kh=kh, gr=gr, dec=dec, kb=kb, mkk=mkk, eg=eg, mqk=mqk, attn=mqk * dec, etl=etl, qd=qh * eg, kt=kh * etl)
    return f


def _dn_specs(rows, rev=None):
    at = (lambda n: n) if rev is None else rev
    hw = DN_HEADS * DN_DIM
    return dict(
        qkv=[pl.BlockSpec((rows, hw), lambda n, j=j: (at(n), j)) for j in range(3)],
        ba=pl.BlockSpec((rows, 128), lambda n: (at(n), BA_BLOCK)),
        hp=pl.BlockSpec((8, 128), lambda n: (0, 0)),
        tok=pl.BlockSpec((rows, hw), lambda n: (at(n), 0)),
        attn=pl.BlockSpec((rows, DN_HEADS * DN_CHUNK), lambda n: (at(n), 0)),
        gate=pl.BlockSpec((rows // DN_CHUNK, 8, 128), lambda n: (at(n), 0, 0)),
        state=pl.BlockSpec((rows // DN_CHUNK, DN_HEADS, DN_DIM, DN_DIM), lambda n: (at(n), 0, 0, 0)),
    )


def _call_with_exchange(body, exchange, *, name, steps, in_specs, out_specs, out_shape, args, vmem=None, scratch_shapes=()):
    if exchange is None:
        res = pl.pallas_call(body, name=name, grid=(steps,), in_specs=in_specs, out_specs=out_specs, out_shape=out_shape,
                             scratch_shapes=list(scratch_shapes), compiler_params=_params(("arbitrary",), vmem))(*args)
        return list(res), None
    n_in, n_out, m, n_scr = len(in_specs), len(out_specs), len(exchange.operands), len(scratch_shapes)

    def hosted(*refs):
        ins, ex_ins = refs[:n_in], refs[n_in:n_in + m]
        outs, ex_outs = refs[n_in + m:n_in + m + n_out], refs[n_in + m + n_out:n_in + 2 * m + n_out]
        scratch, sems = refs[n_in + 2 * m + n_out:n_in + 2 * m + n_out + n_scr], refs[n_in + 2 * m + n_out + n_scr:]

        @pl.when(pl.program_id(0) == 0)
        def _():
            exchange.start(ex_ins, ex_outs, sems)

        body(*ins, *outs, *scratch)

        @pl.when(pl.program_id(0) == steps - 1)
        def _():
            exchange.finish(ex_ins, ex_outs, sems)

    res = pl.pallas_call(
        hosted, name=name, grid=(steps,), in_specs=list(in_specs) + [ANY] * m, out_specs=list(out_specs) + [ANY] * m,
        out_shape=list(out_shape) + list(exchange.out_shape), scratch_shapes=list(scratch_shapes) + list(exchange.scratch),
        compiler_params=pltpu.CompilerParams(dimension_semantics=("arbitrary",), vmem_limit_bytes=vmem, has_side_effects=True),
    )(*args, *exchange.operands)
    return list(res[:n_out]), exchange.finalize(list(res[n_out:]))


def _dn_prep(qkv, proj, hp, name, exchange=None):
    T = qkv.shape[0]
    n_chunks = T // DN_CHUNK
    blocks = max(1, min(DN_PREP_CHUNKS, n_chunks) * DN_CHUNK // DN_BLOCK)
    group = blocks * DN_BLOCK // DN_CHUNK
    rows = blocks * DN_BLOCK
    hw = DN_HEADS * DN_DIM

    def body(q_ref, k_ref, v_ref, ba_ref, hp_ref, u_ref, w_ref, p_ref, qd_ref, kt_ref, gl_ref, inv_ref):
        mk = _BlockMasks()
        hp_v = hp_ref[...]
        chains = []
        for j in range(blocks):
            rs = slice(j * DN_BLOCK, (j + 1) * DN_BLOCK)
            beta_t, gcum_t, gl_t = _dn_block_gates(mk, ba_ref[rs, :], hp_v)[:3]
            for c in range(DN_BLOCK // DN_CHUNK):
                gl_ref[j * (DN_BLOCK // DN_CHUNK) + c] = jnp.broadcast_to(gl_t[c * DN_CHUNK:c * DN_CHUNK + 1, :], (8, 128))
            for h in range(DN_HEADS):
                sl = slice(h * DN_DIM, (h + 1) * DN_DIM)
                gate = slice(DN_HEADS + h, DN_HEADS + h + 1)
                bc = beta_t[:, h:h + 1]
                f = _dn_local(mk, q_ref[rs, sl], k_ref[rs, sl], bc, gcum_t[:, gate], gl_t[:, gate])
                for c in range(DN_BLOCK // DN_CHUNK):
                    cr = slice(c * DN_CHUNK, (c + 1) * DN_CHUNK)
                    p_ref[j * DN_BLOCK + c * DN_CHUNK:j * DN_BLOCK + (c + 1) * DN_CHUNK, h * DN_CHUNK:(h + 1) * DN_CHUNK] = _bf(f["attn"][cr, cr])
                qd_ref[rs, sl] = _bf(f["qd"])
                kt_ref[rs, sl] = _bf(f["kt"])
                chains.append((rs, sl, bc, f))
        invs = _unit_lower_inverses([jnp.where(mk.strict_lower, f["mkk"] * f["dec"], 0.0) for _, _, _, f in chains], mk.eye)
        for (rs, sl, bc, f), inv in zip(chains, invs):
            inv_ref[rs, sl] = inv
            sol = _dot_mid(inv, jnp.concatenate([v_ref[rs, sl] * bc, f["kb"] * f["eg"]], axis=1))
            u_ref[rs, sl] = sol[:, :DN_DIM]
            w_ref[rs, sl] = _bf(sol[:, DN_DIM:])

    sp = _dn_specs(rows)
    tok16 = jax.ShapeDtypeStruct((T, hw), BF16)
    return _call_with_exchange(
        body, exchange, name=name, steps=n_chunks // group,
        in_specs=sp["qkv"] + [sp["ba"], sp["hp"]],
        out_specs=[sp["tok"], sp["tok"], sp["attn"], sp["tok"], sp["tok"], sp["gate"], sp["tok"]],
        out_shape=[jax.ShapeDtypeStruct((T, hw), F32), tok16, jax.ShapeDtypeStruct((T, DN_HEADS * DN_CHUNK), BF16),
                   tok16, tok16, jax.ShapeDtypeStruct((n_chunks, 8, 128), F32), jax.ShapeDtypeStruct((T, hw), F32)],
        args=(qkv, qkv, qkv, proj, hp))


def _dn_scan(u, w, p, qd, kt, gl, name, exchange=None):
    T = u.shape[0]
    n_chunks = T // DN_CHUNK
    group = min(DN_SCAN_CHUNKS, n_chunks)
    rows = group * DN_CHUNK
    hw = DN_HEADS * DN_DIM

    def body(u_ref, w_ref, p_ref, qd_ref, kt_ref, gl_ref, o_ref, vn_ref, sall_ref, s_s):
        @pl.when(pl.program_id(0) == 0)
        def _():
            s_s[...] = jnp.zeros_like(s_s)

        state = [s_s[h] for h in range(DN_HEADS)]
        for j in range(group):
            rs = slice(j * DN_CHUNK, (j + 1) * DN_CHUNK)
            for h in range(DN_HEADS):
                sl = slice(h * DN_DIM, (h + 1) * DN_DIM)
                sall_ref[j, h] = state[h]
                sb = _bf(state[h])
                vnb = _bf(u_ref[rs, sl] - _dot(w_ref[rs, sl], sb))
                vn_ref[rs, sl] = vnb
                o_ref[rs, sl] = _dot(qd_ref[rs, sl], sb) + _dot(p_ref[rs, h * DN_CHUNK:(h + 1) * DN_CHUNK], vnb)
                egl = jnp.exp(gl_ref[j, 0:1, DN_HEADS + h:DN_HEADS + h + 1])
                state[h] = state[h] * egl + _dot_tn(kt_ref[rs, sl], vnb)
        for h in range(DN_HEADS):
            s_s[h] = state[h]

    sp = _dn_specs(rows)
    return _call_with_exchange(
        body, exchange, name=name, steps=n_chunks // group,
        in_specs=[sp["tok"], sp["tok"], sp["attn"], sp["tok"], sp["tok"], sp["gate"]],
        out_specs=[sp["tok"], sp["tok"], sp["state"]],
        out_shape=[jax.ShapeDtypeStruct((T, hw), F32), jax.ShapeDtypeStruct((T, hw), BF16),
                   jax.ShapeDtypeStruct((n_chunks, DN_HEADS, DN_DIM, DN_DIM), F32)],
        scratch_shapes=[pltpu.VMEM((DN_HEADS, DN_DIM, DN_DIM), F32)],
        args=(u, w, p, qd, kt, gl))


def _dn_scan_bwd(w, p, qd, kt, gl, vn, sall, do, name, exchange=None):
    T = w.shape[0]
    n_chunks = T // DN_CHUNK
    group = min(DN_SCAN_CHUNKS, n_chunks)
    rows = group * DN_CHUNK
    hw = DN_HEADS * DN_DIM
    last = n_chunks // group - 1

    def body(w_ref, p_ref, qd_ref, kt_ref, gl_ref, vn_ref, sall_ref, do_ref, dvn_ref, dkt_ref, dgl_ref, ds_s):
        @pl.when(pl.program_id(0) == 0)
        def _():
            ds_s[...] = jnp.zeros_like(ds_s)

        lane = lax.broadcasted_iota(jnp.int32, (8, 128), 1)
        d_state = [ds_s[h] for h in range(DN_HEADS)]
        for j in reversed(range(group)):
            rs = slice(j * DN_CHUNK, (j + 1) * DN_CHUNK)
            dgl_tile = jnp.zeros((8, 128), F32)
            for h in range(DN_HEADS):
                sl = slice(h * DN_DIM, (h + 1) * DN_DIM)
                d_out = _bf(do_ref[rs, sl])
                d_new = d_state[h]
                d_newb = _bf(d_new)
                d_vn = _dot_tn(p_ref[rs, h * DN_CHUNK:(h + 1) * DN_CHUNK], d_out) + _dot(kt_ref[rs, sl], d_newb)
                dvn_ref[rs, sl] = d_vn
                dkt_ref[rs, sl] = _dot_nt(vn_ref[rs, sl], d_newb)
                egl = jnp.exp(gl_ref[j, 0:1, DN_HEADS + h:DN_HEADS + h + 1])
                prod = jnp.sum(d_new * sall_ref[j, h], axis=1, keepdims=True)
                dgl_tile = jnp.where(lane == DN_HEADS + h, jnp.sum(prod, axis=0, keepdims=True) * egl, dgl_tile)
                d_state[h] = d_new * egl + _dot_tn(qd_ref[rs, sl], d_out) - _dot_tn(w_ref[rs, sl], _bf(d_vn))
            dgl_ref[j] = dgl_tile
        for h in range(DN_HEADS):
            ds_s[h] = d_state[h]

    sp = _dn_specs(rows, rev=lambda n: last - n)
    return _call_with_exchange(
        body, exchange, name=name, steps=n_chunks // group,
        in_specs=[sp["tok"], sp["attn"], sp["tok"], sp["tok"], sp["gate"], sp["tok"], sp["state"], sp["tok"]],
        out_specs=[sp["tok"], sp["tok"], sp["gate"]],
        out_shape=[jax.ShapeDtypeStruct((T, hw), F32), jax.ShapeDtypeStruct((T, hw), F32),
                   jax.ShapeDtypeStruct((n_chunks, 8, 128), F32)],
        scratch_shapes=[pltpu.VMEM((DN_HEADS, DN_DIM, DN_DIM), F32)],
        args=(w, p, qd, kt, gl, vn, sall, do))


def _dn_prep_bwd(qkv, proj, hp, sall, vn, do, dvn, dkt, dgl, inv, u, w, name, exchange=None):
    T = qkv.shape[0]
    n_chunks = T // DN_CHUNK
    blocks = max(1, min(DN_PREP_CHUNKS, n_chunks) * DN_CHUNK // DN_BLOCK)
    per_block = DN_BLOCK // DN_CHUNK
    group = blocks * per_block
    rows = blocks * DN_BLOCK
    hw = DN_HEADS * DN_DIM
    first_rows, second_rows = slice(0, DN_CHUNK), slice(DN_CHUNK, DN_BLOCK)

    def rowsum(x):
        return jnp.sum(x, axis=1, keepdims=True)

    def by_chunk(x, s0, s1, fn):
        return jnp.concatenate([fn(x[first_rows], s0), fn(x[second_rows], s1)], axis=0)

    def body(q_ref, k_ref, v_ref, ba_ref, hp_ref, sall_ref, vn_ref, do_ref, dvn_ref, dkt_ref, dgl_ref,
             inv_ref, u_ref, w_ref, dqkv_ref, dba_ref, dhp_ref):
        @pl.when(pl.program_id(0) == 0)
        def _():
            dhp_ref[...] = jnp.zeros_like(dhp_ref)

        mk = _BlockMasks()
        hp_v = hp_ref[...]
        chains = []
        for j in range(blocks):
            rs = slice(j * DN_BLOCK, (j + 1) * DN_BLOCK)
            chains.append(one_block(mk, hp_v, *(r.at[rs, :] for r in (q_ref, k_ref, v_ref, ba_ref)),
                                    sall_ref.at[pl.ds(j * per_block, per_block)],
                                    *(r.at[rs, :] for r in (vn_ref, do_ref, dvn_ref, dkt_ref)),
                                    dgl_ref.at[pl.ds(j * per_block, per_block)],
                                    *(r.at[rs, :] for r in (inv_ref, u_ref, w_ref)),
                                    *(dqkv_ref.at[rs, pl.ds(i * hw, hw)] for i in range(3)), dba_ref.at[rs, :]))
        total = jnp.zeros((8, 128), F32)
        for part in _run_interleaved(chains):
            total = total + part
        dhp_ref[...] += total

    def one_block(mk, hp_v, q_ref, k_ref, v_ref, ba_ref, state_ref, vn_ref, do_ref, dvn_ref, dkt_ref, dgl_ref,
                  inv_ref, u_ref, w_ref, dq_ref, dk_ref, dv_ref, dba_ref):
        ba = ba_ref[...]
        beta_t, gcum_t, gl_t, graw_t, coef, pre = _dn_block_gates(mk, ba, hp_v)
        lane = lax.broadcasted_iota(jnp.int32, (DN_BLOCK, 128), 1)
        rowi = lax.broadcasted_iota(jnp.int32, (DN_BLOCK, 1), 0)

        def head(h):
            sl = slice(h * DN_DIM, (h + 1) * DN_DIM)
            gate = slice(DN_HEADS + h, DN_HEADS + h + 1)
            gc = gcum_t[:, gate]
            bc = beta_t[:, h:h + 1]
            sb0, sb1 = _bf(state_ref[0, h]), _bf(state_ref[1, h])
            vh = v_ref[:, sl]
            f = _dn_local(mk, q_ref[:, sl], k_ref[:, sl], bc, gc, gl_t[:, gate])
            yield
            qh, kh, kb, dec, eg, etl = f["qh"], f["kh"], f["kb"], f["dec"], f["eg"], f["etl"]
            qd, kt = f["qd"], f["kt"]
            qb, kbf, kbb = _bf(qh), _bf(kh), _bf(kb)
            dec_t = jnp.where(mk.upper, jnp.exp(jnp.where(mk.upper, f["gr"] - gc, 0.0)), 0.0)
            mkk_t = f["mkk"].T
            inv_t = inv_ref[:, sl].T
            mqk_t = f["mqk"].T

            d_out = _bf(do_ref[:, sl])
            vnb = vn_ref[:, sl]
            d_qd = by_chunk(d_out, sb0, sb1, _dot_nt)
            d_attn = _dot_nt(d_out, vnb)
            d_attn_t = _dot_nt(vnb, d_out)
            d_vn = dvn_ref[:, sl]
            d_kt = dkt_ref[:, sl]
            d_w = -by_chunk(_bf(d_vn), sb0, sb1, _dot_nt)
            yield
            d_rhs = _dot_mid(inv_t, jnp.concatenate([d_vn, d_w], axis=1))
            yield
            d_bu, d_bw = d_rhs[:, :DN_DIM], d_rhs[:, DN_DIM:]
            ub, wb, d_bub, d_bwb = _bf(u_ref[:, sl]), w_ref[:, sl], _bf(d_bu), _bf(d_bw)
            d_low = -(_dot_nt(d_bub, ub) + _dot_nt(d_bwb, wb))
            d_low_t = -(_dot_nt(ub, d_bub) + _dot_nt(wb, d_bwb))
            yield
            d_mkk = jnp.where(mk.strict_lower, d_low * dec, 0.0)
            d_mkk_t = jnp.where(mk.strict_upper, d_low_t * dec_t, 0.0)
            d_mqk = jnp.where(mk.lower, d_attn * dec, 0.0)
            d_mqk_t = jnp.where(mk.upper, d_attn_t * dec_t, 0.0)
            bw = kb * eg
            d_kb = _dot(_bf(d_mkk), kbf) + d_bw * eg
            d_k = _dot(_bf(d_mkk_t), kbb) + _dot(_bf(d_mqk_t), qb) + d_kt * etl + d_kb * bc
            d_q = _dot(_bf(d_mqk), kbf) + d_qd * eg
            yield
            d_beta = rowsum(d_kb * kh) + rowsum(d_bu * vh)
            dv_ref[:, sl] = d_bu * bc
            e_mat = d_mkk * f["mkk"] + d_mqk * f["mqk"]
            e_mat_t = d_mkk_t * mkk_t + d_mqk_t * mqk_t
            kt_term = rowsum(d_kt * kt)
            d_g = rowsum(e_mat) - rowsum(e_mat_t) + rowsum(d_qd * qd) + rowsum(d_bw * bw) - kt_term
            for c, chunk_rows in enumerate((mk.first, ~mk.first)):
                d_glast = dgl_ref[c, 0:1, gate] + jnp.sum(jnp.where(chunk_rows, kt_term, 0.0), axis=0, keepdims=True)
                d_g = d_g + jnp.where(rowi == (c + 1) * DN_CHUNK - 1, d_glast, 0.0)
            qn = f["qn"]
            d_qs = d_q * (DN_DIM ** -0.5)
            dq_ref[:, sl] = f["rq"] * (d_qs - qn * rowsum(d_qs * qn))
            dk_ref[:, sl] = f["rk"] * (d_k - kh * rowsum(d_k * kh))
            return d_g, d_beta

        per_head = yield from _interleave([head(h) for h in range(DN_HEADS)])
        dgcum_t = jnp.zeros((DN_BLOCK, 128), F32)
        dbeta_t = jnp.zeros((DN_BLOCK, 128), F32)
        for h, (d_g, d_beta) in enumerate(per_head):
            dgcum_t = jnp.where(lane == DN_HEADS + h, d_g, dgcum_t)
            dbeta_t = jnp.where(lane == h, d_beta, dbeta_t)
        dgraw_t = _dot_hi(mk.upper.astype(F32), dgcum_t)
        sp = _sigmoid(pre)
        d_pre = dgraw_t * coef * sp
        dba_ref[...] = jnp.where(lane < DN_HEADS, dbeta_t * beta_t * (1.0 - beta_t),
                                 jnp.where(lane < 2 * DN_HEADS, d_pre, 0.0))
        in_g = (lane >= DN_HEADS) & (lane < 2 * DN_HEADS)
        d_alog = jnp.sum(jnp.where(in_g, dgraw_t * graw_t, 0.0), axis=0, keepdims=True)
        d_dtb = jnp.sum(jnp.where(in_g, d_pre, 0.0), axis=0, keepdims=True)
        return jnp.concatenate([d_alog, d_dtb, jnp.zeros((6, 128), F32)], axis=0)

    sp = _dn_specs(rows)
    return _call_with_exchange(
        body, exchange, name=name, steps=n_chunks // group,
        in_specs=sp["qkv"] + [sp["ba"], sp["hp"], sp["state"]] + [sp["tok"]] * 4 + [sp["gate"]] + [sp["tok"]] * 3,
        out_specs=[pl.BlockSpec((rows, 3 * hw), lambda n: (n, 0)), pl.BlockSpec((rows, 128), lambda n: (n, 0)), sp["hp"]],
        out_shape=[jax.ShapeDtypeStruct((T, 3 * hw), F32), jax.ShapeDtypeStruct((T, 128), F32),
                   jax.ShapeDtypeStruct((8, 128), F32)],
        args=(qkv, qkv, qkv, proj, hp, sall, vn, do, dvn, dkt, dgl, inv, u, w))


def _mix_fwd(o, proj, dn_norm, sg_norm, sg_w, sg_bt, name):
    T = o.shape[0]
    tm = _tile(T, 512)
    hw = DN_HEADS * DN_DIM
    nc = tm // SG_CHUNK

    def body(o_ref, z_ref, su_ref, sv_ref, dnn_ref, sgn_ref, sgw_ref, sgb_ref, mix_ref):
        dnn = dnn_ref[...]
        for h in range(DN_HEADS):
            sl = slice(h * DN_DIM, (h + 1) * DN_DIM)
            xhat, _ = _rms_stats(o_ref[:, sl])
            z = z_ref[:, sl]
            mix_ref[:, sl] = _bf(xhat * dnn * (z * _sigmoid(z)))
        tri = lax.broadcasted_iota(jnp.int32, (SG_CHUNK, SG_CHUNK), 0) >= lax.broadcasted_iota(jnp.int32, (SG_CHUNK, SG_CHUNK), 1)
        for g in range(SG_GROUPS):
            sl = slice(g * SG_DIM, (g + 1) * SG_DIM)
            xhat, _ = _rms_stats(_gelu(sv_ref[:, sl]))
            svn = _bf(xhat * sgn_ref[g:g + 1, :])
            sua = _gelu(su_ref[:, sl])
            wt = _bf(jnp.where(tri, sgw_ref[g], 0.0))
            bias = sgb_ref[:, g:g + 1]
            for c in range(nc):
                rows = slice(c * SG_CHUNK, (c + 1) * SG_CHUNK)
                mixed = _dot(wt, svn[rows, :]) + bias
                mix_ref[rows, hw + g * SG_DIM:hw + (g + 1) * SG_DIM] = _bf(sua[rows, :] * mixed)

    full = lambda shape: pl.BlockSpec(shape, lambda i: (0,) * len(shape))
    return pl.pallas_call(
        body, name=name, grid=(T // tm,),
        in_specs=[pl.BlockSpec((tm, hw), lambda i: (i, 0)),
                  pl.BlockSpec((tm, hw), lambda i: (i, 3)),
                  pl.BlockSpec((tm, hw), lambda i: (i, 4)),
                  pl.BlockSpec((tm, hw), lambda i: (i, 5)),
                  full((1, DN_DIM)), full((SG_GROUPS, SG_DIM)), full((SG_GROUPS, SG_CHUNK, SG_CHUNK)),
                  full((SG_CHUNK, 128))],
        out_specs=pl.BlockSpec((tm, 2 * hw), lambda i: (i, 0)),
        out_shape=jax.ShapeDtypeStruct((T, 2 * hw), BF16),
        compiler_params=_params(("parallel",)),
    )(o, proj, proj, proj, dn_norm, sg_norm, sg_w, sg_bt)


def _mix_bwd(dmix, o, proj, dn_norm, sg_norm, sg_w, sg_bt, name):
    T = o.shape[0]
    tm = _tile(T, 512)
    hw = DN_HEADS * DN_DIM
    nc = tm // SG_CHUNK

    def body(dm_ref, o_ref, z_ref, su_ref, sv_ref, dnn_ref, sgn_ref, sgw_ref, sgb_ref,
             do_ref, dz_ref, ddnn_ref, dsgn_ref, dsgw_ref, dsgb_ref):
        @pl.when(pl.program_id(0) == 0)
        def _():
            ddnn_ref[...] = jnp.zeros_like(ddnn_ref)
            dsgn_ref[...] = jnp.zeros_like(dsgn_ref)
            dsgw_ref[...] = jnp.zeros_like(dsgw_ref)
            dsgb_ref[...] = jnp.zeros_like(dsgb_ref)

        dnn = dnn_ref[...]
        ddnn = jnp.zeros((1, DN_DIM), F32)
        for h in range(DN_HEADS):
            sl = slice(h * DN_DIM, (h + 1) * DN_DIM)
            xhat, r = _rms_stats(o_ref[:, sl])
            z = z_ref[:, sl]
            sz = _sigmoid(z)
            doa = dm_ref[:, sl]
            dyn = doa * (z * sz)
            dz_ref[:, sl] = _bf(doa * xhat * dnn * _silu_grad(z, sz))
            do_ref[:, sl] = _rms_bwd(dyn, xhat, r, dnn)
            ddnn = ddnn + jnp.sum(dyn * xhat, axis=0, keepdims=True)
        ddnn_ref[...] += ddnn
        tri = lax.broadcasted_iota(jnp.int32, (SG_CHUNK, SG_CHUNK), 0) >= lax.broadcasted_iota(jnp.int32, (SG_CHUNK, SG_CHUNK), 1)
        lane = lax.broadcasted_iota(jnp.int32, (SG_CHUNK, 128), 1)
        dsgb = jnp.zeros((SG_CHUNK, 128), F32)
        dsgn_rows = []
        for g in range(SG_GROUPS):
            sl = slice(g * SG_DIM, (g + 1) * SG_DIM)
            sv = sv_ref[:, sl]
            su = su_ref[:, sl]
            xhat, r = _rms_stats(_gelu(sv))
            sgn = sgn_ref[g:g + 1, :]
            svn = _bf(xhat * sgn)
            sua = _gelu(su)
            wt = _bf(jnp.where(tri, sgw_ref[g], 0.0))
            bias = sgb_ref[:, g:g + 1]
            dw = jnp.zeros((SG_CHUNK, SG_CHUNK), F32)
            db = jnp.zeros((SG_CHUNK, 1), F32)
            dsua, dsvn = [], []
            for c in range(nc):
                rows = slice(c * SG_CHUNK, (c + 1) * SG_CHUNK)
                mixed = _dot(wt, svn[rows, :]) + bias
                dob = dm_ref[rows, hw + g * SG_DIM:hw + (g + 1) * SG_DIM]
                dsua.append(dob * mixed)
                dmixed = dob * sua[rows, :]
                dmb = _bf(dmixed)
                dsvn.append(_dot_tn(wt, dmb))
                dw = dw + _dot_nt(dmb, svn[rows, :])
                db = db + jnp.sum(dmixed, axis=1, keepdims=True)
            dsua = jnp.concatenate(dsua, axis=0) if nc > 1 else dsua[0]
            dsvn = jnp.concatenate(dsvn, axis=0) if nc > 1 else dsvn[0]
            dz_ref[:, hw + g * SG_DIM:hw + (g + 1) * SG_DIM] = _bf(dsua * _gelu_grad(su))
            dz_ref[:, 2 * hw + g * SG_DIM:2 * hw + (g + 1) * SG_DIM] = _bf(_rms_bwd(dsvn, xhat, r, sgn) * _gelu_grad(sv))
            dsgn_rows.append(jnp.sum(dsvn * xhat, axis=0, keepdims=True))
            dsgw_ref[g] += jnp.where(tri, dw, 0.0)
            dsgb = jnp.where(lane == g, db, dsgb)
        dsgn_ref[...] += jnp.concatenate(dsgn_rows, axis=0)
        dsgb_ref[...] += dsgb

    full = lambda shape: pl.BlockSpec(shape, lambda i: (0,) * len(shape))
    return pl.pallas_call(
        body, name=name, grid=(T // tm,),
        in_specs=[pl.BlockSpec((tm, 2 * hw), lambda i: (i, 0)),
                  pl.BlockSpec((tm, hw), lambda i: (i, 0)),
                  pl.BlockSpec((tm, hw), lambda i: (i, 3)),
                  pl.BlockSpec((tm, hw), lambda i: (i, 4)),
                  pl.BlockSpec((tm, hw), lambda i: (i, 5)),
                  full((1, DN_DIM)), full((SG_GROUPS, SG_DIM)), full((SG_GROUPS, SG_CHUNK, SG_CHUNK)),
                  full((SG_CHUNK, 128))],
        out_specs=[pl.BlockSpec((tm, hw), lambda i: (i, 0)),
                   pl.BlockSpec((tm, 3 * hw), lambda i: (i, 0)),
                   full((1, DN_DIM)), full((SG_GROUPS, SG_DIM)), full((SG_GROUPS, SG_CHUNK, SG_CHUNK)),
                   full((SG_CHUNK, 128))],
        out_shape=[jax.ShapeDtypeStruct((T, hw), F32), jax.ShapeDtypeStruct((T, 3 * hw), BF16),
                   jax.ShapeDtypeStruct((1, DN_DIM), F32), jax.ShapeDtypeStruct((SG_GROUPS, SG_DIM), F32),
                   jax.ShapeDtypeStruct((SG_GROUPS, SG_CHUNK, SG_CHUNK), F32),
                   jax.ShapeDtypeStruct((SG_CHUNK, 128), F32)],
        compiler_params=_params(("arbitrary",)),
    )(dmix, o, proj, proj, proj, dn_norm, sg_norm, sg_w, sg_bt)


def _window_sums(h, sign):
    sums, s, w = {}, h, 1
    while w < POOL_WINDOWS[-1]:
        s = s + _shift_rows(s, sign * w)
        w *= 2
        sums[w] = s
    return sums


def _pool_counts(t_global):
    return [jnp.minimum(t_global + 1, win).astype(F32) for win in POOL_WINDOWS]


def _pooled_groups(ext_h, row0, tm):
    sums = _window_sums(ext_h, 1)
    t_global = row0 + lax.broadcasted_iota(jnp.int32, (tm, 1), 0)
    counts = _pool_counts(t_global)
    out = []
    for gi, win in enumerate(POOL_WINDOWS):
        cols = slice(gi * POOL_DIM, (gi + 1) * POOL_DIM)
        out.append(sums[win][POOL_HALO:, cols] / counts[gi] - ext_h[POOL_HALO:, cols])
    return out


def _pool_fwd(x, nw, pool_w, pool_scale, layer, name):
    T, D = x.shape
    tm = _tile(T, 256)
    hb = tm // POOL_HALO

    def body(x_ref, xp_ref, n_ref, w_ref, s_ref, xo_ref):
        i = pl.program_id(0)
        prev = jnp.where(i == 0, 0.0, xp_ref[...])
        ext = jnp.concatenate([prev, x_ref[...]], axis=0)
        xhat, _ = _rms_stats(ext)
        pooled = _pooled_groups(xhat * n_ref[...], i * tm, tm)
        for gi in range(len(POOL_WINDOWS)):
            cols = slice(gi * POOL_DIM, (gi + 1) * POOL_DIM)
            xo_ref[:, cols] = x_ref[:, cols] + _dot(_bf(pooled[gi]), w_ref[gi]) * s_ref[:, cols]

    return pl.pallas_call(
        body, name=name, grid=(T // tm,),
        in_specs=[pl.BlockSpec((tm, D), lambda i: (i, 0)),
                  pl.BlockSpec((POOL_HALO, D), lambda i: (jnp.maximum(i * hb - 1, 0), 0)),
                  pl.BlockSpec((None, 1, D), lambda i: (layer, 0, 0)),
                  pl.BlockSpec(pool_w.shape, lambda i: (0, 0, 0)),
                  pl.BlockSpec((1, D), lambda i: (0, 0))],
        out_specs=pl.BlockSpec((tm, D), lambda i: (i, 0)),
        out_shape=jax.ShapeDtypeStruct((T, D), F32),
        compiler_params=_params(("parallel",)),
    )(x, x, nw, pool_w, pool_scale)


def _pool_bwd(dxo, x, nw, pool_w, pool_scale, layer, name):
    T, D = x.shape
    tm = _tile(T, 256)
    hb = tm // POOL_HALO
    nt = T // tm
    ng = len(POOL_WINDOWS)

    def body(dxo_ref, dxn_ref, x_ref, xp_ref, n_ref, w_ref, s_ref, dx_ref, dw_ref, ds_ref, dn_ref):
        i = pl.program_id(0)

        @pl.when(i == 0)
        def _():
            dw_ref[...] = jnp.zeros_like(dw_ref)
            ds_ref[...] = jnp.zeros_like(ds_ref)
            dn_ref[...] = jnp.zeros_like(dn_ref)

        prev = jnp.where(i == 0, 0.0, xp_ref[...])
        ext = jnp.concatenate([prev, x_ref[...]], axis=0)
        xhat_ext, r_ext = _rms_stats(ext)
        nv = n_ref[...]
        pooled = _pooled_groups(xhat_ext * nv, i * tm, tm)
        dxo = dxo_ref[...]
        scale = s_ref[...]
        dout_ext = jnp.concatenate([dxo, jnp.where(i == nt - 1, 0.0, dxn_ref[...])], axis=0) * scale
        t_ext = i * tm + lax.broadcasted_iota(jnp.int32, (tm + POOL_HALO, 1), 0)
        counts = _pool_counts(t_ext)
        dh_cols, ds_cols = [], []
        for gi, win in enumerate(POOL_WINDOWS):
            cols = slice(gi * POOL_DIM, (gi + 1) * POOL_DIM)
            wg = w_ref[gi]
            pb = _bf(pooled[gi])
            doutb = _bf(dout_ext[:, cols])
            dpooled = _dot_nt(doutb, wg)
            ahead = _window_sums(dpooled / counts[gi], -1)[win]
            dh_cols.append(ahead[:tm, :] - dpooled[:tm, :])
            dw_ref[gi] += _dot_tn(pb, doutb[:tm, :])
            ds_cols.append(jnp.sum(dxo[:, cols] * _dot(pb, wg), axis=0, keepdims=True))
        dh = jnp.concatenate(dh_cols, axis=1)
        xhat, r = xhat_ext[POOL_HALO:, :], r_ext[POOL_HALO:, :]
        dx_ref[...] = dxo + _rms_bwd(dh, xhat, r, nv)
        dn_ref[...] += jnp.sum(dh * xhat, axis=0, keepdims=True)
        ds_ref[...] += jnp.concatenate(ds_cols, axis=1)

    last_halo = T // POOL_HALO - 1
    return pl.pallas_call(
        body, name=name, grid=(nt,),
        in_specs=[pl.BlockSpec((tm, D), lambda i: (i, 0)),
                  pl.BlockSpec((POOL_HALO, D), lambda i: (jnp.minimum((i + 1) * hb, last_halo), 0)),
                  pl.BlockSpec((tm, D), lambda i: (i, 0)),
                  pl.BlockSpec((POOL_HALO, D), lambda i: (jnp.maximum(i * hb - 1, 0), 0)),
                  pl.BlockSpec((None, 1, D), lambda i: (layer, 0, 0)),
                  pl.BlockSpec(pool_w.shape, lambda i: (0, 0, 0)),
                  pl.BlockSpec((1, D), lambda i: (0, 0))],
        out_specs=[pl.BlockSpec((tm, D), lambda i: (i, 0)),
                   pl.BlockSpec((ng, POOL_DIM, POOL_DIM), lambda i: (0, 0, 0)),
                   pl.BlockSpec((1, D), lambda i: (0, 0)),
                   pl.BlockSpec((1, D), lambda i: (0, 0))],
        out_shape=[jax.ShapeDtypeStruct((T, D), F32), jax.ShapeDtypeStruct((ng, POOL_DIM, POOL_DIM), F32),
                   jax.ShapeDtypeStruct((1, D), F32), jax.ShapeDtypeStruct((1, D), F32)],
        compiler_params=_params(("arbitrary",)),
    )(dxo, dxo, x, x, nw, pool_w, pool_scale)


def _loss_head(x, target, fn, name):
    T, D = x.shape
    tm = _tile(T, 512)

    def body(x_ref, t_ref, n_ref, loss_ref, dx_ref, dn_ref):
        @pl.when(pl.program_id(0) == 0)
        def _():
            loss_ref[...] = jnp.zeros_like(loss_ref)
            dn_ref[...] = jnp.zeros_like(dn_ref)

        xhat, r = _rms_stats(x_ref[...])
        nv = n_ref[...]
        err = xhat * nv - t_ref[...]
        part = jnp.sum(jnp.sum(err * err, axis=1, keepdims=True), axis=0, keepdims=True)
        loss_ref[...] += 0.5 * part / D
        dy = err / D
        dx_ref[...] = _rms_bwd(dy, xhat, r, nv)
        dn_ref[...] += jnp.sum(dy * xhat, axis=0, keepdims=True)

    row = pl.BlockSpec((tm, D), lambda i: (i, 0))
    return pl.pallas_call(
        body, name=name, grid=(T // tm,),
        in_specs=[row, row, pl.BlockSpec((1, D), lambda i: (0, 0))],
        out_specs=[pl.BlockSpec((1, 1), lambda i: (0, 0)), row, pl.BlockSpec((1, D), lambda i: (0, 0))],
        out_shape=[jax.ShapeDtypeStruct((1, 1), F32), jax.ShapeDtypeStruct((T, D), F32),
                   jax.ShapeDtypeStruct((1, D), F32)],
        compiler_params=_params(("arbitrary",)),
    )(x, target, fn)


def _adamw(w, g, m, v, name):
    R, C = w.shape
    br = R
    for cand in (512, 256, 128, 64, 32, 16, 8):
        if R % cand == 0 and cand * C * 4 <= 2 * 1024 * 1024:
            br = cand
            break

    def body(w_ref, g_ref, m_ref, v_ref, d_ref, mo_ref, vo_ref):
        gv = g_ref[...]
        m_new = ADAM_B1 * m_ref[...] + (1.0 - ADAM_B1) * gv
        v_new = ADAM_B2 * v_ref[...] + (1.0 - ADAM_B2) * (gv * gv)
        m_hat = m_new / (1.0 - ADAM_B1 ** ADAM_STEP)
        v_hat = v_new / (1.0 - ADAM_B2 ** ADAM_STEP)
        d_ref[...] = -ADAM_LR * (m_hat / (jnp.sqrt(v_hat) + ADAM_EPS) + ADAM_WD * w_ref[...])
        mo_ref[...] = m_new
        vo_ref[...] = v_new

    blk = pl.BlockSpec((br, C), lambda i: (i, 0))
    return pl.pallas_call(
        body, name=name, grid=(R // br,), in_specs=[blk] * 4, out_specs=[blk] * 3,
        out_shape=[jax.ShapeDtypeStruct((R, C), F32)] * 3,
        compiler_params=_params(("parallel",)),
    )(w, g, m, v)


def _mesh_pos():
    return lax.axis_index("x"), lax.axis_index("y"), lax.axis_index("c")


def _other_chips(x, y):
    return [(1 - x, y), (x, 1 - y), (1 - x, 1 - y)]


def _half_of(ref, shape, h):
    size = shape[0] // 2
    return ref.at[pl.ds(h * size, size)]


class _ChipGather:
    def __init__(self, shards, split):
        self.shards, self.split = list(shards), list(split)
        self.operands = self.shards
        n = len(self.shards)
        self.out_shape = [jax.ShapeDtypeStruct((N_CHIPS,) + s.shape, s.dtype) for s in self.shards]
        self.scratch = [pltpu.SemaphoreType.DMA((n, 3))] * 4

    def _piece(self, a, ref, h):
        return _half_of(ref, self.shards[a].shape, h) if self.split[a] else ref

    def start(self, ins, outs, sems):
        send_sems, recv_sems = sems[0], sems[1]
        x, y, c = _mesh_pos()
        me = 2 * x + y
        for a in range(len(ins)):
            for k, (px, py) in enumerate(_other_chips(x, y)):
                pltpu.make_async_remote_copy(self._piece(a, ins[a], c), self._piece(a, outs[a].at[me], c),
                                             send_sems.at[a, k], recv_sems.at[a, k],
                                             device_id=(px, py, c), device_id_type=MESH).start()

    def finish(self, ins, outs, sems):
        send_sems, recv_sems, fwd_send_sems, fwd_recv_sems = sems
        x, y, c = _mesh_pos()
        sibling = (x, y, 1 - c)
        chips = _other_chips(x, y)
        n = len(ins)
        forwards = []
        for a in range(n):
            for k, (px, py) in enumerate(chips):
                landed = self._piece(a, outs[a].at[2 * px + py], c)
                pltpu.make_async_remote_copy(landed, landed, send_sems.at[a, k], recv_sems.at[a, k],
                                             device_id=(px, py, c), device_id_type=MESH).wait_recv()
                if self.split[a]:
                    fwd = pltpu.make_async_remote_copy(landed, landed, fwd_send_sems.at[a, k], fwd_recv_sems.at[a, k],
                                                       device_id=sibling, device_id_type=MESH)
                    fwd.start()
                    forwards.append(fwd)
        for a in range(n):
            if self.split[a]:
                for k, (px, py) in enumerate(chips):
                    other = self._piece(a, outs[a].at[2 * px + py], 1 - c)
                    pltpu.make_async_remote_copy(other, other, fwd_send_sems.at[a, k], fwd_recv_sems.at[a, k],
                                                 device_id=sibling, device_id_type=MESH).wait_recv()
        for a in range(n):
            for k, (px, py) in enumerate(chips):
                sent = self._piece(a, ins[a], c)
                pltpu.make_async_remote_copy(sent, sent, send_sems.at[a, k], recv_sems.at[a, k],
                                             device_id=(px, py, c), device_id_type=MESH).wait_send()
        for fwd in forwards:
            fwd.wait_send()

    def finalize(self, gathered):
        x, y, _ = _mesh_pos()
        return [lax.dynamic_update_index_in_dim(g, s, 2 * x + y, 0) for g, s in zip(gathered, self.shards)]

    def run(self, name):
        n = len(self.shards)

        def body(*refs):
            ins, outs, sems = refs[:n], refs[n:2 * n], refs[2 * n:]
            self.start(ins, outs, sems)
            self.finish(ins, outs, sems)

        gathered = pl.pallas_call(
            body, name=name, in_specs=[ANY] * n, out_specs=[ANY] * n, out_shape=self.out_shape,
            scratch_shapes=self.scratch, compiler_params=pltpu.CompilerParams(has_side_effects=True),
        )(*self.shards)
        return self.finalize(gathered)


def _ffn_weight_grads(hb, dgu, a, dyb, tag):
    dwin = _matmul_tn(hb, dgu, D_MODEL, FF_CHUNK, f"{tag}_dw_in", stack_n=True)
    dwo = _matmul_tn(a, dyb, FF_CHUNK, D_MODEL, f"{tag}_dw_out")
    return dwin, dwo.reshape(N_CHIPS, D_FF // N_CHIPS, D_MODEL)


def _local_step(x, target, w, late=None, reduce=False):
    g = {}
    acts = []
    w = dict(w)

    def ffn_weights(which, layer):
        return w[f"n{which}"], w[f"win{which}_l{layer}"], w[f"wout{which}_l{layer}"]

    def hosting(name):
        exchange, layouts = late.get(name, (None, None)) if late else (None, None)
        return exchange, (lambda arrived: w.update(layouts(arrived)) if exchange is not None else None)

    def ffn(xin, which, layer):
        name = f"ffn{which}_l{layer}_fwd"
        exchange, keep = hosting(name)
        (xo, gv, uv, hb), arrived = _ffn_fwd(xin, *ffn_weights(which, layer), layer, name, exchange)
        keep(arrived)
        acts.append((xin, gv, uv, hb))
        return xo

    x1 = ffn(x, 1, 0)
    hb_mix = _rms_fwd_call(x1, w["nmix"], 0, "ab_norm_fwd")
    proj = _matmul(hb_mix, w["wp"], "ab_in_proj")
    qkv = _conv_fwd(proj, w["conv_w"], "dn_conv_fwd")
    exchange, keep = hosting("dn_prep")
    (dn_u, dn_w, dn_p, dn_qd, dn_kt, dn_gl, dn_inv), arrived = _dn_prep(qkv, proj, w["hp"], "dn_prep", exchange)
    keep(arrived)
    exchange, keep = hosting("dn_scan")
    (o, dn_vn, sall), arrived = _dn_scan(dn_u, dn_w, dn_p, dn_qd, dn_kt, dn_gl, "dn_scan", exchange)
    keep(arrived)
    mix = _mix_fwd(o, proj, w["dn_norm"], w["sg_norm"], w["sg_w"], w["sg_bt"], "ab_gate_fwd")
    x2 = _matmul(mix, w["wo"], "ab_out_proj", res=x1)
    x3 = ffn(x2, 2, 0)
    x4 = ffn(x3, 1, 1)
    x5 = _pool_fwd(x4, w["nmix"], w["pool_w"], w["pool_scale"], 1, "pool_fwd")
    x6 = ffn(x5, 2, 1)
    loss, dx, g["fn"] = _loss_head(x6, target, w["fn"], "loss_head")

    dn = {1: [None, None], 2: [None, None]}
    dwin = {1: [None, None], 2: [None, None]}
    dwout = {1: [None, None], 2: [None, None]}

    def ffn_back(dxo, which, layer, saved, exchange=None):
        nw, win, wout = ffn_weights(which, layer)
        xin, gv, uv, hb = saved
        tag = f"ffn{which}_l{layer}"
        (dxi, dgu, a, dyb, dnw), arrived = _ffn_bwd(dxo, xin, nw, gv, uv, win, wout, layer, f"{tag}_bwd", exchange)
        dn[which][layer] = dnw
        dwin[which][layer], dwout[which][layer] = _ffn_weight_grads(hb, dgu, a, dyb, tag)
        return dxi, arrived

    for which in (1, 2):
        g[f"win{which}"] = dwin[which]
        g[f"wout{which}"] = dwout[which]
    reduced = {}

    def open_round(tag, keys):
        have = _sharded_grads(g)
        return _GradRound(tag, {k: have[k] for k in keys})

    dx, _ = ffn_back(dx, 2, 1, acts[3])
    dx, g["pool_w"], g["pool_scale"], dnmix1 = _pool_bwd(dx, x4, w["nmix"], w["pool_w"], w["pool_scale"], 1, "pool_bwd")
    dx, _ = ffn_back(dx, 1, 1, acts[2])
    round_a = open_round("a", REDUCE_ROUNDS[0]) if reduce else None
    dx2, arrived = ffn_back(dx, 2, 0, acts[1], round_a.swap if reduce else None)
    if reduce:
        round_a.pair_sum(arrived)
    round_b = open_round("b", REDUCE_ROUNDS[1]) if reduce else None
    dmix = _matmul(dx2, w["wo"], "ab_out_proj_bwd", trans_b=True)
    g["wo"] = _matmul_tn(mix, dx2, D_MODEL, D_MODEL, "ab_out_proj_dw")
    do, dzuv, g["dn_norm"], g["sg_norm"], g["sg_w"], g["sg_bt"] = _mix_bwd(
        dmix, o, proj, w["dn_norm"], w["sg_norm"], w["sg_w"], w["sg_bt"], "ab_gate_bwd")
    (dvn, dkt, dgl), arrived = _dn_scan_bwd(dn_w, dn_p, dn_qd, dn_kt, dn_gl, dn_vn, sall, do, "dn_scan_bwd",
                                            round_b.swap if reduce else None)
    if reduce:
        round_b.pair_sum(arrived)
    (dqkv_act, dba, g["hp"]), arrived = _dn_prep_bwd(qkv, proj, w["hp"], sall, dn_vn, do, dvn, dkt, dgl, dn_inv, dn_u, dn_w,
                                                     "dn_prep_bwd", round_b.scatter if reduce else None)
    if reduce:
        reduced.update(round_b.finish(arrived))
    dqkv, g["conv_w"] = _conv_bwd(dqkv_act, proj, w["conv_w"], "dn_conv_bwd")
    dproj = jnp.concatenate([dqkv, dzuv, dba.astype(BF16)], axis=1)
    dh = _matmul(dproj, w["wp"], "ab_in_proj_bwd", trans_b=True)
    g["wp"] = _matmul_tn(hb_mix, dproj, D_MODEL, 640, "ab_in_proj_dw")
    dx1, dnmix0 = _rms_bwd_call(dh, x1, w["nmix"], dx2, 0, "ab_norm_bwd")
    dx0, arrived = ffn_back(dx1, 1, 0, acts[0], round_a.scatter if reduce else None)
    if reduce:
        reduced.update(round_a.finish(arrived))
        round_c = open_round("c", REDUCE_ROUNDS[2])
        round_c.pair_sum(round_c.swap.run("grad_c_pair_swap"))
        reduced.update(round_c.finish(round_c.scatter.run("grad_c_chip_scatter")))

    g["n1"] = jnp.concatenate(dn[1], axis=0)
    g["n2"] = jnp.concatenate(dn[2], axis=0)
    g["nmix"] = jnp.concatenate([dnmix0, dnmix1], axis=0)
    return loss, dx0, g, reduced


SHARDED = ("ffn1_w_in", "ffn1_w_out", "ffn2_w_in", "ffn2_w_out", "ab_w_in", "ab_w_out", "pool_w", "dn_conv_w", "pool_scale")
REPLICATED = ("ffn_norm1", "mix_norm", "ffn_norm2", "dn_a_log", "dn_dt_bias", "dn_out_norm", "sg_norm", "sg_w", "sg_b", "final_norm")
QKVZ = 4 * DN_HEADS * DN_DIM
N_GATES = 2 * DN_HEADS
IN_PROJ = QKVZ + N_GATES + 2 * SG_GROUPS * SG_DIM


def _shard_pieces(wts, keys):
    out = []
    for n, layer in keys:
        a = wts[n][0 if layer is None else layer]
        a = a[None] if a.ndim == 1 else a
        out.append(a.astype(BF16) if n in MATRICES else a)
    return out


def _replicated_layouts(rep):
    per_layer = lambda a: a.reshape(a.shape[0], 1, D_MODEL)
    w = {"n1": per_layer(rep["ffn_norm1"]), "nmix": per_layer(rep["mix_norm"]), "n2": per_layer(rep["ffn_norm2"])}
    hp = jnp.zeros((8, 128), F32)
    w["hp"] = hp.at[0, DN_HEADS:N_GATES].set(rep["dn_a_log"][0]).at[1, DN_HEADS:N_GATES].set(rep["dn_dt_bias"][0])
    w["dn_norm"] = rep["dn_out_norm"]
    w["sg_norm"] = rep["sg_norm"][0]
    w["sg_w"] = rep["sg_w"][0]
    w["sg_bt"] = jnp.zeros((SG_CHUNK, 128), F32).at[:, :SG_GROUPS].set(rep["sg_b"][0].T)
    w["fn"] = rep["final_norm"].reshape(1, D_MODEL)
    return w


def _layouts_from(gathered):
    w = {}
    for (n, layer), a in gathered.items():
        if n in ("ffn1_w_in", "ffn2_w_in"):
            w[f"win{n[3]}_l{layer}"] = a
        elif n in ("ffn1_w_out", "ffn2_w_out"):
            w[f"wout{n[3]}_l{layer}"] = a
        elif n == "ab_w_in":
            ab_in = jnp.transpose(a, (1, 0, 2)).reshape(D_MODEL, IN_PROJ)
            w["wp"] = jnp.concatenate([ab_in[:, :QKVZ], ab_in[:, QKVZ + N_GATES:], ab_in[:, QKVZ:QKVZ + N_GATES],
                                       jnp.zeros((D_MODEL, PROJ_W - IN_PROJ), ab_in.dtype)], axis=1)
        elif n == "dn_conv_w":
            w["conv_w"] = jnp.transpose(a, (1, 0, 2)).reshape(DN_CONV, 3 * DN_HEADS * DN_DIM)
        elif n == "ab_w_out":
            w["wo"] = a.reshape(D_MODEL, D_MODEL)
        elif n == "pool_w":
            w["pool_w"] = jnp.transpose(a, (1, 0, 2, 3)).reshape(len(POOL_WINDOWS), POOL_DIM, POOL_DIM)
        elif n == "pool_scale":
            w["pool_scale"] = a.reshape(1, D_MODEL)
    return w


def _sharded_grads(g):
    nw = len(POOL_WINDOWS)
    sharded = {}
    for n, key in (("ffn1_w_in", "win1"), ("ffn1_w_out", "wout1"), ("ffn2_w_in", "win2"), ("ffn2_w_out", "wout2")):
        for layer, a in enumerate(g.get(key, ())):
            if a is not None:
                sharded[(n, layer)] = a
    if "wp" in g:
        wp = g["wp"]
        ab_in = jnp.concatenate([wp[:, :QKVZ], wp[:, IN_PROJ - N_GATES:IN_PROJ], wp[:, QKVZ:IN_PROJ - N_GATES]], axis=1)
        sharded[("ab_w_in", None)] = jnp.transpose(ab_in.reshape(D_MODEL, N_CHIPS, IN_PROJ // N_CHIPS), (1, 0, 2))
    if "wo" in g:
        sharded[("ab_w_out", None)] = g["wo"].reshape(N_CHIPS, D_MODEL // N_CHIPS, D_MODEL)
    if "pool_w" in g:
        sharded[("pool_w", None)] = jnp.transpose(g["pool_w"].reshape(nw, N_CHIPS, POOL_DIM // N_CHIPS, POOL_DIM), (1, 0, 2, 3))
    if "conv_w" in g:
        sharded[("dn_conv_w", None)] = jnp.transpose(g["conv_w"].reshape(DN_CONV, N_CHIPS, -1), (1, 0, 2))
    if "pool_scale" in g:
        sharded[("pool_scale", None)] = g["pool_scale"].reshape(N_CHIPS, 1, D_MODEL // N_CHIPS)
    return sharded


def _replicated_grads(g):
    rep = {
        "ffn_norm1": g["n1"], "mix_norm": g["nmix"], "ffn_norm2": g["n2"],
        "dn_a_log": g["hp"][0:1, DN_HEADS:N_GATES], "dn_dt_bias": g["hp"][1:2, DN_HEADS:N_GATES],
        "dn_out_norm": g["dn_norm"], "sg_norm": g["sg_norm"][None], "sg_w": g["sg_w"][None],
        "sg_b": g["sg_bt"][:, :SG_GROUPS].T[None], "final_norm": g["fn"].reshape(D_MODEL),
    }
    return rep


def _as_halves(a):
    shape = a.shape[1:]
    if len(shape) >= 2 and shape[0] % 2 == 0:
        return a.reshape(N_CHIPS, 2, -1, shape[-1])
    return a.reshape(N_CHIPS, 2, 1, -1)


def _row_block(rows):
    for cand in (256, 176, 128, 64, 32, 16):
        if rows % cand == 0:
            return cand
    return rows


def _from_halves(pairs, core, shape):
    mine_first = jnp.stack([t for mine, other in pairs for t in (mine, other)])
    other_first = jnp.stack([t for mine, other in pairs for t in (other, mine)])
    return jnp.where(core == 0, mine_first, other_first).reshape(shape)


class _PairSwap:
    def __init__(self, packs):
        self.operands = list(packs)
        n = len(self.operands)
        self.out_shape = [jax.ShapeDtypeStruct((p.shape[0],) + p.shape[2:], p.dtype) for p in self.operands]
        self.scratch = [pltpu.SemaphoreType.DMA((n,))] * 2

    def _copies(self, ins, outs, sems):
        x, y, c = _mesh_pos()
        return [pltpu.make_async_remote_copy(ins[k].at[:, 1 - c], outs[k], sems[0].at[k], sems[1].at[k],
                                             device_id=(x, y, 1 - c), device_id_type=MESH) for k in range(len(ins))]

    def start(self, ins, outs, sems):
        for cp in self._copies(ins, outs, sems):
            cp.start()

    def finish(self, ins, outs, sems):
        for cp in self._copies(ins, outs, sems):
            cp.wait()

    def finalize(self, results):
        return results

    def run(self, name):
        n = len(self.operands)

        def body(*refs):
            ins, outs, sems = refs[:n], refs[n:2 * n], refs[2 * n:]
            self.start(ins, outs, sems)
            self.finish(ins, outs, sems)

        return pl.pallas_call(
            body, name=name, in_specs=[ANY] * n, out_specs=[ANY] * n, out_shape=self.out_shape, scratch_shapes=self.scratch,
            compiler_params=pltpu.CompilerParams(has_side_effects=True),
        )(*self.operands)


def _add_pair(pack, recv, core, name):
    nchip, _, rows, lanes = pack.shape
    rb = _row_block(rows)

    def body(c_ref, a_ref, b_ref, o32_ref, o16_ref):
        s = a_ref[...] + b_ref[...]
        o32_ref[...] = s
        o16_ref[...] = _bf(s)

    blk = pl.BlockSpec((None, rb, lanes), lambda p, i, c: (p, i, 0))
    return pl.pallas_call(
        body, name=name,
        grid_spec=pltpu.PrefetchScalarGridSpec(
            num_scalar_prefetch=1, grid=(nchip, rows // rb),
            in_specs=[pl.BlockSpec((None, None, rb, lanes), lambda p, i, c: (p, c[0], i, 0)), blk],
            out_specs=[blk, blk]),
        out_shape=[jax.ShapeDtypeStruct((nchip, rows, lanes), F32), jax.ShapeDtypeStruct((nchip, rows, lanes), BF16)],
        compiler_params=_params(("parallel", "parallel")),
    )(core, pack, recv)


class _ChipScatter:
    def __init__(self, parts16):
        self.operands = list(parts16)
        n = len(self.operands)
        self.out_shape = [jax.ShapeDtypeStruct((N_CHIPS - 1,) + p.shape[1:], p.dtype) for p in self.operands]
        self.scratch = [pltpu.SemaphoreType.DMA((n, N_CHIPS - 1))] * 2

    def _copies(self, ins, outs, sems):
        x, y, c = _mesh_pos()
        return [pltpu.make_async_remote_copy(ins[a].at[2 * px + py], outs[a].at[k], sems[0].at[a, k], sems[1].at[a, k],
                                             device_id=(px, py, c), device_id_type=MESH)
                for a in range(len(ins)) for k, (px, py) in enumerate(_other_chips(x, y))]

    def start(self, ins, outs, sems):
        for cp in self._copies(ins, outs, sems):
            cp.start()

    def finish(self, ins, outs, sems):
        for cp in self._copies(ins, outs, sems):
            cp.wait()

    def finalize(self, results):
        return results

    def run(self, name):
        n = len(self.operands)

        def body(*refs):
            ins, outs, sems = refs[:n], refs[n:2 * n], refs[2 * n:]
            self.start(ins, outs, sems)
            self.finish(ins, outs, sems)

        return pl.pallas_call(
            body, name=name, in_specs=[ANY] * n, out_specs=[ANY] * n, out_shape=self.out_shape, scratch_shapes=self.scratch,
            compiler_params=pltpu.CompilerParams(has_side_effects=True),
        )(*self.operands)


def _sum_chips(part32, recv16, chip, name):
    nchip, rows, lanes = part32.shape
    rb = _row_block(rows)

    def body(p_ref, own_ref, r_ref, o_ref):
        s = own_ref[...]
        for k in range(nchip - 1):
            s = s + r_ref[k].astype(F32)
        o_ref[...] = s

    return pl.pallas_call(
        body, name=name,
        grid_spec=pltpu.PrefetchScalarGridSpec(
            num_scalar_prefetch=1, grid=(rows // rb,),
            in_specs=[pl.BlockSpec((None, rb, lanes), lambda i, p: (p[0], i, 0)),
                      pl.BlockSpec((nchip - 1, rb, lanes), lambda i, p: (0, i, 0))],
            out_specs=pl.BlockSpec((rb, lanes), lambda i, p: (i, 0))),
        out_shape=jax.ShapeDtypeStruct((rows, lanes), F32),
        compiler_params=_params(("parallel",)),
    )(chip, part32, recv16)


def _share_with_sibling(halves, name):
    n = len(halves)

    def body(*refs):
        ins, outs, send_sems, recv_sems = refs[:n], refs[n:2 * n], refs[2 * n], refs[2 * n + 1]
        x, y, c = _mesh_pos()
        copies = [pltpu.make_async_remote_copy(ins[k], outs[k], send_sems.at[k], recv_sems.at[k],
                                               device_id=(x, y, 1 - c), device_id_type=MESH) for k in range(n)]
        for cp in copies:
            cp.start()
        for cp in copies:
            cp.wait()

    return pl.pallas_call(
        body, name=name, in_specs=[ANY] * n, out_specs=[ANY] * n,
        out_shape=[jax.ShapeDtypeStruct(h.shape, h.dtype) for h in halves],
        scratch_shapes=[pltpu.SemaphoreType.DMA((n,)), pltpu.SemaphoreType.DMA((n,))],
        compiler_params=pltpu.CompilerParams(has_side_effects=True),
    )(*halves)


class _GradRound:
    def __init__(self, tag, pieces):
        self.tag, self.keys = tag, list(pieces)
        self.shapes = [pieces[k].shape[1:] for k in self.keys]
        self.packs = [_as_halves(pieces[k]) for k in self.keys]
        self.swap = _PairSwap(self.packs)

    def pair_sum(self, recvs):
        _, _, c = _mesh_pos()
        core = jnp.reshape(c, (1,)).astype(jnp.int32)
        sums = [_add_pair(p, r, core, f"grad_{self.tag}_pair_add_{i}") for i, (p, r) in enumerate(zip(self.packs, recvs))]
        self.parts32 = [s[0] for s in sums]
        self.scatter = _ChipScatter([s[1] for s in sums])

    def finish(self, recvs16):
        x, y, _ = _mesh_pos()
        chip = jnp.reshape(2 * x + y, (1,)).astype(jnp.int32)
        halves = [_sum_chips(p, r, chip, f"grad_{self.tag}_chip_sum_{i}") for i, (p, r) in enumerate(zip(self.parts32, recvs16))]
        others = _share_with_sibling(halves, f"grad_{self.tag}_pair_share")
        return dict(zip(self.keys, zip(halves, others)))


def _pack_small(vals):
    parts = []
    for n in REPLICATED:
        flat = vals[n].reshape(-1)
        rows = -(-flat.shape[0] // 128)
        rows = -(-rows // 8) * 8
        parts.append(jnp.pad(flat, (0, rows * 128 - flat.shape[0])).reshape(rows, 128))
    return jnp.concatenate(parts, axis=0)


def _unpack_small(pack, like):
    out, off = {}, 0
    for n in REPLICATED:
        size = like[n].size
        rows = -(-size // 128)
        rows = -(-rows // 8) * 8
        out[n] = pack[off:off + rows].reshape(-1)[:size].reshape(like[n].shape)
        off += rows
    return out


def _all_to_all_small(pack, name):
    rows, lanes = pack.shape
    flips = [(dx, dy, dc) for dx in (0, 1) for dy in (0, 1) for dc in (0, 1)][1:]

    def body(src_ref, out_ref, send_sems, recv_sems, local_sem):
        x, y, c = _mesh_pos()
        me = 4 * x + 2 * y + c
        loc = pltpu.make_async_copy(src_ref, out_ref.at[me], local_sem)
        loc.start()
        copies = []
        for k, (dx, dy, dc) in enumerate(flips):
            peer = (x ^ dx, y ^ dy, c ^ dc)
            cp = pltpu.make_async_remote_copy(src_ref, out_ref.at[me], send_sems.at[k], recv_sems.at[k],
                                              device_id=peer, device_id_type=MESH)
            cp.start()
            copies.append(cp)
        for k, (dx, dy, dc) in enumerate(flips):
            peer = (x ^ dx, y ^ dy, c ^ dc)
            pltpu.make_async_remote_copy(src_ref, out_ref.at[4 * peer[0] + 2 * peer[1] + peer[2]], send_sems.at[k],
                                         recv_sems.at[k], device_id=peer, device_id_type=MESH).wait_recv()
        for cp in copies:
            cp.wait_send()
        loc.wait()

    return pl.pallas_call(
        body, name=name, in_specs=[ANY], out_specs=ANY,
        out_shape=jax.ShapeDtypeStruct((8, rows, lanes), pack.dtype),
        scratch_shapes=[pltpu.SemaphoreType.DMA((7,)), pltpu.SemaphoreType.DMA((7,)), pltpu.SemaphoreType.DMA],
        compiler_params=pltpu.CompilerParams(has_side_effects=True),
    )(pack)


def _sum_devices(stack, name):
    ndev, rows, lanes = stack.shape

    def body(s_ref, o_ref):
        s = s_ref[0]
        for d in range(1, ndev):
            s = s + s_ref[d]
        o_ref[...] = s

    return pl.pallas_call(
        body, name=name, grid=(1,),
        in_specs=[pl.BlockSpec((ndev, rows, lanes), lambda i: (0, 0, 0))],
        out_specs=pl.BlockSpec((rows, lanes), lambda i: (0, 0)),
        out_shape=jax.ShapeDtypeStruct((rows, lanes), F32),
    )(stack)


WEIGHT_ORDER = ("ffn_norm1", "ffn1_w_in", "ffn1_w_out", "mix_norm", "ffn_norm2", "ffn2_w_in", "ffn2_w_out", "ab_w_in",
                "dn_conv_w", "dn_a_log", "dn_dt_bias", "dn_out_norm", "sg_norm", "sg_w", "sg_b", "ab_w_out", "pool_w",
                "pool_scale", "final_norm")
MATRICES = ("ffn1_w_in", "ffn1_w_out", "ffn2_w_in", "ffn2_w_out", "ab_w_in", "ab_w_out", "pool_w")
GATHER_FIRST = (("ffn1_w_in", 0), ("ffn1_w_out", 0))
GATHER_LATER = {"ffn1_l0_fwd": (("ab_w_in", None), ("dn_conv_w", None), ("ab_w_out", None)),
                "dn_prep": (("ffn2_w_in", 0), ("ffn2_w_out", 0)),
                "dn_scan": (("ffn1_w_in", 1), ("ffn1_w_out", 1)),
                "ffn2_l0_fwd": (("pool_w", None), ("pool_scale", None), ("ffn2_w_in", 1), ("ffn2_w_out", 1))}
REDUCE_ROUNDS = ((("ffn2_w_in", 1), ("ffn2_w_out", 1), ("ffn1_w_in", 1), ("ffn1_w_out", 1), ("pool_w", None), ("pool_scale", None)),
                 (("ffn2_w_in", 0), ("ffn2_w_out", 0)),
                 (("ffn1_w_in", 0), ("ffn1_w_out", 0), ("ab_w_out", None), ("ab_w_in", None), ("dn_conv_w", None)))


def _as_2d(a):
    return a.reshape(-1, a.shape[-1])


def kernel(x, ffn_norm1, ffn1_w_in, ffn1_w_out, mix_norm, ffn_norm2, ffn2_w_in, ffn2_w_out, ab_w_in, dn_conv_w, dn_a_log, dn_dt_bias, dn_out_norm, sg_norm, sg_w, sg_b, ab_w_out, pool_w, pool_scale, final_norm, loss_target, m_ffn_norm1, m_ffn1_w_in, m_ffn1_w_out, m_mix_norm, m_ffn_norm2, m_ffn2_w_in, m_ffn2_w_out, m_ab_w_in, m_dn_conv_w, m_dn_a_log, m_dn_dt_bias, m_dn_out_norm, m_sg_norm, m_sg_w, m_sg_b, m_ab_w_out, m_pool_w, m_pool_scale, m_final_norm, v_ffn_norm1, v_ffn1_w_in, v_ffn1_w_out, v_mix_norm, v_ffn_norm2, v_ffn2_w_in, v_ffn2_w_out, v_ab_w_in, v_dn_conv_w, v_dn_a_log, v_dn_dt_bias, v_dn_out_norm, v_sg_norm, v_sg_w, v_sg_b, v_ab_w_out, v_pool_w, v_pool_scale, v_final_norm):
    given = dict(locals())
    wts = {n: given[n] for n in WEIGHT_ORDER}
    mom_m = {n: given["m_" + n] for n in WEIGHT_ORDER}
    mom_v = {n: given["v_" + n] for n in WEIGHT_ORDER}

    rep = {n: wts[n] for n in REPLICATED}
    first = _ChipGather(_shard_pieces(wts, GATHER_FIRST), [n in MATRICES for n, _ in GATHER_FIRST])
    w = {**_replicated_layouts(rep), **_layouts_from(dict(zip(GATHER_FIRST, first.run("weight_gather_first"))))}
    late = {host: (_ChipGather(_shard_pieces(wts, keys), [n in MATRICES for n, _ in keys]),
                   functools.partial(lambda keys, arrived: _layouts_from(dict(zip(keys, arrived))), keys))
            for host, keys in GATHER_LATER.items()}

    loss, dx, g, reduced = _local_step(x[0], loss_target[0], w, reduce=True, late=late)
    g_rep = _replicated_grads(g)
    grads = {}
    core = lax.axis_index("c")
    for n in SHARDED:
        layers = [reduced[(n, layer)] for layer in range(wts[n].shape[0])] if (n, 0) in reduced else [reduced[(n, None)]]
        grads[n] = _from_halves(layers, core, wts[n].shape)
    small = _sum_devices(_all_to_all_small(_pack_small(g_rep), "grad_small_exchange"), "grad_small_sum")
    grads.update(_unpack_small(small, rep))

    delta, new_m, new_v = {}, {}, {}
    for n in SHARDED:
        d, m1, v1 = _adamw(_as_2d(wts[n]), _as_2d(grads[n]), _as_2d(mom_m[n]), _as_2d(mom_v[n]), f"adamw_{n}")
        delta[n], new_m[n], new_v[n] = (t.reshape(wts[n].shape) for t in (d, m1, v1))
    d, m1, v1 = _adamw(_pack_small(rep), small, _pack_small({n: mom_m[n] for n in REPLICATED}),
                       _pack_small({n: mom_v[n] for n in REPLICATED}), "adamw_replicated")
    for tgt, packed in ((delta, d), (new_m, m1), (new_v, v1)):
        tgt.update(_unpack_small(packed, rep))

    total = lax.psum(loss[0, 0], ("x", "y", "c"))
    outs = [total, dx[None]]
    for group in (grads, delta, new_m, new_v):
        outs.extend(group[n] for n in WEIGHT_ORDER)
    return tuple(outs)
```

```python
import functools

import jax
import jax.numpy as jnp
from jax import lax
from jax.experimental import pallas as pl
from jax.experimental.pallas import tpu as pltpu

F32, BF16 = jnp.float32, jnp.bfloat16
NORM_EPS = 1e-6
D_MODEL = 1024
D_FF = 2816
N_CHIPS = 4
FF_CHUNK = 2 * D_FF // N_CHIPS
DN_HEADS, DN_DIM, DN_CHUNK, DN_CONV = 4, 128, 64, 4
DN_BLOCK = 2 * DN_CHUNK
DN_PREP_CHUNKS = 8
DN_SCAN_CHUNKS = 8
SG_GROUPS, SG_DIM, SG_CHUNK = 4, 128, 128
POOL_WINDOWS = (2, 4, 8, 16)
POOL_DIM = 256
POOL_HALO = 16
CONV_HALO = 8
PROJ_W = 3200
BA_BLOCK = 3072 // 128
ADAM_LR, ADAM_B1, ADAM_B2, ADAM_EPS, ADAM_WD, ADAM_STEP = 0.001, 0.9, 0.999, 1e-08, 0.01, 10
VMEM_BIG = 52 * 1024 * 1024
FFN_FWD_ROWS = 512
FFN_BWD_ROWS = 256
MESH = pl.DeviceIdType.MESH
HI = lax.Precision.HIGHEST
ANY = pl.BlockSpec(memory_space=pl.ANY)


def _params(sem=None, vmem=None):
    return pltpu.CompilerParams(dimension_semantics=sem, vmem_limit_bytes=vmem)


def _dot(a, b):
    return jnp.dot(a, b, preferred_element_type=F32)


def _dot_nt(a, b):
    return lax.dot_general(a, b, (((1,), (1,)), ((), ())), preferred_element_type=F32)


def _dot_tn(a, b):
    return lax.dot_general(a, b, (((0,), (0,)), ((), ())), preferred_element_type=F32)


def _dot_hi(a, b):
    return jnp.dot(a, b, preferred_element_type=F32, precision=HI)


def _dot_mid(a, b):
    return jnp.dot(a, b, preferred_element_type=F32, precision=lax.Precision.HIGH)


def _bf(a):
    return a.astype(BF16)


def _rms_stats(x):
    r = lax.rsqrt(jnp.mean(x * x, axis=-1, keepdims=True) + NORM_EPS)
    return x * r, r


def _rms_bwd(dh, xhat, r, w):
    dhn = dh * w
    return r * (dhn - xhat * jnp.mean(dhn * xhat, axis=-1, keepdims=True))


def _sigmoid(x):
    return jax.nn.sigmoid(x)


def _silu_grad(x, s):
    return s * (1.0 + x * (1.0 - s))


def _gelu(x):
    return 0.5 * x * (1.0 + lax.erf(x * 0.7071067811865476))


def _gelu_grad(x):
    return 0.5 * (1.0 + lax.erf(x * 0.7071067811865476)) + x * jnp.exp(-0.5 * x * x) * 0.3989422804014327


def _softplus(x):
    return jnp.maximum(x, 0.0) + jnp.log(1.0 + jnp.exp(-jnp.abs(x)))


def _tile(n, pref):
    t = min(n, pref)
    assert n % t == 0, (n, t)
    return t


def _ffn_weight_specs():
    once = pl.Buffered(1)
    return [pl.BlockSpec((N_CHIPS, D_MODEL, FF_CHUNK), lambda i: (0, 0, 0), pipeline_mode=once),
            pl.BlockSpec((N_CHIPS, D_FF // N_CHIPS, D_MODEL), lambda i: (0, 0, 0), pipeline_mode=once)]


def _ffn_fwd(x, nw, win, wout, layer, name, exchange=None):
    T, D = x.shape
    tm = _tile(T, FFN_FWD_ROWS)
    nj = N_CHIPS // 2

    def body(x_ref, n_ref, win_ref, wo_ref, xo_ref, g_ref, u_ref, hb_ref):
        xv = x_ref[...]
        xhat, _ = _rms_stats(xv)
        h = _bf(xhat * n_ref[...])
        hb_ref[...] = h
        acc = None
        for j in range(nj):
            cols = slice(j * FF_CHUNK, (j + 1) * FF_CHUNK)
            g = _dot(h, win_ref[j])
            u = _dot(h, win_ref[nj + j])
            g_ref[:, cols] = _bf(g)
            u_ref[:, cols] = _bf(u)
            part = _dot(_bf(g * _sigmoid(g) * u), wo_ref[2 * j:2 * j + 2].reshape(FF_CHUNK, D))
            acc = part if acc is None else acc + part
        xo_ref[...] = xv + 0.5 * acc

    row = pl.BlockSpec((tm, D), lambda i: (i, 0))
    wide = pl.BlockSpec((tm, D_FF), lambda i: (i, 0))
    return _call_with_exchange(
        body, exchange, name=name, steps=T // tm, vmem=VMEM_BIG,
        in_specs=[row, pl.BlockSpec((None, 1, D), lambda i: (layer, 0, 0))] + _ffn_weight_specs(),
        out_specs=[row, wide, wide, row],
        out_shape=[jax.ShapeDtypeStruct((T, D), F32), jax.ShapeDtypeStruct((T, D_FF), BF16),
                   jax.ShapeDtypeStruct((T, D_FF), BF16), jax.ShapeDtypeStruct((T, D), BF16)],
        args=(x, nw, win, wout))


def _ffn_bwd(dxo, x, nw, g, u, win, wout, layer, name, exchange=None):
    T, D = x.shape
    tm = _tile(T, FFN_BWD_ROWS)
    nj = N_CHIPS // 2

    def body(dxo_ref, x_ref, n_ref, g_ref, u_ref, win_ref, wo_ref, dx_ref, dgu_ref, a_ref, dyb_ref, dn_ref):
        @pl.when(pl.program_id(0) == 0)
        def _():
            dn_ref[...] = jnp.zeros_like(dn_ref)

        dxo = dxo_ref[...]
        dyb = _bf(0.5 * dxo)
        dyb_ref[...] = dyb
        dh = None
        for j in range(nj):
            cols = slice(j * FF_CHUNK, (j + 1) * FF_CHUNK)
            da = _dot_nt(dyb, wo_ref[2 * j:2 * j + 2].reshape(FF_CHUNK, D))
            gv = g_ref[:, cols].astype(F32)
            uv = u_ref[:, cols].astype(F32)
            sg = _sigmoid(gv)
            sl = gv * sg
            dgb = _bf(da * uv * _silu_grad(gv, sg))
            dub = _bf(da * sl)
            a_ref[:, cols] = _bf(sl * uv)
            dgu_ref[:, cols] = dgb
            dgu_ref[:, D_FF + j * FF_CHUNK:D_FF + (j + 1) * FF_CHUNK] = dub
            part = _dot_nt(dgb, win_ref[j]) + _dot_nt(dub, win_ref[nj + j])
            dh = part if dh is None else dh + part
        xhat, r = _rms_stats(x_ref[...])
        dx_ref[...] = dxo + _rms_bwd(dh, xhat, r, n_ref[...])
        dn_ref[...] += jnp.sum(dh * xhat, axis=0, keepdims=True)

    row = pl.BlockSpec((tm, D), lambda i: (i, 0))
    wide = pl.BlockSpec((tm, D_FF), lambda i: (i, 0))
    return _call_with_exchange(
        body, exchange, name=name, steps=T // tm, vmem=VMEM_BIG,
        in_specs=[row, row, pl.BlockSpec((None, 1, D), lambda i: (layer, 0, 0)), wide, wide] + _ffn_weight_specs(),
        out_specs=[row, pl.BlockSpec((tm, 2 * D_FF), lambda i: (i, 0)), wide, row, pl.BlockSpec((1, D), lambda i: (0, 0))],
        out_shape=[jax.ShapeDtypeStruct((T, D), F32), jax.ShapeDtypeStruct((T, 2 * D_FF), BF16),
                   jax.ShapeDtypeStruct((T, D_FF), BF16), jax.ShapeDtypeStruct((T, D), BF16),
                   jax.ShapeDtypeStruct((1, D), F32)],
        args=(dxo, x, nw, g, u, win, wout))


def _matmul_tn(a, b, bm, bn, name, stack_n=False):
    T, M = a.shape
    N = b.shape[1]
    tk = _tile(T, 1024)
    bm, bn = _tile(M, bm), _tile(N, bn)

    def body(a_ref, b_ref, o_ref):
        @pl.when(pl.program_id(2) == 0)
        def _():
            o_ref[...] = jnp.zeros_like(o_ref)

        o_ref[...] += _dot_tn(_bf(a_ref[...]), _bf(b_ref[...]))

    if stack_n:
        out_spec = pl.BlockSpec((None, bm, bn), lambda m, n, k: (n, m, 0))
        out_shape = jax.ShapeDtypeStruct((N // bn, M, bn), F32)
    else:
        out_spec = pl.BlockSpec((bm, bn), lambda m, n, k: (m, n))
        out_shape = jax.ShapeDtypeStruct((M, N), F32)
    return pl.pallas_call(
        body, name=name, grid=(M // bm, N // bn, T // tk),
        in_specs=[pl.BlockSpec((tk, bm), lambda m, n, k: (k, m)),
                  pl.BlockSpec((tk, bn), lambda m, n, k: (k, n))],
        out_specs=out_spec, out_shape=out_shape,
        compiler_params=_params(("parallel", "parallel", "arbitrary"), VMEM_BIG),
    )(a, b)


def _matmul(a, b, name, trans_b=False, res=None, out_dtype=F32):
    T, K = a.shape
    N = b.shape[0] if trans_b else b.shape[1]
    tm = _tile(T, 512)

    def body(*refs):
        a_ref, b_ref = refs[0], refs[1]
        o_ref = refs[-1]
        av, bv = _bf(a_ref[...]), _bf(b_ref[...])
        acc = _dot_nt(av, bv) if trans_b else _dot(av, bv)
        if res is not None:
            acc = acc + refs[2][...]
        o_ref[...] = acc.astype(out_dtype)

    in_specs = [pl.BlockSpec((tm, K), lambda i: (i, 0)), pl.BlockSpec(b.shape, lambda i: (0, 0))]
    args = [a, b]
    if res is not None:
        in_specs.append(pl.BlockSpec((tm, N), lambda i: (i, 0)))
        args.append(res)
    return pl.pallas_call(
        body, name=name, grid=(T // tm,), in_specs=in_specs,
        out_specs=pl.BlockSpec((tm, N), lambda i: (i, 0)),
        out_shape=jax.ShapeDtypeStruct((T, N), out_dtype),
        compiler_params=_params(("parallel",), VMEM_BIG),
    )(*args)


def _matmul_nt_pieces(pieces, b, name):
    T = pieces[0].shape[0]
    N = b.shape[0]
    tm = _tile(T, 512)
    widths = [p.shape[1] for p in pieces]
    offsets = [sum(widths[:k]) for k in range(len(pieces))]
    assert offsets[-1] + widths[-1] == b.shape[1] and all(o % 128 == 0 for o in offsets)

    def body(*refs):
        b_ref, o_ref = refs[len(pieces)], refs[-1]
        acc = None
        for a_ref, off, width in zip(refs, offsets, widths):
            part = _dot_nt(_bf(a_ref[...]), b_ref[:, off:off + width])
            acc = part if acc is None else acc + part
        o_ref[...] = acc

    return pl.pallas_call(
        body, name=name, grid=(T // tm,),
        in_specs=[pl.BlockSpec((tm, wd), lambda i: (i, 0)) for wd in widths] + [pl.BlockSpec(b.shape, lambda i: (0, 0))],
        out_specs=pl.BlockSpec((tm, N), lambda i: (i, 0)),
        out_shape=jax.ShapeDtypeStruct((T, N), F32),
        compiler_params=_params(("parallel",), VMEM_BIG),
    )(*pieces, b)


def _rms_fwd_call(x, nw, layer, name):
    T, D = x.shape
    tm = _tile(T, 512)

    def body(x_ref, n_ref, o_ref):
        xhat, _ = _rms_stats(x_ref[...])
        o_ref[...] = _bf(xhat * n_ref[...])

    return pl.pallas_call(
        body, name=name, grid=(T // tm,),
        in_specs=[pl.BlockSpec((tm, D), lambda i: (i, 0)), pl.BlockSpec((None, 1, D), lambda i: (layer, 0, 0))],
        out_specs=pl.BlockSpec((tm, D), lambda i: (i, 0)),
        out_shape=jax.ShapeDtypeStruct((T, D), BF16),
        compiler_params=_params(("parallel",)),
    )(x, nw)


def _rms_bwd_call(dh, x, nw, dres, layer, name):
    T, D = x.shape
    tm = _tile(T, 512)

    def body(dh_ref, x_ref, n_ref, dr_ref, dx_ref, dn_ref):
        @pl.when(pl.program_id(0) == 0)
        def _():
            dn_ref[...] = jnp.zeros_like(dn_ref)

        xhat, r = _rms_stats(x_ref[...])
        dh_v = dh_ref[...]
        dx_ref[...] = dr_ref[...] + _rms_bwd(dh_v, xhat, r, n_ref[...])
        dn_ref[...] += jnp.sum(dh_v * xhat, axis=0, keepdims=True)

    row = pl.BlockSpec((tm, D), lambda i: (i, 0))
    return pl.pallas_call(
        body, name=name, grid=(T // tm,),
        in_specs=[row, row, pl.BlockSpec((None, 1, D), lambda i: (layer, 0, 0)), row],
        out_specs=[row, pl.BlockSpec((1, D), lambda i: (0, 0))],
        out_shape=[jax.ShapeDtypeStruct((T, D), F32), jax.ShapeDtypeStruct((1, D), F32)],
        compiler_params=_params(("arbitrary",)),
    )(dh, x, nw, dres)


def _shift_rows(x, s):
    n = x.shape[0]
    s = s % n
    return x if s == 0 else pltpu.roll(x, s, 0)


def _conv_fwd(proj, conv_w, name):
    T = proj.shape[0]
    C = 3 * DN_HEADS * DN_DIM
    cb = 512
    tm = _tile(T, 512)
    hb = tm // CONV_HALO

    def body(x_ref, xp_ref, w_ref, o_ref):
        i = pl.program_id(1)
        prev = jnp.where(i == 0, 0.0, xp_ref[...])
        ext = jnp.concatenate([prev, x_ref[...]], axis=0)
        w = w_ref[...]
        y = ext * w[DN_CONV - 1:DN_CONV, :]
        for k in range(DN_CONV - 1):
            y = y + _shift_rows(ext, DN_CONV - 1 - k) * w[k:k + 1, :]
        y = y[CONV_HALO:, :]
        o_ref[...] = y * _sigmoid(y)

    return pl.pallas_call(
        body, name=name, grid=(C // cb, T // tm),
        in_specs=[pl.BlockSpec((tm, cb), lambda c, i: (i, c)),
                  pl.BlockSpec((CONV_HALO, cb), lambda c, i: (jnp.maximum(i * hb - 1, 0), c)),
                  pl.BlockSpec((DN_CONV, cb), lambda c, i: (0, c))],
        out_specs=pl.BlockSpec((tm, cb), lambda c, i: (i, c)),
        out_shape=jax.ShapeDtypeStruct((T, C), F32),
        compiler_params=_params(("parallel", "parallel")),
    )(proj, proj, conv_w)


def _conv_bwd(dy, proj, conv_w, name):
    T = proj.shape[0]
    C = 3 * DN_HEADS * DN_DIM
    cb = 512
    tm = _tile(T, 512)
    hb = tm // CONV_HALO
    nt = T // tm

    def body(x_ref, xp_ref, xn_ref, dy_ref, dyn_ref, w_ref, dx_ref, dw_ref):
        i = pl.program_id(1)

        @pl.when(i == 0)
        def _():
            dw_ref[...] = jnp.zeros_like(dw_ref)

        prev = jnp.where(i == 0, 0.0, xp_ref[...])
        ext = jnp.concatenate([prev, x_ref[...], xn_ref[...]], axis=0)
        dy_ext = jnp.concatenate([jnp.zeros((CONV_HALO, cb), F32), dy_ref[...],
                                  jnp.where(i == nt - 1, 0.0, dyn_ref[...])], axis=0)
        w = w_ref[...]
        shifted = [_shift_rows(ext, DN_CONV - 1 - k) for k in range(DN_CONV)]
        y = shifted[0] * w[0:1, :]
        for k in range(1, DN_CONV):
            y = y + shifted[k] * w[k:k + 1, :]
        s = _sigmoid(y)
        dpre = dy_ext * _silu_grad(y, s)
        dx = dpre * w[DN_CONV - 1:DN_CONV, :]
        for k in range(DN_CONV - 1):
            dx = dx + _shift_rows(dpre, -(DN_CONV - 1 - k)) * w[k:k + 1, :]
        dx_ref[...] = _bf(dx[CONV_HALO:CONV_HALO + tm, :])
        rows = [jnp.sum((dpre * shifted[k])[CONV_HALO:CONV_HALO + tm, :], axis=0, keepdims=True) for k in range(DN_CONV)]
        dw_ref[...] += jnp.concatenate(rows, axis=0)

    last_halo = T // CONV_HALO - 1
    return pl.pallas_call(
        body, name=name, grid=(C // cb, nt),
        in_specs=[pl.BlockSpec((tm, cb), lambda c, i: (i, c)),
                  pl.BlockSpec((CONV_HALO, cb), lambda c, i: (jnp.maximum(i * hb - 1, 0), c)),
                  pl.BlockSpec((CONV_HALO, cb), lambda c, i: (jnp.minimum((i + 1) * hb, last_halo), c)),
                  pl.BlockSpec((tm, cb), lambda c, i: (i, c)),
                  pl.BlockSpec((CONV_HALO, cb), lambda c, i: (jnp.minimum((i + 1) * hb, last_halo), c)),
                  pl.BlockSpec((DN_CONV, cb), lambda c, i: (0, c))],
        out_specs=[pl.BlockSpec((tm, cb), lambda c, i: (i, c)),
                   pl.BlockSpec((DN_CONV, cb), lambda c, i: (0, c))],
        out_shape=[jax.ShapeDtypeStruct((T, C), BF16), jax.ShapeDtypeStruct((DN_CONV, C), F32)],
        compiler_params=_params(("parallel", "arbitrary")),
    )(proj, proj, proj, dy, dy, conv_w)


def _unit_lower_inverses(lows, eye):
    def each(fn, *lists):
        return [fn(*args) for args in zip(*lists)]

    p1 = [-low for low in lows]
    p2 = each(_dot_mid, p1, p1)
    p4 = each(_dot_mid, p2, p2)
    a = each(lambda x, y: eye + x + y + _dot_mid(x, y), p1, p2)
    p8 = each(_dot_mid, p4, p4)
    p16 = each(_dot_mid, p8, p8)
    b = each(lambda x, y: eye + x + y + _dot_mid(x, y), p4, p8)
    p32 = each(_dot_mid, p16, p16)
    ab = each(_dot_mid, a, b)
    c = each(lambda x, y: eye + x + y + _dot_mid(x, y), p16, p32)
    return each(_dot_mid, ab, c)


def _interleave(chains):
    results = [None] * len(chains)
    live = list(range(len(chains)))
    while live:
        for i in list(live):
            try:
                next(chains[i])
            except StopIteration as stop:
                results[i] = stop.value
                live.remove(i)
        yield
    return results


def _run_interleaved(chains):
    rounds = _interleave(chains)
    while True:
        try:
            next(rounds)
        except StopIteration as stop:
            return stop.value


def _l2_unit(x):
    r = lax.rsqrt(jnp.sum(x * x, axis=-1, keepdims=True) + NORM_EPS)
    return x * r, r


class _BlockMasks:
    def __init__(self):
        n = DN_BLOCK
        row = lax.broadcasted_iota(jnp.int32, (n, n), 0)
        col = lax.broadcasted_iota(jnp.int32, (n, n), 1)
        same = (row // DN_CHUNK) == (col // DN_CHUNK)
        self.lower, self.strict_lower = same & (row >= col), same & (row > col)
        self.upper, self.strict_upper = same & (row <= col), same & (row < col)
        self.eye = (row == col).astype(F32)
        self.first = lax.broadcasted_iota(jnp.int32, (n, 1), 0) < DN_CHUNK


def _dn_gates(ba, hp):
    coef = -jnp.exp(hp[0:1, :])
    pre = ba + hp[1:2, :]
    return _sigmoid(ba), coef * _softplus(pre), coef, pre


def _dn_block_gates(mk, ba, hp):
    assert DN_BLOCK == 2 * DN_CHUNK
    beta_t, graw_t, coef, pre = _dn_gates(ba, hp)
    gcum_t = _dot_hi(mk.lower.astype(F32), graw_t)
    gl_t = jnp.where(mk.first, gcum_t[DN_CHUNK - 1:DN_CHUNK, :], gcum_t[DN_BLOCK - 1:DN_BLOCK, :])
    return beta_t, gcum_t, gl_t, graw_t, coef, pre


def _dn_local(mk, qraw, kraw, bc, gc, gl):
    f = {}
    f["qn"], f["rq"] = _l2_unit(qraw)
    qh = f["qn"] * (DN_DIM ** -0.5)
    kh, f["rk"] = _l2_unit(kraw)
    gr = jnp.broadcast_to(gc, (DN_BLOCK, DN_BLOCK)).T
    dec = jnp.where(mk.lower, jnp.exp(jnp.where(mk.lower, gc - gr, 0.0)), 0.0)
    kb = kh * bc
    mkk = _dot_nt(_bf(kb), _bf(kh))
    eg = jnp.exp(gc)
    mqk = _dot_nt(_bf(qh), _bf(kh))
    etl = jnp.exp(gl - gc)
    f.update(qh=qh, kh=kh, gr=gr, dec=dec, kb=kb, mkk=mkk, eg=eg, mqk=mqk, attn=mqk * dec, etl=etl, qd=qh * eg, kt=kh * etl)
    return f


def _dn_specs(rows, rev=None):
    at = (lambda n: n) if rev is None else rev
    hw = DN_HEADS * DN_DIM
    return dict(
        qkv=[pl.BlockSpec((rows, hw), lambda n, j=j: (at(n), j)) for j in range(3)],
        ba=pl.BlockSpec((rows, 128), lambda n: (at(n), BA_BLOCK)),
        hp=pl.BlockSpec((8, 128), lambda n: (0, 0)),
        tok=pl.BlockSpec((rows, hw), lambda n: (at(n), 0)),
        attn=pl.BlockSpec((rows, DN_HEADS * DN_CHUNK), lambda n: (at(n), 0)),
        gate=pl.BlockSpec((rows // DN_CHUNK, 8, 128), lambda n: (at(n), 0, 0)),
        state=pl.BlockSpec((rows // DN_CHUNK, DN_HEADS, DN_DIM, DN_DIM), lambda n: (at(n), 0, 0, 0)),
    )


def _call_with_exchange(body, exchange, *, name, steps, in_specs, out_specs, out_shape, args, vmem=None, scratch_shapes=()):
    if exchange is None:
        res = pl.pallas_call(body, name=name, grid=(steps,), in_specs=in_specs, out_specs=out_specs, out_shape=out_shape,
                             scratch_shapes=list(scratch_shapes), compiler_params=_params(("arbitrary",), vmem))(*args)
        return list(res), None
    n_in, n_out, m, n_scr = len(in_specs), len(out_specs), len(exchange.operands), len(scratch_shapes)

    def hosted(*refs):
        ins, ex_ins = refs[:n_in], refs[n_in:n_in + m]
        outs, ex_outs = refs[n_in + m:n_in + m + n_out], refs[n_in + m + n_out:n_in + 2 * m + n_out]
        scratch, sems = refs[n_in + 2 * m + n_out:n_in + 2 * m + n_out + n_scr], refs[n_in + 2 * m + n_out + n_scr:]

        @pl.when(pl.program_id(0) == 0)
        def _():
            exchange.start(ex_ins, ex_outs, sems)

        body(*ins, *outs, *scratch)

        @pl.when(pl.program_id(0) == steps - 1)
        def _():
            exchange.finish(ex_ins, ex_outs, sems)

    res = pl.pallas_call(
        hosted, name=name, grid=(steps,), in_specs=list(in_specs) + [ANY] * m, out_specs=list(out_specs) + [ANY] * m,
        out_shape=list(out_shape) + list(exchange.out_shape), scratch_shapes=list(scratch_shapes) + list(exchange.scratch),
        compiler_params=pltpu.CompilerParams(dimension_semantics=("arbitrary",), vmem_limit_bytes=vmem, has_side_effects=True),
    )(*args, *exchange.operands)
    return list(res[:n_out]), exchange.finalize(list(res[n_out:]))


def _dn_prep(qkv, proj, hp, name, exchange=None):
    T = qkv.shape[0]
    n_chunks = T // DN_CHUNK
    blocks = max(1, min(DN_PREP_CHUNKS, n_chunks) * DN_CHUNK // DN_BLOCK)
    group = blocks * DN_BLOCK // DN_CHUNK
    rows = blocks * DN_BLOCK
    hw = DN_HEADS * DN_DIM

    def body(q_ref, k_ref, v_ref, ba_ref, hp_ref, u_ref, w_ref, p_ref, qd_ref, kt_ref, gl_ref, inv_ref):
        mk = _BlockMasks()
        hp_v = hp_ref[...]
        chains = []
        for j in range(blocks):
            rs = slice(j * DN_BLOCK, (j + 1) * DN_BLOCK)
            beta_t, gcum_t, gl_t = _dn_block_gates(mk, ba_ref[rs, :], hp_v)[:3]
            for c in range(DN_BLOCK // DN_CHUNK):
                gl_ref[j * (DN_BLOCK // DN_CHUNK) + c] = jnp.broadcast_to(gl_t[c * DN_CHUNK:c * DN_CHUNK + 1, :], (8, 128))
            for h in range(DN_HEADS):
                sl = slice(h * DN_DIM, (h + 1) * DN_DIM)
                gate = slice(DN_HEADS + h, DN_HEADS + h + 1)
                bc = beta_t[:, h:h + 1]
                f = _dn_local(mk, q_ref[rs, sl], k_ref[rs, sl], bc, gcum_t[:, gate], gl_t[:, gate])
                for c in range(DN_BLOCK // DN_CHUNK):
                    cr = slice(c * DN_CHUNK, (c + 1) * DN_CHUNK)
                    p_ref[j * DN_BLOCK + c * DN_CHUNK:j * DN_BLOCK + (c + 1) * DN_CHUNK, h * DN_CHUNK:(h + 1) * DN_CHUNK] = _bf(f["attn"][cr, cr])
                qd_ref[rs, sl] = _bf(f["qd"])
                kt_ref[rs, sl] = _bf(f["kt"])
                chains.append((rs, sl, bc, f))
        invs = _unit_lower_inverses([jnp.where(mk.strict_lower, f["mkk"] * f["dec"], 0.0) for _, _, _, f in chains], mk.eye)
        for (rs, sl, bc, f), inv in zip(chains, invs):
            inv_ref[rs, sl] = inv
            sol = _dot_mid(inv, jnp.concatenate([v_ref[rs, sl] * bc, f["kb"] * f["eg"]], axis=1))
            u_ref[rs, sl] = sol[:, :DN_DIM]
            w_ref[rs, sl] = _bf(sol[:, DN_DIM:])

    sp = _dn_specs(rows)
    tok16 = jax.ShapeDtypeStruct((T, hw), BF16)
    return _call_with_exchange(
        body, exchange, name=name, steps=n_chunks // group,
        in_specs=sp["qkv"] + [sp["ba"], sp["hp"]],
        out_specs=[sp["tok"], sp["tok"], sp["attn"], sp["tok"], sp["tok"], sp["gate"], sp["tok"]],
        out_shape=[jax.ShapeDtypeStruct((T, hw), F32), tok16, jax.ShapeDtypeStruct((T, DN_HEADS * DN_CHUNK), BF16),
                   tok16, tok16, jax.ShapeDtypeStruct((n_chunks, 8, 128), F32), jax.ShapeDtypeStruct((T, hw), F32)],
        args=(qkv, qkv, qkv, proj, hp))


def _dn_scan(u, w, p, qd, kt, gl, name, exchange=None):
    T = u.shape[0]
    n_chunks = T // DN_CHUNK
    group = min(DN_SCAN_CHUNKS, n_chunks)
    rows = group * DN_CHUNK
    hw = DN_HEADS * DN_DIM

    def body(u_ref, w_ref, p_ref, qd_ref, kt_ref, gl_ref, o_ref, vn_ref, sall_ref, s_s):
        @pl.when(pl.program_id(0) == 0)
        def _():
            s_s[...] = jnp.zeros_like(s_s)

        state = [s_s[h] for h in range(DN_HEADS)]
        for j in range(group):
            rs = slice(j * DN_CHUNK, (j + 1) * DN_CHUNK)
            for h in range(DN_HEADS):
                sl = slice(h * DN_DIM, (h + 1) * DN_DIM)
                sall_ref[j, h] = state[h]
                sb = _bf(state[h])
                vnb = _bf(u_ref[rs, sl] - _dot(w_ref[rs, sl], sb))
                vn_ref[rs, sl] = vnb
                o_ref[rs, sl] = _dot(qd_ref[rs, sl], sb) + _dot(p_ref[rs, h * DN_CHUNK:(h + 1) * DN_CHUNK], vnb)
                egl = jnp.exp(gl_ref[j, 0:1, DN_HEADS + h:DN_HEADS + h + 1])
                state[h] = state[h] * egl + _dot_tn(kt_ref[rs, sl], vnb)
        for h in range(DN_HEADS):
            s_s[h] = state[h]

    sp = _dn_specs(rows)
    return _call_with_exchange(
        body, exchange, name=name, steps=n_chunks // group,
        in_specs=[sp["tok"], sp["tok"], sp["attn"], sp["tok"], sp["tok"], sp["gate"]],
        out_specs=[sp["tok"], sp["tok"], sp["state"]],
        out_shape=[jax.ShapeDtypeStruct((T, hw), F32), jax.ShapeDtypeStruct((T, hw), BF16),
                   jax.ShapeDtypeStruct((n_chunks, DN_HEADS, DN_DIM, DN_DIM), F32)],
        scratch_shapes=[pltpu.VMEM((DN_HEADS, DN_DIM, DN_DIM), F32)],
        args=(u, w, p, qd, kt, gl))


def _dn_scan_bwd(w, p, qd, kt, gl, vn, sall, do, name, exchange=None):
    T = w.shape[0]
    n_chunks = T // DN_CHUNK
    group = min(DN_SCAN_CHUNKS, n_chunks)
    rows = group * DN_CHUNK
    hw = DN_HEADS * DN_DIM
    last = n_chunks // group - 1

    def body(w_ref, p_ref, qd_ref, kt_ref, gl_ref, vn_ref, sall_ref, do_ref, dvn_ref, dkt_ref, dgl_ref, ds_s):
        @pl.when(pl.program_id(0) == 0)
        def _():
            ds_s[...] = jnp.zeros_like(ds_s)

        lane = lax.broadcasted_iota(jnp.int32, (8, 128), 1)
        d_state = [ds_s[h] for h in range(DN_HEADS)]
        for j in reversed(range(group)):
            rs = slice(j * DN_CHUNK, (j + 1) * DN_CHUNK)
            dgl_tile = jnp.zeros((8, 128), F32)
            for h in range(DN_HEADS):
                sl = slice(h * DN_DIM, (h + 1) * DN_DIM)
                d_out = _bf(do_ref[rs, sl])
                d_new = d_state[h]
                d_newb = _bf(d_new)
                d_vn = _dot_tn(p_ref[rs, h * DN_CHUNK:(h + 1) * DN_CHUNK], d_out) + _dot(kt_ref[rs, sl], d_newb)
                dvn_ref[rs, sl] = d_vn
                dkt_ref[rs, sl] = _dot_nt(vn_ref[rs, sl], d_newb)
                egl = jnp.exp(gl_ref[j, 0:1, DN_HEADS + h:DN_HEADS + h + 1])
                prod = jnp.sum(d_new * sall_ref[j, h], axis=1, keepdims=True)
                dgl_tile = jnp.where(lane == DN_HEADS + h, jnp.sum(prod, axis=0, keepdims=True) * egl, dgl_tile)
                d_state[h] = d_new * egl + _dot_tn(qd_ref[rs, sl], d_out) - _dot_tn(w_ref[rs, sl], _bf(d_vn))
            dgl_ref[j] = dgl_tile
        for h in range(DN_HEADS):
            ds_s[h] = d_state[h]

    sp = _dn_specs(rows, rev=lambda n: last - n)
    return _call_with_exchange(
        body, exchange, name=name, steps=n_chunks // group,
        in_specs=[sp["tok"], sp["attn"], sp["tok"], sp["tok"], sp["gate"], sp["tok"], sp["state"], sp["tok"]],
        out_specs=[sp["tok"], sp["tok"], sp["gate"]],
        out_shape=[jax.ShapeDtypeStruct((T, hw), F32), jax.ShapeDtypeStruct((T, hw), F32),
                   jax.ShapeDtypeStruct((n_chunks, 8, 128), F32)],
        scratch_shapes=[pltpu.VMEM((DN_HEADS, DN_DIM, DN_DIM), F32)],
        args=(w, p, qd, kt, gl, vn, sall, do))


def _dn_prep_bwd(qkv, proj, hp, sall, vn, do, dvn, dkt, dgl, inv, u, w, name, exchange=None):
    T = qkv.shape[0]
    n_chunks = T // DN_CHUNK
    blocks = max(1, min(DN_PREP_CHUNKS, n_chunks) * DN_CHUNK // DN_BLOCK)
    per_block = DN_BLOCK // DN_CHUNK
    group = blocks * per_block
    rows = blocks * DN_BLOCK
    hw = DN_HEADS * DN_DIM
    first_rows, second_rows = slice(0, DN_CHUNK), slice(DN_CHUNK, DN_BLOCK)

    def rowsum(x):
        return jnp.sum(x, axis=1, keepdims=True)

    def by_chunk(x, s0, s1, fn):
        return jnp.concatenate([fn(x[first_rows], s0), fn(x[second_rows], s1)], axis=0)

    def body(q_ref, k_ref, v_ref, ba_ref, hp_ref, sall_ref, vn_ref, do_ref, dvn_ref, dkt_ref, dgl_ref,
             inv_ref, u_ref, w_ref, dqkv_ref, dba_ref, dhp_ref):
        @pl.when(pl.program_id(0) == 0)
        def _():
            dhp_ref[...] = jnp.zeros_like(dhp_ref)

        mk = _BlockMasks()
        hp_v = hp_ref[...]
        chains = []
        for j in range(blocks):
            rs = slice(j * DN_BLOCK, (j + 1) * DN_BLOCK)
            chains.append(one_block(mk, hp_v, *(r.at[rs, :] for r in (q_ref, k_ref, v_ref, ba_ref)),
                                    sall_ref.at[pl.ds(j * per_block, per_block)],
                                    *(r.at[rs, :] for r in (vn_ref, do_ref, dvn_ref, dkt_ref)),
                                    dgl_ref.at[pl.ds(j * per_block, per_block)],
                                    *(r.at[rs, :] for r in (inv_ref, u_ref, w_ref)),
                                    *(dqkv_ref.at[rs, pl.ds(i * hw, hw)] for i in range(3)), dba_ref.at[rs, :]))
        total = jnp.zeros((8, 128), F32)
        for part in _run_interleaved(chains):
            total = total + part
        dhp_ref[...] += total

    def one_block(mk, hp_v, q_ref, k_ref, v_ref, ba_ref, state_ref, vn_ref, do_ref, dvn_ref, dkt_ref, dgl_ref,
                  inv_ref, u_ref, w_ref, dq_ref, dk_ref, dv_ref, dba_ref):
        ba = ba_ref[...]
        beta_t, gcum_t, gl_t, graw_t, coef, pre = _dn_block_gates(mk, ba, hp_v)
        lane = lax.broadcasted_iota(jnp.int32, (DN_BLOCK, 128), 1)
        rowi = lax.broadcasted_iota(jnp.int32, (DN_BLOCK, 1), 0)

        def head(h):
            sl = slice(h * DN_DIM, (h + 1) * DN_DIM)
            gate = slice(DN_HEADS + h, DN_HEADS + h + 1)
            gc = gcum_t[:, gate]
            bc = beta_t[:, h:h + 1]
            sb0, sb1 = _bf(state_ref[0, h]), _bf(state_ref[1, h])
            vh = v_ref[:, sl]
            f = _dn_local(mk, q_ref[:, sl], k_ref[:, sl], bc, gc, gl_t[:, gate])
            yield
            qh, kh, kb, dec, eg, etl = f["qh"], f["kh"], f["kb"], f["dec"], f["eg"], f["etl"]
            qd, kt = f["qd"], f["kt"]
            qb, kbf, kbb = _bf(qh), _bf(kh), _bf(kb)
            dec_t = jnp.where(mk.upper, jnp.exp(jnp.where(mk.upper, f["gr"] - gc, 0.0)), 0.0)
            mkk_t = f["mkk"].T
            inv_t = inv_ref[:, sl].T
            mqk_t = f["mqk"].T

            d_out = _bf(do_ref[:, sl])
            vnb = vn_ref[:, sl]
            d_qd = by_chunk(d_out, sb0, sb1, _dot_nt)
            d_attn = _dot_nt(d_out, vnb)
            d_attn_t = _dot_nt(vnb, d_out)
            d_vn = dvn_ref[:, sl]
            d_kt = dkt_ref[:, sl]
            d_w = -by_chunk(_bf(d_vn), sb0, sb1, _dot_nt)
            yield
            d_rhs = _dot_mid(inv_t, jnp.concatenate([d_vn, d_w], axis=1))
            yield
            d_bu, d_bw = d_rhs[:, :DN_DIM], d_rhs[:, DN_DIM:]
            ub, wb, d_bub, d_bwb = _bf(u_ref[:, sl]), w_ref[:, sl], _bf(d_bu), _bf(d_bw)
            d_low = -(_dot_nt(d_bub, ub) + _dot_nt(d_bwb, wb))
            d_low_t = -(_dot_nt(ub, d_bub) + _dot_nt(wb, d_bwb))
            yield
            d_mkk = jnp.where(mk.strict_lower, d_low * dec, 0.0)
            d_mkk_t = jnp.where(mk.strict_upper, d_low_t * dec_t, 0.0)
            d_mqk = jnp.where(mk.lower, d_attn * dec, 0.0)
            d_mqk_t = jnp.where(mk.upper, d_attn_t * dec_t, 0.0)
            bw = kb * eg
            d_kb = _dot(_bf(d_mkk), kbf) + d_bw * eg
            d_k = _dot(_bf(d_mkk_t), kbb) + _dot(_bf(d_mqk_t), qb) + d_kt * etl + d_kb * bc
            d_q = _dot(_bf(d_mqk), kbf) + d_qd * eg
            yield
            d_beta = rowsum(d_kb * kh) + rowsum(d_bu * vh)
            dv_ref[:, sl] = d_bu * bc
            e_mat = d_mkk * f["mkk"] + d_mqk * f["mqk"]
            e_mat_t = d_mkk_t * mkk_t + d_mqk_t * mqk_t
            kt_term = rowsum(d_kt * kt)
            d_g = rowsum(e_mat) - rowsum(e_mat_t) + rowsum(d_qd * qd) + rowsum(d_bw * bw) - kt_term
            for c, chunk_rows in enumerate((mk.first, ~mk.first)):
                d_glast = dgl_ref[c, 0:1, gate] + jnp.sum(jnp.where(chunk_rows, kt_term, 0.0), axis=0, keepdims=True)
                d_g = d_g + jnp.where(rowi == (c + 1) * DN_CHUNK - 1, d_glast, 0.0)
            qn = f["qn"]
            d_qs = d_q * (DN_DIM ** -0.5)
            dq_ref[:, sl] = f["rq"] * (d_qs - qn * rowsum(d_qs * qn))
            dk_ref[:, sl] = f["rk"] * (d_k - kh * rowsum(d_k * kh))
            return d_g, d_beta

        per_head = yield from _interleave([head(h) for h in range(DN_HEADS)])
        dgcum_t = jnp.zeros((DN_BLOCK, 128), F32)
        dbeta_t = jnp.zeros((DN_BLOCK, 128), F32)
        for h, (d_g, d_beta) in enumerate(per_head):
            dgcum_t = jnp.where(lane == DN_HEADS + h, d_g, dgcum_t)
            dbeta_t = jnp.where(lane == h, d_beta, dbeta_t)
        dgraw_t = _dot_hi(mk.upper.astype(F32), dgcum_t)
        sp = _sigmoid(pre)
        d_pre = dgraw_t * coef * sp
        dba_ref[...] = jnp.where(lane < DN_HEADS, dbeta_t * beta_t * (1.0 - beta_t),
                                 jnp.where(lane < 2 * DN_HEADS, d_pre, 0.0))
        in_g = (lane >= DN_HEADS) & (lane < 2 * DN_HEADS)
        d_alog = jnp.sum(jnp.where(in_g, dgraw_t * graw_t, 0.0), axis=0, keepdims=True)
        d_dtb = jnp.sum(jnp.where(in_g, d_pre, 0.0), axis=0, keepdims=True)
        return jnp.concatenate([d_alog, d_dtb, jnp.zeros((6, 128), F32)], axis=0)

    sp = _dn_specs(rows)
    return _call_with_exchange(
        body, exchange, name=name, steps=n_chunks // group,
        in_specs=sp["qkv"] + [sp["ba"], sp["hp"], sp["state"]] + [sp["tok"]] * 4 + [sp["gate"]] + [sp["tok"]] * 3,
        out_specs=[pl.BlockSpec((rows, 3 * hw), lambda n: (n, 0)), pl.BlockSpec((rows, 128), lambda n: (n, 0)), sp["hp"]],
        out_shape=[jax.ShapeDtypeStruct((T, 3 * hw), F32), jax.ShapeDtypeStruct((T, 128), F32),
                   jax.ShapeDtypeStruct((8, 128), F32)],
        args=(qkv, qkv, qkv, proj, hp, sall, vn, do, dvn, dkt, dgl, inv, u, w))


def _mix_fwd(o, proj, dn_norm, sg_norm, sg_w, sg_bt, name):
    T = o.shape[0]
    tm = _tile(T, 512)
    hw = DN_HEADS * DN_DIM
    nc = tm // SG_CHUNK

    def body(o_ref, z_ref, su_ref, sv_ref, dnn_ref, sgn_ref, sgw_ref, sgb_ref, mix_ref):
        dnn = dnn_ref[...]
        for h in range(DN_HEADS):
            sl = slice(h * DN_DIM, (h + 1) * DN_DIM)
            xhat, _ = _rms_stats(o_ref[:, sl])
            z = z_ref[:, sl]
            mix_ref[:, sl] = _bf(xhat * dnn * (z * _sigmoid(z)))
        tri = lax.broadcasted_iota(jnp.int32, (SG_CHUNK, SG_CHUNK), 0) >= lax.broadcasted_iota(jnp.int32, (SG_CHUNK, SG_CHUNK), 1)
        for g in range(SG_GROUPS):
            sl = slice(g * SG_DIM, (g + 1) * SG_DIM)
            xhat, _ = _rms_stats(_gelu(sv_ref[:, sl]))
            svn = _bf(xhat * sgn_ref[g:g + 1, :])
            sua = _gelu(su_ref[:, sl])
            wt = _bf(jnp.where(tri, sgw_ref[g], 0.0))
            bias = sgb_ref[:, g:g + 1]
            for c in range(nc):
                rows = slice(c * SG_CHUNK, (c + 1) * SG_CHUNK)
                mixed = _dot(wt, svn[rows, :]) + bias
                mix_ref[rows, hw + g * SG_DIM:hw + (g + 1) * SG_DIM] = _bf(sua[rows, :] * mixed)

    full = lambda shape: pl.BlockSpec(shape, lambda i: (0,) * len(shape))
    return pl.pallas_call(
        body, name=name, grid=(T // tm,),
        in_specs=[pl.BlockSpec((tm, hw), lambda i: (i, 0)),
                  pl.BlockSpec((tm, hw), lambda i: (i, 3)),
                  pl.BlockSpec((tm, hw), lambda i: (i, 4)),
                  pl.BlockSpec((tm, hw), lambda i: (i, 5)),
                  full((1, DN_DIM)), full((SG_GROUPS, SG_DIM)), full((SG_GROUPS, SG_CHUNK, SG_CHUNK)),
                  full((SG_CHUNK, 128))],
        out_specs=pl.BlockSpec((tm, 2 * hw), lambda i: (i, 0)),
        out_shape=jax.ShapeDtypeStruct((T, 2 * hw), BF16),
        compiler_params=_params(("parallel",)),
    )(o, proj, proj, proj, dn_norm, sg_norm, sg_w, sg_bt)


def _mix_bwd(dmix, o, proj, dn_norm, sg_norm, sg_w, sg_bt, name):
    T = o.shape[0]
    tm = _tile(T, 512)
    hw = DN_HEADS * DN_DIM
    nc = tm // SG_CHUNK

    def body(dm_ref, o_ref, z_ref, su_ref, sv_ref, dnn_ref, sgn_ref, sgw_ref, sgb_ref,
             do_ref, dz_ref, ddnn_ref, dsgn_ref, dsgw_ref, dsgb_ref):
        @pl.when(pl.program_id(0) == 0)
        def _():
            ddnn_ref[...] = jnp.zeros_like(ddnn_ref)
            dsgn_ref[...] = jnp.zeros_like(dsgn_ref)
            dsgw_ref[...] = jnp.zeros_like(dsgw_ref)
            dsgb_ref[...] = jnp.zeros_like(dsgb_ref)

        dnn = dnn_ref[...]
        ddnn = jnp.zeros((1, DN_DIM), F32)
        for h in range(DN_HEADS):
            sl = slice(h * DN_DIM, (h + 1) * DN_DIM)
            xhat, r = _rms_stats(o_ref[:, sl])
            z = z_ref[:, sl]
            sz = _sigmoid(z)
            doa = dm_ref[:, sl]
            dyn = doa * (z * sz)
            dz_ref[:, sl] = _bf(doa * xhat * dnn * _silu_grad(z, sz))
            do_ref[:, sl] = _rms_bwd(dyn, xhat, r, dnn)
            ddnn = ddnn + jnp.sum(dyn * xhat, axis=0, keepdims=True)
        ddnn_ref[...] += ddnn
        tri = lax.broadcasted_iota(jnp.int32, (SG_CHUNK, SG_CHUNK), 0) >= lax.broadcasted_iota(jnp.int32, (SG_CHUNK, SG_CHUNK), 1)
        lane = lax.broadcasted_iota(jnp.int32, (SG_CHUNK, 128), 1)
        dsgb = jnp.zeros((SG_CHUNK, 128), F32)
        dsgn_rows = []
        for g in range(SG_GROUPS):
            sl = slice(g * SG_DIM, (g + 1) * SG_DIM)
            sv = sv_ref[:, sl]
            su = su_ref[:, sl]
            xhat, r = _rms_stats(_gelu(sv))
            sgn = sgn_ref[g:g + 1, :]
            svn = _bf(xhat * sgn)
            sua = _gelu(su)
            wt = _bf(jnp.where(tri, sgw_ref[g], 0.0))
            bias = sgb_ref[:, g:g + 1]
            dw = jnp.zeros((SG_CHUNK, SG_CHUNK), F32)
            db = jnp.zeros((SG_CHUNK, 1), F32)
            dsua, dsvn = [], []
            for c in range(nc):
                rows = slice(c * SG_CHUNK, (c + 1) * SG_CHUNK)
                mixed = _dot(wt, svn[rows, :]) + bias
                dob = dm_ref[rows, hw + g * SG_DIM:hw + (g + 1) * SG_DIM]
                dsua.append(dob * mixed)
                dmixed = dob * sua[rows, :]
                dmb = _bf(dmixed)
                dsvn.append(_dot_tn(wt, dmb))
                dw = dw + _dot_nt(dmb, svn[rows, :])
                db = db + jnp.sum(dmixed, axis=1, keepdims=True)
            dsua = jnp.concatenate(dsua, axis=0) if nc > 1 else dsua[0]
            dsvn = jnp.concatenate(dsvn, axis=0) if nc > 1 else dsvn[0]
            dz_ref[:, hw + g * SG_DIM:hw + (g + 1) * SG_DIM] = _bf(dsua * _gelu_grad(su))
            dz_ref[:, 2 * hw + g * SG_DIM:2 * hw + (g + 1) * SG_DIM] = _bf(_rms_bwd(dsvn, xhat, r, sgn) * _gelu_grad(sv))
            dsgn_rows.append(jnp.sum(dsvn * xhat, axis=0, keepdims=True))
            dsgw_ref[g] += jnp.where(tri, dw, 0.0)
            dsgb = jnp.where(lane == g, db, dsgb)
        dsgn_ref[...] += jnp.concatenate(dsgn_rows, axis=0)
        dsgb_ref[...] += dsgb

    full = lambda shape: pl.BlockSpec(shape, lambda i: (0,) * len(shape))
    return pl.pallas_call(
        body, name=name, grid=(T // tm,),
        in_specs=[pl.BlockSpec((tm, 2 * hw), lambda i: (i, 0)),
                  pl.BlockSpec((tm, hw), lambda i: (i, 0)),
                  pl.BlockSpec((tm, hw), lambda i: (i, 3)),
                  pl.BlockSpec((tm, hw), lambda i: (i, 4)),
                  pl.BlockSpec((tm, hw), lambda i: (i, 5)),
                  full((1, DN_DIM)), full((SG_GROUPS, SG_DIM)), full((SG_GROUPS, SG_CHUNK, SG_CHUNK)),
                  full((SG_CHUNK, 128))],
        out_specs=[pl.BlockSpec((tm, hw), lambda i: (i, 0)),
                   pl.BlockSpec((tm, 3 * hw), lambda i: (i, 0)),
                   full((1, DN_DIM)), full((SG_GROUPS, SG_DIM)), full((SG_GROUPS, SG_CHUNK, SG_CHUNK)),
                   full((SG_CHUNK, 128))],
        out_shape=[jax.ShapeDtypeStruct((T, hw), F32), jax.ShapeDtypeStruct((T, 3 * hw), BF16),
                   jax.ShapeDtypeStruct((1, DN_DIM), F32), jax.ShapeDtypeStruct((SG_GROUPS, SG_DIM), F32),
                   jax.ShapeDtypeStruct((SG_GROUPS, SG_CHUNK, SG_CHUNK), F32),
                   jax.ShapeDtypeStruct((SG_CHUNK, 128), F32)],
        compiler_params=_params(("arbitrary",)),
    )(dmix, o, proj, proj, proj, dn_norm, sg_norm, sg_w, sg_bt)


def _window_sums(h, sign):
    sums, s, w = {}, h, 1
    while w < POOL_WINDOWS[-1]:
        s = s + _shift_rows(s, sign * w)
        w *= 2
        sums[w] = s
    return sums


def _pool_counts(t_global):
    return [jnp.minimum(t_global + 1, win).astype(F32) for win in POOL_WINDOWS]


def _pooled_groups(ext_h, row0, tm):
    sums = _window_sums(ext_h, 1)
    t_global = row0 + lax.broadcasted_iota(jnp.int32, (tm, 1), 0)
    counts = _pool_counts(t_global)
    out = []
    for gi, win in enumerate(POOL_WINDOWS):
        cols = slice(gi * POOL_DIM, (gi + 1) * POOL_DIM)
        out.append(sums[win][POOL_HALO:, cols] / counts[gi] - ext_h[POOL_HALO:, cols])
    return out


def _pool_fwd(x, nw, pool_w, pool_scale, layer, name):
    T, D = x.shape
    tm = _tile(T, 256)
    hb = tm // POOL_HALO

    def body(x_ref, xp_ref, n_ref, w_ref, s_ref, xo_ref):
        i = pl.program_id(0)
        prev = jnp.where(i == 0, 0.0, xp_ref[...])
        ext = jnp.concatenate([prev, x_ref[...]], axis=0)
        xhat, _ = _rms_stats(ext)
        pooled = _pooled_groups(xhat * n_ref[...], i * tm, tm)
        for gi in range(len(POOL_WINDOWS)):
            cols = slice(gi * POOL_DIM, (gi + 1) * POOL_DIM)
            xo_ref[:, cols] = x_ref[:, cols] + _dot(_bf(pooled[gi]), w_ref[gi]) * s_ref[:, cols]

    return pl.pallas_call(
        body, name=name, grid=(T // tm,),
        in_specs=[pl.BlockSpec((tm, D), lambda i: (i, 0)),
                  pl.BlockSpec((POOL_HALO, D), lambda i: (jnp.maximum(i * hb - 1, 0), 0)),
                  pl.BlockSpec((None, 1, D), lambda i: (layer, 0, 0)),
                  pl.BlockSpec(pool_w.shape, lambda i: (0, 0, 0)),
                  pl.BlockSpec((1, D), lambda i: (0, 0))],
        out_specs=pl.BlockSpec((tm, D), lambda i: (i, 0)),
        out_shape=jax.ShapeDtypeStruct((T, D), F32),
        compiler_params=_params(("parallel",)),
    )(x, x, nw, pool_w, pool_scale)


def _pool_bwd(dxo, x, nw, pool_w, pool_scale, layer, name):
    T, D = x.shape
    tm = _tile(T, 256)
    hb = tm // POOL_HALO
    nt = T // tm
    ng = len(POOL_WINDOWS)

    def body(dxo_ref, dxn_ref, x_ref, xp_ref, n_ref, w_ref, s_ref, dx_ref, dw_ref, ds_ref, dn_ref):
        i = pl.program_id(0)

        @pl.when(i == 0)
        def _():
            dw_ref[...] = jnp.zeros_like(dw_ref)
            ds_ref[...] = jnp.zeros_like(ds_ref)
            dn_ref[...] = jnp.zeros_like(dn_ref)

        prev = jnp.where(i == 0, 0.0, xp_ref[...])
        ext = jnp.concatenate([prev, x_ref[...]], axis=0)
        xhat_ext, r_ext = _rms_stats(ext)
        nv = n_ref[...]
        pooled = _pooled_groups(xhat_ext * nv, i * tm, tm)
        dxo = dxo_ref[...]
        scale = s_ref[...]
        dout_ext = jnp.concatenate([dxo, jnp.where(i == nt - 1, 0.0, dxn_ref[...])], axis=0) * scale
        t_ext = i * tm + lax.broadcasted_iota(jnp.int32, (tm + POOL_HALO, 1), 0)
        counts = _pool_counts(t_ext)
        dh_cols, ds_cols = [], []
        for gi, win in enumerate(POOL_WINDOWS):
            cols = slice(gi * POOL_DIM, (gi + 1) * POOL_DIM)
            wg = w_ref[gi]
            pb = _bf(pooled[gi])
            doutb = _bf(dout_ext[:, cols])
            dpooled = _dot_nt(doutb, wg)
            ahead = _window_sums(dpooled / counts[gi], -1)[win]
            dh_cols.append(ahead[:tm, :] - dpooled[:tm, :])
            dw_ref[gi] += _dot_tn(pb, doutb[:tm, :])
            ds_cols.append(jnp.sum(dxo[:, cols] * _dot(pb, wg), axis=0, keepdims=True))
        dh = jnp.concatenate(dh_cols, axis=1)
        xhat, r = xhat_ext[POOL_HALO:, :], r_ext[POOL_HALO:, :]
        dx_ref[...] = dxo + _rms_bwd(dh, xhat, r, nv)
        dn_ref[...] += jnp.sum(dh * xhat, axis=0, keepdims=True)
        ds_ref[...] += jnp.concatenate(ds_cols, axis=1)

    last_halo = T // POOL_HALO - 1
    return pl.pallas_call(
        body, name=name, grid=(nt,),
        in_specs=[pl.BlockSpec((tm, D), lambda i: (i, 0)),
                  pl.BlockSpec((POOL_HALO, D), lambda i: (jnp.minimum((i + 1) * hb, last_halo), 0)),
                  pl.BlockSpec((tm, D), lambda i: (i, 0)),
                  pl.BlockSpec((POOL_HALO, D), lambda i: (jnp.maximum(i * hb - 1, 0), 0)),
                  pl.BlockSpec((None, 1, D), lambda i: (layer, 0, 0)),
                  pl.BlockSpec(pool_w.shape, lambda i: (0, 0, 0)),
                  pl.BlockSpec((1, D), lambda i: (0, 0))],
        out_specs=[pl.BlockSpec((tm, D), lambda i: (i, 0)),
                   pl.BlockSpec((ng, POOL_DIM, POOL_DIM), lambda i: (0, 0, 0)),
                   pl.BlockSpec((1, D), lambda i: (0, 0)),
                   pl.BlockSpec((1, D), lambda i: (0, 0))],
        out_shape=[jax.ShapeDtypeStruct((T, D), F32), jax.ShapeDtypeStruct((ng, POOL_DIM, POOL_DIM), F32),
                   jax.ShapeDtypeStruct((1, D), F32), jax.ShapeDtypeStruct((1, D), F32)],
        compiler_params=_params(("arbitrary",)),
    )(dxo, dxo, x, x, nw, pool_w, pool_scale)


def _loss_head(x, target, fn, name):
    T, D = x.shape
    tm = _tile(T, 512)

    def body(x_ref, t_ref, n_ref, loss_ref, dx_ref, dn_ref):
        @pl.when(pl.program_id(0) == 0)
        def _():
            loss_ref[...] = jnp.zeros_like(loss_ref)
            dn_ref[...] = jnp.zeros_like(dn_ref)

        xhat, r = _rms_stats(x_ref[...])
        nv = n_ref[...]
        err = xhat * nv - t_ref[...]
        part = jnp.sum(jnp.sum(err * err, axis=1, keepdims=True), axis=0, keepdims=True)
        loss_ref[...] += 0.5 * part / D
        dy = err / D
        dx_ref[...] = _rms_bwd(dy, xhat, r, nv)
        dn_ref[...] += jnp.sum(dy * xhat, axis=0, keepdims=True)

    row = pl.BlockSpec((tm, D), lambda i: (i, 0))
    return pl.pallas_call(
        body, name=name, grid=(T // tm,),
        in_specs=[row, row, pl.BlockSpec((1, D), lambda i: (0, 0))],
        out_specs=[pl.BlockSpec((1, 1), lambda i: (0, 0)), row, pl.BlockSpec((1, D), lambda i: (0, 0))],
        out_shape=[jax.ShapeDtypeStruct((1, 1), F32), jax.ShapeDtypeStruct((T, D), F32),
                   jax.ShapeDtypeStruct((1, D), F32)],
        compiler_params=_params(("arbitrary",)),
    )(x, target, fn)


def _adamw(w, g, m, v, name):
    R, C = w.shape
    br = R
    for cand in (512, 256, 128, 64, 32, 16, 8):
        if R % cand == 0 and cand * C * 4 <= 2 * 1024 * 1024:
            br = cand
            break

    def body(w_ref, g_ref, m_ref, v_ref, d_ref, mo_ref, vo_ref):
        gv = g_ref[...]
        m_new = ADAM_B1 * m_ref[...] + (1.0 - ADAM_B1) * gv
        v_new = ADAM_B2 * v_ref[...] + (1.0 - ADAM_B2) * (gv * gv)
        m_hat = m_new / (1.0 - ADAM_B1 ** ADAM_STEP)
        v_hat = v_new / (1.0 - ADAM_B2 ** ADAM_STEP)
        d_ref[...] = -ADAM_LR * (m_hat / (jnp.sqrt(v_hat) + ADAM_EPS) + ADAM_WD * w_ref[...])
        mo_ref[...] = m_new
        vo_ref[...] = v_new

    blk = pl.BlockSpec((br, C), lambda i: (i, 0))
    return pl.pallas_call(
        body, name=name, grid=(R // br,), in_specs=[blk] * 4, out_specs=[blk] * 3,
        out_shape=[jax.ShapeDtypeStruct((R, C), F32)] * 3,
        compiler_params=_params(("parallel",)),
    )(w, g, m, v)


def _mesh_pos():
    return lax.axis_index("x"), lax.axis_index("y"), lax.axis_index("c")


def _other_chips(x, y):
    return [(1 - x, y), (x, 1 - y), (1 - x, 1 - y)]


def _half_of(ref, shape, h):
    size = shape[0] // 2
    return ref.at[pl.ds(h * size, size)]


class _ChipGather:
    def __init__(self, shards, split):
        self.shards, self.split = list(shards), list(split)
        self.operands = self.shards
        n = len(self.shards)
        self.out_shape = [jax.ShapeDtypeStruct((N_CHIPS,) + s.shape, s.dtype) for s in self.shards]
        self.scratch = [pltpu.SemaphoreType.DMA((n, 3))] * 4 + [pltpu.SemaphoreType.DMA((n,))] * 2

    def _piece(self, a, ref, h):
        return _half_of(ref, self.shards[a].shape, h) if self.split[a] else ref

    def _own(self, a, ins, outs, sems):
        x, y, c = _mesh_pos()
        return pltpu.make_async_remote_copy(ins[a], outs[a].at[2 * x + y], sems[4].at[a], sems[5].at[a],
                                            device_id=(x, y, 1 - c), device_id_type=MESH)

    def start(self, ins, outs, sems):
        send_sems, recv_sems = sems[0], sems[1]
        x, y, c = _mesh_pos()
        me = 2 * x + y
        for a in range(len(ins)):
            for k, (px, py) in enumerate(_other_chips(x, y)):
                pltpu.make_async_remote_copy(self._piece(a, ins[a], c), self._piece(a, outs[a].at[me], c),
                                             send_sems.at[a, k], recv_sems.at[a, k],
                                             device_id=(px, py, c), device_id_type=MESH).start()
            self._own(a, ins, outs, sems).start()

    def finish(self, ins, outs, sems):
        send_sems, recv_sems, fwd_send_sems, fwd_recv_sems = sems[:4]
        x, y, c = _mesh_pos()
        sibling = (x, y, 1 - c)
        chips = _other_chips(x, y)
        n = len(ins)
        forwards = []
        for a in range(n):
            self._own(a, ins, outs, sems).wait()
        for a in range(n):
            for k, (px, py) in enumerate(chips):
                landed = self._piece(a, outs[a].at[2 * px + py], c)
                pltpu.make_async_remote_copy(landed, landed, send_sems.at[a, k], recv_sems.at[a, k],
                                             device_id=(px, py, c), device_id_type=MESH).wait_recv()
                if self.split[a]:
                    fwd = pltpu.make_async_remote_copy(landed, landed, fwd_send_sems.at[a, k], fwd_recv_sems.at[a, k],
                                                       device_id=sibling, device_id_type=MESH)
                    fwd.start()
                    forwards.append(fwd)
        for a in range(n):
            if self.split[a]:
                for k, (px, py) in enumerate(chips):
                    other = self._piece(a, outs[a].at[2 * px + py], 1 - c)
                    pltpu.make_async_remote_copy(other, other, fwd_send_sems.at[a, k], fwd_recv_sems.at[a, k],
                                                 device_id=sibling, device_id_type=MESH).wait_recv()
        for a in range(n):
            for k, (px, py) in enumerate(chips):
                sent = self._piece(a, ins[a], c)
                pltpu.make_async_remote_copy(sent, sent, send_sems.at[a, k], recv_sems.at[a, k],
                                             device_id=(px, py, c), device_id_type=MESH).wait_send()
        for fwd in forwards:
            fwd.wait_send()

    def finalize(self, gathered):
        return gathered

    def run(self, name):
        n = len(self.shards)

        def body(*refs):
            ins, outs, sems = refs[:n], refs[n:2 * n], refs[2 * n:]
            self.start(ins, outs, sems)
            self.finish(ins, outs, sems)

        gathered = pl.pallas_call(
            body, name=name, in_specs=[ANY] * n, out_specs=[ANY] * n, out_shape=self.out_shape,
            scratch_shapes=self.scratch, compiler_params=pltpu.CompilerParams(has_side_effects=True),
        )(*self.shards)
        return self.finalize(gathered)


def _ffn_weight_grads(hb, dgu, a, dyb, tag):
    dwin = _matmul_tn(hb, dgu, D_MODEL, FF_CHUNK, f"{tag}_dw_in", stack_n=True)
    dwo = _matmul_tn(a, dyb, FF_CHUNK, D_MODEL, f"{tag}_dw_out")
    return dwin, dwo.reshape(N_CHIPS, D_FF // N_CHIPS, D_MODEL)


def _local_step(x, target, w, late=None, reduce=False):
    g = {}
    acts = []
    w = dict(w)

    def ffn_weights(which, layer):
        return w[f"n{which}"], w[f"win{which}_l{layer}"], w[f"wout{which}_l{layer}"]

    def hosting(name):
        exchange, layouts = late.get(name, (None, None)) if late else (None, None)
        return exchange, (lambda arrived: w.update(layouts(arrived)) if exchange is not None else None)

    def ffn(xin, which, layer):
        name = f"ffn{which}_l{layer}_fwd"
        exchange, keep = hosting(name)
        (xo, gv, uv, hb), arrived = _ffn_fwd(xin, *ffn_weights(which, layer), layer, name, exchange)
        keep(arrived)
        acts.append((xin, gv, uv, hb))
        return xo

    x1 = ffn(x, 1, 0)
    hb_mix = _rms_fwd_call(x1, w["nmix"], 0, "ab_norm_fwd")
    proj = _matmul(hb_mix, w["wp"], "ab_in_proj")
    qkv = _conv_fwd(proj, w["conv_w"], "dn_conv_fwd")
    exchange, keep = hosting("dn_prep")
    (dn_u, dn_w, dn_p, dn_qd, dn_kt, dn_gl, dn_inv), arrived = _dn_prep(qkv, proj, w["hp"], "dn_prep", exchange)
    keep(arrived)
    exchange, keep = hosting("dn_scan")
    (o, dn_vn, sall), arrived = _dn_scan(dn_u, dn_w, dn_p, dn_qd, dn_kt, dn_gl, "dn_scan", exchange)
    keep(arrived)
    mix = _mix_fwd(o, proj, w["dn_norm"], w["sg_norm"], w["sg_w"], w["sg_bt"], "ab_gate_fwd")
    x2 = _matmul(mix, w["wo"], "ab_out_proj", res=x1)
    x3 = ffn(x2, 2, 0)
    x4 = ffn(x3, 1, 1)
    x5 = _pool_fwd(x4, w["nmix"], w["pool_w"], w["pool_scale"], 1, "pool_fwd")
    x6 = ffn(x5, 2, 1)
    loss, dx, g["fn"] = _loss_head(x6, target, w["fn"], "loss_head")

    dn = {1: [None, None], 2: [None, None]}
    dwin = {1: [None, None], 2: [None, None]}
    dwout = {1: [None, None], 2: [None, None]}

    def ffn_back(dxo, which, layer, saved, exchange=None):
        nw, win, wout = ffn_weights(which, layer)
        xin, gv, uv, hb = saved
        tag = f"ffn{which}_l{layer}"
        (dxi, dgu, a, dyb, dnw), arrived = _ffn_bwd(dxo, xin, nw, gv, uv, win, wout, layer, f"{tag}_bwd", exchange)
        dn[which][layer] = dnw
        dwin[which][layer], dwout[which][layer] = _ffn_weight_grads(hb, dgu, a, dyb, tag)
        return dxi, arrived

    for which in (1, 2):
        g[f"win{which}"] = dwin[which]
        g[f"wout{which}"] = dwout[which]
    reduced = {}

    def open_round(tag, keys):
        have = _sharded_grads(g)
        return _GradRound(tag, {k: have[k] for k in keys})

    dx, _ = ffn_back(dx, 2, 1, acts[3])
    dx, g["pool_w"], g["pool_scale"], dnmix1 = _pool_bwd(dx, x4, w["nmix"], w["pool_w"], w["pool_scale"], 1, "pool_bwd")
    dx, _ = ffn_back(dx, 1, 1, acts[2])
    round_a = open_round("a", REDUCE_ROUNDS[0]) if reduce else None
    dx2, arrived = ffn_back(dx, 2, 0, acts[1], round_a.swap if reduce else None)
    if reduce:
        round_a.pair_sum(arrived)
    round_b = open_round("b", REDUCE_ROUNDS[1]) if reduce else None
    dmix = _matmul(dx2, w["wo"], "ab_out_proj_bwd", trans_b=True)
    g["wo"] = _matmul_tn(mix, dx2, D_MODEL, D_MODEL, "ab_out_proj_dw")
    do, dzuv, g["dn_norm"], g["sg_norm"], g["sg_w"], g["sg_bt"] = _mix_bwd(
        dmix, o, proj, w["dn_norm"], w["sg_norm"], w["sg_w"], w["sg_bt"], "ab_gate_bwd")
    (dvn, dkt, dgl), arrived = _dn_scan_bwd(dn_w, dn_p, dn_qd, dn_kt, dn_gl, dn_vn, sall, do, "dn_scan_bwd",
                                            round_b.swap if reduce else None)
    if reduce:
        round_b.pair_sum(arrived)
    (dqkv_act, dba, g["hp"]), arrived = _dn_prep_bwd(qkv, proj, w["hp"], sall, dn_vn, do, dvn, dkt, dgl, dn_inv, dn_u, dn_w,
                                                     "dn_prep_bwd", round_b.scatter if reduce else None)
    if reduce:
        reduced.update(round_b.finish(arrived))
    dqkv, g["conv_w"] = _conv_bwd(dqkv_act, proj, w["conv_w"], "dn_conv_bwd")
    dproj = [dqkv, dzuv, dba]
    dh = _matmul_nt_pieces(dproj, w["wp"], "ab_in_proj_bwd")
    g["wp"] = [_matmul_tn(hb_mix, piece, D_MODEL, 768, f"ab_in_proj_dw_{k}") for k, piece in enumerate(dproj)]
    dx1, dnmix0 = _rms_bwd_call(dh, x1, w["nmix"], dx2, 0, "ab_norm_bwd")
    dx0, arrived = ffn_back(dx1, 1, 0, acts[0], round_a.scatter if reduce else None)
    if reduce:
        reduced.update(round_a.finish(arrived))
        round_c = open_round("c", REDUCE_ROUNDS[2])
        round_c.pair_sum(round_c.swap.run("grad_c_pair_swap"))
        reduced.update(round_c.finish(round_c.scatter.run("grad_c_chip_scatter")))

    g["n1"] = jnp.concatenate(dn[1], axis=0)
    g["n2"] = jnp.concatenate(dn[2], axis=0)
    g["nmix"] = jnp.concatenate([dnmix0, dnmix1], axis=0)
    return loss, dx0, g, reduced


SHARDED = ("ffn1_w_in", "ffn1_w_out", "ffn2_w_in", "ffn2_w_out", "ab_w_in", "ab_w_out", "pool_w", "dn_conv_w", "pool_scale")
REPLICATED = ("ffn_norm1", "mix_norm", "ffn_norm2", "dn_a_log", "dn_dt_bias", "dn_out_norm", "sg_norm", "sg_w", "sg_b", "final_norm")
QKVZ = 4 * DN_HEADS * DN_DIM
N_GATES = 2 * DN_HEADS
IN_PROJ = QKVZ + N_GATES + 2 * SG_GROUPS * SG_DIM


def _shard_pieces(wts, keys):
    out = []
    for n, layer in keys:
        a = wts[n][0 if layer is None else layer]
        a = a[None] if a.ndim == 1 else a
        out.append(a.astype(BF16) if n in MATRICES else a)
    return out


def _replicated_layouts(rep):
    per_layer = lambda a: a.reshape(a.shape[0], 1, D_MODEL)
    w = {"n1": per_layer(rep["ffn_norm1"]), "nmix": per_layer(rep["mix_norm"]), "n2": per_layer(rep["ffn_norm2"])}
    hp = jnp.zeros((8, 128), F32)
    w["hp"] = hp.at[0, DN_HEADS:N_GATES].set(rep["dn_a_log"][0]).at[1, DN_HEADS:N_GATES].set(rep["dn_dt_bias"][0])
    w["dn_norm"] = rep["dn_out_norm"]
    w["sg_norm"] = rep["sg_norm"][0]
    w["sg_w"] = rep["sg_w"][0]
    w["sg_bt"] = jnp.zeros((SG_CHUNK, 128), F32).at[:, :SG_GROUPS].set(rep["sg_b"][0].T)
    w["fn"] = rep["final_norm"].reshape(1, D_MODEL)
    return w


def _layouts_from(gathered):
    w = {}
    for (n, layer), a in gathered.items():
        if n in ("ffn1_w_in", "ffn2_w_in"):
            w[f"win{n[3]}_l{layer}"] = a
        elif n in ("ffn1_w_out", "ffn2_w_out"):
            w[f"wout{n[3]}_l{layer}"] = a
        elif n == "ab_w_in":
            ab_in = jnp.transpose(a, (1, 0, 2)).reshape(D_MODEL, IN_PROJ)
            w["wp"] = jnp.concatenate([ab_in[:, :QKVZ], ab_in[:, QKVZ + N_GATES:], ab_in[:, QKVZ:QKVZ + N_GATES],
                                       jnp.zeros((D_MODEL, PROJ_W - IN_PROJ), ab_in.dtype)], axis=1)
        elif n == "dn_conv_w":
            w["conv_w"] = jnp.transpose(a, (1, 0, 2)).reshape(DN_CONV, 3 * DN_HEADS * DN_DIM)
        elif n == "ab_w_out":
            w["wo"] = a.reshape(D_MODEL, D_MODEL)
        elif n == "pool_w":
            w["pool_w"] = jnp.transpose(a, (1, 0, 2, 3)).reshape(len(POOL_WINDOWS), POOL_DIM, POOL_DIM)
        elif n == "pool_scale":
            w["pool_scale"] = a.reshape(1, D_MODEL)
    return w


def _sharded_grads(g):
    nw = len(POOL_WINDOWS)
    sharded = {}
    for n, key in (("ffn1_w_in", "win1"), ("ffn1_w_out", "wout1"), ("ffn2_w_in", "win2"), ("ffn2_w_out", "wout2")):
        for layer, a in enumerate(g.get(key, ())):
            if a is not None:
                sharded[(n, layer)] = a
    if "wp" in g:
        qkv, zuv, gates = g["wp"]
        z_width = DN_HEADS * DN_DIM
        ab_in = jnp.concatenate([qkv, zuv[:, :z_width], gates[:, :N_GATES], zuv[:, z_width:]], axis=1)
        sharded[("ab_w_in", None)] = jnp.transpose(ab_in.reshape(D_MODEL, N_CHIPS, IN_PROJ // N_CHIPS), (1, 0, 2))
    if "wo" in g:
        sharded[("ab_w_out", None)] = g["wo"].reshape(N_CHIPS, D_MODEL // N_CHIPS, D_MODEL)
    if "pool_w" in g:
        sharded[("pool_w", None)] = jnp.transpose(g["pool_w"].reshape(nw, N_CHIPS, POOL_DIM // N_CHIPS, POOL_DIM), (1, 0, 2, 3))
    if "conv_w" in g:
        sharded[("dn_conv_w", None)] = jnp.transpose(g["conv_w"].reshape(DN_CONV, N_CHIPS, -1), (1, 0, 2))
    if "pool_scale" in g:
        sharded[("pool_scale", None)] = g["pool_scale"].reshape(N_CHIPS, 1, D_MODEL // N_CHIPS)
    return sharded


def _replicated_grads(g):
    rep = {
        "ffn_norm1": g["n1"], "mix_norm": g["nmix"], "ffn_norm2": g["n2"],
        "dn_a_log": g["hp"][0:1, DN_HEADS:N_GATES], "dn_dt_bias": g["hp"][1:2, DN_HEADS:N_GATES],
        "dn_out_norm": g["dn_norm"], "sg_norm": g["sg_norm"][None], "sg_w": g["sg_w"][None],
        "sg_b": g["sg_bt"][:, :SG_GROUPS].T[None], "final_norm": g["fn"].reshape(D_MODEL),
    }
    return rep


def _as_halves(a):
    shape = a.shape[1:]
    if len(shape) >= 2 and shape[0] % 2 == 0:
        return a.reshape(N_CHIPS, 2, -1, shape[-1])
    return a.reshape(N_CHIPS, 2, 1, -1)


def _row_block(rows):
    for cand in (256, 176, 128, 64, 32, 16):
        if rows % cand == 0:
            return cand
    return rows


def _from_halves(pairs, core, shape):
    mine_first = jnp.stack([t for mine, other in pairs for t in (mine, other)])
    other_first = jnp.stack([t for mine, other in pairs for t in (other, mine)])
    return jnp.where(core == 0, mine_first, other_first).reshape(shape)


class _PairSwap:
    def __init__(self, packs):
        self.operands = list(packs)
        n = len(self.operands)
        self.out_shape = [jax.ShapeDtypeStruct((p.shape[0],) + p.shape[2:], p.dtype) for p in self.operands]
        self.scratch = [pltpu.SemaphoreType.DMA((n,))] * 2

    def _copies(self, ins, outs, sems):
        x, y, c = _mesh_pos()
        return [pltpu.make_async_remote_copy(ins[k].at[:, 1 - c], outs[k], sems[0].at[k], sems[1].at[k],
                                             device_id=(x, y, 1 - c), device_id_type=MESH) for k in range(len(ins))]

    def start(self, ins, outs, sems):
        for cp in self._copies(ins, outs, sems):
            cp.start()

    def finish(self, ins, outs, sems):
        for cp in self._copies(ins, outs, sems):
            cp.wait()

    def finalize(self, results):
        return results

    def run(self, name):
        n = len(self.operands)

        def body(*refs):
            ins, outs, sems = refs[:n], refs[n:2 * n], refs[2 * n:]
            self.start(ins, outs, sems)
            self.finish(ins, outs, sems)

        return pl.pallas_call(
            body, name=name, in_specs=[ANY] * n, out_specs=[ANY] * n, out_shape=self.out_shape, scratch_shapes=self.scratch,
            compiler_params=pltpu.CompilerParams(has_side_effects=True),
        )(*self.operands)


def _add_pair(pack, recv, core, name):
    nchip, _, rows, lanes = pack.shape
    rb = _row_block(rows)

    def body(c_ref, a_ref, b_ref, o32_ref, o16_ref):
        s = a_ref[...] + b_ref[...]
        o32_ref[...] = s
        o16_ref[...] = _bf(s)

    blk = pl.BlockSpec((None, rb, lanes), lambda p, i, c: (p, i, 0))
    return pl.pallas_call(
        body, name=name,
        grid_spec=pltpu.PrefetchScalarGridSpec(
            num_scalar_prefetch=1, grid=(nchip, rows // rb),
            in_specs=[pl.BlockSpec((None, None, rb, lanes), lambda p, i, c: (p, c[0], i, 0)), blk],
            out_specs=[blk, blk]),
        out_shape=[jax.ShapeDtypeStruct((nchip, rows, lanes), F32), jax.ShapeDtypeStruct((nchip, rows, lanes), BF16)],
        compiler_params=_params(("parallel", "parallel")),
    )(core, pack, recv)


class _ChipScatter:
    def __init__(self, parts16):
        self.operands = list(parts16)
        n = len(self.operands)
        self.out_shape = [jax.ShapeDtypeStruct((N_CHIPS - 1,) + p.shape[1:], p.dtype) for p in self.operands]
        self.scratch = [pltpu.SemaphoreType.DMA((n, N_CHIPS - 1))] * 2

    def _copies(self, ins, outs, sems):
        x, y, c = _mesh_pos()
        return [pltpu.make_async_remote_copy(ins[a].at[2 * px + py], outs[a].at[k], sems[0].at[a, k], sems[1].at[a, k],
                                             device_id=(px, py, c), device_id_type=MESH)
                for a in range(len(ins)) for k, (px, py) in enumerate(_other_chips(x, y))]

    def start(self, ins, outs, sems):
        for cp in self._copies(ins, outs, sems):
            cp.start()

    def finish(self, ins, outs, sems):
        for cp in self._copies(ins, outs, sems):
            cp.wait()

    def finalize(self, results):
        return results

    def run(self, name):
        n = len(self.operands)

        def body(*refs):
            ins, outs, sems = refs[:n], refs[n:2 * n], refs[2 * n:]
            self.start(ins, outs, sems)
            self.finish(ins, outs, sems)

        return pl.pallas_call(
            body, name=name, in_specs=[ANY] * n, out_specs=[ANY] * n, out_shape=self.out_shape, scratch_shapes=self.scratch,
            compiler_params=pltpu.CompilerParams(has_side_effects=True),
        )(*self.operands)


def _sum_chips(part32, recv16, chip, name):
    nchip, rows, lanes = part32.shape
    rb = _row_block(rows)

    def body(p_ref, own_ref, r_ref, o_ref):
        s = own_ref[...]
        for k in range(nchip - 1):
            s = s + r_ref[k].astype(F32)
        o_ref[...] = s

    return pl.pallas_call(
        body, name=name,
        grid_spec=pltpu.PrefetchScalarGridSpec(
            num_scalar_prefetch=1, grid=(rows // rb,),
            in_specs=[pl.BlockSpec((None, rb, lanes), lambda i, p: (p[0], i, 0)),
                      pl.BlockSpec((nchip - 1, rb, lanes), lambda i, p: (0, i, 0))],
            out_specs=pl.BlockSpec((rb, lanes), lambda i, p: (i, 0))),
        out_shape=jax.ShapeDtypeStruct((rows, lanes), F32),
        compiler_params=_params(("parallel",)),
    )(chip, part32, recv16)


def _share_with_sibling(halves, name):
    n = len(halves)

    def body(*refs):
        ins, outs, send_sems, recv_sems = refs[:n], refs[n:2 * n], refs[2 * n], refs[2 * n + 1]
        x, y, c = _mesh_pos()
        copies = [pltpu.make_async_remote_copy(ins[k], outs[k], send_sems.at[k], recv_sems.at[k],
                                               device_id=(x, y, 1 - c), device_id_type=MESH) for k in range(n)]
        for cp in copies:
            cp.start()
        for cp in copies:
            cp.wait()

    return pl.pallas_call(
        body, name=name, in_specs=[ANY] * n, out_specs=[ANY] * n,
        out_shape=[jax.ShapeDtypeStruct(h.shape, h.dtype) for h in halves],
        scratch_shapes=[pltpu.SemaphoreType.DMA((n,)), pltpu.SemaphoreType.DMA((n,))],
        compiler_params=pltpu.CompilerParams(has_side_effects=True),
    )(*halves)


class _GradRound:
    def __init__(self, tag, pieces):
        self.tag, self.keys = tag, list(pieces)
        self.shapes = [pieces[k].shape[1:] for k in self.keys]
        self.packs = [_as_halves(pieces[k]) for k in self.keys]
        self.swap = _PairSwap(self.packs)

    def pair_sum(self, recvs):
        _, _, c = _mesh_pos()
        core = jnp.reshape(c, (1,)).astype(jnp.int32)
        sums = [_add_pair(p, r, core, f"grad_{self.tag}_pair_add_{i}") for i, (p, r) in enumerate(zip(self.packs, recvs))]
        self.parts32 = [s[0] for s in sums]
        self.scatter = _ChipScatter([s[1] for s in sums])

    def finish(self, recvs16):
        x, y, _ = _mesh_pos()
        chip = jnp.reshape(2 * x + y, (1,)).astype(jnp.int32)
        halves = [_sum_chips(p, r, chip, f"grad_{self.tag}_chip_sum_{i}") for i, (p, r) in enumerate(zip(self.parts32, recvs16))]
        others = _share_with_sibling(halves, f"grad_{self.tag}_pair_share")
        return dict(zip(self.keys, zip(halves, others)))


def _pack_small(vals):
    parts = []
    for n in REPLICATED:
        flat = vals[n].reshape(-1)
        rows = -(-flat.shape[0] // 128)
        rows = -(-rows // 8) * 8
        parts.append(jnp.pad(flat, (0, rows * 128 - flat.shape[0])).reshape(rows, 128))
    return jnp.concatenate(parts, axis=0)


def _unpack_small(pack, like):
    out, off = {}, 0
    for n in REPLICATED:
        size = like[n].size
        rows = -(-size // 128)
        rows = -(-rows // 8) * 8
        out[n] = pack[off:off + rows].reshape(-1)[:size].reshape(like[n].shape)
        off += rows
    return out


def _all_to_all_small(pack, name):
    rows, lanes = pack.shape
    flips = [(dx, dy, dc) for dx in (0, 1) for dy in (0, 1) for dc in (0, 1)][1:]

    def body(src_ref, out_ref, send_sems, recv_sems, local_sem):
        x, y, c = _mesh_pos()
        me = 4 * x + 2 * y + c
        loc = pltpu.make_async_copy(src_ref, out_ref.at[me], local_sem)
        loc.start()
        copies = []
        for k, (dx, dy, dc) in enumerate(flips):
            peer = (x ^ dx, y ^ dy, c ^ dc)
            cp = pltpu.make_async_remote_copy(src_ref, out_ref.at[me], send_sems.at[k], recv_sems.at[k],
                                              device_id=peer, device_id_type=MESH)
            cp.start()
            copies.append(cp)
        for k, (dx, dy, dc) in enumerate(flips):
            peer = (x ^ dx, y ^ dy, c ^ dc)
            pltpu.make_async_remote_copy(src_ref, out_ref.at[4 * peer[0] + 2 * peer[1] + peer[2]], send_sems.at[k],
                                         recv_sems.at[k], device_id=peer, device_id_type=MESH).wait_recv()
        for cp in copies:
            cp.wait_send()
        loc.wait()

    return pl.pallas_call(
        body, name=name, in_specs=[ANY], out_specs=ANY,
        out_shape=jax.ShapeDtypeStruct((8, rows, lanes), pack.dtype),
        scratch_shapes=[pltpu.SemaphoreType.DMA((7,)), pltpu.SemaphoreType.DMA((7,)), pltpu.SemaphoreType.DMA],
        compiler_params=pltpu.CompilerParams(has_side_effects=True),
    )(pack)


def _sum_devices(stack, name):
    ndev, rows, lanes = stack.shape

    def body(s_ref, o_ref):
        s = s_ref[0]
        for d in range(1, ndev):
            s = s + s_ref[d]
        o_ref[...] = s

    return pl.pallas_call(
        body, name=name, grid=(1,),
        in_specs=[pl.BlockSpec((ndev, rows, lanes), lambda i: (0, 0, 0))],
        out_specs=pl.BlockSpec((rows, lanes), lambda i: (0, 0)),
        out_shape=jax.ShapeDtypeStruct((rows, lanes), F32),
    )(stack)


WEIGHT_ORDER = ("ffn_norm1", "ffn1_w_in", "ffn1_w_out", "mix_norm", "ffn_norm2", "ffn2_w_in", "ffn2_w_out", "ab_w_in",
                "dn_conv_w", "dn_a_log", "dn_dt_bias", "dn_out_norm", "sg_norm", "sg_w", "sg_b", "ab_w_out", "pool_w",
                "pool_scale", "final_norm")
MATRICES = ("ffn1_w_in", "ffn1_w_out", "ffn2_w_in", "ffn2_w_out", "ab_w_in", "ab_w_out", "pool_w")
GATHER_FIRST = (("ffn1_w_in", 0), ("ffn1_w_out", 0))
GATHER_LATER = {"ffn1_l0_fwd": (("ab_w_in", None), ("dn_conv_w", None), ("ab_w_out", None)),
                "dn_prep": (("ffn2_w_in", 0), ("ffn2_w_out", 0)),
                "dn_scan": (("ffn1_w_in", 1), ("ffn1_w_out", 1)),
                "ffn2_l0_fwd": (("pool_w", None), ("pool_scale", None), ("ffn2_w_in", 1), ("ffn2_w_out", 1))}
REDUCE_ROUNDS = ((("ffn2_w_in", 1), ("ffn2_w_out", 1), ("ffn1_w_in", 1), ("ffn1_w_out", 1), ("pool_w", None), ("pool_scale", None)),
                 (("ffn2_w_in", 0), ("ffn2_w_out", 0)),
                 (("ffn1_w_in", 0), ("ffn1_w_out", 0), ("ab_w_out", None), ("ab_w_in", None), ("dn_conv_w", None)))


def _as_2d(a):
    return a.reshape(-1, a.shape[-1])


def kernel(x, ffn_norm1, ffn1_w_in, ffn1_w_out, mix_norm, ffn_norm2, ffn2_w_in, ffn2_w_out, ab_w_in, dn_conv_w, dn_a_log, dn_dt_bias, dn_out_norm, sg_norm, sg_w, sg_b, ab_w_out, pool_w, pool_scale, final_norm, loss_target, m_ffn_norm1, m_ffn1_w_in, m_ffn1_w_out, m_mix_norm, m_ffn_norm2, m_ffn2_w_in, m_ffn2_w_out, m_ab_w_in, m_dn_conv_w, m_dn_a_log, m_dn_dt_bias, m_dn_out_norm, m_sg_norm, m_sg_w, m_sg_b, m_ab_w_out, m_pool_w, m_pool_scale, m_final_norm, v_ffn_norm1, v_ffn1_w_in, v_ffn1_w_out, v_mix_norm, v_ffn_norm2, v_ffn2_w_in, v_ffn2_w_out, v_ab_w_in, v_dn_conv_w, v_dn_a_log, v_dn_dt_bias, v_dn_out_norm, v_sg_norm, v_sg_w, v_sg_b, v_ab_w_out, v_pool_w, v_pool_scale, v_final_norm):
    given = dict(locals())
    wts = {n: given[n] for n in WEIGHT_ORDER}
    mom_m = {n: given["m_" + n] for n in WEIGHT_ORDER}
    mom_v = {n: given["v_" + n] for n in WEIGHT_ORDER}

    rep = {n: wts[n] for n in REPLICATED}
    first = _ChipGather(_shard_pieces(wts, GATHER_FIRST), [n in MATRICES for n, _ in GATHER_FIRST])
    w = {**_replicated_layouts(rep), **_layouts_from(dict(zip(GATHER_FIRST, first.run("weight_gather_first"))))}
    late = {host: (_ChipGather(_shard_pieces(wts, keys), [n in MATRICES for n, _ in keys]),
                   functools.partial(lambda keys, arrived: _layouts_from(dict(zip(keys, arrived))), keys))
            for host, keys in GATHER_LATER.items()}

    loss, dx, g, reduced = _local_step(x[0], loss_target[0], w, reduce=True, late=late)
    g_rep = _replicated_grads(g)
    grads = {}
    core = lax.axis_index("c")
    for n in SHARDED:
        layers = [reduced[(n, layer)] for layer in range(wts[n].shape[0])] if (n, 0) in reduced else [reduced[(n, None)]]
        grads[n] = _from_halves(layers, core, wts[n].shape)
    small = _sum_devices(_all_to_all_small(_pack_small(g_rep), "grad_small_exchange"), "grad_small_sum")
    grads.update(_unpack_small(small, rep))

    delta, new_m, new_v = {}, {}, {}
    for n in SHARDED:
        d, m1, v1 = _adamw(_as_2d(wts[n]), _as_2d(grads[n]), _as_2d(mom_m[n]), _as_2d(mom_v[n]), f"adamw_{n}")
        delta[n], new_m[n], new_v[n] = (t.reshape(wts[n].shape) for t in (d, m1, v1))
    d, m1, v1 = _adamw(_pack_small(rep), small, _pack_small({n: mom_m[n] for n in REPLICATED}),
                       _pack_small({n: mom_v[n] for n in REPLICATED}), "adamw_replicated")
    for tgt, packed in ((delta, d), (new_m, m1), (new_v, v1)):
        tgt.update(_unpack_small(packed, rep))

    total = lax.psum(loss[0, 0], ("x", "y", "c"))
    outs = [total, dx[None]]
    for group in (grads, delta, new_m, new_v):
        outs.extend(group[n] for n in WEIGHT_ORDER)
    return tuple(outs)
```

```python
import functools

import jax
import jax.numpy as jnp
from jax import lax
from jax.experimental import pallas as pl
from jax.experimental.pallas import tpu as pltpu

F32, BF16 = jnp.float32, jnp.bfloat16
NORM_EPS = 1e-6
D_MODEL = 1024
D_FF = 2816
N_CHIPS = 4
FF_CHUNK = 2 * D_FF // N_CHIPS
DN_HEADS, DN_DIM, DN_CHUNK, DN_CONV = 4, 128, 64, 4
DN_BLOCK = 2 * DN_CHUNK
DN_PREP_CHUNKS = 8
DN_SCAN_CHUNKS = 8
SG_GROUPS, SG_DIM, SG_CHUNK = 4, 128, 128
POOL_WINDOWS = (2, 4, 8, 16)
POOL_DIM = 256
POOL_HALO = 16
CONV_HALO = 8
PROJ_W = 3200
BA_BLOCK = 3072 // 128
ADAM_LR, ADAM_B1, ADAM_B2, ADAM_EPS, ADAM_WD, ADAM_STEP = 0.001, 0.9, 0.999, 1e-08, 0.01, 10
VMEM_BIG = 52 * 1024 * 1024
FFN_FWD_ROWS = 512
FFN_BWD_ROWS = 256
MESH = pl.DeviceIdType.MESH
HI = lax.Precision.HIGHEST
ANY = pl.BlockSpec(memory_space=pl.ANY)


def _params(sem=None, vmem=None):
    return pltpu.CompilerParams(dimension_semantics=sem, vmem_limit_bytes=vmem)


def _dot(a, b):
    return jnp.dot(a, b, preferred_element_type=F32)


def _dot_nt(a, b):
    return lax.dot_general(a, b, (((1,), (1,)), ((), ())), preferred_element_type=F32)


def _dot_tn(a, b):
    return lax.dot_general(a, b, (((0,), (0,)), ((), ())), preferred_element_type=F32)


def _dot_hi(a, b):
    return jnp.dot(a, b, preferred_element_type=F32, precision=HI)


def _dot_mid(a, b):
    return jnp.dot(a, b, preferred_element_type=F32, precision=lax.Precision.HIGH)


def _bf(a):
    return a.astype(BF16)


def _rms_stats(x):
    r = lax.rsqrt(jnp.mean(x * x, axis=-1, keepdims=True) + NORM_EPS)
    return x * r, r


def _rms_bwd(dh, xhat, r, w):
    dhn = dh * w
    return r * (dhn - xhat * jnp.mean(dhn * xhat, axis=-1, keepdims=True))


def _sigmoid(x):
    return jax.nn.sigmoid(x)


def _silu_grad(x, s):
    return s * (1.0 + x * (1.0 - s))


def _gelu(x):
    return 0.5 * x * (1.0 + lax.erf(x * 0.7071067811865476))


def _gelu_grad(x):
    return 0.5 * (1.0 + lax.erf(x * 0.7071067811865476)) + x * jnp.exp(-0.5 * x * x) * 0.3989422804014327


def _softplus(x):
    return jnp.maximum(x, 0.0) + jnp.log(1.0 + jnp.exp(-jnp.abs(x)))


def _tile(n, pref):
    t = min(n, pref)
    assert n % t == 0, (n, t)
    return t


def _ffn_weight_specs():
    once = pl.Buffered(1)
    return [pl.BlockSpec((N_CHIPS, D_MODEL, FF_CHUNK), lambda i: (0, 0, 0), pipeline_mode=once),
            pl.BlockSpec((N_CHIPS, D_FF // N_CHIPS, D_MODEL), lambda i: (0, 0, 0), pipeline_mode=once)]


def _ffn_fwd(x, nw, win, wout, layer, name, exchange=None):
    T, D = x.shape
    tm = _tile(T, FFN_FWD_ROWS)
    nj = N_CHIPS // 2

    def body(x_ref, n_ref, win_ref, wo_ref, xo_ref, g_ref, u_ref, hb_ref):
        xv = x_ref[...]
        xhat, _ = _rms_stats(xv)
        h = _bf(xhat * n_ref[...])
        hb_ref[...] = h
        acc = None
        for j in range(nj):
            cols = slice(j * FF_CHUNK, (j + 1) * FF_CHUNK)
            g = _dot(h, win_ref[j])
            u = _dot(h, win_ref[nj + j])
            g_ref[:, cols] = _bf(g)
            u_ref[:, cols] = _bf(u)
            part = _dot(_bf(g * _sigmoid(g) * u), wo_ref[2 * j:2 * j + 2].reshape(FF_CHUNK, D))
            acc = part if acc is None else acc + part
        xo_ref[...] = xv + 0.5 * acc

    row = pl.BlockSpec((tm, D), lambda i: (i, 0))
    wide = pl.BlockSpec((tm, D_FF), lambda i: (i, 0))
    return _call_with_exchange(
        body, exchange, name=name, steps=T // tm, vmem=VMEM_BIG,
        in_specs=[row, pl.BlockSpec((None, 1, D), lambda i: (layer, 0, 0))] + _ffn_weight_specs(),
        out_specs=[row, wide, wide, row],
        out_shape=[jax.ShapeDtypeStruct((T, D), F32), jax.ShapeDtypeStruct((T, D_FF), BF16),
                   jax.ShapeDtypeStruct((T, D_FF), BF16), jax.ShapeDtypeStruct((T, D), BF16)],
        args=(x, nw, win, wout))


def _ffn_bwd(dxo, x, nw, g, u, win, wout, layer, name, exchange=None):
    T, D = x.shape
    tm = _tile(T, FFN_BWD_ROWS)
    nj = N_CHIPS // 2

    def body(dxo_ref, x_ref, n_ref, g_ref, u_ref, win_ref, wo_ref, dx_ref, dgu_ref, a_ref, dyb_ref, dn_ref):
        @pl.when(pl.program_id(0) == 0)
        def _():
            dn_ref[...] = jnp.zeros_like(dn_ref)

        dxo = dxo_ref[...]
        dyb = _bf(0.5 * dxo)
        dyb_ref[...] = dyb
        dh = None
        for j in range(nj):
            cols = slice(j * FF_CHUNK, (j + 1) * FF_CHUNK)
            da = _dot_nt(dyb, wo_ref[2 * j:2 * j + 2].reshape(FF_CHUNK, D))
            gv = g_ref[:, cols].astype(F32)
            uv = u_ref[:, cols].astype(F32)
            sg = _sigmoid(gv)
            sl = gv * sg
            dgb = _bf(da * uv * _silu_grad(gv, sg))
            dub = _bf(da * sl)
            a_ref[:, cols] = _bf(sl * uv)
            dgu_ref[:, cols] = dgb
            dgu_ref[:, D_FF + j * FF_CHUNK:D_FF + (j + 1) * FF_CHUNK] = dub
            part = _dot_nt(dgb, win_ref[j]) + _dot_nt(dub, win_ref[nj + j])
            dh = part if dh is None else dh + part
        xhat, r = _rms_stats(x_ref[...])
        dx_ref[...] = dxo + _rms_bwd(dh, xhat, r, n_ref[...])
        dn_ref[...] += jnp.sum(dh * xhat, axis=0, keepdims=True)

    row = pl.BlockSpec((tm, D), lambda i: (i, 0))
    wide = pl.BlockSpec((tm, D_FF), lambda i: (i, 0))
    return _call_with_exchange(
        body, exchange, name=name, steps=T // tm, vmem=VMEM_BIG,
        in_specs=[row, row, pl.BlockSpec((None, 1, D), lambda i: (layer, 0, 0)), wide, wide] + _ffn_weight_specs(),
        out_specs=[row, pl.BlockSpec((tm, 2 * D_FF), lambda i: (i, 0)), wide, row, pl.BlockSpec((1, D), lambda i: (0, 0))],
        out_shape=[jax.ShapeDtypeStruct((T, D), F32), jax.ShapeDtypeStruct((T, 2 * D_FF), BF16),
                   jax.ShapeDtypeStruct((T, D_FF), BF16), jax.ShapeDtypeStruct((T, D), BF16),
                   jax.ShapeDtypeStruct((1, D), F32)],
        args=(dxo, x, nw, g, u, win, wout))


def _matmul_tn(a, b, bm, bn, name, stack_n=False):
    T, M = a.shape
    N = b.shape[1]
    tk = _tile(T, 1024)
    bm, bn = _tile(M, bm), _tile(N, bn)

    def body(a_ref, b_ref, o_ref):
        @pl.when(pl.program_id(2) == 0)
        def _():
            o_ref[...] = jnp.zeros_like(o_ref)

        o_ref[...] += _dot_tn(_bf(a_ref[...]), _bf(b_ref[...]))

    if stack_n:
        out_spec = pl.BlockSpec((None, bm, bn), lambda m, n, k: (n, m, 0))
        out_shape = jax.ShapeDtypeStruct((N // bn, M, bn), F32)
    else:
        out_spec = pl.BlockSpec((bm, bn), lambda m, n, k: (m, n))
        out_shape = jax.ShapeDtypeStruct((M, N), F32)
    return pl.pallas_call(
        body, name=name, grid=(M // bm, N // bn, T // tk),
        in_specs=[pl.BlockSpec((tk, bm), lambda m, n, k: (k, m)),
                  pl.BlockSpec((tk, bn), lambda m, n, k: (k, n))],
        out_specs=out_spec, out_shape=out_shape,
        compiler_params=_params(("parallel", "parallel", "arbitrary"), VMEM_BIG),
    )(a, b)


def _matmul(a, b, name, trans_b=False, res=None, out_dtype=F32):
    T, K = a.shape
    N = b.shape[0] if trans_b else b.shape[1]
    tm = _tile(T, 512)

    def body(*refs):
        a_ref, b_ref = refs[0], refs[1]
        o_ref = refs[-1]
        av, bv = _bf(a_ref[...]), _bf(b_ref[...])
        acc = _dot_nt(av, bv) if trans_b else _dot(av, bv)
        if res is not None:
            acc = acc + refs[2][...]
        o_ref[...] = acc.astype(out_dtype)

    in_specs = [pl.BlockSpec((tm, K), lambda i: (i, 0)), pl.BlockSpec(b.shape, lambda i: (0, 0))]
    args = [a, b]
    if res is not None:
        in_specs.append(pl.BlockSpec((tm, N), lambda i: (i, 0)))
        args.append(res)
    return pl.pallas_call(
        body, name=name, grid=(T // tm,), in_specs=in_specs,
        out_specs=pl.BlockSpec((tm, N), lambda i: (i, 0)),
        out_shape=jax.ShapeDtypeStruct((T, N), out_dtype),
        compiler_params=_params(("parallel",), VMEM_BIG),
    )(*args)


def _norm_matmul(x, nw, layer, b, name):
    T, D = x.shape
    N = b.shape[1]
    tm = _tile(T, 512)

    def body(x_ref, n_ref, b_ref, hb_ref, o_ref):
        xhat, _ = _rms_stats(x_ref[...])
        h = _bf(xhat * n_ref[...])
        hb_ref[...] = h
        o_ref[...] = _dot(h, b_ref[...])

    return pl.pallas_call(
        body, name=name, grid=(T // tm,),
        in_specs=[pl.BlockSpec((tm, D), lambda i: (i, 0)), pl.BlockSpec((None, 1, D), lambda i: (layer, 0, 0)),
                  pl.BlockSpec(b.shape, lambda i: (0, 0))],
        out_specs=[pl.BlockSpec((tm, D), lambda i: (i, 0)), pl.BlockSpec((tm, N), lambda i: (i, 0))],
        out_shape=[jax.ShapeDtypeStruct((T, D), BF16), jax.ShapeDtypeStruct((T, N), F32)],
        compiler_params=_params(("parallel",), VMEM_BIG),
    )(x, nw, b)


def _norm_matmul_bwd(pieces, b, x, nw, dres, layer, name):
    T, D = x.shape
    tm = _tile(T, 512)
    widths = [p.shape[1] for p in pieces]
    offsets = [sum(widths[:k]) for k in range(len(pieces))]
    assert offsets[-1] + widths[-1] == b.shape[1] and all(o % 128 == 0 for o in offsets) and b.shape[0] == D
    n = len(pieces)

    def body(*refs):
        b_ref, x_ref, n_ref, dr_ref, dx_ref, dn_ref = refs[n:]

        @pl.when(pl.program_id(0) == 0)
        def _():
            dn_ref[...] = jnp.zeros_like(dn_ref)

        dh = None
        for a_ref, off, width in zip(refs[:n], offsets, widths):
            part = _dot_nt(_bf(a_ref[...]), b_ref[:, off:off + width])
            dh = part if dh is None else dh + part
        xhat, r = _rms_stats(x_ref[...])
        dx_ref[...] = dr_ref[...] + _rms_bwd(dh, xhat, r, n_ref[...])
        dn_ref[...] += jnp.sum(dh * xhat, axis=0, keepdims=True)

    row = pl.BlockSpec((tm, D), lambda i: (i, 0))
    return pl.pallas_call(
        body, name=name, grid=(T // tm,),
        in_specs=[pl.BlockSpec((tm, wd), lambda i: (i, 0)) for wd in widths]
        + [pl.BlockSpec(b.shape, lambda i: (0, 0)), row, pl.BlockSpec((None, 1, D), lambda i: (layer, 0, 0)), row],
        out_specs=[row, pl.BlockSpec((1, D), lambda i: (0, 0))],
        out_shape=[jax.ShapeDtypeStruct((T, D), F32), jax.ShapeDtypeStruct((1, D), F32)],
        compiler_params=_params(("arbitrary",), VMEM_BIG),
    )(*pieces, b, x, nw, dres)


def _shift_rows(x, s):
    n = x.shape[0]
    s = s % n
    return x if s == 0 else pltpu.roll(x, s, 0)


def _conv_fwd(proj, conv_w, name):
    T = proj.shape[0]
    C = 3 * DN_HEADS * DN_DIM
    cb = 512
    tm = _tile(T, 512)
    hb = tm // CONV_HALO

    def body(x_ref, xp_ref, w_ref, o_ref):
        i = pl.program_id(1)
        prev = jnp.where(i == 0, 0.0, xp_ref[...])
        ext = jnp.concatenate([prev, x_ref[...]], axis=0)
        w = w_ref[...]
        y = ext * w[DN_CONV - 1:DN_CONV, :]
        for k in range(DN_CONV - 1):
            y = y + _shift_rows(ext, DN_CONV - 1 - k) * w[k:k + 1, :]
        y = y[CONV_HALO:, :]
        o_ref[...] = y * _sigmoid(y)

    return pl.pallas_call(
        body, name=name, grid=(C // cb, T // tm),
        in_specs=[pl.BlockSpec((tm, cb), lambda c, i: (i, c)),
                  pl.BlockSpec((CONV_HALO, cb), lambda c, i: (jnp.maximum(i * hb - 1, 0), c)),
                  pl.BlockSpec((DN_CONV, cb), lambda c, i: (0, c))],
        out_specs=pl.BlockSpec((tm, cb), lambda c, i: (i, c)),
        out_shape=jax.ShapeDtypeStruct((T, C), F32),
        compiler_params=_params(("parallel", "parallel")),
    )(proj, proj, conv_w)


def _conv_bwd(dy, proj, conv_w, name):
    T = proj.shape[0]
    C = 3 * DN_HEADS * DN_DIM
    cb = 512
    tm = _tile(T, 512)
    hb = tm // CONV_HALO
    nt = T // tm

    def body(x_ref, xp_ref, xn_ref, dy_ref, dyn_ref, w_ref, dx_ref, dw_ref):
        i = pl.program_id(1)

        @pl.when(i == 0)
        def _():
            dw_ref[...] = jnp.zeros_like(dw_ref)

        prev = jnp.where(i == 0, 0.0, xp_ref[...])
        ext = jnp.concatenate([prev, x_ref[...], xn_ref[...]], axis=0)
        dy_ext = jnp.concatenate([jnp.zeros((CONV_HALO, cb), F32), dy_ref[...],
                                  jnp.where(i == nt - 1, 0.0, dyn_ref[...])], axis=0)
        w = w_ref[...]
        shifted = [_shift_rows(ext, DN_CONV - 1 - k) for k in range(DN_CONV)]
        y = shifted[0] * w[0:1, :]
        for k in range(1, DN_CONV):
            y = y + shifted[k] * w[k:k + 1, :]
        s = _sigmoid(y)
        dpre = dy_ext * _silu_grad(y, s)
        dx = dpre * w[DN_CONV - 1:DN_CONV, :]
        for k in range(DN_CONV - 1):
            dx = dx + _shift_rows(dpre, -(DN_CONV - 1 - k)) * w[k:k + 1, :]
        dx_ref[...] = _bf(dx[CONV_HALO:CONV_HALO + tm, :])
        rows = [jnp.sum((dpre * shifted[k])[CONV_HALO:CONV_HALO + tm, :], axis=0, keepdims=True) for k in range(DN_CONV)]
        dw_ref[...] += jnp.concatenate(rows, axis=0)

    last_halo = T // CONV_HALO - 1
    return pl.pallas_call(
        body, name=name, grid=(C // cb, nt),
        in_specs=[pl.BlockSpec((tm, cb), lambda c, i: (i, c)),
                  pl.BlockSpec((CONV_HALO, cb), lambda c, i: (jnp.maximum(i * hb - 1, 0), c)),
                  pl.BlockSpec((CONV_HALO, cb), lambda c, i: (jnp.minimum((i + 1) * hb, last_halo), c)),
                  pl.BlockSpec((tm, cb), lambda c, i: (i, c)),
                  pl.BlockSpec((CONV_HALO, cb), lambda c, i: (jnp.minimum((i + 1) * hb, last_halo), c)),
                  pl.BlockSpec((DN_CONV, cb), lambda c, i: (0, c))],
        out_specs=[pl.BlockSpec((tm, cb), lambda c, i: (i, c)),
                   pl.BlockSpec((DN_CONV, cb), lambda c, i: (0, c))],
        out_shape=[jax.ShapeDtypeStruct((T, C), BF16), jax.ShapeDtypeStruct((DN_CONV, C), F32)],
        compiler_params=_params(("parallel", "arbitrary")),
    )(proj, proj, proj, dy, dy, conv_w)


def _unit_lower_inverses(lows, eye):
    def each(fn, *lists):
        return [fn(*args) for args in zip(*lists)]

    p1 = [-low for low in lows]
    p2 = each(_dot_mid, p1, p1)
    p4 = each(_dot_mid, p2, p2)
    a = each(lambda x, y: eye + x + y + _dot_mid(x, y), p1, p2)
    p8 = each(_dot_mid, p4, p4)
    p16 = each(_dot_mid, p8, p8)
    b = each(lambda x, y: eye + x + y + _dot_mid(x, y), p4, p8)
    p32 = each(_dot_mid, p16, p16)
    ab = each(_dot_mid, a, b)
    c = each(lambda x, y: eye + x + y + _dot_mid(x, y), p16, p32)
    return each(_dot_mid, ab, c)


def _interleave(chains):
    results = [None] * len(chains)
    live = list(range(len(chains)))
    while live:
        for i in list(live):
            try:
                next(chains[i])
            except StopIteration as stop:
                results[i] = stop.value
                live.remove(i)
        yield
    return results


def _run_interleaved(chains):
    rounds = _interleave(chains)
    while True:
        try:
            next(rounds)
        except StopIteration as stop:
            return stop.value


def _l2_unit(x):
    r = lax.rsqrt(jnp.sum(x * x, axis=-1, keepdims=True) + NORM_EPS)
    return x * r, r


class _BlockMasks:
    def __init__(self):
        n = DN_BLOCK
        row = lax.broadcasted_iota(jnp.int32, (n, n), 0)
        col = lax.broadcasted_iota(jnp.int32, (n, n), 1)
        same = (row // DN_CHUNK) == (col // DN_CHUNK)
        self.lower, self.strict_lower = same & (row >= col), same & (row > col)
        self.upper, self.strict_upper = same & (row <= col), same & (row < col)
        self.eye = (row == col).astype(F32)
        self.first = lax.broadcasted_iota(jnp.int32, (n, 1), 0) < DN_CHUNK


def _dn_gates(ba, hp):
    coef = -jnp.exp(hp[0:1, :])
    pre = ba + hp[1:2, :]
    return _sigmoid(ba), coef * _softplus(pre), coef, pre


def _dn_block_gates(mk, ba, hp):
    assert DN_BLOCK == 2 * DN_CHUNK
    beta_t, graw_t, coef, pre = _dn_gates(ba, hp)
    gcum_t = _dot_hi(mk.lower.astype(F32), graw_t)
    gl_t = jnp.where(mk.first, gcum_t[DN_CHUNK - 1:DN_CHUNK, :], gcum_t[DN_BLOCK - 1:DN_BLOCK, :])
    return beta_t, gcum_t, gl_t, graw_t, coef, pre


def _dn_local(mk, qraw, kraw, bc, gc, gl):
    f = {}
    f["qn"], f["rq"] = _l2_unit(qraw)
    qh = f["qn"] * (DN_DIM ** -0.5)
    kh, f["rk"] = _l2_unit(kraw)
    gr = jnp.broadcast_to(gc, (DN_BLOCK, DN_BLOCK)).T
    dec = jnp.where(mk.lower, jnp.exp(jnp.where(mk.lower, gc - gr, 0.0)), 0.0)
    kb = kh * bc
    mkk = _dot_nt(_bf(kb), _bf(kh))
    eg = jnp.exp(gc)
    mqk = _dot_nt(_bf(qh), _bf(kh))
    etl = jnp.exp(gl - gc)
    f.update(qh=qh, kh=kh, gr=gr, dec=dec, kb=kb, mkk=mkk, eg=eg, mqk=mqk, attn=mqk * dec, etl=etl, qd=qh * eg, kt=kh * etl)
    return f


def _dn_specs(rows, rev=None):
    at = (lambda n: n) if rev is None else rev
    hw = DN_HEADS * DN_DIM
    return dict(
        qkv=[pl.BlockSpec((rows, hw), lambda n, j=j: (at(n), j)) for j in range(3)],
        ba=pl.BlockSpec((rows, 128), lambda n: (at(n), BA_BLOCK)),
        hp=pl.BlockSpec((8, 128), lambda n: (0, 0)),
        tok=pl.BlockSpec((rows, hw), lambda n: (at(n), 0)),
        attn=pl.BlockSpec((rows, DN_HEADS * DN_CHUNK), lambda n: (at(n), 0)),
        gate=pl.BlockSpec((rows // DN_CHUNK, 8, 128), lambda n: (at(n), 0, 0)),
        state=pl.BlockSpec((rows // DN_CHUNK, DN_HEADS, DN_DIM, DN_DIM), lambda n: (at(n), 0, 0, 0)),
    )


def _call_with_exchange(body, exchange, *, name, steps, in_specs, out_specs, out_shape, args, vmem=None, scratch_shapes=()):
    if exchange is None:
        res = pl.pallas_call(body, name=name, grid=(steps,), in_specs=in_specs, out_specs=out_specs, out_shape=out_shape,
                             scratch_shapes=list(scratch_shapes), compiler_params=_params(("arbitrary",), vmem))(*args)
        return list(res), None
    n_in, n_out, m, n_scr = len(in_specs), len(out_specs), len(exchange.operands), len(scratch_shapes)

    def hosted(*refs):
        ins, ex_ins = refs[:n_in], refs[n_in:n_in + m]
        outs, ex_outs = refs[n_in + m:n_in + m + n_out], refs[n_in + m + n_out:n_in + 2 * m + n_out]
        scratch, sems = refs[n_in + 2 * m + n_out:n_in + 2 * m + n_out + n_scr], refs[n_in + 2 * m + n_out + n_scr:]

        @pl.when(pl.program_id(0) == 0)
        def _():
            exchange.start(ex_ins, ex_outs, sems)

        body(*ins, *outs, *scratch)

        @pl.when(pl.program_id(0) == steps - 1)
        def _():
            exchange.finish(ex_ins, ex_outs, sems)

    res = pl.pallas_call(
        hosted, name=name, grid=(steps,), in_specs=list(in_specs) + [ANY] * m, out_specs=list(out_specs) + [ANY] * m,
        out_shape=list(out_shape) + list(exchange.out_shape), scratch_shapes=list(scratch_shapes) + list(exchange.scratch),
        compiler_params=pltpu.CompilerParams(dimension_semantics=("arbitrary",), vmem_limit_bytes=vmem, has_side_effects=True),
    )(*args, *exchange.operands)
    return list(res[:n_out]), exchange.finalize(list(res[n_out:]))


def _dn_prep(qkv, proj, hp, name, exchange=None):
    T = qkv.shape[0]
    n_chunks = T // DN_CHUNK
    blocks = max(1, min(DN_PREP_CHUNKS, n_chunks) * DN_CHUNK // DN_BLOCK)
    group = blocks * DN_BLOCK // DN_CHUNK
    rows = blocks * DN_BLOCK
    hw = DN_HEADS * DN_DIM

    def body(q_ref, k_ref, v_ref, ba_ref, hp_ref, u_ref, w_ref, p_ref, qd_ref, kt_ref, gl_ref, inv_ref):
        mk = _BlockMasks()
        hp_v = hp_ref[...]
        chains = []
        for j in range(blocks):
            rs = slice(j * DN_BLOCK, (j + 1) * DN_BLOCK)
            beta_t, gcum_t, gl_t = _dn_block_gates(mk, ba_ref[rs, :], hp_v)[:3]
            for c in range(DN_BLOCK // DN_CHUNK):
                gl_ref[j * (DN_BLOCK // DN_CHUNK) + c] = jnp.broadcast_to(gl_t[c * DN_CHUNK:c * DN_CHUNK + 1, :], (8, 128))
            for h in range(DN_HEADS):
                sl = slice(h * DN_DIM, (h + 1) * DN_DIM)
                gate = slice(DN_HEADS + h, DN_HEADS + h + 1)
                bc = beta_t[:, h:h + 1]
                f = _dn_local(mk, q_ref[rs, sl], k_ref[rs, sl], bc, gcum_t[:, gate], gl_t[:, gate])
                for c in range(DN_BLOCK // DN_CHUNK):
                    cr = slice(c * DN_CHUNK, (c + 1) * DN_CHUNK)
                    p_ref[j * DN_BLOCK + c * DN_CHUNK:j * DN_BLOCK + (c + 1) * DN_CHUNK, h * DN_CHUNK:(h + 1) * DN_CHUNK] = _bf(f["attn"][cr, cr])
                qd_ref[rs, sl] = _bf(f["qd"])
                kt_ref[rs, sl] = _bf(f["kt"])
                chains.append((rs, sl, bc, f))
        invs = _unit_lower_inverses([jnp.where(mk.strict_lower, f["mkk"] * f["dec"], 0.0) for _, _, _, f in chains], mk.eye)
        for (rs, sl, bc, f), inv in zip(chains, invs):
            inv_ref[rs, sl] = inv
            sol = _dot_mid(inv, jnp.concatenate([v_ref[rs, sl] * bc, f["kb"] * f["eg"]], axis=1))
            u_ref[rs, sl] = sol[:, :DN_DIM]
            w_ref[rs, sl] = _bf(sol[:, DN_DIM:])

    sp = _dn_specs(rows)
    tok16 = jax.ShapeDtypeStruct((T, hw), BF16)
    return _call_with_exchange(
        body, exchange, name=name, steps=n_chunks // group,
        in_specs=sp["qkv"] + [sp["ba"], sp["hp"]],
        out_specs=[sp["tok"], sp["tok"], sp["attn"], sp["tok"], sp["tok"], sp["gate"], sp["tok"]],
        out_shape=[jax.ShapeDtypeStruct((T, hw), F32), tok16, jax.ShapeDtypeStruct((T, DN_HEADS * DN_CHUNK), BF16),
                   tok16, tok16, jax.ShapeDtypeStruct((n_chunks, 8, 128), F32), jax.ShapeDtypeStruct((T, hw), F32)],
        args=(qkv, qkv, qkv, proj, hp))


def _dn_scan(u, w, p, qd, kt, gl, name, exchange=None):
    T = u.shape[0]
    n_chunks = T // DN_CHUNK
    group = min(DN_SCAN_CHUNKS, n_chunks)
    rows = group * DN_CHUNK
    hw = DN_HEADS * DN_DIM

    def body(u_ref, w_ref, p_ref, qd_ref, kt_ref, gl_ref, o_ref, vn_ref, sall_ref, s_s):
        @pl.when(pl.program_id(0) == 0)
        def _():
            s_s[...] = jnp.zeros_like(s_s)

        state = [s_s[h] for h in range(DN_HEADS)]
        for j in range(group):
            rs = slice(j * DN_CHUNK, (j + 1) * DN_CHUNK)
            for h in range(DN_HEADS):
                sl = slice(h * DN_DIM, (h + 1) * DN_DIM)
                sall_ref[j, h] = state[h]
                sb = _bf(state[h])
                vnb = _bf(u_ref[rs, sl] - _dot(w_ref[rs, sl], sb))
                vn_ref[rs, sl] = vnb
                o_ref[rs, sl] = _dot(qd_ref[rs, sl], sb) + _dot(p_ref[rs, h * DN_CHUNK:(h + 1) * DN_CHUNK], vnb)
                egl = jnp.exp(gl_ref[j, 0:1, DN_HEADS + h:DN_HEADS + h + 1])
                state[h] = state[h] * egl + _dot_tn(kt_ref[rs, sl], vnb)
        for h in range(DN_HEADS):
            s_s[h] = state[h]

    sp = _dn_specs(rows)
    return _call_with_exchange(
        body, exchange, name=name, steps=n_chunks // group,
        in_specs=[sp["tok"], sp["tok"], sp["attn"], sp["tok"], sp["tok"], sp["gate"]],
        out_specs=[sp["tok"], sp["tok"], sp["state"]],
        out_shape=[jax.ShapeDtypeStruct((T, hw), F32), jax.ShapeDtypeStruct((T, hw), BF16),
                   jax.ShapeDtypeStruct((n_chunks, DN_HEADS, DN_DIM, DN_DIM), F32)],
        scratch_shapes=[pltpu.VMEM((DN_HEADS, DN_DIM, DN_DIM), F32)],
        args=(u, w, p, qd, kt, gl))


def _dn_scan_bwd(w, p, qd, kt, gl, vn, sall, do, name, exchange=None):
    T = w.shape[0]
    n_chunks = T // DN_CHUNK
    group = min(DN_SCAN_CHUNKS, n_chunks)
    rows = group * DN_CHUNK
    hw = DN_HEADS * DN_DIM
    last = n_chunks // group - 1

    def body(w_ref, p_ref, qd_ref, kt_ref, gl_ref, vn_ref, sall_ref, do_ref, dvn_ref, dkt_ref, dgl_ref, ds_s):
        @pl.when(pl.program_id(0) == 0)
        def _():
            ds_s[...] = jnp.zeros_like(ds_s)

        lane = lax.broadcasted_iota(jnp.int32, (8, 128), 1)
        d_state = [ds_s[h] for h in range(DN_HEADS)]
        for j in reversed(range(group)):
            rs = slice(j * DN_CHUNK, (j + 1) * DN_CHUNK)
            dgl_tile = jnp.zeros((8, 128), F32)
            for h in range(DN_HEADS):
                sl = slice(h * DN_DIM, (h + 1) * DN_DIM)
                d_out = _bf(do_ref[rs, sl])
                d_new = d_state[h]
                d_newb = _bf(d_new)
                d_vn = _dot_tn(p_ref[rs, h * DN_CHUNK:(h + 1) * DN_CHUNK], d_out) + _dot(kt_ref[rs, sl], d_newb)
                dvn_ref[rs, sl] = d_vn
                dkt_ref[rs, sl] = _dot_nt(vn_ref[rs, sl], d_newb)
                egl = jnp.exp(gl_ref[j, 0:1, DN_HEADS + h:DN_HEADS + h + 1])
                prod = jnp.sum(d_new * sall_ref[j, h], axis=1, keepdims=True)
                dgl_tile = jnp.where(lane == DN_HEADS + h, jnp.sum(prod, axis=0, keepdims=True) * egl, dgl_tile)
                d_state[h] = d_new * egl + _dot_tn(qd_ref[rs, sl], d_out) - _dot_tn(w_ref[rs, sl], _bf(d_vn))
            dgl_ref[j] = dgl_tile
        for h in range(DN_HEADS):
            ds_s[h] = d_state[h]

    sp = _dn_specs(rows, rev=lambda n: last - n)
    return _call_with_exchange(
        body, exchange, name=name, steps=n_chunks // group,
        in_specs=[sp["tok"], sp["attn"], sp["tok"], sp["tok"], sp["gate"], sp["tok"], sp["state"], sp["tok"]],
        out_specs=[sp["tok"], sp["tok"], sp["gate"]],
        out_shape=[jax.ShapeDtypeStruct((T, hw), F32), jax.ShapeDtypeStruct((T, hw), F32),
                   jax.ShapeDtypeStruct((n_chunks, 8, 128), F32)],
        scratch_shapes=[pltpu.VMEM((DN_HEADS, DN_DIM, DN_DIM), F32)],
        args=(w, p, qd, kt, gl, vn, sall, do))


def _dn_prep_bwd(qkv, proj, hp, sall, vn, do, dvn, dkt, dgl, inv, u, w, name, exchange=None):
    T = qkv.shape[0]
    n_chunks = T // DN_CHUNK
    blocks = max(1, min(DN_PREP_CHUNKS, n_chunks) * DN_CHUNK // DN_BLOCK)
    per_block = DN_BLOCK // DN_CHUNK
    group = blocks * per_block
    rows = blocks * DN_BLOCK
    hw = DN_HEADS * DN_DIM
    first_rows, second_rows = slice(0, DN_CHUNK), slice(DN_CHUNK, DN_BLOCK)

    def rowsum(x):
        return jnp.sum(x, axis=1, keepdims=True)

    def by_chunk(x, s0, s1, fn):
        return jnp.concatenate([fn(x[first_rows], s0), fn(x[second_rows], s1)], axis=0)

    def body(q_ref, k_ref, v_ref, ba_ref, hp_ref, sall_ref, vn_ref, do_ref, dvn_ref, dkt_ref, dgl_ref,
             inv_ref, u_ref, w_ref, dqkv_ref, dba_ref, dhp_ref):
        @pl.when(pl.program_id(0) == 0)
        def _():
            dhp_ref[...] = jnp.zeros_like(dhp_ref)

        mk = _BlockMasks()
        hp_v = hp_ref[...]
        chains = []
        for j in range(blocks):
            rs = slice(j * DN_BLOCK, (j + 1) * DN_BLOCK)
            chains.append(one_block(mk, hp_v, *(r.at[rs, :] for r in (q_ref, k_ref, v_ref, ba_ref)),
                                    sall_ref.at[pl.ds(j * per_block, per_block)],
                                    *(r.at[rs, :] for r in (vn_ref, do_ref, dvn_ref, dkt_ref)),
                                    dgl_ref.at[pl.ds(j * per_block, per_block)],
                                    *(r.at[rs, :] for r in (inv_ref, u_ref, w_ref)),
                                    *(dqkv_ref.at[rs, pl.ds(i * hw, hw)] for i in range(3)), dba_ref.at[rs, :]))
        total = jnp.zeros((8, 128), F32)
        for part in _run_interleaved(chains):
            total = total + part
        dhp_ref[...] += total

    def one_block(mk, hp_v, q_ref, k_ref, v_ref, ba_ref, state_ref, vn_ref, do_ref, dvn_ref, dkt_ref, dgl_ref,
                  inv_ref, u_ref, w_ref, dq_ref, dk_ref, dv_ref, dba_ref):
        ba = ba_ref[...]
        beta_t, gcum_t, gl_t, graw_t, coef, pre = _dn_block_gates(mk, ba, hp_v)
        lane = lax.broadcasted_iota(jnp.int32, (DN_BLOCK, 128), 1)
        rowi = lax.broadcasted_iota(jnp.int32, (DN_BLOCK, 1), 0)

        def head(h):
            sl = slice(h * DN_DIM, (h + 1) * DN_DIM)
            gate = slice(DN_HEADS + h, DN_HEADS + h + 1)
            gc = gcum_t[:, gate]
            bc = beta_t[:, h:h + 1]
            sb0, sb1 = _bf(state_ref[0, h]), _bf(state_ref[1, h])
            vh = v_ref[:, sl]
            f = _dn_local(mk, q_ref[:, sl], k_ref[:, sl], bc, gc, gl_t[:, gate])
            yield
            qh, kh, kb, dec, eg, etl = f["qh"], f["kh"], f["kb"], f["dec"], f["eg"], f["etl"]
            qd, kt = f["qd"], f["kt"]
            qb, kbf, kbb = _bf(qh), _bf(kh), _bf(kb)
            dec_t = jnp.where(mk.upper, jnp.exp(jnp.where(mk.upper, f["gr"] - gc, 0.0)), 0.0)
            mkk_t = f["mkk"].T
            inv_t = inv_ref[:, sl].T
            mqk_t = f["mqk"].T

            d_out = _bf(do_ref[:, sl])
            vnb = vn_ref[:, sl]
            d_qd = by_chunk(d_out, sb0, sb1, _dot_nt)
            d_attn = _dot_nt(d_out, vnb)
            d_attn_t = _dot_nt(vnb, d_out)
            d_vn = dvn_ref[:, sl]
            d_kt = dkt_ref[:, sl]
            d_w = -by_chunk(_bf(d_vn), sb0, sb1, _dot_nt)
            yield
            d_rhs = _dot_mid(inv_t, jnp.concatenate([d_vn, d_w], axis=1))
            yield
            d_bu, d_bw = d_rhs[:, :DN_DIM], d_rhs[:, DN_DIM:]
            ub, wb, d_bub, d_bwb = _bf(u_ref[:, sl]), w_ref[:, sl], _bf(d_bu), _bf(d_bw)
            d_low = -(_dot_nt(d_bub, ub) + _dot_nt(d_bwb, wb))
            d_low_t = -(_dot_nt(ub, d_bub) + _dot_nt(wb, d_bwb))
            yield
            d_mkk = jnp.where(mk.strict_lower, d_low * dec, 0.0)
            d_mkk_t = jnp.where(mk.strict_upper, d_low_t * dec_t, 0.0)
            d_mqk = jnp.where(mk.lower, d_attn * dec, 0.0)
            d_mqk_t = jnp.where(mk.upper, d_attn_t * dec_t, 0.0)
            bw = kb * eg
            d_kb = _dot(_bf(d_mkk), kbf) + d_bw * eg
            d_k = _dot(_bf(d_mkk_t), kbb) + _dot(_bf(d_mqk_t), qb) + d_kt * etl + d_kb * bc
            d_q = _dot(_bf(d_mqk), kbf) + d_qd * eg
            yield
            d_beta = rowsum(d_kb * kh) + rowsum(d_bu * vh)
            dv_ref[:, sl] = d_bu * bc
            e_mat = d_mkk * f["mkk"] + d_mqk * f["mqk"]
            e_mat_t = d_mkk_t * mkk_t + d_mqk_t * mqk_t
            kt_term = rowsum(d_kt * kt)
            d_g = rowsum(e_mat) - rowsum(e_mat_t) + rowsum(d_qd * qd) + rowsum(d_bw * bw) - kt_term
            for c, chunk_rows in enumerate((mk.first, ~mk.first)):
                d_glast = dgl_ref[c, 0:1, gate] + jnp.sum(jnp.where(chunk_rows, kt_term, 0.0), axis=0, keepdims=True)
                d_g = d_g + jnp.where(rowi == (c + 1) * DN_CHUNK - 1, d_glast, 0.0)
            qn = f["qn"]
            d_qs = d_q * (DN_DIM ** -0.5)
            dq_ref[:, sl] = f["rq"] * (d_qs - qn * rowsum(d_qs * qn))
            dk_ref[:, sl] = f["rk"] * (d_k - kh * rowsum(d_k * kh))
            return d_g, d_beta

        per_head = yield from _interleave([head(h) for h in range(DN_HEADS)])
        dgcum_t = jnp.zeros((DN_BLOCK, 128), F32)
        dbeta_t = jnp.zeros((DN_BLOCK, 128), F32)
        for h, (d_g, d_beta) in enumerate(per_head):
            dgcum_t = jnp.where(lane == DN_HEADS + h, d_g, dgcum_t)
            dbeta_t = jnp.where(lane == h, d_beta, dbeta_t)
        dgraw_t = _dot_hi(mk.upper.astype(F32), dgcum_t)
        sp = _sigmoid(pre)
        d_pre = dgraw_t * coef * sp
        dba_ref[...] = jnp.where(lane < DN_HEADS, dbeta_t * beta_t * (1.0 - beta_t),
                                 jnp.where(lane < 2 * DN_HEADS, d_pre, 0.0))
        in_g = (lane >= DN_HEADS) & (lane < 2 * DN_HEADS)
        d_alog = jnp.sum(jnp.where(in_g, dgraw_t * graw_t, 0.0), axis=0, keepdims=True)
        d_dtb = jnp.sum(jnp.where(in_g, d_pre, 0.0), axis=0, keepdims=True)
        return jnp.concatenate([d_alog, d_dtb, jnp.zeros((6, 128), F32)], axis=0)

    sp = _dn_specs(rows)
    return _call_with_exchange(
        body, exchange, name=name, steps=n_chunks // group,
        in_specs=sp["qkv"] + [sp["ba"], sp["hp"], sp["state"]] + [sp["tok"]] * 4 + [sp["gate"]] + [sp["tok"]] * 3,
        out_specs=[pl.BlockSpec((rows, 3 * hw), lambda n: (n, 0)), pl.BlockSpec((rows, 128), lambda n: (n, 0)), sp["hp"]],
        out_shape=[jax.ShapeDtypeStruct((T, 3 * hw), F32), jax.ShapeDtypeStruct((T, 128), F32),
                   jax.ShapeDtypeStruct((8, 128), F32)],
        args=(qkv, qkv, qkv, proj, hp, sall, vn, do, dvn, dkt, dgl, inv, u, w))


def _mix_fwd(o, proj, dn_norm, sg_norm, sg_w, sg_bt, name):
    T = o.shape[0]
    tm = _tile(T, 512)
    hw = DN_HEADS * DN_DIM
    nc = tm // SG_CHUNK

    def body(o_ref, z_ref, su_ref, sv_ref, dnn_ref, sgn_ref, sgw_ref, sgb_ref, mix_ref):
        dnn = dnn_ref[...]
        for h in range(DN_HEADS):
            sl = slice(h * DN_DIM, (h + 1) * DN_DIM)
            xhat, _ = _rms_stats(o_ref[:, sl])
            z = z_ref[:, sl]
            mix_ref[:, sl] = _bf(xhat * dnn * (z * _sigmoid(z)))
        tri = lax.broadcasted_iota(jnp.int32, (SG_CHUNK, SG_CHUNK), 0) >= lax.broadcasted_iota(jnp.int32, (SG_CHUNK, SG_CHUNK), 1)
        for g in range(SG_GROUPS):
            sl = slice(g * SG_DIM, (g + 1) * SG_DIM)
            xhat, _ = _rms_stats(_gelu(sv_ref[:, sl]))
            svn = _bf(xhat * sgn_ref[g:g + 1, :])
            sua = _gelu(su_ref[:, sl])
            wt = _bf(jnp.where(tri, sgw_ref[g], 0.0))
            bias = sgb_ref[:, g:g + 1]
            for c in range(nc):
                rows = slice(c * SG_CHUNK, (c + 1) * SG_CHUNK)
                mixed = _dot(wt, svn[rows, :]) + bias
                mix_ref[rows, hw + g * SG_DIM:hw + (g + 1) * SG_DIM] = _bf(sua[rows, :] * mixed)

    full = lambda shape: pl.BlockSpec(shape, lambda i: (0,) * len(shape))
    return pl.pallas_call(
        body, name=name, grid=(T // tm,),
        in_specs=[pl.BlockSpec((tm, hw), lambda i: (i, 0)),
                  pl.BlockSpec((tm, hw), lambda i: (i, 3)),
                  pl.BlockSpec((tm, hw), lambda i: (i, 4)),
                  pl.BlockSpec((tm, hw), lambda i: (i, 5)),
                  full((1, DN_DIM)), full((SG_GROUPS, SG_DIM)), full((SG_GROUPS, SG_CHUNK, SG_CHUNK)),
                  full((SG_CHUNK, 128))],
        out_specs=pl.BlockSpec((tm, 2 * hw), lambda i: (i, 0)),
        out_shape=jax.ShapeDtypeStruct((T, 2 * hw), BF16),
        compiler_params=_params(("parallel",)),
    )(o, proj, proj, proj, dn_norm, sg_norm, sg_w, sg_bt)


def _mix_bwd(dmix, o, proj, dn_norm, sg_norm, sg_w, sg_bt, name):
    T = o.shape[0]
    tm = _tile(T, 512)
    hw = DN_HEADS * DN_DIM
    nc = tm // SG_CHUNK

    def body(dm_ref, o_ref, z_ref, su_ref, sv_ref, dnn_ref, sgn_ref, sgw_ref, sgb_ref,
             do_ref, dz_ref, ddnn_ref, dsgn_ref, dsgw_ref, dsgb_ref):
        @pl.when(pl.program_id(0) == 0)
        def _():
            ddnn_ref[...] = jnp.zeros_like(ddnn_ref)
            dsgn_ref[...] = jnp.zeros_like(dsgn_ref)
            dsgw_ref[...] = jnp.zeros_like(dsgw_ref)
            dsgb_ref[...] = jnp.zeros_like(dsgb_ref)

        dnn = dnn_ref[...]
        ddnn = jnp.zeros((1, DN_DIM), F32)
        for h in range(DN_HEADS):
            sl = slice(h * DN_DIM, (h + 1) * DN_DIM)
            xhat, r = _rms_stats(o_ref[:, sl])
            z = z_ref[:, sl]
            sz = _sigmoid(z)
            doa = dm_ref[:, sl]
            dyn = doa * (z * sz)
            dz_ref[:, sl] = _bf(doa * xhat * dnn * _silu_grad(z, sz))
            do_ref[:, sl] = _rms_bwd(dyn, xhat, r, dnn)
            ddnn = ddnn + jnp.sum(dyn * xhat, axis=0, keepdims=True)
        ddnn_ref[...] += ddnn
        tri = lax.broadcasted_iota(jnp.int32, (SG_CHUNK, SG_CHUNK), 0) >= lax.broadcasted_iota(jnp.int32, (SG_CHUNK, SG_CHUNK), 1)
        lane = lax.broadcasted_iota(jnp.int32, (SG_CHUNK, 128), 1)
        dsgb = jnp.zeros((SG_CHUNK, 128), F32)
        dsgn_rows = []
        for g in range(SG_GROUPS):
            sl = slice(g * SG_DIM, (g + 1) * SG_DIM)
            sv = sv_ref[:, sl]
            su = su_ref[:, sl]
            xhat, r = _rms_stats(_gelu(sv))
            sgn = sgn_ref[g:g + 1, :]
            svn = _bf(xhat * sgn)
            sua = _gelu(su)
            wt = _bf(jnp.where(tri, sgw_ref[g], 0.0))
            bias = sgb_ref[:, g:g + 1]
            dw = jnp.zeros((SG_CHUNK, SG_CHUNK), F32)
            db = jnp.zeros((SG_CHUNK, 1), F32)
            dsua, dsvn = [], []
            for c in range(nc):
                rows = slice(c * SG_CHUNK, (c + 1) * SG_CHUNK)
                mixed = _dot(wt, svn[rows, :]) + bias
                dob = dm_ref[rows, hw + g * SG_DIM:hw + (g + 1) * SG_DIM]
                dsua.append(dob * mixed)
                dmixed = dob * sua[rows, :]
                dmb = _bf(dmixed)
                dsvn.append(_dot_tn(wt, dmb))
                dw = dw + _dot_nt(dmb, svn[rows, :])
                db = db + jnp.sum(dmixed, axis=1, keepdims=True)
            dsua = jnp.concatenate(dsua, axis=0) if nc > 1 else dsua[0]
            dsvn = jnp.concatenate(dsvn, axis=0) if nc > 1 else dsvn[0]
            dz_ref[:, hw + g * SG_DIM:hw + (g + 1) * SG_DIM] = _bf(dsua * _gelu_grad(su))
            dz_ref[:, 2 * hw + g * SG_DIM:2 * hw + (g + 1) * SG_DIM] = _bf(_rms_bwd(dsvn, xhat, r, sgn) * _gelu_grad(sv))
            dsgn_rows.append(jnp.sum(dsvn * xhat, axis=0, keepdims=True))
            dsgw_ref[g] += jnp.where(tri, dw, 0.0)
            dsgb = jnp.where(lane == g, db, dsgb)
        dsgn_ref[...] += jnp.concatenate(dsgn_rows, axis=0)
        dsgb_ref[...] += dsgb

    full = lambda shape: pl.BlockSpec(shape, lambda i: (0,) * len(shape))
    return pl.pallas_call(
        body, name=name, grid=(T // tm,),
        in_specs=[pl.BlockSpec((tm, 2 * hw), lambda i: (i, 0)),
                  pl.BlockSpec((tm, hw), lambda i: (i, 0)),
                  pl.BlockSpec((tm, hw), lambda i: (i, 3)),
                  pl.BlockSpec((tm, hw), lambda i: (i, 4)),
                  pl.BlockSpec((tm, hw), lambda i: (i, 5)),
                  full((1, DN_DIM)), full((SG_GROUPS, SG_DIM)), full((SG_GROUPS, SG_CHUNK, SG_CHUNK)),
                  full((SG_CHUNK, 128))],
        out_specs=[pl.BlockSpec((tm, hw), lambda i: (i, 0)),
                   pl.BlockSpec((tm, 3 * hw), lambda i: (i, 0)),
                   full((1, DN_DIM)), full((SG_GROUPS, SG_DIM)), full((SG_GROUPS, SG_CHUNK, SG_CHUNK)),
                   full((SG_CHUNK, 128))],
        out_shape=[jax.ShapeDtypeStruct((T, hw), F32), jax.ShapeDtypeStruct((T, 3 * hw), BF16),
                   jax.ShapeDtypeStruct((1, DN_DIM), F32), jax.ShapeDtypeStruct((SG_GROUPS, SG_DIM), F32),
                   jax.ShapeDtypeStruct((SG_GROUPS, SG_CHUNK, SG_CHUNK), F32),
                   jax.ShapeDtypeStruct((SG_CHUNK, 128), F32)],
        compiler_params=_params(("arbitrary",)),
    )(dmix, o, proj, proj, proj, dn_norm, sg_norm, sg_w, sg_bt)


def _window_sums(h, sign):
    sums, s, w = {}, h, 1
    while w < POOL_WINDOWS[-1]:
        s = s + _shift_rows(s, sign * w)
        w *= 2
        sums[w] = s
    return sums


def _pool_counts(t_global):
    return [jnp.minimum(t_global + 1, win).astype(F32) for win in POOL_WINDOWS]


def _pooled_groups(ext_h, row0, tm):
    sums = _window_sums(ext_h, 1)
    t_global = row0 + lax.broadcasted_iota(jnp.int32, (tm, 1), 0)
    counts = _pool_counts(t_global)
    out = []
    for gi, win in enumerate(POOL_WINDOWS):
        cols = slice(gi * POOL_DIM, (gi + 1) * POOL_DIM)
        out.append(sums[win][POOL_HALO:, cols] / counts[gi] - ext_h[POOL_HALO:, cols])
    return out


def _pool_fwd(x, nw, pool_w, pool_scale, layer, name):
    T, D = x.shape
    tm = _tile(T, 256)
    hb = tm // POOL_HALO

    def body(x_ref, xp_ref, n_ref, w_ref, s_ref, xo_ref):
        i = pl.program_id(0)
        prev = jnp.where(i == 0, 0.0, xp_ref[...])
        ext = jnp.concatenate([prev, x_ref[...]], axis=0)
        xhat, _ = _rms_stats(ext)
        pooled = _pooled_groups(xhat * n_ref[...], i * tm, tm)
        for gi in range(len(POOL_WINDOWS)):
            cols = slice(gi * POOL_DIM, (gi + 1) * POOL_DIM)
            xo_ref[:, cols] = x_ref[:, cols] + _dot(_bf(pooled[gi]), w_ref[gi]) * s_ref[:, cols]

    return pl.pallas_call(
        body, name=name, grid=(T // tm,),
        in_specs=[pl.BlockSpec((tm, D), lambda i: (i, 0)),
                  pl.BlockSpec((POOL_HALO, D), lambda i: (jnp.maximum(i * hb - 1, 0), 0)),
                  pl.BlockSpec((None, 1, D), lambda i: (layer, 0, 0)),
                  pl.BlockSpec(pool_w.shape, lambda i: (0, 0, 0)),
                  pl.BlockSpec((1, D), lambda i: (0, 0))],
        out_specs=pl.BlockSpec((tm, D), lambda i: (i, 0)),
        out_shape=jax.ShapeDtypeStruct((T, D), F32),
        compiler_params=_params(("parallel",)),
    )(x, x, nw, pool_w, pool_scale)


def _pool_bwd(dxo, x, nw, pool_w, pool_scale, layer, name):
    T, D = x.shape
    tm = _tile(T, 256)
    hb = tm // POOL_HALO
    nt = T // tm
    ng = len(POOL_WINDOWS)

    def body(dxo_ref, dxn_ref, x_ref, xp_ref, n_ref, w_ref, s_ref, dx_ref, dw_ref, ds_ref, dn_ref):
        i = pl.program_id(0)

        @pl.when(i == 0)
        def _():
            dw_ref[...] = jnp.zeros_like(dw_ref)
            ds_ref[...] = jnp.zeros_like(ds_ref)
            dn_ref[...] = jnp.zeros_like(dn_ref)

        prev = jnp.where(i == 0, 0.0, xp_ref[...])
        ext = jnp.concatenate([prev, x_ref[...]], axis=0)
        xhat_ext, r_ext = _rms_stats(ext)
        nv = n_ref[...]
        pooled = _pooled_groups(xhat_ext * nv, i * tm, tm)
        dxo = dxo_ref[...]
        scale = s_ref[...]
        dout_ext = jnp.concatenate([dxo, jnp.where(i == nt - 1, 0.0, dxn_ref[...])], axis=0) * scale
        t_ext = i * tm + lax.broadcasted_iota(jnp.int32, (tm + POOL_HALO, 1), 0)
        counts = _pool_counts(t_ext)
        dh_cols, ds_cols = [], []
        for gi, win in enumerate(POOL_WINDOWS):
            cols = slice(gi * POOL_DIM, (gi + 1) * POOL_DIM)
            wg = w_ref[gi]
            pb = _bf(pooled[gi])
            doutb = _bf(dout_ext[:, cols])
            dpooled = _dot_nt(doutb, wg)
            ahead = _window_sums(dpooled / counts[gi], -1)[win]
            dh_cols.append(ahead[:tm, :] - dpooled[:tm, :])
            dw_ref[gi] += _dot_tn(pb, doutb[:tm, :])
            ds_cols.append(jnp.sum(dxo[:, cols] * _dot(pb, wg), axis=0, keepdims=True))
        dh = jnp.concatenate(dh_cols, axis=1)
        xhat, r = xhat_ext[POOL_HALO:, :], r_ext[POOL_HALO:, :]
        dx_ref[...] = dxo + _rms_bwd(dh, xhat, r, nv)
        dn_ref[...] += jnp.sum(dh * xhat, axis=0, keepdims=True)
        ds_ref[...] += jnp.concatenate(ds_cols, axis=1)

    last_halo = T // POOL_HALO - 1
    return pl.pallas_call(
        body, name=name, grid=(nt,),
        in_specs=[pl.BlockSpec((tm, D), lambda i: (i, 0)),
                  pl.BlockSpec((POOL_HALO, D), lambda i: (jnp.minimum((i + 1) * hb, last_halo), 0)),
                  pl.BlockSpec((tm, D), lambda i: (i, 0)),
                  pl.BlockSpec((POOL_HALO, D), lambda i: (jnp.maximum(i * hb - 1, 0), 0)),
                  pl.BlockSpec((None, 1, D), lambda i: (layer, 0, 0)),
                  pl.BlockSpec(pool_w.shape, lambda i: (0, 0, 0)),
                  pl.BlockSpec((1, D), lambda i: (0, 0))],
        out_specs=[pl.BlockSpec((tm, D), lambda i: (i, 0)),
                   pl.BlockSpec((ng, POOL_DIM, POOL_DIM), lambda i: (0, 0, 0)),
                   pl.BlockSpec((1, D), lambda i: (0, 0)),
                   pl.BlockSpec((1, D), lambda i: (0, 0))],
        out_shape=[jax.ShapeDtypeStruct((T, D), F32), jax.ShapeDtypeStruct((ng, POOL_DIM, POOL_DIM), F32),
                   jax.ShapeDtypeStruct((1, D), F32), jax.ShapeDtypeStruct((1, D), F32)],
        compiler_params=_params(("arbitrary",)),
    )(dxo, dxo, x, x, nw, pool_w, pool_scale)


def _loss_head(x, target, fn, name):
    T, D = x.shape
    tm = _tile(T, 512)

    def body(x_ref, t_ref, n_ref, loss_ref, dx_ref, dn_ref):
        @pl.when(pl.program_id(0) == 0)
        def _():
            loss_ref[...] = jnp.zeros_like(loss_ref)
            dn_ref[...] = jnp.zeros_like(dn_ref)

        xhat, r = _rms_stats(x_ref[...])
        nv = n_ref[...]
        err = xhat * nv - t_ref[...]
        part = jnp.sum(jnp.sum(err * err, axis=1, keepdims=True), axis=0, keepdims=True)
        loss_ref[...] += 0.5 * part / D
        dy = err / D
        dx_ref[...] = _rms_bwd(dy, xhat, r, nv)
        dn_ref[...] += jnp.sum(dy * xhat, axis=0, keepdims=True)

    row = pl.BlockSpec((tm, D), lambda i: (i, 0))
    return pl.pallas_call(
        body, name=name, grid=(T // tm,),
        in_specs=[row, row, pl.BlockSpec((1, D), lambda i: (0, 0))],
        out_specs=[pl.BlockSpec((1, 1), lambda i: (0, 0)), row, pl.BlockSpec((1, D), lambda i: (0, 0))],
        out_shape=[jax.ShapeDtypeStruct((1, 1), F32), jax.ShapeDtypeStruct((T, D), F32),
                   jax.ShapeDtypeStruct((1, D), F32)],
        compiler_params=_params(("arbitrary",)),
    )(x, target, fn)


def _adamw(w, g, m, v, name):
    R, C = w.shape
    br = R
    for cand in (512, 256, 128, 64, 32, 16, 8):
        if R % cand == 0 and cand * C * 4 <= 2 * 1024 * 1024:
            br = cand
            break

    def body(w_ref, g_ref, m_ref, v_ref, d_ref, mo_ref, vo_ref):
        gv = g_ref[...]
        m_new = ADAM_B1 * m_ref[...] + (1.0 - ADAM_B1) * gv
        v_new = ADAM_B2 * v_ref[...] + (1.0 - ADAM_B2) * (gv * gv)
        m_hat = m_new / (1.0 - ADAM_B1 ** ADAM_STEP)
        v_hat = v_new / (1.0 - ADAM_B2 ** ADAM_STEP)
        d_ref[...] = -ADAM_LR * (m_hat / (jnp.sqrt(v_hat) + ADAM_EPS) + ADAM_WD * w_ref[...])
        mo_ref[...] = m_new
        vo_ref[...] = v_new

    blk = pl.BlockSpec((br, C), lambda i: (i, 0))
    return pl.pallas_call(
        body, name=name, grid=(R // br,), in_specs=[blk] * 4, out_specs=[blk] * 3,
        out_shape=[jax.ShapeDtypeStruct((R, C), F32)] * 3,
        compiler_params=_params(("parallel",)),
    )(w, g, m, v)


def _mesh_pos():
    return lax.axis_index("x"), lax.axis_index("y"), lax.axis_index("c")


def _other_chips(x, y):
    return [(1 - x, y), (x, 1 - y), (1 - x, 1 - y)]


def _half_of(ref, shape, h):
    size = shape[0] // 2
    return ref.at[pl.ds(h * size, size)]


class _ChipGather:
    def __init__(self, shards, split):
        self.shards, self.split = list(shards), list(split)
        self.operands = self.shards
        n = len(self.shards)
        self.out_shape = [jax.ShapeDtypeStruct((N_CHIPS,) + s.shape, s.dtype) for s in self.shards]
        self.scratch = [pltpu.SemaphoreType.DMA((n, 3))] * 4 + [pltpu.SemaphoreType.DMA((n,))] * 2

    def _piece(self, a, ref, h):
        return _half_of(ref, self.shards[a].shape, h) if self.split[a] else ref

    def _own(self, a, ins, outs, sems):
        x, y, c = _mesh_pos()
        return pltpu.make_async_remote_copy(ins[a], outs[a].at[2 * x + y], sems[4].at[a], sems[5].at[a],
                                            device_id=(x, y, 1 - c), device_id_type=MESH)

    def start(self, ins, outs, sems):
        send_sems, recv_sems = sems[0], sems[1]
        x, y, c = _mesh_pos()
        me = 2 * x + y
        for a in range(len(ins)):
            for k, (px, py) in enumerate(_other_chips(x, y)):
                pltpu.make_async_remote_copy(self._piece(a, ins[a], c), self._piece(a, outs[a].at[me], c),
                                             send_sems.at[a, k], recv_sems.at[a, k],
                                             device_id=(px, py, c), device_id_type=MESH).start()
            self._own(a, ins, outs, sems).start()

    def finish(self, ins, outs, sems):
        send_sems, recv_sems, fwd_send_sems, fwd_recv_sems = sems[:4]
        x, y, c = _mesh_pos()
        sibling = (x, y, 1 - c)
        chips = _other_chips(x, y)
        n = len(ins)
        forwards = []
        for a in range(n):
            self._own(a, ins, outs, sems).wait()
        for a in range(n):
            for k, (px, py) in enumerate(chips):
                landed = self._piece(a, outs[a].at[2 * px + py], c)
                pltpu.make_async_remote_copy(landed, landed, send_sems.at[a, k], recv_sems.at[a, k],
                                             device_id=(px, py, c), device_id_type=MESH).wait_recv()
                if self.split[a]:
                    fwd = pltpu.make_async_remote_copy(landed, landed, fwd_send_sems.at[a, k], fwd_recv_sems.at[a, k],
                                                       device_id=sibling, device_id_type=MESH)
                    fwd.start()
                    forwards.append(fwd)
        for a in range(n):
            if self.split[a]:
                for k, (px, py) in enumerate(chips):
                    other = self._piece(a, outs[a].at[2 * px + py], 1 - c)
                    pltpu.make_async_remote_copy(other, other, fwd_send_sems.at[a, k], fwd_recv_sems.at[a, k],
                                                 device_id=sibling, device_id_type=MESH).wait_recv()
        for a in range(n):
            for k, (px, py) in enumerate(chips):
                sent = self._piece(a, ins[a], c)
                pltpu.make_async_remote_copy(sent, sent, send_sems.at[a, k], recv_sems.at[a, k],
                                             device_id=(px, py, c), device_id_type=MESH).wait_send()
        for fwd in forwards:
            fwd.wait_send()

    def finalize(self, gathered):
        return gathered

    def run(self, name):
        n = len(self.shards)

        def body(*refs):
            ins, outs, sems = refs[:n], refs[n:2 * n], refs[2 * n:]
            self.start(ins, outs, sems)
            self.finish(ins, outs, sems)

        gathered = pl.pallas_call(
            body, name=name, in_specs=[ANY] * n, out_specs=[ANY] * n, out_shape=self.out_shape,
            scratch_shapes=self.scratch, compiler_params=pltpu.CompilerParams(has_side_effects=True),
        )(*self.shards)
        return self.finalize(gathered)


def _ffn_weight_grads(hb, dgu, a, dyb, tag):
    dwin = _matmul_tn(hb, dgu, D_MODEL, FF_CHUNK, f"{tag}_dw_in", stack_n=True)
    dwo = _matmul_tn(a, dyb, FF_CHUNK, D_MODEL, f"{tag}_dw_out")
    return dwin, dwo.reshape(N_CHIPS, D_FF // N_CHIPS, D_MODEL)


def _local_step(x, target, w, late=None, reduce=False):
    g = {}
    acts = []
    w = dict(w)

    def ffn_weights(which, layer):
        return w[f"n{which}"], w[f"win{which}_l{layer}"], w[f"wout{which}_l{layer}"]

    def hosting(name):
        exchange, layouts = late.get(name, (None, None)) if late else (None, None)
        return exchange, (lambda arrived: w.update(layouts(arrived)) if exchange is not None else None)

    def ffn(xin, which, layer):
        name = f"ffn{which}_l{layer}_fwd"
        exchange, keep = hosting(name)
        (xo, gv, uv, hb), arrived = _ffn_fwd(xin, *ffn_weights(which, layer), layer, name, exchange)
        keep(arrived)
        acts.append((xin, gv, uv, hb))
        return xo

    x1 = ffn(x, 1, 0)
    hb_mix, proj = _norm_matmul(x1, w["nmix"], 0, w["wp"], "ab_in_proj")
    qkv = _conv_fwd(proj, w["conv_w"], "dn_conv_fwd")
    exchange, keep = hosting("dn_prep")
    (dn_u, dn_w, dn_p, dn_qd, dn_kt, dn_gl, dn_inv), arrived = _dn_prep(qkv, proj, w["hp"], "dn_prep", exchange)
    keep(arrived)
    exchange, keep = hosting("dn_scan")
    (o, dn_vn, sall), arrived = _dn_scan(dn_u, dn_w, dn_p, dn_qd, dn_kt, dn_gl, "dn_scan", exchange)
    keep(arrived)
    mix = _mix_fwd(o, proj, w["dn_norm"], w["sg_norm"], w["sg_w"], w["sg_bt"], "ab_gate_fwd")
    x2 = _matmul(mix, w["wo"], "ab_out_proj", res=x1)
    x3 = ffn(x2, 2, 0)
    x4 = ffn(x3, 1, 1)
    x5 = _pool_fwd(x4, w["nmix"], w["pool_w"], w["pool_scale"], 1, "pool_fwd")
    x6 = ffn(x5, 2, 1)
    loss, dx, g["fn"] = _loss_head(x6, target, w["fn"], "loss_head")

    dn = {1: [None, None], 2: [None, None]}
    dwin = {1: [None, None], 2: [None, None]}
    dwout = {1: [None, None], 2: [None, None]}

    def ffn_back(dxo, which, layer, saved, exchange=None):
        nw, win, wout = ffn_weights(which, layer)
        xin, gv, uv, hb = saved
        tag = f"ffn{which}_l{layer}"
        (dxi, dgu, a, dyb, dnw), arrived = _ffn_bwd(dxo, xin, nw, gv, uv, win, wout, layer, f"{tag}_bwd", exchange)
        dn[which][layer] = dnw
        dwin[which][layer], dwout[which][layer] = _ffn_weight_grads(hb, dgu, a, dyb, tag)
        return dxi, arrived

    for which in (1, 2):
        g[f"win{which}"] = dwin[which]
        g[f"wout{which}"] = dwout[which]
    reduced = {}

    def open_round(tag, keys):
        have = _sharded_grads(g)
        return _GradRound(tag, {k: have[k] for k in keys})

    dx, _ = ffn_back(dx, 2, 1, acts[3])
    dx, g["pool_w"], g["pool_scale"], dnmix1 = _pool_bwd(dx, x4, w["nmix"], w["pool_w"], w["pool_scale"], 1, "pool_bwd")
    dx, _ = ffn_back(dx, 1, 1, acts[2])
    round_a = open_round("a", REDUCE_ROUNDS[0]) if reduce else None
    dx2, arrived = ffn_back(dx, 2, 0, acts[1], round_a.swap if reduce else None)
    if reduce:
        round_a.pair_sum(arrived)
    round_b = open_round("b", REDUCE_ROUNDS[1]) if reduce else None
    dmix = _matmul(dx2, w["wo"], "ab_out_proj_bwd", trans_b=True)
    g["wo"] = _matmul_tn(mix, dx2, D_MODEL, D_MODEL, "ab_out_proj_dw")
    do, dzuv, g["dn_norm"], g["sg_norm"], g["sg_w"], g["sg_bt"] = _mix_bwd(
        dmix, o, proj, w["dn_norm"], w["sg_norm"], w["sg_w"], w["sg_bt"], "ab_gate_bwd")
    (dvn, dkt, dgl), arrived = _dn_scan_bwd(dn_w, dn_p, dn_qd, dn_kt, dn_gl, dn_vn, sall, do, "dn_scan_bwd",
                                            round_b.swap if reduce else None)
    if reduce:
        round_b.pair_sum(arrived)
    (dqkv_act, dba, g["hp"]), arrived = _dn_prep_bwd(qkv, proj, w["hp"], sall, dn_vn, do, dvn, dkt, dgl, dn_inv, dn_u, dn_w,
                                                     "dn_prep_bwd", round_b.scatter if reduce else None)
    if reduce:
        reduced.update(round_b.finish(arrived))
    dqkv, g["conv_w"] = _conv_bwd(dqkv_act, proj, w["conv_w"], "dn_conv_bwd")
    dproj = [dqkv, dzuv, dba]
    dx1, dnmix0 = _norm_matmul_bwd(dproj, w["wp"], x1, w["nmix"], dx2, 0, "ab_in_proj_bwd")
    g["wp"] = [_matmul_tn(hb_mix, piece, D_MODEL, 768, f"ab_in_proj_dw_{k}") for k, piece in enumerate(dproj)]
    dx0, arrived = ffn_back(dx1, 1, 0, acts[0], round_a.scatter if reduce else None)
    if reduce:
        reduced.update(round_a.finish(arrived))
        round_c = open_round("c", REDUCE_ROUNDS[2])
        round_c.pair_sum(round_c.swap.run("grad_c_pair_swap"))
        reduced.update(round_c.finish(round_c.scatter.run("grad_c_chip_scatter")))

    g["n1"] = jnp.concatenate(dn[1], axis=0)
    g["n2"] = jnp.concatenate(dn[2], axis=0)
    g["nmix"] = jnp.concatenate([dnmix0, dnmix1], axis=0)
    return loss, dx0, g, reduced


SHARDED = ("ffn1_w_in", "ffn1_w_out", "ffn2_w_in", "ffn2_w_out", "ab_w_in", "ab_w_out", "pool_w", "dn_conv_w", "pool_scale")
REPLICATED = ("ffn_norm1", "mix_norm", "ffn_norm2", "dn_a_log", "dn_dt_bias", "dn_out_norm", "sg_norm", "sg_w", "sg_b", "final_norm")
QKVZ = 4 * DN_HEADS * DN_DIM
N_GATES = 2 * DN_HEADS
IN_PROJ = QKVZ + N_GATES + 2 * SG_GROUPS * SG_DIM


def _shard_pieces(wts, keys):
    out = []
    for n, layer in keys:
        a = wts[n][0 if layer is None else layer]
        a = a[None] if a.ndim == 1 else a
        out.append(a.astype(BF16) if n in MATRICES else a)
    return out


def _replicated_layouts(rep):
    per_layer = lambda a: a.reshape(a.shape[0], 1, D_MODEL)
    w = {"n1": per_layer(rep["ffn_norm1"]), "nmix": per_layer(rep["mix_norm"]), "n2": per_layer(rep["ffn_norm2"])}
    hp = jnp.zeros((8, 128), F32)
    w["hp"] = hp.at[0, DN_HEADS:N_GATES].set(rep["dn_a_log"][0]).at[1, DN_HEADS:N_GATES].set(rep["dn_dt_bias"][0])
    w["dn_norm"] = rep["dn_out_norm"]
    w["sg_norm"] = rep["sg_norm"][0]
    w["sg_w"] = rep["sg_w"][0]
    w["sg_bt"] = jnp.zeros((SG_CHUNK, 128), F32).at[:, :SG_GROUPS].set(rep["sg_b"][0].T)
    w["fn"] = rep["final_norm"].reshape(1, D_MODEL)
    return w


def _layouts_from(gathered):
    w = {}
    for (n, layer), a in gathered.items():
        if n in ("ffn1_w_in", "ffn2_w_in"):
            w[f"win{n[3]}_l{layer}"] = a
        elif n in ("ffn1_w_out", "ffn2_w_out"):
            w[f"wout{n[3]}_l{layer}"] = a
        elif n == "ab_w_in":
            ab_in = jnp.transpose(a, (1, 0, 2)).reshape(D_MODEL, IN_PROJ)
            w["wp"] = jnp.concatenate([ab_in[:, :QKVZ], ab_in[:, QKVZ + N_GATES:], ab_in[:, QKVZ:QKVZ + N_GATES],
                                       jnp.zeros((D_MODEL, PROJ_W - IN_PROJ), ab_in.dtype)], axis=1)
        elif n == "dn_conv_w":
            w["conv_w"] = jnp.transpose(a, (1, 0, 2)).reshape(DN_CONV, 3 * DN_HEADS * DN_DIM)
        elif n == "ab_w_out":
            w["wo"] = a.reshape(D_MODEL, D_MODEL)
        elif n == "pool_w":
            w["pool_w"] = jnp.transpose(a, (1, 0, 2, 3)).reshape(len(POOL_WINDOWS), POOL_DIM, POOL_DIM)
        elif n == "pool_scale":
            w["pool_scale"] = a.reshape(1, D_MODEL)
    return w


def _sharded_grads(g):
    nw = len(POOL_WINDOWS)
    sharded = {}
    for n, key in (("ffn1_w_in", "win1"), ("ffn1_w_out", "wout1"), ("ffn2_w_in", "win2"), ("ffn2_w_out", "wout2")):
        for layer, a in enumerate(g.get(key, ())):
            if a is not None:
                sharded[(n, layer)] = a
    if "wp" in g:
        qkv, zuv, gates = g["wp"]
        z_width = DN_HEADS * DN_DIM
        ab_in = jnp.concatenate([qkv, zuv[:, :z_width], gates[:, :N_GATES], zuv[:, z_width:]], axis=1)
        sharded[("ab_w_in", None)] = jnp.transpose(ab_in.reshape(D_MODEL, N_CHIPS, IN_PROJ // N_CHIPS), (1, 0, 2))
    if "wo" in g:
        sharded[("ab_w_out", None)] = g["wo"].reshape(N_CHIPS, D_MODEL // N_CHIPS, D_MODEL)
    if "pool_w" in g:
        sharded[("pool_w", None)] = jnp.transpose(g["pool_w"].reshape(nw, N_CHIPS, POOL_DIM // N_CHIPS, POOL_DIM), (1, 0, 2, 3))
    if "conv_w" in g:
        sharded[("dn_conv_w", None)] = jnp.transpose(g["conv_w"].reshape(DN_CONV, N_CHIPS, -1), (1, 0, 2))
    if "pool_scale" in g:
        sharded[("pool_scale", None)] = g["pool_scale"].reshape(N_CHIPS, 1, D_MODEL // N_CHIPS)
    return sharded


def _replicated_grads(g):
    rep = {
        "ffn_norm1": g["n1"], "mix_norm": g["nmix"], "ffn_norm2": g["n2"],
        "dn_a_log": g["hp"][0:1, DN_HEADS:N_GATES], "dn_dt_bias": g["hp"][1:2, DN_HEADS:N_GATES],
        "dn_out_norm": g["dn_norm"], "sg_norm": g["sg_norm"][None], "sg_w": g["sg_w"][None],
        "sg_b": g["sg_bt"][:, :SG_GROUPS].T[None], "final_norm": g["fn"].reshape(D_MODEL),
    }
    return rep


def _as_halves(a):
    shape = a.shape[1:]
    if len(shape) >= 2 and shape[0] % 2 == 0:
        return a.reshape(N_CHIPS, 2, -1, shape[-1])
    return a.reshape(N_CHIPS, 2, 1, -1)


def _row_block(rows):
    for cand in (256, 176, 128, 64, 32, 16):
        if rows % cand == 0:
            return cand
    return rows


def _from_halves(pairs, core, shape):
    mine_first = jnp.stack([t for mine, other in pairs for t in (mine, other)])
    other_first = jnp.stack([t for mine, other in pairs for t in (other, mine)])
    return jnp.where(core == 0, mine_first, other_first).reshape(shape)


class _PairSwap:
    def __init__(self, packs):
        self.operands = list(packs)
        n = len(self.operands)
        self.out_shape = [jax.ShapeDtypeStruct((p.shape[0],) + p.shape[2:], p.dtype) for p in self.operands]
        self.scratch = [pltpu.SemaphoreType.DMA((n,))] * 2

    def _copies(self, ins, outs, sems):
        x, y, c = _mesh_pos()
        return [pltpu.make_async_remote_copy(ins[k].at[:, 1 - c], outs[k], sems[0].at[k], sems[1].at[k],
                                             device_id=(x, y, 1 - c), device_id_type=MESH) for k in range(len(ins))]

    def start(self, ins, outs, sems):
        for cp in self._copies(ins, outs, sems):
            cp.start()

    def finish(self, ins, outs, sems):
        for cp in self._copies(ins, outs, sems):
            cp.wait()

    def finalize(self, results):
        return results

    def run(self, name):
        n = len(self.operands)

        def body(*refs):
            ins, outs, sems = refs[:n], refs[n:2 * n], refs[2 * n:]
            self.start(ins, outs, sems)
            self.finish(ins, outs, sems)

        return pl.pallas_call(
            body, name=name, in_specs=[ANY] * n, out_specs=[ANY] * n, out_shape=self.out_shape, scratch_shapes=self.scratch,
            compiler_params=pltpu.CompilerParams(has_side_effects=True),
        )(*self.operands)


def _add_pair(pack, recv, core, name):
    nchip, _, rows, lanes = pack.shape
    rb = _row_block(rows)

    def body(c_ref, a_ref, b_ref, o32_ref, o16_ref):
        s = a_ref[...] + b_ref[...]
        o32_ref[...] = s
        o16_ref[...] = _bf(s)

    blk = pl.BlockSpec((None, rb, lanes), lambda p, i, c: (p, i, 0))
    return pl.pallas_call(
        body, name=name,
        grid_spec=pltpu.PrefetchScalarGridSpec(
            num_scalar_prefetch=1, grid=(nchip, rows // rb),
            in_specs=[pl.BlockSpec((None, None, rb, lanes), lambda p, i, c: (p, c[0], i, 0)), blk],
            out_specs=[blk, blk]),
        out_shape=[jax.ShapeDtypeStruct((nchip, rows, lanes), F32), jax.ShapeDtypeStruct((nchip, rows, lanes), BF16)],
        compiler_params=_params(("parallel", "parallel")),
    )(core, pack, recv)


class _ChipScatter:
    def __init__(self, parts16):
        self.operands = list(parts16)
        n = len(self.operands)
        self.out_shape = [jax.ShapeDtypeStruct((N_CHIPS - 1,) + p.shape[1:], p.dtype) for p in self.operands]
        self.scratch = [pltpu.SemaphoreType.DMA((n, N_CHIPS - 1))] * 2

    def _copies(self, ins, outs, sems):
        x, y, c = _mesh_pos()
        return [pltpu.make_async_remote_copy(ins[a].at[2 * px + py], outs[a].at[k], sems[0].at[a, k], sems[1].at[a, k],
                                             device_id=(px, py, c), device_id_type=MESH)
                for a in range(len(ins)) for k, (px, py) in enumerate(_other_chips(x, y))]

    def start(self, ins, outs, sems):
        for cp in self._copies(ins, outs, sems):
            cp.start()

    def finish(self, ins, outs, sems):
        for cp in self._copies(ins, outs, sems):
            cp.wait()

    def finalize(self, results):
        return results

    def run(self, name):
        n = len(self.operands)

        def body(*refs):
            ins, outs, sems = refs[:n], refs[n:2 * n], refs[2 * n:]
            self.start(ins, outs, sems)
            self.finish(ins, outs, sems)

        return pl.pallas_call(
            body, name=name, in_specs=[ANY] * n, out_specs=[ANY] * n, out_shape=self.out_shape, scratch_shapes=self.scratch,
            compiler_params=pltpu.CompilerParams(has_side_effects=True),
        )(*self.operands)


def _sum_chips(part32, recv16, chip, name):
    nchip, rows, lanes = part32.shape
    rb = _row_block(rows)

    def body(p_ref, own_ref, r_ref, o_ref):
        s = own_ref[...]
        for k in range(nchip - 1):
            s = s + r_ref[k].astype(F32)
        o_ref[...] = s

    return pl.pallas_call(
        body, name=name,
        grid_spec=pltpu.PrefetchScalarGridSpec(
            num_scalar_prefetch=1, grid=(rows // rb,),
            in_specs=[pl.BlockSpec((None, rb, lanes), lambda i, p: (p[0], i, 0)),
                      pl.BlockSpec((nchip - 1, rb, lanes), lambda i, p: (0, i, 0))],
            out_specs=pl.BlockSpec((rb, lanes), lambda i, p: (i, 0))),
        out_shape=jax.ShapeDtypeStruct((rows, lanes), F32),
        compiler_params=_params(("parallel",)),
    )(chip, part32, recv16)


def _share_with_sibling(halves, name):
    n = len(halves)

    def body(*refs):
        ins, outs, send_sems, recv_sems = refs[:n], refs[n:2 * n], refs[2 * n], refs[2 * n + 1]
        x, y, c = _mesh_pos()
        copies = [pltpu.make_async_remote_copy(ins[k], outs[k], send_sems.at[k], recv_sems.at[k],
                                               device_id=(x, y, 1 - c), device_id_type=MESH) for k in range(n)]
        for cp in copies:
            cp.start()
        for cp in copies:
            cp.wait()

    return pl.pallas_call(
        body, name=name, in_specs=[ANY] * n, out_specs=[ANY] * n,
        out_shape=[jax.ShapeDtypeStruct(h.shape, h.dtype) for h in halves],
        scratch_shapes=[pltpu.SemaphoreType.DMA((n,)), pltpu.SemaphoreType.DMA((n,))],
        compiler_params=pltpu.CompilerParams(has_side_effects=True),
    )(*halves)


class _GradRound:
    def __init__(self, tag, pieces):
        self.tag, self.keys = tag, list(pieces)
        self.shapes = [pieces[k].shape[1:] for k in self.keys]
        self.packs = [_as_halves(pieces[k]) for k in self.keys]
        self.swap = _PairSwap(self.packs)

    def pair_sum(self, recvs):
        _, _, c = _mesh_pos()
        core = jnp.reshape(c, (1,)).astype(jnp.int32)
        sums = [_add_pair(p, r, core, f"grad_{self.tag}_pair_add_{i}") for i, (p, r) in enumerate(zip(self.packs, recvs))]
        self.parts32 = [s[0] for s in sums]
        self.scatter = _ChipScatter([s[1] for s in sums])

    def finish(self, recvs16):
        x, y, _ = _mesh_pos()
        chip = jnp.reshape(2 * x + y, (1,)).astype(jnp.int32)
        halves = [_sum_chips(p, r, chip, f"grad_{self.tag}_chip_sum_{i}") for i, (p, r) in enumerate(zip(self.parts32, recvs16))]
        others = _share_with_sibling(halves, f"grad_{self.tag}_pair_share")
        return dict(zip(self.keys, zip(halves, others)))


def _pack_small(vals):
    parts = []
    for n in REPLICATED:
        flat = vals[n].reshape(-1)
        rows = -(-flat.shape[0] // 128)
        rows = -(-rows // 8) * 8
        parts.append(jnp.pad(flat, (0, rows * 128 - flat.shape[0])).reshape(rows, 128))
    return jnp.concatenate(parts, axis=0)


def _unpack_small(pack, like):
    out, off = {}, 0
    for n in REPLICATED:
        size = like[n].size
        rows = -(-size // 128)
        rows = -(-rows // 8) * 8
        out[n] = pack[off:off + rows].reshape(-1)[:size].reshape(like[n].shape)
        off += rows
    return out


def _all_to_all_small(pack, name):
    rows, lanes = pack.shape
    flips = [(dx, dy, dc) for dx in (0, 1) for dy in (0, 1) for dc in (0, 1)][1:]

    def body(src_ref, out_ref, send_sems, recv_sems, local_sem):
        x, y, c = _mesh_pos()
        me = 4 * x + 2 * y + c
        loc = pltpu.make_async_copy(src_ref, out_ref.at[me], local_sem)
        loc.start()
        copies = []
        for k, (dx, dy, dc) in enumerate(flips):
            peer = (x ^ dx, y ^ dy, c ^ dc)
            cp = pltpu.make_async_remote_copy(src_ref, out_ref.at[me], send_sems.at[k], recv_sems.at[k],
                                              device_id=peer, device_id_type=MESH)
            cp.start()
            copies.append(cp)
        for k, (dx, dy, dc) in enumerate(flips):
            peer = (x ^ dx, y ^ dy, c ^ dc)
            pltpu.make_async_remote_copy(src_ref, out_ref.at[4 * peer[0] + 2 * peer[1] + peer[2]], send_sems.at[k],
                                         recv_sems.at[k], device_id=peer, device_id_type=MESH).wait_recv()
        for cp in copies:
            cp.wait_send()
        loc.wait()

    return pl.pallas_call(
        body, name=name, in_specs=[ANY], out_specs=ANY,
        out_shape=jax.ShapeDtypeStruct((8, rows, lanes), pack.dtype),
        scratch_shapes=[pltpu.SemaphoreType.DMA((7,)), pltpu.SemaphoreType.DMA((7,)), pltpu.SemaphoreType.DMA],
        compiler_params=pltpu.CompilerParams(has_side_effects=True),
    )(pack)


def _sum_devices(stack, name):
    ndev, rows, lanes = stack.shape

    def body(s_ref, o_ref):
        s = s_ref[0]
        for d in range(1, ndev):
            s = s + s_ref[d]
        o_ref[...] = s

    return pl.pallas_call(
        body, name=name, grid=(1,),
        in_specs=[pl.BlockSpec((ndev, rows, lanes), lambda i: (0, 0, 0))],
        out_specs=pl.BlockSpec((rows, lanes), lambda i: (0, 0)),
        out_shape=jax.ShapeDtypeStruct((rows, lanes), F32),
    )(stack)


WEIGHT_ORDER = ("ffn_norm1", "ffn1_w_in", "ffn1_w_out", "mix_norm", "ffn_norm2", "ffn2_w_in", "ffn2_w_out", "ab_w_in",
                "dn_conv_w", "dn_a_log", "dn_dt_bias", "dn_out_norm", "sg_norm", "sg_w", "sg_b", "ab_w_out", "pool_w",
                "pool_scale", "final_norm")
MATRICES = ("ffn1_w_in", "ffn1_w_out", "ffn2_w_in", "ffn2_w_out", "ab_w_in", "ab_w_out", "pool_w")
GATHER_FIRST = (("ffn1_w_in", 0), ("ffn1_w_out", 0))
GATHER_LATER = {"ffn1_l0_fwd": (("ab_w_in", None), ("dn_conv_w", None), ("ab_w_out", None)),
                "dn_prep": (("ffn2_w_in", 0), ("ffn2_w_out", 0)),
                "dn_scan": (("ffn1_w_in", 1), ("ffn1_w_out", 1)),
                "ffn2_l0_fwd": (("pool_w", None), ("pool_scale", None), ("ffn2_w_in", 1), ("ffn2_w_out", 1))}
REDUCE_ROUNDS = ((("ffn2_w_in", 1), ("ffn2_w_out", 1), ("ffn1_w_in", 1), ("ffn1_w_out", 1), ("pool_w", None), ("pool_scale", None)),
                 (("ffn2_w_in", 0), ("ffn2_w_out", 0)),
                 (("ffn1_w_in", 0), ("ffn1_w_out", 0), ("ab_w_out", None), ("ab_w_in", None), ("dn_conv_w", None)))


def _as_2d(a):
    return a.reshape(-1, a.shape[-1])


def kernel(x, ffn_norm1, ffn1_w_in, ffn1_w_out, mix_norm, ffn_norm2, ffn2_w_in, ffn2_w_out, ab_w_in, dn_conv_w, dn_a_log, dn_dt_bias, dn_out_norm, sg_norm, sg_w, sg_b, ab_w_out, pool_w, pool_scale, final_norm, loss_target, m_ffn_norm1, m_ffn1_w_in, m_ffn1_w_out, m_mix_norm, m_ffn_norm2, m_ffn2_w_in, m_ffn2_w_out, m_ab_w_in, m_dn_conv_w, m_dn_a_log, m_dn_dt_bias, m_dn_out_norm, m_sg_norm, m_sg_w, m_sg_b, m_ab_w_out, m_pool_w, m_pool_scale, m_final_norm, v_ffn_norm1, v_ffn1_w_in, v_ffn1_w_out, v_mix_norm, v_ffn_norm2, v_ffn2_w_in, v_ffn2_w_out, v_ab_w_in, v_dn_conv_w, v_dn_a_log, v_dn_dt_bias, v_dn_out_norm, v_sg_norm, v_sg_w, v_sg_b, v_ab_w_out, v_pool_w, v_pool_scale, v_final_norm):
    given = dict(locals())
    wts = {n: given[n] for n in WEIGHT_ORDER}
    mom_m = {n: given["m_" + n] for n in WEIGHT_ORDER}
    mom_v = {n: given["v_" + n] for n in WEIGHT_ORDER}

    rep = {n: wts[n] for n in REPLICATED}
    first = _ChipGather(_shard_pieces(wts, GATHER_FIRST), [n in MATRICES for n, _ in GATHER_FIRST])
    w = {**_replicated_layouts(rep), **_layouts_from(dict(zip(GATHER_FIRST, first.run("weight_gather_first"))))}
    late = {host: (_ChipGather(_shard_pieces(wts, keys), [n in MATRICES for n, _ in keys]),
                   functools.partial(lambda keys, arrived: _layouts_from(dict(zip(keys, arrived))), keys))
            for host, keys in GATHER_LATER.items()}

    loss, dx, g, reduced = _local_step(x[0], loss_target[0], w, reduce=True, late=late)
    g_rep = _replicated_grads(g)
    grads = {}
    core = lax.axis_index("c")
    for n in SHARDED:
        layers = [reduced[(n, layer)] for layer in range(wts[n].shape[0])] if (n, 0) in reduced else [reduced[(n, None)]]
        grads[n] = _from_halves(layers, core, wts[n].shape)
    small = _sum_devices(_all_to_all_small(_pack_small(g_rep), "grad_small_exchange"), "grad_small_sum")
    grads.update(_unpack_small(small, rep))

    delta, new_m, new_v = {}, {}, {}
    for n in SHARDED:
        d, m1, v1 = _adamw(_as_2d(wts[n]), _as_2d(grads[n]), _as_2d(mom_m[n]), _as_2d(mom_v[n]), f"adamw_{n}")
        delta[n], new_m[n], new_v[n] = (t.reshape(wts[n].shape) for t in (d, m1, v1))
    d, m1, v1 = _adamw(_pack_small(rep), small, _pack_small({n: mom_m[n] for n in REPLICATED}),
                       _pack_small({n: mom_v[n] for n in REPLICATED}), "adamw_replicated")
    for tgt, packed in ((delta, d), (new_m, m1), (new_v, v1)):
        tgt.update(_unpack_small(packed, rep))

    total = lax.psum(loss[0, 0], ("x", "y", "c"))
    outs = [total, dx[None]]
    for group in (grads, delta, new_m, new_v):
        outs.extend(group[n] for n in WEIGHT_ORDER)
    return tuple(outs)
```

```python
import functools

import jax
import jax.numpy as jnp
from jax import lax
from jax.experimental import pallas as pl
from jax.experimental.pallas import tpu as pltpu

F32, BF16 = jnp.float32, jnp.bfloat16
NORM_EPS = 1e-6
D_MODEL = 1024
D_FF = 2816
N_CHIPS = 4
FF_CHUNK = 2 * D_FF // N_CHIPS
DN_HEADS, DN_DIM, DN_CHUNK, DN_CONV = 4, 128, 64, 4
DN_BLOCK = 2 * DN_CHUNK
DN_PREP_CHUNKS = 8
DN_SCAN_CHUNKS = 8
SG_GROUPS, SG_DIM, SG_CHUNK = 4, 128, 128
POOL_WINDOWS = (2, 4, 8, 16)
POOL_DIM = 256
POOL_HALO = 16
CONV_HALO = 8
PROJ_W = 3200
BA_BLOCK = 3072 // 128
ADAM_LR, ADAM_B1, ADAM_B2, ADAM_EPS, ADAM_WD, ADAM_STEP = 0.001, 0.9, 0.999, 1e-08, 0.01, 10
VMEM_BIG = 52 * 1024 * 1024
FFN_FWD_ROWS = 512
FFN_BWD_ROWS = 256
MESH = pl.DeviceIdType.MESH
HI = lax.Precision.HIGHEST
ANY = pl.BlockSpec(memory_space=pl.ANY)


def _params(sem=None, vmem=None):
    return pltpu.CompilerParams(dimension_semantics=sem, vmem_limit_bytes=vmem)


def _dot(a, b):
    return jnp.dot(a, b, preferred_element_type=F32)


def _dot_nt(a, b):
    return lax.dot_general(a, b, (((1,), (1,)), ((), ())), preferred_element_type=F32)


def _dot_tn(a, b):
    return lax.dot_general(a, b, (((0,), (0,)), ((), ())), preferred_element_type=F32)


def _dot_hi(a, b):
    return jnp.dot(a, b, preferred_element_type=F32, precision=HI)


def _dot_mid(a, b):
    return jnp.dot(a, b, preferred_element_type=F32, precision=lax.Precision.HIGH)


def _bf(a):
    return a.astype(BF16)


def _rms_stats(x):
    r = lax.rsqrt(jnp.mean(x * x, axis=-1, keepdims=True) + NORM_EPS)
    return x * r, r


def _rms_bwd(dh, xhat, r, w):
    dhn = dh * w
    return r * (dhn - xhat * jnp.mean(dhn * xhat, axis=-1, keepdims=True))


def _sigmoid(x):
    return jax.nn.sigmoid(x)


def _silu_grad(x, s):
    return s * (1.0 + x * (1.0 - s))


def _gelu(x):
    return 0.5 * x * (1.0 + lax.erf(x * 0.7071067811865476))


def _gelu_grad(x):
    return 0.5 * (1.0 + lax.erf(x * 0.7071067811865476)) + x * jnp.exp(-0.5 * x * x) * 0.3989422804014327


def _softplus(x):
    return jnp.maximum(x, 0.0) + jnp.log(1.0 + jnp.exp(-jnp.abs(x)))


def _tile(n, pref):
    t = min(n, pref)
    assert n % t == 0, (n, t)
    return t


def _ffn_weight_specs():
    once = pl.Buffered(1)
    return [pl.BlockSpec((N_CHIPS, D_MODEL, FF_CHUNK), lambda i: (0, 0, 0), pipeline_mode=once),
            pl.BlockSpec((N_CHIPS, D_FF // N_CHIPS, D_MODEL), lambda i: (0, 0, 0), pipeline_mode=once)]


def _ffn_fwd(x, nw, win, wout, layer, name, exchange=None):
    T, D = x.shape
    tm = _tile(T, FFN_FWD_ROWS)
    nj = N_CHIPS // 2

    def body(x_ref, n_ref, win_ref, wo_ref, xo_ref, g_ref, u_ref, hb_ref):
        xv = x_ref[...]
        xhat, _ = _rms_stats(xv)
        h = _bf(xhat * n_ref[...])
        hb_ref[...] = h
        acc = None
        for j in range(nj):
            cols = slice(j * FF_CHUNK, (j + 1) * FF_CHUNK)
            g = _dot(h, win_ref[j])
            u = _dot(h, win_ref[nj + j])
            g_ref[:, cols] = _bf(g)
            u_ref[:, cols] = _bf(u)
            part = _dot(_bf(g * _sigmoid(g) * u), wo_ref[2 * j:2 * j + 2].reshape(FF_CHUNK, D))
            acc = part if acc is None else acc + part
        xo_ref[...] = xv + 0.5 * acc

    row = pl.BlockSpec((tm, D), lambda i: (i, 0))
    wide = pl.BlockSpec((tm, D_FF), lambda i: (i, 0))
    return _call_with_exchange(
        body, exchange, name=name, steps=T // tm, vmem=VMEM_BIG,
        in_specs=[row, pl.BlockSpec((None, 1, D), lambda i: (layer, 0, 0))] + _ffn_weight_specs(),
        out_specs=[row, wide, wide, row],
        out_shape=[jax.ShapeDtypeStruct((T, D), F32), jax.ShapeDtypeStruct((T, D_FF), BF16),
                   jax.ShapeDtypeStruct((T, D_FF), BF16), jax.ShapeDtypeStruct((T, D), BF16)],
        args=(x, nw, win, wout))


def _ffn_bwd(dxo, x, nw, g, u, win, wout, layer, name, exchange=None):
    T, D = x.shape
    tm = _tile(T, FFN_BWD_ROWS)
    nj = N_CHIPS // 2

    def body(dxo_ref, x_ref, n_ref, g_ref, u_ref, win_ref, wo_ref, dx_ref, dgu_ref, a_ref, dyb_ref, dn_ref):
        @pl.when(pl.program_id(0) == 0)
        def _():
            dn_ref[...] = jnp.zeros_like(dn_ref)

        dxo = dxo_ref[...]
        dyb = _bf(0.5 * dxo)
        dyb_ref[...] = dyb
        dh = None
        for j in range(nj):
            cols = slice(j * FF_CHUNK, (j + 1) * FF_CHUNK)
            da = _dot_nt(dyb, wo_ref[2 * j:2 * j + 2].reshape(FF_CHUNK, D))
            gv = g_ref[:, cols].astype(F32)
            uv = u_ref[:, cols].astype(F32)
            sg = _sigmoid(gv)
            sl = gv * sg
            dgb = _bf(da * uv * _silu_grad(gv, sg))
            dub = _bf(da * sl)
            a_ref[:, cols] = _bf(sl * uv)
            dgu_ref[:, cols] = dgb
            dgu_ref[:, D_FF + j * FF_CHUNK:D_FF + (j + 1) * FF_CHUNK] = dub
            part = _dot_nt(dgb, win_ref[j]) + _dot_nt(dub, win_ref[nj + j])
            dh = part if dh is None else dh + part
        xhat, r = _rms_stats(x_ref[...])
        dx_ref[...] = dxo + _rms_bwd(dh, xhat, r, n_ref[...])
        dn_ref[...] += jnp.sum(dh * xhat, axis=0, keepdims=True)

    row = pl.BlockSpec((tm, D), lambda i: (i, 0))
    wide = pl.BlockSpec((tm, D_FF), lambda i: (i, 0))
    return _call_with_exchange(
        body, exchange, name=name, steps=T // tm, vmem=VMEM_BIG,
        in_specs=[row, row, pl.BlockSpec((None, 1, D), lambda i: (layer, 0, 0)), wide, wide] + _ffn_weight_specs(),
        out_specs=[row, pl.BlockSpec((tm, 2 * D_FF), lambda i: (i, 0)), wide, row, pl.BlockSpec((1, D), lambda i: (0, 0))],
        out_shape=[jax.ShapeDtypeStruct((T, D), F32), jax.ShapeDtypeStruct((T, 2 * D_FF), BF16),
                   jax.ShapeDtypeStruct((T, D_FF), BF16), jax.ShapeDtypeStruct((T, D), BF16),
                   jax.ShapeDtypeStruct((1, D), F32)],
        args=(dxo, x, nw, g, u, win, wout))


def _matmul_tn(a, b, bm, bn, name, stack_n=False):
    T, M = a.shape
    N = b.shape[1]
    tk = _tile(T, 2048)
    bm, bn = _tile(M, bm), _tile(N, bn)

    def body(a_ref, b_ref, o_ref):
        @pl.when(pl.program_id(2) == 0)
        def _():
            o_ref[...] = jnp.zeros_like(o_ref)

        o_ref[...] += _dot_tn(_bf(a_ref[...]), _bf(b_ref[...]))

    if stack_n:
        out_spec = pl.BlockSpec((None, bm, bn), lambda m, n, k: (n, m, 0))
        out_shape = jax.ShapeDtypeStruct((N // bn, M, bn), F32)
    else:
        out_spec = pl.BlockSpec((bm, bn), lambda m, n, k: (m, n))
        out_shape = jax.ShapeDtypeStruct((M, N), F32)
    return pl.pallas_call(
        body, name=name, grid=(M // bm, N // bn, T // tk),
        in_specs=[pl.BlockSpec((tk, bm), lambda m, n, k: (k, m)),
                  pl.BlockSpec((tk, bn), lambda m, n, k: (k, n))],
        out_specs=out_spec, out_shape=out_shape,
        compiler_params=_params(("parallel", "parallel", "arbitrary"), VMEM_BIG),
    )(a, b)


def _matmul(a, b, name, trans_b=False, res=None, out_dtype=F32):
    T, K = a.shape
    N = b.shape[0] if trans_b else b.shape[1]
    tm = _tile(T, 512)

    def body(*refs):
        a_ref, b_ref = refs[0], refs[1]
        o_ref = refs[-1]
        av, bv = _bf(a_ref[...]), _bf(b_ref[...])
        acc = _dot_nt(av, bv) if trans_b else _dot(av, bv)
        if res is not None:
            acc = acc + refs[2][...]
        o_ref[...] = acc.astype(out_dtype)

    in_specs = [pl.BlockSpec((tm, K), lambda i: (i, 0)), pl.BlockSpec(b.shape, lambda i: (0, 0))]
    args = [a, b]
    if res is not None:
        in_specs.append(pl.BlockSpec((tm, N), lambda i: (i, 0)))
        args.append(res)
    return pl.pallas_call(
        body, name=name, grid=(T // tm,), in_specs=in_specs,
        out_specs=pl.BlockSpec((tm, N), lambda i: (i, 0)),
        out_shape=jax.ShapeDtypeStruct((T, N), out_dtype),
        compiler_params=_params(("parallel",), VMEM_BIG),
    )(*args)


def _norm_matmul(x, nw, layer, b, name):
    T, D = x.shape
    N = b.shape[1]
    tm = _tile(T, 512)

    def body(x_ref, n_ref, b_ref, hb_ref, o_ref):
        xhat, _ = _rms_stats(x_ref[...])
        h = _bf(xhat * n_ref[...])
        hb_ref[...] = h
        o_ref[...] = _dot(h, b_ref[...])

    return pl.pallas_call(
        body, name=name, grid=(T // tm,),
        in_specs=[pl.BlockSpec((tm, D), lambda i: (i, 0)), pl.BlockSpec((None, 1, D), lambda i: (layer, 0, 0)),
                  pl.BlockSpec(b.shape, lambda i: (0, 0))],
        out_specs=[pl.BlockSpec((tm, D), lambda i: (i, 0)), pl.BlockSpec((tm, N), lambda i: (i, 0))],
        out_shape=[jax.ShapeDtypeStruct((T, D), BF16), jax.ShapeDtypeStruct((T, N), F32)],
        compiler_params=_params(("parallel",), VMEM_BIG),
    )(x, nw, b)


def _norm_matmul_bwd(pieces, b, x, nw, dres, layer, name):
    T, D = x.shape
    tm = _tile(T, 512)
    widths = [p.shape[1] for p in pieces]
    offsets = [sum(widths[:k]) for k in range(len(pieces))]
    assert offsets[-1] + widths[-1] == b.shape[1] and all(o % 128 == 0 for o in offsets) and b.shape[0] == D
    n = len(pieces)

    def body(*refs):
        b_ref, x_ref, n_ref, dr_ref, dx_ref, dn_ref = refs[n:]

        @pl.when(pl.program_id(0) == 0)
        def _():
            dn_ref[...] = jnp.zeros_like(dn_ref)

        dh = None
        for a_ref, off, width in zip(refs[:n], offsets, widths):
            part = _dot_nt(_bf(a_ref[...]), b_ref[:, off:off + width])
            dh = part if dh is None else dh + part
        xhat, r = _rms_stats(x_ref[...])
        dx_ref[...] = dr_ref[...] + _rms_bwd(dh, xhat, r, n_ref[...])
        dn_ref[...] += jnp.sum(dh * xhat, axis=0, keepdims=True)

    row = pl.BlockSpec((tm, D), lambda i: (i, 0))
    return pl.pallas_call(
        body, name=name, grid=(T // tm,),
        in_specs=[pl.BlockSpec((tm, wd), lambda i: (i, 0)) for wd in widths]
        + [pl.BlockSpec(b.shape, lambda i: (0, 0)), row, pl.BlockSpec((None, 1, D), lambda i: (layer, 0, 0)), row],
        out_specs=[row, pl.BlockSpec((1, D), lambda i: (0, 0))],
        out_shape=[jax.ShapeDtypeStruct((T, D), F32), jax.ShapeDtypeStruct((1, D), F32)],
        compiler_params=_params(("arbitrary",), VMEM_BIG),
    )(*pieces, b, x, nw, dres)


def _shift_rows(x, s):
    n = x.shape[0]
    s = s % n
    return x if s == 0 else pltpu.roll(x, s, 0)


def _conv_fwd(proj, conv_w, name):
    T = proj.shape[0]
    C = 3 * DN_HEADS * DN_DIM
    cb = 512
    tm = _tile(T, 512)
    hb = tm // CONV_HALO

    def body(x_ref, xp_ref, w_ref, o_ref):
        i = pl.program_id(1)
        prev = jnp.where(i == 0, 0.0, xp_ref[...])
        ext = jnp.concatenate([prev, x_ref[...]], axis=0)
        w = w_ref[...]
        y = ext * w[DN_CONV - 1:DN_CONV, :]
        for k in range(DN_CONV - 1):
            y = y + _shift_rows(ext, DN_CONV - 1 - k) * w[k:k + 1, :]
        y = y[CONV_HALO:, :]
        o_ref[...] = y * _sigmoid(y)

    return pl.pallas_call(
        body, name=name, grid=(C // cb, T // tm),
        in_specs=[pl.BlockSpec((tm, cb), lambda c, i: (i, c)),
                  pl.BlockSpec((CONV_HALO, cb), lambda c, i: (jnp.maximum(i * hb - 1, 0), c)),
                  pl.BlockSpec((DN_CONV, cb), lambda c, i: (0, c))],
        out_specs=pl.BlockSpec((tm, cb), lambda c, i: (i, c)),
        out_shape=jax.ShapeDtypeStruct((T, C), F32),
        compiler_params=_params(("parallel", "parallel")),
    )(proj, proj, conv_w)


def _conv_bwd(dy, proj, conv_w, name):
    T = proj.shape[0]
    C = 3 * DN_HEADS * DN_DIM
    cb = 512
    tm = _tile(T, 512)
    hb = tm // CONV_HALO
    nt = T // tm

    def body(x_ref, xp_ref, xn_ref, dy_ref, dyn_ref, w_ref, dx_ref, dw_ref):
        i = pl.program_id(1)

        @pl.when(i == 0)
        def _():
            dw_ref[...] = jnp.zeros_like(dw_ref)

        prev = jnp.where(i == 0, 0.0, xp_ref[...])
        ext = jnp.concatenate([prev, x_ref[...], xn_ref[...]], axis=0)
        dy_ext = jnp.concatenate([jnp.zeros((CONV_HALO, cb), F32), dy_ref[...],
                                  jnp.where(i == nt - 1, 0.0, dyn_ref[...])], axis=0)
        w = w_ref[...]
        shifted = [_shift_rows(ext, DN_CONV - 1 - k) for k in range(DN_CONV)]
        y = shifted[0] * w[0:1, :]
        for k in range(1, DN_CONV):
            y = y + shifted[k] * w[k:k + 1, :]
        s = _sigmoid(y)
        dpre = dy_ext * _silu_grad(y, s)
        dx = dpre * w[DN_CONV - 1:DN_CONV, :]
        for k in range(DN_CONV - 1):
            dx = dx + _shift_rows(dpre, -(DN_CONV - 1 - k)) * w[k:k + 1, :]
        dx_ref[...] = _bf(dx[CONV_HALO:CONV_HALO + tm, :])
        rows = [jnp.sum((dpre * shifted[k])[CONV_HALO:CONV_HALO + tm, :], axis=0, keepdims=True) for k in range(DN_CONV)]
        dw_ref[...] += jnp.concatenate(rows, axis=0)

    last_halo = T // CONV_HALO - 1
    return pl.pallas_call(
        body, name=name, grid=(C // cb, nt),
        in_specs=[pl.BlockSpec((tm, cb), lambda c, i: (i, c)),
                  pl.BlockSpec((CONV_HALO, cb), lambda c, i: (jnp.maximum(i * hb - 1, 0), c)),
                  pl.BlockSpec((CONV_HALO, cb), lambda c, i: (jnp.minimum((i + 1) * hb, last_halo), c)),
                  pl.BlockSpec((tm, cb), lambda c, i: (i, c)),
                  pl.BlockSpec((CONV_HALO, cb), lambda c, i: (jnp.minimum((i + 1) * hb, last_halo), c)),
                  pl.BlockSpec((DN_CONV, cb), lambda c, i: (0, c))],
        out_specs=[pl.BlockSpec((tm, cb), lambda c, i: (i, c)),
                   pl.BlockSpec((DN_CONV, cb), lambda c, i: (0, c))],
        out_shape=[jax.ShapeDtypeStruct((T, C), BF16), jax.ShapeDtypeStruct((DN_CONV, C), F32)],
        compiler_params=_params(("parallel", "arbitrary")),
    )(proj, proj, proj, dy, dy, conv_w)


def _unit_lower_inverses(lows, eye):
    def each(fn, *lists):
        return [fn(*args) for args in zip(*lists)]

    p1 = [-low for low in lows]
    p2 = each(_dot_mid, p1, p1)
    p4 = each(_dot_mid, p2, p2)
    a = each(lambda x, y: eye + x + y + _dot_mid(x, y), p1, p2)
    p8 = each(_dot_mid, p4, p4)
    p16 = each(_dot_mid, p8, p8)
    b = each(lambda x, y: eye + x + y + _dot_mid(x, y), p4, p8)
    p32 = each(_dot_mid, p16, p16)
    ab = each(_dot_mid, a, b)
    c = each(lambda x, y: eye + x + y + _dot_mid(x, y), p16, p32)
    return each(_dot_mid, ab, c)


def _interleave(chains):
    results = [None] * len(chains)
    live = list(range(len(chains)))
    while live:
        for i in list(live):
            try:
                next(chains[i])
            except StopIteration as stop:
                results[i] = stop.value
                live.remove(i)
        yield
    return results


def _run_interleaved(chains):
    rounds = _interleave(chains)
    while True:
        try:
            next(rounds)
        except StopIteration as stop:
            return stop.value


def _l2_unit(x):
    r = lax.rsqrt(jnp.sum(x * x, axis=-1, keepdims=True) + NORM_EPS)
    return x * r, r


class _BlockMasks:
    def __init__(self):
        n = DN_BLOCK
        row = lax.broadcasted_iota(jnp.int32, (n, n), 0)
        col = lax.broadcasted_iota(jnp.int32, (n, n), 1)
        same = (row // DN_CHUNK) == (col // DN_CHUNK)
        self.lower, self.strict_lower = same & (row >= col), same & (row > col)
        self.upper, self.strict_upper = same & (row <= col), same & (row < col)
        self.eye = (row == col).astype(F32)
        self.first = lax.broadcasted_iota(jnp.int32, (n, 1), 0) < DN_CHUNK


def _dn_gates(ba, hp):
    coef = -jnp.exp(hp[0:1, :])
    pre = ba + hp[1:2, :]
    return _sigmoid(ba), coef * _softplus(pre), coef, pre


def _dn_block_gates(mk, ba, hp):
    assert DN_BLOCK == 2 * DN_CHUNK
    beta_t, graw_t, coef, pre = _dn_gates(ba, hp)
    gcum_t = _dot_hi(mk.lower.astype(F32), graw_t)
    gl_t = jnp.where(mk.first, gcum_t[DN_CHUNK - 1:DN_CHUNK, :], gcum_t[DN_BLOCK - 1:DN_BLOCK, :])
    return beta_t, gcum_t, gl_t, graw_t, coef, pre


def _dn_local(mk, qraw, kraw, bc, gc, gl):
    f = {}
    f["qn"], f["rq"] = _l2_unit(qraw)
    qh = f["qn"] * (DN_DIM ** -0.5)
    kh, f["rk"] = _l2_unit(kraw)
    gr = jnp.broadcast_to(gc, (DN_BLOCK, DN_BLOCK)).T
    dec = jnp.where(mk.lower, jnp.exp(jnp.where(mk.lower, gc - gr, 0.0)), 0.0)
    kb = kh * bc
    mkk = _dot_nt(_bf(kb), _bf(kh))
    eg = jnp.exp(gc)
    mqk = _dot_nt(_bf(qh), _bf(kh))
    etl = jnp.exp(gl - gc)
    f.update(qh=qh, kh=kh, gr=gr, dec=dec, kb=kb, mkk=mkk, eg=eg, mqk=mqk, attn=mqk * dec, etl=etl, qd=qh * eg, kt=kh * etl)
    return f


def _dn_specs(rows, rev=None):
    at = (lambda n: n) if rev is None else rev
    hw = DN_HEADS * DN_DIM
    return dict(
        qkv=[pl.BlockSpec((rows, hw), lambda n, j=j: (at(n), j)) for j in range(3)],
        ba=pl.BlockSpec((rows, 128), lambda n: (at(n), BA_BLOCK)),
        hp=pl.BlockSpec((8, 128), lambda n: (0, 0)),
        tok=pl.BlockSpec((rows, hw), lambda n: (at(n), 0)),
        attn=pl.BlockSpec((rows, DN_HEADS * DN_CHUNK), lambda n: (at(n), 0)),
        gate=pl.BlockSpec((rows // DN_CHUNK, 8, 128), lambda n: (at(n), 0, 0)),
        state=pl.BlockSpec((rows // DN_CHUNK, DN_HEADS, DN_DIM, DN_DIM), lambda n: (at(n), 0, 0, 0)),
    )


def _call_with_exchange(body, exchange, *, name, steps, in_specs, out_specs, out_shape, args, vmem=None, scratch_shapes=()):
    if exchange is None:
        res = pl.pallas_call(body, name=name, grid=(steps,), in_specs=in_specs, out_specs=out_specs, out_shape=out_shape,
                             scratch_shapes=list(scratch_shapes), compiler_params=_params(("arbitrary",), vmem))(*args)
        return list(res), None
    n_in, n_out, m, n_scr = len(in_specs), len(out_specs), len(exchange.operands), len(scratch_shapes)

    def hosted(*refs):
        ins, ex_ins = refs[:n_in], refs[n_in:n_in + m]
        outs, ex_outs = refs[n_in + m:n_in + m + n_out], refs[n_in + m + n_out:n_in + 2 * m + n_out]
        scratch, sems = refs[n_in + 2 * m + n_out:n_in + 2 * m + n_out + n_scr], refs[n_in + 2 * m + n_out + n_scr:]

        @pl.when(pl.program_id(0) == 0)
        def _():
            exchange.start(ex_ins, ex_outs, sems)

        body(*ins, *outs, *scratch)

        @pl.when(pl.program_id(0) == steps - 1)
        def _():
            exchange.finish(ex_ins, ex_outs, sems)

    res = pl.pallas_call(
        hosted, name=name, grid=(steps,), in_specs=list(in_specs) + [ANY] * m, out_specs=list(out_specs) + [ANY] * m,
        out_shape=list(out_shape) + list(exchange.out_shape), scratch_shapes=list(scratch_shapes) + list(exchange.scratch),
        compiler_params=pltpu.CompilerParams(dimension_semantics=("arbitrary",), vmem_limit_bytes=vmem, has_side_effects=True),
    )(*args, *exchange.operands)
    return list(res[:n_out]), exchange.finalize(list(res[n_out:]))


def _dn_prep(qkv, proj, hp, name, exchange=None):
    T = qkv.shape[0]
    n_chunks = T // DN_CHUNK
    blocks = max(1, min(DN_PREP_CHUNKS, n_chunks) * DN_CHUNK // DN_BLOCK)
    group = blocks * DN_BLOCK // DN_CHUNK
    rows = blocks * DN_BLOCK
    hw = DN_HEADS * DN_DIM

    def body(q_ref, k_ref, v_ref, ba_ref, hp_ref, u_ref, w_ref, p_ref, qd_ref, kt_ref, gl_ref, inv_ref):
        mk = _BlockMasks()
        hp_v = hp_ref[...]
        chains = []
        for j in range(blocks):
            rs = slice(j * DN_BLOCK, (j + 1) * DN_BLOCK)
            beta_t, gcum_t, gl_t = _dn_block_gates(mk, ba_ref[rs, :], hp_v)[:3]
            for c in range(DN_BLOCK // DN_CHUNK):
                gl_ref[j * (DN_BLOCK // DN_CHUNK) + c] = jnp.broadcast_to(gl_t[c * DN_CHUNK:c * DN_CHUNK + 1, :], (8, 128))
            for h in range(DN_HEADS):
                sl = slice(h * DN_DIM, (h + 1) * DN_DIM)
                gate = slice(DN_HEADS + h, DN_HEADS + h + 1)
                bc = beta_t[:, h:h + 1]
                f = _dn_local(mk, q_ref[rs, sl], k_ref[rs, sl], bc, gcum_t[:, gate], gl_t[:, gate])
                for c in range(DN_BLOCK // DN_CHUNK):
                    cr = slice(c * DN_CHUNK, (c + 1) * DN_CHUNK)
                    p_ref[j * DN_BLOCK + c * DN_CHUNK:j * DN_BLOCK + (c + 1) * DN_CHUNK, h * DN_CHUNK:(h + 1) * DN_CHUNK] = _bf(f["attn"][cr, cr])
                qd_ref[rs, sl] = _bf(f["qd"])
                kt_ref[rs, sl] = _bf(f["kt"])
                chains.append((rs, sl, bc, f))
        invs = _unit_lower_inverses([jnp.where(mk.strict_lower, f["mkk"] * f["dec"], 0.0) for _, _, _, f in chains], mk.eye)
        for (rs, sl, bc, f), inv in zip(chains, invs):
            inv_ref[rs, sl] = inv
            sol = _dot_mid(inv, jnp.concatenate([v_ref[rs, sl] * bc, f["kb"] * f["eg"]], axis=1))
            u_ref[rs, sl] = sol[:, :DN_DIM]
            w_ref[rs, sl] = _bf(sol[:, DN_DIM:])

    sp = _dn_specs(rows)
    tok16 = jax.ShapeDtypeStruct((T, hw), BF16)
    return _call_with_exchange(
        body, exchange, name=name, steps=n_chunks // group,
        in_specs=sp["qkv"] + [sp["ba"], sp["hp"]],
        out_specs=[sp["tok"], sp["tok"], sp["attn"], sp["tok"], sp["tok"], sp["gate"], sp["tok"]],
        out_shape=[jax.ShapeDtypeStruct((T, hw), F32), tok16, jax.ShapeDtypeStruct((T, DN_HEADS * DN_CHUNK), BF16),
                   tok16, tok16, jax.ShapeDtypeStruct((n_chunks, 8, 128), F32), jax.ShapeDtypeStruct((T, hw), F32)],
        args=(qkv, qkv, qkv, proj, hp))


def _dn_scan(u, w, p, qd, kt, gl, name, exchange=None):
    T = u.shape[0]
    n_chunks = T // DN_CHUNK
    group = min(DN_SCAN_CHUNKS, n_chunks)
    rows = group * DN_CHUNK
    hw = DN_HEADS * DN_DIM

    def body(u_ref, w_ref, p_ref, qd_ref, kt_ref, gl_ref, o_ref, vn_ref, sall_ref, s_s):
        @pl.when(pl.program_id(0) == 0)
        def _():
            s_s[...] = jnp.zeros_like(s_s)

        state = [s_s[h] for h in range(DN_HEADS)]
        for j in range(group):
            rs = slice(j * DN_CHUNK, (j + 1) * DN_CHUNK)
            for h in range(DN_HEADS):
                sl = slice(h * DN_DIM, (h + 1) * DN_DIM)
                sall_ref[j, h] = state[h]
                sb = _bf(state[h])
                vnb = _bf(u_ref[rs, sl] - _dot(w_ref[rs, sl], sb))
                vn_ref[rs, sl] = vnb
                o_ref[rs, sl] = _dot(qd_ref[rs, sl], sb) + _dot(p_ref[rs, h * DN_CHUNK:(h + 1) * DN_CHUNK], vnb)
                egl = jnp.exp(gl_ref[j, 0:1, DN_HEADS + h:DN_HEADS + h + 1])
                state[h] = state[h] * egl + _dot_tn(kt_ref[rs, sl], vnb)
        for h in range(DN_HEADS):
            s_s[h] = state[h]

    sp = _dn_specs(rows)
    return _call_with_exchange(
        body, exchange, name=name, steps=n_chunks // group,
        in_specs=[sp["tok"], sp["tok"], sp["attn"], sp["tok"], sp["tok"], sp["gate"]],
        out_specs=[sp["tok"], sp["tok"], sp["state"]],
        out_shape=[jax.ShapeDtypeStruct((T, hw), F32), jax.ShapeDtypeStruct((T, hw), BF16),
                   jax.ShapeDtypeStruct((n_chunks, DN_HEADS, DN_DIM, DN_DIM), F32)],
        scratch_shapes=[pltpu.VMEM((DN_HEADS, DN_DIM, DN_DIM), F32)],
        args=(u, w, p, qd, kt, gl))


def _dn_scan_bwd(w, p, qd, kt, gl, vn, sall, do, name, exchange=None):
    T = w.shape[0]
    n_chunks = T // DN_CHUNK
    group = min(DN_SCAN_CHUNKS, n_chunks)
    rows = group * DN_CHUNK
    hw = DN_HEADS * DN_DIM
    last = n_chunks // group - 1

    def body(w_ref, p_ref, qd_ref, kt_ref, gl_ref, vn_ref, sall_ref, do_ref, dvn_ref, dkt_ref, dgl_ref, ds_s):
        @pl.when(pl.program_id(0) == 0)
        def _():
            ds_s[...] = jnp.zeros_like(ds_s)

        lane = lax.broadcasted_iota(jnp.int32, (8, 128), 1)
        d_state = [ds_s[h] for h in range(DN_HEADS)]
        for j in reversed(range(group)):
            rs = slice(j * DN_CHUNK, (j + 1) * DN_CHUNK)
            dgl_tile = jnp.zeros((8, 128), F32)
            for h in range(DN_HEADS):
                sl = slice(h * DN_DIM, (h + 1) * DN_DIM)
                d_out = _bf(do_ref[rs, sl])
                d_new = d_state[h]
                d_newb = _bf(d_new)
                d_vn = _dot_tn(p_ref[rs, h * DN_CHUNK:(h + 1) * DN_CHUNK], d_out) + _dot(kt_ref[rs, sl], d_newb)
                dvn_ref[rs, sl] = d_vn
                dkt_ref[rs, sl] = _dot_nt(vn_ref[rs, sl], d_newb)
                egl = jnp.exp(gl_ref[j, 0:1, DN_HEADS + h:DN_HEADS + h + 1])
                prod = jnp.sum(d_new * sall_ref[j, h], axis=1, keepdims=True)
                dgl_tile = jnp.where(lane == DN_HEADS + h, jnp.sum(prod, axis=0, keepdims=True) * egl, dgl_tile)
                d_state[h] = d_new * egl + _dot_tn(qd_ref[rs, sl], d_out) - _dot_tn(w_ref[rs, sl], _bf(d_vn))
            dgl_ref[j] = dgl_tile
        for h in range(DN_HEADS):
            ds_s[h] = d_state[h]

    sp = _dn_specs(rows, rev=lambda n: last - n)
    return _call_with_exchange(
        body, exchange, name=name, steps=n_chunks // group,
        in_specs=[sp["tok"], sp["attn"], sp["tok"], sp["tok"], sp["gate"], sp["tok"], sp["state"], sp["tok"]],
        out_specs=[sp["tok"], sp["tok"], sp["gate"]],
        out_shape=[jax.ShapeDtypeStruct((T, hw), F32), jax.ShapeDtypeStruct((T, hw), F32),
                   jax.ShapeDtypeStruct((n_chunks, 8, 128), F32)],
        scratch_shapes=[pltpu.VMEM((DN_HEADS, DN_DIM, DN_DIM), F32)],
        args=(w, p, qd, kt, gl, vn, sall, do))


def _dn_prep_bwd(qkv, proj, hp, sall, vn, do, dvn, dkt, dgl, inv, u, w, name, exchange=None):
    T = qkv.shape[0]
    n_chunks = T // DN_CHUNK
    blocks = max(1, min(DN_PREP_CHUNKS, n_chunks) * DN_CHUNK // DN_BLOCK)
    per_block = DN_BLOCK // DN_CHUNK
    group = blocks * per_block
    rows = blocks * DN_BLOCK
    hw = DN_HEADS * DN_DIM
    first_rows, second_rows = slice(0, DN_CHUNK), slice(DN_CHUNK, DN_BLOCK)

    def rowsum(x):
        return jnp.sum(x, axis=1, keepdims=True)

    def by_chunk(x, s0, s1, fn):
        return jnp.concatenate([fn(x[first_rows], s0), fn(x[second_rows], s1)], axis=0)

    def body(q_ref, k_ref, v_ref, ba_ref, hp_ref, sall_ref, vn_ref, do_ref, dvn_ref, dkt_ref, dgl_ref,
             inv_ref, u_ref, w_ref, dqkv_ref, dba_ref, dhp_ref):
        @pl.when(pl.program_id(0) == 0)
        def _():
            dhp_ref[...] = jnp.zeros_like(dhp_ref)

        mk = _BlockMasks()
        hp_v = hp_ref[...]
        chains = []
        for j in range(blocks):
            rs = slice(j * DN_BLOCK, (j + 1) * DN_BLOCK)
            chains.append(one_block(mk, hp_v, *(r.at[rs, :] for r in (q_ref, k_ref, v_ref, ba_ref)),
                                    sall_ref.at[pl.ds(j * per_block, per_block)],
                                    *(r.at[rs, :] for r in (vn_ref, do_ref, dvn_ref, dkt_ref)),
                                    dgl_ref.at[pl.ds(j * per_block, per_block)],
                                    *(r.at[rs, :] for r in (inv_ref, u_ref, w_ref)),
                                    *(dqkv_ref.at[rs, pl.ds(i * hw, hw)] for i in range(3)), dba_ref.at[rs, :]))
        total = jnp.zeros((8, 128), F32)
        for part in _run_interleaved(chains):
            total = total + part
        dhp_ref[...] += total

    def one_block(mk, hp_v, q_ref, k_ref, v_ref, ba_ref, state_ref, vn_ref, do_ref, dvn_ref, dkt_ref, dgl_ref,
                  inv_ref, u_ref, w_ref, dq_ref, dk_ref, dv_ref, dba_ref):
        ba = ba_ref[...]
        beta_t, gcum_t, gl_t, graw_t, coef, pre = _dn_block_gates(mk, ba, hp_v)
        lane = lax.broadcasted_iota(jnp.int32, (DN_BLOCK, 128), 1)
        rowi = lax.broadcasted_iota(jnp.int32, (DN_BLOCK, 1), 0)

        def head(h):
            sl = slice(h * DN_DIM, (h + 1) * DN_DIM)
            gate = slice(DN_HEADS + h, DN_HEADS + h + 1)
            gc = gcum_t[:, gate]
            bc = beta_t[:, h:h + 1]
            sb0, sb1 = _bf(state_ref[0, h]), _bf(state_ref[1, h])
            vh = v_ref[:, sl]
            f = _dn_local(mk, q_ref[:, sl], k_ref[:, sl], bc, gc, gl_t[:, gate])
            yield
            qh, kh, kb, dec, eg, etl = f["qh"], f["kh"], f["kb"], f["dec"], f["eg"], f["etl"]
            qd, kt = f["qd"], f["kt"]
            qb, kbf, kbb = _bf(qh), _bf(kh), _bf(kb)
            dec_t = jnp.where(mk.upper, jnp.exp(jnp.where(mk.upper, f["gr"] - gc, 0.0)), 0.0)
            mkk_t = f["mkk"].T
            inv_t = inv_ref[:, sl].T
            mqk_t = f["mqk"].T

            d_out = _bf(do_ref[:, sl])
            vnb = vn_ref[:, sl]
            d_qd = by_chunk(d_out, sb0, sb1, _dot_nt)
            d_attn = _dot_nt(d_out, vnb)
            d_attn_t = _dot_nt(vnb, d_out)
            d_vn = dvn_ref[:, sl]
            d_kt = dkt_ref[:, sl]
            d_w = -by_chunk(_bf(d_vn), sb0, sb1, _dot_nt)
            yield
            d_rhs = _dot_mid(inv_t, jnp.concatenate([d_vn, d_w], axis=1))
            yield
            d_bu, d_bw = d_rhs[:, :DN_DIM], d_rhs[:, DN_DIM:]
            ub, wb, d_bub, d_bwb = _bf(u_ref[:, sl]), w_ref[:, sl], _bf(d_bu), _bf(d_bw)
            d_low = -(_dot_nt(d_bub, ub) + _dot_nt(d_bwb, wb))
            d_low_t = -(_dot_nt(ub, d_bub) + _dot_nt(wb, d_bwb))
            yield
            d_mkk = jnp.where(mk.strict_lower, d_low * dec, 0.0)
            d_mkk_t = jnp.where(mk.strict_upper, d_low_t * dec_t, 0.0)
            d_mqk = jnp.where(mk.lower, d_attn * dec, 0.0)
            d_mqk_t = jnp.where(mk.upper, d_attn_t * dec_t, 0.0)
            bw = kb * eg
            d_kb = _dot(_bf(d_mkk), kbf) + d_bw * eg
            d_k = _dot(_bf(d_mkk_t), kbb) + _dot(_bf(d_mqk_t), qb) + d_kt * etl + d_kb * bc
            d_q = _dot(_bf(d_mqk), kbf) + d_qd * eg
            yield
            d_beta = rowsum(d_kb * kh) + rowsum(d_bu * vh)
            dv_ref[:, sl] = d_bu * bc
            e_mat = d_mkk * f["mkk"] + d_mqk * f["mqk"]
            e_mat_t = d_mkk_t * mkk_t + d_mqk_t * mqk_t
            kt_term = rowsum(d_kt * kt)
            d_g = rowsum(e_mat) - rowsum(e_mat_t) + rowsum(d_qd * qd) + rowsum(d_bw * bw) - kt_term
            for c, chunk_rows in enumerate((mk.first, ~mk.first)):
                d_glast = dgl_ref[c, 0:1, gate] + jnp.sum(jnp.where(chunk_rows, kt_term, 0.0), axis=0, keepdims=True)
                d_g = d_g + jnp.where(rowi == (c + 1) * DN_CHUNK - 1, d_glast, 0.0)
            qn = f["qn"]
            d_qs = d_q * (DN_DIM ** -0.5)
            dq_ref[:, sl] = f["rq"] * (d_qs - qn * rowsum(d_qs * qn))
            dk_ref[:, sl] = f["rk"] * (d_k - kh * rowsum(d_k * kh))
            return d_g, d_beta

        per_head = yield from _interleave([head(h) for h in range(DN_HEADS)])
        dgcum_t = jnp.zeros((DN_BLOCK, 128), F32)
        dbeta_t = jnp.zeros((DN_BLOCK, 128), F32)
        for h, (d_g, d_beta) in enumerate(per_head):
            dgcum_t = jnp.where(lane == DN_HEADS + h, d_g, dgcum_t)
            dbeta_t = jnp.where(lane == h, d_beta, dbeta_t)
        dgraw_t = _dot_hi(mk.upper.astype(F32), dgcum_t)
        sp = _sigmoid(pre)
        d_pre = dgraw_t * coef * sp
        dba_ref[...] = jnp.where(lane < DN_HEADS, dbeta_t * beta_t * (1.0 - beta_t),
                                 jnp.where(lane < 2 * DN_HEADS, d_pre, 0.0))
        in_g = (lane >= DN_HEADS) & (lane < 2 * DN_HEADS)
        d_alog = jnp.sum(jnp.where(in_g, dgraw_t * graw_t, 0.0), axis=0, keepdims=True)
        d_dtb = jnp.sum(jnp.where(in_g, d_pre, 0.0), axis=0, keepdims=True)
        return jnp.concatenate([d_alog, d_dtb, jnp.zeros((6, 128), F32)], axis=0)

    sp = _dn_specs(rows)
    return _call_with_exchange(
        body, exchange, name=name, steps=n_chunks // group,
        in_specs=sp["qkv"] + [sp["ba"], sp["hp"], sp["state"]] + [sp["tok"]] * 4 + [sp["gate"]] + [sp["tok"]] * 3,
        out_specs=[pl.BlockSpec((rows, 3 * hw), lambda n: (n, 0)), pl.BlockSpec((rows, 128), lambda n: (n, 0)), sp["hp"]],
        out_shape=[jax.ShapeDtypeStruct((T, 3 * hw), F32), jax.ShapeDtypeStruct((T, 128), F32),
                   jax.ShapeDtypeStruct((8, 128), F32)],
        args=(qkv, qkv, qkv, proj, hp, sall, vn, do, dvn, dkt, dgl, inv, u, w))


def _mix_fwd(o, proj, dn_norm, sg_norm, sg_w, sg_bt, name):
    T = o.shape[0]
    tm = _tile(T, 512)
    hw = DN_HEADS * DN_DIM
    nc = tm // SG_CHUNK

    def body(o_ref, z_ref, su_ref, sv_ref, dnn_ref, sgn_ref, sgw_ref, sgb_ref, mix_ref):
        dnn = dnn_ref[...]
        for h in range(DN_HEADS):
            sl = slice(h * DN_DIM, (h + 1) * DN_DIM)
            xhat, _ = _rms_stats(o_ref[:, sl])
            z = z_ref[:, sl]
            mix_ref[:, sl] = _bf(xhat * dnn * (z * _sigmoid(z)))
        tri = lax.broadcasted_iota(jnp.int32, (SG_CHUNK, SG_CHUNK), 0) >= lax.broadcasted_iota(jnp.int32, (SG_CHUNK, SG_CHUNK), 1)
        for g in range(SG_GROUPS):
            sl = slice(g * SG_DIM, (g + 1) * SG_DIM)
            xhat, _ = _rms_stats(_gelu(sv_ref[:, sl]))
            svn = _bf(xhat * sgn_ref[g:g + 1, :])
            sua = _gelu(su_ref[:, sl])
            wt = _bf(jnp.where(tri, sgw_ref[g], 0.0))
            bias = sgb_ref[:, g:g + 1]
            for c in range(nc):
                rows = slice(c * SG_CHUNK, (c + 1) * SG_CHUNK)
                mixed = _dot(wt, svn[rows, :]) + bias
                mix_ref[rows, hw + g * SG_DIM:hw + (g + 1) * SG_DIM] = _bf(sua[rows, :] * mixed)

    full = lambda shape: pl.BlockSpec(shape, lambda i: (0,) * len(shape))
    return pl.pallas_call(
        body, name=name, grid=(T // tm,),
        in_specs=[pl.BlockSpec((tm, hw), lambda i: (i, 0)),
                  pl.BlockSpec((tm, hw), lambda i: (i, 3)),
                  pl.BlockSpec((tm, hw), lambda i: (i, 4)),
                  pl.BlockSpec((tm, hw), lambda i: (i, 5)),
                  full((1, DN_DIM)), full((SG_GROUPS, SG_DIM)), full((SG_GROUPS, SG_CHUNK, SG_CHUNK)),
                  full((SG_CHUNK, 128))],
        out_specs=pl.BlockSpec((tm, 2 * hw), lambda i: (i, 0)),
        out_shape=jax.ShapeDtypeStruct((T, 2 * hw), BF16),
        compiler_params=_params(("parallel",)),
    )(o, proj, proj, proj, dn_norm, sg_norm, sg_w, sg_bt)


def _mix_bwd(dmix, o, proj, dn_norm, sg_norm, sg_w, sg_bt, name):
    T = o.shape[0]
    tm = _tile(T, 512)
    hw = DN_HEADS * DN_DIM
    nc = tm // SG_CHUNK

    def body(dm_ref, o_ref, z_ref, su_ref, sv_ref, dnn_ref, sgn_ref, sgw_ref, sgb_ref,
             do_ref, dz_ref, ddnn_ref, dsgn_ref, dsgw_ref, dsgb_ref):
        @pl.when(pl.program_id(0) == 0)
        def _():
            ddnn_ref[...] = jnp.zeros_like(ddnn_ref)
            dsgn_ref[...] = jnp.zeros_like(dsgn_ref)
            dsgw_ref[...] = jnp.zeros_like(dsgw_ref)
            dsgb_ref[...] = jnp.zeros_like(dsgb_ref)

        dnn = dnn_ref[...]
        ddnn = jnp.zeros((1, DN_DIM), F32)
        for h in range(DN_HEADS):
            sl = slice(h * DN_DIM, (h + 1) * DN_DIM)
            xhat, r = _rms_stats(o_ref[:, sl])
            z = z_ref[:, sl]
            sz = _sigmoid(z)
            doa = dm_ref[:, sl]
            dyn = doa * (z * sz)
            dz_ref[:, sl] = _bf(doa * xhat * dnn * _silu_grad(z, sz))
            do_ref[:, sl] = _rms_bwd(dyn, xhat, r, dnn)
            ddnn = ddnn + jnp.sum(dyn * xhat, axis=0, keepdims=True)
        ddnn_ref[...] += ddnn
        tri = lax.broadcasted_iota(jnp.int32, (SG_CHUNK, SG_CHUNK), 0) >= lax.broadcasted_iota(jnp.int32, (SG_CHUNK, SG_CHUNK), 1)
        lane = lax.broadcasted_iota(jnp.int32, (SG_CHUNK, 128), 1)
        dsgb = jnp.zeros((SG_CHUNK, 128), F32)
        dsgn_rows = []
        for g in range(SG_GROUPS):
            sl = slice(g * SG_DIM, (g + 1) * SG_DIM)
            sv = sv_ref[:, sl]
            su = su_ref[:, sl]
            xhat, r = _rms_stats(_gelu(sv))
            sgn = sgn_ref[g:g + 1, :]
            svn = _bf(xhat * sgn)
            sua = _gelu(su)
            wt = _bf(jnp.where(tri, sgw_ref[g], 0.0))
            bias = sgb_ref[:, g:g + 1]
            dw = jnp.zeros((SG_CHUNK, SG_CHUNK), F32)
            db = jnp.zeros((SG_CHUNK, 1), F32)
            dsua, dsvn = [], []
            for c in range(nc):
                rows = slice(c * SG_CHUNK, (c + 1) * SG_CHUNK)
                mixed = _dot(wt, svn[rows, :]) + bias
                dob = dm_ref[rows, hw + g * SG_DIM:hw + (g + 1) * SG_DIM]
                dsua.append(dob * mixed)
                dmixed = dob * sua[rows, :]
                dmb = _bf(dmixed)
                dsvn.append(_dot_tn(wt, dmb))
                dw = dw + _dot_nt(dmb, svn[rows, :])
                db = db + jnp.sum(dmixed, axis=1, keepdims=True)
            dsua = jnp.concatenate(dsua, axis=0) if nc > 1 else dsua[0]
            dsvn = jnp.concatenate(dsvn, axis=0) if nc > 1 else dsvn[0]
            dz_ref[:, hw + g * SG_DIM:hw + (g + 1) * SG_DIM] = _bf(dsua * _gelu_grad(su))
            dz_ref[:, 2 * hw + g * SG_DIM:2 * hw + (g + 1) * SG_DIM] = _bf(_rms_bwd(dsvn, xhat, r, sgn) * _gelu_grad(sv))
            dsgn_rows.append(jnp.sum(dsvn * xhat, axis=0, keepdims=True))
            dsgw_ref[g] += jnp.where(tri, dw, 0.0)
            dsgb = jnp.where(lane == g, db, dsgb)
        dsgn_ref[...] += jnp.concatenate(dsgn_rows, axis=0)
        dsgb_ref[...] += dsgb

    full = lambda shape: pl.BlockSpec(shape, lambda i: (0,) * len(shape))
    return pl.pallas_call(
        body, name=name, grid=(T // tm,),
        in_specs=[pl.BlockSpec((tm, 2 * hw), lambda i: (i, 0)),
                  pl.BlockSpec((tm, hw), lambda i: (i, 0)),
                  pl.BlockSpec((tm, hw), lambda i: (i, 3)),
                  pl.BlockSpec((tm, hw), lambda i: (i, 4)),
                  pl.BlockSpec((tm, hw), lambda i: (i, 5)),
                  full((1, DN_DIM)), full((SG_GROUPS, SG_DIM)), full((SG_GROUPS, SG_CHUNK, SG_CHUNK)),
                  full((SG_CHUNK, 128))],
        out_specs=[pl.BlockSpec((tm, hw), lambda i: (i, 0)),
                   pl.BlockSpec((tm, 3 * hw), lambda i: (i, 0)),
                   full((1, DN_DIM)), full((SG_GROUPS, SG_DIM)), full((SG_GROUPS, SG_CHUNK, SG_CHUNK)),
                   full((SG_CHUNK, 128))],
        out_shape=[jax.ShapeDtypeStruct((T, hw), F32), jax.ShapeDtypeStruct((T, 3 * hw), BF16),
                   jax.ShapeDtypeStruct((1, DN_DIM), F32), jax.ShapeDtypeStruct((SG_GROUPS, SG_DIM), F32),
                   jax.ShapeDtypeStruct((SG_GROUPS, SG_CHUNK, SG_CHUNK), F32),
                   jax.ShapeDtypeStruct((SG_CHUNK, 128), F32)],
        compiler_params=_params(("arbitrary",)),
    )(dmix, o, proj, proj, proj, dn_norm, sg_norm, sg_w, sg_bt)


def _window_sums(h, sign):
    sums, s, w = {}, h, 1
    while w < POOL_WINDOWS[-1]:
        s = s + _shift_rows(s, sign * w)
        w *= 2
        sums[w] = s
    return sums


def _pool_counts(t_global):
    return [jnp.minimum(t_global + 1, win).astype(F32) for win in POOL_WINDOWS]


def _pooled_groups(ext_h, row0, tm):
    sums = _window_sums(ext_h, 1)
    t_global = row0 + lax.broadcasted_iota(jnp.int32, (tm, 1), 0)
    counts = _pool_counts(t_global)
    out = []
    for gi, win in enumerate(POOL_WINDOWS):
        cols = slice(gi * POOL_DIM, (gi + 1) * POOL_DIM)
        out.append(sums[win][POOL_HALO:, cols] / counts[gi] - ext_h[POOL_HALO:, cols])
    return out


def _pool_fwd(x, nw, pool_w, pool_scale, layer, name):
    T, D = x.shape
    tm = _tile(T, 256)
    hb = tm // POOL_HALO

    def body(x_ref, xp_ref, n_ref, w_ref, s_ref, xo_ref):
        i = pl.program_id(0)
        prev = jnp.where(i == 0, 0.0, xp_ref[...])
        ext = jnp.concatenate([prev, x_ref[...]], axis=0)
        xhat, _ = _rms_stats(ext)
        pooled = _pooled_groups(xhat * n_ref[...], i * tm, tm)
        for gi in range(len(POOL_WINDOWS)):
            cols = slice(gi * POOL_DIM, (gi + 1) * POOL_DIM)
            xo_ref[:, cols] = x_ref[:, cols] + _dot(_bf(pooled[gi]), w_ref[gi]) * s_ref[:, cols]

    return pl.pallas_call(
        body, name=name, grid=(T // tm,),
        in_specs=[pl.BlockSpec((tm, D), lambda i: (i, 0)),
                  pl.BlockSpec((POOL_HALO, D), lambda i: (jnp.maximum(i * hb - 1, 0), 0)),
                  pl.BlockSpec((None, 1, D), lambda i: (layer, 0, 0)),
                  pl.BlockSpec(pool_w.shape, lambda i: (0, 0, 0)),
                  pl.BlockSpec((1, D), lambda i: (0, 0))],
        out_specs=pl.BlockSpec((tm, D), lambda i: (i, 0)),
        out_shape=jax.ShapeDtypeStruct((T, D), F32),
        compiler_params=_params(("parallel",)),
    )(x, x, nw, pool_w, pool_scale)


def _pool_bwd(dxo, x, nw, pool_w, pool_scale, layer, name):
    T, D = x.shape
    tm = _tile(T, 256)
    hb = tm // POOL_HALO
    nt = T // tm
    ng = len(POOL_WINDOWS)

    def body(dxo_ref, dxn_ref, x_ref, xp_ref, n_ref, w_ref, s_ref, dx_ref, dw_ref, ds_ref, dn_ref):
        i = pl.program_id(0)

        @pl.when(i == 0)
        def _():
            dw_ref[...] = jnp.zeros_like(dw_ref)
            ds_ref[...] = jnp.zeros_like(ds_ref)
            dn_ref[...] = jnp.zeros_like(dn_ref)

        prev = jnp.where(i == 0, 0.0, xp_ref[...])
        ext = jnp.concatenate([prev, x_ref[...]], axis=0)
        xhat_ext, r_ext = _rms_stats(ext)
        nv = n_ref[...]
        pooled = _pooled_groups(xhat_ext * nv, i * tm, tm)
        dxo = dxo_ref[...]
        scale = s_ref[...]
        dout_ext = jnp.concatenate([dxo, jnp.where(i == nt - 1, 0.0, dxn_ref[...])], axis=0) * scale
        t_ext = i * tm + lax.broadcasted_iota(jnp.int32, (tm + POOL_HALO, 1), 0)
        counts = _pool_counts(t_ext)
        dh_cols, ds_cols = [], []
        for gi, win in enumerate(POOL_WINDOWS):
            cols = slice(gi * POOL_DIM, (gi + 1) * POOL_DIM)
            wg = w_ref[gi]
            pb = _bf(pooled[gi])
            doutb = _bf(dout_ext[:, cols])
            dpooled = _dot_nt(doutb, wg)
            ahead = _window_sums(dpooled / counts[gi], -1)[win]
            dh_cols.append(ahead[:tm, :] - dpooled[:tm, :])
            dw_ref[gi] += _dot_tn(pb, doutb[:tm, :])
            ds_cols.append(jnp.sum(dxo[:, cols] * _dot(pb, wg), axis=0, keepdims=True))
        dh = jnp.concatenate(dh_cols, axis=1)
        xhat, r = xhat_ext[POOL_HALO:, :], r_ext[POOL_HALO:, :]
        dx_ref[...] = dxo + _rms_bwd(dh, xhat, r, nv)
        dn_ref[...] += jnp.sum(dh * xhat, axis=0, keepdims=True)
        ds_ref[...] += jnp.concatenate(ds_cols, axis=1)

    last_halo = T // POOL_HALO - 1
    return pl.pallas_call(
        body, name=name, grid=(nt,),
        in_specs=[pl.BlockSpec((tm, D), lambda i: (i, 0)),
                  pl.BlockSpec((POOL_HALO, D), lambda i: (jnp.minimum((i + 1) * hb, last_halo), 0)),
                  pl.BlockSpec((tm, D), lambda i: (i, 0)),
                  pl.BlockSpec((POOL_HALO, D), lambda i: (jnp.maximum(i * hb - 1, 0), 0)),
                  pl.BlockSpec((None, 1, D), lambda i: (layer, 0, 0)),
                  pl.BlockSpec(pool_w.shape, lambda i: (0, 0, 0)),
                  pl.BlockSpec((1, D), lambda i: (0, 0))],
        out_specs=[pl.BlockSpec((tm, D), lambda i: (i, 0)),
                   pl.BlockSpec((ng, POOL_DIM, POOL_DIM), lambda i: (0, 0, 0)),
                   pl.BlockSpec((1, D), lambda i: (0, 0)),
                   pl.BlockSpec((1, D), lambda i: (0, 0))],
        out_shape=[jax.ShapeDtypeStruct((T, D), F32), jax.ShapeDtypeStruct((ng, POOL_DIM, POOL_DIM), F32),
                   jax.ShapeDtypeStruct((1, D), F32), jax.ShapeDtypeStruct((1, D), F32)],
        compiler_params=_params(("arbitrary",)),
    )(dxo, dxo, x, x, nw, pool_w, pool_scale)


def _loss_head(x, target, fn, name):
    T, D = x.shape
    tm = _tile(T, 512)

    def body(x_ref, t_ref, n_ref, loss_ref, dx_ref, dn_ref):
        @pl.when(pl.program_id(0) == 0)
        def _():
            loss_ref[...] = jnp.zeros_like(loss_ref)
            dn_ref[...] = jnp.zeros_like(dn_ref)

        xhat, r = _rms_stats(x_ref[...])
        nv = n_ref[...]
        err = xhat * nv - t_ref[...]
        part = jnp.sum(jnp.sum(err * err, axis=1, keepdims=True), axis=0, keepdims=True)
        loss_ref[...] += 0.5 * part / D
        dy = err / D
        dx_ref[...] = _rms_bwd(dy, xhat, r, nv)
        dn_ref[...] += jnp.sum(dy * xhat, axis=0, keepdims=True)

    row = pl.BlockSpec((tm, D), lambda i: (i, 0))
    return pl.pallas_call(
        body, name=name, grid=(T // tm,),
        in_specs=[row, row, pl.BlockSpec((1, D), lambda i: (0, 0))],
        out_specs=[pl.BlockSpec((1, 1), lambda i: (0, 0)), row, pl.BlockSpec((1, D), lambda i: (0, 0))],
        out_shape=[jax.ShapeDtypeStruct((1, 1), F32), jax.ShapeDtypeStruct((T, D), F32),
                   jax.ShapeDtypeStruct((1, D), F32)],
        compiler_params=_params(("arbitrary",)),
    )(x, target, fn)


def _adamw(w, g, m, v, name):
    R, C = w.shape
    br = R
    for cand in (512, 256, 128, 64, 32, 16, 8):
        if R % cand == 0 and cand * C * 4 <= 2 * 1024 * 1024:
            br = cand
            break

    def body(w_ref, g_ref, m_ref, v_ref, d_ref, mo_ref, vo_ref):
        gv = g_ref[...]
        m_new = ADAM_B1 * m_ref[...] + (1.0 - ADAM_B1) * gv
        v_new = ADAM_B2 * v_ref[...] + (1.0 - ADAM_B2) * (gv * gv)
        m_hat = m_new / (1.0 - ADAM_B1 ** ADAM_STEP)
        v_hat = v_new / (1.0 - ADAM_B2 ** ADAM_STEP)
        d_ref[...] = -ADAM_LR * (m_hat / (jnp.sqrt(v_hat) + ADAM_EPS) + ADAM_WD * w_ref[...])
        mo_ref[...] = m_new
        vo_ref[...] = v_new

    blk = pl.BlockSpec((br, C), lambda i: (i, 0))
    return pl.pallas_call(
        body, name=name, grid=(R // br,), in_specs=[blk] * 4, out_specs=[blk] * 3,
        out_shape=[jax.ShapeDtypeStruct((R, C), F32)] * 3,
        compiler_params=_params(("parallel",)),
    )(w, g, m, v)


def _mesh_pos():
    return lax.axis_index("x"), lax.axis_index("y"), lax.axis_index("c")


def _other_chips(x, y):
    return [(1 - x, y), (x, 1 - y), (1 - x, 1 - y)]


def _half_of(ref, shape, h):
    size = shape[0] // 2
    return ref.at[pl.ds(h * size, size)]


class _ChipGather:
    def __init__(self, shards, split):
        self.shards, self.split = list(shards), list(split)
        self.operands = self.shards
        n = len(self.shards)
        self.out_shape = [jax.ShapeDtypeStruct((N_CHIPS,) + s.shape, s.dtype) for s in self.shards]
        self.scratch = [pltpu.SemaphoreType.DMA((n, 3))] * 4 + [pltpu.SemaphoreType.DMA((n,))] * 2

    def _piece(self, a, ref, h):
        return _half_of(ref, self.shards[a].shape, h) if self.split[a] else ref

    def _own(self, a, ins, outs, sems):
        x, y, c = _mesh_pos()
        return pltpu.make_async_remote_copy(ins[a], outs[a].at[2 * x + y], sems[4].at[a], sems[5].at[a],
                                            device_id=(x, y, 1 - c), device_id_type=MESH)

    def start(self, ins, outs, sems):
        send_sems, recv_sems = sems[0], sems[1]
        x, y, c = _mesh_pos()
        me = 2 * x + y
        for a in range(len(ins)):
            for k, (px, py) in enumerate(_other_chips(x, y)):
                pltpu.make_async_remote_copy(self._piece(a, ins[a], c), self._piece(a, outs[a].at[me], c),
                                             send_sems.at[a, k], recv_sems.at[a, k],
                                             device_id=(px, py, c), device_id_type=MESH).start()
            self._own(a, ins, outs, sems).start()

    def finish(self, ins, outs, sems):
        send_sems, recv_sems, fwd_send_sems, fwd_recv_sems = sems[:4]
        x, y, c = _mesh_pos()
        sibling = (x, y, 1 - c)
        chips = _other_chips(x, y)
        n = len(ins)
        forwards = []
        for a in range(n):
            self._own(a, ins, outs, sems).wait()
        for a in range(n):
            for k, (px, py) in enumerate(chips):
                landed = self._piece(a, outs[a].at[2 * px + py], c)
                pltpu.make_async_remote_copy(landed, landed, send_sems.at[a, k], recv_sems.at[a, k],
                                             device_id=(px, py, c), device_id_type=MESH).wait_recv()
                if self.split[a]:
                    fwd = pltpu.make_async_remote_copy(landed, landed, fwd_send_sems.at[a, k], fwd_recv_sems.at[a, k],
                                                       device_id=sibling, device_id_type=MESH)
                    fwd.start()
                    forwards.append(fwd)
        for a in range(n):
            if self.split[a]:
                for k, (px, py) in enumerate(chips):
                    other = self._piece(a, outs[a].at[2 * px + py], 1 - c)
                    pltpu.make_async_remote_copy(other, other, fwd_send_sems.at[a, k], fwd_recv_sems.at[a, k],
                                                 device_id=sibling, device_id_type=MESH).wait_recv()
        for a in range(n):
            for k, (px, py) in enumerate(chips):
                sent = self._piece(a, ins[a], c)
                pltpu.make_async_remote_copy(sent, sent, send_sems.at[a, k], recv_sems.at[a, k],
                                             device_id=(px, py, c), device_id_type=MESH).wait_send()
        for fwd in forwards:
            fwd.wait_send()

    def finalize(self, gathered):
        return gathered

    def run(self, name):
        n = len(self.shards)

        def body(*refs):
            ins, outs, sems = refs[:n], refs[n:2 * n], refs[2 * n:]
            self.start(ins, outs, sems)
            self.finish(ins, outs, sems)

        gathered = pl.pallas_call(
            body, name=name, in_specs=[ANY] * n, out_specs=[ANY] * n, out_shape=self.out_shape,
            scratch_shapes=self.scratch, compiler_params=pltpu.CompilerParams(has_side_effects=True),
        )(*self.shards)
        return self.finalize(gathered)


def _ffn_weight_grads(hb, dgu, a, dyb, tag):
    dwin = _matmul_tn(hb, dgu, D_MODEL, FF_CHUNK, f"{tag}_dw_in", stack_n=True)
    dwo = _matmul_tn(a, dyb, FF_CHUNK, D_MODEL, f"{tag}_dw_out")
    return dwin, dwo.reshape(N_CHIPS, D_FF // N_CHIPS, D_MODEL)


def _local_step(x, target, w, late=None, reduce=False):
    g = {}
    acts = []
    w = dict(w)

    def ffn_weights(which, layer):
        return w[f"n{which}"], w[f"win{which}_l{layer}"], w[f"wout{which}_l{layer}"]

    def hosting(name):
        exchange, layouts = late.get(name, (None, None)) if late else (None, None)
        return exchange, (lambda arrived: w.update(layouts(arrived)) if exchange is not None else None)

    def ffn(xin, which, layer):
        name = f"ffn{which}_l{layer}_fwd"
        exchange, keep = hosting(name)
        (xo, gv, uv, hb), arrived = _ffn_fwd(xin, *ffn_weights(which, layer), layer, name, exchange)
        keep(arrived)
        acts.append((xin, gv, uv, hb))
        return xo

    x1 = ffn(x, 1, 0)
    hb_mix, proj = _norm_matmul(x1, w["nmix"], 0, w["wp"], "ab_in_proj")
    qkv = _conv_fwd(proj, w["conv_w"], "dn_conv_fwd")
    exchange, keep = hosting("dn_prep")
    (dn_u, dn_w, dn_p, dn_qd, dn_kt, dn_gl, dn_inv), arrived = _dn_prep(qkv, proj, w["hp"], "dn_prep", exchange)
    keep(arrived)
    exchange, keep = hosting("dn_scan")
    (o, dn_vn, sall), arrived = _dn_scan(dn_u, dn_w, dn_p, dn_qd, dn_kt, dn_gl, "dn_scan", exchange)
    keep(arrived)
    mix = _mix_fwd(o, proj, w["dn_norm"], w["sg_norm"], w["sg_w"], w["sg_bt"], "ab_gate_fwd")
    x2 = _matmul(mix, w["wo"], "ab_out_proj", res=x1)
    x3 = ffn(x2, 2, 0)
    x4 = ffn(x3, 1, 1)
    x5 = _pool_fwd(x4, w["nmix"], w["pool_w"], w["pool_scale"], 1, "pool_fwd")
    x6 = ffn(x5, 2, 1)
    loss, dx, g["fn"] = _loss_head(x6, target, w["fn"], "loss_head")

    dn = {1: [None, None], 2: [None, None]}
    dwin = {1: [None, None], 2: [None, None]}
    dwout = {1: [None, None], 2: [None, None]}

    def ffn_back(dxo, which, layer, saved, exchange=None):
        nw, win, wout = ffn_weights(which, layer)
        xin, gv, uv, hb = saved
        tag = f"ffn{which}_l{layer}"
        (dxi, dgu, a, dyb, dnw), arrived = _ffn_bwd(dxo, xin, nw, gv, uv, win, wout, layer, f"{tag}_bwd", exchange)
        dn[which][layer] = dnw
        dwin[which][layer], dwout[which][layer] = _ffn_weight_grads(hb, dgu, a, dyb, tag)
        return dxi, arrived

    for which in (1, 2):
        g[f"win{which}"] = dwin[which]
        g[f"wout{which}"] = dwout[which]
    reduced = {}

    def open_round(tag, keys):
        have = _sharded_grads(g)
        return _GradRound(tag, {k: have[k] for k in keys})

    dx, _ = ffn_back(dx, 2, 1, acts[3])
    dx, g["pool_w"], g["pool_scale"], dnmix1 = _pool_bwd(dx, x4, w["nmix"], w["pool_w"], w["pool_scale"], 1, "pool_bwd")
    dx, _ = ffn_back(dx, 1, 1, acts[2])
    round_a = open_round("a", REDUCE_ROUNDS[0]) if reduce else None
    dx2, arrived = ffn_back(dx, 2, 0, acts[1], round_a.swap if reduce else None)
    if reduce:
        round_a.pair_sum(arrived)
    round_b = open_round("b", REDUCE_ROUNDS[1]) if reduce else None
    dmix = _matmul(dx2, w["wo"], "ab_out_proj_bwd", trans_b=True)
    g["wo"] = _matmul_tn(mix, dx2, D_MODEL, D_MODEL, "ab_out_proj_dw")
    do, dzuv, g["dn_norm"], g["sg_norm"], g["sg_w"], g["sg_bt"] = _mix_bwd(
        dmix, o, proj, w["dn_norm"], w["sg_norm"], w["sg_w"], w["sg_bt"], "ab_gate_bwd")
    (dvn, dkt, dgl), arrived = _dn_scan_bwd(dn_w, dn_p, dn_qd, dn_kt, dn_gl, dn_vn, sall, do, "dn_scan_bwd",
                                            round_b.swap if reduce else None)
    if reduce:
        round_b.pair_sum(arrived)
    (dqkv_act, dba, g["hp"]), arrived = _dn_prep_bwd(qkv, proj, w["hp"], sall, dn_vn, do, dvn, dkt, dgl, dn_inv, dn_u, dn_w,
                                                     "dn_prep_bwd", round_b.scatter if reduce else None)
    if reduce:
        reduced.update(round_b.finish(arrived))
    dqkv, g["conv_w"] = _conv_bwd(dqkv_act, proj, w["conv_w"], "dn_conv_bwd")
    dproj = [dqkv, dzuv, dba]
    dx1, dnmix0 = _norm_matmul_bwd(dproj, w["wp"], x1, w["nmix"], dx2, 0, "ab_in_proj_bwd")
    g["wp"] = [_matmul_tn(hb_mix, piece, D_MODEL, 768, f"ab_in_proj_dw_{k}") for k, piece in enumerate(dproj)]
    dx0, arrived = ffn_back(dx1, 1, 0, acts[0], round_a.scatter if reduce else None)
    if reduce:
        reduced.update(round_a.finish(arrived))
        round_c = open_round("c", REDUCE_ROUNDS[2])
        round_c.pair_sum(round_c.swap.run("grad_c_pair_swap"))
        reduced.update(round_c.finish(round_c.scatter.run("grad_c_chip_scatter")))

    g["n1"] = jnp.concatenate(dn[1], axis=0)
    g["n2"] = jnp.concatenate(dn[2], axis=0)
    g["nmix"] = jnp.concatenate([dnmix0, dnmix1], axis=0)
    return loss, dx0, g, reduced


SHARDED = ("ffn1_w_in", "ffn1_w_out", "ffn2_w_in", "ffn2_w_out", "ab_w_in", "ab_w_out", "pool_w", "dn_conv_w", "pool_scale")
REPLICATED = ("ffn_norm1", "mix_norm", "ffn_norm2", "dn_a_log", "dn_dt_bias", "dn_out_norm", "sg_norm", "sg_w", "sg_b", "final_norm")
QKVZ = 4 * DN_HEADS * DN_DIM
N_GATES = 2 * DN_HEADS
IN_PROJ = QKVZ + N_GATES + 2 * SG_GROUPS * SG_DIM


def _shard_pieces(wts, keys):
    out = []
    for n, layer in keys:
        a = wts[n][0 if layer is None else layer]
        a = a[None] if a.ndim == 1 else a
        out.append(a.astype(BF16) if n in MATRICES else a)
    return out


def _replicated_layouts(rep):
    per_layer = lambda a: a.reshape(a.shape[0], 1, D_MODEL)
    w = {"n1": per_layer(rep["ffn_norm1"]), "nmix": per_layer(rep["mix_norm"]), "n2": per_layer(rep["ffn_norm2"])}
    hp = jnp.zeros((8, 128), F32)
    w["hp"] = hp.at[0, DN_HEADS:N_GATES].set(rep["dn_a_log"][0]).at[1, DN_HEADS:N_GATES].set(rep["dn_dt_bias"][0])
    w["dn_norm"] = rep["dn_out_norm"]
    w["sg_norm"] = rep["sg_norm"][0]
    w["sg_w"] = rep["sg_w"][0]
    w["sg_bt"] = jnp.zeros((SG_CHUNK, 128), F32).at[:, :SG_GROUPS].set(rep["sg_b"][0].T)
    w["fn"] = rep["final_norm"].reshape(1, D_MODEL)
    return w


def _layouts_from(gathered):
    w = {}
    for (n, layer), a in gathered.items():
        if n in ("ffn1_w_in", "ffn2_w_in"):
            w[f"win{n[3]}_l{layer}"] = a
        elif n in ("ffn1_w_out", "ffn2_w_out"):
            w[f"wout{n[3]}_l{layer}"] = a
        elif n == "ab_w_in":
            ab_in = jnp.transpose(a, (1, 0, 2)).reshape(D_MODEL, IN_PROJ)
            w["wp"] = jnp.concatenate([ab_in[:, :QKVZ], ab_in[:, QKVZ + N_GATES:], ab_in[:, QKVZ:QKVZ + N_GATES],
                                       jnp.zeros((D_MODEL, PROJ_W - IN_PROJ), ab_in.dtype)], axis=1)
        elif n == "dn_conv_w":
            w["conv_w"] = jnp.transpose(a, (1, 0, 2)).reshape(DN_CONV, 3 * DN_HEADS * DN_DIM)
        elif n == "ab_w_out":
            w["wo"] = a.reshape(D_MODEL, D_MODEL)
        elif n == "pool_w":
            w["pool_w"] = jnp.transpose(a, (1, 0, 2, 3)).reshape(len(POOL_WINDOWS), POOL_DIM, POOL_DIM)
        elif n == "pool_scale":
            w["pool_scale"] = a.reshape(1, D_MODEL)
    return w


def _sharded_grads(g):
    nw = len(POOL_WINDOWS)
    sharded = {}
    for n, key in (("ffn1_w_in", "win1"), ("ffn1_w_out", "wout1"), ("ffn2_w_in", "win2"), ("ffn2_w_out", "wout2")):
        for layer, a in enumerate(g.get(key, ())):
            if a is not None:
                sharded[(n, layer)] = a
    if "wp" in g:
        qkv, zuv, gates = g["wp"]
        z_width = DN_HEADS * DN_DIM
        ab_in = jnp.concatenate([qkv, zuv[:, :z_width], gates[:, :N_GATES], zuv[:, z_width:]], axis=1)
        sharded[("ab_w_in", None)] = jnp.transpose(ab_in.reshape(D_MODEL, N_CHIPS, IN_PROJ // N_CHIPS), (1, 0, 2))
    if "wo" in g:
        sharded[("ab_w_out", None)] = g["wo"].reshape(N_CHIPS, D_MODEL // N_CHIPS, D_MODEL)
    if "pool_w" in g:
        sharded[("pool_w", None)] = jnp.transpose(g["pool_w"].reshape(nw, N_CHIPS, POOL_DIM // N_CHIPS, POOL_DIM), (1, 0, 2, 3))
    if "conv_w" in g:
        sharded[("dn_conv_w", None)] = jnp.transpose(g["conv_w"].reshape(DN_CONV, N_CHIPS, -1), (1, 0, 2))
    if "pool_scale" in g:
        sharded[("pool_scale", None)] = g["pool_scale"].reshape(N_CHIPS, 1, D_MODEL // N_CHIPS)
    return sharded


def _replicated_grads(g):
    rep = {
        "ffn_norm1": g["n1"], "mix_norm": g["nmix"], "ffn_norm2": g["n2"],
        "dn_a_log": g["hp"][0:1, DN_HEADS:N_GATES], "dn_dt_bias": g["hp"][1:2, DN_HEADS:N_GATES],
        "dn_out_norm": g["dn_norm"], "sg_norm": g["sg_norm"][None], "sg_w": g["sg_w"][None],
        "sg_b": g["sg_bt"][:, :SG_GROUPS].T[None], "final_norm": g["fn"].reshape(D_MODEL),
    }
    return rep


def _as_halves(a):
    shape = a.shape[1:]
    if len(shape) >= 2 and shape[0] % 2 == 0:
        return a.reshape(N_CHIPS, 2, -1, shape[-1])
    return a.reshape(N_CHIPS, 2, 1, -1)


def _row_block(rows):
    for cand in (256, 176, 128, 64, 32, 16):
        if rows % cand == 0:
            return cand
    return rows


def _from_halves(pairs, core, shape):
    mine_first = jnp.stack([t for mine, other in pairs for t in (mine, other)])
    other_first = jnp.stack([t for mine, other in pairs for t in (other, mine)])
    return jnp.where(core == 0, mine_first, other_first).reshape(shape)


class _PairSwap:
    def __init__(self, packs):
        self.operands = list(packs)
        n = len(self.operands)
        self.out_shape = [jax.ShapeDtypeStruct((p.shape[0],) + p.shape[2:], p.dtype) for p in self.operands]
        self.scratch = [pltpu.SemaphoreType.DMA((n,))] * 2

    def _copies(self, ins, outs, sems):
        x, y, c = _mesh_pos()
        return [pltpu.make_async_remote_copy(ins[k].at[:, 1 - c], outs[k], sems[0].at[k], sems[1].at[k],
                                             device_id=(x, y, 1 - c), device_id_type=MESH) for k in range(len(ins))]

    def start(self, ins, outs, sems):
        for cp in self._copies(ins, outs, sems):
            cp.start()

    def finish(self, ins, outs, sems):
        for cp in self._copies(ins, outs, sems):
            cp.wait()

    def finalize(self, results):
        return results

    def run(self, name):
        n = len(self.operands)

        def body(*refs):
            ins, outs, sems = refs[:n], refs[n:2 * n], refs[2 * n:]
            self.start(ins, outs, sems)
            self.finish(ins, outs, sems)

        return pl.pallas_call(
            body, name=name, in_specs=[ANY] * n, out_specs=[ANY] * n, out_shape=self.out_shape, scratch_shapes=self.scratch,
            compiler_params=pltpu.CompilerParams(has_side_effects=True),
        )(*self.operands)


def _add_pair(pack, recv, core, name):
    nchip, _, rows, lanes = pack.shape
    rb = _row_block(rows)

    def body(c_ref, a_ref, b_ref, o32_ref, o16_ref):
        s = a_ref[...] + b_ref[...]
        o32_ref[...] = s
        o16_ref[...] = _bf(s)

    blk = pl.BlockSpec((None, rb, lanes), lambda p, i, c: (p, i, 0))
    return pl.pallas_call(
        body, name=name,
        grid_spec=pltpu.PrefetchScalarGridSpec(
            num_scalar_prefetch=1, grid=(nchip, rows // rb),
            in_specs=[pl.BlockSpec((None, None, rb, lanes), lambda p, i, c: (p, c[0], i, 0)), blk],
            out_specs=[blk, blk]),
        out_shape=[jax.ShapeDtypeStruct((nchip, rows, lanes), F32), jax.ShapeDtypeStruct((nchip, rows, lanes), BF16)],
        compiler_params=_params(("parallel", "parallel")),
    )(core, pack, recv)


class _ChipScatter:
    def __init__(self, parts16):
        self.operands = list(parts16)
        n = len(self.operands)
        self.out_shape = [jax.ShapeDtypeStruct((N_CHIPS - 1,) + p.shape[1:], p.dtype) for p in self.operands]
        self.scratch = [pltpu.SemaphoreType.DMA((n, N_CHIPS - 1))] * 2

    def _copies(self, ins, outs, sems):
        x, y, c = _mesh_pos()
        return [pltpu.make_async_remote_copy(ins[a].at[2 * px + py], outs[a].at[k], sems[0].at[a, k], sems[1].at[a, k],
                                             device_id=(px, py, c), device_id_type=MESH)
                for a in range(len(ins)) for k, (px, py) in enumerate(_other_chips(x, y))]

    def start(self, ins, outs, sems):
        for cp in self._copies(ins, outs, sems):
            cp.start()

    def finish(self, ins, outs, sems):
        for cp in self._copies(ins, outs, sems):
            cp.wait()

    def finalize(self, results):
        return results

    def run(self, name):
        n = len(self.operands)

        def body(*refs):
            ins, outs, sems = refs[:n], refs[n:2 * n], refs[2 * n:]
            self.start(ins, outs, sems)
            self.finish(ins, outs, sems)

        return pl.pallas_call(
            body, name=name, in_specs=[ANY] * n, out_specs=[ANY] * n, out_shape=self.out_shape, scratch_shapes=self.scratch,
            compiler_params=pltpu.CompilerParams(has_side_effects=True),
        )(*self.operands)


def _sum_chips(part32, recv16, chip, name):
    nchip, rows, lanes = part32.shape
    rb = _row_block(rows)

    def body(p_ref, own_ref, r_ref, o_ref):
        s = own_ref[...]
        for k in range(nchip - 1):
            s = s + r_ref[k].astype(F32)
        o_ref[...] = s

    return pl.pallas_call(
        body, name=name,
        grid_spec=pltpu.PrefetchScalarGridSpec(
            num_scalar_prefetch=1, grid=(rows // rb,),
            in_specs=[pl.BlockSpec((None, rb, lanes), lambda i, p: (p[0], i, 0)),
                      pl.BlockSpec((nchip - 1, rb, lanes), lambda i, p: (0, i, 0))],
            out_specs=pl.BlockSpec((rb, lanes), lambda i, p: (i, 0))),
        out_shape=jax.ShapeDtypeStruct((rows, lanes), F32),
        compiler_params=_params(("parallel",)),
    )(chip, part32, recv16)


def _share_with_sibling(halves, name):
    n = len(halves)

    def body(*refs):
        ins, outs, send_sems, recv_sems = refs[:n], refs[n:2 * n], refs[2 * n], refs[2 * n + 1]
        x, y, c = _mesh_pos()
        copies = [pltpu.make_async_remote_copy(ins[k], outs[k], send_sems.at[k], recv_sems.at[k],
                                               device_id=(x, y, 1 - c), device_id_type=MESH) for k in range(n)]
        for cp in copies:
            cp.start()
        for cp in copies:
            cp.wait()

    return pl.pallas_call(
        body, name=name, in_specs=[ANY] * n, out_specs=[ANY] * n,
        out_shape=[jax.ShapeDtypeStruct(h.shape, h.dtype) for h in halves],
        scratch_shapes=[pltpu.SemaphoreType.DMA((n,)), pltpu.SemaphoreType.DMA((n,))],
        compiler_params=pltpu.CompilerParams(has_side_effects=True),
    )(*halves)


class _GradRound:
    def __init__(self, tag, pieces):
        self.tag, self.keys = tag, list(pieces)
        self.shapes = [pieces[k].shape[1:] for k in self.keys]
        self.packs = [_as_halves(pieces[k]) for k in self.keys]
        self.swap = _PairSwap(self.packs)

    def pair_sum(self, recvs):
        _, _, c = _mesh_pos()
        core = jnp.reshape(c, (1,)).astype(jnp.int32)
        sums = [_add_pair(p, r, core, f"grad_{self.tag}_pair_add_{i}") for i, (p, r) in enumerate(zip(self.packs, recvs))]
        self.parts32 = [s[0] for s in sums]
        self.scatter = _ChipScatter([s[1] for s in sums])

    def finish(self, recvs16):
        x, y, _ = _mesh_pos()
        chip = jnp.reshape(2 * x + y, (1,)).astype(jnp.int32)
        halves = [_sum_chips(p, r, chip, f"grad_{self.tag}_chip_sum_{i}") for i, (p, r) in enumerate(zip(self.parts32, recvs16))]
        others = _share_with_sibling(halves, f"grad_{self.tag}_pair_share")
        return dict(zip(self.keys, zip(halves, others)))


def _pack_small(vals):
    parts = []
    for n in REPLICATED:
        flat = vals[n].reshape(-1)
        rows = -(-flat.shape[0] // 128)
        rows = -(-rows // 8) * 8
        parts.append(jnp.pad(flat, (0, rows * 128 - flat.shape[0])).reshape(rows, 128))
    return jnp.concatenate(parts, axis=0)


def _unpack_small(pack, like):
    out, off = {}, 0
    for n in REPLICATED:
        size = like[n].size
        rows = -(-size // 128)
        rows = -(-rows // 8) * 8
        out[n] = pack[off:off + rows].reshape(-1)[:size].reshape(like[n].shape)
        off += rows
    return out


def _all_to_all_small(pack, name):
    rows, lanes = pack.shape
    flips = [(dx, dy, dc) for dx in (0, 1) for dy in (0, 1) for dc in (0, 1)][1:]

    def body(src_ref, out_ref, send_sems, recv_sems, local_sem):
        x, y, c = _mesh_pos()
        me = 4 * x + 2 * y + c
        loc = pltpu.make_async_copy(src_ref, out_ref.at[me], local_sem)
        loc.start()
        copies = []
        for k, (dx, dy, dc) in enumerate(flips):
            peer = (x ^ dx, y ^ dy, c ^ dc)
            cp = pltpu.make_async_remote_copy(src_ref, out_ref.at[me], send_sems.at[k], recv_sems.at[k],
                                              device_id=peer, device_id_type=MESH)
            cp.start()
            copies.append(cp)
        for k, (dx, dy, dc) in enumerate(flips):
            peer = (x ^ dx, y ^ dy, c ^ dc)
            pltpu.make_async_remote_copy(src_ref, out_ref.at[4 * peer[0] + 2 * peer[1] + peer[2]], send_sems.at[k],
                                         recv_sems.at[k], device_id=peer, device_id_type=MESH).wait_recv()
        for cp in copies:
            cp.wait_send()
        loc.wait()

    return pl.pallas_call(
        body, name=name, in_specs=[ANY], out_specs=ANY,
        out_shape=jax.ShapeDtypeStruct((8, rows, lanes), pack.dtype),
        scratch_shapes=[pltpu.SemaphoreType.DMA((7,)), pltpu.SemaphoreType.DMA((7,)), pltpu.SemaphoreType.DMA],
        compiler_params=pltpu.CompilerParams(has_side_effects=True),
    )(pack)


def _sum_devices(stack, name):
    ndev, rows, lanes = stack.shape

    def body(s_ref, o_ref):
        s = s_ref[0]
        for d in range(1, ndev):
            s = s + s_ref[d]
        o_ref[...] = s

    return pl.pallas_call(
        body, name=name, grid=(1,),
        in_specs=[pl.BlockSpec((ndev, rows, lanes), lambda i: (0, 0, 0))],
        out_specs=pl.BlockSpec((rows, lanes), lambda i: (0, 0)),
        out_shape=jax.ShapeDtypeStruct((rows, lanes), F32),
    )(stack)


WEIGHT_ORDER = ("ffn_norm1", "ffn1_w_in", "ffn1_w_out", "mix_norm", "ffn_norm2", "ffn2_w_in", "ffn2_w_out", "ab_w_in",
                "dn_conv_w", "dn_a_log", "dn_dt_bias", "dn_out_norm", "sg_norm", "sg_w", "sg_b", "ab_w_out", "pool_w",
                "pool_scale", "final_norm")
MATRICES = ("ffn1_w_in", "ffn1_w_out", "ffn2_w_in", "ffn2_w_out", "ab_w_in", "ab_w_out", "pool_w")
GATHER_FIRST = (("ffn1_w_in", 0), ("ffn1_w_out", 0))
GATHER_LATER = {"ffn1_l0_fwd": (("ab_w_in", None), ("dn_conv_w", None), ("ab_w_out", None)),
                "dn_prep": (("ffn2_w_in", 0), ("ffn2_w_out", 0)),
                "dn_scan": (("ffn1_w_in", 1), ("ffn1_w_out", 1)),
                "ffn2_l0_fwd": (("pool_w", None), ("pool_scale", None), ("ffn2_w_in", 1), ("ffn2_w_out", 1))}
REDUCE_ROUNDS = ((("ffn2_w_in", 1), ("ffn2_w_out", 1), ("ffn1_w_in", 1), ("ffn1_w_out", 1), ("pool_w", None), ("pool_scale", None)),
                 (("ffn2_w_in", 0), ("ffn2_w_out", 0)),
                 (("ffn1_w_in", 0), ("ffn1_w_out", 0), ("ab_w_out", None), ("ab_w_in", None), ("dn_conv_w", None)))


def _as_2d(a):
    return a.reshape(-1, a.shape[-1])


def kernel(x, ffn_norm1, ffn1_w_in, ffn1_w_out, mix_norm, ffn_norm2, ffn2_w_in, ffn2_w_out, ab_w_in, dn_conv_w, dn_a_log, dn_dt_bias, dn_out_norm, sg_norm, sg_w, sg_b, ab_w_out, pool_w, pool_scale, final_norm, loss_target, m_ffn_norm1, m_ffn1_w_in, m_ffn1_w_out, m_mix_norm, m_ffn_norm2, m_ffn2_w_in, m_ffn2_w_out, m_ab_w_in, m_dn_conv_w, m_dn_a_log, m_dn_dt_bias, m_dn_out_norm, m_sg_norm, m_sg_w, m_sg_b, m_ab_w_out, m_pool_w, m_pool_scale, m_final_norm, v_ffn_norm1, v_ffn1_w_in, v_ffn1_w_out, v_mix_norm, v_ffn_norm2, v_ffn2_w_in, v_ffn2_w_out, v_ab_w_in, v_dn_conv_w, v_dn_a_log, v_dn_dt_bias, v_dn_out_norm, v_sg_norm, v_sg_w, v_sg_b, v_ab_w_out, v_pool_w, v_pool_scale, v_final_norm):
    given = dict(locals())
    wts = {n: given[n] for n in WEIGHT_ORDER}
    mom_m = {n: given["m_" + n] for n in WEIGHT_ORDER}
    mom_v = {n: given["v_" + n] for n in WEIGHT_ORDER}

    rep = {n: wts[n] for n in REPLICATED}
    first = _ChipGather(_shard_pieces(wts, GATHER_FIRST), [n in MATRICES for n, _ in GATHER_FIRST])
    w = {**_replicated_layouts(rep), **_layouts_from(dict(zip(GATHER_FIRST, first.run("weight_gather_first"))))}
    late = {host: (_ChipGather(_shard_pieces(wts, keys), [n in MATRICES for n, _ in keys]),
                   functools.partial(lambda keys, arrived: _layouts_from(dict(zip(keys, arrived))), keys))
            for host, keys in GATHER_LATER.items()}

    loss, dx, g, reduced = _local_step(x[0], loss_target[0], w, reduce=True, late=late)
    g_rep = _replicated_grads(g)
    grads = {}
    core = lax.axis_index("c")
    for n in SHARDED:
        layers = [reduced[(n, layer)] for layer in range(wts[n].shape[0])] if (n, 0) in reduced else [reduced[(n, None)]]
        grads[n] = _from_halves(layers, core, wts[n].shape)
    small = _sum_devices(_all_to_all_small(_pack_small(g_rep), "grad_small_exchange"), "grad_small_sum")
    grads.update(_unpack_small(small, rep))

    delta, new_m, new_v = {}, {}, {}
    for n in SHARDED:
        d, m1, v1 = _adamw(_as_2d(wts[n]), _as_2d(grads[n]), _as_2d(mom_m[n]), _as_2d(mom_v[n]), f"adamw_{n}")
        delta[n], new_m[n], new_v[n] = (t.reshape(wts[n].shape) for t in (d, m1, v1))
    d, m1, v1 = _adamw(_pack_small(rep), small, _pack_small({n: mom_m[n] for n in REPLICATED}),
                       _pack_small({n: mom_v[n] for n in REPLICATED}), "adamw_replicated")
    for tgt, packed in ((delta, d), (new_m, m1), (new_v, v1)):
        tgt.update(_unpack_small(packed, rep))

    total = lax.psum(loss[0, 0], ("x", "y", "c"))
    outs = [total, dx[None]]
    for group in (grads, delta, new_m, new_v):
        outs.extend(group[n] for n in WEIGHT_ORDER)
    return tuple(outs)
```
